```python
import jax, jax.numpy as jnp
from jax import lax
import numpy as np

D_MODEL = 2048
BATCH = 8
SEQ = 8192
DEPTH = 1

N_HEADS_MLA = 8
Q_LORA = 512
KV_LORA = 256
MLA_NOPE = 128
MLA_ROPE = 64
MLA_V = 128
MLA_QK = MLA_NOPE + MLA_ROPE
DIL_GROUPS = ((128, 1), (512, 4), (2048, 16))
N_DIL_GROUPS = len(DIL_GROUPS)
N_HEADS_DIL = 8
DIL_HEAD = 128
DIL_ROT = DIL_HEAD // 4
ROPE_THETA = 500000.0
D_FF = 5632
PLE_DIM = 256
EPS = 1e-6
Q_BLOCK = 128
NEG = -1e30

OFF_CQ = 0
OFF_CKV = OFF_CQ + Q_LORA
OFF_KR = OFF_CKV + KV_LORA
OFF_DIL = OFF_KR + MLA_ROPE
DIL_QKV = N_DIL_GROUPS * 3 * N_HEADS_DIL * DIL_HEAD
OFF_GATE = OFF_DIL + DIL_QKV
N_BRANCH = 2
D_IN = OFF_GATE + N_BRANCH * D_MODEL

kernel_name = "hybrid_mla_dilated_gated_macaron"


def rmsnorm(x, g):
    xf = x.astype(jnp.float32)
    y = xf * lax.rsqrt(jnp.mean(xf * xf, axis=-1, keepdims=True) + EPS)
    return (y * g.astype(jnp.float32)).astype(x.dtype)


def rope(x, pos):
    rd = x.shape[-1]
    half = rd // 2
    inv = ROPE_THETA ** (-jnp.arange(half, dtype=jnp.float32) * 2.0 / rd)
    ang = pos.astype(jnp.float32)[..., None] * inv
    cos = jnp.cos(ang)[:, :, None, :]
    sin = jnp.sin(ang)[:, :, None, :]
    xf = x.astype(jnp.float32)
    x1, x2 = xf[..., :half], xf[..., half:]
    return jnp.concatenate([x1 * cos - x2 * sin, x2 * cos + x1 * sin], axis=-1).astype(x.dtype)


def swiglu(h, wg, wu, wd):
    return (jax.nn.silu(h @ wg) * (h @ wu)) @ wd


def causal_block_attention(q, k, v):
    B, S, H, Dq = q.shape
    nq = S // Q_BLOCK
    qb = q.reshape(B, nq, Q_BLOCK, H, Dq).transpose(1, 0, 2, 3, 4)
    starts = jnp.arange(nq, dtype=jnp.int32) * Q_BLOCK
    kpos = jnp.arange(S, dtype=jnp.int32)

    def one_block(args):
        qi, s0 = args
        sc = jnp.einsum('bqhd,bkhd->bhqk', qi, k).astype(jnp.float32)
        qpos = s0 + jnp.arange(Q_BLOCK, dtype=jnp.int32)
        mask = kpos[None, :] <= qpos[:, None]
        pr = jax.nn.softmax(jnp.where(mask[None, None], sc, NEG), axis=-1)
        return jnp.einsum('bhqk,bkhd->bqhd', pr.astype(v.dtype), v)

    out = lax.map(one_block, (qb, starts))
    return out.transpose(1, 0, 2, 3, 4).reshape(B, S, H, v.shape[-1])


def dilated_window_attention(q, k, v, dil, n):
    B, S, H, Dh = q.shape
    L = S // dil
    Lp = -(-L // n) * n
    nb = Lp // n

    def phase_major(t):
        t = t.reshape(B, L, dil, H, Dh).transpose(0, 2, 1, 3, 4)
        return jnp.pad(t, ((0, 0), (0, 0), (0, Lp - L), (0, 0), (0, 0)))

    def windows(t):
        t = jnp.pad(phase_major(t), ((0, 0), (0, 0), (n, 0), (0, 0), (0, 0)))
        t = t.reshape(B, dil, nb + 1, n, H, Dh)
        return jnp.concatenate([t[:, :, :-1], t[:, :, 1:]], axis=3)

    qb = phase_major(q).reshape(B, dil, nb, n, H, Dh)
    kw = windows(k)
    vw = windows(v)
    sc = jnp.einsum('brnqhc,brnkhc->brnhqk', qb, kw).astype(jnp.float32)
    i = jnp.arange(n)[:, None]
    j = jnp.arange(2 * n)[None, :]
    dist = i + n - j
    band = (dist >= 0) & (dist <= n)
    keyok = ((jnp.arange(nb)[:, None] - 1) * n + jnp.arange(2 * n)[None, :]) >= 0
    valid = (band[None] & keyok[:, None, :])[None, None, :, None]
    m = jnp.max(jnp.where(valid, sc, NEG), axis=-1)
    e = jnp.where(valid, jnp.exp(sc - m[..., None]), 0.0)
    l = jnp.sum(e, axis=-1)
    o = jnp.einsum('brnhqk,brnkhc->brnqhc', (e / l[..., None]).astype(v.dtype), vw)
    o = o.reshape(B, dil, Lp, H, Dh)[:, :, :L].transpose(0, 2, 1, 3, 4).reshape(B, S, H, Dh)

    def back(s):
        s = s.transpose(0, 1, 2, 4, 3).reshape(B, dil, Lp, H)[:, :, :L]
        return s.transpose(0, 2, 1, 3).reshape(B, S, H)

    return o, back(m), back(l)


def _fwd_setup_inputs(seed: int = 0) -> dict:
    key = jax.random.key(seed)
    ks = jax.random.split(key, 32)
    f32 = jnp.float32

    def nrm(k, shape, fan):
        return jax.random.normal(k, shape, f32) * fan ** -0.5

    def gain(k, shape):
        return 1.0 + 0.1 * jax.random.normal(k, shape, f32)

    Dp = DEPTH
    return {
        "x": jax.random.normal(ks[0], (BATCH, SEQ, D_MODEL), f32),
        "p": jax.random.normal(ks[1], (DEPTH, BATCH, SEQ, PLE_DIM), f32),
        "positions": jnp.arange(SEQ, dtype=jnp.int32)[None, :]
        + jax.random.randint(ks[2], (BATCH, 1), 0, 4096, dtype=jnp.int32),
        "g_ffn1": gain(ks[3], (Dp, D_MODEL)),
        "w1_gate": nrm(ks[4], (Dp, D_MODEL, D_FF), D_MODEL),
        "w1_up": nrm(ks[5], (Dp, D_MODEL, D_FF), D_MODEL),
        "w1_down": nrm(ks[6], (Dp, D_FF, D_MODEL), D_FF),
        "g_mix": gain(ks[7], (Dp, D_MODEL)),
        "w_in": nrm(ks[8], (Dp, D_MODEL, D_IN), D_MODEL),
        "g_cq": gain(ks[9], (Dp, Q_LORA)),
        "w_uq": nrm(ks[10], (Dp, Q_LORA, N_HEADS_MLA * MLA_QK), Q_LORA),
        "g_ckv": gain(ks[11], (Dp, KV_LORA)),
        "w_ukv": nrm(ks[12], (Dp, KV_LORA, N_HEADS_MLA * (MLA_NOPE + MLA_V)), KV_LORA),
        "g_q_mla": gain(ks[13], (Dp, MLA_QK)),
        "g_k_mla": gain(ks[14], (Dp, MLA_QK)),
        "g_q_dil": gain(ks[15], (Dp, N_DIL_GROUPS, DIL_HEAD)),
        "g_k_dil": gain(ks[16], (Dp, N_DIL_GROUPS, DIL_HEAD)),
        "w_br_mla": nrm(ks[17], (Dp, N_HEADS_MLA * MLA_V, D_MODEL), N_HEADS_MLA * MLA_V),
        "w_br_dil": nrm(ks[18], (Dp, N_HEADS_DIL * DIL_HEAD, D_MODEL), N_HEADS_DIL * DIL_HEAD),
        "w_o": nrm(ks[19], (Dp, D_MODEL, D_MODEL), D_MODEL),
        "g_ffn2": gain(ks[20], (Dp, D_MODEL)),
        "w2_gate": nrm(ks[21], (Dp, D_MODEL, D_FF), D_MODEL),
        "w2_up": nrm(ks[22], (Dp, D_MODEL, D_FF), D_MODEL),
        "w2_down": nrm(ks[23], (Dp, D_FF, D_MODEL), D_FF),
        "g_ple": gain(ks[24], (Dp, D_MODEL)),
        "w_ple_gate": nrm(ks[25], (Dp, D_MODEL, D_MODEL), D_MODEL),
        "w_ple_proj": nrm(ks[26], (Dp, PLE_DIM, D_MODEL), PLE_DIM),
    }


def _fwd_reference(x, p, positions, g_ffn1, w1_gate, w1_up, w1_down, g_mix, w_in, g_cq, w_uq,
              g_ckv, w_ukv, g_q_mla, g_k_mla, g_q_dil, g_k_dil, w_br_mla, w_br_dil, w_o,
              g_ffn2, w2_gate, w2_up, w2_down, g_ple, w_ple_gate, w_ple_proj):
    B, S, _ = x.shape
    scale_mla = MLA_QK ** -0.5
    scale_dil = DIL_HEAD ** -0.5
    for i in range(DEPTH):
        x = x + 0.5 * swiglu(rmsnorm(x, g_ffn1[i]), w1_gate[i], w1_up[i], w1_down[i])

        h = rmsnorm(x, g_mix[i])
        proj = h @ w_in[i]

        cq = rmsnorm(proj[..., OFF_CQ:OFF_CKV], g_cq[i])
        q = (cq @ w_uq[i]).reshape(B, S, N_HEADS_MLA, MLA_QK)
        ckv = rmsnorm(proj[..., OFF_CKV:OFF_KR], g_ckv[i])
        kv = (ckv @ w_ukv[i]).reshape(B, S, N_HEADS_MLA, MLA_NOPE + MLA_V)
        k_nope, v = kv[..., :MLA_NOPE], kv[..., MLA_NOPE:]
        k_rope = jnp.broadcast_to(proj[:, :, None, OFF_KR:OFF_DIL], (B, S, N_HEADS_MLA, MLA_ROPE))
        k = jnp.concatenate([k_nope, k_rope], axis=-1)
        q = rmsnorm(q, g_q_mla[i])
        k = rmsnorm(k, g_k_mla[i])
        q = jnp.concatenate([q[..., :MLA_NOPE], rope(q[..., MLA_NOPE:], positions)], axis=-1) * scale_mla
        k = jnp.concatenate([k[..., :MLA_NOPE], rope(k[..., MLA_NOPE:], positions)], axis=-1)
        o_mla = causal_block_attention(q, k, v).reshape(B, S, N_HEADS_MLA * MLA_V)

        dqkv = proj[..., OFF_DIL:OFF_GATE].reshape(B, S, N_DIL_GROUPS, 3, N_HEADS_DIL, DIL_HEAD)
        outs, ms, ls = [], [], []
        for g, (win, dil) in enumerate(DIL_GROUPS):
            qg = rmsnorm(dqkv[:, :, g, 0], g_q_dil[i, g])
            kg = rmsnorm(dqkv[:, :, g, 1], g_k_dil[i, g])
            vg = dqkv[:, :, g, 2]
            qg = jnp.concatenate([rope(qg[..., :DIL_ROT], positions), qg[..., DIL_ROT:]], axis=-1) * scale_dil
            kg = jnp.concatenate([rope(kg[..., :DIL_ROT], positions), kg[..., DIL_ROT:]], axis=-1)
            o, m, l = dilated_window_attention(qg, kg, vg, dil, win // dil)
            outs.append(o)
            ms.append(m)
            ls.append(l)
        m_all = jnp.stack(ms, axis=0)
        l_all = jnp.stack(ls, axis=0)
        wgt = l_all * jnp.exp(m_all - jnp.max(m_all, axis=0, keepdims=True))
        wgt = wgt / jnp.sum(wgt, axis=0, keepdims=True)
        o_dil = jnp.einsum('gbsh,gbshc->bshc', wgt.astype(x.dtype), jnp.stack(outs, axis=0))
        o_dil = o_dil.reshape(B, S, N_HEADS_DIL * DIL_HEAD)

        gates = jax.nn.sigmoid(proj[..., OFF_GATE:].reshape(B, S, N_BRANCH, D_MODEL))
        merged = gates[:, :, 0] * (o_mla @ w_br_mla[i]) + gates[:, :, 1] * (o_dil @ w_br_dil[i])
        x = x + merged @ w_o[i]

        x = x + 0.5 * swiglu(rmsnorm(x, g_ffn2[i]), w2_gate[i], w2_up[i], w2_down[i])

        ple_gate = jax.nn.sigmoid(rmsnorm(x, g_ple[i]) @ w_ple_gate[i])
        x = x + ple_gate * (p[i] @ w_ple_proj[i])
    return x


import jax as _jax
import jax.numpy as _jnp

TWIN_FORMAT = 'train_step'
FWD_PARAMS = ['x', 'p', 'positions', 'g_ffn1', 'w1_gate', 'w1_up', 'w1_down', 'g_mix', 'w_in', 'g_cq', 'w_uq', 'g_ckv', 'w_ukv', 'g_q_mla', 'g_k_mla', 'g_q_dil', 'g_k_dil', 'w_br_mla', 'w_br_dil', 'w_o', 'g_ffn2', 'w2_gate', 'w2_up', 'w2_down', 'g_ple', 'w_ple_gate', 'w_ple_proj']
TWIN_WEIGHTS = ['g_ffn1', 'w1_gate', 'w1_up', 'w1_down', 'g_mix', 'w_in', 'g_cq', 'w_uq', 'g_ckv', 'w_ukv', 'g_q_mla', 'g_k_mla', 'g_q_dil', 'g_k_dil', 'w_br_mla', 'w_br_dil', 'w_o', 'g_ffn2', 'w2_gate', 'w2_up', 'w2_down', 'g_ple', 'w_ple_gate', 'w_ple_proj']
TWIN_DIFF_INPUT = 'x'
TWIN_INPUTS = ['x', 'p', 'positions', 'g_ffn1', 'w1_gate', 'w1_up', 'w1_down', 'g_mix', 'w_in', 'g_cq', 'w_uq', 'g_ckv', 'w_ukv', 'g_q_mla', 'g_k_mla', 'g_q_dil', 'g_k_dil', 'w_br_mla', 'w_br_dil', 'w_o', 'g_ffn2', 'w2_gate', 'w2_up', 'w2_down', 'g_ple', 'w_ple_gate', 'w_ple_proj', 'loss_target', 'm_g_ffn1', 'm_w1_gate', 'm_w1_up', 'm_w1_down', 'm_g_mix', 'm_w_in', 'm_g_cq', 'm_w_uq', 'm_g_ckv', 'm_w_ukv', 'm_g_q_mla', 'm_g_k_mla', 'm_g_q_dil', 'm_g_k_dil', 'm_w_br_mla', 'm_w_br_dil', 'm_w_o', 'm_g_ffn2', 'm_w2_gate', 'm_w2_up', 'm_w2_down', 'm_g_ple', 'm_w_ple_gate', 'm_w_ple_proj', 'v_g_ffn1', 'v_w1_gate', 'v_w1_up', 'v_w1_down', 'v_g_mix', 'v_w_in', 'v_g_cq', 'v_w_uq', 'v_g_ckv', 'v_w_ukv', 'v_g_q_mla', 'v_g_k_mla', 'v_g_q_dil', 'v_g_k_dil', 'v_w_br_mla', 'v_w_br_dil', 'v_w_o', 'v_g_ffn2', 'v_w2_gate', 'v_w2_up', 'v_w2_down', 'v_g_ple', 'v_w_ple_gate', 'v_w_ple_proj']
TWIN_OUTPUTS = ['loss', 'grad_x', 'grad_g_ffn1', 'grad_w1_gate', 'grad_w1_up', 'grad_w1_down', 'grad_g_mix', 'grad_w_in', 'grad_g_cq', 'grad_w_uq', 'grad_g_ckv', 'grad_w_ukv', 'grad_g_q_mla', 'grad_g_k_mla', 'grad_g_q_dil', 'grad_g_k_dil', 'grad_w_br_mla', 'grad_w_br_dil', 'grad_w_o', 'grad_g_ffn2', 'grad_w2_gate', 'grad_w2_up', 'grad_w2_down', 'grad_g_ple', 'grad_w_ple_gate', 'grad_w_ple_proj', 'delta_g_ffn1', 'delta_w1_gate', 'delta_w1_up', 'delta_w1_down', 'delta_g_mix', 'delta_w_in', 'delta_g_cq', 'delta_w_uq', 'delta_g_ckv', 'delta_w_ukv', 'delta_g_q_mla', 'delta_g_k_mla', 'delta_g_q_dil', 'delta_g_k_dil', 'delta_w_br_mla', 'delta_w_br_dil', 'delta_w_o', 'delta_g_ffn2', 'delta_w2_gate', 'delta_w2_up', 'delta_w2_down', 'delta_g_ple', 'delta_w_ple_gate', 'delta_w_ple_proj', 'new_m_g_ffn1', 'new_m_w1_gate', 'new_m_w1_up', 'new_m_w1_down', 'new_m_g_mix', 'new_m_w_in', 'new_m_g_cq', 'new_m_w_uq', 'new_m_g_ckv', 'new_m_w_ukv', 'new_m_g_q_mla', 'new_m_g_k_mla', 'new_m_g_q_dil', 'new_m_g_k_dil', 'new_m_w_br_mla', 'new_m_w_br_dil', 'new_m_w_o', 'new_m_g_ffn2', 'new_m_w2_gate', 'new_m_w2_up', 'new_m_w2_down', 'new_m_g_ple', 'new_m_w_ple_gate', 'new_m_w_ple_proj', 'new_v_g_ffn1', 'new_v_w1_gate', 'new_v_w1_up', 'new_v_w1_down', 'new_v_g_mix', 'new_v_w_in', 'new_v_g_cq', 'new_v_w_uq', 'new_v_g_ckv', 'new_v_w_ukv', 'new_v_g_q_mla', 'new_v_g_k_mla', 'new_v_g_q_dil', 'new_v_g_k_dil', 'new_v_w_br_mla', 'new_v_w_br_dil', 'new_v_w_o', 'new_v_g_ffn2', 'new_v_w2_gate', 'new_v_w2_up', 'new_v_w2_down', 'new_v_g_ple', 'new_v_w_ple_gate', 'new_v_w_ple_proj']
TWIN_LEAF_KINDS = {'loss': 'loss', 'grad_x': 'grad_x', 'grad_g_ffn1': 'grad_w', 'grad_w1_gate': 'grad_w', 'grad_w1_up': 'grad_w', 'grad_w1_down': 'grad_w', 'grad_g_mix': 'grad_w', 'grad_w_in': 'grad_w', 'grad_g_cq': 'grad_w', 'grad_w_uq': 'grad_w', 'grad_g_ckv': 'grad_w', 'grad_w_ukv': 'grad_w', 'grad_g_q_mla': 'grad_w', 'grad_g_k_mla': 'grad_w', 'grad_g_q_dil': 'grad_w', 'grad_g_k_dil': 'grad_w', 'grad_w_br_mla': 'grad_w', 'grad_w_br_dil': 'grad_w', 'grad_w_o': 'grad_w', 'grad_g_ffn2': 'grad_w', 'grad_w2_gate': 'grad_w', 'grad_w2_up': 'grad_w', 'grad_w2_down': 'grad_w', 'grad_g_ple': 'grad_w', 'grad_w_ple_gate': 'grad_w', 'grad_w_ple_proj': 'grad_w', 'delta_g_ffn1': 'delta_w', 'delta_w1_gate': 'delta_w', 'delta_w1_up': 'delta_w', 'delta_w1_down': 'delta_w', 'delta_g_mix': 'delta_w', 'delta_w_in': 'delta_w', 'delta_g_cq': 'delta_w', 'delta_w_uq': 'delta_w', 'delta_g_ckv': 'delta_w', 'delta_w_ukv': 'delta_w', 'delta_g_q_mla': 'delta_w', 'delta_g_k_mla': 'delta_w', 'delta_g_q_dil': 'delta_w', 'delta_g_k_dil': 'delta_w', 'delta_w_br_mla': 'delta_w', 'delta_w_br_dil': 'delta_w', 'delta_w_o': 'delta_w', 'delta_g_ffn2': 'delta_w', 'delta_w2_gate': 'delta_w', 'delta_w2_up': 'delta_w', 'delta_w2_down': 'delta_w', 'delta_g_ple': 'delta_w', 'delta_w_ple_gate': 'delta_w', 'delta_w_ple_proj': 'delta_w', 'new_m_g_ffn1': 'new_m', 'new_m_w1_gate': 'new_m', 'new_m_w1_up': 'new_m', 'new_m_w1_down': 'new_m', 'new_m_g_mix': 'new_m', 'new_m_w_in': 'new_m', 'new_m_g_cq': 'new_m', 'new_m_w_uq': 'new_m', 'new_m_g_ckv': 'new_m', 'new_m_w_ukv': 'new_m', 'new_m_g_q_mla': 'new_m', 'new_m_g_k_mla': 'new_m', 'new_m_g_q_dil': 'new_m', 'new_m_g_k_dil': 'new_m', 'new_m_w_br_mla': 'new_m', 'new_m_w_br_dil': 'new_m', 'new_m_w_o': 'new_m', 'new_m_g_ffn2': 'new_m', 'new_m_w2_gate': 'new_m', 'new_m_w2_up': 'new_m', 'new_m_w2_down': 'new_m', 'new_m_g_ple': 'new_m', 'new_m_w_ple_gate': 'new_m', 'new_m_w_ple_proj': 'new_m', 'new_v_g_ffn1': 'new_v', 'new_v_w1_gate': 'new_v', 'new_v_w1_up': 'new_v', 'new_v_w1_down': 'new_v', 'new_v_g_mix': 'new_v', 'new_v_w_in': 'new_v', 'new_v_g_cq': 'new_v', 'new_v_w_uq': 'new_v', 'new_v_g_ckv': 'new_v', 'new_v_w_ukv': 'new_v', 'new_v_g_q_mla': 'new_v', 'new_v_g_k_mla': 'new_v', 'new_v_g_q_dil': 'new_v', 'new_v_g_k_dil': 'new_v', 'new_v_w_br_mla': 'new_v', 'new_v_w_br_dil': 'new_v', 'new_v_w_o': 'new_v', 'new_v_g_ffn2': 'new_v', 'new_v_w2_gate': 'new_v', 'new_v_w2_up': 'new_v', 'new_v_w2_down': 'new_v', 'new_v_g_ple': 'new_v', 'new_v_w_ple_gate': 'new_v', 'new_v_w_ple_proj': 'new_v'}


def _forward(args):
    return _fwd_reference(*[args[k] for k in FWD_PARAMS])


def _output_shape():
    def fwd():
        inp = _fwd_setup_inputs(0)
        return _fwd_reference(*[inp[k] for k in FWD_PARAMS])
    out = _jax.eval_shape(fwd)
    return out.shape, out.dtype

N_MICROBATCH = 1
ADAM_LR = 0.001
ADAM_B1 = 0.9
ADAM_B2 = 0.999
ADAM_EPS = 1e-08
ADAM_WD = 0.01
ADAM_STEP = 10
PER_EXAMPLE_BATCH_AXIS = {'x': 0, 'p': 1, 'positions': 0, 'loss_target': 0}
SHARED_INPUTS = []
_WEIGHT_DTYPES = {'g_ffn1': _jnp.float32, 'w1_gate': _jnp.float32, 'w1_up': _jnp.float32, 'w1_down': _jnp.float32, 'g_mix': _jnp.float32, 'w_in': _jnp.float32, 'g_cq': _jnp.float32, 'w_uq': _jnp.float32, 'g_ckv': _jnp.float32, 'w_ukv': _jnp.float32, 'g_q_mla': _jnp.float32, 'g_k_mla': _jnp.float32, 'g_q_dil': _jnp.float32, 'g_k_dil': _jnp.float32, 'w_br_mla': _jnp.float32, 'w_br_dil': _jnp.float32, 'w_o': _jnp.float32, 'g_ffn2': _jnp.float32, 'w2_gate': _jnp.float32, 'w2_up': _jnp.float32, 'w2_down': _jnp.float32, 'g_ple': _jnp.float32, 'w_ple_gate': _jnp.float32, 'w_ple_proj': _jnp.float32}
MOMENT_SCALE = {'g_ffn1': 6.236347e+00, 'w1_gate': 4.487007e-02, 'w1_up': 4.904969e-02, 'w1_down': 7.924017e-02, 'g_mix': 1.012059e-01, 'w_in': 2.615946e-02, 'g_cq': 5.329777e-02, 'w_uq': 3.088621e-02, 'g_ckv': 6.915485e-01, 'w_ukv': 4.158327e-02, 'g_q_mla': 5.346654e-01, 'g_k_mla': 5.367112e-01, 'g_q_dil': 2.767777e-01, 'g_k_dil': 2.763110e-01, 'w_br_mla': 3.173386e-02, 'w_br_dil': 2.532163e-02, 'w_o': 3.886333e-02, 'g_ffn2': 6.354335e+00, 'w2_gate': 4.335824e-02, 'w2_up': 4.899573e-02, 'w2_down': 7.893243e-02, 'g_ple': 9.903115e-01, 'w_ple_gate': 5.085941e-02, 'w_ple_proj': 4.571378e-01}


def _to_microbatches(a, axis):
    t = _jnp.moveaxis(a, axis, 0)
    t = t.reshape((N_MICROBATCH, t.shape[0] // N_MICROBATCH) + t.shape[1:])
    return _jnp.moveaxis(t, 1, axis + 1)


def setup_inputs(seed: int = 0) -> dict:
    inp = _fwd_setup_inputs(seed)
    key = _jax.random.fold_in(_jax.random.key(seed), 7919)
    shape, _ = _output_shape()
    out = dict(inp)
    out["loss_target"] = _jax.random.normal(_jax.random.fold_in(key, 0), shape, _jnp.float32)
    for i, name in enumerate(TWIN_WEIGHTS):
        w = inp[name].astype(_jnp.float32)
        if MOMENT_SCALE is None:
            s = _jnp.sqrt(_jnp.mean(_jnp.square(w)) + 1e-30)
        else:
            s = MOMENT_SCALE[name]
        km, kv = _jax.random.split(_jax.random.fold_in(key, i + 1))
        out[name] = w
        out["m_" + name] = s * _jax.random.normal(km, w.shape, _jnp.float32)
        out["v_" + name] = (s * s) * _jax.random.uniform(kv, w.shape, _jnp.float32, 0.5, 1.5)
    if N_MICROBATCH > 1:
        for name, axis in PER_EXAMPLE_BATCH_AXIS.items():
            out[name] = _to_microbatches(out[name], axis)
    return {'x': out['x'], 'p': out['p'], 'positions': out['positions'], 'g_ffn1': out['g_ffn1'], 'w1_gate': out['w1_gate'], 'w1_up': out['w1_up'], 'w1_down': out['w1_down'], 'g_mix': out['g_mix'], 'w_in': out['w_in'], 'g_cq': out['g_cq'], 'w_uq': out['w_uq'], 'g_ckv': out['g_ckv'], 'w_ukv': out['w_ukv'], 'g_q_mla': out['g_q_mla'], 'g_k_mla': out['g_k_mla'], 'g_q_dil': out['g_q_dil'], 'g_k_dil': out['g_k_dil'], 'w_br_mla': out['w_br_mla'], 'w_br_dil': out['w_br_dil'], 'w_o': out['w_o'], 'g_ffn2': out['g_ffn2'], 'w2_gate': out['w2_gate'], 'w2_up': out['w2_up'], 'w2_down': out['w2_down'], 'g_ple': out['g_ple'], 'w_ple_gate': out['w_ple_gate'], 'w_ple_proj': out['w_ple_proj'], 'loss_target': out['loss_target'], 'm_g_ffn1': out['m_g_ffn1'], 'm_w1_gate': out['m_w1_gate'], 'm_w1_up': out['m_w1_up'], 'm_w1_down': out['m_w1_down'], 'm_g_mix': out['m_g_mix'], 'm_w_in': out['m_w_in'], 'm_g_cq': out['m_g_cq'], 'm_w_uq': out['m_w_uq'], 'm_g_ckv': out['m_g_ckv'], 'm_w_ukv': out['m_w_ukv'], 'm_g_q_mla': out['m_g_q_mla'], 'm_g_k_mla': out['m_g_k_mla'], 'm_g_q_dil': out['m_g_q_dil'], 'm_g_k_dil': out['m_g_k_dil'], 'm_w_br_mla': out['m_w_br_mla'], 'm_w_br_dil': out['m_w_br_dil'], 'm_w_o': out['m_w_o'], 'm_g_ffn2': out['m_g_ffn2'], 'm_w2_gate': out['m_w2_gate'], 'm_w2_up': out['m_w2_up'], 'm_w2_down': out['m_w2_down'], 'm_g_ple': out['m_g_ple'], 'm_w_ple_gate': out['m_w_ple_gate'], 'm_w_ple_proj': out['m_w_ple_proj'], 'v_g_ffn1': out['v_g_ffn1'], 'v_w1_gate': out['v_w1_gate'], 'v_w1_up': out['v_w1_up'], 'v_w1_down': out['v_w1_down'], 'v_g_mix': out['v_g_mix'], 'v_w_in': out['v_w_in'], 'v_g_cq': out['v_g_cq'], 'v_w_uq': out['v_w_uq'], 'v_g_ckv': out['v_g_ckv'], 'v_w_ukv': out['v_w_ukv'], 'v_g_q_mla': out['v_g_q_mla'], 'v_g_k_mla': out['v_g_k_mla'], 'v_g_q_dil': out['v_g_q_dil'], 'v_g_k_dil': out['v_g_k_dil'], 'v_w_br_mla': out['v_w_br_mla'], 'v_w_br_dil': out['v_w_br_dil'], 'v_w_o': out['v_w_o'], 'v_g_ffn2': out['v_g_ffn2'], 'v_w2_gate': out['v_w2_gate'], 'v_w2_up': out['v_w2_up'], 'v_w2_down': out['v_w2_down'], 'v_g_ple': out['v_g_ple'], 'v_w_ple_gate': out['v_w_ple_gate'], 'v_w_ple_proj': out['v_w_ple_proj']}


def _loss(weights, diff, rest, loss_target):
    with _jax.named_scope("forward"):
        args = {**rest, TWIN_DIFF_INPUT: diff, **{k: w.astype(_WEIGHT_DTYPES[k]) for k, w in weights.items()}}
        y = _forward(args)
    with _jax.named_scope("loss_head"):
        err = _jnp.square(y.astype(_jnp.float32) - loss_target)
        return 0.5 * _jnp.sum(_jnp.mean(err, axis=-1)) if err.ndim else 0.5 * err


def _adamw(w, g, m, v):
    m = ADAM_B1 * m + (1.0 - ADAM_B1) * g
    v = ADAM_B2 * v + (1.0 - ADAM_B2) * _jnp.square(g)
    m_hat = m / (1.0 - ADAM_B1 ** ADAM_STEP)
    v_hat = v / (1.0 - ADAM_B2 ** ADAM_STEP)
    delta = -ADAM_LR * (m_hat / (_jnp.sqrt(v_hat) + ADAM_EPS) + ADAM_WD * w)
    return delta, m, v


def reference(x, p, positions, g_ffn1, w1_gate, w1_up, w1_down, g_mix, w_in, g_cq, w_uq, g_ckv, w_ukv, g_q_mla, g_k_mla, g_q_dil, g_k_dil, w_br_mla, w_br_dil, w_o, g_ffn2, w2_gate, w2_up, w2_down, g_ple, w_ple_gate, w_ple_proj, loss_target, m_g_ffn1, m_w1_gate, m_w1_up, m_w1_down, m_g_mix, m_w_in, m_g_cq, m_w_uq, m_g_ckv, m_w_ukv, m_g_q_mla, m_g_k_mla, m_g_q_dil, m_g_k_dil, m_w_br_mla, m_w_br_dil, m_w_o, m_g_ffn2, m_w2_gate, m_w2_up, m_w2_down, m_g_ple, m_w_ple_gate, m_w_ple_proj, v_g_ffn1, v_w1_gate, v_w1_up, v_w1_down, v_g_mix, v_w_in, v_g_cq, v_w_uq, v_g_ckv, v_w_ukv, v_g_q_mla, v_g_k_mla, v_g_q_dil, v_g_k_dil, v_w_br_mla, v_w_br_dil, v_w_o, v_g_ffn2, v_w2_gate, v_w2_up, v_w2_down, v_g_ple, v_w_ple_gate, v_w_ple_proj):
    given = dict(x=x, p=p, positions=positions, g_ffn1=g_ffn1, w1_gate=w1_gate, w1_up=w1_up, w1_down=w1_down, g_mix=g_mix, w_in=w_in, g_cq=g_cq, w_uq=w_uq, g_ckv=g_ckv, w_ukv=w_ukv, g_q_mla=g_q_mla, g_k_mla=g_k_mla, g_q_dil=g_q_dil, g_k_dil=g_k_dil, w_br_mla=w_br_mla, w_br_dil=w_br_dil, w_o=w_o, g_ffn2=g_ffn2, w2_gate=w2_gate, w2_up=w2_up, w2_down=w2_down, g_ple=g_ple, w_ple_gate=w_ple_gate, w_ple_proj=w_ple_proj, loss_target=loss_target, m_g_ffn1=m_g_ffn1, m_w1_gate=m_w1_gate, m_w1_up=m_w1_up, m_w1_down=m_w1_down, m_g_mix=m_g_mix, m_w_in=m_w_in, m_g_cq=m_g_cq, m_w_uq=m_w_uq, m_g_ckv=m_g_ckv, m_w_ukv=m_w_ukv, m_g_q_mla=m_g_q_mla, m_g_k_mla=m_g_k_mla, m_g_q_dil=m_g_q_dil, m_g_k_dil=m_g_k_dil, m_w_br_mla=m_w_br_mla, m_w_br_dil=m_w_br_dil, m_w_o=m_w_o, m_g_ffn2=m_g_ffn2, m_w2_gate=m_w2_gate, m_w2_up=m_w2_up, m_w2_down=m_w2_down, m_g_ple=m_g_ple, m_w_ple_gate=m_w_ple_gate, m_w_ple_proj=m_w_ple_proj, v_g_ffn1=v_g_ffn1, v_w1_gate=v_w1_gate, v_w1_up=v_w1_up, v_w1_down=v_w1_down, v_g_mix=v_g_mix, v_w_in=v_w_in, v_g_cq=v_g_cq, v_w_uq=v_w_uq, v_g_ckv=v_g_ckv, v_w_ukv=v_w_ukv, v_g_q_mla=v_g_q_mla, v_g_k_mla=v_g_k_mla, v_g_q_dil=v_g_q_dil, v_g_k_dil=v_g_k_dil, v_w_br_mla=v_w_br_mla, v_w_br_dil=v_w_br_dil, v_w_o=v_w_o, v_g_ffn2=v_g_ffn2, v_w2_gate=v_w2_gate, v_w2_up=v_w2_up, v_w2_down=v_w2_down, v_g_ple=v_g_ple, v_w_ple_gate=v_w_ple_gate, v_w_ple_proj=v_w_ple_proj)
    weights = {n: given[n] for n in TWIN_WEIGHTS}
    shared = {n: given[n] for n in SHARED_INPUTS}
    per_example = {n: given[n] for n in ['x', 'p', 'positions']}
    grad_fn = _jax.value_and_grad(_loss, argnums=(0, 1))

    def one_microbatch(ex, loss_target):
        ex = dict(ex)
        diff = ex.pop(TWIN_DIFF_INPUT)
        return grad_fn(weights, diff, {**shared, **ex}, loss_target)

    if N_MICROBATCH == 1:
        loss, (grad_w, grad_x) = one_microbatch(per_example, given["loss_target"])
    else:
        def body(carry, xs):
            loss_sum, grad_sum = carry
            l_k, (gw_k, gx_k) = one_microbatch(xs[0], xs[1])
            with _jax.named_scope("update"):
                return (loss_sum + l_k, _jax.tree.map(_jnp.add, grad_sum, gw_k)), gx_k

        init = (_jnp.zeros((), _jnp.float32), _jax.tree.map(_jnp.zeros_like, weights))
        (loss, grad_w), grad_x = _jax.lax.scan(body, init, (per_example, given["loss_target"]))
    with _jax.named_scope("update"):
        delta_w, new_m, new_v = {}, {}, {}
        for n in TWIN_WEIGHTS:
            delta_w[n], new_m[n], new_v[n] = _adamw(weights[n], grad_w[n], given["m_" + n], given["v_" + n])
    return (loss, grad_x, *[grad_w[n] for n in TWIN_WEIGHTS], *[delta_w[n] for n in TWIN_WEIGHTS],
            *[new_m[n] for n in TWIN_WEIGHTS], *[new_v[n] for n in TWIN_WEIGHTS])
```

```python
import functools

import numpy as np
import jax
import jax.numpy as jnp
from jax import lax
from jax.experimental import pallas as pl
from jax.experimental.pallas import tpu as pltpu

F32 = jnp.float32
BF16 = jnp.bfloat16
MESH_ID = pl.DeviceIdType.MESH

MLA_NOPE = 128
MLA_ROPE = 64
MLA_V = 128
MLA_QK = MLA_NOPE + MLA_ROPE
MLA_QK_PAD = 256
DIL_GROUPS = ((128, 1), (512, 4), (2048, 16))
DIL_HEAD = 128
DIL_ROT = DIL_HEAD // 4
DIL_BLOCK = 128
ROPE_THETA = 500000.0
EPS = 1e-6
NEG = -1e30
ADAM_LR = 0.001
ADAM_B1 = 0.9
ADAM_B2 = 0.999
ADAM_EPS = 1e-08
ADAM_WD = 0.01
ADAM_STEP = 10

LANES = 128
PACK_COLS = 1024
VMEM_LIMIT_BYTES = 56 * 1024 * 1024

BIG = ("w1_gate", "w1_up", "w1_down", "w_in", "w_uq", "w_ukv", "w_br_mla", "w_br_dil", "w_o",
       "w2_gate", "w2_up", "w2_down", "w_ple_gate", "w_ple_proj")
ROW_SHARDED = ("w1_down", "w_o", "w2_down", "w_ple_gate")
SMALL = ("g_ffn1", "g_mix", "g_cq", "g_ckv", "g_q_mla", "g_k_mla", "g_q_dil", "g_k_dil", "g_ffn2", "g_ple")
WEIGHTS = ("g_ffn1", "w1_gate", "w1_up", "w1_down", "g_mix", "w_in", "g_cq", "w_uq", "g_ckv", "w_ukv", "g_q_mla",
           "g_k_mla", "g_q_dil", "g_k_dil", "w_br_mla", "w_br_dil", "w_o", "g_ffn2", "w2_gate", "w2_up", "w2_down",
           "g_ple", "w_ple_gate", "w_ple_proj")


def _pick(n, target, align=LANES):
    if n <= target:
        return n
    t = (target // align) * align
    while t >= align:
        if n % t == 0:
            return t
        t -= align
    return n


def _params(n_axes):
    return pltpu.CompilerParams(dimension_semantics=("arbitrary",) * n_axes, vmem_limit_bytes=VMEM_LIMIT_BYTES)


def _sigmoid(x):
    return 1.0 / (1.0 + jnp.exp(-x))


def _mm(a, b, mode, out_dtype, name, res=None, alpha=1.0, tm=1024, tn=1536, tk=1024):
    if mode == "nn":
        (M, K), (K2, N) = a.shape, b.shape
    elif mode == "nt":
        (M, K), (N, K2) = a.shape, b.shape
    else:
        (K, M), (K2, N) = a.shape, b.shape
    assert K == K2, (name, a.shape, b.shape)
    assert a.dtype == BF16 and b.dtype == BF16, name
    tm, tn, tk = _pick(M, tm), _pick(N, tn), _pick(K, tk)
    nk = K // tk
    if mode == "nn":
        a_spec = pl.BlockSpec((tm, tk), lambda i, j, k: (i, k))
        b_spec = pl.BlockSpec((tk, tn), lambda i, j, k: (k, j))
        dims = (((1,), (0,)), ((), ()))
    elif mode == "nt":
        a_spec = pl.BlockSpec((tm, tk), lambda i, j, k: (i, k))
        b_spec = pl.BlockSpec((tn, tk), lambda i, j, k: (j, k))
        dims = (((1,), (1,)), ((), ()))
    else:
        a_spec = pl.BlockSpec((tk, tm), lambda i, j, k: (k, i))
        b_spec = pl.BlockSpec((tk, tn), lambda i, j, k: (k, j))
        dims = (((0,), (0,)), ((), ()))
    o_spec = pl.BlockSpec((tm, tn), lambda i, j, k: (i, j))
    has_res = res is not None

    def kern(*refs):
        a_ref, b_ref = refs[0], refs[1]
        r_ref = refs[2] if has_res else None
        o_ref = refs[3] if has_res else refs[2]
        part = lax.dot_general(a_ref[...], b_ref[...], dims, preferred_element_type=F32)

        def finish(r):
            if alpha != 1.0:
                r = r * alpha
            if has_res:
                r = r_ref[...] + r
            o_ref[...] = r.astype(o_ref.dtype)

        if nk == 1:
            finish(part)
            return
        acc_ref = refs[-1]
        k = pl.program_id(2)

        @pl.when(k == 0)
        def _():
            acc_ref[...] = part

        @pl.when(k > 0)
        def _():
            acc_ref[...] += part

        @pl.when(k == nk - 1)
        def _():
            finish(acc_ref[...])

    ins = (a, b, res) if has_res else (a, b)
    in_specs = [a_spec, b_spec] + ([o_spec] if has_res else [])
    return pl.pallas_call(
        kern, name=name, grid=(M // tm, N // tn, nk), in_specs=in_specs, out_specs=o_spec,
        out_shape=jax.ShapeDtypeStruct((M, N), out_dtype),
        scratch_shapes=[pltpu.VMEM((tm, tn), F32)] if nk > 1 else [],
        compiler_params=_params(3))(*ins)


def _vcall(body, grid, ins, in_specs, out_shapes, out_specs, name, n_inner_acc=0, n_acc=0):
    n_in, n_out = len(ins), len(out_shapes)
    n_plain = n_out - n_acc - n_inner_acc

    def kern(*refs):
        vals = body(*[r[...] for r in refs[:n_in]])
        if not isinstance(vals, (tuple, list)):
            vals = (vals,)
        out_refs = refs[n_in:]
        inner_first = pl.program_id(len(grid) - 1) == 0
        first = inner_first
        for ax in range(len(grid) - 1):
            first = jnp.logical_and(first, pl.program_id(ax) == 0)
        for idx, (r, v) in enumerate(zip(out_refs, vals)):
            if idx < n_plain:
                r[...] = v.astype(r.dtype)
                continue
            start = inner_first if idx < n_plain + n_inner_acc else first

            @pl.when(start)
            def _(r=r, v=v):
                r[...] = v.astype(r.dtype)

            @pl.when(jnp.logical_not(start))
            def _(r=r, v=v):
                r[...] += v.astype(r.dtype)

    out = pl.pallas_call(kern, name=name, grid=grid, in_specs=in_specs, out_specs=out_specs, out_shape=out_shapes,
                         compiler_params=_params(len(grid)))(*ins)
    return out


def _rows(tm, c):
    return pl.BlockSpec((tm, c), lambda i: (i, 0))


def _vec(c):
    return pl.BlockSpec((1, c), lambda i: (0, 0))


def _sds(shape, dtype):
    return jax.ShapeDtypeStruct(shape, dtype)


def _rstd(x, c):
    return lax.rsqrt(jnp.sum(x * x, axis=-1, keepdims=True) * (1.0 / c) + EPS)


def _rms_bwd(xh, r, g, dn, c):
    u = dn * g
    dx = r * (u - xh * (jnp.sum(xh * u, axis=-1, keepdims=True) * (1.0 / c)))
    return dx, jnp.sum(dn * xh, axis=0, keepdims=True)


def _rope(t, c, sa, sb, half):
    return t * c + pltpu.roll(t, LANES - half, 1) * sa + pltpu.roll(t, half, 1) * sb


def _rope_t(d, c, sa, sb, half):
    return d * c + pltpu.roll(d * sa, half, 1) + pltpu.roll(d * sb, LANES - half, 1)


def _rope_tables(pos_b, rd, name):
    T = pos_b.shape[0]
    half = rd // 2
    inv = ROPE_THETA ** (-jnp.arange(half, dtype=F32) * 2.0 / rd)
    inv_full = jnp.concatenate([inv, inv, jnp.zeros((LANES - rd,), F32)]).reshape(1, LANES)
    lane = np.arange(LANES)
    ma = jnp.asarray((lane < half).astype(np.float32)).reshape(1, LANES)
    mb = jnp.asarray(((lane >= half) & (lane < rd)).astype(np.float32)).reshape(1, LANES)
    tm = _pick(T, 1024, 8)

    def body(pos, invf, a, b):
        ang = pos * invf
        c, s = jnp.cos(ang), jnp.sin(ang)
        inside = a + b
        return c * inside + (1.0 - inside), -s * a, s * b

    return _vcall(body, (T // tm,), (pos_b, inv_full, ma, mb), [_rows(tm, LANES)] + [_vec(LANES)] * 3,
                  [_sds((T, LANES), F32)] * 3, [_rows(tm, LANES)] * 3, name)


def _rms_fwd(x, g, name):
    T, C = x.shape
    tm = _pick(T, 512, 8)

    def body(xv, gv):
        return xv * _rstd(xv, C) * gv

    return _vcall(body, (T // tm,), (x, g), [_rows(tm, C), _vec(C)], [_sds((T, C), BF16)], [_rows(tm, C)], name)[0]


def _rms_bwd_call(x, g, dn, dres, name):
    T, C = x.shape
    tm = _pick(T, 256, 8)

    def body(xv, gv, dnv, drv):
        r = _rstd(xv, C)
        dx, dg = _rms_bwd(xv * r, r, gv, dnv.astype(F32), C)
        return drv + dx, dg

    return _vcall(body, (T // tm,), (x, g, dn, dres), [_rows(tm, C), _vec(C), _rows(tm, C), _rows(tm, C)],
                  [_sds((T, C), F32), _sds((1, C), F32)], [_rows(tm, C), _vec(C)], name, n_acc=1)


def _silu_mul(ab, tf, name):
    T, F2 = ab.shape
    F = F2 // 2
    tm = _pick(T, 512, 8)

    def body(v):
        a, b = v[:, :tf].astype(F32), v[:, tf:].astype(F32)
        return a * _sigmoid(a) * b

    return _vcall(body, (T // tm, F // tf), (ab,), [pl.BlockSpec((tm, 2 * tf), lambda i, j: (i, j))],
                  [_sds((T, F), BF16)], [pl.BlockSpec((tm, tf), lambda i, j: (i, j))], name)[0]


def _swiglu_bwd(dact, ab, tf, name):
    T, F = dact.shape
    tm = _pick(T, 512, 8)

    def body(dv, v):
        d = dv.astype(F32)
        a, b = v[:, :tf].astype(F32), v[:, tf:].astype(F32)
        sg = _sigmoid(a)
        da = d * b * (sg * (1.0 + a * (1.0 - sg)))
        db = d * (a * sg)
        return jnp.concatenate([da, db], axis=1)

    return _vcall(body, (T // tm, F // tf), (dact, ab),
                  [pl.BlockSpec((tm, tf), lambda i, j: (i, j)), pl.BlockSpec((tm, 2 * tf), lambda i, j: (i, j))],
                  [_sds((T, 2 * F), BF16)], [pl.BlockSpec((tm, 2 * tf), lambda i, j: (i, j))], name)[0]


def _lat_fwd(lat, g_cq, g_ckv, name):
    T, LP = lat.shape
    QL, KVL = g_cq.shape[1], g_ckv.shape[1]
    tm = _pick(T, 512, 8)

    def body(v, gq, gk):
        xq, xk = v[:, :QL], v[:, QL:QL + KVL]
        return xq * _rstd(xq, QL) * gq, xk * _rstd(xk, KVL) * gk

    return _vcall(body, (T // tm,), (lat, g_cq, g_ckv), [_rows(tm, LP), _vec(QL), _vec(KVL)],
                  [_sds((T, QL), BF16), _sds((T, KVL), BF16)], [_rows(tm, QL), _rows(tm, KVL)], name)


def _lat_bwd(dcq, dckv, dkr, lat, g_cq, g_ckv, name):
    T, LP = lat.shape
    QL, KVL = g_cq.shape[1], g_ckv.shape[1]
    tm = _pick(T, 512, 8)

    def body(dq, dk, dr, v, gq, gk):
        xq, xk = v[:, :QL], v[:, QL:QL + KVL]
        rq, rk = _rstd(xq, QL), _rstd(xk, KVL)
        dxq, dgq = _rms_bwd(xq * rq, rq, gq, dq, QL)
        dxk, dgk = _rms_bwd(xk * rk, rk, gk, dk, KVL)
        return jnp.concatenate([dxq, dxk, dr], axis=1), dgq, dgk

    return _vcall(body, (T // tm,), (dcq, dckv, dkr, lat, g_cq, g_ckv),
                  [_rows(tm, QL), _rows(tm, KVL), _rows(tm, LANES), _rows(tm, LP), _vec(QL), _vec(KVL)],
                  [_sds((T, LP), BF16), _sds((1, QL), F32), _sds((1, KVL), F32)],
                  [_rows(tm, LP), _vec(QL), _vec(KVL)], name, n_acc=2)


def _head_spec(tm, w):
    return pl.BlockSpec((tm, w), lambda i, h: (i, h))


def _row2(tm, w, col=0):
    return pl.BlockSpec((tm, w), lambda i, h: (i, col))


def _vec2(w):
    return pl.BlockSpec((1, w), lambda i, h: (0, 0))


def _mla_q_prep(q_raw, g_q, tabs, H, scale, name):
    T = q_raw.shape[0]
    tm = _pick(T, 512, 8)
    half = MLA_ROPE // 2

    def body(x, g, c, sa, sb):
        n = x * _rstd(x, MLA_QK) * g
        return jnp.concatenate([n[:, :LANES], _rope(n[:, LANES:], c, sa, sb, half)], axis=1) * scale

    return _vcall(body, (T // tm, H), (q_raw, g_q) + tabs,
                  [_head_spec(tm, MLA_QK_PAD), _vec2(MLA_QK_PAD)] + [_row2(tm, LANES)] * 3,
                  [_sds((T, H * MLA_QK_PAD), BF16)], [_head_spec(tm, MLA_QK_PAD)], name)[0]


def _mla_q_bwd(dq, q_raw, g_q, tabs, H, scale, name):
    T = q_raw.shape[0]
    tm = _pick(T, 512, 8)
    half = MLA_ROPE // 2

    def body(d, x, g, c, sa, sb):
        r = _rstd(x, MLA_QK)
        d = d * scale
        dn = jnp.concatenate([d[:, :LANES], _rope_t(d[:, LANES:], c, sa, sb, half)], axis=1)
        return _rms_bwd(x * r, r, g, dn, MLA_QK)

    return _vcall(body, (T // tm, H), (dq, q_raw, g_q) + tabs,
                  [_head_spec(tm, MLA_QK_PAD), _head_spec(tm, MLA_QK_PAD), _vec2(MLA_QK_PAD)] + [_row2(tm, LANES)] * 3,
                  [_sds((T, H * MLA_QK_PAD), BF16), _sds((1, MLA_QK_PAD), F32)],
                  [_head_spec(tm, MLA_QK_PAD), _vec2(MLA_QK_PAD)], name, n_acc=1)


def _mla_k_prep(kv, lat, kr_col, g_k, tabs, H, name):
    T = kv.shape[0]
    tm = _pick(T, 512, 8)
    half = MLA_ROPE // 2

    def body(x, kr, g, c, sa, sb):
        kn = x[:, :LANES]
        r = lax.rsqrt((jnp.sum(kn * kn, axis=-1, keepdims=True) + jnp.sum(kr * kr, axis=-1, keepdims=True))
                      * (1.0 / MLA_QK) + EPS)
        k0 = kn * r * g[:, :LANES]
        k1 = _rope(kr * r * g[:, LANES:], c, sa, sb, half)
        return jnp.concatenate([k0, k1], axis=1), x[:, LANES:]

    return _vcall(body, (T // tm, H), (kv, lat, g_k) + tabs,
                  [_head_spec(tm, 2 * LANES), _row2(tm, LANES, kr_col), _vec2(MLA_QK_PAD)] + [_row2(tm, LANES)] * 3,
                  [_sds((T, H * MLA_QK_PAD), BF16), _sds((T, H * MLA_V), BF16)],
                  [_head_spec(tm, MLA_QK_PAD), _head_spec(tm, MLA_V)], name)


def _mla_k_bwd(dk, dv, kv, lat, kr_col, g_k, tabs, H, name):
    T = kv.shape[0]
    tm = _pick(T, 512, 8)
    half = MLA_ROPE // 2

    def body(d, dvv, x, kr, g, c, sa, sb):
        xx = jnp.concatenate([x[:, :LANES], kr], axis=1)
        r = _rstd(xx, MLA_QK)
        dn = jnp.concatenate([d[:, :LANES], _rope_t(d[:, LANES:], c, sa, sb, half)], axis=1)
        dx, dg = _rms_bwd(xx * r, r, g, dn, MLA_QK)
        return jnp.concatenate([dx[:, :LANES], dvv], axis=1), dx[:, LANES:], dg

    return _vcall(body, (T // tm, H), (dk, dv, kv, lat, g_k) + tabs,
                  [_head_spec(tm, MLA_QK_PAD), _head_spec(tm, MLA_V), _head_spec(tm, 2 * LANES),
                   _row2(tm, LANES, kr_col), _vec2(MLA_QK_PAD)] + [_row2(tm, LANES)] * 3,
                  [_sds((T, H * 2 * LANES), BF16), _sds((T, LANES), F32), _sds((1, MLA_QK_PAD), F32)],
                  [_head_spec(tm, 2 * LANES), _row2(tm, LANES), _vec2(MLA_QK_PAD)], name, n_inner_acc=1, n_acc=1)


def _dil_prep(pd, g_q, g_k, tabs, HD, scale, name):
    T = pd.shape[0]
    W = HD * DIL_HEAD
    G = len(DIL_GROUPS)
    tm = _pick(T, 256, 8)
    half = DIL_ROT // 2

    def body(xq, xk, gq, gk, c, sa, sb):
        outs = []
        for x, g, s in ((xq, gq, scale), (xk, gk, 1.0)):
            heads = []
            for h in range(HD):
                xs = x[:, h * DIL_HEAD:(h + 1) * DIL_HEAD].astype(F32)
                n = _rope(xs * _rstd(xs, DIL_HEAD) * g, c, sa, sb, half)
                heads.append(n * s if s != 1.0 else n)
            outs.append(jnp.concatenate(heads, axis=1))
        return tuple(outs)

    gspec = pl.BlockSpec((None, 1, DIL_HEAD), lambda i, g: (g, 0, 0))
    return _vcall(body, (T // tm, G), (pd, pd, g_q, g_k) + tabs,
                  [pl.BlockSpec((tm, W), lambda i, g: (i, 3 * g)), pl.BlockSpec((tm, W), lambda i, g: (i, 3 * g + 1)),
                   gspec, gspec] + [_row2(tm, LANES)] * 3,
                  [_sds((T, G * W), BF16)] * 2, [pl.BlockSpec((tm, W), lambda i, g: (i, g))] * 2, name)


def _dil_prep_bwd(dq, dk, dv, pd, grp, g_q, g_k, tabs, HD, scale, name):
    T = pd.shape[0]
    W = HD * DIL_HEAD
    tm = _pick(T, 256, 8)
    half = DIL_ROT // 2

    def body(dqv, dkv, dvv, xq, xk, gq, gk, c, sa, sb):
        cols, dgs = [], []
        for d, x, g, s in ((dqv, xq, gq, scale), (dkv, xk, gk, 1.0)):
            heads, dg = [], None
            for h in range(HD):
                sl = slice(h * DIL_HEAD, (h + 1) * DIL_HEAD)
                xs = x[:, sl].astype(F32)
                r = _rstd(xs, DIL_HEAD)
                dh = d[:, sl] * s if s != 1.0 else d[:, sl]
                dx, dgh = _rms_bwd(xs * r, r, g, _rope_t(dh, c, sa, sb, half), DIL_HEAD)
                heads.append(dx)
                dg = dgh if dg is None else dg + dgh
            cols.append(jnp.concatenate(heads, axis=1))
            dgs.append(dg)
        return jnp.concatenate(cols + [dvv], axis=1), dgs[0], dgs[1]

    gq, gk = g_q[grp], g_k[grp]
    return _vcall(body, (T // tm,), (dq, dk, dv, pd, pd, gq, gk) + tabs,
                  [_rows(tm, W)] * 3 + [pl.BlockSpec((tm, W), lambda i: (i, 3 * grp)),
                                        pl.BlockSpec((tm, W), lambda i: (i, 3 * grp + 1)),
                                        _vec(DIL_HEAD), _vec(DIL_HEAD)] + [_rows(tm, LANES)] * 3,
                  [_sds((T, 3 * W), BF16), _sds((1, DIL_HEAD), F32), _sds((1, DIL_HEAD), F32)],
                  [_rows(tm, 3 * W), _vec(DIL_HEAD), _vec(DIL_HEAD)], name, n_acc=2)


def _dil_merge(os_, lses, name):
    T, W = os_[0].shape
    tm = _pick(T, 256, 8)

    def body(o0, o1, o2, l0, l1, l2):
        m = jnp.maximum(jnp.maximum(l0, l1), l2)
        w0, w1, w2 = jnp.exp(l0 - m), jnp.exp(l1 - m), jnp.exp(l2 - m)
        z = w0 + w1 + w2
        return (w0 * o0 + w1 * o1 + w2 * o2) / z, m + jnp.log(z)

    return _vcall(body, (T // tm,), tuple(os_) + tuple(lses), [_rows(tm, W)] * 6,
                  [_sds((T, W), BF16), _sds((T, W), F32)], [_rows(tm, W)] * 2, name)


def _gate_merge(pg, bm, bd, name):
    T, D = bm.shape
    tm = _pick(T, 256, 8)

    def body(g, m, d):
        g = g.astype(F32)
        return _sigmoid(g[:, :D]) * m + _sigmoid(g[:, D:]) * d

    return _vcall(body, (T // tm,), (pg, bm, bd), [_rows(tm, 2 * D), _rows(tm, D), _rows(tm, D)],
                  [_sds((T, D), BF16)], [_rows(tm, D)], name)[0]


def _gate_bwd(dmerged, pg, bm, bd, name):
    T, D = bm.shape
    tm = _pick(T, 256, 8)

    def body(dm, g, m, d):
        g = g.astype(F32)
        s0, s1 = _sigmoid(g[:, :D]), _sigmoid(g[:, D:])
        dpg = jnp.concatenate([dm * m * s0 * (1.0 - s0), dm * d * s1 * (1.0 - s1)], axis=1)
        return dm * s0, dm * s1, dpg

    return _vcall(body, (T // tm,), (dmerged, pg, bm, bd), [_rows(tm, D), _rows(tm, 2 * D), _rows(tm, D), _rows(tm, D)],
                  [_sds((T, D), BF16), _sds((T, D), BF16), _sds((T, 2 * D), BF16)],
                  [_rows(tm, D), _rows(tm, D), _rows(tm, 2 * D)], name)


def _ple_loss(x3, zg, pp, target, name):
    T, D = x3.shape
    tm = _pick(T, 256, 8)

    def body(x, z, p_, t):
        s = _sigmoid(z)
        e = x + s * p_ - t
        dy = e * (1.0 / D)
        part = 0.5 * jnp.sum(jnp.sum(e * e, axis=1, keepdims=True), axis=0, keepdims=True) * (1.0 / D)
        return dy, dy * s, dy * p_ * s * (1.0 - s), jnp.broadcast_to(part, (1, LANES))

    return _vcall(body, (T // tm,), (x3, zg, pp, target), [_rows(tm, D)] * 4,
                  [_sds((T, D), F32), _sds((T, D), BF16), _sds((T, D), BF16), _sds((1, LANES), F32)],
                  [_rows(tm, D)] * 3 + [_vec(LANES)], name, n_acc=1)


NT = (((1,), (1,)), ((), ()))
TN = (((0,), (0,)), ((), ()))


def _causal_mask(s, qi, kj, tq, tk):
    row = lax.broadcasted_iota(jnp.int32, s.shape, 0) + qi * tq
    col = lax.broadcasted_iota(jnp.int32, s.shape, 1) + kj * tk
    return jnp.where(col <= row, s, NEG)


def _mla_fwd(q, k, v, H, name):
    T = q.shape[0]
    tq = _pick(T, 512)
    nq = T // tq

    def kern(q_ref, k_ref, v_ref, o_ref, lse_ref, m_sc, l_sc, acc_sc):
        qi, kj = pl.program_id(1), pl.program_id(2)

        @pl.when(kj == 0)
        def _():
            m_sc[...] = jnp.full_like(m_sc, NEG)
            l_sc[...] = jnp.zeros_like(l_sc)
            acc_sc[...] = jnp.zeros_like(acc_sc)

        @pl.when(kj <= qi)
        def _():
            s = lax.dot_general(q_ref[...], k_ref[...], NT, preferred_element_type=F32)
            s = _causal_mask(s, qi, kj, tq, tq)
            m_prev = m_sc[...]
            m_new = jnp.maximum(m_prev, jnp.max(s, axis=1, keepdims=True))
            alpha = jnp.exp(m_prev - m_new)
            p = jnp.exp(s - m_new)
            l_sc[...] = alpha * l_sc[...] + jnp.sum(p, axis=1, keepdims=True)
            acc_sc[...] = alpha * acc_sc[...] + jnp.dot(p.astype(BF16), v_ref[...], preferred_element_type=F32)
            m_sc[...] = m_new

        @pl.when(kj == qi)
        def _():
            l = l_sc[...]
            o_ref[...] = (acc_sc[...] / l).astype(o_ref.dtype)
            lse_ref[...] = jnp.broadcast_to(m_sc[...] + jnp.log(l), lse_ref.shape)

    qspec = lambda w: pl.BlockSpec((tq, w), lambda h, i, j: (i, h))
    kspec = lambda w: pl.BlockSpec((tq, w), lambda h, i, j: (jnp.minimum(j, i), h))
    return pl.pallas_call(
        kern, name=name, grid=(H, nq, nq), in_specs=[qspec(MLA_QK_PAD), kspec(MLA_QK_PAD), kspec(MLA_V)],
        out_specs=[qspec(MLA_V), qspec(LANES)],
        out_shape=[_sds((T, H * MLA_V), BF16), _sds((T, H * LANES), F32)],
        scratch_shapes=[pltpu.VMEM((tq, 1), F32), pltpu.VMEM((tq, 1), F32), pltpu.VMEM((tq, MLA_V), F32)],
        compiler_params=_params(3))(q, k, v)


def _mla_bwd_dq(q, k, v, do, o, lse, H, name):
    T = q.shape[0]
    tq = _pick(T, 512)
    nq = T // tq

    def kern(q_ref, k_ref, v_ref, do_ref, o_ref, lse_ref, dq_ref, acc_sc, dl_sc):
        qi, kj = pl.program_id(1), pl.program_id(2)

        @pl.when(kj == 0)
        def _():
            acc_sc[...] = jnp.zeros_like(acc_sc)
            dl_sc[...] = jnp.sum(do_ref[...].astype(F32) * o_ref[...].astype(F32), axis=1, keepdims=True)

        @pl.when(kj <= qi)
        def _():
            s = lax.dot_general(q_ref[...], k_ref[...], NT, preferred_element_type=F32)
            p = jnp.exp(_causal_mask(s, qi, kj, tq, tq) - lse_ref[:, :1])
            dp = lax.dot_general(do_ref[...], v_ref[...], NT, preferred_element_type=F32)
            ds = p * (dp - dl_sc[...])
            acc_sc[...] += jnp.dot(ds.astype(BF16), k_ref[...], preferred_element_type=F32)

        @pl.when(kj == qi)
        def _():
            dq_ref[...] = acc_sc[...]

    qspec = lambda w: pl.BlockSpec((tq, w), lambda h, i, j: (i, h))
    kspec = lambda w: pl.BlockSpec((tq, w), lambda h, i, j: (jnp.minimum(j, i), h))
    return pl.pallas_call(
        kern, name=name, grid=(H, nq, nq),
        in_specs=[qspec(MLA_QK_PAD), kspec(MLA_QK_PAD), kspec(MLA_V), qspec(MLA_V), qspec(MLA_V), qspec(LANES)],
        out_specs=qspec(MLA_QK_PAD), out_shape=_sds((T, H * MLA_QK_PAD), F32),
        scratch_shapes=[pltpu.VMEM((tq, MLA_QK_PAD), F32), pltpu.VMEM((tq, 1), F32)],
        compiler_params=_params(3))(q, k, v, do, o, lse)


def _mla_bwd_dkv(q, k, v, do, o, lse, H, name):
    T = q.shape[0]
    tq = _pick(T, 512)
    nq = T // tq

    def kern(q_ref, k_ref, v_ref, do_ref, o_ref, lse_ref, dk_ref, dv_ref, dk_sc, dv_sc):
        kj, qi = pl.program_id(1), pl.program_id(2)

        @pl.when(qi == 0)
        def _():
            dk_sc[...] = jnp.zeros_like(dk_sc)
            dv_sc[...] = jnp.zeros_like(dv_sc)

        @pl.when(qi >= kj)
        def _():
            s = lax.dot_general(q_ref[...], k_ref[...], NT, preferred_element_type=F32)
            p = jnp.exp(_causal_mask(s, qi, kj, tq, tq) - lse_ref[:, :1])
            dv_sc[...] += lax.dot_general(p.astype(BF16), do_ref[...], TN, preferred_element_type=F32)
            dl = jnp.sum(do_ref[...].astype(F32) * o_ref[...].astype(F32), axis=1, keepdims=True)
            dp = lax.dot_general(do_ref[...], v_ref[...], NT, preferred_element_type=F32)
            ds = p * (dp - dl)
            dk_sc[...] += lax.dot_general(ds.astype(BF16), q_ref[...], TN, preferred_element_type=F32)

        @pl.when(qi == nq - 1)
        def _():
            dk_ref[...] = dk_sc[...]
            dv_ref[...] = dv_sc[...]

    qspec = lambda w: pl.BlockSpec((tq, w), lambda h, j, i: (jnp.maximum(i, j), h))
    kspec = lambda w: pl.BlockSpec((tq, w), lambda h, j, i: (j, h))
    return pl.pallas_call(
        kern, name=name, grid=(H, nq, nq),
        in_specs=[qspec(MLA_QK_PAD), kspec(MLA_QK_PAD), kspec(MLA_V), qspec(MLA_V), qspec(MLA_V), qspec(LANES)],
        out_specs=[kspec(MLA_QK_PAD), kspec(MLA_V)],
        out_shape=[_sds((T, H * MLA_QK_PAD), F32), _sds((T, H * MLA_V), F32)],
        scratch_shapes=[pltpu.VMEM((tq, MLA_QK_PAD), F32), pltpu.VMEM((tq, MLA_V), F32)],
        compiler_params=_params(3))(q, k, v, do, o, lse)


def _dil_views(qd, kd, pd, grp, dil, W):
    T = qd.shape[0]
    L = T // dil
    G = len(DIL_GROUPS)
    prev = lambda b: jnp.maximum(b - 1, 0)
    blk = lambda f: pl.BlockSpec((DIL_BLOCK, W), f)
    arrays = (qd.reshape(L, dil * G * W), kd.reshape(L, dil * G * W), pd.reshape(L, dil * 3 * G * W))
    q_spec = blk(lambda r, b: (b, r * G + grp))
    kc_spec = q_spec
    kp_spec = blk(lambda r, b: (prev(b), r * G + grp))
    vc_spec = blk(lambda r, b: (b, r * 3 * G + 3 * grp + 2))
    vp_spec = blk(lambda r, b: (prev(b), r * 3 * G + 3 * grp + 2))
    return arrays, (q_spec, kc_spec, kp_spec, vc_spec, vp_spec)


def _band_masks(b, nb_next=None):
    row = lax.broadcasted_iota(jnp.int32, (DIL_BLOCK, DIL_BLOCK), 0)
    col = lax.broadcasted_iota(jnp.int32, (DIL_BLOCK, DIL_BLOCK), 1)
    return row, col


def _dil_fwd(qd, kd, pd, grp, dil, HD, name):
    T = qd.shape[0]
    W = HD * DIL_HEAD
    L = T // dil
    nb = L // DIL_BLOCK
    (qv, kv_, vv), (q_spec, kc_spec, kp_spec, vc_spec, vp_spec) = _dil_views(qd, kd, pd, grp, dil, W)

    def kern(q_ref, kc_ref, kp_ref, vc_ref, vp_ref, o_ref, lse_ref):
        b = pl.program_id(1)
        row, col = _band_masks(b)
        off = jnp.where(b > 0, 0, 2 * DIL_BLOCK)
        ok_prev = col >= row + off
        ok_cur = col <= row
        for h in range(HD):
            sl = slice(h * DIL_HEAD, (h + 1) * DIL_HEAD)
            qh = q_ref[:, sl]
            sa = jnp.where(ok_prev, lax.dot_general(qh, kp_ref[:, sl], NT, preferred_element_type=F32), NEG)
            sb = jnp.where(ok_cur, lax.dot_general(qh, kc_ref[:, sl], NT, preferred_element_type=F32), NEG)
            m = jnp.maximum(jnp.max(sa, axis=1, keepdims=True), jnp.max(sb, axis=1, keepdims=True))
            ea, eb = jnp.exp(sa - m), jnp.exp(sb - m)
            l = jnp.sum(ea, axis=1, keepdims=True) + jnp.sum(eb, axis=1, keepdims=True)
            acc = (jnp.dot(ea.astype(BF16), vp_ref[:, sl], preferred_element_type=F32)
                   + jnp.dot(eb.astype(BF16), vc_ref[:, sl], preferred_element_type=F32))
            o_ref[:, sl] = acc / l
            lse_ref[:, sl] = jnp.broadcast_to(m + jnp.log(l), (DIL_BLOCK, DIL_HEAD))

    o_spec = pl.BlockSpec((DIL_BLOCK, W), lambda r, b: (b, r))
    o, lse = pl.pallas_call(
        kern, name=name, grid=(dil, nb), in_specs=[q_spec, kc_spec, kp_spec, vc_spec, vp_spec],
        out_specs=[o_spec, o_spec], out_shape=[_sds((L, dil * W), F32)] * 2,
        compiler_params=_params(2))(qv, kv_, kv_, vv, vv)
    return o.reshape(T, W), lse.reshape(T, W)


def _dil_bwd_dq(qd, kd, pd, do, o, lse, grp, dil, HD, name):
    T = qd.shape[0]
    W = HD * DIL_HEAD
    L = T // dil
    nb = L // DIL_BLOCK
    (qv, kv_, vv), (q_spec, kc_spec, kp_spec, vc_spec, vp_spec) = _dil_views(qd, kd, pd, grp, dil, W)
    t_spec = pl.BlockSpec((DIL_BLOCK, W), lambda r, b: (b, r))
    view = lambda a: a.reshape(L, dil * W)

    def kern(q_ref, kc_ref, kp_ref, vc_ref, vp_ref, do_ref, o_ref, lse_ref, dq_ref):
        b = pl.program_id(1)
        row, col = _band_masks(b)
        off = jnp.where(b > 0, 0, 2 * DIL_BLOCK)
        ok_prev = col >= row + off
        ok_cur = col <= row
        for h in range(HD):
            sl = slice(h * DIL_HEAD, (h + 1) * DIL_HEAD)
            qh, doh = q_ref[:, sl], do_ref[:, sl]
            lse_h = lse_ref[:, sl]
            dl = jnp.sum(doh.astype(F32) * o_ref[:, sl].astype(F32), axis=1, keepdims=True)
            sa = jnp.where(ok_prev, lax.dot_general(qh, kp_ref[:, sl], NT, preferred_element_type=F32), NEG)
            sb = jnp.where(ok_cur, lax.dot_general(qh, kc_ref[:, sl], NT, preferred_element_type=F32), NEG)
            pa, pb = jnp.exp(sa - lse_h), jnp.exp(sb - lse_h)
            dsa = pa * (lax.dot_general(doh, vp_ref[:, sl], NT, preferred_element_type=F32) - dl)
            dsb = pb * (lax.dot_general(doh, vc_ref[:, sl], NT, preferred_element_type=F32) - dl)
            dq_ref[:, sl] = (jnp.dot(dsa.astype(BF16), kp_ref[:, sl], preferred_element_type=F32)
                             + jnp.dot(dsb.astype(BF16), kc_ref[:, sl], preferred_element_type=F32))

    dq = pl.pallas_call(
        kern, name=name, grid=(dil, nb),
        in_specs=[q_spec, kc_spec, kp_spec, vc_spec, vp_spec, t_spec, t_spec, t_spec],
        out_specs=t_spec, out_shape=_sds((L, dil * W), F32),
        compiler_params=_params(2))(qv, kv_, kv_, vv, vv, view(do), view(o), view(lse))
    return dq.reshape(T, W)


def _dil_bwd_dkv(qd, kd, pd, do, o, lse, grp, dil, HD, name):
    T = qd.shape[0]
    W = HD * DIL_HEAD
    L = T // dil
    nb = L // DIL_BLOCK
    G = len(DIL_GROUPS)
    nxt = lambda b: jnp.minimum(b + 1, nb - 1)
    blk = lambda f: pl.BlockSpec((DIL_BLOCK, W), f)
    qc_spec = blk(lambda r, b: (b, r * G + grp))
    qn_spec = blk(lambda r, b: (nxt(b), r * G + grp))
    v_spec = blk(lambda r, b: (b, r * 3 * G + 3 * grp + 2))
    tc_spec = blk(lambda r, b: (b, r))
    tn_spec = blk(lambda r, b: (nxt(b), r))
    view = lambda a: a.reshape(L, dil * W)
    qv, kv_, vv = qd.reshape(L, dil * G * W), kd.reshape(L, dil * G * W), pd.reshape(L, dil * 3 * G * W)

    def kern(qc_ref, qn_ref, k_ref, v_ref, doc_ref, don_ref, oc_ref, on_ref, lc_ref, ln_ref, dk_ref, dv_ref):
        b = pl.program_id(1)
        row, col = _band_masks(b)
        off = jnp.where(b + 1 < nb, 0, 2 * DIL_BLOCK)
        ok_next = col >= row + off
        ok_cur = col <= row
        for h in range(HD):
            sl = slice(h * DIL_HEAD, (h + 1) * DIL_HEAD)
            kh, vh = k_ref[:, sl], v_ref[:, sl]
            dk = jnp.zeros((DIL_BLOCK, DIL_HEAD), F32)
            dv = jnp.zeros((DIL_BLOCK, DIL_HEAD), F32)
            for q_ref, do_ref, o_ref, l_ref, ok in ((qc_ref, doc_ref, oc_ref, lc_ref, ok_cur),
                                                    (qn_ref, don_ref, on_ref, ln_ref, ok_next)):
                qh, doh = q_ref[:, sl], do_ref[:, sl]
                dl = jnp.sum(doh.astype(F32) * o_ref[:, sl].astype(F32), axis=1, keepdims=True)
                s = jnp.where(ok, lax.dot_general(qh, kh, NT, preferred_element_type=F32), NEG)
                p = jnp.exp(s - l_ref[:, sl])
                ds = p * (lax.dot_general(doh, vh, NT, preferred_element_type=F32) - dl)
                dv = dv + lax.dot_general(p.astype(BF16), doh, TN, preferred_element_type=F32)
                dk = dk + lax.dot_general(ds.astype(BF16), qh, TN, preferred_element_type=F32)
            dk_ref[:, sl] = dk
            dv_ref[:, sl] = dv

    dk, dv = pl.pallas_call(
        kern, name=name, grid=(dil, nb),
        in_specs=[qc_spec, qn_spec, qc_spec, v_spec, tc_spec, tn_spec, tc_spec, tn_spec, tc_spec, tn_spec],
        out_specs=[tc_spec, tc_spec], out_shape=[_sds((L, dil * W), F32)] * 2,
        compiler_params=_params(2))(qv, qv, kv_, vv, view(do), view(do), view(o), view(o), view(lse), view(lse))
    return dk.reshape(T, W), dv.reshape(T, W)


ANY = pl.BlockSpec(memory_space=pl.ANY)


def _place():
    return lax.axis_index("x"), lax.axis_index("y"), lax.axis_index("c")


def _other_chips(x, y):
    return [(1 - x, y), (x, 1 - y), (1 - x, 1 - y)]


def _all_gather_shards(pack):
    R, C = pack.shape
    Hh = R // 2

    def body(x_ref, out_ref, send_sems, recv_sems, local_sem):
        x, y, c = _place()
        me = 2 * x + y
        sibling = (x, y, 1 - c)
        chips = _other_chips(x, y)

        def half(chip, h):
            return out_ref.at[chip, pl.ds(h * Hh, Hh), :]

        def copy(k, src, dst, to):
            return pltpu.make_async_remote_copy(src_ref=src, dst_ref=dst, send_sem=send_sems.at[k],
                                                recv_sem=recv_sems.at[k], device_id=to, device_id_type=MESH_ID)

        mine = pltpu.make_async_copy(x_ref, out_ref.at[me], local_sem)
        mine.start()
        sent = [copy(j, x_ref.at[pl.ds(c * Hh, Hh), :], half(me, c), (px, py, c)) for j, (px, py) in enumerate(chips)]
        for cp in sent:
            cp.start()
        passed = []
        for j, (px, py) in enumerate(chips):
            landed = half(2 * px + py, c)
            copy(j, landed, landed, (px, py, c)).wait_recv()
            fw = copy(3 + j, landed, landed, sibling)
            fw.start()
            passed.append(fw)
        for j, (px, py) in enumerate(chips):
            other = half(2 * px + py, 1 - c)
            copy(3 + j, other, other, sibling).wait_recv()
        for cp in sent + passed:
            cp.wait_send()
        mine.wait()

    return pl.pallas_call(
        body, name="ag_weights", out_shape=_sds((4, R, C), pack.dtype), in_specs=[ANY], out_specs=ANY,
        scratch_shapes=[pltpu.SemaphoreType.DMA((6,)), pltpu.SemaphoreType.DMA((6,)), pltpu.SemaphoreType.DMA],
    )(pack)


def _send_other_half(g):
    n, _, Hh, C = g.shape

    def body(g_ref, out_ref, send_sem, recv_sem):
        x, y, c = _place()
        cp = pltpu.make_async_remote_copy(src_ref=g_ref.at[:, 1 - c], dst_ref=out_ref, send_sem=send_sem,
                                          recv_sem=recv_sem, device_id=(x, y, 1 - c), device_id_type=MESH_ID)
        cp.start()
        cp.wait()

    return pl.pallas_call(
        body, name="rs_sibling", out_shape=_sds((n, Hh, C), g.dtype), in_specs=[ANY], out_specs=ANY,
        scratch_shapes=[pltpu.SemaphoreType.DMA, pltpu.SemaphoreType.DMA])(g)


def _pair_sum(g, got, c_idx):
    n, _, Hh, C = g.shape
    tr = _pick(Hh, 512, 16)

    def kern(c_ref, a_ref, b_ref, o_ref):
        o_ref[...] = (a_ref[...] + b_ref[...]).astype(o_ref.dtype)

    grid_spec = pltpu.PrefetchScalarGridSpec(
        num_scalar_prefetch=1, grid=(n, Hh // tr),
        in_specs=[pl.BlockSpec((None, None, tr, C), lambda j, i, c_ref: (j, c_ref[0], i, 0)),
                  pl.BlockSpec((None, tr, C), lambda j, i, c_ref: (j, i, 0))],
        out_specs=pl.BlockSpec((None, tr, C), lambda j, i, c_ref: (j, i, 0)))
    return pl.pallas_call(kern, name="rs_pair_sum", grid_spec=grid_spec, out_shape=_sds((n, Hh, C), BF16),
                          compiler_params=_params(2))(c_idx, g, got)


def _exchange_chips(part):
    n, Hh, C = part.shape

    def body(p_ref, out_ref, send_sems, recv_sems, local_sem):
        x, y, c = _place()
        me = 2 * x + y
        chips = _other_chips(x, y)
        mine = pltpu.make_async_copy(p_ref.at[me], out_ref.at[me], local_sem)
        mine.start()
        sent = []
        for j, (px, py) in enumerate(chips):
            cp = pltpu.make_async_remote_copy(src_ref=p_ref.at[2 * px + py], dst_ref=out_ref.at[me],
                                              send_sem=send_sems.at[j], recv_sem=recv_sems.at[j],
                                              device_id=(px, py, c), device_id_type=MESH_ID)
            cp.start()
            sent.append(cp)
        for j, (px, py) in enumerate(chips):
            slot = out_ref.at[2 * px + py]
            pltpu.make_async_remote_copy(src_ref=slot, dst_ref=slot, send_sem=send_sems.at[j], recv_sem=recv_sems.at[j],
                                         device_id=(px, py, c), device_id_type=MESH_ID).wait_recv()
        for cp in sent:
            cp.wait_send()
        mine.wait()

    return pl.pallas_call(
        body, name="rs_chips", out_shape=_sds((n, Hh, C), part.dtype), in_specs=[ANY], out_specs=ANY,
        scratch_shapes=[pltpu.SemaphoreType.DMA((3,)), pltpu.SemaphoreType.DMA((3,)), pltpu.SemaphoreType.DMA])(part)


def _sum_slots(parts):
    n, Hh, C = parts.shape
    tr = _pick(Hh, 512, 16)

    def kern(p_ref, o_ref):
        acc = p_ref[0].astype(F32)
        for j in range(1, n):
            acc = acc + p_ref[j].astype(F32)
        o_ref[...] = acc

    return pl.pallas_call(
        kern, name="rs_sum", grid=(Hh // tr,), in_specs=[pl.BlockSpec((n, tr, C), lambda i: (0, i, 0))],
        out_specs=pl.BlockSpec((tr, C), lambda i: (i, 0)), out_shape=_sds((Hh, C), F32),
        compiler_params=_params(1))(parts)


def _join_halves(mine):
    Hh, C = mine.shape

    def body(m_ref, out_ref, send_sem, recv_sem, local_sem):
        x, y, c = _place()
        keep = pltpu.make_async_copy(m_ref, out_ref.at[c], local_sem)
        keep.start()
        cp = pltpu.make_async_remote_copy(src_ref=m_ref, dst_ref=out_ref.at[c], send_sem=send_sem, recv_sem=recv_sem,
                                          device_id=(x, y, 1 - c), device_id_type=MESH_ID)
        cp.start()
        other = out_ref.at[1 - c]
        pltpu.make_async_remote_copy(src_ref=other, dst_ref=other, send_sem=send_sem, recv_sem=recv_sem,
                                     device_id=(x, y, 1 - c), device_id_type=MESH_ID).wait_recv()
        cp.wait_send()
        keep.wait()

    return pl.pallas_call(
        body, name="rs_join", out_shape=_sds((2, Hh, C), mine.dtype), in_specs=[ANY], out_specs=ANY,
        scratch_shapes=[pltpu.SemaphoreType.DMA, pltpu.SemaphoreType.DMA, pltpu.SemaphoreType.DMA])(mine)


def _all_reduce_small(vec):
    N = vec.shape[1]
    n_dev = 8

    def body(v_ref, out_ref, slots, send_sems, recv_sems):
        x, y, c = _place()
        me = 4 * x + 2 * y + c
        slots[me] = v_ref[...]
        sent = []
        for k in range(1, n_dev):
            px, py, pc = x ^ (k >> 2), y ^ ((k >> 1) & 1), c ^ (k & 1)
            cp = pltpu.make_async_remote_copy(src_ref=v_ref, dst_ref=slots.at[me], send_sem=send_sems.at[k - 1],
                                              recv_sem=recv_sems.at[k - 1], device_id=(px, py, pc),
                                              device_id_type=MESH_ID)
            cp.start()
            sent.append(cp)
        for k in range(1, n_dev):
            px, py, pc = x ^ (k >> 2), y ^ ((k >> 1) & 1), c ^ (k & 1)
            slot = slots.at[4 * px + 2 * py + pc]
            pltpu.make_async_remote_copy(src_ref=slot, dst_ref=slot, send_sem=send_sems.at[k - 1],
                                         recv_sem=recv_sems.at[k - 1], device_id=(px, py, pc),
                                         device_id_type=MESH_ID).wait_recv()
        for cp in sent:
            cp.wait_send()
        acc = slots[0]
        for j in range(1, n_dev):
            acc = acc + slots[j]
        out_ref[...] = acc

    vm = pl.BlockSpec(memory_space=pltpu.VMEM)
    return pl.pallas_call(
        body, name="ar_gains", out_shape=_sds((1, N), F32), in_specs=[vm], out_specs=vm,
        scratch_shapes=[pltpu.VMEM((n_dev, 1, N), F32), pltpu.SemaphoreType.DMA((n_dev - 1,)),
                        pltpu.SemaphoreType.DMA((n_dev - 1,))])(vec)


def _adamw(w, g, m, v, name):
    R, C = w.shape
    tr = _pick(R, 512, 8)

    def body(wv, gv, mv, vv):
        m2 = ADAM_B1 * mv + (1.0 - ADAM_B1) * gv
        v2 = ADAM_B2 * vv + (1.0 - ADAM_B2) * (gv * gv)
        m_hat = m2 / (1.0 - ADAM_B1 ** ADAM_STEP)
        v_hat = v2 / (1.0 - ADAM_B2 ** ADAM_STEP)
        return -ADAM_LR * (m_hat / (jnp.sqrt(v_hat) + ADAM_EPS) + ADAM_WD * wv), m2, v2

    return _vcall(body, (R // tr,), (w, g, m, v), [_rows(tr, C)] * 4, [_sds((R, C), F32)] * 3, [_rows(tr, C)] * 3, name)


def _pad_to(a, n, axis):
    extra = n - a.shape[axis]
    if extra == 0:
        return a
    pads = [(0, 0)] * a.ndim
    pads[axis] = (0, extra)
    return jnp.pad(a, pads)


def _round_up(n, m):
    return -(-n // m) * m


def _pack_rows(n_elems):
    return _round_up(-(-n_elems // PACK_COLS), 1024)


def _pack_shards(shards, dtype):
    flat = jnp.concatenate([shards[n].astype(dtype).reshape(-1) for n in BIG])
    R = _pack_rows(flat.shape[0])
    return _pad_to(flat, R * PACK_COLS, 0).reshape(R, PACK_COLS)


def _unpack_shards(buf, shapes):
    flat = buf.reshape(-1)
    out, off = {}, 0
    for n in BIG:
        r, c = shapes[n]
        out[n] = flat[off:off + r * c].reshape(r, c)
        off += r * c
    return out


def _unpack_full(gathered, shapes):
    flat = gathered.reshape(4, -1)
    out, off = {}, 0
    for n in BIG:
        r, c = shapes[n]
        seg = flat[:, off:off + r * c].reshape(4, r, c)
        off += r * c
        out[n] = seg.reshape(4 * r, c) if n in ROW_SHARDED else seg.transpose(1, 0, 2).reshape(r, 4 * c)
    return out


def _pack_full(full, shapes):
    segs = []
    for n in BIG:
        r, c = shapes[n]
        a = full[n]
        segs.append(a.reshape(4, r * c) if n in ROW_SHARDED else a.reshape(r, 4, c).transpose(1, 0, 2).reshape(4, r * c))
    flat = jnp.concatenate(segs, axis=1)
    R = _pack_rows(flat.shape[1])
    return _pad_to(flat, R * PACK_COLS, 1).reshape(4, R, PACK_COLS)


def _pack_small(vals):
    return jnp.concatenate([_pad_to(vals[n].reshape(1, -1), _round_up(vals[n].size, LANES), 1) for n in SMALL], axis=1)


def _unpack_small(vec, shapes):
    out, off = {}, 0
    for n in SMALL:
        size = int(np.prod(shapes[n]))
        out[n] = vec[:, off:off + size].reshape(shapes[n])
        off += _round_up(size, LANES)
    return out


def _interleave(a, b, tf):
    r, f = a.shape
    return jnp.stack([a.reshape(r, f // tf, tf), b.reshape(r, f // tf, tf)], axis=2).reshape(r, 2 * f)


def _deinterleave(ab, tf):
    r, f2 = ab.shape
    t = ab.reshape(r, f2 // (2 * tf), 2, tf)
    return t[:, :, 0].reshape(r, f2 // 2), t[:, :, 1].reshape(r, f2 // 2)


def _ffn_fwd(x, g, w_gu, w_d, tf, tag):
    n = _rms_fwd(x, g, f"{tag}_norm")
    ab = _mm(n, w_gu, "nn", BF16, f"{tag}_gate_up", tn=2 * tf)
    act = _silu_mul(ab, tf, f"{tag}_act")
    out = _mm(act, w_d, "nn", F32, f"{tag}_down", res=x, alpha=0.5)
    return out, (n, ab, act)


def _ffn_bwd(dout, x, g, w_gu, w_d, saved, tf, tag):
    n, ab, act = saved
    dout_b = dout.astype(BF16)
    d_wd = _mm(act, dout_b, "tn", F32, f"{tag}_d_wdown", alpha=0.5)
    dact = _mm(dout_b, w_d, "nt", BF16, f"{tag}_d_act", alpha=0.5)
    dab = _swiglu_bwd(dact, ab, tf, f"{tag}_d_gate_up")
    d_wgu = _mm(n, dab, "tn", F32, f"{tag}_d_wgu", tn=2 * tf)
    dn = _mm(dab, w_gu, "nt", F32, f"{tag}_d_norm")
    dx, dg = _rms_bwd_call(x, g, dn, dout, f"{tag}_d_x")
    return dx, dg, d_wgu, d_wd


def _local_step(x, p, pos_b, target, W, Gn, dims):
    T, D = x.shape
    H, HD, QL, KVL, LP, tf = dims["H"], dims["HD"], dims["QL"], dims["KVL"], dims["LP"], dims["tf"]
    Wd = HD * DIL_HEAD
    scale_mla, scale_dil = MLA_QK ** -0.5, DIL_HEAD ** -0.5
    kr_col = (QL + KVL) // LANES
    tab_mla = tuple(_rope_tables(pos_b, MLA_ROPE, "rope_tab_mla"))
    tab_dil = tuple(_rope_tables(pos_b, DIL_ROT, "rope_tab_dil"))

    x1, ffn1 = _ffn_fwd(x, Gn["g_ffn1"], W["w1_gu"], W["w1_down"], tf, "ffn1")
    h = _rms_fwd(x1, Gn["g_mix"], "mix_norm")
    lat = _mm(h, W["w_lat"], "nn", F32, "proj_lat")
    pd = _mm(h, W["w_dil"], "nn", BF16, "proj_dil")
    pg = _mm(h, W["w_gin"], "nn", BF16, "proj_gate")

    cq, ckv = _lat_fwd(lat, Gn["g_cq"], Gn["g_ckv"], "lat_norm")
    q_raw = _mm(cq, W["w_uq"], "nn", F32, "mla_q_up")
    kv = _mm(ckv, W["w_ukv"], "nn", F32, "mla_kv_up")
    q = _mla_q_prep(q_raw, Gn["g_q_mla"], tab_mla, H, scale_mla, "mla_q_prep")
    k, v = _mla_k_prep(kv, lat, kr_col, Gn["g_k_mla"], tab_mla, H, "mla_k_prep")
    o_mla, lse_mla = _mla_fwd(q, k, v, H, "mla_attn")

    qd, kd = _dil_prep(pd, Gn["g_q_dil"], Gn["g_k_dil"], tab_dil, HD, scale_dil, "dil_prep")
    og, lg = [], []
    for grp, (win, dil) in enumerate(DIL_GROUPS):
        o_, l_ = _dil_fwd(qd, kd, pd, grp, dil, HD, f"dil_attn{grp}")
        og.append(o_)
        lg.append(l_)
    o_dil, lse_dil = _dil_merge(og, lg, "dil_merge")

    bm = _mm(o_mla, W["w_br_mla"], "nn", F32, "branch_mla")
    bd = _mm(o_dil, W["w_br_dil"], "nn", F32, "branch_dil")
    merged = _gate_merge(pg, bm, bd, "gate_merge")
    x2 = _mm(merged, W["w_o"], "nn", F32, "out_proj", res=x1)

    x3, ffn2 = _ffn_fwd(x2, Gn["g_ffn2"], W["w2_gu"], W["w2_down"], tf, "ffn2")
    n4 = _rms_fwd(x3, Gn["g_ple"], "ple_norm")
    zg = _mm(n4, W["w_ple_gate"], "nn", F32, "ple_gate")
    p_b = p.astype(BF16)
    pp = _mm(p_b, W["w_ple_proj"], "nn", F32, "ple_proj")
    dy, dpp, dzg, loss = _ple_loss(x3, zg, pp, target, "ple_loss")

    gw, gg = {}, {}
    gw["w_ple_proj"] = _mm(p_b, dpp, "tn", F32, "d_w_ple_proj")
    gw["w_ple_gate"] = _mm(n4, dzg, "tn", F32, "d_w_ple_gate")
    dn4 = _mm(dzg, W["w_ple_gate"], "nt", F32, "d_ple_norm")
    dx3, gg["g_ple"] = _rms_bwd_call(x3, Gn["g_ple"], dn4, dy, "d_x3")

    dx2, gg["g_ffn2"], gw["w2_gu"], gw["w2_down"] = _ffn_bwd(dx3, x2, Gn["g_ffn2"], W["w2_gu"], W["w2_down"], ffn2, tf, "ffn2")

    dx2_b = dx2.astype(BF16)
    gw["w_o"] = _mm(merged, dx2_b, "tn", F32, "d_w_o")
    dmerged = _mm(dx2_b, W["w_o"], "nt", F32, "d_merged")
    dbm, dbd, dpg = _gate_bwd(dmerged, pg, bm, bd, "d_gate")
    gw["w_br_mla"] = _mm(o_mla, dbm, "tn", F32, "d_w_br_mla")
    gw["w_br_dil"] = _mm(o_dil, dbd, "tn", F32, "d_w_br_dil")
    do_mla = _mm(dbm, W["w_br_mla"], "nt", BF16, "d_o_mla")
    do_dil = _mm(dbd, W["w_br_dil"], "nt", BF16, "d_o_dil")

    dh = _mm(dpg, W["w_gin"], "nt", F32, "d_h_gate")
    gw["w_gin"] = _mm(h, dpg, "tn", F32, "d_w_gin")
    gq_d, gk_d = Gn["g_q_dil"], Gn["g_k_dil"]
    d_wdil, dgq_d, dgk_d = [], [], []
    for grp, (win, dil) in enumerate(DIL_GROUPS):
        dq_ = _dil_bwd_dq(qd, kd, pd, do_dil, o_dil, lse_dil, grp, dil, HD, f"dil_dq{grp}")
        dk_, dv_ = _dil_bwd_dkv(qd, kd, pd, do_dil, o_dil, lse_dil, grp, dil, HD, f"dil_dkv{grp}")
        dpd_g, dgq_, dgk_ = _dil_prep_bwd(dq_, dk_, dv_, pd, grp, gq_d, gk_d, tab_dil, HD, scale_dil, f"d_dil_prep{grp}")
        dgq_d.append(dgq_)
        dgk_d.append(dgk_)
        w_g = W["w_dil"][:, grp * 3 * Wd:(grp + 1) * 3 * Wd]
        dh = _mm(dpd_g, w_g, "nt", F32, f"d_h_dil{grp}", res=dh)
        d_wdil.append(_mm(h, dpd_g, "tn", F32, f"d_w_dil{grp}"))
    gw["w_dil"] = jnp.concatenate(d_wdil, axis=1)
    gg["g_q_dil"] = jnp.concatenate(dgq_d, axis=0)
    gg["g_k_dil"] = jnp.concatenate(dgk_d, axis=0)

    dq = _mla_bwd_dq(q, k, v, do_mla, o_mla, lse_mla, H, "mla_dq")
    dk, dv = _mla_bwd_dkv(q, k, v, do_mla, o_mla, lse_mla, H, "mla_dkv")
    dq_raw, gg["g_q_mla"] = _mla_q_bwd(dq, q_raw, Gn["g_q_mla"], tab_mla, H, scale_mla, "d_mla_q_prep")
    dkv, dkr, gg["g_k_mla"] = _mla_k_bwd(dk, dv, kv, lat, kr_col, Gn["g_k_mla"], tab_mla, H, "d_mla_k_prep")
    gw["w_uq"] = _mm(cq, dq_raw, "tn", F32, "d_w_uq")
    gw["w_ukv"] = _mm(ckv, dkv, "tn", F32, "d_w_ukv")
    dcq = _mm(dq_raw, W["w_uq"], "nt", F32, "d_cq")
    dckv = _mm(dkv, W["w_ukv"], "nt", F32, "d_ckv")
    dlat, gg["g_cq"], gg["g_ckv"] = _lat_bwd(dcq, dckv, dkr, lat, Gn["g_cq"], Gn["g_ckv"], "d_lat_norm")
    dh = _mm(dlat, W["w_lat"], "nt", F32, "d_h_lat", res=dh)
    gw["w_lat"] = _mm(h, dlat, "tn", F32, "d_w_lat")

    dx1, gg["g_mix"] = _rms_bwd_call(x1, Gn["g_mix"], dh, dx2, "d_x1")
    dx, gg["g_ffn1"], gw["w1_gu"], gw["w1_down"] = _ffn_bwd(dx1, x, Gn["g_ffn1"], W["w1_gu"], W["w1_down"], ffn1, tf, "ffn1")
    return loss, dx, gw, gg


def _compute_layout(full, dims):
    H, QL, KVL, LP, tf, Wd = dims["H"], dims["QL"], dims["KVL"], dims["LP"], dims["tf"], dims["HD"] * DIL_HEAD
    off_dil = QL + KVL + MLA_ROPE
    off_gate = off_dil + 3 * len(DIL_GROUPS) * Wd
    w_in = full["w_in"]
    w_uq = full["w_uq"]
    return {
        "w1_gu": _interleave(full["w1_gate"], full["w1_up"], tf), "w1_down": full["w1_down"],
        "w2_gu": _interleave(full["w2_gate"], full["w2_up"], tf), "w2_down": full["w2_down"],
        "w_lat": _pad_to(w_in[:, :off_dil], LP, 1), "w_dil": w_in[:, off_dil:off_gate], "w_gin": w_in[:, off_gate:],
        "w_uq": _pad_to(w_uq.reshape(QL, H, MLA_QK), MLA_QK_PAD, 2).reshape(QL, H * MLA_QK_PAD),
        "w_ukv": full["w_ukv"], "w_br_mla": full["w_br_mla"], "w_br_dil": full["w_br_dil"], "w_o": full["w_o"],
        "w_ple_gate": full["w_ple_gate"], "w_ple_proj": full["w_ple_proj"],
    }


def _natural_grads(gw, dims):
    H, QL, KVL, tf = dims["H"], dims["QL"], dims["KVL"], dims["tf"]
    off_dil = QL + KVL + MLA_ROPE
    out = {n: gw[n] for n in ("w1_down", "w2_down", "w_ukv", "w_br_mla", "w_br_dil", "w_o", "w_ple_gate", "w_ple_proj")}
    out["w1_gate"], out["w1_up"] = _deinterleave(gw["w1_gu"], tf)
    out["w2_gate"], out["w2_up"] = _deinterleave(gw["w2_gu"], tf)
    out["w_in"] = jnp.concatenate([gw["w_lat"][:, :off_dil], gw["w_dil"], gw["w_gin"]], axis=1)
    out["w_uq"] = gw["w_uq"].reshape(QL, H, MLA_QK_PAD)[:, :, :MLA_QK].reshape(QL, H * MLA_QK)
    return out


def _step(x, p, positions, loss_target, w, m, v):
    T, D = x.shape[1], x.shape[2]
    QL, KVL = w["g_cq"].shape[1], w["g_ckv"].shape[1]
    dims = {
        "H": 4 * w["w_uq"].shape[2] // MLA_QK, "HD": w["w_br_dil"].shape[1] // DIL_HEAD, "QL": QL, "KVL": KVL,
        "LP": _round_up(QL + KVL + MLA_ROPE, LANES), "tf": _pick(4 * w["w1_gate"].shape[2], 512),
    }
    shard_shapes = {n: w[n].shape[1:] for n in BIG}
    small_shapes = {n: w[n].shape for n in SMALL}

    gathered = _all_gather_shards(_pack_shards({n: w[n][0] for n in BIG}, BF16))
    W = _compute_layout(_unpack_full(gathered, shard_shapes), dims)
    Gn = {n: w[n] for n in SMALL}
    Gn["g_q_mla"] = _pad_to(Gn["g_q_mla"], MLA_QK_PAD, 1)
    Gn["g_k_mla"] = _pad_to(Gn["g_k_mla"], MLA_QK_PAD, 1)
    Gn["g_q_dil"] = Gn["g_q_dil"].reshape(len(DIL_GROUPS), 1, DIL_HEAD)
    Gn["g_k_dil"] = Gn["g_k_dil"].reshape(len(DIL_GROUPS), 1, DIL_HEAD)

    pos_b = jnp.broadcast_to(positions.astype(F32).reshape(T, 1), (T, LANES))
    loss, dx, gw, gg = _local_step(x[0], p[0, 0], pos_b, loss_target[0], W, Gn, dims)
    loss = lax.psum(loss[0, 0], ("x", "y", "c"))

    packed = _pack_full(_natural_grads(gw, dims), shard_shapes)
    R = packed.shape[1]
    halves = packed.reshape(4, 2, R // 2, PACK_COLS)
    c_idx = lax.axis_index("c").astype(jnp.int32).reshape(1)
    pair = _pair_sum(halves, _send_other_half(halves), c_idx)
    reduced = _join_halves(_sum_slots(_exchange_chips(pair))).reshape(R, PACK_COLS)

    delta_p, m_p, v_p = _adamw(_pack_shards({n: w[n][0] for n in BIG}, F32), reduced,
                               _pack_shards({n: m[n][0] for n in BIG}, F32),
                               _pack_shards({n: v[n][0] for n in BIG}, F32), "adamw_big")
    outs = {"grad": {}, "delta": {}, "m": {}, "v": {}}
    for kind, buf in (("grad", reduced), ("delta", delta_p), ("m", m_p), ("v", v_p)):
        for n, a in _unpack_shards(buf, shard_shapes).items():
            outs[kind][n] = a.reshape((1,) + a.shape)

    gg["g_q_mla"] = gg["g_q_mla"][:, :MLA_QK]
    gg["g_k_mla"] = gg["g_k_mla"][:, :MLA_QK]
    g_small = _all_reduce_small(_pack_small(gg))
    d_s, m_s, v_s = _adamw(_pack_small({n: w[n] for n in SMALL}), g_small, _pack_small({n: m[n] for n in SMALL}),
                           _pack_small({n: v[n] for n in SMALL}), "adamw_gains")
    for kind, buf in (("grad", g_small), ("delta", d_s), ("m", m_s), ("v", v_s)):
        outs[kind].update(_unpack_small(buf, small_shapes))

    grad_x = dx.reshape(1, T, D)
    return (loss, grad_x, *[outs["grad"][n] for n in WEIGHTS], *[outs["delta"][n] for n in WEIGHTS],
            *[outs["m"][n] for n in WEIGHTS], *[outs["v"][n] for n in WEIGHTS])


def kernel(x, p, positions, g_ffn1, w1_gate, w1_up, w1_down, g_mix, w_in, g_cq, w_uq, g_ckv, w_ukv, g_q_mla, g_k_mla, g_q_dil, g_k_dil, w_br_mla, w_br_dil, w_o, g_ffn2, w2_gate, w2_up, w2_down, g_ple, w_ple_gate, w_ple_proj, loss_target, m_g_ffn1, m_w1_gate, m_w1_up, m_w1_down, m_g_mix, m_w_in, m_g_cq, m_w_uq, m_g_ckv, m_w_ukv, m_g_q_mla, m_g_k_mla, m_g_q_dil, m_g_k_dil, m_w_br_mla, m_w_br_dil, m_w_o, m_g_ffn2, m_w2_gate, m_w2_up, m_w2_down, m_g_ple, m_w_ple_gate, m_w_ple_proj, v_g_ffn1, v_w1_gate, v_w1_up, v_w1_down, v_g_mix, v_w_in, v_g_cq, v_w_uq, v_g_ckv, v_w_ukv, v_g_q_mla, v_g_k_mla, v_g_q_dil, v_g_k_dil, v_w_br_mla, v_w_br_dil, v_w_o, v_g_ffn2, v_w2_gate, v_w2_up, v_w2_down, v_g_ple, v_w_ple_gate, v_w_ple_proj):
    args = locals()
    w = {n: args[n] for n in WEIGHTS}
    m = {n: args["m_" + n] for n in WEIGHTS}
    v = {n: args["v_" + n] for n in WEIGHTS}
    return _step(x, p, positions, loss_target, w, m, v)
```

```python
import functools

import numpy as np
import jax
import jax.numpy as jnp
from jax import lax
from jax.experimental import pallas as pl
from jax.experimental.pallas import tpu as pltpu

F32 = jnp.float32
BF16 = jnp.bfloat16
MESH_ID = pl.DeviceIdType.MESH

MLA_NOPE = 128
MLA_ROPE = 64
MLA_V = 128
MLA_QK = MLA_NOPE + MLA_ROPE
MLA_QK_PAD = 256
DIL_GROUPS = ((128, 1), (512, 4), (2048, 16))
DIL_HEAD = 128
DIL_ROT = DIL_HEAD // 4
DIL_BLOCK = 128
ROPE_THETA = 500000.0
EPS = 1e-6
NEG = -1e30
ADAM_LR = 0.001
ADAM_B1 = 0.9
ADAM_B2 = 0.999
ADAM_EPS = 1e-08
ADAM_WD = 0.01
ADAM_STEP = 10

LANES = 128
PACK_COLS = 1024
VMEM_LIMIT_BYTES = 56 * 1024 * 1024

BIG = ("w1_gate", "w1_up", "w1_down", "w_in", "w_uq", "w_ukv", "w_br_mla", "w_br_dil", "w_o",
       "w2_gate", "w2_up", "w2_down", "w_ple_gate", "w_ple_proj")
GATHER_PLAN = (("w1_gu", ("w1_gate", "w1_up")), ("w1_down", ("w1_down",)), ("w_in", ("w_in",)), ("w_uq", ("w_uq",)),
               ("w_ukv", ("w_ukv",)), ("w_br_mla", ("w_br_mla",)), ("w_br_dil", ("w_br_dil",)), ("w_o", ("w_o",)),
               ("w2_gu", ("w2_gate", "w2_up")), ("w2_down", ("w2_down",)), ("w_ple_gate", ("w_ple_gate",)),
               ("w_ple_proj", ("w_ple_proj",)))
ROW_SHARDED = ("w1_down", "w_o", "w2_down", "w_ple_gate")
SMALL = ("g_ffn1", "g_mix", "g_cq", "g_ckv", "g_q_mla", "g_k_mla", "g_q_dil", "g_k_dil", "g_ffn2", "g_ple")
WEIGHTS = ("g_ffn1", "w1_gate", "w1_up", "w1_down", "g_mix", "w_in", "g_cq", "w_uq", "g_ckv", "w_ukv", "g_q_mla",
           "g_k_mla", "g_q_dil", "g_k_dil", "w_br_mla", "w_br_dil", "w_o", "g_ffn2", "w2_gate", "w2_up", "w2_down",
           "g_ple", "w_ple_gate", "w_ple_proj")


def _pick(n, target, align=LANES):
    if n <= target:
        return n
    t = (target // align) * align
    while t >= align:
        if n % t == 0:
            return t
        t -= align
    return n


def _params(n_axes):
    return pltpu.CompilerParams(dimension_semantics=("arbitrary",) * n_axes, vmem_limit_bytes=VMEM_LIMIT_BYTES)


def _sigmoid(x):
    return 1.0 / (1.0 + jnp.exp(-x))


def _mm(a, b, mode, out_dtype, name, res=None, alpha=1.0, a2=None, b2=None, b2_k_offset=0, tm=1024, tn=1536, tk=1024):
    if mode == "nn":
        (M, K), (K2, N) = a.shape, b.shape
    elif mode == "nt":
        (M, K), (N, K2) = a.shape, b.shape
    else:
        (K, M), (K2, N) = a.shape, b.shape
    assert K == K2 or (mode == "nt" and K2 > K), (name, a.shape, b.shape)
    assert a.dtype == BF16 and b.dtype == BF16, name
    tm, tn, tk = _pick(M, tm), _pick(N, tn), _pick(K, tk)
    nk = K // tk
    assert b2_k_offset % tk == 0 and (b2_k_offset == 0 or mode == "nt"), name
    k_off2 = b2_k_offset // tk
    if mode == "nn":
        a_spec = pl.BlockSpec((tm, tk), lambda i, j, k: (i, k))
        b_spec = pl.BlockSpec((tk, tn), lambda i, j, k: (k, j))
        dims = (((1,), (0,)), ((), ()))
    elif mode == "nt":
        a_spec = pl.BlockSpec((tm, tk), lambda i, j, k: (i, k))
        b_spec = pl.BlockSpec((tn, tk), lambda i, j, k: (j, k))
        b2_spec = pl.BlockSpec((tn, tk), lambda i, j, k: (j, k + k_off2))
        dims = (((1,), (1,)), ((), ()))
    else:
        a_spec = pl.BlockSpec((tk, tm), lambda i, j, k: (k, i))
        b_spec = pl.BlockSpec((tk, tn), lambda i, j, k: (k, j))
        dims = (((0,), (0,)), ((), ()))
    o_spec = pl.BlockSpec((tm, tn), lambda i, j, k: (i, j))
    has_res = res is not None
    n_pairs = 1 if a2 is None else 2

    def kern(*refs):
        r_ref = refs[2 * n_pairs] if has_res else None
        o_ref = refs[2 * n_pairs + int(has_res)]
        part = lax.dot_general(refs[0][...], refs[1][...], dims, preferred_element_type=F32)
        if n_pairs == 2:
            part = part + lax.dot_general(refs[2][...], refs[3][...], dims, preferred_element_type=F32)

        def finish(r):
            if alpha != 1.0:
                r = r * alpha
            if has_res:
                r = r_ref[...] + r
            o_ref[...] = r.astype(o_ref.dtype)

        if nk == 1:
            finish(part)
            return
        acc_ref = refs[-1]
        k = pl.program_id(2)

        @pl.when(k == 0)
        def _():
            acc_ref[...] = part

        @pl.when(k > 0)
        def _():
            acc_ref[...] += part

        @pl.when(k == nk - 1)
        def _():
            finish(acc_ref[...])

    ins = (a, b) + ((a2, b2) if n_pairs == 2 else ()) + ((res,) if has_res else ())
    in_specs = [a_spec, b_spec] + ([a_spec, b2_spec if mode == "nt" else b_spec] if n_pairs == 2 else [])
    in_specs += [o_spec] if has_res else []
    return pl.pallas_call(
        kern, name=name, grid=(M // tm, N // tn, nk), in_specs=in_specs, out_specs=o_spec,
        out_shape=jax.ShapeDtypeStruct((M, N), out_dtype),
        scratch_shapes=[pltpu.VMEM((tm, tn), F32)] if nk > 1 else [],
        compiler_params=_params(3))(*ins)


def _vcall(body, grid, ins, in_specs, out_shapes, out_specs, name, n_inner_acc=0, n_acc=0):
    n_in, n_out = len(ins), len(out_shapes)
    n_plain = n_out - n_acc - n_inner_acc

    def kern(*refs):
        vals = body(*[r[...] for r in refs[:n_in]])
        if not isinstance(vals, (tuple, list)):
            vals = (vals,)
        out_refs = refs[n_in:]
        inner_first = pl.program_id(len(grid) - 1) == 0
        first = inner_first
        for ax in range(len(grid) - 1):
            first = jnp.logical_and(first, pl.program_id(ax) == 0)
        for idx, (r, v) in enumerate(zip(out_refs, vals)):
            if idx < n_plain:
                r[...] = v.astype(r.dtype)
                continue
            start = inner_first if idx < n_plain + n_inner_acc else first

            @pl.when(start)
            def _(r=r, v=v):
                r[...] = v.astype(r.dtype)

            @pl.when(jnp.logical_not(start))
            def _(r=r, v=v):
                r[...] += v.astype(r.dtype)

    out = pl.pallas_call(kern, name=name, grid=grid, in_specs=in_specs, out_specs=out_specs, out_shape=out_shapes,
                         compiler_params=_params(len(grid)))(*ins)
    return out


def _rows(tm, c):
    return pl.BlockSpec((tm, c), lambda i: (i, 0))


def _vec(c):
    return pl.BlockSpec((1, c), lambda i: (0, 0))


def _sds(shape, dtype):
    return jax.ShapeDtypeStruct(shape, dtype)


def _rstd(x, c):
    return lax.rsqrt(jnp.sum(x * x, axis=-1, keepdims=True) * (1.0 / c) + EPS)


def _rms_bwd(xh, r, g, dn, c):
    u = dn * g
    dx = r * (u - xh * (jnp.sum(xh * u, axis=-1, keepdims=True) * (1.0 / c)))
    return dx, jnp.sum(dn * xh, axis=0, keepdims=True)


def _rope(t, c, sa, sb, half):
    return t * c + pltpu.roll(t, LANES - half, 1) * sa + pltpu.roll(t, half, 1) * sb


def _rope_t(d, c, sa, sb, half):
    return d * c + pltpu.roll(d * sa, half, 1) + pltpu.roll(d * sb, LANES - half, 1)


def _rope_tables(pos_b, rd, name):
    T = pos_b.shape[0]
    half = rd // 2
    inv = ROPE_THETA ** (-jnp.arange(half, dtype=F32) * 2.0 / rd)
    inv_full = jnp.concatenate([inv, inv, jnp.zeros((LANES - rd,), F32)]).reshape(1, LANES)
    lane = np.arange(LANES)
    ma = jnp.asarray((lane < half).astype(np.float32)).reshape(1, LANES)
    mb = jnp.asarray(((lane >= half) & (lane < rd)).astype(np.float32)).reshape(1, LANES)
    tm = _pick(T, 1024, 8)

    def body(pos, invf, a, b):
        ang = pos * invf
        c, s = jnp.cos(ang), jnp.sin(ang)
        inside = a + b
        return c * inside + (1.0 - inside), -s * a, s * b

    return _vcall(body, (T // tm,), (pos_b, inv_full, ma, mb), [_rows(tm, LANES)] + [_vec(LANES)] * 3,
                  [_sds((T, LANES), F32)] * 3, [_rows(tm, LANES)] * 3, name)


def _rms_fwd(x, g, name):
    T, C = x.shape
    tm = _pick(T, 512, 8)

    def body(xv, gv):
        return xv * _rstd(xv, C) * gv

    return _vcall(body, (T // tm,), (x, g), [_rows(tm, C), _vec(C)], [_sds((T, C), BF16)], [_rows(tm, C)], name)[0]


def _rms_bwd_call(x, g, dn, dres, name):
    T, C = x.shape
    tm = _pick(T, 256, 8)

    def body(xv, gv, dnv, drv):
        r = _rstd(xv, C)
        dx, dg = _rms_bwd(xv * r, r, gv, dnv.astype(F32), C)
        return drv + dx, dg

    return _vcall(body, (T // tm,), (x, g, dn, dres), [_rows(tm, C), _vec(C), _rows(tm, C), _rows(tm, C)],
                  [_sds((T, C), F32), _sds((1, C), F32)], [_rows(tm, C), _vec(C)], name, n_acc=1)


def _silu_mul(ab, tf, name):
    T, F2 = ab.shape
    F = F2 // 2
    nf = F // tf
    tm = _pick(T, 512, 8)

    def body(a, b):
        a, b = a.astype(F32), b.astype(F32)
        return a * _sigmoid(a) * b

    tile = pl.BlockSpec((tm, tf), lambda i, j: (i, j))
    return _vcall(body, (T // tm, nf), (ab, ab), [tile, pl.BlockSpec((tm, tf), lambda i, j: (i, j + nf))],
                  [_sds((T, F), BF16)], [tile], name)[0]


def _swiglu_bwd(dact, ab, tf, name):
    T, F = dact.shape
    nf = F // tf
    tm = _pick(T, 512, 8)

    def body(dv, a, b):
        d, a, b = dv.astype(F32), a.astype(F32), b.astype(F32)
        sg = _sigmoid(a)
        return d * b * (sg * (1.0 + a * (1.0 - sg))), d * (a * sg)

    tile = pl.BlockSpec((tm, tf), lambda i, j: (i, j))
    return _vcall(body, (T // tm, nf), (dact, ab, ab), [tile, tile, pl.BlockSpec((tm, tf), lambda i, j: (i, j + nf))],
                  [_sds((T, F), BF16)] * 2, [tile, tile], name)


def _lat_fwd(lat, g_cq, g_ckv, name):
    T, LP = lat.shape
    QL, KVL = g_cq.shape[1], g_ckv.shape[1]
    tm = _pick(T, 512, 8)

    def body(v, gq, gk):
        xq, xk = v[:, :QL], v[:, QL:QL + KVL]
        return xq * _rstd(xq, QL) * gq, xk * _rstd(xk, KVL) * gk

    return _vcall(body, (T // tm,), (lat, g_cq, g_ckv), [_rows(tm, LP), _vec(QL), _vec(KVL)],
                  [_sds((T, QL), BF16), _sds((T, KVL), BF16)], [_rows(tm, QL), _rows(tm, KVL)], name)


def _lat_bwd(dcq, dckv, dkr, lat, g_cq, g_ckv, name):
    T, LP = lat.shape
    QL, KVL = g_cq.shape[1], g_ckv.shape[1]
    tm = _pick(T, 512, 8)

    def body(dq, dk, dr, v, gq, gk):
        xq, xk = v[:, :QL], v[:, QL:QL + KVL]
        rq, rk = _rstd(xq, QL), _rstd(xk, KVL)
        dxq, dgq = _rms_bwd(xq * rq, rq, gq, dq, QL)
        dxk, dgk = _rms_bwd(xk * rk, rk, gk, dk, KVL)
        return jnp.concatenate([dxq, dxk, dr], axis=1), dgq, dgk

    return _vcall(body, (T // tm,), (dcq, dckv, dkr, lat, g_cq, g_ckv),
                  [_rows(tm, QL), _rows(tm, KVL), _rows(tm, LANES), _rows(tm, LP), _vec(QL), _vec(KVL)],
                  [_sds((T, LP), BF16), _sds((1, QL), F32), _sds((1, KVL), F32)],
                  [_rows(tm, LP), _vec(QL), _vec(KVL)], name, n_acc=2)


def _head_spec(tm, w):
    return pl.BlockSpec((tm, w), lambda i, h: (i, h))


def _row2(tm, w, col=0):
    return pl.BlockSpec((tm, w), lambda i, h: (i, col))


def _vec2(w):
    return pl.BlockSpec((1, w), lambda i, h: (0, 0))


def _mla_q_prep(q_raw, g_q, tabs, H, scale, name):
    T = q_raw.shape[0]
    tm = _pick(T, 512, 8)
    half = MLA_ROPE // 2

    def body(x, g, c, sa, sb):
        n = x * _rstd(x, MLA_QK) * g
        return jnp.concatenate([n[:, :LANES], _rope(n[:, LANES:], c, sa, sb, half)], axis=1) * scale

    return _vcall(body, (T // tm, H), (q_raw, g_q) + tabs,
                  [_head_spec(tm, MLA_QK_PAD), _vec2(MLA_QK_PAD)] + [_row2(tm, LANES)] * 3,
                  [_sds((T, H * MLA_QK_PAD), BF16)], [_head_spec(tm, MLA_QK_PAD)], name)[0]


def _mla_q_bwd(dq, q_raw, g_q, tabs, H, scale, name):
    T = q_raw.shape[0]
    tm = _pick(T, 512, 8)
    half = MLA_ROPE // 2

    def body(d, x, g, c, sa, sb):
        r = _rstd(x, MLA_QK)
        d = d * scale
        dn = jnp.concatenate([d[:, :LANES], _rope_t(d[:, LANES:], c, sa, sb, half)], axis=1)
        return _rms_bwd(x * r, r, g, dn, MLA_QK)

    return _vcall(body, (T // tm, H), (dq, q_raw, g_q) + tabs,
                  [_head_spec(tm, MLA_QK_PAD), _head_spec(tm, MLA_QK_PAD), _vec2(MLA_QK_PAD)] + [_row2(tm, LANES)] * 3,
                  [_sds((T, H * MLA_QK_PAD), BF16), _sds((1, MLA_QK_PAD), F32)],
                  [_head_spec(tm, MLA_QK_PAD), _vec2(MLA_QK_PAD)], name, n_acc=1)


def _mla_k_prep(kv, lat, kr_col, g_k, tabs, H, name):
    T = kv.shape[0]
    tm = _pick(T, 512, 8)
    half = MLA_ROPE // 2

    def body(x, kr, g, c, sa, sb):
        kn = x[:, :LANES]
        r = lax.rsqrt((jnp.sum(kn * kn, axis=-1, keepdims=True) + jnp.sum(kr * kr, axis=-1, keepdims=True))
                      * (1.0 / MLA_QK) + EPS)
        k0 = kn * r * g[:, :LANES]
        k1 = _rope(kr * r * g[:, LANES:], c, sa, sb, half)
        return jnp.concatenate([k0, k1], axis=1), x[:, LANES:]

    return _vcall(body, (T // tm, H), (kv, lat, g_k) + tabs,
                  [_head_spec(tm, 2 * LANES), _row2(tm, LANES, kr_col), _vec2(MLA_QK_PAD)] + [_row2(tm, LANES)] * 3,
                  [_sds((T, H * MLA_QK_PAD), BF16), _sds((T, H * MLA_V), BF16)],
                  [_head_spec(tm, MLA_QK_PAD), _head_spec(tm, MLA_V)], name)


def _mla_k_bwd(dk, dv, kv, lat, kr_col, g_k, tabs, H, name):
    T = kv.shape[0]
    tm = _pick(T, 512, 8)
    half = MLA_ROPE // 2

    def body(d, dvv, x, kr, g, c, sa, sb):
        xx = jnp.concatenate([x[:, :LANES], kr], axis=1)
        r = _rstd(xx, MLA_QK)
        dn = jnp.concatenate([d[:, :LANES], _rope_t(d[:, LANES:], c, sa, sb, half)], axis=1)
        dx, dg = _rms_bwd(xx * r, r, g, dn, MLA_QK)
        return jnp.concatenate([dx[:, :LANES], dvv], axis=1), dx[:, LANES:], dg

    return _vcall(body, (T // tm, H), (dk, dv, kv, lat, g_k) + tabs,
                  [_head_spec(tm, MLA_QK_PAD), _head_spec(tm, MLA_V), _head_spec(tm, 2 * LANES),
                   _row2(tm, LANES, kr_col), _vec2(MLA_QK_PAD)] + [_row2(tm, LANES)] * 3,
                  [_sds((T, H * 2 * LANES), BF16), _sds((T, LANES), F32), _sds((1, MLA_QK_PAD), F32)],
                  [_head_spec(tm, 2 * LANES), _row2(tm, LANES), _vec2(MLA_QK_PAD)], name, n_inner_acc=1, n_acc=1)


def _dil_prep(pd, g_q, g_k, tabs, HD, scale, name):
    T = pd.shape[0]
    W = HD * DIL_HEAD
    G = len(DIL_GROUPS)
    tm = _pick(T, 256, 8)
    half = DIL_ROT // 2

    def body(xq, xk, gq, gk, c, sa, sb):
        outs = []
        for x, g, s in ((xq, gq, scale), (xk, gk, 1.0)):
            heads = []
            for h in range(HD):
                xs = x[:, h * DIL_HEAD:(h + 1) * DIL_HEAD].astype(F32)
                n = _rope(xs * _rstd(xs, DIL_HEAD) * g, c, sa, sb, half)
                heads.append(n * s if s != 1.0 else n)
            outs.append(jnp.concatenate(heads, axis=1))
        return tuple(outs)

    gspec = pl.BlockSpec((None, 1, DIL_HEAD), lambda i, g: (g, 0, 0))
    return _vcall(body, (T // tm, G), (pd, pd, g_q, g_k) + tabs,
                  [pl.BlockSpec((tm, W), lambda i, g: (i, 3 * g)), pl.BlockSpec((tm, W), lambda i, g: (i, 3 * g + 1)),
                   gspec, gspec] + [_row2(tm, LANES)] * 3,
                  [_sds((T, G * W), BF16)] * 2, [pl.BlockSpec((tm, W), lambda i, g: (i, g))] * 2, name)


def _dil_prep_bwd(dq, dk, dv, pd, grp, g_q, g_k, tabs, HD, scale, name):
    T = pd.shape[0]
    W = HD * DIL_HEAD
    tm = _pick(T, 256, 8)
    half = DIL_ROT // 2

    def body(dqv, dkv, dvv, xq, xk, gq, gk, c, sa, sb):
        cols, dgs = [], []
        for d, x, g, s in ((dqv, xq, gq, scale), (dkv, xk, gk, 1.0)):
            heads, dg = [], None
            for h in range(HD):
                sl = slice(h * DIL_HEAD, (h + 1) * DIL_HEAD)
                xs = x[:, sl].astype(F32)
                r = _rstd(xs, DIL_HEAD)
                dh = d[:, sl] * s if s != 1.0 else d[:, sl]
                dx, dgh = _rms_bwd(xs * r, r, g, _rope_t(dh, c, sa, sb, half), DIL_HEAD)
                heads.append(dx)
                dg = dgh if dg is None else dg + dgh
            cols.append(jnp.concatenate(heads, axis=1))
            dgs.append(dg)
        return jnp.concatenate(cols + [dvv], axis=1), dgs[0], dgs[1]

    gq, gk = g_q[grp], g_k[grp]
    return _vcall(body, (T // tm,), (dq, dk, dv, pd, pd, gq, gk) + tabs,
                  [_rows(tm, W)] * 3 + [pl.BlockSpec((tm, W), lambda i: (i, 3 * grp)),
                                        pl.BlockSpec((tm, W), lambda i: (i, 3 * grp + 1)),
                                        _vec(DIL_HEAD), _vec(DIL_HEAD)] + [_rows(tm, LANES)] * 3,
                  [_sds((T, 3 * W), BF16), _sds((1, DIL_HEAD), F32), _sds((1, DIL_HEAD), F32)],
                  [_rows(tm, 3 * W), _vec(DIL_HEAD), _vec(DIL_HEAD)], name, n_acc=2)


def _dil_merge(os_, lses, name):
    T, W = os_[0].shape
    tm = _pick(T, 256, 8)

    def body(o0, o1, o2, l0, l1, l2):
        m = jnp.maximum(jnp.maximum(l0, l1), l2)
        w0, w1, w2 = jnp.exp(l0 - m), jnp.exp(l1 - m), jnp.exp(l2 - m)
        z = w0 + w1 + w2
        return (w0 * o0 + w1 * o1 + w2 * o2) / z, m + jnp.log(z)

    return _vcall(body, (T // tm,), tuple(os_) + tuple(lses), [_rows(tm, W)] * 6,
                  [_sds((T, W), BF16), _sds((T, W), F32)], [_rows(tm, W)] * 2, name)


def _gate_merge(pg, bm, bd, name):
    T, D = bm.shape
    tm = _pick(T, 256, 8)

    def body(g, m, d):
        g = g.astype(F32)
        return _sigmoid(g[:, :D]) * m + _sigmoid(g[:, D:]) * d

    return _vcall(body, (T // tm,), (pg, bm, bd), [_rows(tm, 2 * D), _rows(tm, D), _rows(tm, D)],
                  [_sds((T, D), BF16)], [_rows(tm, D)], name)[0]


def _gate_bwd(dmerged, pg, bm, bd, name):
    T, D = bm.shape
    tm = _pick(T, 256, 8)

    def body(dm, g, m, d):
        g = g.astype(F32)
        s0, s1 = _sigmoid(g[:, :D]), _sigmoid(g[:, D:])
        dpg = jnp.concatenate([dm * m * s0 * (1.0 - s0), dm * d * s1 * (1.0 - s1)], axis=1)
        return dm * s0, dm * s1, dpg

    return _vcall(body, (T // tm,), (dmerged, pg, bm, bd), [_rows(tm, D), _rows(tm, 2 * D), _rows(tm, D), _rows(tm, D)],
                  [_sds((T, D), BF16), _sds((T, D), BF16), _sds((T, 2 * D), BF16)],
                  [_rows(tm, D), _rows(tm, D), _rows(tm, 2 * D)], name)


def _ple_loss(x3, zg, pp, target, name):
    T, D = x3.shape
    tm = _pick(T, 256, 8)

    def body(x, z, p_, t):
        s = _sigmoid(z)
        e = x + s * p_ - t
        dy = e * (1.0 / D)
        part = 0.5 * jnp.sum(jnp.sum(e * e, axis=1, keepdims=True), axis=0, keepdims=True) * (1.0 / D)
        return dy, dy * s, dy * p_ * s * (1.0 - s), jnp.broadcast_to(part, (1, LANES))

    return _vcall(body, (T // tm,), (x3, zg, pp, target), [_rows(tm, D)] * 4,
                  [_sds((T, D), F32), _sds((T, D), BF16), _sds((T, D), BF16), _sds((1, LANES), F32)],
                  [_rows(tm, D)] * 3 + [_vec(LANES)], name, n_acc=1)


NT = (((1,), (1,)), ((), ()))
TN = (((0,), (0,)), ((), ()))


def _diag_mask(s):
    row = lax.broadcasted_iota(jnp.int32, s.shape, 0)
    col = lax.broadcasted_iota(jnp.int32, s.shape, 1)
    return jnp.where(col <= row, s, NEG)


def _causal_pairs(nq, key_major):
    if key_major:
        pairs = [(i, j) for j in range(nq) for i in range(j, nq)]
    else:
        pairs = [(i, j) for i in range(nq) for j in range(i + 1)]
    return (jnp.asarray([pr[0] for pr in pairs], jnp.int32), jnp.asarray([pr[1] for pr in pairs], jnp.int32))


def _mla_fwd(q, k, v, H, name):
    T = q.shape[0]
    tq = _pick(T, 512)
    nq = T // tq
    hb = 2 if H % 2 == 0 else 1
    qi_tab, kj_tab = _causal_pairs(nq, key_major=False)

    def kern(qi_ref, kj_ref, q_ref, k_ref, v_ref, o_ref, lse_ref, m_sc, l_sc, acc_sc):
        t = pl.program_id(1)
        qi, kj = qi_ref[t], kj_ref[t]

        @pl.when(kj == 0)
        def _():
            m_sc[...] = jnp.full_like(m_sc, NEG)
            l_sc[...] = jnp.zeros_like(l_sc)
            acc_sc[...] = jnp.zeros_like(acc_sc)

        def tile(diagonal):
            for hh in range(hb):
                qs = slice(hh * MLA_QK_PAD, (hh + 1) * MLA_QK_PAD)
                vs = slice(hh * MLA_V, (hh + 1) * MLA_V)
                s = lax.dot_general(q_ref[:, qs], k_ref[:, qs], NT, preferred_element_type=F32)
                if diagonal:
                    s = _diag_mask(s)
                m_prev = m_sc[hh]
                m_new = jnp.maximum(m_prev, jnp.max(s, axis=1, keepdims=True))
                alpha = jnp.exp(m_prev - m_new)
                p = jnp.exp(s - m_new)
                l_new = alpha * l_sc[hh] + jnp.sum(p, axis=1, keepdims=True)
                acc = alpha * acc_sc[hh] + jnp.dot(p.astype(BF16), v_ref[:, vs], preferred_element_type=F32)
                if diagonal:
                    o_ref[:, vs] = (acc / l_new).astype(o_ref.dtype)
                    lse_ref[:, hh * LANES:(hh + 1) * LANES] = jnp.broadcast_to(m_new + jnp.log(l_new), (tq, LANES))
                else:
                    m_sc[hh] = m_new
                    l_sc[hh] = l_new
                    acc_sc[hh] = acc

        @pl.when(kj < qi)
        def _():
            tile(False)

        @pl.when(kj == qi)
        def _():
            tile(True)

    qspec = lambda w: pl.BlockSpec((tq, hb * w), lambda h, t, qi_ref, kj_ref: (qi_ref[t], h))
    kspec = lambda w: pl.BlockSpec((tq, hb * w), lambda h, t, qi_ref, kj_ref: (kj_ref[t], h))
    grid_spec = pltpu.PrefetchScalarGridSpec(
        num_scalar_prefetch=2, grid=(H // hb, qi_tab.shape[0]),
        in_specs=[qspec(MLA_QK_PAD), kspec(MLA_QK_PAD), kspec(MLA_V)], out_specs=[qspec(MLA_V), qspec(LANES)],
        scratch_shapes=[pltpu.VMEM((hb, tq, 1), F32), pltpu.VMEM((hb, tq, 1), F32), pltpu.VMEM((hb, tq, MLA_V), F32)])
    return pl.pallas_call(
        kern, name=name, grid_spec=grid_spec, out_shape=[_sds((T, H * MLA_V), BF16), _sds((T, H * LANES), F32)],
        compiler_params=_params(2))(qi_tab, kj_tab, q, k, v)


def _mla_bwd(q, k, v, do, o, lse, H, name):
    T = q.shape[0]
    tq = _pick(T, 512)
    nq = T // tq
    qi_tab, kj_tab = _causal_pairs(nq, key_major=True)

    def kern(qi_ref, kj_ref, q_ref, k_ref, v_ref, do_ref, o_ref, lse_ref, dq_ref, dk_ref, dv_ref, dk_sc, dv_sc):
        t = pl.program_id(1)
        qi, kj = qi_ref[t], kj_ref[t]
        rows = pl.ds(pl.multiple_of(qi * tq, tq), tq)

        def tile(diagonal):
            s = lax.dot_general(q_ref[...], k_ref[...], NT, preferred_element_type=F32)
            if diagonal:
                s = _diag_mask(s)
            p = jnp.exp(s - lse_ref[:, :1])
            dl = jnp.sum(do_ref[...].astype(F32) * o_ref[...].astype(F32), axis=1, keepdims=True)
            dp = lax.dot_general(do_ref[...], v_ref[...], NT, preferred_element_type=F32)
            ds = (p * (dp - dl)).astype(BF16)
            dv = lax.dot_general(p.astype(BF16), do_ref[...], TN, preferred_element_type=F32)
            dk = lax.dot_general(ds, q_ref[...], TN, preferred_element_type=F32)
            dq = jnp.dot(ds, k_ref[...], preferred_element_type=F32)
            if diagonal:
                dv_sc[...] = dv
                dk_sc[...] = dk
            else:
                dv_sc[...] += dv
                dk_sc[...] += dk

            @pl.when(kj == 0)
            def _():
                dq_ref[rows, :] = dq

            @pl.when(kj > 0)
            def _():
                dq_ref[rows, :] += dq

        @pl.when(qi == kj)
        def _():
            tile(True)

        @pl.when(qi > kj)
        def _():
            tile(False)

        @pl.when(qi == nq - 1)
        def _():
            dk_ref[...] = dk_sc[...]
            dv_ref[...] = dv_sc[...]

    qspec = lambda w: pl.BlockSpec((tq, w), lambda h, t, qi_ref, kj_ref: (qi_ref[t], h))
    kspec = lambda w: pl.BlockSpec((tq, w), lambda h, t, qi_ref, kj_ref: (kj_ref[t], h))
    grid_spec = pltpu.PrefetchScalarGridSpec(
        num_scalar_prefetch=2, grid=(H, qi_tab.shape[0]),
        in_specs=[qspec(MLA_QK_PAD), kspec(MLA_QK_PAD), kspec(MLA_V), qspec(MLA_V), qspec(MLA_V), qspec(LANES)],
        out_specs=[pl.BlockSpec((T, MLA_QK_PAD), lambda h, t, qi_ref, kj_ref: (0, h)), kspec(MLA_QK_PAD), kspec(MLA_V)],
        scratch_shapes=[pltpu.VMEM((tq, MLA_QK_PAD), F32), pltpu.VMEM((tq, MLA_V), F32)])
    return pl.pallas_call(
        kern, name=name, grid_spec=grid_spec,
        out_shape=[_sds((T, H * MLA_QK_PAD), F32), _sds((T, H * MLA_QK_PAD), F32), _sds((T, H * MLA_V), F32)],
        compiler_params=_params(2))(qi_tab, kj_tab, q, k, v, do, o, lse)


def _dil_views(qd, kd, pd, grp, dil, W):
    T = qd.shape[0]
    L = T // dil
    G = len(DIL_GROUPS)
    prev = lambda b: jnp.maximum(b - 1, 0)
    blk = lambda f: pl.BlockSpec((DIL_BLOCK, W), f)
    arrays = (qd.reshape(L, dil * G * W), kd.reshape(L, dil * G * W), pd.reshape(L, dil * 3 * G * W))
    q_spec = blk(lambda r, b: (b, r * G + grp))
    kc_spec = q_spec
    kp_spec = blk(lambda r, b: (prev(b), r * G + grp))
    vc_spec = blk(lambda r, b: (b, r * 3 * G + 3 * grp + 2))
    vp_spec = blk(lambda r, b: (prev(b), r * 3 * G + 3 * grp + 2))
    return arrays, (q_spec, kc_spec, kp_spec, vc_spec, vp_spec)


def _band_masks(b, nb_next=None):
    row = lax.broadcasted_iota(jnp.int32, (DIL_BLOCK, DIL_BLOCK), 0)
    col = lax.broadcasted_iota(jnp.int32, (DIL_BLOCK, DIL_BLOCK), 1)
    return row, col


def _dil_fwd(qd, kd, pd, grp, dil, HD, name):
    T = qd.shape[0]
    W = HD * DIL_HEAD
    L = T // dil
    nb = L // DIL_BLOCK
    (qv, kv_, vv), (q_spec, kc_spec, kp_spec, vc_spec, vp_spec) = _dil_views(qd, kd, pd, grp, dil, W)

    def kern(q_ref, kc_ref, kp_ref, vc_ref, vp_ref, o_ref, lse_ref):
        b = pl.program_id(1)
        row, col = _band_masks(b)
        off = jnp.where(b > 0, 0, 2 * DIL_BLOCK)
        ok_prev = col >= row + off
        ok_cur = col <= row
        for h in range(HD):
            sl = slice(h * DIL_HEAD, (h + 1) * DIL_HEAD)
            qh = q_ref[:, sl]
            sa = jnp.where(ok_prev, lax.dot_general(qh, kp_ref[:, sl], NT, preferred_element_type=F32), NEG)
            sb = jnp.where(ok_cur, lax.dot_general(qh, kc_ref[:, sl], NT, preferred_element_type=F32), NEG)
            m = jnp.maximum(jnp.max(sa, axis=1, keepdims=True), jnp.max(sb, axis=1, keepdims=True))
            ea, eb = jnp.exp(sa - m), jnp.exp(sb - m)
            l = jnp.sum(ea, axis=1, keepdims=True) + jnp.sum(eb, axis=1, keepdims=True)
            acc = (jnp.dot(ea.astype(BF16), vp_ref[:, sl], preferred_element_type=F32)
                   + jnp.dot(eb.astype(BF16), vc_ref[:, sl], preferred_element_type=F32))
            o_ref[:, sl] = acc / l
            lse_ref[:, sl] = jnp.broadcast_to(m + jnp.log(l), (DIL_BLOCK, DIL_HEAD))

    o_spec = pl.BlockSpec((DIL_BLOCK, W), lambda r, b: (b, r))
    o, lse = pl.pallas_call(
        kern, name=name, grid=(dil, nb), in_specs=[q_spec, kc_spec, kp_spec, vc_spec, vp_spec],
        out_specs=[o_spec, o_spec], out_shape=[_sds((L, dil * W), F32)] * 2,
        compiler_params=_params(2))(qv, kv_, kv_, vv, vv)
    return o.reshape(T, W), lse.reshape(T, W)


def _dil_bwd_dq(qd, kd, pd, do, o, lse, grp, dil, HD, name):
    T = qd.shape[0]
    W = HD * DIL_HEAD
    L = T // dil
    nb = L // DIL_BLOCK
    (qv, kv_, vv), (q_spec, kc_spec, kp_spec, vc_spec, vp_spec) = _dil_views(qd, kd, pd, grp, dil, W)
    t_spec = pl.BlockSpec((DIL_BLOCK, W), lambda r, b: (b, r))
    view = lambda a: a.reshape(L, dil * W)

    def kern(q_ref, kc_ref, kp_ref, vc_ref, vp_ref, do_ref, o_ref, lse_ref, dq_ref):
        b = pl.program_id(1)
        row, col = _band_masks(b)
        off = jnp.where(b > 0, 0, 2 * DIL_BLOCK)
        ok_prev = col >= row + off
        ok_cur = col <= row
        for h in range(HD):
            sl = slice(h * DIL_HEAD, (h + 1) * DIL_HEAD)
            qh, doh = q_ref[:, sl], do_ref[:, sl]
            lse_h = lse_ref[:, sl]
            dl = jnp.sum(doh.astype(F32) * o_ref[:, sl].astype(F32), axis=1, keepdims=True)
            sa = jnp.where(ok_prev, lax.dot_general(qh, kp_ref[:, sl], NT, preferred_element_type=F32), NEG)
            sb = jnp.where(ok_cur, lax.dot_general(qh, kc_ref[:, sl], NT, preferred_element_type=F32), NEG)
            pa, pb = jnp.exp(sa - lse_h), jnp.exp(sb - lse_h)
            dsa = pa * (lax.dot_general(doh, vp_ref[:, sl], NT, preferred_element_type=F32) - dl)
            dsb = pb * (lax.dot_general(doh, vc_ref[:, sl], NT, preferred_element_type=F32) - dl)
            dq_ref[:, sl] = (jnp.dot(dsa.astype(BF16), kp_ref[:, sl], preferred_element_type=F32)
                             + jnp.dot(dsb.astype(BF16), kc_ref[:, sl], preferred_element_type=F32))

    dq = pl.pallas_call(
        kern, name=name, grid=(dil, nb),
        in_specs=[q_spec, kc_spec, kp_spec, vc_spec, vp_spec, t_spec, t_spec, t_spec],
        out_specs=t_spec, out_shape=_sds((L, dil * W), F32),
        compiler_params=_params(2))(qv, kv_, kv_, vv, vv, view(do), view(o), view(lse))
    return dq.reshape(T, W)


def _dil_bwd_dkv(qd, kd, pd, do, o, lse, grp, dil, HD, name):
    T = qd.shape[0]
    W = HD * DIL_HEAD
    L = T // dil
    nb = L // DIL_BLOCK
    G = len(DIL_GROUPS)
    nxt = lambda b: jnp.minimum(b + 1, nb - 1)
    blk = lambda f: pl.BlockSpec((DIL_BLOCK, W), f)
    qc_spec = blk(lambda r, b: (b, r * G + grp))
    qn_spec = blk(lambda r, b: (nxt(b), r * G + grp))
    v_spec = blk(lambda r, b: (b, r * 3 * G + 3 * grp + 2))
    tc_spec = blk(lambda r, b: (b, r))
    tn_spec = blk(lambda r, b: (nxt(b), r))
    view = lambda a: a.reshape(L, dil * W)
    qv, kv_, vv = qd.reshape(L, dil * G * W), kd.reshape(L, dil * G * W), pd.reshape(L, dil * 3 * G * W)

    def kern(qc_ref, qn_ref, k_ref, v_ref, doc_ref, don_ref, oc_ref, on_ref, lc_ref, ln_ref, dk_ref, dv_ref):
        b = pl.program_id(1)
        row, col = _band_masks(b)
        off = jnp.where(b + 1 < nb, 0, 2 * DIL_BLOCK)
        ok_next = col >= row + off
        ok_cur = col <= row
        for h in range(HD):
            sl = slice(h * DIL_HEAD, (h + 1) * DIL_HEAD)
            kh, vh = k_ref[:, sl], v_ref[:, sl]
            dk = jnp.zeros((DIL_BLOCK, DIL_HEAD), F32)
            dv = jnp.zeros((DIL_BLOCK, DIL_HEAD), F32)
            for q_ref, do_ref, o_ref, l_ref, ok in ((qc_ref, doc_ref, oc_ref, lc_ref, ok_cur),
                                                    (qn_ref, don_ref, on_ref, ln_ref, ok_next)):
                qh, doh = q_ref[:, sl], do_ref[:, sl]
                dl = jnp.sum(doh.astype(F32) * o_ref[:, sl].astype(F32), axis=1, keepdims=True)
                s = jnp.where(ok, lax.dot_general(qh, kh, NT, preferred_element_type=F32), NEG)
                p = jnp.exp(s - l_ref[:, sl])
                ds = p * (lax.dot_general(doh, vh, NT, preferred_element_type=F32) - dl)
                dv = dv + lax.dot_general(p.astype(BF16), doh, TN, preferred_element_type=F32)
                dk = dk + lax.dot_general(ds.astype(BF16), qh, TN, preferred_element_type=F32)
            dk_ref[:, sl] = dk
            dv_ref[:, sl] = dv

    dk, dv = pl.pallas_call(
        kern, name=name, grid=(dil, nb),
        in_specs=[qc_spec, qn_spec, qc_spec, v_spec, tc_spec, tn_spec, tc_spec, tn_spec, tc_spec, tn_spec],
        out_specs=[tc_spec, tc_spec], out_shape=[_sds((L, dil * W), F32)] * 2,
        compiler_params=_params(2))(qv, qv, kv_, vv, view(do), view(do), view(o), view(o), view(lse), view(lse))
    return dk.reshape(T, W), dv.reshape(T, W)


ANY = pl.BlockSpec(memory_space=pl.ANY)


def _place():
    return lax.axis_index("x"), lax.axis_index("y"), lax.axis_index("c")


def _other_chips(x, y):
    return [(1 - x, y), (x, 1 - y), (1 - x, 1 - y)]


def _kind(name, shard_shape):
    if name in ROW_SHARDED:
        return "row"
    return "col" if shard_shape[1] % LANES == 0 else "stack"


def _remote(src, dst, send_sem, recv_sem, to):
    return pltpu.make_async_remote_copy(src_ref=src, dst_ref=dst, send_sem=send_sem, recv_sem=recv_sem,
                                        device_id=to, device_id_type=MESH_ID)


def _all_gather_weights(shards):
    plan, out_shapes, out_names = [], [], []
    for out_name, names in GATHER_PLAN:
        r, c = shards[names[0]].shape
        kind = _kind(names[0], (r, c))
        assert kind == "col" or len(names) == 1, out_name
        out_shapes.append(_sds((r, 4 * c * len(names)) if kind == "col" else (4, r, c), BF16))
        out_names.append(out_name)
        for i, n in enumerate(names):
            assert shards[n].shape == (r, c), n
            plan.append((n, len(out_shapes) - 1, kind, i * 4 * c, r, c))
    n_w, n_out = len(plan), len(out_shapes)

    def body(*refs):
        ins, outs = refs[:n_w], refs[n_w:n_w + n_out]
        send_sems, recv_sems, local_sems = refs[n_w + n_out:]
        x, y, c = _place()
        me = 2 * x + y
        sibling = (x, y, 1 - c)
        chips = _other_chips(x, y)

        def region(w, chip, h=None):
            _, oi, kind, base, r, cc = plan[w]
            rows = slice(None) if h is None else pl.ds(h * (r // 2), r // 2)
            if kind == "col":
                return outs[oi].at[rows, pl.ds(pl.multiple_of(base + chip * cc, LANES), cc)]
            return outs[oi].at[chip, rows, :]

        local = [pltpu.make_async_copy(ins[w], region(w, me), local_sems.at[w]) for w in range(n_w)]
        for cp in local:
            cp.start()
        sent = []
        for w in range(n_w):
            r = plan[w][4]
            for j, (px, py) in enumerate(chips):
                k = 3 * w + j
                cp = _remote(ins[w].at[pl.ds(c * (r // 2), r // 2), :], region(w, me, c), send_sems.at[k], recv_sems.at[k],
                             (px, py, c))
                cp.start()
                sent.append(cp)
        passed = []
        for w in range(n_w):
            for j, (px, py) in enumerate(chips):
                k = 3 * w + j
                landed = region(w, 2 * px + py, c)
                _remote(landed, landed, send_sems.at[k], recv_sems.at[k], (px, py, c)).wait_recv()
                fw = _remote(landed, landed, send_sems.at[3 * n_w + k], recv_sems.at[3 * n_w + k], sibling)
                fw.start()
                passed.append(fw)
        for w in range(n_w):
            for j, (px, py) in enumerate(chips):
                k = 3 * n_w + 3 * w + j
                other = region(w, 2 * px + py, 1 - c)
                _remote(other, other, send_sems.at[k], recv_sems.at[k], sibling).wait_recv()
        for cp in sent + passed:
            cp.wait_send()
        for cp in local:
            cp.wait()

    outs = pl.pallas_call(
        body, name="ag_weights", out_shape=out_shapes, in_specs=[ANY] * n_w, out_specs=[ANY] * n_out,
        scratch_shapes=[pltpu.SemaphoreType.DMA((6 * n_w,)), pltpu.SemaphoreType.DMA((6 * n_w,)),
                        pltpu.SemaphoreType.DMA((n_w,))])(*[shards[pl_[0]] for pl_ in plan])
    return dict(zip(out_names, outs))


def _send_other_halves(views):
    n = len(views)

    def body(*refs):
        ins, outs, send_sems, recv_sems = refs[:n], refs[n:2 * n], refs[2 * n], refs[2 * n + 1]
        x, y, c = _place()
        cps = [_remote(ins[w].at[:, 1 - c], outs[w], send_sems.at[w], recv_sems.at[w], (x, y, 1 - c)) for w in range(n)]
        for cp in cps:
            cp.start()
        for cp in cps:
            cp.wait()

    return pl.pallas_call(
        body, name="rs_sibling", out_shape=[_sds((v.shape[0],) + v.shape[2:], v.dtype) for v in views],
        in_specs=[ANY] * n, out_specs=[ANY] * n,
        scratch_shapes=[pltpu.SemaphoreType.DMA((n,)), pltpu.SemaphoreType.DMA((n,))])(*views)


def _row_tile(rows, cols, itemsize, align):
    return _pick(rows, max(align, (2 * 1024 * 1024) // (cols * itemsize)), align)


def _pair_sum(g, got, c_idx, name):
    n, _, rows, C = g.shape
    tr = _row_tile(rows, C, 4, 16)

    def kern(c_ref, a_ref, b_ref, o_ref):
        o_ref[...] = (a_ref[...] + b_ref[...]).astype(o_ref.dtype)

    grid_spec = pltpu.PrefetchScalarGridSpec(
        num_scalar_prefetch=1, grid=(n, rows // tr),
        in_specs=[pl.BlockSpec((None, None, tr, C), lambda j, i, c_ref: (j, c_ref[0], i, 0)),
                  pl.BlockSpec((None, tr, C), lambda j, i, c_ref: (j, i, 0))],
        out_specs=pl.BlockSpec((None, tr, C), lambda j, i, c_ref: (j, i, 0)))
    return pl.pallas_call(kern, name=name, grid_spec=grid_spec, out_shape=_sds((n, rows, C), BF16),
                          compiler_params=_params(2))(c_idx, g, got)


def _exchange_chips(parts, kinds, widths):
    n = len(parts)

    def body(*refs):
        ins, outs = refs[:n], refs[n:2 * n]
        send_sems, recv_sems, local_sems = refs[2 * n:]
        x, y, c = _place()
        me = 2 * x + y
        chips = _other_chips(x, y)

        def piece(w, chip):
            if kinds[w] == "col":
                return ins[w].at[0, :, pl.ds(pl.multiple_of(chip * widths[w], LANES), widths[w])]
            return ins[w].at[chip]

        local = [pltpu.make_async_copy(piece(w, me), outs[w].at[me], local_sems.at[w]) for w in range(n)]
        for cp in local:
            cp.start()
        sent = []
        for w in range(n):
            for j, (px, py) in enumerate(chips):
                k = 3 * w + j
                cp = _remote(piece(w, 2 * px + py), outs[w].at[me], send_sems.at[k], recv_sems.at[k], (px, py, c))
                cp.start()
                sent.append(cp)
        for w in range(n):
            for j, (px, py) in enumerate(chips):
                k = 3 * w + j
                slot = outs[w].at[2 * px + py]
                _remote(slot, slot, send_sems.at[k], recv_sems.at[k], (px, py, c)).wait_recv()
        for cp in sent:
            cp.wait_send()
        for cp in local:
            cp.wait()

    return pl.pallas_call(
        body, name="rs_chips", out_shape=[_sds((4, p_.shape[1], widths[w]), p_.dtype) for w, p_ in enumerate(parts)],
        in_specs=[ANY] * n, out_specs=[ANY] * n,
        scratch_shapes=[pltpu.SemaphoreType.DMA((3 * n,)), pltpu.SemaphoreType.DMA((3 * n,)),
                        pltpu.SemaphoreType.DMA((n,))])(*parts)


def _sum_slots(parts, name):
    n, rows, C = parts.shape
    tr = _row_tile(rows, C, 2 * n, 16)

    def kern(p_ref, o_ref):
        acc = p_ref[0].astype(F32)
        for j in range(1, n):
            acc = acc + p_ref[j].astype(F32)
        o_ref[...] = acc

    return pl.pallas_call(
        kern, name=name, grid=(rows // tr,), in_specs=[pl.BlockSpec((n, tr, C), lambda i: (0, i, 0))],
        out_specs=pl.BlockSpec((tr, C), lambda i: (i, 0)), out_shape=_sds((rows, C), F32),
        compiler_params=_params(1))(parts)


def _join_halves(halves):
    n = len(halves)

    def body(*refs):
        ins, outs = refs[:n], refs[n:2 * n]
        send_sems, recv_sems, local_sems = refs[2 * n:]
        x, y, c = _place()
        sibling = (x, y, 1 - c)
        keep = [pltpu.make_async_copy(ins[w], outs[w].at[c], local_sems.at[w]) for w in range(n)]
        sent = [_remote(ins[w], outs[w].at[c], send_sems.at[w], recv_sems.at[w], sibling) for w in range(n)]
        for cp in keep + sent:
            cp.start()
        for w in range(n):
            other = outs[w].at[1 - c]
            _remote(other, other, send_sems.at[w], recv_sems.at[w], sibling).wait_recv()
        for cp in sent:
            cp.wait_send()
        for cp in keep:
            cp.wait()

    return pl.pallas_call(
        body, name="rs_join", out_shape=[_sds((2,) + h.shape, h.dtype) for h in halves],
        in_specs=[ANY] * n, out_specs=[ANY] * n,
        scratch_shapes=[pltpu.SemaphoreType.DMA((n,)), pltpu.SemaphoreType.DMA((n,)),
                        pltpu.SemaphoreType.DMA((n,))])(*halves)


def _all_reduce_small(vec):
    N = vec.shape[1]
    n_dev = 8

    def body(v_ref, out_ref, slots, send_sems, recv_sems):
        x, y, c = _place()
        me = 4 * x + 2 * y + c
        slots[me] = v_ref[...]
        sent = []
        for k in range(1, n_dev):
            px, py, pc = x ^ (k >> 2), y ^ ((k >> 1) & 1), c ^ (k & 1)
            cp = pltpu.make_async_remote_copy(src_ref=v_ref, dst_ref=slots.at[me], send_sem=send_sems.at[k - 1],
                                              recv_sem=recv_sems.at[k - 1], device_id=(px, py, pc),
                                              device_id_type=MESH_ID)
            cp.start()
            sent.append(cp)
        for k in range(1, n_dev):
            px, py, pc = x ^ (k >> 2), y ^ ((k >> 1) & 1), c ^ (k & 1)
            slot = slots.at[4 * px + 2 * py + pc]
            pltpu.make_async_remote_copy(src_ref=slot, dst_ref=slot, send_sem=send_sems.at[k - 1],
                                         recv_sem=recv_sems.at[k - 1], device_id=(px, py, pc),
                                         device_id_type=MESH_ID).wait_recv()
        for cp in sent:
            cp.wait_send()
        acc = slots[0]
        for j in range(1, n_dev):
            acc = acc + slots[j]
        out_ref[...] = acc

    vm = pl.BlockSpec(memory_space=pltpu.VMEM)
    return pl.pallas_call(
        body, name="ar_gains", out_shape=_sds((1, N), F32), in_specs=[vm], out_specs=vm,
        scratch_shapes=[pltpu.VMEM((n_dev, 1, N), F32), pltpu.SemaphoreType.DMA((n_dev - 1,)),
                        pltpu.SemaphoreType.DMA((n_dev - 1,))])(vec)


def _adamw(w, g, m, v, name):
    R, C = w.shape
    tr = _row_tile(R, C, 8, 8)

    def body(wv, gv, mv, vv):
        m2 = ADAM_B1 * mv + (1.0 - ADAM_B1) * gv
        v2 = ADAM_B2 * vv + (1.0 - ADAM_B2) * (gv * gv)
        m_hat = m2 / (1.0 - ADAM_B1 ** ADAM_STEP)
        v_hat = v2 / (1.0 - ADAM_B2 ** ADAM_STEP)
        return -ADAM_LR * (m_hat / (jnp.sqrt(v_hat) + ADAM_EPS) + ADAM_WD * wv), m2, v2

    return _vcall(body, (R // tr,), (w, g, m, v), [_rows(tr, C)] * 4, [_sds((R, C), F32)] * 3, [_rows(tr, C)] * 3, name)


def _pad_to(a, n, axis):
    extra = n - a.shape[axis]
    if extra == 0:
        return a
    pads = [(0, 0)] * a.ndim
    pads[axis] = (0, extra)
    return jnp.pad(a, pads)


def _round_up(n, m):
    return -(-n // m) * m


def _natural(buf, kind):
    if kind == "col":
        return buf
    n, r, c = buf.shape
    return buf.reshape(n * r, c) if kind == "row" else buf.transpose(1, 0, 2).reshape(r, n * c)


def _halves_view(g, kind, shard_shape):
    r, c = shard_shape
    if kind == "col":
        return g.reshape(1, 2, r // 2, 4 * c)
    if kind == "stack":
        g = g.reshape(r, 4, c).transpose(1, 0, 2)
    return g.reshape(4, 2, r // 2, c)


def _reduce_scatter(grads, shard_shapes):
    names = list(BIG)
    kinds = [_kind(n, shard_shapes[n]) for n in names]
    views = [_halves_view(grads[n], k, shard_shapes[n]) for n, k in zip(names, kinds)]
    c_idx = lax.axis_index("c").astype(jnp.int32).reshape(1)
    got = _send_other_halves(views)
    pairs = [_pair_sum(v, g, c_idx, f"rs_pair_{n}") for n, v, g in zip(names, views, got)]
    pieces = _exchange_chips(pairs, kinds, [shard_shapes[n][1] for n in names])
    halves = [_sum_slots(p_, f"rs_sum_{n}") for n, p_ in zip(names, pieces)]
    joined = _join_halves(halves)
    return {n: j.reshape(shard_shapes[n]) for n, j in zip(names, joined)}


def _pack_small(vals):
    return jnp.concatenate([_pad_to(vals[n].reshape(1, -1), _round_up(vals[n].size, LANES), 1) for n in SMALL], axis=1)


def _unpack_small(vec, shapes):
    out, off = {}, 0
    for n in SMALL:
        size = int(np.prod(shapes[n]))
        out[n] = vec[:, off:off + size].reshape(shapes[n])
        off += _round_up(size, LANES)
    return out


def _ffn_fwd(x, g, w_gu, w_d, tf, tag):
    n = _rms_fwd(x, g, f"{tag}_norm")
    ab = _mm(n, w_gu, "nn", BF16, f"{tag}_gate_up")
    act = _silu_mul(ab, tf, f"{tag}_act")
    out = _mm(act, w_d, "nn", F32, f"{tag}_down", res=x, alpha=0.5)
    return out, (n, ab, act)


def _ffn_bwd(dout, x, g, w_gu, w_d, saved, tf, tag):
    n, ab, act = saved
    F = act.shape[1]
    dout_b = dout.astype(BF16)
    d_wd = _mm(act, dout_b, "tn", F32, f"{tag}_d_wdown", alpha=0.5)
    dact = _mm(dout_b, w_d, "nt", BF16, f"{tag}_d_act", alpha=0.5)
    da, db = _swiglu_bwd(dact, ab, tf, f"{tag}_d_gate_up")
    d_wg = _mm(n, da, "tn", F32, f"{tag}_d_wgate")
    d_wu = _mm(n, db, "tn", F32, f"{tag}_d_wup")
    dn = _mm(da, w_gu, "nt", F32, f"{tag}_d_norm", a2=db, b2=w_gu, b2_k_offset=F)
    dx, dg = _rms_bwd_call(x, g, dn, dout, f"{tag}_d_x")
    return dx, dg, d_wg, d_wu, d_wd


def _local_step(x, p, pos_b, target, W, Gn, dims):
    T, D = x.shape
    H, HD, QL, KVL, LP, tf = dims["H"], dims["HD"], dims["QL"], dims["KVL"], dims["LP"], dims["tf"]
    Wd = HD * DIL_HEAD
    scale_mla, scale_dil = MLA_QK ** -0.5, DIL_HEAD ** -0.5
    kr_col = (QL + KVL) // LANES
    tab_mla = tuple(_rope_tables(pos_b, MLA_ROPE, "rope_tab_mla"))
    tab_dil = tuple(_rope_tables(pos_b, DIL_ROT, "rope_tab_dil"))

    x1, ffn1 = _ffn_fwd(x, Gn["g_ffn1"], W["w1_gu"], W["w1_down"], tf, "ffn1")
    h = _rms_fwd(x1, Gn["g_mix"], "mix_norm")
    lat = _mm(h, W["w_lat"], "nn", F32, "proj_lat")
    pd = _mm(h, W["w_dil"], "nn", BF16, "proj_dil")
    pg = _mm(h, W["w_gin"], "nn", BF16, "proj_gate")

    cq, ckv = _lat_fwd(lat, Gn["g_cq"], Gn["g_ckv"], "lat_norm")
    q_raw = _mm(cq, W["w_uq"], "nn", F32, "mla_q_up")
    kv = _mm(ckv, W["w_ukv"], "nn", F32, "mla_kv_up")
    q = _mla_q_prep(q_raw, Gn["g_q_mla"], tab_mla, H, scale_mla, "mla_q_prep")
    k, v = _mla_k_prep(kv, lat, kr_col, Gn["g_k_mla"], tab_mla, H, "mla_k_prep")
    o_mla, lse_mla = _mla_fwd(q, k, v, H, "mla_attn")

    qd, kd = _dil_prep(pd, Gn["g_q_dil"], Gn["g_k_dil"], tab_dil, HD, scale_dil, "dil_prep")
    og, lg = [], []
    for grp, (win, dil) in enumerate(DIL_GROUPS):
        o_, l_ = _dil_fwd(qd, kd, pd, grp, dil, HD, f"dil_attn{grp}")
        og.append(o_)
        lg.append(l_)
    o_dil, lse_dil = _dil_merge(og, lg, "dil_merge")

    bm = _mm(o_mla, W["w_br_mla"], "nn", F32, "branch_mla")
    bd = _mm(o_dil, W["w_br_dil"], "nn", F32, "branch_dil")
    merged = _gate_merge(pg, bm, bd, "gate_merge")
    x2 = _mm(merged, W["w_o"], "nn", F32, "out_proj", res=x1)

    x3, ffn2 = _ffn_fwd(x2, Gn["g_ffn2"], W["w2_gu"], W["w2_down"], tf, "ffn2")
    n4 = _rms_fwd(x3, Gn["g_ple"], "ple_norm")
    zg = _mm(n4, W["w_ple_gate"], "nn", F32, "ple_gate")
    p_b = p.astype(BF16)
    pp = _mm(p_b, W["w_ple_proj"], "nn", F32, "ple_proj")
    dy, dpp, dzg, loss = _ple_loss(x3, zg, pp, target, "ple_loss")

    gw, gg = {}, {}
    gw["w_ple_proj"] = _mm(p_b, dpp, "tn", F32, "d_w_ple_proj")
    gw["w_ple_gate"] = _mm(n4, dzg, "tn", F32, "d_w_ple_gate")
    dn4 = _mm(dzg, W["w_ple_gate"], "nt", F32, "d_ple_norm")
    dx3, gg["g_ple"] = _rms_bwd_call(x3, Gn["g_ple"], dn4, dy, "d_x3")

    dx2, gg["g_ffn2"], gw["w2_gate"], gw["w2_up"], gw["w2_down"] = _ffn_bwd(
        dx3, x2, Gn["g_ffn2"], W["w2_gu"], W["w2_down"], ffn2, tf, "ffn2")

    dx2_b = dx2.astype(BF16)
    gw["w_o"] = _mm(merged, dx2_b, "tn", F32, "d_w_o")
    dmerged = _mm(dx2_b, W["w_o"], "nt", F32, "d_merged")
    dbm, dbd, dpg = _gate_bwd(dmerged, pg, bm, bd, "d_gate")
    gw["w_br_mla"] = _mm(o_mla, dbm, "tn", F32, "d_w_br_mla")
    gw["w_br_dil"] = _mm(o_dil, dbd, "tn", F32, "d_w_br_dil")
    do_mla = _mm(dbm, W["w_br_mla"], "nt", BF16, "d_o_mla")
    do_dil = _mm(dbd, W["w_br_dil"], "nt", BF16, "d_o_dil")

    dh = _mm(dpg, W["w_gin"], "nt", F32, "d_h_gate")
    gw["w_gin"] = _mm(h, dpg, "tn", F32, "d_w_gin")
    gq_d, gk_d = Gn["g_q_dil"], Gn["g_k_dil"]
    d_wdil, dgq_d, dgk_d = [], [], []
    for grp, (win, dil) in enumerate(DIL_GROUPS):
        dq_ = _dil_bwd_dq(qd, kd, pd, do_dil, o_dil, lse_dil, grp, dil, HD, f"dil_dq{grp}")
        dk_, dv_ = _dil_bwd_dkv(qd, kd, pd, do_dil, o_dil, lse_dil, grp, dil, HD, f"dil_dkv{grp}")
        dpd_g, dgq_, dgk_ = _dil_prep_bwd(dq_, dk_, dv_, pd, grp, gq_d, gk_d, tab_dil, HD, scale_dil, f"d_dil_prep{grp}")
        dgq_d.append(dgq_)
        dgk_d.append(dgk_)
        w_g = W["w_dil"][:, grp * 3 * Wd:(grp + 1) * 3 * Wd]
        dh = _mm(dpd_g, w_g, "nt", F32, f"d_h_dil{grp}", res=dh)
        d_wdil.append(_mm(h, dpd_g, "tn", F32, f"d_w_dil{grp}"))
    gw["w_dil"] = jnp.concatenate(d_wdil, axis=1)
    gg["g_q_dil"] = jnp.concatenate(dgq_d, axis=0)
    gg["g_k_dil"] = jnp.concatenate(dgk_d, axis=0)

    dq, dk, dv = _mla_bwd(q, k, v, do_mla, o_mla, lse_mla, H, "mla_bwd")
    dq_raw, gg["g_q_mla"] = _mla_q_bwd(dq, q_raw, Gn["g_q_mla"], tab_mla, H, scale_mla, "d_mla_q_prep")
    dkv, dkr, gg["g_k_mla"] = _mla_k_bwd(dk, dv, kv, lat, kr_col, Gn["g_k_mla"], tab_mla, H, "d_mla_k_prep")
    gw["w_uq"] = _mm(cq, dq_raw, "tn", F32, "d_w_uq")
    gw["w_ukv"] = _mm(ckv, dkv, "tn", F32, "d_w_ukv")
    dcq = _mm(dq_raw, W["w_uq"], "nt", F32, "d_cq")
    dckv = _mm(dkv, W["w_ukv"], "nt", F32, "d_ckv")
    dlat, gg["g_cq"], gg["g_ckv"] = _lat_bwd(dcq, dckv, dkr, lat, Gn["g_cq"], Gn["g_ckv"], "d_lat_norm")
    dh = _mm(dlat, W["w_lat"], "nt", F32, "d_h_lat", res=dh)
    gw["w_lat"] = _mm(h, dlat, "tn", F32, "d_w_lat")

    dx1, gg["g_mix"] = _rms_bwd_call(x1, Gn["g_mix"], dh, dx2, "d_x1")
    dx, gg["g_ffn1"], gw["w1_gate"], gw["w1_up"], gw["w1_down"] = _ffn_bwd(
        dx1, x, Gn["g_ffn1"], W["w1_gu"], W["w1_down"], ffn1, tf, "ffn1")
    return loss, dx, gw, gg


def _compute_layout(full, dims):
    H, QL, KVL, LP, Wd = dims["H"], dims["QL"], dims["KVL"], dims["LP"], dims["HD"] * DIL_HEAD
    off_dil = QL + KVL + MLA_ROPE
    off_gate = off_dil + 3 * len(DIL_GROUPS) * Wd
    w_in = full["w_in"]
    out = {n: full[n] for n in ("w1_gu", "w1_down", "w2_gu", "w2_down", "w_ukv", "w_br_mla", "w_br_dil", "w_o",
                                "w_ple_gate", "w_ple_proj")}
    out["w_lat"] = _pad_to(w_in[:, :off_dil], LP, 1)
    out["w_dil"] = w_in[:, off_dil:off_gate]
    out["w_gin"] = w_in[:, off_gate:]
    out["w_uq"] = _pad_to(full["w_uq"].reshape(QL, H, MLA_QK), MLA_QK_PAD, 2).reshape(QL, H * MLA_QK_PAD)
    return out


def _natural_grads(gw, dims):
    H, QL, KVL = dims["H"], dims["QL"], dims["KVL"]
    off_dil = QL + KVL + MLA_ROPE
    out = {n: gw[n] for n in BIG if n not in ("w_in", "w_uq")}
    out["w_in"] = jnp.concatenate([gw["w_lat"][:, :off_dil], gw["w_dil"], gw["w_gin"]], axis=1)
    out["w_uq"] = gw["w_uq"].reshape(QL, H, MLA_QK_PAD)[:, :, :MLA_QK].reshape(QL, H * MLA_QK)
    return out


def _step(x, p, positions, loss_target, w, m, v):
    T, D = x.shape[1], x.shape[2]
    QL, KVL = w["g_cq"].shape[1], w["g_ckv"].shape[1]
    dims = {
        "H": 4 * w["w_uq"].shape[2] // MLA_QK, "HD": w["w_br_dil"].shape[1] // DIL_HEAD, "QL": QL, "KVL": KVL,
        "LP": _round_up(QL + KVL + MLA_ROPE, LANES), "tf": _pick(4 * w["w1_gate"].shape[2], 512),
    }
    shard_shapes = {n: w[n].shape[1:] for n in BIG}
    small_shapes = {n: w[n].shape for n in SMALL}

    gathered = _all_gather_weights({n: w[n][0].astype(BF16) for n in BIG})
    full = {out_name: _natural(gathered[out_name], _kind(names[0], shard_shapes[names[0]]))
            for out_name, names in GATHER_PLAN}
    W = _compute_layout(full, dims)
    Gn = {n: w[n] for n in SMALL}
    Gn["g_q_mla"] = _pad_to(Gn["g_q_mla"], MLA_QK_PAD, 1)
    Gn["g_k_mla"] = _pad_to(Gn["g_k_mla"], MLA_QK_PAD, 1)
    Gn["g_q_dil"] = Gn["g_q_dil"].reshape(len(DIL_GROUPS), 1, DIL_HEAD)
    Gn["g_k_dil"] = Gn["g_k_dil"].reshape(len(DIL_GROUPS), 1, DIL_HEAD)

    pos_b = jnp.broadcast_to(positions.astype(F32).reshape(T, 1), (T, LANES))
    loss, dx, gw, gg = _local_step(x[0], p[0, 0], pos_b, loss_target[0], W, Gn, dims)
    loss = lax.psum(loss[0, 0], ("x", "y", "c"))

    reduced = _reduce_scatter(_natural_grads(gw, dims), shard_shapes)
    outs = {"grad": {}, "delta": {}, "m": {}, "v": {}}
    for n in BIG:
        d_, m_, v_ = _adamw(w[n][0], reduced[n], m[n][0], v[n][0], f"adamw_{n}")
        for kind, a in (("grad", reduced[n]), ("delta", d_), ("m", m_), ("v", v_)):
            outs[kind][n] = a.reshape((1,) + a.shape)

    gg["g_q_mla"] = gg["g_q_mla"][:, :MLA_QK]
    gg["g_k_mla"] = gg["g_k_mla"][:, :MLA_QK]
    g_small = _all_reduce_small(_pack_small(gg))
    d_s, m_s, v_s = _adamw(_pack_small({n: w[n] for n in SMALL}), g_small, _pack_small({n: m[n] for n in SMALL}),
                           _pack_small({n: v[n] for n in SMALL}), "adamw_gains")
    for kind, buf in (("grad", g_small), ("delta", d_s), ("m", m_s), ("v", v_s)):
        outs[kind].update(_unpack_small(buf, small_shapes))

    grad_x = dx.reshape(1, T, D)
    return (loss, grad_x, *[outs["grad"][n] for n in WEIGHTS], *[outs["delta"][n] for n in WEIGHTS],
            *[outs["m"][n] for n in WEIGHTS], *[outs["v"][n] for n in WEIGHTS])


def kernel(x, p, positions, g_ffn1, w1_gate, w1_up, w1_down, g_mix, w_in, g_cq, w_uq, g_ckv, w_ukv, g_q_mla, g_k_mla, g_q_dil, g_k_dil, w_br_mla, w_br_dil, w_o, g_ffn2, w2_gate, w2_up, w2_down, g_ple, w_ple_gate, w_ple_proj, loss_target, m_g_ffn1, m_w1_gate, m_w1_up, m_w1_down, m_g_mix, m_w_in, m_g_cq, m_w_uq, m_g_ckv, m_w_ukv, m_g_q_mla, m_g_k_mla, m_g_q_dil, m_g_k_dil, m_w_br_mla, m_w_br_dil, m_w_o, m_g_ffn2, m_w2_gate, m_w2_up, m_w2_down, m_g_ple, m_w_ple_gate, m_w_ple_proj, v_g_ffn1, v_w1_gate, v_w1_up, v_w1_down, v_g_mix, v_w_in, v_g_cq, v_w_uq, v_g_ckv, v_w_ukv, v_g_q_mla, v_g_k_mla, v_g_q_dil, v_g_k_dil, v_w_br_mla, v_w_br_dil, v_w_o, v_g_ffn2, v_w2_gate, v_w2_up, v_w2_down, v_g_ple, v_w_ple_gate, v_w_ple_proj):
    args = locals()
    w = {n: args[n] for n in WEIGHTS}
    m = {n: args["m_" + n] for n in WEIGHTS}
    v = {n: args["v_" + n] for n in WEIGHTS}
    return _step(x, p, positions, loss_target, w, m, v)
```

```python
import functools

import numpy as np
import jax
import jax.numpy as jnp
from jax import lax
from jax.experimental import pallas as pl
from jax.experimental.pallas import tpu as pltpu

F32 = jnp.float32
BF16 = jnp.bfloat16
MESH_ID = pl.DeviceIdType.MESH

MLA_NOPE = 128
MLA_ROPE = 64
MLA_V = 128
MLA_QK = MLA_NOPE + MLA_ROPE
MLA_QK_PAD = 256
DIL_GROUPS = ((128, 1), (512, 4), (2048, 16))
DIL_HEAD = 128
DIL_ROT = DIL_HEAD // 4
DIL_BLOCK = 128
ROPE_THETA = 500000.0
EPS = 1e-6
NEG = -1e30
ADAM_LR = 0.001
ADAM_B1 = 0.9
ADAM_B2 = 0.999
ADAM_EPS = 1e-08
ADAM_WD = 0.01
ADAM_STEP = 10

LANES = 128
VMEM_LIMIT_BYTES = 56 * 1024 * 1024

BIG = ("w1_gate", "w1_up", "w1_down", "w_in", "w_uq", "w_ukv", "w_br_mla", "w_br_dil", "w_o",
       "w2_gate", "w2_up", "w2_down", "w_ple_gate", "w_ple_proj")
GATHER_PLAN = (("w1_gu", ("w1_gate", "w1_up")), ("w1_down", ("w1_down",)), ("w_in", ("w_in",)), ("w_uq", ("w_uq",)),
               ("w_ukv", ("w_ukv",)), ("w_br_mla", ("w_br_mla",)), ("w_br_dil", ("w_br_dil",)), ("w_o", ("w_o",)),
               ("w2_gu", ("w2_gate", "w2_up")), ("w2_down", ("w2_down",)), ("w_ple_gate", ("w_ple_gate",)),
               ("w_ple_proj", ("w_ple_proj",)))
ROW_SHARDED = ("w1_down", "w_o", "w2_down", "w_ple_gate")
SMALL = ("g_ffn1", "g_mix", "g_cq", "g_ckv", "g_q_mla", "g_k_mla", "g_q_dil", "g_k_dil", "g_ffn2", "g_ple")
WEIGHTS = ("g_ffn1", "w1_gate", "w1_up", "w1_down", "g_mix", "w_in", "g_cq", "w_uq", "g_ckv", "w_ukv", "g_q_mla",
           "g_k_mla", "g_q_dil", "g_k_dil", "w_br_mla", "w_br_dil", "w_o", "g_ffn2", "w2_gate", "w2_up", "w2_down",
           "g_ple", "w_ple_gate", "w_ple_proj")


def _pick(n, target, align=LANES):
    if n <= target:
        return n
    t = (target // align) * align
    while t >= align:
        if n % t == 0:
            return t
        t -= align
    return n


def _params(n_axes):
    return pltpu.CompilerParams(dimension_semantics=("arbitrary",) * n_axes, vmem_limit_bytes=VMEM_LIMIT_BYTES)


def _sigmoid(x):
    return 1.0 / (1.0 + jnp.exp(-x))


ANY = pl.BlockSpec(memory_space=pl.ANY)


class _Side:
    def __init__(self, arrays, out_shapes, aliases, sem_shapes, phases):
        self.arrays, self.out_shapes, self.aliases = list(arrays), list(out_shapes), dict(aliases)
        self.sem_shapes, self.phases = list(sem_shapes), list(phases)

    def start(self, p, ins, outs, sems):
        for cp in self.phases[p][0](ins, outs, sems):
            cp.start()

    def wait(self, p, ins, outs, sems):
        for cp in self.phases[p][1](ins, outs, sems):
            cp.wait_recv()
        for cp in self.phases[p][0](ins, outs, sems):
            cp.wait_send()

    def run(self, step, n_steps, ins, outs, sems):
        n_ph = len(self.phases)
        if n_steps <= n_ph:
            @pl.when(step == n_steps - 1)
            def _():
                for p in range(n_ph):
                    self.start(p, ins, outs, sems)
                    self.wait(p, ins, outs, sems)
            return
        for p in range(n_ph):
            @pl.when(step == p * (n_steps - 1) // n_ph)
            def _(p=p):
                if p > 0:
                    self.wait(p - 1, ins, outs, sems)
                self.start(p, ins, outs, sems)

        @pl.when(step == n_steps - 1)
        def _():
            self.wait(n_ph - 1, ins, outs, sems)


def _merge_sides(sides):
    arrays, out_shapes, aliases, sem_shapes, spans = [], [], {}, [], []
    for s in sides:
        assert len(s.phases) == 1
        spans.append((len(arrays), len(out_shapes), len(sem_shapes), s))
        aliases.update({len(arrays) + i: len(out_shapes) + o for i, o in s.aliases.items()})
        arrays += s.arrays
        out_shapes += s.out_shapes
        sem_shapes += s.sem_shapes

    def part(which):
        def fn(ins, outs, sems):
            cps = []
            for a0, o0, s0, s in spans:
                cps += s.phases[0][which](ins[a0:a0 + len(s.arrays)], outs[o0:o0 + len(s.out_shapes)],
                                          sems[s0:s0 + len(s.sem_shapes)])
            return cps
        return fn

    return _Side(arrays, out_shapes, aliases, sem_shapes, [(part(0), part(1))])


def _run_side(side, name):
    n_in, n_out = len(side.arrays), len(side.out_shapes)

    def body(*refs):
        ins, outs, sems = refs[:n_in], refs[n_in:n_in + n_out], refs[n_in + n_out:]
        for p in range(len(side.phases)):
            side.start(p, ins, outs, sems)
            side.wait(p, ins, outs, sems)

    return pl.pallas_call(body, name=name, out_shape=side.out_shapes, in_specs=[ANY] * n_in, out_specs=[ANY] * n_out,
                          scratch_shapes=side.sem_shapes, input_output_aliases=side.aliases)(*side.arrays)


def _mm(a, b, mode, out_dtype, name, res=None, alpha=1.0, a2=None, b2=None, b2_k_offset=0, side=None,
        tm=1024, tn=1536, tk=1024):
    if mode == "nn":
        (M, K), (K2, N) = a.shape, b.shape
    elif mode == "nt":
        (M, K), (N, K2) = a.shape, b.shape
    else:
        (K, M), (K2, N) = a.shape, b.shape
    assert K == K2 or (mode == "nt" and K2 > K), (name, a.shape, b.shape)
    assert a.dtype == BF16 and b.dtype == BF16, name
    tm, tn, tk = _pick(M, tm), _pick(N, tn), _pick(K, tk)
    nk = K // tk
    assert b2_k_offset % tk == 0 and (b2_k_offset == 0 or mode == "nt"), name
    k_off2 = b2_k_offset // tk
    if mode == "nn":
        a_spec = pl.BlockSpec((tm, tk), lambda i, j, k: (i, k))
        b_spec = pl.BlockSpec((tk, tn), lambda i, j, k: (k, j))
        dims = (((1,), (0,)), ((), ()))
    elif mode == "nt":
        a_spec = pl.BlockSpec((tm, tk), lambda i, j, k: (i, k))
        b_spec = pl.BlockSpec((tn, tk), lambda i, j, k: (j, k))
        b2_spec = pl.BlockSpec((tn, tk), lambda i, j, k: (j, k + k_off2))
        dims = (((1,), (1,)), ((), ()))
    else:
        a_spec = pl.BlockSpec((tk, tm), lambda i, j, k: (k, i))
        b_spec = pl.BlockSpec((tk, tn), lambda i, j, k: (k, j))
        dims = (((0,), (0,)), ((), ()))
    o_spec = pl.BlockSpec((tm, tn), lambda i, j, k: (i, j))
    has_res = res is not None
    n_pairs = 1 if a2 is None else 2
    n_main = 2 * n_pairs + int(has_res)
    n_side_in = len(side.arrays) if side else 0
    n_side_out = len(side.out_shapes) if side else 0
    n_acc = 1 if nk > 1 else 0
    gi, gj = M // tm, N // tn
    n_steps = gi * gj * nk

    def kern(*refs):
        r_ref = refs[2 * n_pairs] if has_res else None
        o_ref = refs[n_main + n_side_in]
        if side:
            step = (pl.program_id(0) * gj + pl.program_id(1)) * nk + pl.program_id(2)
            side.run(step, n_steps, refs[n_main:n_main + n_side_in],
                     refs[n_main + n_side_in + 1:n_main + n_side_in + 1 + n_side_out],
                     refs[n_main + n_side_in + 1 + n_side_out + n_acc:])
        part = lax.dot_general(refs[0][...], refs[1][...], dims, preferred_element_type=F32)
        if n_pairs == 2:
            part = part + lax.dot_general(refs[2][...], refs[3][...], dims, preferred_element_type=F32)

        def finish(r):
            if alpha != 1.0:
                r = r * alpha
            if has_res:
                r = r_ref[...] + r
            o_ref[...] = r.astype(o_ref.dtype)

        if nk == 1:
            finish(part)
            return
        acc_ref = refs[n_main + n_side_in + 1 + n_side_out]
        k = pl.program_id(2)

        @pl.when(k == 0)
        def _():
            acc_ref[...] = part

        @pl.when(k > 0)
        def _():
            acc_ref[...] += part

        @pl.when(k == nk - 1)
        def _():
            finish(acc_ref[...])

    ins = (a, b) + ((a2, b2) if n_pairs == 2 else ()) + ((res,) if has_res else ())
    in_specs = [a_spec, b_spec] + ([a_spec, b2_spec if mode == "nt" else b_spec] if n_pairs == 2 else [])
    in_specs += [o_spec] if has_res else []
    out_shape = jax.ShapeDtypeStruct((M, N), out_dtype)
    scratch = [pltpu.VMEM((tm, tn), F32)] if nk > 1 else []
    if not side:
        return pl.pallas_call(kern, name=name, grid=(gi, gj, nk), in_specs=in_specs, out_specs=o_spec,
                              out_shape=out_shape, scratch_shapes=scratch, compiler_params=_params(3))(*ins)
    outs = pl.pallas_call(
        kern, name=name, grid=(gi, gj, nk), in_specs=in_specs + [ANY] * n_side_in,
        out_specs=[o_spec] + [ANY] * n_side_out, out_shape=[out_shape] + list(side.out_shapes),
        scratch_shapes=scratch + list(side.sem_shapes),
        input_output_aliases={n_main + i: 1 + o for i, o in side.aliases.items()},
        compiler_params=_params(3))(*ins, *side.arrays)
    return outs[0], list(outs[1:])


def _vcall(body, grid, ins, in_specs, out_shapes, out_specs, name, n_inner_acc=0, n_acc=0):
    n_in, n_out = len(ins), len(out_shapes)
    n_plain = n_out - n_acc - n_inner_acc

    def kern(*refs):
        vals = body(*[r[...] for r in refs[:n_in]])
        if not isinstance(vals, (tuple, list)):
            vals = (vals,)
        out_refs = refs[n_in:]
        inner_first = pl.program_id(len(grid) - 1) == 0
        first = inner_first
        for ax in range(len(grid) - 1):
            first = jnp.logical_and(first, pl.program_id(ax) == 0)
        for idx, (r, v) in enumerate(zip(out_refs, vals)):
            if idx < n_plain:
                r[...] = v.astype(r.dtype)
                continue
            start = inner_first if idx < n_plain + n_inner_acc else first

            @pl.when(start)
            def _(r=r, v=v):
                r[...] = v.astype(r.dtype)

            @pl.when(jnp.logical_not(start))
            def _(r=r, v=v):
                r[...] += v.astype(r.dtype)

    out = pl.pallas_call(kern, name=name, grid=grid, in_specs=in_specs, out_specs=out_specs, out_shape=out_shapes,
                         compiler_params=_params(len(grid)))(*ins)
    return out


def _rows(tm, c):
    return pl.BlockSpec((tm, c), lambda i: (i, 0))


def _vec(c):
    return pl.BlockSpec((1, c), lambda i: (0, 0))


def _sds(shape, dtype):
    return jax.ShapeDtypeStruct(shape, dtype)


def _rstd(x, c):
    return lax.rsqrt(jnp.sum(x * x, axis=-1, keepdims=True) * (1.0 / c) + EPS)


def _rms_bwd(xh, r, g, dn, c):
    u = dn * g
    dx = r * (u - xh * (jnp.sum(xh * u, axis=-1, keepdims=True) * (1.0 / c)))
    return dx, jnp.sum(dn * xh, axis=0, keepdims=True)


def _rope(t, c, sa, sb, half):
    return t * c + pltpu.roll(t, LANES - half, 1) * sa + pltpu.roll(t, half, 1) * sb


def _rope_t(d, c, sa, sb, half):
    return d * c + pltpu.roll(d * sa, half, 1) + pltpu.roll(d * sb, LANES - half, 1)


def _rope_tables(pos_b, rd, name):
    T = pos_b.shape[0]
    half = rd // 2
    inv = ROPE_THETA ** (-jnp.arange(half, dtype=F32) * 2.0 / rd)
    inv_full = jnp.concatenate([inv, inv, jnp.zeros((LANES - rd,), F32)]).reshape(1, LANES)
    lane = np.arange(LANES)
    ma = jnp.asarray((lane < half).astype(np.float32)).reshape(1, LANES)
    mb = jnp.asarray(((lane >= half) & (lane < rd)).astype(np.float32)).reshape(1, LANES)
    tm = _pick(T, 1024, 8)

    def body(pos, invf, a, b):
        ang = pos * invf
        c, s = jnp.cos(ang), jnp.sin(ang)
        inside = a + b
        return c * inside + (1.0 - inside), -s * a, s * b

    return _vcall(body, (T // tm,), (pos_b, inv_full, ma, mb), [_rows(tm, LANES)] + [_vec(LANES)] * 3,
                  [_sds((T, LANES), F32)] * 3, [_rows(tm, LANES)] * 3, name)


def _rms_fwd(x, g, name):
    T, C = x.shape
    tm = _pick(T, 512, 8)

    def body(xv, gv):
        return xv * _rstd(xv, C) * gv

    return _vcall(body, (T // tm,), (x, g), [_rows(tm, C), _vec(C)], [_sds((T, C), BF16)], [_rows(tm, C)], name)[0]


def _rms_bwd_call(x, g, dn, dres, name):
    T, C = x.shape
    tm = _pick(T, 256, 8)

    def body(xv, gv, dnv, drv):
        r = _rstd(xv, C)
        dx, dg = _rms_bwd(xv * r, r, gv, dnv.astype(F32), C)
        return drv + dx, dg

    return _vcall(body, (T // tm,), (x, g, dn, dres), [_rows(tm, C), _vec(C), _rows(tm, C), _rows(tm, C)],
                  [_sds((T, C), F32), _sds((1, C), F32)], [_rows(tm, C), _vec(C)], name, n_acc=1)


def _silu_mul(ab, tf, name):
    T, F2 = ab.shape
    F = F2 // 2
    nf = F // tf
    tm = _pick(T, 512, 8)

    def body(a, b):
        a, b = a.astype(F32), b.astype(F32)
        return a * _sigmoid(a) * b

    tile = pl.BlockSpec((tm, tf), lambda i, j: (i, j))
    return _vcall(body, (T // tm, nf), (ab, ab), [tile, pl.BlockSpec((tm, tf), lambda i, j: (i, j + nf))],
                  [_sds((T, F), BF16)], [tile], name)[0]


def _swiglu_bwd(dact, ab, tf, name):
    T, F = dact.shape
    nf = F // tf
    tm = _pick(T, 512, 8)

    def body(dv, a, b):
        d, a, b = dv.astype(F32), a.astype(F32), b.astype(F32)
        sg = _sigmoid(a)
        return d * b * (sg * (1.0 + a * (1.0 - sg))), d * (a * sg)

    tile = pl.BlockSpec((tm, tf), lambda i, j: (i, j))
    return _vcall(body, (T // tm, nf), (dact, ab, ab), [tile, tile, pl.BlockSpec((tm, tf), lambda i, j: (i, j + nf))],
                  [_sds((T, F), BF16)] * 2, [tile, tile], name)


def _lat_fwd(lat, g_cq, g_ckv, name):
    T, LP = lat.shape
    QL, KVL = g_cq.shape[1], g_ckv.shape[1]
    tm = _pick(T, 512, 8)

    def body(v, gq, gk):
        xq, xk = v[:, :QL], v[:, QL:QL + KVL]
        return xq * _rstd(xq, QL) * gq, xk * _rstd(xk, KVL) * gk

    return _vcall(body, (T // tm,), (lat, g_cq, g_ckv), [_rows(tm, LP), _vec(QL), _vec(KVL)],
                  [_sds((T, QL), BF16), _sds((T, KVL), BF16)], [_rows(tm, QL), _rows(tm, KVL)], name)


def _lat_bwd(dcq, dckv, dkr, lat, g_cq, g_ckv, name):
    T, LP = lat.shape
    QL, KVL = g_cq.shape[1], g_ckv.shape[1]
    tm = _pick(T, 512, 8)

    def body(dq, dk, dr, v, gq, gk):
        xq, xk = v[:, :QL], v[:, QL:QL + KVL]
        rq, rk = _rstd(xq, QL), _rstd(xk, KVL)
        dxq, dgq = _rms_bwd(xq * rq, rq, gq, dq, QL)
        dxk, dgk = _rms_bwd(xk * rk, rk, gk, dk, KVL)
        return jnp.concatenate([dxq, dxk, dr], axis=1), dgq, dgk

    return _vcall(body, (T // tm,), (dcq, dckv, dkr, lat, g_cq, g_ckv),
                  [_rows(tm, QL), _rows(tm, KVL), _rows(tm, LANES), _rows(tm, LP), _vec(QL), _vec(KVL)],
                  [_sds((T, LP), BF16), _sds((1, QL), F32), _sds((1, KVL), F32)],
                  [_rows(tm, LP), _vec(QL), _vec(KVL)], name, n_acc=2)


def _head_spec(tm, w):
    return pl.BlockSpec((tm, w), lambda i, h: (i, h))


def _row2(tm, w, col=0):
    return pl.BlockSpec((tm, w), lambda i, h: (i, col))


def _vec2(w):
    return pl.BlockSpec((1, w), lambda i, h: (0, 0))


def _mla_q_prep(q_raw, g_q, tabs, H, scale, name):
    T = q_raw.shape[0]
    tm = _pick(T, 512, 8)
    half = MLA_ROPE // 2

    def body(x, g, c, sa, sb):
        n = x * _rstd(x, MLA_QK) * g
        return jnp.concatenate([n[:, :LANES], _rope(n[:, LANES:], c, sa, sb, half)], axis=1) * scale

    return _vcall(body, (T // tm, H), (q_raw, g_q) + tabs,
                  [_head_spec(tm, MLA_QK_PAD), _vec2(MLA_QK_PAD)] + [_row2(tm, LANES)] * 3,
                  [_sds((T, H * MLA_QK_PAD), BF16)], [_head_spec(tm, MLA_QK_PAD)], name)[0]


def _mla_q_bwd(dq, q_raw, g_q, tabs, H, scale, name):
    T = q_raw.shape[0]
    tm = _pick(T, 512, 8)
    half = MLA_ROPE // 2

    def body(d, x, g, c, sa, sb):
        r = _rstd(x, MLA_QK)
        d = d * scale
        dn = jnp.concatenate([d[:, :LANES], _rope_t(d[:, LANES:], c, sa, sb, half)], axis=1)
        return _rms_bwd(x * r, r, g, dn, MLA_QK)

    return _vcall(body, (T // tm, H), (dq, q_raw, g_q) + tabs,
                  [_head_spec(tm, MLA_QK_PAD), _head_spec(tm, MLA_QK_PAD), _vec2(MLA_QK_PAD)] + [_row2(tm, LANES)] * 3,
                  [_sds((T, H * MLA_QK_PAD), BF16), _sds((1, MLA_QK_PAD), F32)],
                  [_head_spec(tm, MLA_QK_PAD), _vec2(MLA_QK_PAD)], name, n_acc=1)


def _mla_k_prep(kv, lat, kr_col, g_k, tabs, H, name):
    T = kv.shape[0]
    tm = _pick(T, 512, 8)
    half = MLA_ROPE // 2

    def body(x, kr, g, c, sa, sb):
        kn = x[:, :LANES]
        r = lax.rsqrt((jnp.sum(kn * kn, axis=-1, keepdims=True) + jnp.sum(kr * kr, axis=-1, keepdims=True))
                      * (1.0 / MLA_QK) + EPS)
        k0 = kn * r * g[:, :LANES]
        k1 = _rope(kr * r * g[:, LANES:], c, sa, sb, half)
        return jnp.concatenate([k0, k1], axis=1), x[:, LANES:]

    return _vcall(body, (T // tm, H), (kv, lat, g_k) + tabs,
                  [_head_spec(tm, 2 * LANES), _row2(tm, LANES, kr_col), _vec2(MLA_QK_PAD)] + [_row2(tm, LANES)] * 3,
                  [_sds((T, H * MLA_QK_PAD), BF16), _sds((T, H * MLA_V), BF16)],
                  [_head_spec(tm, MLA_QK_PAD), _head_spec(tm, MLA_V)], name)


def _mla_k_bwd(dk, dv, kv, lat, kr_col, g_k, tabs, H, name):
    T = kv.shape[0]
    tm = _pick(T, 512, 8)
    half = MLA_ROPE // 2

    def body(d, dvv, x, kr, g, c, sa, sb):
        xx = jnp.concatenate([x[:, :LANES], kr], axis=1)
        r = _rstd(xx, MLA_QK)
        dn = jnp.concatenate([d[:, :LANES], _rope_t(d[:, LANES:], c, sa, sb, half)], axis=1)
        dx, dg = _rms_bwd(xx * r, r, g, dn, MLA_QK)
        return jnp.concatenate([dx[:, :LANES], dvv], axis=1), dx[:, LANES:], dg

    return _vcall(body, (T // tm, H), (dk, dv, kv, lat, g_k) + tabs,
                  [_head_spec(tm, MLA_QK_PAD), _head_spec(tm, MLA_V), _head_spec(tm, 2 * LANES),
                   _row2(tm, LANES, kr_col), _vec2(MLA_QK_PAD)] + [_row2(tm, LANES)] * 3,
                  [_sds((T, H * 2 * LANES), BF16), _sds((T, LANES), F32), _sds((1, MLA_QK_PAD), F32)],
                  [_head_spec(tm, 2 * LANES), _row2(tm, LANES), _vec2(MLA_QK_PAD)], name, n_inner_acc=1, n_acc=1)


def _dil_prep(pd, g_q, g_k, tabs, HD, scale, name):
    T = pd.shape[0]
    W = HD * DIL_HEAD
    G = len(DIL_GROUPS)
    tm = _pick(T, 256, 8)
    half = DIL_ROT // 2

    def body(xq, xk, gq, gk, c, sa, sb):
        outs = []
        for x, g, s in ((xq, gq, scale), (xk, gk, 1.0)):
            heads = []
            for h in range(HD):
                xs = x[:, h * DIL_HEAD:(h + 1) * DIL_HEAD].astype(F32)
                n = _rope(xs * _rstd(xs, DIL_HEAD) * g, c, sa, sb, half)
                heads.append(n * s if s != 1.0 else n)
            outs.append(jnp.concatenate(heads, axis=1))
        return tuple(outs)

    gspec = pl.BlockSpec((None, 1, DIL_HEAD), lambda i, g: (g, 0, 0))
    return _vcall(body, (T // tm, G), (pd, pd, g_q, g_k) + tabs,
                  [pl.BlockSpec((tm, W), lambda i, g: (i, 3 * g)), pl.BlockSpec((tm, W), lambda i, g: (i, 3 * g + 1)),
                   gspec, gspec] + [_row2(tm, LANES)] * 3,
                  [_sds((T, G * W), BF16)] * 2, [pl.BlockSpec((tm, W), lambda i, g: (i, g))] * 2, name)


def _dil_prep_bwd(dq, dk, dv, pd, grp, g_q, g_k, tabs, HD, scale, name):
    T = pd.shape[0]
    W = HD * DIL_HEAD
    tm = _pick(T, 256, 8)
    half = DIL_ROT // 2

    def body(dqv, dkv, dvv, xq, xk, gq, gk, c, sa, sb):
        cols, dgs = [], []
        for d, x, g, s in ((dqv, xq, gq, scale), (dkv, xk, gk, 1.0)):
            heads, dg = [], None
            for h in range(HD):
                sl = slice(h * DIL_HEAD, (h + 1) * DIL_HEAD)
                xs = x[:, sl].astype(F32)
                r = _rstd(xs, DIL_HEAD)
                dh = d[:, sl] * s if s != 1.0 else d[:, sl]
                dx, dgh = _rms_bwd(xs * r, r, g, _rope_t(dh, c, sa, sb, half), DIL_HEAD)
                heads.append(dx)
                dg = dgh if dg is None else dg + dgh
            cols.append(jnp.concatenate(heads, axis=1))
            dgs.append(dg)
        return jnp.concatenate(cols + [dvv], axis=1), dgs[0], dgs[1]

    gq, gk = g_q[grp], g_k[grp]
    return _vcall(body, (T // tm,), (dq, dk, dv, pd, pd, gq, gk) + tabs,
                  [_rows(tm, W)] * 3 + [pl.BlockSpec((tm, W), lambda i: (i, 3 * grp)),
                                        pl.BlockSpec((tm, W), lambda i: (i, 3 * grp + 1)),
                                        _vec(DIL_HEAD), _vec(DIL_HEAD)] + [_rows(tm, LANES)] * 3,
                  [_sds((T, 3 * W), BF16), _sds((1, DIL_HEAD), F32), _sds((1, DIL_HEAD), F32)],
                  [_rows(tm, 3 * W), _vec(DIL_HEAD), _vec(DIL_HEAD)], name, n_acc=2)


def _dil_merge(os_, lses, name):
    T, W = os_[0].shape
    tm = _pick(T, 256, 8)

    def body(o0, o1, o2, l0, l1, l2):
        m = jnp.maximum(jnp.maximum(l0, l1), l2)
        w0, w1, w2 = jnp.exp(l0 - m), jnp.exp(l1 - m), jnp.exp(l2 - m)
        z = w0 + w1 + w2
        return (w0 * o0 + w1 * o1 + w2 * o2) / z, m + jnp.log(z)

    return _vcall(body, (T // tm,), tuple(os_) + tuple(lses), [_rows(tm, W)] * 6,
                  [_sds((T, W), BF16), _sds((T, W), F32)], [_rows(tm, W)] * 2, name)


def _gate_merge(pg, bm, bd, name):
    T, D = bm.shape
    tm = _pick(T, 256, 8)

    def body(g, m, d):
        g = g.astype(F32)
        return _sigmoid(g[:, :D]) * m + _sigmoid(g[:, D:]) * d

    return _vcall(body, (T // tm,), (pg, bm, bd), [_rows(tm, 2 * D), _rows(tm, D), _rows(tm, D)],
                  [_sds((T, D), BF16)], [_rows(tm, D)], name)[0]


def _gate_bwd(dmerged, pg, bm, bd, name):
    T, D = bm.shape
    tm = _pick(T, 256, 8)

    def body(dm, g, m, d):
        g = g.astype(F32)
        s0, s1 = _sigmoid(g[:, :D]), _sigmoid(g[:, D:])
        dpg = jnp.concatenate([dm * m * s0 * (1.0 - s0), dm * d * s1 * (1.0 - s1)], axis=1)
        return dm * s0, dm * s1, dpg

    return _vcall(body, (T // tm,), (dmerged, pg, bm, bd), [_rows(tm, D), _rows(tm, 2 * D), _rows(tm, D), _rows(tm, D)],
                  [_sds((T, D), BF16), _sds((T, D), BF16), _sds((T, 2 * D), BF16)],
                  [_rows(tm, D), _rows(tm, D), _rows(tm, 2 * D)], name)


def _ple_loss(x3, zg, pp, target, name):
    T, D = x3.shape
    tm = _pick(T, 256, 8)

    def body(x, z, p_, t):
        s = _sigmoid(z)
        e = x + s * p_ - t
        dy = e * (1.0 / D)
        part = 0.5 * jnp.sum(jnp.sum(e * e, axis=1, keepdims=True), axis=0, keepdims=True) * (1.0 / D)
        return dy, dy * s, dy * p_ * s * (1.0 - s), jnp.broadcast_to(part, (1, LANES))

    return _vcall(body, (T // tm,), (x3, zg, pp, target), [_rows(tm, D)] * 4,
                  [_sds((T, D), F32), _sds((T, D), BF16), _sds((T, D), BF16), _sds((1, LANES), F32)],
                  [_rows(tm, D)] * 3 + [_vec(LANES)], name, n_acc=1)


NT = (((1,), (1,)), ((), ()))
TN = (((0,), (0,)), ((), ()))


def _diag_mask(s):
    row = lax.broadcasted_iota(jnp.int32, s.shape, 0)
    col = lax.broadcasted_iota(jnp.int32, s.shape, 1)
    return jnp.where(col <= row, s, NEG)


def _causal_pairs(nq, key_major):
    if key_major:
        pairs = [(i, j) for j in range(nq) for i in range(j, nq)]
    else:
        pairs = [(i, j) for i in range(nq) for j in range(i + 1)]
    return (jnp.asarray([pr[0] for pr in pairs], jnp.int32), jnp.asarray([pr[1] for pr in pairs], jnp.int32))


def _mla_fwd(q, k, v, H, name):
    T = q.shape[0]
    tq = _pick(T, 512)
    nq = T // tq
    hb = 2 if H % 2 == 0 else 1
    qi_tab, kj_tab = _causal_pairs(nq, key_major=False)

    def kern(qi_ref, kj_ref, q_ref, k_ref, v_ref, o_ref, lse_ref, m_sc, l_sc, acc_sc):
        t = pl.program_id(1)
        qi, kj = qi_ref[t], kj_ref[t]

        @pl.when(kj == 0)
        def _():
            m_sc[...] = jnp.full_like(m_sc, NEG)
            l_sc[...] = jnp.zeros_like(l_sc)
            acc_sc[...] = jnp.zeros_like(acc_sc)

        def tile(diagonal):
            for hh in range(hb):
                qs = slice(hh * MLA_QK_PAD, (hh + 1) * MLA_QK_PAD)
                vs = slice(hh * MLA_V, (hh + 1) * MLA_V)
                s = lax.dot_general(q_ref[:, qs], k_ref[:, qs], NT, preferred_element_type=F32)
                if diagonal:
                    s = _diag_mask(s)
                m_prev = m_sc[hh]
                m_new = jnp.maximum(m_prev, jnp.max(s, axis=1, keepdims=True))
                alpha = jnp.exp(m_prev - m_new)
                p = jnp.exp(s - m_new)
                l_new = alpha * l_sc[hh] + jnp.sum(p, axis=1, keepdims=True)
                acc = alpha * acc_sc[hh] + jnp.dot(p.astype(BF16), v_ref[:, vs], preferred_element_type=F32)
                if diagonal:
                    o_ref[:, vs] = (acc / l_new).astype(o_ref.dtype)
                    lse_ref[:, hh * LANES:(hh + 1) * LANES] = jnp.broadcast_to(m_new + jnp.log(l_new), (tq, LANES))
                else:
                    m_sc[hh] = m_new
                    l_sc[hh] = l_new
                    acc_sc[hh] = acc

        @pl.when(kj < qi)
        def _():
            tile(False)

        @pl.when(kj == qi)
        def _():
            tile(True)

    qspec = lambda w: pl.BlockSpec((tq, hb * w), lambda h, t, qi_ref, kj_ref: (qi_ref[t], h))
    kspec = lambda w: pl.BlockSpec((tq, hb * w), lambda h, t, qi_ref, kj_ref: (kj_ref[t], h))
    grid_spec = pltpu.PrefetchScalarGridSpec(
        num_scalar_prefetch=2, grid=(H // hb, qi_tab.shape[0]),
        in_specs=[qspec(MLA_QK_PAD), kspec(MLA_QK_PAD), kspec(MLA_V)], out_specs=[qspec(MLA_V), qspec(LANES)],
        scratch_shapes=[pltpu.VMEM((hb, tq, 1), F32), pltpu.VMEM((hb, tq, 1), F32), pltpu.VMEM((hb, tq, MLA_V), F32)])
    return pl.pallas_call(
        kern, name=name, grid_spec=grid_spec, out_shape=[_sds((T, H * MLA_V), BF16), _sds((T, H * LANES), F32)],
        compiler_params=_params(2))(qi_tab, kj_tab, q, k, v)


def _mla_bwd(q, k, v, do, o, lse, H, name):
    T = q.shape[0]
    tq = _pick(T, 512)
    nq = T // tq
    qi_tab, kj_tab = _causal_pairs(nq, key_major=True)

    def kern(qi_ref, kj_ref, q_ref, k_ref, v_ref, do_ref, o_ref, lse_ref, dq_ref, dk_ref, dv_ref, dk_sc, dv_sc):
        t = pl.program_id(1)
        qi, kj = qi_ref[t], kj_ref[t]
        rows = pl.ds(pl.multiple_of(qi * tq, tq), tq)

        def tile(diagonal):
            s = lax.dot_general(q_ref[...], k_ref[...], NT, preferred_element_type=F32)
            if diagonal:
                s = _diag_mask(s)
            p = jnp.exp(s - lse_ref[:, :1])
            dl = jnp.sum(do_ref[...].astype(F32) * o_ref[...].astype(F32), axis=1, keepdims=True)
            dp = lax.dot_general(do_ref[...], v_ref[...], NT, preferred_element_type=F32)
            ds = (p * (dp - dl)).astype(BF16)
            dv = lax.dot_general(p.astype(BF16), do_ref[...], TN, preferred_element_type=F32)
            dk = lax.dot_general(ds, q_ref[...], TN, preferred_element_type=F32)
            dq = jnp.dot(ds, k_ref[...], preferred_element_type=F32)
            if diagonal:
                dv_sc[...] = dv
                dk_sc[...] = dk
            else:
                dv_sc[...] += dv
                dk_sc[...] += dk

            @pl.when(kj == 0)
            def _():
                dq_ref[rows, :] = dq

            @pl.when(kj > 0)
            def _():
                dq_ref[rows, :] += dq

        @pl.when(qi == kj)
        def _():
            tile(True)

        @pl.when(qi > kj)
        def _():
            tile(False)

        @pl.when(qi == nq - 1)
        def _():
            dk_ref[...] = dk_sc[...]
            dv_ref[...] = dv_sc[...]

    qspec = lambda w: pl.BlockSpec((tq, w), lambda h, t, qi_ref, kj_ref: (qi_ref[t], h))
    kspec = lambda w: pl.BlockSpec((tq, w), lambda h, t, qi_ref, kj_ref: (kj_ref[t], h))
    grid_spec = pltpu.PrefetchScalarGridSpec(
        num_scalar_prefetch=2, grid=(H, qi_tab.shape[0]),
        in_specs=[qspec(MLA_QK_PAD), kspec(MLA_QK_PAD), kspec(MLA_V), qspec(MLA_V), qspec(MLA_V), qspec(LANES)],
        out_specs=[pl.BlockSpec((T, MLA_QK_PAD), lambda h, t, qi_ref, kj_ref: (0, h)), kspec(MLA_QK_PAD), kspec(MLA_V)],
        scratch_shapes=[pltpu.VMEM((tq, MLA_QK_PAD), F32), pltpu.VMEM((tq, MLA_V), F32)])
    return pl.pallas_call(
        kern, name=name, grid_spec=grid_spec,
        out_shape=[_sds((T, H * MLA_QK_PAD), F32), _sds((T, H * MLA_QK_PAD), F32), _sds((T, H * MLA_V), F32)],
        compiler_params=_params(2))(qi_tab, kj_tab, q, k, v, do, o, lse)


def _dil_views(qd, kd, pd, grp, dil, W):
    T = qd.shape[0]
    L = T // dil
    G = len(DIL_GROUPS)
    prev = lambda b: jnp.maximum(b - 1, 0)
    blk = lambda f: pl.BlockSpec((DIL_BLOCK, W), f)
    arrays = (qd.reshape(L, dil * G * W), kd.reshape(L, dil * G * W), pd.reshape(L, dil * 3 * G * W))
    q_spec = blk(lambda r, b: (b, r * G + grp))
    kc_spec = q_spec
    kp_spec = blk(lambda r, b: (prev(b), r * G + grp))
    vc_spec = blk(lambda r, b: (b, r * 3 * G + 3 * grp + 2))
    vp_spec = blk(lambda r, b: (prev(b), r * 3 * G + 3 * grp + 2))
    return arrays, (q_spec, kc_spec, kp_spec, vc_spec, vp_spec)


def _band_masks(b, nb_next=None):
    row = lax.broadcasted_iota(jnp.int32, (DIL_BLOCK, DIL_BLOCK), 0)
    col = lax.broadcasted_iota(jnp.int32, (DIL_BLOCK, DIL_BLOCK), 1)
    return row, col


def _dil_fwd(qd, kd, pd, grp, dil, HD, name):
    T = qd.shape[0]
    W = HD * DIL_HEAD
    L = T // dil
    nb = L // DIL_BLOCK
    (qv, kv_, vv), (q_spec, kc_spec, kp_spec, vc_spec, vp_spec) = _dil_views(qd, kd, pd, grp, dil, W)

    def kern(q_ref, kc_ref, kp_ref, vc_ref, vp_ref, o_ref, lse_ref):
        b = pl.program_id(1)
        row, col = _band_masks(b)
        off = jnp.where(b > 0, 0, 2 * DIL_BLOCK)
        ok_prev = col >= row + off
        ok_cur = col <= row
        for h in range(HD):
            sl = slice(h * DIL_HEAD, (h + 1) * DIL_HEAD)
            qh = q_ref[:, sl]
            sa = jnp.where(ok_prev, lax.dot_general(qh, kp_ref[:, sl], NT, preferred_element_type=F32), NEG)
            sb = jnp.where(ok_cur, lax.dot_general(qh, kc_ref[:, sl], NT, preferred_element_type=F32), NEG)
            m = jnp.maximum(jnp.max(sa, axis=1, keepdims=True), jnp.max(sb, axis=1, keepdims=True))
            ea, eb = jnp.exp(sa - m), jnp.exp(sb - m)
            l = jnp.sum(ea, axis=1, keepdims=True) + jnp.sum(eb, axis=1, keepdims=True)
            acc = (jnp.dot(ea.astype(BF16), vp_ref[:, sl], preferred_element_type=F32)
                   + jnp.dot(eb.astype(BF16), vc_ref[:, sl], preferred_element_type=F32))
            o_ref[:, sl] = acc / l
            lse_ref[:, sl] = jnp.broadcast_to(m + jnp.log(l), (DIL_BLOCK, DIL_HEAD))

    o_spec = pl.BlockSpec((DIL_BLOCK, W), lambda r, b: (b, r))
    o, lse = pl.pallas_call(
        kern, name=name, grid=(dil, nb), in_specs=[q_spec, kc_spec, kp_spec, vc_spec, vp_spec],
        out_specs=[o_spec, o_spec], out_shape=[_sds((L, dil * W), F32)] * 2,
        compiler_params=_params(2))(qv, kv_, kv_, vv, vv)
    return o.reshape(T, W), lse.reshape(T, W)


def _dil_bwd_dq(qd, kd, pd, do, o, lse, grp, dil, HD, name):
    T = qd.shape[0]
    W = HD * DIL_HEAD
    L = T // dil
    nb = L // DIL_BLOCK
    (qv, kv_, vv), (q_spec, kc_spec, kp_spec, vc_spec, vp_spec) = _dil_views(qd, kd, pd, grp, dil, W)
    t_spec = pl.BlockSpec((DIL_BLOCK, W), lambda r, b: (b, r))
    view = lambda a: a.reshape(L, dil * W)

    def kern(q_ref, kc_ref, kp_ref, vc_ref, vp_ref, do_ref, o_ref, lse_ref, dq_ref):
        b = pl.program_id(1)
        row, col = _band_masks(b)
        off = jnp.where(b > 0, 0, 2 * DIL_BLOCK)
        ok_prev = col >= row + off
        ok_cur = col <= row
        for h in range(HD):
            sl = slice(h * DIL_HEAD, (h + 1) * DIL_HEAD)
            qh, doh = q_ref[:, sl], do_ref[:, sl]
            lse_h = lse_ref[:, sl]
            dl = jnp.sum(doh.astype(F32) * o_ref[:, sl].astype(F32), axis=1, keepdims=True)
            sa = jnp.where(ok_prev, lax.dot_general(qh, kp_ref[:, sl], NT, preferred_element_type=F32), NEG)
            sb = jnp.where(ok_cur, lax.dot_general(qh, kc_ref[:, sl], NT, preferred_element_type=F32), NEG)
            pa, pb = jnp.exp(sa - lse_h), jnp.exp(sb - lse_h)
            dsa = pa * (lax.dot_general(doh, vp_ref[:, sl], NT, preferred_element_type=F32) - dl)
            dsb = pb * (lax.dot_general(doh, vc_ref[:, sl], NT, preferred_element_type=F32) - dl)
            dq_ref[:, sl] = (jnp.dot(dsa.astype(BF16), kp_ref[:, sl], preferred_element_type=F32)
                             + jnp.dot(dsb.astype(BF16), kc_ref[:, sl], preferred_element_type=F32))

    dq = pl.pallas_call(
        kern, name=name, grid=(dil, nb),
        in_specs=[q_spec, kc_spec, kp_spec, vc_spec, vp_spec, t_spec, t_spec, t_spec],
        out_specs=t_spec, out_shape=_sds((L, dil * W), F32),
        compiler_params=_params(2))(qv, kv_, kv_, vv, vv, view(do), view(o), view(lse))
    return dq.reshape(T, W)


def _dil_bwd_dkv(qd, kd, pd, do, o, lse, grp, dil, HD, name):
    T = qd.shape[0]
    W = HD * DIL_HEAD
    L = T // dil
    nb = L // DIL_BLOCK
    G = len(DIL_GROUPS)
    nxt = lambda b: jnp.minimum(b + 1, nb - 1)
    blk = lambda f: pl.BlockSpec((DIL_BLOCK, W), f)
    qc_spec = blk(lambda r, b: (b, r * G + grp))
    qn_spec = blk(lambda r, b: (nxt(b), r * G + grp))
    v_spec = blk(lambda r, b: (b, r * 3 * G + 3 * grp + 2))
    tc_spec = blk(lambda r, b: (b, r))
    tn_spec = blk(lambda r, b: (nxt(b), r))
    view = lambda a: a.reshape(L, dil * W)
    qv, kv_, vv = qd.reshape(L, dil * G * W), kd.reshape(L, dil * G * W), pd.reshape(L, dil * 3 * G * W)

    def kern(qc_ref, qn_ref, k_ref, v_ref, doc_ref, don_ref, oc_ref, on_ref, lc_ref, ln_ref, dk_ref, dv_ref):
        b = pl.program_id(1)
        row, col = _band_masks(b)
        off = jnp.where(b + 1 < nb, 0, 2 * DIL_BLOCK)
        ok_next = col >= row + off
        ok_cur = col <= row
        for h in range(HD):
            sl = slice(h * DIL_HEAD, (h + 1) * DIL_HEAD)
            kh, vh = k_ref[:, sl], v_ref[:, sl]
            dk = jnp.zeros((DIL_BLOCK, DIL_HEAD), F32)
            dv = jnp.zeros((DIL_BLOCK, DIL_HEAD), F32)
            for q_ref, do_ref, o_ref, l_ref, ok in ((qc_ref, doc_ref, oc_ref, lc_ref, ok_cur),
                                                    (qn_ref, don_ref, on_ref, ln_ref, ok_next)):
                qh, doh = q_ref[:, sl], do_ref[:, sl]
                dl = jnp.sum(doh.astype(F32) * o_ref[:, sl].astype(F32), axis=1, keepdims=True)
                s = jnp.where(ok, lax.dot_general(qh, kh, NT, preferred_element_type=F32), NEG)
                p = jnp.exp(s - l_ref[:, sl])
                ds = p * (lax.dot_general(doh, vh, NT, preferred_element_type=F32) - dl)
                dv = dv + lax.dot_general(p.astype(BF16), doh, TN, preferred_element_type=F32)
                dk = dk + lax.dot_general(ds.astype(BF16), qh, TN, preferred_element_type=F32)
            dk_ref[:, sl] = dk
            dv_ref[:, sl] = dv

    dk, dv = pl.pallas_call(
        kern, name=name, grid=(dil, nb),
        in_specs=[qc_spec, qn_spec, qc_spec, v_spec, tc_spec, tn_spec, tc_spec, tn_spec, tc_spec, tn_spec],
        out_specs=[tc_spec, tc_spec], out_shape=[_sds((L, dil * W), F32)] * 2,
        compiler_params=_params(2))(qv, qv, kv_, vv, view(do), view(do), view(o), view(o), view(lse), view(lse))
    return dk.reshape(T, W), dv.reshape(T, W)


def _place():
    return lax.axis_index("x"), lax.axis_index("y"), lax.axis_index("c")


def _other_chips(x, y):
    return [(1 - x, y), (x, 1 - y), (1 - x, 1 - y)]


def _kind(name, shard_shape):
    if name in ROW_SHARDED:
        return "row"
    return "col" if shard_shape[1] % LANES == 0 else "stack"


def _remote(src, dst, send_sem, recv_sem, to):
    return pltpu.make_async_remote_copy(src_ref=src, dst_ref=dst, send_sem=send_sem, recv_sem=recv_sem,
                                        device_id=to, device_id_type=MESH_ID)


def _row_tile(rows, cols, itemsize, align):
    return _pick(rows, max(align, (2 * 1024 * 1024) // (cols * itemsize)), align)


def _dma_sems(n):
    return [pltpu.SemaphoreType.DMA((n,)), pltpu.SemaphoreType.DMA((n,))]


def _gather_plan(shard_shapes):
    info, buf_shapes = {}, {}
    for out_name, names in GATHER_PLAN:
        r, c = shard_shapes[names[0]]
        kind = _kind(names[0], (r, c))
        assert kind == "col" or len(names) == 1, out_name
        buf_shapes[out_name] = (r, 4 * c * len(names)) if kind == "col" else (4, r, c)
        for i, n in enumerate(names):
            assert tuple(shard_shapes[n]) == (r, c), n
            info[n] = (out_name, kind, i * 4 * c, r, c)
    return info, buf_shapes


def _place_own(shard, buf_shape, kind, base, me, name, prev=None):
    r, c = shard.shape
    tr = _row_tile(r, c, 2, 16)

    def kern(me_ref, x_ref, *rest):
        rest[-1][...] = x_ref[...]

    if kind == "col":
        out_spec = pl.BlockSpec((tr, c), lambda i, me_ref: (i, base // c + me_ref[0]))
    else:
        out_spec = pl.BlockSpec((None, tr, c), lambda i, me_ref: (me_ref[0], i, 0))
    in_specs = [pl.BlockSpec((tr, c), lambda i, me_ref: (i, 0))] + ([ANY] if prev is not None else [])
    grid_spec = pltpu.PrefetchScalarGridSpec(num_scalar_prefetch=1, grid=(r // tr,), in_specs=in_specs,
                                             out_specs=out_spec)
    args = (me, shard) + ((prev,) if prev is not None else ())
    return pl.pallas_call(kern, name=name, grid_spec=grid_spec, out_shape=_sds(buf_shape, shard.dtype),
                          input_output_aliases={2: 0} if prev is not None else {}, compiler_params=_params(1))(*args)


def _buffers_of(names, info):
    out_names = []
    for n in names:
        if info[n][0] not in out_names:
            out_names.append(info[n][0])
    return out_names


def _ag_side(names, shards, bufs, info):
    out_names = _buffers_of(names, info)
    n_w = len(names)

    def region(outs, w, chip, h):
        out_name, kind, base, r, cc = info[names[w]]
        o = outs[out_names.index(out_name)]
        rows = pl.ds(h * (r // 2), r // 2)
        if kind == "col":
            return o.at[rows, pl.ds(pl.multiple_of(base + chip * cc, LANES), cc)]
        return o.at[chip, rows, :]

    def hop(first, sending):
        def fn(ins, outs, sems):
            x, y, c = _place()
            me, sibling, cps = 2 * x + y, (x, y, 1 - c), []
            for w in range(n_w):
                r = info[names[w]][3]
                for j, (px, py) in enumerate(_other_chips(x, y)):
                    k = 3 * w + j + (0 if first else 3 * n_w)
                    if first and sending:
                        src, dst, to = ins[w].at[pl.ds(c * (r // 2), r // 2), :], region(outs, w, me, c), (px, py, c)
                    elif first:
                        src = dst = region(outs, w, 2 * px + py, c)
                        to = (px, py, c)
                    else:
                        src = dst = region(outs, w, 2 * px + py, c if sending else 1 - c)
                        to = sibling
                    cps.append(_remote(src, dst, sems[0].at[k], sems[1].at[k], to))
            return cps
        return fn

    return _Side([shards[n] for n in names] + [bufs[o] for o in out_names],
                 [_sds(bufs[o].shape, bufs[o].dtype) for o in out_names], {n_w + i: i for i in range(len(out_names))},
                 _dma_sems(6 * n_w), [(hop(True, True), hop(True, False)), (hop(False, True), hop(False, False))])


def _rs_sibling_side(views):
    n = len(views)

    def fn(sending):
        def copies(ins, outs, sems):
            x, y, c = _place()
            return [_remote(ins[w].at[:, 1 - c] if sending else outs[w], outs[w], sems[0].at[w], sems[1].at[w],
                            (x, y, 1 - c)) for w in range(n)]
        return copies

    return _Side(views, [_sds((v.shape[0],) + v.shape[2:], v.dtype) for v in views], {}, _dma_sems(n),
                 [(fn(True), fn(False))])


def _rs_chips_side(parts, kinds, widths):
    n = len(parts)

    def piece(ins, w, chip):
        if kinds[w] == "col":
            return ins[w].at[0, :, pl.ds(pl.multiple_of(chip * widths[w], LANES), widths[w])]
        return ins[w].at[chip]

    def fn(sending):
        def copies(ins, outs, sems):
            x, y, c = _place()
            cps = []
            for w in range(n):
                for j, (px, py) in enumerate(_other_chips(x, y)):
                    k = 3 * w + j
                    src = piece(ins, w, 2 * px + py) if sending else outs[w].at[j]
                    cps.append(_remote(src, outs[w].at[j], sems[0].at[k], sems[1].at[k], (px, py, c)))
            return cps
        return copies

    return _Side(parts, [_sds((3, p_.shape[1], widths[w]), p_.dtype) for w, p_ in enumerate(parts)], {},
                 _dma_sems(3 * n), [(fn(True), fn(False))])


def _rs_join_side(halves):
    n = len(halves)

    def fn(sending):
        def copies(ins, outs, sems):
            x, y, c = _place()
            return [_remote(ins[w] if sending else outs[w], outs[w], sems[0].at[w], sems[1].at[w], (x, y, 1 - c))
                    for w in range(n)]
        return copies

    return _Side(halves, [_sds(h.shape, h.dtype) for h in halves], {}, _dma_sems(n), [(fn(True), fn(False))])


def _pair_sum(g, got, c_idx, name):
    n, _, rows, C = g.shape
    tr = _row_tile(rows, C, 4, 16)

    def kern(c_ref, a_ref, b_ref, o_ref):
        o_ref[...] = (a_ref[...] + b_ref[...]).astype(o_ref.dtype)

    grid_spec = pltpu.PrefetchScalarGridSpec(
        num_scalar_prefetch=1, grid=(n, rows // tr),
        in_specs=[pl.BlockSpec((None, None, tr, C), lambda j, i, c_ref: (j, c_ref[0], i, 0)),
                  pl.BlockSpec((None, tr, C), lambda j, i, c_ref: (j, i, 0))],
        out_specs=pl.BlockSpec((None, tr, C), lambda j, i, c_ref: (j, i, 0)))
    return pl.pallas_call(kern, name=name, grid_spec=grid_spec, out_shape=_sds((n, rows, C), BF16),
                          compiler_params=_params(2))(c_idx, g, got)


def _sum_pieces(pair, recv, kind, me, name):
    _, rows, c = recv.shape
    tr = _row_tile(rows, c, 8, 16)

    def kern(me_ref, own_ref, r_ref, o_ref):
        acc = own_ref[...].astype(F32)
        for j in range(3):
            acc = acc + r_ref[j].astype(F32)
        o_ref[...] = acc

    if kind == "col":
        own_spec = pl.BlockSpec((None, tr, c), lambda i, me_ref: (0, i, me_ref[0]))
    else:
        own_spec = pl.BlockSpec((None, tr, c), lambda i, me_ref: (me_ref[0], i, 0))
    grid_spec = pltpu.PrefetchScalarGridSpec(
        num_scalar_prefetch=1, grid=(rows // tr,),
        in_specs=[own_spec, pl.BlockSpec((3, tr, c), lambda i, me_ref: (0, i, 0))],
        out_specs=pl.BlockSpec((tr, c), lambda i, me_ref: (i, 0)))
    return pl.pallas_call(kern, name=name, grid_spec=grid_spec, out_shape=_sds((rows, c), F32),
                          compiler_params=_params(1))(me, pair, recv)


def _all_reduce_small(vec):
    N = vec.shape[1]
    n_dev = 8

    def body(v_ref, out_ref, slots, send_sems, recv_sems):
        x, y, c = _place()
        me = 4 * x + 2 * y + c
        slots[me] = v_ref[...]
        sent = []
        for k in range(1, n_dev):
            px, py, pc = x ^ (k >> 2), y ^ ((k >> 1) & 1), c ^ (k & 1)
            cp = pltpu.make_async_remote_copy(src_ref=v_ref, dst_ref=slots.at[me], send_sem=send_sems.at[k - 1],
                                              recv_sem=recv_sems.at[k - 1], device_id=(px, py, pc),
                                              device_id_type=MESH_ID)
            cp.start()
            sent.append(cp)
        for k in range(1, n_dev):
            px, py, pc = x ^ (k >> 2), y ^ ((k >> 1) & 1), c ^ (k & 1)
            slot = slots.at[4 * px + 2 * py + pc]
            pltpu.make_async_remote_copy(src_ref=slot, dst_ref=slot, send_sem=send_sems.at[k - 1],
                                         recv_sem=recv_sems.at[k - 1], device_id=(px, py, pc),
                                         device_id_type=MESH_ID).wait_recv()
        for cp in sent:
            cp.wait_send()
        acc = slots[0]
        for j in range(1, n_dev):
            acc = acc + slots[j]
        out_ref[...] = acc

    vm = pl.BlockSpec(memory_space=pltpu.VMEM)
    return pl.pallas_call(
        body, name="ar_gains", out_shape=_sds((1, N), F32), in_specs=[vm], out_specs=vm,
        scratch_shapes=[pltpu.VMEM((n_dev, 1, N), F32), pltpu.SemaphoreType.DMA((n_dev - 1,)),
                        pltpu.SemaphoreType.DMA((n_dev - 1,))])(vec)


def _adamw_math(wv, gv, mv, vv):
    m2 = ADAM_B1 * mv + (1.0 - ADAM_B1) * gv
    v2 = ADAM_B2 * vv + (1.0 - ADAM_B2) * (gv * gv)
    m_hat = m2 / (1.0 - ADAM_B1 ** ADAM_STEP)
    v_hat = v2 / (1.0 - ADAM_B2 ** ADAM_STEP)
    return -ADAM_LR * (m_hat / (jnp.sqrt(v_hat) + ADAM_EPS) + ADAM_WD * wv), m2, v2


def _adamw(w, g, m, v, name):
    R, C = w.shape
    tr = _row_tile(R, C, 8, 8)
    return _vcall(_adamw_math, (R // tr,), (w, g, m, v), [_rows(tr, C)] * 4, [_sds((R, C), F32)] * 3,
                  [_rows(tr, C)] * 3, name)


def _adamw_halves(w, own, recv, m, v, c_idx, name):
    R, C = w.shape
    rows = R // 2
    tr = _row_tile(rows, C, 8, 8)
    nb = rows // tr

    def kern(c_ref, w_ref, own_ref, recv_ref, m_ref, v_ref, g_out, d_out, m_out, v_out):
        def update(g_ref):
            g = g_ref[...]
            g_out[...] = g
            d_out[...], m_out[...], v_out[...] = _adamw_math(w_ref[...], g, m_ref[...], v_ref[...])

        @pl.when(pl.program_id(0) == c_ref[0])
        def _():
            update(own_ref)

        @pl.when(pl.program_id(0) != c_ref[0])
        def _():
            update(recv_ref)

    full = pl.BlockSpec((tr, C), lambda h, i, c_ref: (h * nb + i, 0))
    own_spec = pl.BlockSpec((tr, C), lambda h, i, c_ref: (jnp.where(h == c_ref[0], i, 0), 0))
    recv_spec = pl.BlockSpec((tr, C), lambda h, i, c_ref: (jnp.where(h == c_ref[0], 0, i), 0))
    grid_spec = pltpu.PrefetchScalarGridSpec(num_scalar_prefetch=1, grid=(2, nb),
                                             in_specs=[full, own_spec, recv_spec, full, full], out_specs=[full] * 4)
    return pl.pallas_call(kern, name=name, grid_spec=grid_spec, out_shape=[_sds((R, C), F32)] * 4,
                          compiler_params=_params(2))(c_idx, w, own, recv, m, v)


def _pad_to(a, n, axis):
    extra = n - a.shape[axis]
    if extra == 0:
        return a
    pads = [(0, 0)] * a.ndim
    pads[axis] = (0, extra)
    return jnp.pad(a, pads)


def _round_up(n, m):
    return -(-n // m) * m


def _natural(buf, kind):
    if kind == "col":
        return buf
    n, r, c = buf.shape
    return buf.reshape(n * r, c) if kind == "row" else buf.transpose(1, 0, 2).reshape(r, n * c)


def _halves_view(g, kind, shard_shape):
    r, c = shard_shape
    if kind == "col":
        return g.reshape(1, 2, r // 2, 4 * c)
    if kind == "stack":
        g = g.reshape(r, 4, c).transpose(1, 0, 2)
    return g.reshape(4, 2, r // 2, c)


def _pack_small(vals):
    return jnp.concatenate([_pad_to(vals[n].reshape(1, -1), _round_up(vals[n].size, LANES), 1) for n in SMALL], axis=1)


def _unpack_small(vec, shapes):
    out, off = {}, 0
    for n in SMALL:
        size = int(np.prod(shapes[n]))
        out[n] = vec[:, off:off + size].reshape(shapes[n])
        off += _round_up(size, LANES)
    return out


def _mm_s(sched, a, b, mode, out_dtype, name, **kw):
    side = sched.side(name)
    if side is None:
        return _mm(a, b, mode, out_dtype, name, **kw)
    out, side_outs = _mm(a, b, mode, out_dtype, name, side=side, **kw)
    sched.done(name, side_outs)
    return out


def _ffn_fwd(sched, x, g, tf, tag):
    w = tag[-1]
    n = _rms_fwd(x, g, f"{tag}_norm")
    ab = _mm_s(sched, n, sched.weight(f"w{w}_gu"), "nn", BF16, f"{tag}_gate_up")
    act = _silu_mul(ab, tf, f"{tag}_act")
    out = _mm_s(sched, act, sched.weight(f"w{w}_down"), "nn", F32, f"{tag}_down", res=x, alpha=0.5)
    return out, (n, ab, act)


def _ffn_bwd(sched, dout, x, g, saved, tf, tag):
    w = tag[-1]
    w_gu, w_d = sched.weight(f"w{w}_gu"), sched.weight(f"w{w}_down")
    n, ab, act = saved
    F = act.shape[1]
    dout_b = dout.astype(BF16)
    sched.grad(f"w{w}_down", _mm_s(sched, act, dout_b, "tn", F32, f"{tag}_d_wdown", alpha=0.5))
    dact = _mm_s(sched, dout_b, w_d, "nt", BF16, f"{tag}_d_act", alpha=0.5)
    da, db = _swiglu_bwd(dact, ab, tf, f"{tag}_d_gate_up")
    sched.grad(f"w{w}_gate", _mm_s(sched, n, da, "tn", F32, f"{tag}_d_wgate"))
    sched.grad(f"w{w}_up", _mm_s(sched, n, db, "tn", F32, f"{tag}_d_wup"))
    dn = _mm_s(sched, da, w_gu, "nt", F32, f"{tag}_d_norm", a2=db, b2=w_gu, b2_k_offset=F)
    return _rms_bwd_call(x, g, dn, dout, f"{tag}_d_x")


def _local_step(sched, x, p, pos_b, target, Gn, dims):
    T, D = x.shape
    H, HD, QL, KVL, LP, tf = dims["H"], dims["HD"], dims["QL"], dims["KVL"], dims["LP"], dims["tf"]
    Wd = HD * DIL_HEAD
    scale_mla, scale_dil = MLA_QK ** -0.5, DIL_HEAD ** -0.5
    kr_col = (QL + KVL) // LANES
    tab_mla = tuple(_rope_tables(pos_b, MLA_ROPE, "rope_tab_mla"))
    tab_dil = tuple(_rope_tables(pos_b, DIL_ROT, "rope_tab_dil"))

    W = sched.weight
    mm = functools.partial(_mm_s, sched)

    x1, ffn1 = _ffn_fwd(sched, x, Gn["g_ffn1"], tf, "ffn1")
    h = _rms_fwd(x1, Gn["g_mix"], "mix_norm")
    lat = mm(h, W("w_lat"), "nn", F32, "proj_lat")
    pd = mm(h, W("w_dil"), "nn", BF16, "proj_dil")
    pg = mm(h, W("w_gin"), "nn", BF16, "proj_gate")

    cq, ckv = _lat_fwd(lat, Gn["g_cq"], Gn["g_ckv"], "lat_norm")
    q_raw = mm(cq, W("w_uq"), "nn", F32, "mla_q_up")
    kv = mm(ckv, W("w_ukv"), "nn", F32, "mla_kv_up")
    q = _mla_q_prep(q_raw, Gn["g_q_mla"], tab_mla, H, scale_mla, "mla_q_prep")
    k, v = _mla_k_prep(kv, lat, kr_col, Gn["g_k_mla"], tab_mla, H, "mla_k_prep")
    o_mla, lse_mla = _mla_fwd(q, k, v, H, "mla_attn")

    qd, kd = _dil_prep(pd, Gn["g_q_dil"], Gn["g_k_dil"], tab_dil, HD, scale_dil, "dil_prep")
    og, lg = [], []
    for grp, (win, dil) in enumerate(DIL_GROUPS):
        o_, l_ = _dil_fwd(qd, kd, pd, grp, dil, HD, f"dil_attn{grp}")
        og.append(o_)
        lg.append(l_)
    o_dil, lse_dil = _dil_merge(og, lg, "dil_merge")

    bm = mm(o_mla, W("w_br_mla"), "nn", F32, "branch_mla")
    bd = mm(o_dil, W("w_br_dil"), "nn", F32, "branch_dil")
    merged = _gate_merge(pg, bm, bd, "gate_merge")
    x2 = mm(merged, W("w_o"), "nn", F32, "out_proj", res=x1)

    x3, ffn2 = _ffn_fwd(sched, x2, Gn["g_ffn2"], tf, "ffn2")
    n4 = _rms_fwd(x3, Gn["g_ple"], "ple_norm")
    zg = mm(n4, W("w_ple_gate"), "nn", F32, "ple_gate")
    p_b = p.astype(BF16)
    pp = mm(p_b, W("w_ple_proj"), "nn", F32, "ple_proj")
    dy, dpp, dzg, loss = _ple_loss(x3, zg, pp, target, "ple_loss")

    gg = {}
    sched.grad("w_ple_proj", mm(p_b, dpp, "tn", F32, "d_w_ple_proj"))
    sched.grad("w_ple_gate", mm(n4, dzg, "tn", F32, "d_w_ple_gate"))
    dn4 = mm(dzg, W("w_ple_gate"), "nt", F32, "d_ple_norm")
    dx3, gg["g_ple"] = _rms_bwd_call(x3, Gn["g_ple"], dn4, dy, "d_x3")

    dx2, gg["g_ffn2"] = _ffn_bwd(sched, dx3, x2, Gn["g_ffn2"], ffn2, tf, "ffn2")

    dx2_b = dx2.astype(BF16)
    sched.grad("w_o", mm(merged, dx2_b, "tn", F32, "d_w_o"))
    dmerged = mm(dx2_b, W("w_o"), "nt", F32, "d_merged")
    dbm, dbd, dpg = _gate_bwd(dmerged, pg, bm, bd, "d_gate")
    sched.grad("w_br_mla", mm(o_mla, dbm, "tn", F32, "d_w_br_mla"))
    sched.grad("w_br_dil", mm(o_dil, dbd, "tn", F32, "d_w_br_dil"))
    do_mla = mm(dbm, W("w_br_mla"), "nt", BF16, "d_o_mla")
    do_dil = mm(dbd, W("w_br_dil"), "nt", BF16, "d_o_dil")

    dh = mm(dpg, W("w_gin"), "nt", F32, "d_h_gate")
    sched.grad("w_gin", mm(h, dpg, "tn", F32, "d_w_gin"))
    gq_d, gk_d = Gn["g_q_dil"], Gn["g_k_dil"]
    dgq_d, dgk_d = [], []
    for grp, (win, dil) in enumerate(DIL_GROUPS):
        dq_ = _dil_bwd_dq(qd, kd, pd, do_dil, o_dil, lse_dil, grp, dil, HD, f"dil_dq{grp}")
        dk_, dv_ = _dil_bwd_dkv(qd, kd, pd, do_dil, o_dil, lse_dil, grp, dil, HD, f"dil_dkv{grp}")
        dpd_g, dgq_, dgk_ = _dil_prep_bwd(dq_, dk_, dv_, pd, grp, gq_d, gk_d, tab_dil, HD, scale_dil, f"d_dil_prep{grp}")
        dgq_d.append(dgq_)
        dgk_d.append(dgk_)
        w_g = W("w_dil")[:, grp * 3 * Wd:(grp + 1) * 3 * Wd]
        dh = mm(dpd_g, w_g, "nt", F32, f"d_h_dil{grp}", res=dh)
        sched.grad(f"w_dil{grp}", mm(h, dpd_g, "tn", F32, f"d_w_dil{grp}"))
    gg["g_q_dil"] = jnp.concatenate(dgq_d, axis=0)
    gg["g_k_dil"] = jnp.concatenate(dgk_d, axis=0)

    dq, dk, dv = _mla_bwd(q, k, v, do_mla, o_mla, lse_mla, H, "mla_bwd")
    dq_raw, gg["g_q_mla"] = _mla_q_bwd(dq, q_raw, Gn["g_q_mla"], tab_mla, H, scale_mla, "d_mla_q_prep")
    dkv, dkr, gg["g_k_mla"] = _mla_k_bwd(dk, dv, kv, lat, kr_col, Gn["g_k_mla"], tab_mla, H, "d_mla_k_prep")
    sched.grad("w_uq", mm(cq, dq_raw, "tn", F32, "d_w_uq"))
    sched.grad("w_ukv", mm(ckv, dkv, "tn", F32, "d_w_ukv"))
    dcq = mm(dq_raw, W("w_uq"), "nt", F32, "d_cq")
    dckv = mm(dkv, W("w_ukv"), "nt", F32, "d_ckv")
    dlat, gg["g_cq"], gg["g_ckv"] = _lat_bwd(dcq, dckv, dkr, lat, Gn["g_cq"], Gn["g_ckv"], "d_lat_norm")
    dh = mm(dlat, W("w_lat"), "nt", F32, "d_h_lat", res=dh)
    sched.grad("w_lat", mm(h, dlat, "tn", F32, "d_w_lat"))

    dx1, gg["g_mix"] = _rms_bwd_call(x1, Gn["g_mix"], dh, dx2, "d_x1")
    dx, gg["g_ffn1"] = _ffn_bwd(sched, dx1, x, Gn["g_ffn1"], ffn1, tf, "ffn1")
    return loss, dx, gg


def _layout_weight(name, full, dims):
    H, QL, KVL, LP, Wd = dims["H"], dims["QL"], dims["KVL"], dims["LP"], dims["HD"] * DIL_HEAD
    off_dil = QL + KVL + MLA_ROPE
    off_gate = off_dil + 3 * len(DIL_GROUPS) * Wd
    if name == "w_lat":
        return _pad_to(full("w_in")[:, :off_dil], LP, 1)
    if name == "w_dil":
        return full("w_in")[:, off_dil:off_gate]
    if name == "w_gin":
        return full("w_in")[:, off_gate:]
    if name == "w_uq":
        return _pad_to(full("w_uq").reshape(QL, H, MLA_QK), MLA_QK_PAD, 2).reshape(QL, H * MLA_QK_PAD)
    return full(name)


def _natural_grad(name, gw, dims):
    H, QL, KVL = dims["H"], dims["QL"], dims["KVL"]
    if name == "w_in":
        return jnp.concatenate([gw["w_lat"][:, :QL + KVL + MLA_ROPE]] + [gw[f"w_dil{g}"] for g in range(len(DIL_GROUPS))]
                               + [gw["w_gin"]], axis=1)
    if name == "w_uq":
        return gw["w_uq"].reshape(QL, H, MLA_QK_PAD)[:, :, :MLA_QK].reshape(QL, H * MLA_QK)
    return gw[name]


WEIGHT_SOURCES = {"w1_gu": ("w1_gate", "w1_up"), "w2_gu": ("w2_gate", "w2_up"), "w_lat": ("w_in",), "w_dil": ("w_in",),
                  "w_gin": ("w_in",)}
AG_FIRST = ("w1_gate", "w1_up")
AG_RIDES = {"ffn1_gate_up": ("w1_down", "w_in", "w_uq", "w_ukv"),
            "ffn1_down": ("w_br_mla", "w_br_dil", "w_o", "w_ple_gate", "w_ple_proj", "w2_down"),
            "proj_dil": ("w2_gate", "w2_up")}
RS_LATE = ("w1_gate", "w1_up", "w1_down")
RS_SIBLING_RIDE = "ffn1_d_wdown"
RS_CHIPS_RIDES = {"ffn1_d_act": ("w2_gate", "w2_up", "w_ukv", "w_uq"), "ffn1_d_wgate": ("w_in",),
                  "ffn1_d_wup": ("w2_down", "w_o", "w_ple_gate", "w_br_mla", "w_br_dil", "w_ple_proj")}
RS_JOIN_RIDE = "ffn1_d_norm"


class _MeshSchedule:
    def __init__(self, w, m, v, dims):
        self.w, self.m, self.v, self.dims = w, m, v, dims
        self.shapes = {n: tuple(w[n].shape[1:]) for n in BIG}
        self.kinds = {n: _kind(n, self.shapes[n]) for n in BIG}
        self.info, buf_shapes = _gather_plan(self.shapes)
        x, y, c = _place()
        self.me = (2 * x + y).astype(jnp.int32).reshape(1)
        self.c_idx = c.astype(jnp.int32).reshape(1)
        self.shards = {n: w[n][0].astype(BF16) for n in BIG}
        self.bufs, self.gathered, self.layout = {}, set(), {}
        for n in BIG:
            out_name, kind, base, _, _ = self.info[n]
            self.bufs[out_name] = _place_own(self.shards[n], buf_shapes[out_name], kind, base, self.me, f"ag_own_{n}",
                                             prev=self.bufs.get(out_name))
        self.early = [n for n in BIG if n not in RS_LATE]
        self.gw, self.views, self.pairs, self.halves, self.recv = {}, {}, {}, {}, {}
        self._ag_done(AG_FIRST, _run_side(self._ag(AG_FIRST), "ag_first"))

    def _ag(self, names):
        return _ag_side(names, self.shards, self.bufs, self.info)

    def _ag_done(self, names, outs):
        for out_name, buf in zip(_buffers_of(names, self.info), outs):
            self.bufs[out_name] = buf
        self.gathered.update(names)

    def weight(self, name):
        if name not in self.layout:
            assert all(s in self.gathered for s in WEIGHT_SOURCES.get(name, (name,))), name
            if name in self.bufs and name not in self.info:
                self.layout[name] = self.bufs[name]
            else:
                full = lambda n: _natural(self.bufs[self.info[n][0]], self.info[n][1])
                self.layout[name] = _layout_weight(name, full, self.dims)
        return self.layout[name]

    def grad(self, name, g):
        self.gw[name] = g

    def _views(self, names):
        for n in names:
            self.views[n] = _halves_view(_natural_grad(n, self.gw, self.dims), self.kinds[n], self.shapes[n])
        return [self.views[n] for n in names]

    def _pair(self, names, got):
        for n, g in zip(names, got):
            self.pairs[n] = _pair_sum(self.views[n], g, self.c_idx, f"rs_pair_{n}")

    def _chips(self, names):
        return _rs_chips_side([self.pairs[n] for n in names], [self.kinds[n] for n in names],
                              [self.shapes[n][1] for n in names])

    def _sum(self, names, recv):
        for n, r in zip(names, recv):
            self.halves[n] = _sum_pieces(self.pairs[n], r, self.kinds[n], self.me, f"rs_sum_{n}")

    def side(self, tag):
        if tag in AG_RIDES:
            return self._ag(AG_RIDES[tag])
        if tag == RS_SIBLING_RIDE:
            return _rs_sibling_side(self._views(self.early))
        if tag in RS_CHIPS_RIDES:
            return self._chips(RS_CHIPS_RIDES[tag])
        if tag == RS_JOIN_RIDE:
            return _merge_sides([_rs_join_side([self.halves[n] for n in self.early]),
                                 _rs_sibling_side(self._views(RS_LATE))])
        return None

    def done(self, tag, outs):
        if tag in AG_RIDES:
            self._ag_done(AG_RIDES[tag], outs)
        elif tag == RS_SIBLING_RIDE:
            self._pair(self.early, outs)
        elif tag in RS_CHIPS_RIDES:
            self._sum(RS_CHIPS_RIDES[tag], outs)
        elif tag == RS_JOIN_RIDE:
            self.recv.update(zip(self.early, outs[:len(self.early)]))
            self._pair(RS_LATE, outs[len(self.early):])

    def finish(self):
        late = list(RS_LATE)
        self._sum(late, _run_side(self._chips(late), "rs_chips_late"))
        self.recv.update(zip(late, _run_side(_rs_join_side([self.halves[n] for n in late]), "rs_join_late")))
        outs = {"grad": {}, "delta": {}, "m": {}, "v": {}}
        for n in BIG:
            res = _adamw_halves(self.w[n][0], self.halves[n], self.recv[n], self.m[n][0], self.v[n][0], self.c_idx,
                                f"adamw_{n}")
            for kind, a in zip(("grad", "delta", "m", "v"), res):
                outs[kind][n] = a.reshape((1,) + a.shape)
        return outs


def _step(x, p, positions, loss_target, w, m, v):
    T, D = x.shape[1], x.shape[2]
    QL, KVL = w["g_cq"].shape[1], w["g_ckv"].shape[1]
    dims = {
        "H": 4 * w["w_uq"].shape[2] // MLA_QK, "HD": w["w_br_dil"].shape[1] // DIL_HEAD, "QL": QL, "KVL": KVL,
        "LP": _round_up(QL + KVL + MLA_ROPE, LANES), "tf": _pick(4 * w["w1_gate"].shape[2], 512),
    }
    small_shapes = {n: w[n].shape for n in SMALL}
    sched = _MeshSchedule(w, m, v, dims)
    Gn = {n: w[n] for n in SMALL}
    Gn["g_q_mla"] = _pad_to(Gn["g_q_mla"], MLA_QK_PAD, 1)
    Gn["g_k_mla"] = _pad_to(Gn["g_k_mla"], MLA_QK_PAD, 1)
    Gn["g_q_dil"] = Gn["g_q_dil"].reshape(len(DIL_GROUPS), 1, DIL_HEAD)
    Gn["g_k_dil"] = Gn["g_k_dil"].reshape(len(DIL_GROUPS), 1, DIL_HEAD)

    pos_b = jnp.broadcast_to(positions.astype(F32).reshape(T, 1), (T, LANES))
    loss, dx, gg = _local_step(sched, x[0], p[0, 0], pos_b, loss_target[0], Gn, dims)
    loss = lax.psum(loss[0, 0], ("x", "y", "c"))
    outs = sched.finish()

    gg["g_q_mla"] = gg["g_q_mla"][:, :MLA_QK]
    gg["g_k_mla"] = gg["g_k_mla"][:, :MLA_QK]
    g_small = _all_reduce_small(_pack_small(gg))
    d_s, m_s, v_s = _adamw(_pack_small({n: w[n] for n in SMALL}), g_small, _pack_small({n: m[n] for n in SMALL}),
                           _pack_small({n: v[n] for n in SMALL}), "adamw_gains")
    for kind, buf in (("grad", g_small), ("delta", d_s), ("m", m_s), ("v", v_s)):
        outs[kind].update(_unpack_small(buf, small_shapes))

    grad_x = dx.reshape(1, T, D)
    return (loss, grad_x, *[outs["grad"][n] for n in WEIGHTS], *[outs["delta"][n] for n in WEIGHTS],
            *[outs["m"][n] for n in WEIGHTS], *[outs["v"][n] for n in WEIGHTS])


def kernel(x, p, positions, g_ffn1, w1_gate, w1_up, w1_down, g_mix, w_in, g_cq, w_uq, g_ckv, w_ukv, g_q_mla, g_k_mla, g_q_dil, g_k_dil, w_br_mla, w_br_dil, w_o, g_ffn2, w2_gate, w2_up, w2_down, g_ple, w_ple_gate, w_ple_proj, loss_target, m_g_ffn1, m_w1_gate, m_w1_up, m_w1_down, m_g_mix, m_w_in, m_g_cq, m_w_uq, m_g_ckv, m_w_ukv, m_g_q_mla, m_g_k_mla, m_g_q_dil, m_g_k_dil, m_w_br_mla, m_w_br_dil, m_w_o, m_g_ffn2, m_w2_gate, m_w2_up, m_w2_down, m_g_ple, m_w_ple_gate, m_w_ple_proj, v_g_ffn1, v_w1_gate, v_w1_up, v_w1_down, v_g_mix, v_w_in, v_g_cq, v_w_uq, v_g_ckv, v_w_ukv, v_g_q_mla, v_g_k_mla, v_g_q_dil, v_g_k_dil, v_w_br_mla, v_w_br_dil, v_w_o, v_g_ffn2, v_w2_gate, v_w2_up, v_w2_down, v_g_ple, v_w_ple_gate, v_w_ple_proj):
    args = locals()
    w = {n: args[n] for n in WEIGHTS}
    m = {n: args["m_" + n] for n in WEIGHTS}
    v = {n: args["v_" + n] for n in WEIGHTS}
    return _step(x, p, positions, loss_target, w, m, v)
```

```python
import functools

import numpy as np
import jax
import jax.numpy as jnp
from jax import lax
from jax.experimental import pallas as pl
from jax.experimental.pallas import tpu as pltpu

F32 = jnp.float32
BF16 = jnp.bfloat16
MESH_ID = pl.DeviceIdType.MESH

MLA_NOPE = 128
MLA_ROPE = 64
MLA_V = 128
MLA_QK = MLA_NOPE + MLA_ROPE
MLA_QK_PAD = 256
DIL_GROUPS = ((128, 1), (512, 4), (2048, 16))
DIL_HEAD = 128
DIL_ROT = DIL_HEAD // 4
DIL_BLOCK = 128
ROPE_THETA = 500000.0
EPS = 1e-6
NEG = -1e30
ADAM_LR = 0.001
ADAM_B1 = 0.9
ADAM_B2 = 0.999
ADAM_EPS = 1e-08
ADAM_WD = 0.01
ADAM_STEP = 10

LANES = 128
VMEM_LIMIT_BYTES = 56 * 1024 * 1024

BIG = ("w1_gate", "w1_up", "w1_down", "w_in", "w_uq", "w_ukv", "w_br_mla", "w_br_dil", "w_o",
       "w2_gate", "w2_up", "w2_down", "w_ple_gate", "w_ple_proj")
GATHER_PLAN = (("w1_gu", ("w1_gate", "w1_up")), ("w1_down", ("w1_down",)), ("w_in", ("w_in",)), ("w_uq", ("w_uq",)),
               ("w_ukv", ("w_ukv",)), ("w_br_mla", ("w_br_mla",)), ("w_br_dil", ("w_br_dil",)), ("w_o", ("w_o",)),
               ("w2_gu", ("w2_gate", "w2_up")), ("w2_down", ("w2_down",)), ("w_ple_gate", ("w_ple_gate",)),
               ("w_ple_proj", ("w_ple_proj",)))
ROW_SHARDED = ("w1_down", "w_o", "w2_down", "w_ple_gate")
SMALL = ("g_ffn1", "g_mix", "g_cq", "g_ckv", "g_q_mla", "g_k_mla", "g_q_dil", "g_k_dil", "g_ffn2", "g_ple")
WEIGHTS = ("g_ffn1", "w1_gate", "w1_up", "w1_down", "g_mix", "w_in", "g_cq", "w_uq", "g_ckv", "w_ukv", "g_q_mla",
           "g_k_mla", "g_q_dil", "g_k_dil", "w_br_mla", "w_br_dil", "w_o", "g_ffn2", "w2_gate", "w2_up", "w2_down",
           "g_ple", "w_ple_gate", "w_ple_proj")


def _pick(n, target, align=LANES):
    if n <= target:
        return n
    t = (target // align) * align
    while t >= align:
        if n % t == 0:
            return t
        t -= align
    return n


def _params(n_axes):
    return pltpu.CompilerParams(dimension_semantics=("arbitrary",) * n_axes, vmem_limit_bytes=VMEM_LIMIT_BYTES)


def _sigmoid(x):
    return 1.0 / (1.0 + jnp.exp(-x))


ANY = pl.BlockSpec(memory_space=pl.ANY)


class _Side:
    def __init__(self, arrays, out_shapes, aliases, sem_shapes, phases):
        self.arrays, self.out_shapes, self.aliases = list(arrays), list(out_shapes), dict(aliases)
        self.sem_shapes, self.phases = list(sem_shapes), list(phases)

    def start(self, p, ins, outs, sems):
        for cp in self.phases[p][0](ins, outs, sems):
            cp.start()

    def wait(self, p, ins, outs, sems):
        for cp in self.phases[p][1](ins, outs, sems):
            cp.wait_recv()
        for cp in self.phases[p][0](ins, outs, sems):
            cp.wait_send()

    def run(self, step, n_steps, ins, outs, sems):
        n_ph = len(self.phases)
        assert n_ph <= 2
        starts = (0, int(0.85 * (n_steps - 1)))
        if n_steps <= n_ph:
            @pl.when(step == n_steps - 1)
            def _():
                for p in range(n_ph):
                    self.start(p, ins, outs, sems)
                    self.wait(p, ins, outs, sems)
            return
        for p in range(n_ph):
            @pl.when(step == starts[p])
            def _(p=p):
                if p > 0:
                    self.wait(p - 1, ins, outs, sems)
                self.start(p, ins, outs, sems)

        @pl.when(step == n_steps - 1)
        def _():
            self.wait(n_ph - 1, ins, outs, sems)


def _merge_sides(sides):
    arrays, out_shapes, aliases, sem_shapes, spans = [], [], {}, [], []
    for s in sides:
        assert len(s.phases) == 1
        spans.append((len(arrays), len(out_shapes), len(sem_shapes), s))
        aliases.update({len(arrays) + i: len(out_shapes) + o for i, o in s.aliases.items()})
        arrays += s.arrays
        out_shapes += s.out_shapes
        sem_shapes += s.sem_shapes

    def part(which):
        def fn(ins, outs, sems):
            cps = []
            for a0, o0, s0, s in spans:
                cps += s.phases[0][which](ins[a0:a0 + len(s.arrays)], outs[o0:o0 + len(s.out_shapes)],
                                          sems[s0:s0 + len(s.sem_shapes)])
            return cps
        return fn

    return _Side(arrays, out_shapes, aliases, sem_shapes, [(part(0), part(1))])


def _side_parts(side, n_lead, n_out):
    if side is None:
        return [], [], [], {}
    return (side.arrays, side.out_shapes, side.sem_shapes, {n_lead + i: n_out + o for i, o in side.aliases.items()})


def _carry(kern, side, n_lead, n_out, n_scratch, step_of, n_steps):
    if side is None:
        return kern
    a = n_lead
    b = a + len(side.arrays)
    c = b + n_out
    d = c + len(side.out_shapes)
    e = d + n_scratch

    def wrapped(*refs):
        side.run(step_of(), n_steps, refs[a:b], refs[c:d], refs[e:])
        kern(*refs[:a], *refs[b:c], *refs[d:e])

    return wrapped


def _run_side(side, name):
    n_in, n_out = len(side.arrays), len(side.out_shapes)

    def body(*refs):
        ins, outs, sems = refs[:n_in], refs[n_in:n_in + n_out], refs[n_in + n_out:]
        for p in range(len(side.phases)):
            side.start(p, ins, outs, sems)
            side.wait(p, ins, outs, sems)

    return pl.pallas_call(body, name=name, out_shape=side.out_shapes, in_specs=[ANY] * n_in, out_specs=[ANY] * n_out,
                          scratch_shapes=side.sem_shapes, input_output_aliases=side.aliases)(*side.arrays)


def _mm(a, b, mode, out_dtype, name, res=None, alpha=1.0, a2=None, b2=None, b2_k_offset=0, side=None,
        tm=1024, tn=1536, tk=1024):
    if mode == "nn":
        (M, K), (K2, N) = a.shape, b.shape
    elif mode == "nt":
        (M, K), (N, K2) = a.shape, b.shape
    else:
        (K, M), (K2, N) = a.shape, b.shape
    assert K == K2 or (mode == "nt" and K2 > K), (name, a.shape, b.shape)
    assert a.dtype == BF16 and b.dtype == BF16, name
    tm, tn, tk = _pick(M, tm), _pick(N, tn), _pick(K, tk)
    nk = K // tk
    assert b2_k_offset % tk == 0 and (b2_k_offset == 0 or mode == "nt"), name
    k_off2 = b2_k_offset // tk
    if mode == "nn":
        a_spec = pl.BlockSpec((tm, tk), lambda i, j, k: (i, k))
        b_spec = pl.BlockSpec((tk, tn), lambda i, j, k: (k, j))
        dims = (((1,), (0,)), ((), ()))
    elif mode == "nt":
        a_spec = pl.BlockSpec((tm, tk), lambda i, j, k: (i, k))
        b_spec = pl.BlockSpec((tn, tk), lambda i, j, k: (j, k))
        b2_spec = pl.BlockSpec((tn, tk), lambda i, j, k: (j, k + k_off2))
        dims = (((1,), (1,)), ((), ()))
    else:
        a_spec = pl.BlockSpec((tk, tm), lambda i, j, k: (k, i))
        b_spec = pl.BlockSpec((tk, tn), lambda i, j, k: (k, j))
        dims = (((0,), (0,)), ((), ()))
    o_spec = pl.BlockSpec((tm, tn), lambda i, j, k: (i, j))
    has_res = res is not None
    n_pairs = 1 if a2 is None else 2
    n_main = 2 * n_pairs + int(has_res)
    n_side_in = len(side.arrays) if side else 0
    n_side_out = len(side.out_shapes) if side else 0
    n_acc = 1 if nk > 1 else 0
    gi, gj = M // tm, N // tn
    n_steps = gi * gj * nk

    def kern(*refs):
        r_ref = refs[2 * n_pairs] if has_res else None
        o_ref = refs[n_main + n_side_in]
        if side:
            step = (pl.program_id(0) * gj + pl.program_id(1)) * nk + pl.program_id(2)
            side.run(step, n_steps, refs[n_main:n_main + n_side_in],
                     refs[n_main + n_side_in + 1:n_main + n_side_in + 1 + n_side_out],
                     refs[n_main + n_side_in + 1 + n_side_out + n_acc:])
        part = lax.dot_general(refs[0][...], refs[1][...], dims, preferred_element_type=F32)
        if n_pairs == 2:
            part = part + lax.dot_general(refs[2][...], refs[3][...], dims, preferred_element_type=F32)

        def finish(r):
            if alpha != 1.0:
                r = r * alpha
            if has_res:
                r = r_ref[...] + r
            o_ref[...] = r.astype(o_ref.dtype)

        if nk == 1:
            finish(part)
            return
        acc_ref = refs[n_main + n_side_in + 1 + n_side_out]
        k = pl.program_id(2)

        @pl.when(k == 0)
        def _():
            acc_ref[...] = part

        @pl.when(k > 0)
        def _():
            acc_ref[...] += part

        @pl.when(k == nk - 1)
        def _():
            finish(acc_ref[...])

    ins = (a, b) + ((a2, b2) if n_pairs == 2 else ()) + ((res,) if has_res else ())
    in_specs = [a_spec, b_spec] + ([a_spec, b2_spec if mode == "nt" else b_spec] if n_pairs == 2 else [])
    in_specs += [o_spec] if has_res else []
    out_shape = jax.ShapeDtypeStruct((M, N), out_dtype)
    scratch = [pltpu.VMEM((tm, tn), F32)] if nk > 1 else []
    if not side:
        return pl.pallas_call(kern, name=name, grid=(gi, gj, nk), in_specs=in_specs, out_specs=o_spec,
                              out_shape=out_shape, scratch_shapes=scratch, compiler_params=_params(3))(*ins)
    outs = pl.pallas_call(
        kern, name=name, grid=(gi, gj, nk), in_specs=in_specs + [ANY] * n_side_in,
        out_specs=[o_spec] + [ANY] * n_side_out, out_shape=[out_shape] + list(side.out_shapes),
        scratch_shapes=scratch + list(side.sem_shapes),
        input_output_aliases={n_main + i: 1 + o for i, o in side.aliases.items()},
        compiler_params=_params(3))(*ins, *side.arrays)
    return outs[0], list(outs[1:])


def _vcall(body, grid, ins, in_specs, out_shapes, out_specs, name, n_inner_acc=0, n_acc=0):
    n_in, n_out = len(ins), len(out_shapes)
    n_plain = n_out - n_acc - n_inner_acc

    def kern(*refs):
        vals = body(*[r[...] for r in refs[:n_in]])
        if not isinstance(vals, (tuple, list)):
            vals = (vals,)
        out_refs = refs[n_in:]
        inner_first = pl.program_id(len(grid) - 1) == 0
        first = inner_first
        for ax in range(len(grid) - 1):
            first = jnp.logical_and(first, pl.program_id(ax) == 0)
        for idx, (r, v) in enumerate(zip(out_refs, vals)):
            if idx < n_plain:
                r[...] = v.astype(r.dtype)
                continue
            start = inner_first if idx < n_plain + n_inner_acc else first

            @pl.when(start)
            def _(r=r, v=v):
                r[...] = v.astype(r.dtype)

            @pl.when(jnp.logical_not(start))
            def _(r=r, v=v):
                r[...] += v.astype(r.dtype)

    out = pl.pallas_call(kern, name=name, grid=grid, in_specs=in_specs, out_specs=out_specs, out_shape=out_shapes,
                         compiler_params=_params(len(grid)))(*ins)
    return out


def _rows(tm, c):
    return pl.BlockSpec((tm, c), lambda i: (i, 0))


def _vec(c):
    return pl.BlockSpec((1, c), lambda i: (0, 0))


def _sds(shape, dtype):
    return jax.ShapeDtypeStruct(shape, dtype)


def _rstd(x, c):
    return lax.rsqrt(jnp.sum(x * x, axis=-1, keepdims=True) * (1.0 / c) + EPS)


def _rms_bwd(xh, r, g, dn, c):
    u = dn * g
    dx = r * (u - xh * (jnp.sum(xh * u, axis=-1, keepdims=True) * (1.0 / c)))
    return dx, jnp.sum(dn * xh, axis=0, keepdims=True)


def _rope(t, c, sa, sb, half):
    return t * c + pltpu.roll(t, LANES - half, 1) * sa + pltpu.roll(t, half, 1) * sb


def _rope_t(d, c, sa, sb, half):
    return d * c + pltpu.roll(d * sa, half, 1) + pltpu.roll(d * sb, LANES - half, 1)


def _rope_tables(pos_b, rd, name):
    T = pos_b.shape[0]
    half = rd // 2
    inv = ROPE_THETA ** (-jnp.arange(half, dtype=F32) * 2.0 / rd)
    inv_full = jnp.concatenate([inv, inv, jnp.zeros((LANES - rd,), F32)]).reshape(1, LANES)
    lane = np.arange(LANES)
    ma = jnp.asarray((lane < half).astype(np.float32)).reshape(1, LANES)
    mb = jnp.asarray(((lane >= half) & (lane < rd)).astype(np.float32)).reshape(1, LANES)
    tm = _pick(T, 1024, 8)

    def body(pos, invf, a, b):
        ang = pos * invf
        c, s = jnp.cos(ang), jnp.sin(ang)
        inside = a + b
        return c * inside + (1.0 - inside), -s * a, s * b

    return _vcall(body, (T // tm,), (pos_b, inv_full, ma, mb), [_rows(tm, LANES)] + [_vec(LANES)] * 3,
                  [_sds((T, LANES), F32)] * 3, [_rows(tm, LANES)] * 3, name)


def _rms_fwd(x, g, name):
    T, C = x.shape
    tm = _pick(T, 512, 8)

    def body(xv, gv):
        return xv * _rstd(xv, C) * gv

    return _vcall(body, (T // tm,), (x, g), [_rows(tm, C), _vec(C)], [_sds((T, C), BF16)], [_rows(tm, C)], name)[0]


def _rms_bwd_call(x, g, dn, dres, name):
    T, C = x.shape
    tm = _pick(T, 256, 8)

    def body(xv, gv, dnv, drv):
        r = _rstd(xv, C)
        dx, dg = _rms_bwd(xv * r, r, gv, dnv.astype(F32), C)
        return drv + dx, dg

    return _vcall(body, (T // tm,), (x, g, dn, dres), [_rows(tm, C), _vec(C), _rows(tm, C), _rows(tm, C)],
                  [_sds((T, C), F32), _sds((1, C), F32)], [_rows(tm, C), _vec(C)], name, n_acc=1)


def _silu_mul(ab, tf, name):
    T, F2 = ab.shape
    F = F2 // 2
    nf = F // tf
    tm = _pick(T, 512, 8)

    def body(a, b):
        a, b = a.astype(F32), b.astype(F32)
        return a * _sigmoid(a) * b

    tile = pl.BlockSpec((tm, tf), lambda i, j: (i, j))
    return _vcall(body, (T // tm, nf), (ab, ab), [tile, pl.BlockSpec((tm, tf), lambda i, j: (i, j + nf))],
                  [_sds((T, F), BF16)], [tile], name)[0]


def _swiglu_bwd(dact, ab, tf, name):
    T, F = dact.shape
    nf = F // tf
    tm = _pick(T, 512, 8)

    def body(dv, a, b):
        d, a, b = dv.astype(F32), a.astype(F32), b.astype(F32)
        sg = _sigmoid(a)
        return d * b * (sg * (1.0 + a * (1.0 - sg))), d * (a * sg)

    tile = pl.BlockSpec((tm, tf), lambda i, j: (i, j))
    return _vcall(body, (T // tm, nf), (dact, ab, ab), [tile, tile, pl.BlockSpec((tm, tf), lambda i, j: (i, j + nf))],
                  [_sds((T, F), BF16)] * 2, [tile, tile], name)


def _lat_fwd(lat, g_cq, g_ckv, name):
    T, LP = lat.shape
    QL, KVL = g_cq.shape[1], g_ckv.shape[1]
    tm = _pick(T, 512, 8)

    def body(v, gq, gk):
        xq, xk = v[:, :QL], v[:, QL:QL + KVL]
        return xq * _rstd(xq, QL) * gq, xk * _rstd(xk, KVL) * gk

    return _vcall(body, (T // tm,), (lat, g_cq, g_ckv), [_rows(tm, LP), _vec(QL), _vec(KVL)],
                  [_sds((T, QL), BF16), _sds((T, KVL), BF16)], [_rows(tm, QL), _rows(tm, KVL)], name)


def _lat_bwd(dcq, dckv, dkr, lat, g_cq, g_ckv, name):
    T, LP = lat.shape
    QL, KVL = g_cq.shape[1], g_ckv.shape[1]
    tm = _pick(T, 512, 8)

    def body(dq, dk, dr, v, gq, gk):
        xq, xk = v[:, :QL], v[:, QL:QL + KVL]
        rq, rk = _rstd(xq, QL), _rstd(xk, KVL)
        dxq, dgq = _rms_bwd(xq * rq, rq, gq, dq, QL)
        dxk, dgk = _rms_bwd(xk * rk, rk, gk, dk, KVL)
        return jnp.concatenate([dxq, dxk, dr], axis=1), dgq, dgk

    return _vcall(body, (T // tm,), (dcq, dckv, dkr, lat, g_cq, g_ckv),
                  [_rows(tm, QL), _rows(tm, KVL), _rows(tm, LANES), _rows(tm, LP), _vec(QL), _vec(KVL)],
                  [_sds((T, LP), BF16), _sds((1, QL), F32), _sds((1, KVL), F32)],
                  [_rows(tm, LP), _vec(QL), _vec(KVL)], name, n_acc=2)


def _head_spec(tm, w):
    return pl.BlockSpec((tm, w), lambda i, h: (i, h))


def _row2(tm, w, col=0):
    return pl.BlockSpec((tm, w), lambda i, h: (i, col))


def _vec2(w):
    return pl.BlockSpec((1, w), lambda i, h: (0, 0))


def _mla_q_prep(q_raw, g_q, tabs, H, scale, name):
    T = q_raw.shape[0]
    tm = _pick(T, 512, 8)
    half = MLA_ROPE // 2

    def body(x, g, c, sa, sb):
        n = x * _rstd(x, MLA_QK) * g
        return jnp.concatenate([n[:, :LANES], _rope(n[:, LANES:], c, sa, sb, half)], axis=1) * scale

    return _vcall(body, (T // tm, H), (q_raw, g_q) + tabs,
                  [_head_spec(tm, MLA_QK_PAD), _vec2(MLA_QK_PAD)] + [_row2(tm, LANES)] * 3,
                  [_sds((T, H * MLA_QK_PAD), BF16)], [_head_spec(tm, MLA_QK_PAD)], name)[0]


def _mla_q_bwd(dq, q_raw, g_q, tabs, H, scale, name):
    T = q_raw.shape[0]
    tm = _pick(T, 512, 8)
    half = MLA_ROPE // 2

    def body(d, x, g, c, sa, sb):
        r = _rstd(x, MLA_QK)
        d = d * scale
        dn = jnp.concatenate([d[:, :LANES], _rope_t(d[:, LANES:], c, sa, sb, half)], axis=1)
        return _rms_bwd(x * r, r, g, dn, MLA_QK)

    return _vcall(body, (T // tm, H), (dq, q_raw, g_q) + tabs,
                  [_head_spec(tm, MLA_QK_PAD), _head_spec(tm, MLA_QK_PAD), _vec2(MLA_QK_PAD)] + [_row2(tm, LANES)] * 3,
                  [_sds((T, H * MLA_QK_PAD), BF16), _sds((1, MLA_QK_PAD), F32)],
                  [_head_spec(tm, MLA_QK_PAD), _vec2(MLA_QK_PAD)], name, n_acc=1)


def _mla_k_prep(kv, lat, kr_col, g_k, tabs, H, name):
    T = kv.shape[0]
    tm = _pick(T, 512, 8)
    half = MLA_ROPE // 2

    def body(x, kr, g, c, sa, sb):
        kn = x[:, :LANES]
        r = lax.rsqrt((jnp.sum(kn * kn, axis=-1, keepdims=True) + jnp.sum(kr * kr, axis=-1, keepdims=True))
                      * (1.0 / MLA_QK) + EPS)
        k0 = kn * r * g[:, :LANES]
        k1 = _rope(kr * r * g[:, LANES:], c, sa, sb, half)
        return jnp.concatenate([k0, k1], axis=1), x[:, LANES:]

    return _vcall(body, (T // tm, H), (kv, lat, g_k) + tabs,
                  [_head_spec(tm, 2 * LANES), _row2(tm, LANES, kr_col), _vec2(MLA_QK_PAD)] + [_row2(tm, LANES)] * 3,
                  [_sds((T, H * MLA_QK_PAD), BF16), _sds((T, H * MLA_V), BF16)],
                  [_head_spec(tm, MLA_QK_PAD), _head_spec(tm, MLA_V)], name)


def _mla_k_bwd(dk, dv, kv, lat, kr_col, g_k, tabs, H, name):
    T = kv.shape[0]
    tm = _pick(T, 512, 8)
    half = MLA_ROPE // 2

    def body(d, dvv, x, kr, g, c, sa, sb):
        xx = jnp.concatenate([x[:, :LANES], kr], axis=1)
        r = _rstd(xx, MLA_QK)
        dn = jnp.concatenate([d[:, :LANES], _rope_t(d[:, LANES:], c, sa, sb, half)], axis=1)
        dx, dg = _rms_bwd(xx * r, r, g, dn, MLA_QK)
        return jnp.concatenate([dx[:, :LANES], dvv], axis=1), dx[:, LANES:], dg

    return _vcall(body, (T // tm, H), (dk, dv, kv, lat, g_k) + tabs,
                  [_head_spec(tm, MLA_QK_PAD), _head_spec(tm, MLA_V), _head_spec(tm, 2 * LANES),
                   _row2(tm, LANES, kr_col), _vec2(MLA_QK_PAD)] + [_row2(tm, LANES)] * 3,
                  [_sds((T, H * 2 * LANES), BF16), _sds((T, LANES), F32), _sds((1, MLA_QK_PAD), F32)],
                  [_head_spec(tm, 2 * LANES), _row2(tm, LANES), _vec2(MLA_QK_PAD)], name, n_inner_acc=1, n_acc=1)


def _dil_prep(pd, g_q, g_k, tabs, HD, scale, name):
    T = pd.shape[0]
    W = HD * DIL_HEAD
    G = len(DIL_GROUPS)
    tm = _pick(T, 256, 8)
    half = DIL_ROT // 2

    def body(xq, xk, xv, gq, gk, c, sa, sb):
        outs = []
        for x, g, s in ((xq, gq, scale), (xk, gk, 1.0)):
            heads = []
            for h in range(HD):
                xs = x[:, h * DIL_HEAD:(h + 1) * DIL_HEAD].astype(F32)
                n = _rope(xs * _rstd(xs, DIL_HEAD) * g, c, sa, sb, half)
                heads.append(n * s if s != 1.0 else n)
            outs.append(jnp.concatenate(heads, axis=1))
        return outs[0], outs[1], xv

    gspec = pl.BlockSpec((None, 1, DIL_HEAD), lambda i, g: (g, 0, 0))
    return _vcall(body, (T // tm, G), (pd, pd, pd, g_q, g_k) + tabs,
                  [pl.BlockSpec((tm, W), lambda i, g: (i, 3 * g)), pl.BlockSpec((tm, W), lambda i, g: (i, 3 * g + 1)),
                   pl.BlockSpec((tm, W), lambda i, g: (i, 3 * g + 2)), gspec, gspec] + [_row2(tm, LANES)] * 3,
                  [_sds((T, G * W), F32)] * 3, [pl.BlockSpec((tm, W), lambda i, g: (i, g))] * 3, name)


def _dil_prep_bwd(dq, dk, dv, pd, grp, g_q, g_k, tabs, HD, scale, name):
    T = pd.shape[0]
    W = HD * DIL_HEAD
    tm = _pick(T, 256, 8)
    half = DIL_ROT // 2

    def body(dqv, dkv, dvv, xq, xk, gq, gk, c, sa, sb):
        cols, dgs = [], []
        for d, x, g, s in ((dqv, xq, gq, scale), (dkv, xk, gk, 1.0)):
            heads, dg = [], None
            for h in range(HD):
                sl = slice(h * DIL_HEAD, (h + 1) * DIL_HEAD)
                xs = x[:, sl].astype(F32)
                r = _rstd(xs, DIL_HEAD)
                dh = d[:, sl] * s if s != 1.0 else d[:, sl]
                dx, dgh = _rms_bwd(xs * r, r, g, _rope_t(dh, c, sa, sb, half), DIL_HEAD)
                heads.append(dx)
                dg = dgh if dg is None else dg + dgh
            cols.append(jnp.concatenate(heads, axis=1))
            dgs.append(dg)
        return jnp.concatenate(cols + [dvv], axis=1), dgs[0], dgs[1]

    gq, gk = g_q[grp], g_k[grp]
    return _vcall(body, (T // tm,), (dq, dk, dv, pd, pd, gq, gk) + tabs,
                  [_rows(tm, W)] * 3 + [pl.BlockSpec((tm, W), lambda i: (i, 3 * grp)),
                                        pl.BlockSpec((tm, W), lambda i: (i, 3 * grp + 1)),
                                        _vec(DIL_HEAD), _vec(DIL_HEAD)] + [_rows(tm, LANES)] * 3,
                  [_sds((T, 3 * W), BF16), _sds((1, DIL_HEAD), F32), _sds((1, DIL_HEAD), F32)],
                  [_rows(tm, 3 * W), _vec(DIL_HEAD), _vec(DIL_HEAD)], name, n_acc=2)


def _dil_merge(os_, lses, name):
    T, W = os_[0].shape
    tm = _pick(T, 256, 8)

    def body(o0, o1, o2, l0, l1, l2):
        m = jnp.maximum(jnp.maximum(l0, l1), l2)
        w0, w1, w2 = jnp.exp(l0 - m), jnp.exp(l1 - m), jnp.exp(l2 - m)
        z = w0 + w1 + w2
        return (w0 * o0 + w1 * o1 + w2 * o2) / z, m + jnp.log(z)

    return _vcall(body, (T // tm,), tuple(os_) + tuple(lses), [_rows(tm, W)] * 6,
                  [_sds((T, W), BF16), _sds((T, W), F32)], [_rows(tm, W)] * 2, name)


def _gate_merge(pg, bm, bd, name):
    T, D = bm.shape
    tm = _pick(T, 256, 8)

    def body(g, m, d):
        g = g.astype(F32)
        return _sigmoid(g[:, :D]) * m + _sigmoid(g[:, D:]) * d

    return _vcall(body, (T // tm,), (pg, bm, bd), [_rows(tm, 2 * D), _rows(tm, D), _rows(tm, D)],
                  [_sds((T, D), BF16)], [_rows(tm, D)], name)[0]


def _gate_bwd(dmerged, pg, bm, bd, name):
    T, D = bm.shape
    tm = _pick(T, 256, 8)

    def body(dm, g, m, d):
        g = g.astype(F32)
        s0, s1 = _sigmoid(g[:, :D]), _sigmoid(g[:, D:])
        dpg = jnp.concatenate([dm * m * s0 * (1.0 - s0), dm * d * s1 * (1.0 - s1)], axis=1)
        return dm * s0, dm * s1, dpg

    return _vcall(body, (T // tm,), (dmerged, pg, bm, bd), [_rows(tm, D), _rows(tm, 2 * D), _rows(tm, D), _rows(tm, D)],
                  [_sds((T, D), BF16), _sds((T, D), BF16), _sds((T, 2 * D), BF16)],
                  [_rows(tm, D), _rows(tm, D), _rows(tm, 2 * D)], name)


def _ple_loss(x3, zg, pp, target, name):
    T, D = x3.shape
    tm = _pick(T, 256, 8)

    def body(x, z, p_, t):
        s = _sigmoid(z)
        e = x + s * p_ - t
        dy = e * (1.0 / D)
        part = 0.5 * jnp.sum(jnp.sum(e * e, axis=1, keepdims=True), axis=0, keepdims=True) * (1.0 / D)
        return dy, dy * s, dy * p_ * s * (1.0 - s), jnp.broadcast_to(part, (1, LANES))

    return _vcall(body, (T // tm,), (x3, zg, pp, target), [_rows(tm, D)] * 4,
                  [_sds((T, D), F32), _sds((T, D), BF16), _sds((T, D), BF16), _sds((1, LANES), F32)],
                  [_rows(tm, D)] * 3 + [_vec(LANES)], name, n_acc=1)


NT = (((1,), (1,)), ((), ()))
TN = (((0,), (0,)), ((), ()))


def _diag_mask(s):
    row = lax.broadcasted_iota(jnp.int32, s.shape, 0)
    col = lax.broadcasted_iota(jnp.int32, s.shape, 1)
    return jnp.where(col <= row, s, NEG)


def _causal_pairs(nq, key_major):
    if key_major:
        pairs = [(i, j) for j in range(nq) for i in range(j, nq)]
    else:
        pairs = [(i, j) for i in range(nq) for j in range(i + 1)]
    return (jnp.asarray([pr[0] for pr in pairs], jnp.int32), jnp.asarray([pr[1] for pr in pairs], jnp.int32))


def _mla_fwd(q, k, v, H, name, side=None):
    T = q.shape[0]
    tq = _pick(T, 512)
    nq = T // tq
    hb = 2 if H % 2 == 0 else 1
    qi_tab, kj_tab = _causal_pairs(nq, key_major=False)

    def kern(qi_ref, kj_ref, q_ref, k_ref, v_ref, o_ref, lse_ref, m_sc, l_sc, acc_sc):
        t = pl.program_id(1)
        qi, kj = qi_ref[t], kj_ref[t]

        @pl.when(kj == 0)
        def _():
            m_sc[...] = jnp.full_like(m_sc, NEG)
            l_sc[...] = jnp.zeros_like(l_sc)
            acc_sc[...] = jnp.zeros_like(acc_sc)

        def tile(diagonal):
            for hh in range(hb):
                qs = slice(hh * MLA_QK_PAD, (hh + 1) * MLA_QK_PAD)
                vs = slice(hh * MLA_V, (hh + 1) * MLA_V)
                s = lax.dot_general(q_ref[:, qs], k_ref[:, qs], NT, preferred_element_type=F32)
                if diagonal:
                    s = _diag_mask(s)
                m_prev = m_sc[hh]
                m_new = jnp.maximum(m_prev, jnp.max(s, axis=1, keepdims=True))
                alpha = jnp.exp(m_prev - m_new)
                p = jnp.exp(s - m_new)
                l_new = alpha * l_sc[hh] + jnp.sum(p, axis=1, keepdims=True)
                acc = alpha * acc_sc[hh] + jnp.dot(p.astype(BF16), v_ref[:, vs], preferred_element_type=F32)
                if diagonal:
                    o_ref[:, vs] = (acc / l_new).astype(o_ref.dtype)
                    lse_ref[:, hh * LANES:(hh + 1) * LANES] = jnp.broadcast_to(m_new + jnp.log(l_new), (tq, LANES))
                else:
                    m_sc[hh] = m_new
                    l_sc[hh] = l_new
                    acc_sc[hh] = acc

        @pl.when(kj < qi)
        def _():
            tile(False)

        @pl.when(kj == qi)
        def _():
            tile(True)

    qspec = lambda w: pl.BlockSpec((tq, hb * w), lambda h, t, qi_ref, kj_ref: (qi_ref[t], h))
    kspec = lambda w: pl.BlockSpec((tq, hb * w), lambda h, t, qi_ref, kj_ref: (kj_ref[t], h))
    n_pairs = qi_tab.shape[0]
    s_in, s_out, s_sems, s_alias = _side_parts(side, 5, 2)
    grid_spec = pltpu.PrefetchScalarGridSpec(
        num_scalar_prefetch=2, grid=(H // hb, n_pairs),
        in_specs=[qspec(MLA_QK_PAD), kspec(MLA_QK_PAD), kspec(MLA_V)] + [ANY] * len(s_in),
        out_specs=[qspec(MLA_V), qspec(LANES)] + [ANY] * len(s_out),
        scratch_shapes=[pltpu.VMEM((hb, tq, 1), F32), pltpu.VMEM((hb, tq, 1), F32),
                        pltpu.VMEM((hb, tq, MLA_V), F32)] + s_sems)
    step_of = lambda: pl.program_id(0) * n_pairs + pl.program_id(1)
    return pl.pallas_call(
        _carry(kern, side, 5, 2, 3, step_of, (H // hb) * n_pairs), name=name, grid_spec=grid_spec,
        out_shape=[_sds((T, H * MLA_V), BF16), _sds((T, H * LANES), F32)] + s_out, input_output_aliases=s_alias,
        compiler_params=_params(2))(qi_tab, kj_tab, q, k, v, *s_in)


def _mla_bwd(q, k, v, do, o, lse, H, name, side=None):
    T = q.shape[0]
    tq = _pick(T, 512)
    nq = T // tq
    qi_tab, kj_tab = _causal_pairs(nq, key_major=True)

    def kern(qi_ref, kj_ref, q_ref, k_ref, v_ref, do_ref, o_ref, lse_ref, dq_ref, dk_ref, dv_ref, dk_sc, dv_sc):
        t = pl.program_id(1)
        qi, kj = qi_ref[t], kj_ref[t]
        rows = pl.ds(pl.multiple_of(qi * tq, tq), tq)

        def tile(diagonal):
            s = lax.dot_general(q_ref[...], k_ref[...], NT, preferred_element_type=F32)
            if diagonal:
                s = _diag_mask(s)
            p = jnp.exp(s - lse_ref[:, :1])
            dl = jnp.sum(do_ref[...].astype(F32) * o_ref[...].astype(F32), axis=1, keepdims=True)
            dp = lax.dot_general(do_ref[...], v_ref[...], NT, preferred_element_type=F32)
            ds = (p * (dp - dl)).astype(BF16)
            dv = lax.dot_general(p.astype(BF16), do_ref[...], TN, preferred_element_type=F32)
            dk = lax.dot_general(ds, q_ref[...], TN, preferred_element_type=F32)
            dq = jnp.dot(ds, k_ref[...], preferred_element_type=F32)
            if diagonal:
                dv_sc[...] = dv
                dk_sc[...] = dk
            else:
                dv_sc[...] += dv
                dk_sc[...] += dk

            @pl.when(kj == 0)
            def _():
                dq_ref[rows, :] = dq

            @pl.when(kj > 0)
            def _():
                dq_ref[rows, :] += dq

        @pl.when(qi == kj)
        def _():
            tile(True)

        @pl.when(qi > kj)
        def _():
            tile(False)

        @pl.when(qi == nq - 1)
        def _():
            dk_ref[...] = dk_sc[...]
            dv_ref[...] = dv_sc[...]

    qspec = lambda w: pl.BlockSpec((tq, w), lambda h, t, qi_ref, kj_ref: (qi_ref[t], h))
    kspec = lambda w: pl.BlockSpec((tq, w), lambda h, t, qi_ref, kj_ref: (kj_ref[t], h))
    n_pairs = qi_tab.shape[0]
    s_in, s_out, s_sems, s_alias = _side_parts(side, 8, 3)
    grid_spec = pltpu.PrefetchScalarGridSpec(
        num_scalar_prefetch=2, grid=(H, n_pairs),
        in_specs=[qspec(MLA_QK_PAD), kspec(MLA_QK_PAD), kspec(MLA_V), qspec(MLA_V), qspec(MLA_V), qspec(LANES)]
        + [ANY] * len(s_in),
        out_specs=[pl.BlockSpec((T, MLA_QK_PAD), lambda h, t, qi_ref, kj_ref: (0, h)), kspec(MLA_QK_PAD), kspec(MLA_V)]
        + [ANY] * len(s_out),
        scratch_shapes=[pltpu.VMEM((tq, MLA_QK_PAD), F32), pltpu.VMEM((tq, MLA_V), F32)] + s_sems)
    step_of = lambda: pl.program_id(0) * n_pairs + pl.program_id(1)
    return pl.pallas_call(
        _carry(kern, side, 8, 3, 2, step_of, H * n_pairs), name=name, grid_spec=grid_spec,
        out_shape=[_sds((T, H * MLA_QK_PAD), F32), _sds((T, H * MLA_QK_PAD), F32), _sds((T, H * MLA_V), F32)] + s_out,
        input_output_aliases=s_alias, compiler_params=_params(2))(qi_tab, kj_tab, q, k, v, do, o, lse, *s_in)


class _DilGeometry:
    def __init__(self, T, dil, HD, grp):
        self.dil, self.sub = dil, max(1, 8 // dil)
        self.tb = self.sub * DIL_BLOCK * dil
        assert T % self.tb == 0, (T, dil)
        self.nblk = T // self.tb
        last = self.nblk - 1
        self.cur_g = pl.BlockSpec((self.tb, DIL_HEAD), lambda i, h: (i, grp * HD + h))
        self.prev_g = pl.BlockSpec((self.tb, DIL_HEAD), lambda i, h: (jnp.maximum(i - 1, 0), grp * HD + h))
        self.next_g = pl.BlockSpec((self.tb, DIL_HEAD), lambda i, h: (jnp.minimum(i + 1, last), grp * HD + h))
        self.cur = pl.BlockSpec((self.tb, DIL_HEAD), lambda i, h: (i, h))
        self.next = pl.BlockSpec((self.tb, DIL_HEAD), lambda i, h: (jnp.minimum(i + 1, last), h))

    def rows(self, b, r):
        if self.dil == 1:
            return pl.ds(b * DIL_BLOCK, DIL_BLOCK)
        return pl.ds(b * DIL_BLOCK * self.dil + r, DIL_BLOCK, stride=self.dil)

    def tiles(self):
        return [(b, r) for b in range(self.sub) for r in range(self.dil)]

    def masks(self, i):
        row = lax.broadcasted_iota(jnp.int32, (DIL_BLOCK, DIL_BLOCK), 0)
        col = lax.broadcasted_iota(jnp.int32, (DIL_BLOCK, DIL_BLOCK), 1)
        no_prev = jnp.where(i > 0, 0, 2 * DIL_BLOCK)
        no_next = jnp.where(i + 1 < self.nblk, 0, 2 * DIL_BLOCK)
        return col <= row, col >= row, col >= row + no_prev, col >= row + no_next


def _dil_fwd(qd, kd, vd, grp, dil, HD, name):
    T = qd.shape[0]
    W = HD * DIL_HEAD
    geo = _DilGeometry(T, dil, HD, grp)

    def kern(q_ref, kc_ref, kp_ref, vc_ref, vp_ref, o_ref, lse_ref):
        ok_cur, ok_prev, ok_first, _ = geo.masks(pl.program_id(0))
        for b, r in geo.tiles():
            R = geo.rows(b, r)
            q = q_ref[R, :].astype(BF16)
            if b == 0:
                Rp = geo.rows(geo.sub - 1, r)
                kp, vp, ok = kp_ref[Rp, :], vp_ref[Rp, :], ok_first
            else:
                Rp = geo.rows(b - 1, r)
                kp, vp, ok = kc_ref[Rp, :], vc_ref[Rp, :], ok_prev
            sa = jnp.where(ok, lax.dot_general(q, kp.astype(BF16), NT, preferred_element_type=F32), NEG)
            sb = jnp.where(ok_cur, lax.dot_general(q, kc_ref[R, :].astype(BF16), NT, preferred_element_type=F32), NEG)
            m = jnp.maximum(jnp.max(sa, axis=1, keepdims=True), jnp.max(sb, axis=1, keepdims=True))
            ea, eb = jnp.exp(sa - m), jnp.exp(sb - m)
            l = jnp.sum(ea, axis=1, keepdims=True) + jnp.sum(eb, axis=1, keepdims=True)
            acc = (jnp.dot(ea.astype(BF16), vp.astype(BF16), preferred_element_type=F32)
                   + jnp.dot(eb.astype(BF16), vc_ref[R, :].astype(BF16), preferred_element_type=F32))
            o_ref[R, :] = acc / l
            lse_ref[R, :] = jnp.broadcast_to(m + jnp.log(l), (DIL_BLOCK, DIL_HEAD))

    return pl.pallas_call(
        kern, name=name, grid=(geo.nblk, HD), in_specs=[geo.cur_g, geo.cur_g, geo.prev_g, geo.cur_g, geo.prev_g],
        out_specs=[geo.cur, geo.cur], out_shape=[_sds((T, W), F32)] * 2,
        compiler_params=_params(2))(qd, kd, kd, vd, vd)


def _dil_delta(do, o, HD, name):
    T, W = do.shape
    tm = _pick(T, 512, 8)

    def body(d, ov):
        prod = d * ov.astype(F32)
        return jnp.concatenate(
            [jnp.broadcast_to(jnp.sum(prod[:, h * DIL_HEAD:(h + 1) * DIL_HEAD], axis=1, keepdims=True), (tm, DIL_HEAD))
             for h in range(HD)], axis=1)

    return _vcall(body, (T // tm,), (do, o), [_rows(tm, W)] * 2, [_sds((T, W), F32)], [_rows(tm, W)], name)[0]


def _dil_bwd(qd, kd, vd, do, delta, lse, grp, dil, HD, name):
    T = qd.shape[0]
    W = HD * DIL_HEAD
    geo = _DilGeometry(T, dil, HD, grp)

    def kern(q_ref, k_ref, v_ref, do_ref, dl_ref, ls_ref, kp_ref, vp_ref, qn_ref, don_ref, dln_ref, lsn_ref,
             dq_ref, dk_ref, dv_ref):
        ok_cur, ok_prev, ok_first, ok_next = geo.masks(pl.program_id(0))

        def tile(q, do_, dl, ls, k, v, ok):
            s = jnp.where(ok, lax.dot_general(q, k, NT, preferred_element_type=F32), NEG)
            p = jnp.exp(s - ls)
            ds = p * (lax.dot_general(do_, v, NT, preferred_element_type=F32) - dl)
            return ds.astype(BF16), p.astype(BF16)

        for b, r in geo.tiles():
            R = geo.rows(b, r)
            q, do_ = q_ref[R, :].astype(BF16), do_ref[R, :].astype(BF16)
            k, v = k_ref[R, :].astype(BF16), v_ref[R, :].astype(BF16)
            dl, ls = dl_ref[R, :], ls_ref[R, :]
            ds, p_ = tile(q, do_, dl, ls, k, v, ok_cur)
            dq = jnp.dot(ds, k, preferred_element_type=F32)
            dk_ref[R, :] = lax.dot_general(ds, q, TN, preferred_element_type=F32)
            dv_ref[R, :] = lax.dot_general(p_, do_, TN, preferred_element_type=F32)
            if b == 0:
                Rp = geo.rows(geo.sub - 1, r)
                kp, vp = kp_ref[Rp, :].astype(BF16), vp_ref[Rp, :].astype(BF16)
                ds, p_ = tile(q, do_, dl, ls, kp, vp, ok_first)
            else:
                Rp = geo.rows(b - 1, r)
                kp, vp = k_ref[Rp, :].astype(BF16), v_ref[Rp, :].astype(BF16)
                ds, p_ = tile(q, do_, dl, ls, kp, vp, ok_prev)
                dk_ref[Rp, :] += lax.dot_general(ds, q, TN, preferred_element_type=F32)
                dv_ref[Rp, :] += lax.dot_general(p_, do_, TN, preferred_element_type=F32)
            dq_ref[R, :] = dq + jnp.dot(ds, kp, preferred_element_type=F32)
        for r in range(dil):
            R, Rn = geo.rows(geo.sub - 1, r), geo.rows(0, r)
            qn, don = qn_ref[Rn, :].astype(BF16), don_ref[Rn, :].astype(BF16)
            ds, p_ = tile(qn, don, dln_ref[Rn, :], lsn_ref[Rn, :], k_ref[R, :].astype(BF16), v_ref[R, :].astype(BF16),
                          ok_next)
            dk_ref[R, :] += lax.dot_general(ds, qn, TN, preferred_element_type=F32)
            dv_ref[R, :] += lax.dot_general(p_, don, TN, preferred_element_type=F32)

    return pl.pallas_call(
        kern, name=name, grid=(geo.nblk, HD),
        in_specs=[geo.cur_g, geo.cur_g, geo.cur_g, geo.cur, geo.cur, geo.cur, geo.prev_g, geo.prev_g,
                  geo.next_g, geo.next, geo.next, geo.next],
        out_specs=[geo.cur] * 3, out_shape=[_sds((T, W), F32)] * 3,
        compiler_params=_params(2))(qd, kd, vd, do, delta, lse, kd, vd, qd, do, delta, lse)


def _place():
    return lax.axis_index("x"), lax.axis_index("y"), lax.axis_index("c")


def _other_chips(x, y):
    return [(1 - x, y), (x, 1 - y), (1 - x, 1 - y)]


def _kind(name, shard_shape):
    if name in ROW_SHARDED:
        return "row"
    return "col" if shard_shape[1] % LANES == 0 else "stack"


def _remote(src, dst, send_sem, recv_sem, to):
    return pltpu.make_async_remote_copy(src_ref=src, dst_ref=dst, send_sem=send_sem, recv_sem=recv_sem,
                                        device_id=to, device_id_type=MESH_ID)


def _row_tile(rows, cols, itemsize, align):
    return _pick(rows, max(align, (2 * 1024 * 1024) // (cols * itemsize)), align)


def _dma_sems(n):
    return [pltpu.SemaphoreType.DMA((n,)), pltpu.SemaphoreType.DMA((n,))]


def _gather_plan(shard_shapes):
    info, buf_shapes = {}, {}
    for out_name, names in GATHER_PLAN:
        r, c = shard_shapes[names[0]]
        kind = _kind(names[0], (r, c))
        assert kind == "col" or len(names) == 1, out_name
        buf_shapes[out_name] = (r, 4 * c * len(names)) if kind == "col" else (4, r, c)
        for i, n in enumerate(names):
            assert tuple(shard_shapes[n]) == (r, c), n
            info[n] = (out_name, kind, i * 4 * c, r, c)
    return info, buf_shapes


def _place_own(shard, buf_shape, kind, base, me, name, prev=None):
    r, c = shard.shape
    tr = _row_tile(r, c, 2, 16)

    def kern(me_ref, x_ref, *rest):
        rest[-1][...] = x_ref[...]

    if kind == "col":
        out_spec = pl.BlockSpec((tr, c), lambda i, me_ref: (i, base // c + me_ref[0]))
    else:
        out_spec = pl.BlockSpec((None, tr, c), lambda i, me_ref: (me_ref[0], i, 0))
    in_specs = [pl.BlockSpec((tr, c), lambda i, me_ref: (i, 0))] + ([ANY] if prev is not None else [])
    grid_spec = pltpu.PrefetchScalarGridSpec(num_scalar_prefetch=1, grid=(r // tr,), in_specs=in_specs,
                                             out_specs=out_spec)
    args = (me, shard) + ((prev,) if prev is not None else ())
    return pl.pallas_call(kern, name=name, grid_spec=grid_spec, out_shape=_sds(buf_shape, shard.dtype),
                          input_output_aliases={2: 0} if prev is not None else {}, compiler_params=_params(1))(*args)


def _buffers_of(names, info):
    out_names = []
    for n in names:
        if info[n][0] not in out_names:
            out_names.append(info[n][0])
    return out_names


def _ag_side(names, shards, bufs, info):
    out_names = _buffers_of(names, info)
    n_w = len(names)

    def region(outs, w, chip, h):
        out_name, kind, base, r, cc = info[names[w]]
        o = outs[out_names.index(out_name)]
        rows = pl.ds(h * (r // 2), r // 2)
        if kind == "col":
            return o.at[rows, pl.ds(pl.multiple_of(base + chip * cc, LANES), cc)]
        return o.at[chip, rows, :]

    def hop(first, sending):
        def fn(ins, outs, sems):
            x, y, c = _place()
            me, sibling, cps = 2 * x + y, (x, y, 1 - c), []
            for w in range(n_w):
                r = info[names[w]][3]
                for j, (px, py) in enumerate(_other_chips(x, y)):
                    k = 3 * w + j + (0 if first else 3 * n_w)
                    if first and sending:
                        src, dst, to = ins[w].at[pl.ds(c * (r // 2), r // 2), :], region(outs, w, me, c), (px, py, c)
                    elif first:
                        src = dst = region(outs, w, 2 * px + py, c)
                        to = (px, py, c)
                    else:
                        src = dst = region(outs, w, 2 * px + py, c if sending else 1 - c)
                        to = sibling
                    cps.append(_remote(src, dst, sems[0].at[k], sems[1].at[k], to))
            return cps
        return fn

    return _Side([shards[n] for n in names] + [bufs[o] for o in out_names],
                 [_sds(bufs[o].shape, bufs[o].dtype) for o in out_names], {n_w + i: i for i in range(len(out_names))},
                 _dma_sems(6 * n_w), [(hop(True, True), hop(True, False)), (hop(False, True), hop(False, False))])


def _rs_sibling_side(views):
    n = len(views)

    def fn(sending):
        def copies(ins, outs, sems):
            x, y, c = _place()
            return [_remote(ins[w].at[:, 1 - c] if sending else outs[w], outs[w], sems[0].at[w], sems[1].at[w],
                            (x, y, 1 - c)) for w in range(n)]
        return copies

    return _Side(views, [_sds((v.shape[0],) + v.shape[2:], v.dtype) for v in views], {}, _dma_sems(n),
                 [(fn(True), fn(False))])


def _rs_chips_side(parts, kinds, widths):
    n = len(parts)

    def piece(ins, w, chip):
        if kinds[w] == "col":
            return ins[w].at[0, :, pl.ds(pl.multiple_of(chip * widths[w], LANES), widths[w])]
        return ins[w].at[chip]

    def fn(sending):
        def copies(ins, outs, sems):
            x, y, c = _place()
            cps = []
            for w in range(n):
                for j, (px, py) in enumerate(_other_chips(x, y)):
                    k = 3 * w + j
                    src = piece(ins, w, 2 * px + py) if sending else outs[w].at[j]
                    cps.append(_remote(src, outs[w].at[j], sems[0].at[k], sems[1].at[k], (px, py, c)))
            return cps
        return copies

    return _Side(parts, [_sds((3, p_.shape[1], widths[w]), p_.dtype) for w, p_ in enumerate(parts)], {},
                 _dma_sems(3 * n), [(fn(True), fn(False))])


def _rs_join_side(halves):
    n = len(halves)

    def fn(sending):
        def copies(ins, outs, sems):
            x, y, c = _place()
            return [_remote(ins[w] if sending else outs[w], outs[w], sems[0].at[w], sems[1].at[w], (x, y, 1 - c))
                    for w in range(n)]
        return copies

    return _Side(halves, [_sds(h.shape, h.dtype) for h in halves], {}, _dma_sems(n), [(fn(True), fn(False))])


def _pair_sum(g, got, c_idx, name):
    n, _, rows, C = g.shape
    tr = _row_tile(rows, C, 4, 16)

    def kern(c_ref, a_ref, b_ref, o_ref):
        o_ref[...] = (a_ref[...] + b_ref[...]).astype(o_ref.dtype)

    grid_spec = pltpu.PrefetchScalarGridSpec(
        num_scalar_prefetch=1, grid=(n, rows // tr),
        in_specs=[pl.BlockSpec((None, None, tr, C), lambda j, i, c_ref: (j, c_ref[0], i, 0)),
                  pl.BlockSpec((None, tr, C), lambda j, i, c_ref: (j, i, 0))],
        out_specs=pl.BlockSpec((None, tr, C), lambda j, i, c_ref: (j, i, 0)))
    return pl.pallas_call(kern, name=name, grid_spec=grid_spec, out_shape=_sds((n, rows, C), BF16),
                          compiler_params=_params(2))(c_idx, g, got)


def _sum_pieces(pair, recv, kind, me, name):
    _, rows, c = recv.shape
    tr = _row_tile(rows, c, 8, 16)

    def kern(me_ref, own_ref, r_ref, o_ref):
        acc = own_ref[...].astype(F32)
        for j in range(3):
            acc = acc + r_ref[j].astype(F32)
        o_ref[...] = acc

    if kind == "col":
        own_spec = pl.BlockSpec((None, tr, c), lambda i, me_ref: (0, i, me_ref[0]))
    else:
        own_spec = pl.BlockSpec((None, tr, c), lambda i, me_ref: (me_ref[0], i, 0))
    grid_spec = pltpu.PrefetchScalarGridSpec(
        num_scalar_prefetch=1, grid=(rows // tr,),
        in_specs=[own_spec, pl.BlockSpec((3, tr, c), lambda i, me_ref: (0, i, 0))],
        out_specs=pl.BlockSpec((tr, c), lambda i, me_ref: (i, 0)))
    return pl.pallas_call(kern, name=name, grid_spec=grid_spec, out_shape=_sds((rows, c), F32),
                          compiler_params=_params(1))(me, pair, recv)


def _all_reduce_small(vec):
    N = vec.shape[1]
    n_dev = 8

    def body(v_ref, out_ref, slots, send_sems, recv_sems):
        x, y, c = _place()
        me = 4 * x + 2 * y + c
        slots[me] = v_ref[...]
        sent = []
        for k in range(1, n_dev):
            px, py, pc = x ^ (k >> 2), y ^ ((k >> 1) & 1), c ^ (k & 1)
            cp = pltpu.make_async_remote_copy(src_ref=v_ref, dst_ref=slots.at[me], send_sem=send_sems.at[k - 1],
                                              recv_sem=recv_sems.at[k - 1], device_id=(px, py, pc),
                                              device_id_type=MESH_ID)
            cp.start()
            sent.append(cp)
        for k in range(1, n_dev):
            px, py, pc = x ^ (k >> 2), y ^ ((k >> 1) & 1), c ^ (k & 1)
            slot = slots.at[4 * px + 2 * py + pc]
            pltpu.make_async_remote_copy(src_ref=slot, dst_ref=slot, send_sem=send_sems.at[k - 1],
                                         recv_sem=recv_sems.at[k - 1], device_id=(px, py, pc),
                                         device_id_type=MESH_ID).wait_recv()
        for cp in sent:
            cp.wait_send()
        acc = slots[0]
        for j in range(1, n_dev):
            acc = acc + slots[j]
        out_ref[...] = acc

    vm = pl.BlockSpec(memory_space=pltpu.VMEM)
    return pl.pallas_call(
        body, name="ar_gains", out_shape=_sds((1, N), F32), in_specs=[vm], out_specs=vm,
        scratch_shapes=[pltpu.VMEM((n_dev, 1, N), F32), pltpu.SemaphoreType.DMA((n_dev - 1,)),
                        pltpu.SemaphoreType.DMA((n_dev - 1,))])(vec)


def _adamw_math(wv, gv, mv, vv):
    m2 = ADAM_B1 * mv + (1.0 - ADAM_B1) * gv
    v2 = ADAM_B2 * vv + (1.0 - ADAM_B2) * (gv * gv)
    m_hat = m2 / (1.0 - ADAM_B1 ** ADAM_STEP)
    v_hat = v2 / (1.0 - ADAM_B2 ** ADAM_STEP)
    return -ADAM_LR * (m_hat / (jnp.sqrt(v_hat) + ADAM_EPS) + ADAM_WD * wv), m2, v2


def _adamw(w, g, m, v, name):
    R, C = w.shape
    tr = _row_tile(R, C, 8, 8)
    return _vcall(_adamw_math, (R // tr,), (w, g, m, v), [_rows(tr, C)] * 4, [_sds((R, C), F32)] * 3,
                  [_rows(tr, C)] * 3, name)


def _adamw_halves(w, own, recv, m, v, c_idx, name):
    R, C = w.shape
    rows = R // 2
    tr = _row_tile(rows, C, 8, 8)
    nb = rows // tr

    def kern(c_ref, w_ref, own_ref, recv_ref, m_ref, v_ref, g_out, d_out, m_out, v_out):
        def update(g_ref):
            g = g_ref[...]
            g_out[...] = g
            d_out[...], m_out[...], v_out[...] = _adamw_math(w_ref[...], g, m_ref[...], v_ref[...])

        @pl.when(pl.program_id(0) == c_ref[0])
        def _():
            update(own_ref)

        @pl.when(pl.program_id(0) != c_ref[0])
        def _():
            update(recv_ref)

    full = pl.BlockSpec((tr, C), lambda h, i, c_ref: (h * nb + i, 0))
    own_spec = pl.BlockSpec((tr, C), lambda h, i, c_ref: (jnp.where(h == c_ref[0], i, 0), 0))
    recv_spec = pl.BlockSpec((tr, C), lambda h, i, c_ref: (jnp.where(h == c_ref[0], 0, i), 0))
    grid_spec = pltpu.PrefetchScalarGridSpec(num_scalar_prefetch=1, grid=(2, nb),
                                             in_specs=[full, own_spec, recv_spec, full, full], out_specs=[full] * 4)
    return pl.pallas_call(kern, name=name, grid_spec=grid_spec, out_shape=[_sds((R, C), F32)] * 4,
                          compiler_params=_params(2))(c_idx, w, own, recv, m, v)


def _pad_to(a, n, axis):
    extra = n - a.shape[axis]
    if extra == 0:
        return a
    pads = [(0, 0)] * a.ndim
    pads[axis] = (0, extra)
    return jnp.pad(a, pads)


def _round_up(n, m):
    return -(-n // m) * m


def _natural(buf, kind):
    if kind == "col":
        return buf
    n, r, c = buf.shape
    return buf.reshape(n * r, c) if kind == "row" else buf.transpose(1, 0, 2).reshape(r, n * c)


def _halves_view(g, kind, shard_shape):
    r, c = shard_shape
    if kind == "col":
        return g.reshape(1, 2, r // 2, 4 * c)
    if kind == "stack":
        g = g.reshape(r, 4, c).transpose(1, 0, 2)
    return g.reshape(4, 2, r // 2, c)


def _pack_small(vals):
    return jnp.concatenate([_pad_to(vals[n].reshape(1, -1), _round_up(vals[n].size, LANES), 1) for n in SMALL], axis=1)


def _unpack_small(vec, shapes):
    out, off = {}, 0
    for n in SMALL:
        size = int(np.prod(shapes[n]))
        out[n] = vec[:, off:off + size].reshape(shapes[n])
        off += _round_up(size, LANES)
    return out


def _mm_s(sched, a, b, mode, out_dtype, name, **kw):
    side = sched.side(name)
    if side is None:
        return _mm(a, b, mode, out_dtype, name, **kw)
    out, side_outs = _mm(a, b, mode, out_dtype, name, side=side, **kw)
    sched.done(name, side_outs)
    return out


def _call_s(sched, name, n_out, fn):
    side = sched.side(name)
    outs = fn(side)
    if side is not None:
        sched.done(name, list(outs[n_out:]))
    return outs[:n_out]


def _ffn_fwd(sched, x, g, tf, tag):
    w = tag[-1]
    n = _rms_fwd(x, g, f"{tag}_norm")
    ab = _mm_s(sched, n, sched.weight(f"w{w}_gu"), "nn", BF16, f"{tag}_gate_up")
    act = _silu_mul(ab, tf, f"{tag}_act")
    out = _mm_s(sched, act, sched.weight(f"w{w}_down"), "nn", F32, f"{tag}_down", res=x, alpha=0.5)
    return out, (n, ab, act)


def _ffn_bwd(sched, dout, x, g, saved, tf, tag):
    w = tag[-1]
    w_gu, w_d = sched.weight(f"w{w}_gu"), sched.weight(f"w{w}_down")
    n, ab, act = saved
    F = act.shape[1]
    dout_b = dout.astype(BF16)
    sched.grad(f"w{w}_down", _mm_s(sched, act, dout_b, "tn", F32, f"{tag}_d_wdown", alpha=0.5))
    dact = _mm_s(sched, dout_b, w_d, "nt", BF16, f"{tag}_d_act", alpha=0.5)
    da, db = _swiglu_bwd(dact, ab, tf, f"{tag}_d_gate_up")
    sched.grad(f"w{w}_gate", _mm_s(sched, n, da, "tn", F32, f"{tag}_d_wgate"))
    sched.grad(f"w{w}_up", _mm_s(sched, n, db, "tn", F32, f"{tag}_d_wup"))
    dn = _mm_s(sched, da, w_gu, "nt", F32, f"{tag}_d_norm", a2=db, b2=w_gu, b2_k_offset=F)
    return _rms_bwd_call(x, g, dn, dout, f"{tag}_d_x")


def _local_step(sched, x, p, pos_b, target, Gn, dims):
    T, D = x.shape
    H, HD, QL, KVL, LP, tf = dims["H"], dims["HD"], dims["QL"], dims["KVL"], dims["LP"], dims["tf"]
    Wd = HD * DIL_HEAD
    scale_mla, scale_dil = MLA_QK ** -0.5, DIL_HEAD ** -0.5
    kr_col = (QL + KVL) // LANES
    tab_mla = tuple(_rope_tables(pos_b, MLA_ROPE, "rope_tab_mla"))
    tab_dil = tuple(_rope_tables(pos_b, DIL_ROT, "rope_tab_dil"))

    W = sched.weight
    mm = functools.partial(_mm_s, sched)

    x1, ffn1 = _ffn_fwd(sched, x, Gn["g_ffn1"], tf, "ffn1")
    h = _rms_fwd(x1, Gn["g_mix"], "mix_norm")
    lat = mm(h, W("w_lat"), "nn", F32, "proj_lat")
    pd = mm(h, W("w_dil"), "nn", BF16, "proj_dil")
    pg = mm(h, W("w_gin"), "nn", BF16, "proj_gate")

    cq, ckv = _lat_fwd(lat, Gn["g_cq"], Gn["g_ckv"], "lat_norm")
    q_raw = mm(cq, W("w_uq"), "nn", F32, "mla_q_up")
    kv = mm(ckv, W("w_ukv"), "nn", F32, "mla_kv_up")
    q = _mla_q_prep(q_raw, Gn["g_q_mla"], tab_mla, H, scale_mla, "mla_q_prep")
    k, v = _mla_k_prep(kv, lat, kr_col, Gn["g_k_mla"], tab_mla, H, "mla_k_prep")
    o_mla, lse_mla = _call_s(sched, "mla_attn", 2, lambda side: _mla_fwd(q, k, v, H, "mla_attn", side=side))

    qd, kd, vd = _dil_prep(pd, Gn["g_q_dil"], Gn["g_k_dil"], tab_dil, HD, scale_dil, "dil_prep")
    og, lg = [], []
    for grp, (win, dil) in enumerate(DIL_GROUPS):
        o_, l_ = _dil_fwd(qd, kd, vd, grp, dil, HD, f"dil_attn{grp}")
        og.append(o_)
        lg.append(l_)
    o_dil, lse_dil = _dil_merge(og, lg, "dil_merge")

    bm = mm(o_mla, W("w_br_mla"), "nn", F32, "branch_mla")
    bd = mm(o_dil, W("w_br_dil"), "nn", F32, "branch_dil")
    merged = _gate_merge(pg, bm, bd, "gate_merge")
    x2 = mm(merged, W("w_o"), "nn", F32, "out_proj", res=x1)

    x3, ffn2 = _ffn_fwd(sched, x2, Gn["g_ffn2"], tf, "ffn2")
    n4 = _rms_fwd(x3, Gn["g_ple"], "ple_norm")
    zg = mm(n4, W("w_ple_gate"), "nn", F32, "ple_gate")
    p_b = p.astype(BF16)
    pp = mm(p_b, W("w_ple_proj"), "nn", F32, "ple_proj")
    dy, dpp, dzg, loss = _ple_loss(x3, zg, pp, target, "ple_loss")

    gg = {}
    sched.grad("w_ple_proj", mm(p_b, dpp, "tn", F32, "d_w_ple_proj"))
    sched.grad("w_ple_gate", mm(n4, dzg, "tn", F32, "d_w_ple_gate"))
    dn4 = mm(dzg, W("w_ple_gate"), "nt", F32, "d_ple_norm")
    dx3, gg["g_ple"] = _rms_bwd_call(x3, Gn["g_ple"], dn4, dy, "d_x3")

    dx2, gg["g_ffn2"] = _ffn_bwd(sched, dx3, x2, Gn["g_ffn2"], ffn2, tf, "ffn2")

    dx2_b = dx2.astype(BF16)
    sched.grad("w_o", mm(merged, dx2_b, "tn", F32, "d_w_o"))
    dmerged = mm(dx2_b, W("w_o"), "nt", F32, "d_merged")
    dbm, dbd, dpg = _gate_bwd(dmerged, pg, bm, bd, "d_gate")
    sched.grad("w_br_mla", mm(o_mla, dbm, "tn", F32, "d_w_br_mla"))
    sched.grad("w_br_dil", mm(o_dil, dbd, "tn", F32, "d_w_br_dil"))
    do_mla = mm(dbm, W("w_br_mla"), "nt", BF16, "d_o_mla")
    do_dil = mm(dbd, W("w_br_dil"), "nt", F32, "d_o_dil")
    delta_dil = _dil_delta(do_dil, o_dil, HD, "dil_delta")

    dh = mm(dpg, W("w_gin"), "nt", F32, "d_h_gate")
    sched.grad("w_gin", mm(h, dpg, "tn", F32, "d_w_gin"))
    gq_d, gk_d = Gn["g_q_dil"], Gn["g_k_dil"]
    dgq_d, dgk_d = [], []
    for grp, (win, dil) in enumerate(DIL_GROUPS):
        dq_, dk_, dv_ = _dil_bwd(qd, kd, vd, do_dil, delta_dil, lse_dil, grp, dil, HD, f"dil_bwd{grp}")
        dpd_g, dgq_, dgk_ = _dil_prep_bwd(dq_, dk_, dv_, pd, grp, gq_d, gk_d, tab_dil, HD, scale_dil, f"d_dil_prep{grp}")
        dgq_d.append(dgq_)
        dgk_d.append(dgk_)
        w_g = W("w_dil")[:, grp * 3 * Wd:(grp + 1) * 3 * Wd]
        dh = mm(dpd_g, w_g, "nt", F32, f"d_h_dil{grp}", res=dh)
        sched.grad(f"w_dil{grp}", mm(h, dpd_g, "tn", F32, f"d_w_dil{grp}"))
    gg["g_q_dil"] = jnp.concatenate(dgq_d, axis=0)
    gg["g_k_dil"] = jnp.concatenate(dgk_d, axis=0)

    dq, dk, dv = _call_s(sched, "mla_bwd", 3,
                         lambda side: _mla_bwd(q, k, v, do_mla, o_mla, lse_mla, H, "mla_bwd", side=side))
    dq_raw, gg["g_q_mla"] = _mla_q_bwd(dq, q_raw, Gn["g_q_mla"], tab_mla, H, scale_mla, "d_mla_q_prep")
    dkv, dkr, gg["g_k_mla"] = _mla_k_bwd(dk, dv, kv, lat, kr_col, Gn["g_k_mla"], tab_mla, H, "d_mla_k_prep")
    sched.grad("w_uq", mm(cq, dq_raw, "tn", F32, "d_w_uq"))
    sched.grad("w_ukv", mm(ckv, dkv, "tn", F32, "d_w_ukv"))
    dcq = mm(dq_raw, W("w_uq"), "nt", F32, "d_cq")
    dckv = mm(dkv, W("w_ukv"), "nt", F32, "d_ckv")
    dlat, gg["g_cq"], gg["g_ckv"] = _lat_bwd(dcq, dckv, dkr, lat, Gn["g_cq"], Gn["g_ckv"], "d_lat_norm")
    dh = mm(dlat, W("w_lat"), "nt", F32, "d_h_lat", res=dh)
    sched.grad("w_lat", mm(h, dlat, "tn", F32, "d_w_lat"))

    dx1, gg["g_mix"] = _rms_bwd_call(x1, Gn["g_mix"], dh, dx2, "d_x1")
    dx, gg["g_ffn1"] = _ffn_bwd(sched, dx1, x, Gn["g_ffn1"], ffn1, tf, "ffn1")
    return loss, dx, gg


def _layout_weight(name, full, dims):
    H, QL, KVL, LP, Wd = dims["H"], dims["QL"], dims["KVL"], dims["LP"], dims["HD"] * DIL_HEAD
    off_dil = QL + KVL + MLA_ROPE
    off_gate = off_dil + 3 * len(DIL_GROUPS) * Wd
    if name == "w_lat":
        return _pad_to(full("w_in")[:, :off_dil], LP, 1)
    if name == "w_dil":
        return full("w_in")[:, off_dil:off_gate]
    if name == "w_gin":
        return full("w_in")[:, off_gate:]
    if name == "w_uq":
        return _pad_to(full("w_uq").reshape(QL, H, MLA_QK), MLA_QK_PAD, 2).reshape(QL, H * MLA_QK_PAD)
    return full(name)


def _natural_grad(name, gw, dims):
    H, QL, KVL = dims["H"], dims["QL"], dims["KVL"]
    if name == "w_in":
        return jnp.concatenate([gw["w_lat"][:, :QL + KVL + MLA_ROPE]] + [gw[f"w_dil{g}"] for g in range(len(DIL_GROUPS))]
                               + [gw["w_gin"]], axis=1)
    if name == "w_uq":
        return gw["w_uq"].reshape(QL, H, MLA_QK_PAD)[:, :, :MLA_QK].reshape(QL, H * MLA_QK)
    return gw[name]


WEIGHT_SOURCES = {"w1_gu": ("w1_gate", "w1_up"), "w2_gu": ("w2_gate", "w2_up"), "w_lat": ("w_in",), "w_dil": ("w_in",),
                  "w_gin": ("w_in",)}
AG_FIRST = ("w1_gate", "w1_up")
AG_RIDES = {"ffn1_gate_up": ("w1_down", "w_in"), "ffn1_down": ("w_uq", "w_ukv"),
            "mla_attn": ("w_br_mla", "w_br_dil", "w_o", "w_ple_gate", "w_ple_proj", "w2_gate", "w2_up", "w2_down")}
RS_FFN2 = ("w_ple_proj", "w_ple_gate", "w2_down", "w2_gate", "w2_up")
RS_MIXER = ("w_o", "w_br_mla", "w_br_dil", "w_in", "w_uq", "w_ukv")
RS_RIDES = {
    "d_merged": (("sibling", RS_FFN2),),
    "mla_bwd": (("chips", RS_FFN2),),
    "d_h_lat": (("join", RS_FFN2),),
    "ffn1_d_wdown": (("sibling", RS_MIXER),),
    "ffn1_d_act": (("chips", ("w_o", "w_br_mla", "w_br_dil", "w_uq", "w_ukv")), ("sibling", ("w1_down",))),
    "ffn1_d_wgate": (("chips", ("w1_down",)),),
    "ffn1_d_wup": (("sibling", ("w1_gate",)),),
    "ffn1_d_norm": (("chips", ("w_in", "w1_gate")), ("sibling", ("w1_up",))),
}
RS_LAST = ((("chips", ("w1_up",)),), (("join", RS_MIXER + ("w1_down", "w1_gate", "w1_up")),))


class _MeshSchedule:
    def __init__(self, w, m, v, dims):
        self.w, self.m, self.v, self.dims = w, m, v, dims
        self.shapes = {n: tuple(w[n].shape[1:]) for n in BIG}
        self.kinds = {n: _kind(n, self.shapes[n]) for n in BIG}
        self.info, buf_shapes = _gather_plan(self.shapes)
        x, y, c = _place()
        self.me = (2 * x + y).astype(jnp.int32).reshape(1)
        self.c_idx = c.astype(jnp.int32).reshape(1)
        self.shards = {n: w[n][0].astype(BF16) for n in BIG}
        self.bufs, self.gathered, self.layout = {}, set(), {}
        for n in BIG:
            out_name, kind, base, _, _ = self.info[n]
            self.bufs[out_name] = _place_own(self.shards[n], buf_shapes[out_name], kind, base, self.me, f"ag_own_{n}",
                                             prev=self.bufs.get(out_name))
        self.gw, self.views, self.pairs, self.halves, self.recv = {}, {}, {}, {}, {}
        self._ag_done(AG_FIRST, _run_side(self._ag(AG_FIRST), "ag_first"))

    def _ag(self, names):
        return _ag_side(names, self.shards, self.bufs, self.info)

    def _ag_done(self, names, outs):
        for out_name, buf in zip(_buffers_of(names, self.info), outs):
            self.bufs[out_name] = buf
        self.gathered.update(names)

    def weight(self, name):
        if name not in self.layout:
            assert all(s in self.gathered for s in WEIGHT_SOURCES.get(name, (name,))), name
            if name in self.bufs and name not in self.info:
                self.layout[name] = self.bufs[name]
            else:
                full = lambda n: _natural(self.bufs[self.info[n][0]], self.info[n][1])
                self.layout[name] = _layout_weight(name, full, self.dims)
        return self.layout[name]

    def grad(self, name, g):
        self.gw[name] = g

    def _rs_side(self, stages):
        sides = []
        for stage, names in stages:
            if stage == "sibling":
                for n in names:
                    self.views[n] = _halves_view(_natural_grad(n, self.gw, self.dims), self.kinds[n], self.shapes[n])
                sides.append(_rs_sibling_side([self.views[n] for n in names]))
            elif stage == "chips":
                sides.append(_rs_chips_side([self.pairs[n] for n in names], [self.kinds[n] for n in names],
                                            [self.shapes[n][1] for n in names]))
            else:
                sides.append(_rs_join_side([self.halves[n] for n in names]))
        return sides[0] if len(sides) == 1 else _merge_sides(sides)

    def _rs_done(self, stages, outs):
        for stage, names in stages:
            got, outs = outs[:len(names)], outs[len(names):]
            for n, a in zip(names, got):
                if stage == "sibling":
                    self.pairs[n] = _pair_sum(self.views[n], a, self.c_idx, f"rs_pair_{n}")
                elif stage == "chips":
                    self.halves[n] = _sum_pieces(self.pairs[n], a, self.kinds[n], self.me, f"rs_sum_{n}")
                else:
                    self.recv[n] = a

    def side(self, tag):
        if tag in AG_RIDES:
            return self._ag(AG_RIDES[tag])
        if tag in RS_RIDES:
            return self._rs_side(RS_RIDES[tag])
        return None

    def done(self, tag, outs):
        if tag in AG_RIDES:
            self._ag_done(AG_RIDES[tag], outs)
        else:
            self._rs_done(RS_RIDES[tag], outs)

    def finish(self):
        for k, stages in enumerate(RS_LAST):
            self._rs_done(stages, _run_side(self._rs_side(stages), f"rs_last{k}"))
        outs = {"grad": {}, "delta": {}, "m": {}, "v": {}}
        for n in BIG:
            res = _adamw_halves(self.w[n][0], self.halves[n], self.recv[n], self.m[n][0], self.v[n][0], self.c_idx,
                                f"adamw_{n}")
            for kind, a in zip(("grad", "delta", "m", "v"), res):
                outs[kind][n] = a.reshape((1,) + a.shape)
        return outs


def _step(x, p, positions, loss_target, w, m, v):
    T, D = x.shape[1], x.shape[2]
    QL, KVL = w["g_cq"].shape[1], w["g_ckv"].shape[1]
    dims = {
        "H": 4 * w["w_uq"].shape[2] // MLA_QK, "HD": w["w_br_dil"].shape[1] // DIL_HEAD, "QL": QL, "KVL": KVL,
        "LP": _round_up(QL + KVL + MLA_ROPE, LANES), "tf": _pick(4 * w["w1_gate"].shape[2], 512),
    }
    small_shapes = {n: w[n].shape for n in SMALL}
    sched = _MeshSchedule(w, m, v, dims)
    Gn = {n: w[n] for n in SMALL}
    Gn["g_q_mla"] = _pad_to(Gn["g_q_mla"], MLA_QK_PAD, 1)
    Gn["g_k_mla"] = _pad_to(Gn["g_k_mla"], MLA_QK_PAD, 1)
    Gn["g_q_dil"] = Gn["g_q_dil"].reshape(len(DIL_GROUPS), 1, DIL_HEAD)
    Gn["g_k_dil"] = Gn["g_k_dil"].reshape(len(DIL_GROUPS), 1, DIL_HEAD)

    pos_b = jnp.broadcast_to(positions.astype(F32).reshape(T, 1), (T, LANES))
    loss, dx, gg = _local_step(sched, x[0], p[0, 0], pos_b, loss_target[0], Gn, dims)
    loss = lax.psum(loss[0, 0], ("x", "y", "c"))
    outs = sched.finish()

    gg["g_q_mla"] = gg["g_q_mla"][:, :MLA_QK]
    gg["g_k_mla"] = gg["g_k_mla"][:, :MLA_QK]
    g_small = _all_reduce_small(_pack_small(gg))
    d_s, m_s, v_s = _adamw(_pack_small({n: w[n] for n in SMALL}), g_small, _pack_small({n: m[n] for n in SMALL}),
                           _pack_small({n: v[n] for n in SMALL}), "adamw_gains")
    for kind, buf in (("grad", g_small), ("delta", d_s), ("m", m_s), ("v", v_s)):
        outs[kind].update(_unpack_small(buf, small_shapes))

    grad_x = dx.reshape(1, T, D)
    return (loss, grad_x, *[outs["grad"][n] for n in WEIGHTS], *[outs["delta"][n] for n in WEIGHTS],
            *[outs["m"][n] for n in WEIGHTS], *[outs["v"][n] for n in WEIGHTS])


def kernel(x, p, positions, g_ffn1, w1_gate, w1_up, w1_down, g_mix, w_in, g_cq, w_uq, g_ckv, w_ukv, g_q_mla, g_k_mla, g_q_dil, g_k_dil, w_br_mla, w_br_dil, w_o, g_ffn2, w2_gate, w2_up, w2_down, g_ple, w_ple_gate, w_ple_proj, loss_target, m_g_ffn1, m_w1_gate, m_w1_up, m_w1_down, m_g_mix, m_w_in, m_g_cq, m_w_uq, m_g_ckv, m_w_ukv, m_g_q_mla, m_g_k_mla, m_g_q_dil, m_g_k_dil, m_w_br_mla, m_w_br_dil, m_w_o, m_g_ffn2, m_w2_gate, m_w2_up, m_w2_down, m_g_ple, m_w_ple_gate, m_w_ple_proj, v_g_ffn1, v_w1_gate, v_w1_up, v_w1_down, v_g_mix, v_w_in, v_g_cq, v_w_uq, v_g_ckv, v_w_ukv, v_g_q_mla, v_g_k_mla, v_g_q_dil, v_g_k_dil, v_w_br_mla, v_w_br_dil, v_w_o, v_g_ffn2, v_w2_gate, v_w2_up, v_w2_down, v_g_ple, v_w_ple_gate, v_w_ple_proj):
    args = locals()
    w = {n: args[n] for n in WEIGHTS}
    m = {n: args["m_" + n] for n in WEIGHTS}
    v = {n: args["v_" + n] for n in WEIGHTS}
    return _step(x, p, positions, loss_target, w, m, v)
```

```python
import functools

import numpy as np
import jax
import jax.numpy as jnp
from jax import lax
from jax.experimental import pallas as pl
from jax.experimental.pallas import tpu as pltpu

F32 = jnp.float32
BF16 = jnp.bfloat16
MESH_ID = pl.DeviceIdType.MESH

MLA_NOPE = 128
MLA_ROPE = 64
MLA_V = 128
MLA_QK = MLA_NOPE + MLA_ROPE
MLA_QK_PAD = 256
DIL_GROUPS = ((128, 1), (512, 4), (2048, 16))
DIL_HEAD = 128
DIL_ROT = DIL_HEAD // 4
DIL_BLOCK = 128
ROPE_THETA = 500000.0
EPS = 1e-6
NEG = -1e30
ADAM_LR = 0.001
ADAM_B1 = 0.9
ADAM_B2 = 0.999
ADAM_EPS = 1e-08
ADAM_WD = 0.01
ADAM_STEP = 10

LANES = 128
VMEM_LIMIT_BYTES = 56 * 1024 * 1024
MM_VMEM_BYTES = 46 * 1024 * 1024

BIG = ("w1_gate", "w1_up", "w1_down", "w_in", "w_uq", "w_ukv", "w_br_mla", "w_br_dil", "w_o",
       "w2_gate", "w2_up", "w2_down", "w_ple_gate", "w_ple_proj")
GATHER_PLAN = (("w1_gu", ("w1_gate", "w1_up")), ("w1_down", ("w1_down",)), ("w_in", ("w_in",)), ("w_uq", ("w_uq",)),
               ("w_ukv", ("w_ukv",)), ("w_br_mla", ("w_br_mla",)), ("w_br_dil", ("w_br_dil",)), ("w_o", ("w_o",)),
               ("w2_gu", ("w2_gate", "w2_up")), ("w2_down", ("w2_down",)), ("w_ple_gate", ("w_ple_gate",)),
               ("w_ple_proj", ("w_ple_proj",)))
ROW_SHARDED = ("w1_down", "w_o", "w2_down", "w_ple_gate")
SMALL = ("g_ffn1", "g_mix", "g_cq", "g_ckv", "g_q_mla", "g_k_mla", "g_q_dil", "g_k_dil", "g_ffn2", "g_ple")
WEIGHTS = ("g_ffn1", "w1_gate", "w1_up", "w1_down", "g_mix", "w_in", "g_cq", "w_uq", "g_ckv", "w_ukv", "g_q_mla",
           "g_k_mla", "g_q_dil", "g_k_dil", "w_br_mla", "w_br_dil", "w_o", "g_ffn2", "w2_gate", "w2_up", "w2_down",
           "g_ple", "w_ple_gate", "w_ple_proj")


def _pick(n, target, align=LANES):
    if n <= target:
        return n
    t = (target // align) * align
    while t >= align:
        if n % t == 0:
            return t
        t -= align
    return n


def _params(n_axes):
    return pltpu.CompilerParams(dimension_semantics=("arbitrary",) * n_axes, vmem_limit_bytes=VMEM_LIMIT_BYTES)


def _sigmoid(x):
    return 1.0 / (1.0 + jnp.exp(-x))


ANY = pl.BlockSpec(memory_space=pl.ANY)


class _Side:
    def __init__(self, arrays, out_shapes, aliases, sem_shapes, phases):
        self.arrays, self.out_shapes, self.aliases = list(arrays), list(out_shapes), dict(aliases)
        self.sem_shapes, self.phases = list(sem_shapes), list(phases)

    def start(self, p, ins, outs, sems):
        for cp in self.phases[p][0](ins, outs, sems):
            cp.start()

    def wait(self, p, ins, outs, sems):
        for cp in self.phases[p][1](ins, outs, sems):
            cp.wait_recv()
        for cp in self.phases[p][0](ins, outs, sems):
            cp.wait_send()

    def run(self, step, n_steps, ins, outs, sems):
        n_ph = len(self.phases)
        assert n_ph <= 2
        starts = (0, int(0.85 * (n_steps - 1)))
        if n_steps <= n_ph:
            @pl.when(step == n_steps - 1)
            def _():
                for p in range(n_ph):
                    self.start(p, ins, outs, sems)
                    self.wait(p, ins, outs, sems)
            return
        for p in range(n_ph):
            @pl.when(step == starts[p])
            def _(p=p):
                if p > 0:
                    self.wait(p - 1, ins, outs, sems)
                self.start(p, ins, outs, sems)

        @pl.when(step == n_steps - 1)
        def _():
            self.wait(n_ph - 1, ins, outs, sems)


def _merge_sides(sides):
    arrays, out_shapes, aliases, sem_shapes, spans = [], [], {}, [], []
    for s in sides:
        assert len(s.phases) == 1
        spans.append((len(arrays), len(out_shapes), len(sem_shapes), s))
        aliases.update({len(arrays) + i: len(out_shapes) + o for i, o in s.aliases.items()})
        arrays += s.arrays
        out_shapes += s.out_shapes
        sem_shapes += s.sem_shapes

    def part(which):
        def fn(ins, outs, sems):
            cps = []
            for a0, o0, s0, s in spans:
                cps += s.phases[0][which](ins[a0:a0 + len(s.arrays)], outs[o0:o0 + len(s.out_shapes)],
                                          sems[s0:s0 + len(s.sem_shapes)])
            return cps
        return fn

    return _Side(arrays, out_shapes, aliases, sem_shapes, [(part(0), part(1))])


def _side_parts(side, n_lead, n_out):
    if side is None:
        return [], [], [], {}
    return (side.arrays, side.out_shapes, side.sem_shapes, {n_lead + i: n_out + o for i, o in side.aliases.items()})


def _carry(kern, side, n_lead, n_out, n_scratch, step_of, n_steps):
    if side is None:
        return kern
    a = n_lead
    b = a + len(side.arrays)
    c = b + n_out
    d = c + len(side.out_shapes)
    e = d + n_scratch

    def wrapped(*refs):
        side.run(step_of(), n_steps, refs[a:b], refs[c:d], refs[e:])
        kern(*refs[:a], *refs[b:c], *refs[d:e])

    return wrapped


def _run_side(side, name):
    n_in, n_out = len(side.arrays), len(side.out_shapes)

    def body(*refs):
        ins, outs, sems = refs[:n_in], refs[n_in:n_in + n_out], refs[n_in + n_out:]
        for p in range(len(side.phases)):
            side.start(p, ins, outs, sems)
            side.wait(p, ins, outs, sems)

    return pl.pallas_call(body, name=name, out_shape=side.out_shapes, in_specs=[ANY] * n_in, out_specs=[ANY] * n_out,
                          scratch_shapes=side.sem_shapes, input_output_aliases=side.aliases)(*side.arrays)


def _mm_tiles(M, N, K, n_pairs, out_bytes, has_res):
    tm = _pick(M, 1024)
    tks = sorted({_pick(K, t) for t in (8192, 5632, 4096, 2816, 2048, 1408, 1024, 512)}, reverse=True)
    tns = sorted({_pick(N, t) for t in (1536, 1024, 512)}, reverse=True)
    for tk in tks:
        for tn in tns:
            need = 4 * n_pairs * (tm * tk + tk * tn) + tm * tn * (4 * (K > tk) + 2 * out_bytes + 8 * has_res + 4)
            if need <= MM_VMEM_BYTES:
                return tm, tn, tk
    raise ValueError((M, N, K))


def _mm(a, b, mode, out_dtype, name, res=None, alpha=1.0, a2=None, b2=None, b2_k_offset=0, side=None):
    if mode == "nn":
        (M, K), (K2, N) = a.shape, b.shape
    elif mode == "nt":
        (M, K), (N, K2) = a.shape, b.shape
    else:
        (K, M), (K2, N) = a.shape, b.shape
    assert K == K2 or (mode == "nt" and K2 > K), (name, a.shape, b.shape)
    assert a.dtype == BF16 and b.dtype == BF16, name
    tm, tn, tk = _mm_tiles(M, N, K, 1 if a2 is None else 2, jnp.dtype(out_dtype).itemsize, res is not None)
    nk = K // tk
    assert b2_k_offset % tk == 0 and (b2_k_offset == 0 or mode == "nt"), name
    k_off2 = b2_k_offset // tk
    if mode == "nn":
        a_spec = pl.BlockSpec((tm, tk), lambda i, j, k: (i, k))
        b_spec = pl.BlockSpec((tk, tn), lambda i, j, k: (k, j))
        dims = (((1,), (0,)), ((), ()))
    elif mode == "nt":
        a_spec = pl.BlockSpec((tm, tk), lambda i, j, k: (i, k))
        b_spec = pl.BlockSpec((tn, tk), lambda i, j, k: (j, k))
        b2_spec = pl.BlockSpec((tn, tk), lambda i, j, k: (j, k + k_off2))
        dims = (((1,), (1,)), ((), ()))
    else:
        a_spec = pl.BlockSpec((tk, tm), lambda i, j, k: (k, i))
        b_spec = pl.BlockSpec((tk, tn), lambda i, j, k: (k, j))
        dims = (((0,), (0,)), ((), ()))
    o_spec = pl.BlockSpec((tm, tn), lambda i, j, k: (i, j))
    has_res = res is not None
    n_pairs = 1 if a2 is None else 2
    n_main = 2 * n_pairs + int(has_res)
    n_side_in = len(side.arrays) if side else 0
    n_side_out = len(side.out_shapes) if side else 0
    n_acc = 1 if nk > 1 else 0
    gi, gj = M // tm, N // tn
    n_steps = gi * gj * nk

    def kern(*refs):
        r_ref = refs[2 * n_pairs] if has_res else None
        o_ref = refs[n_main + n_side_in]
        if side:
            step = (pl.program_id(0) * gj + pl.program_id(1)) * nk + pl.program_id(2)
            side.run(step, n_steps, refs[n_main:n_main + n_side_in],
                     refs[n_main + n_side_in + 1:n_main + n_side_in + 1 + n_side_out],
                     refs[n_main + n_side_in + 1 + n_side_out + n_acc:])
        part = lax.dot_general(refs[0][...], refs[1][...], dims, preferred_element_type=F32)
        if n_pairs == 2:
            part = part + lax.dot_general(refs[2][...], refs[3][...], dims, preferred_element_type=F32)

        def finish(r):
            if alpha != 1.0:
                r = r * alpha
            if has_res:
                r = r_ref[...] + r
            o_ref[...] = r.astype(o_ref.dtype)

        if nk == 1:
            finish(part)
            return
        acc_ref = refs[n_main + n_side_in + 1 + n_side_out]
        k = pl.program_id(2)

        @pl.when(k == 0)
        def _():
            acc_ref[...] = part

        @pl.when(k > 0)
        def _():
            acc_ref[...] += part

        @pl.when(k == nk - 1)
        def _():
            finish(acc_ref[...])

    ins = (a, b) + ((a2, b2) if n_pairs == 2 else ()) + ((res,) if has_res else ())
    in_specs = [a_spec, b_spec] + ([a_spec, b2_spec if mode == "nt" else b_spec] if n_pairs == 2 else [])
    in_specs += [o_spec] if has_res else []
    out_shape = jax.ShapeDtypeStruct((M, N), out_dtype)
    scratch = [pltpu.VMEM((tm, tn), F32)] if nk > 1 else []
    if not side:
        return pl.pallas_call(kern, name=name, grid=(gi, gj, nk), in_specs=in_specs, out_specs=o_spec,
                              out_shape=out_shape, scratch_shapes=scratch, compiler_params=_params(3))(*ins)
    outs = pl.pallas_call(
        kern, name=name, grid=(gi, gj, nk), in_specs=in_specs + [ANY] * n_side_in,
        out_specs=[o_spec] + [ANY] * n_side_out, out_shape=[out_shape] + list(side.out_shapes),
        scratch_shapes=scratch + list(side.sem_shapes),
        input_output_aliases={n_main + i: 1 + o for i, o in side.aliases.items()},
        compiler_params=_params(3))(*ins, *side.arrays)
    return outs[0], list(outs[1:])


def _vcall(body, grid, ins, in_specs, out_shapes, out_specs, name, n_inner_acc=0, n_acc=0):
    n_in, n_out = len(ins), len(out_shapes)
    n_plain = n_out - n_acc - n_inner_acc

    def kern(*refs):
        vals = body(*[r[...] for r in refs[:n_in]])
        if not isinstance(vals, (tuple, list)):
            vals = (vals,)
        out_refs = refs[n_in:]
        inner_first = pl.program_id(len(grid) - 1) == 0
        first = inner_first
        for ax in range(len(grid) - 1):
            first = jnp.logical_and(first, pl.program_id(ax) == 0)
        for idx, (r, v) in enumerate(zip(out_refs, vals)):
            if idx < n_plain:
                r[...] = v.astype(r.dtype)
                continue
            start = inner_first if idx < n_plain + n_inner_acc else first

            @pl.when(start)
            def _(r=r, v=v):
                r[...] = v.astype(r.dtype)

            @pl.when(jnp.logical_not(start))
            def _(r=r, v=v):
                r[...] += v.astype(r.dtype)

    out = pl.pallas_call(kern, name=name, grid=grid, in_specs=in_specs, out_specs=out_specs, out_shape=out_shapes,
                         compiler_params=_params(len(grid)))(*ins)
    return out


def _rows(tm, c):
    return pl.BlockSpec((tm, c), lambda i: (i, 0))


def _vec(c):
    return pl.BlockSpec((1, c), lambda i: (0, 0))


def _sds(shape, dtype):
    return jax.ShapeDtypeStruct(shape, dtype)


def _rstd(x, c):
    return lax.rsqrt(jnp.sum(x * x, axis=-1, keepdims=True) * (1.0 / c) + EPS)


def _rms_bwd(xh, r, g, dn, c):
    u = dn * g
    dx = r * (u - xh * (jnp.sum(xh * u, axis=-1, keepdims=True) * (1.0 / c)))
    return dx, jnp.sum(dn * xh, axis=0, keepdims=True)


def _rope(t, c, sa, sb, half):
    return t * c + pltpu.roll(t, LANES - half, 1) * sa + pltpu.roll(t, half, 1) * sb


def _rope_t(d, c, sa, sb, half):
    return d * c + pltpu.roll(d * sa, half, 1) + pltpu.roll(d * sb, LANES - half, 1)


def _rope_tables(pos_b, rd, name):
    T = pos_b.shape[0]
    half = rd // 2
    inv = ROPE_THETA ** (-jnp.arange(half, dtype=F32) * 2.0 / rd)
    inv_full = jnp.concatenate([inv, inv, jnp.zeros((LANES - rd,), F32)]).reshape(1, LANES)
    lane = np.arange(LANES)
    ma = jnp.asarray((lane < half).astype(np.float32)).reshape(1, LANES)
    mb = jnp.asarray(((lane >= half) & (lane < rd)).astype(np.float32)).reshape(1, LANES)
    tm = _pick(T, 1024, 8)

    def body(pos, invf, a, b):
        ang = pos * invf
        c, s = jnp.cos(ang), jnp.sin(ang)
        inside = a + b
        return c * inside + (1.0 - inside), -s * a, s * b

    return _vcall(body, (T // tm,), (pos_b, inv_full, ma, mb), [_rows(tm, LANES)] + [_vec(LANES)] * 3,
                  [_sds((T, LANES), F32)] * 3, [_rows(tm, LANES)] * 3, name)


def _rms_fwd(x, g, name):
    T, C = x.shape
    tm = _pick(T, 512, 8)

    def body(xv, gv):
        return xv * _rstd(xv, C) * gv

    return _vcall(body, (T // tm,), (x, g), [_rows(tm, C), _vec(C)], [_sds((T, C), BF16)], [_rows(tm, C)], name)[0]


def _rms_bwd_call(x, g, dn, dres, name):
    T, C = x.shape
    tm = _pick(T, 256, 8)

    def body(xv, gv, dnv, drv):
        r = _rstd(xv, C)
        dx, dg = _rms_bwd(xv * r, r, gv, dnv.astype(F32), C)
        return drv + dx, dg

    return _vcall(body, (T // tm,), (x, g, dn, dres), [_rows(tm, C), _vec(C), _rows(tm, C), _rows(tm, C)],
                  [_sds((T, C), F32), _sds((1, C), F32)], [_rows(tm, C), _vec(C)], name, n_acc=1)


def _gate_up(n, w_gu, tf, name, side=None):
    T, D = n.shape
    F = w_gu.shape[1] // 2
    tm = _pick(T, 1024, 16)
    nf = F // tf

    def kern(n_ref, wg_ref, wu_ref, a_ref, b_ref, act_ref):
        x = n_ref[...]
        a = jnp.dot(x, wg_ref[...], preferred_element_type=F32)
        b = jnp.dot(x, wu_ref[...], preferred_element_type=F32)
        a_ref[...] = a.astype(BF16)
        b_ref[...] = b.astype(BF16)
        act_ref[...] = (a * _sigmoid(a) * b).astype(BF16)

    tile = pl.BlockSpec((tm, tf), lambda i, j: (i, j))
    s_in, s_out, s_sems, s_alias = _side_parts(side, 3, 3)
    step_of = lambda: pl.program_id(0) * nf + pl.program_id(1)
    outs = pl.pallas_call(
        _carry(kern, side, 3, 3, 0, step_of, (T // tm) * nf), name=name, grid=(T // tm, nf),
        in_specs=[pl.BlockSpec((tm, D), lambda i, j: (i, 0)), pl.BlockSpec((D, tf), lambda i, j: (0, j)),
                  pl.BlockSpec((D, tf), lambda i, j: (0, j + nf))] + [ANY] * len(s_in),
        out_specs=[tile] * 3 + [ANY] * len(s_out), out_shape=[_sds((T, F), BF16)] * 3 + s_out,
        scratch_shapes=s_sems, input_output_aliases=s_alias, compiler_params=_params(2))(n, w_gu, w_gu, *s_in)
    return outs


def _d_gate_up(dout_b, w_d, a, b, tf, name, side=None):
    T, D = dout_b.shape
    F = w_d.shape[0]
    tm = _pick(T, 1024, 16)
    nf = F // tf

    def kern(d_ref, w_ref, a_ref, b_ref, da_ref, db_ref):
        d = 0.5 * lax.dot_general(d_ref[...], w_ref[...], NT, preferred_element_type=F32)
        a, b = a_ref[...].astype(F32), b_ref[...].astype(F32)
        sg = _sigmoid(a)
        da_ref[...] = (d * b * (sg * (1.0 + a * (1.0 - sg)))).astype(BF16)
        db_ref[...] = (d * (a * sg)).astype(BF16)

    tile = pl.BlockSpec((tm, tf), lambda i, j: (i, j))
    s_in, s_out, s_sems, s_alias = _side_parts(side, 4, 2)
    step_of = lambda: pl.program_id(0) * nf + pl.program_id(1)
    outs = pl.pallas_call(
        _carry(kern, side, 4, 2, 0, step_of, (T // tm) * nf), name=name, grid=(T // tm, nf),
        in_specs=[pl.BlockSpec((tm, D), lambda i, j: (i, 0)), pl.BlockSpec((tf, D), lambda i, j: (j, 0)), tile, tile]
        + [ANY] * len(s_in),
        out_specs=[tile] * 2 + [ANY] * len(s_out), out_shape=[_sds((T, F), BF16)] * 2 + s_out,
        scratch_shapes=s_sems, input_output_aliases=s_alias, compiler_params=_params(2))(dout_b, w_d, a, b, *s_in)
    return outs


def _lat_fwd(lat, g_cq, g_ckv, name):
    T, LP = lat.shape
    QL, KVL = g_cq.shape[1], g_ckv.shape[1]
    tm = _pick(T, 512, 8)

    def body(v, gq, gk):
        xq, xk = v[:, :QL], v[:, QL:QL + KVL]
        return xq * _rstd(xq, QL) * gq, xk * _rstd(xk, KVL) * gk

    return _vcall(body, (T // tm,), (lat, g_cq, g_ckv), [_rows(tm, LP), _vec(QL), _vec(KVL)],
                  [_sds((T, QL), BF16), _sds((T, KVL), BF16)], [_rows(tm, QL), _rows(tm, KVL)], name)


def _lat_bwd(dcq, dckv, dkr, lat, g_cq, g_ckv, name):
    T, LP = lat.shape
    QL, KVL = g_cq.shape[1], g_ckv.shape[1]
    tm = _pick(T, 512, 8)

    def body(dq, dk, dr, v, gq, gk):
        xq, xk = v[:, :QL], v[:, QL:QL + KVL]
        rq, rk = _rstd(xq, QL), _rstd(xk, KVL)
        dxq, dgq = _rms_bwd(xq * rq, rq, gq, dq, QL)
        dxk, dgk = _rms_bwd(xk * rk, rk, gk, dk, KVL)
        return jnp.concatenate([dxq, dxk, dr], axis=1), dgq, dgk

    return _vcall(body, (T // tm,), (dcq, dckv, dkr, lat, g_cq, g_ckv),
                  [_rows(tm, QL), _rows(tm, KVL), _rows(tm, LANES), _rows(tm, LP), _vec(QL), _vec(KVL)],
                  [_sds((T, LP), BF16), _sds((1, QL), F32), _sds((1, KVL), F32)],
                  [_rows(tm, LP), _vec(QL), _vec(KVL)], name, n_acc=2)


def _head_spec(tm, w):
    return pl.BlockSpec((tm, w), lambda i, h: (i, h))


def _row2(tm, w, col=0):
    return pl.BlockSpec((tm, w), lambda i, h: (i, col))


def _vec2(w):
    return pl.BlockSpec((1, w), lambda i, h: (0, 0))


def _mla_q_prep(q_raw, g_q, tabs, H, scale, name):
    T = q_raw.shape[0]
    tm = _pick(T, 512, 8)
    half = MLA_ROPE // 2

    def body(x, g, c, sa, sb):
        n = x * _rstd(x, MLA_QK) * g
        return jnp.concatenate([n[:, :LANES], _rope(n[:, LANES:], c, sa, sb, half)], axis=1) * scale

    return _vcall(body, (T // tm, H), (q_raw, g_q) + tabs,
                  [_head_spec(tm, MLA_QK_PAD), _vec2(MLA_QK_PAD)] + [_row2(tm, LANES)] * 3,
                  [_sds((T, H * MLA_QK_PAD), BF16)], [_head_spec(tm, MLA_QK_PAD)], name)[0]


def _mla_q_bwd(dq, q_raw, g_q, tabs, H, scale, name):
    T = q_raw.shape[0]
    tm = _pick(T, 512, 8)
    half = MLA_ROPE // 2

    def body(d, x, g, c, sa, sb):
        r = _rstd(x, MLA_QK)
        d = d * scale
        dn = jnp.concatenate([d[:, :LANES], _rope_t(d[:, LANES:], c, sa, sb, half)], axis=1)
        return _rms_bwd(x * r, r, g, dn, MLA_QK)

    return _vcall(body, (T // tm, H), (dq, q_raw, g_q) + tabs,
                  [_head_spec(tm, MLA_QK_PAD), _head_spec(tm, MLA_QK_PAD), _vec2(MLA_QK_PAD)] + [_row2(tm, LANES)] * 3,
                  [_sds((T, H * MLA_QK_PAD), BF16), _sds((1, MLA_QK_PAD), F32)],
                  [_head_spec(tm, MLA_QK_PAD), _vec2(MLA_QK_PAD)], name, n_acc=1)


def _mla_k_prep(kv, lat, kr_col, g_k, tabs, H, name):
    T = kv.shape[0]
    tm = _pick(T, 512, 8)
    half = MLA_ROPE // 2

    def body(x, kr, g, c, sa, sb):
        kn = x[:, :LANES]
        r = lax.rsqrt((jnp.sum(kn * kn, axis=-1, keepdims=True) + jnp.sum(kr * kr, axis=-1, keepdims=True))
                      * (1.0 / MLA_QK) + EPS)
        k0 = kn * r * g[:, :LANES]
        k1 = _rope(kr * r * g[:, LANES:], c, sa, sb, half)
        return jnp.concatenate([k0, k1], axis=1), x[:, LANES:]

    return _vcall(body, (T // tm, H), (kv, lat, g_k) + tabs,
                  [_head_spec(tm, 2 * LANES), _row2(tm, LANES, kr_col), _vec2(MLA_QK_PAD)] + [_row2(tm, LANES)] * 3,
                  [_sds((T, H * MLA_QK_PAD), BF16), _sds((T, H * MLA_V), BF16)],
                  [_head_spec(tm, MLA_QK_PAD), _head_spec(tm, MLA_V)], name)


def _mla_k_bwd(dk, dv, kv, lat, kr_col, g_k, tabs, H, name):
    T = kv.shape[0]
    tm = _pick(T, 512, 8)
    half = MLA_ROPE // 2

    def body(d, dvv, x, kr, g, c, sa, sb):
        xx = jnp.concatenate([x[:, :LANES], kr], axis=1)
        r = _rstd(xx, MLA_QK)
        dn = jnp.concatenate([d[:, :LANES], _rope_t(d[:, LANES:], c, sa, sb, half)], axis=1)
        dx, dg = _rms_bwd(xx * r, r, g, dn, MLA_QK)
        return jnp.concatenate([dx[:, :LANES], dvv], axis=1), dx[:, LANES:], dg

    return _vcall(body, (T // tm, H), (dk, dv, kv, lat, g_k) + tabs,
                  [_head_spec(tm, MLA_QK_PAD), _head_spec(tm, MLA_V), _head_spec(tm, 2 * LANES),
                   _row2(tm, LANES, kr_col), _vec2(MLA_QK_PAD)] + [_row2(tm, LANES)] * 3,
                  [_sds((T, H * 2 * LANES), BF16), _sds((T, LANES), F32), _sds((1, MLA_QK_PAD), F32)],
                  [_head_spec(tm, 2 * LANES), _row2(tm, LANES), _vec2(MLA_QK_PAD)], name, n_inner_acc=1, n_acc=1)


def _dil_prep(pd, g_q, g_k, tabs, HD, scale, name):
    T = pd.shape[0]
    W = HD * DIL_HEAD
    G = len(DIL_GROUPS)
    tm = _pick(T, 256, 8)
    half = DIL_ROT // 2

    def body(xq, xk, xv, gq, gk, c, sa, sb):
        outs = []
        for x, g, s in ((xq, gq, scale), (xk, gk, 1.0)):
            heads = []
            for h in range(HD):
                xs = x[:, h * DIL_HEAD:(h + 1) * DIL_HEAD].astype(F32)
                n = _rope(xs * _rstd(xs, DIL_HEAD) * g, c, sa, sb, half)
                heads.append(n * s if s != 1.0 else n)
            outs.append(jnp.concatenate(heads, axis=1))
        return outs[0], outs[1], xv

    gspec = pl.BlockSpec((None, 1, DIL_HEAD), lambda i, g: (g, 0, 0))
    return _vcall(body, (T // tm, G), (pd, pd, pd, g_q, g_k) + tabs,
                  [pl.BlockSpec((tm, W), lambda i, g: (i, 3 * g)), pl.BlockSpec((tm, W), lambda i, g: (i, 3 * g + 1)),
                   pl.BlockSpec((tm, W), lambda i, g: (i, 3 * g + 2)), gspec, gspec] + [_row2(tm, LANES)] * 3,
                  [_sds((T, G * W), F32)] * 3, [pl.BlockSpec((tm, W), lambda i, g: (i, g))] * 3, name)


def _dil_prep_bwd(dq, dk, dv, pd, grp, g_q, g_k, tabs, HD, scale, name):
    T = pd.shape[0]
    W = HD * DIL_HEAD
    tm = _pick(T, 256, 8)
    half = DIL_ROT // 2

    def body(dqv, dkv, dvv, xq, xk, gq, gk, c, sa, sb):
        cols, dgs = [], []
        for d, x, g, s in ((dqv, xq, gq, scale), (dkv, xk, gk, 1.0)):
            heads, dg = [], None
            for h in range(HD):
                sl = slice(h * DIL_HEAD, (h + 1) * DIL_HEAD)
                xs = x[:, sl].astype(F32)
                r = _rstd(xs, DIL_HEAD)
                dh = d[:, sl] * s if s != 1.0 else d[:, sl]
                dx, dgh = _rms_bwd(xs * r, r, g, _rope_t(dh, c, sa, sb, half), DIL_HEAD)
                heads.append(dx)
                dg = dgh if dg is None else dg + dgh
            cols.append(jnp.concatenate(heads, axis=1))
            dgs.append(dg)
        return jnp.concatenate(cols + [dvv], axis=1), dgs[0], dgs[1]

    gq, gk = g_q[grp], g_k[grp]
    return _vcall(body, (T // tm,), (dq, dk, dv, pd, pd, gq, gk) + tabs,
                  [_rows(tm, W)] * 3 + [pl.BlockSpec((tm, W), lambda i: (i, 3 * grp)),
                                        pl.BlockSpec((tm, W), lambda i: (i, 3 * grp + 1)),
                                        _vec(DIL_HEAD), _vec(DIL_HEAD)] + [_rows(tm, LANES)] * 3,
                  [_sds((T, 3 * W), BF16), _sds((1, DIL_HEAD), F32), _sds((1, DIL_HEAD), F32)],
                  [_rows(tm, 3 * W), _vec(DIL_HEAD), _vec(DIL_HEAD)], name, n_acc=2)


def _dil_merge(os_, lses, name):
    T, W = os_[0].shape
    tm = _pick(T, 256, 8)

    def body(o0, o1, o2, l0, l1, l2):
        m = jnp.maximum(jnp.maximum(l0, l1), l2)
        w0, w1, w2 = jnp.exp(l0 - m), jnp.exp(l1 - m), jnp.exp(l2 - m)
        z = w0 + w1 + w2
        return (w0 * o0 + w1 * o1 + w2 * o2) / z, m + jnp.log(z)

    return _vcall(body, (T // tm,), tuple(os_) + tuple(lses), [_rows(tm, W)] * 6,
                  [_sds((T, W), BF16), _sds((T, W), F32)], [_rows(tm, W)] * 2, name)


def _gate_merge(pg, bm, bd, name):
    T, D = bm.shape
    tm = _pick(T, 256, 8)

    def body(g, m, d):
        g = g.astype(F32)
        return _sigmoid(g[:, :D]) * m + _sigmoid(g[:, D:]) * d

    return _vcall(body, (T // tm,), (pg, bm, bd), [_rows(tm, 2 * D), _rows(tm, D), _rows(tm, D)],
                  [_sds((T, D), BF16)], [_rows(tm, D)], name)[0]


def _gate_bwd(dmerged, pg, bm, bd, name):
    T, D = bm.shape
    tm = _pick(T, 256, 8)

    def body(dm, g, m, d):
        g = g.astype(F32)
        s0, s1 = _sigmoid(g[:, :D]), _sigmoid(g[:, D:])
        dpg = jnp.concatenate([dm * m * s0 * (1.0 - s0), dm * d * s1 * (1.0 - s1)], axis=1)
        return dm * s0, dm * s1, dpg

    return _vcall(body, (T // tm,), (dmerged, pg, bm, bd), [_rows(tm, D), _rows(tm, 2 * D), _rows(tm, D), _rows(tm, D)],
                  [_sds((T, D), BF16), _sds((T, D), BF16), _sds((T, 2 * D), BF16)],
                  [_rows(tm, D), _rows(tm, D), _rows(tm, 2 * D)], name)


def _ple_loss(x3, zg, pp, target, name):
    T, D = x3.shape
    tm = _pick(T, 256, 8)

    def body(x, z, p_, t):
        s = _sigmoid(z)
        e = x + s * p_ - t
        dy = e * (1.0 / D)
        part = 0.5 * jnp.sum(jnp.sum(e * e, axis=1, keepdims=True), axis=0, keepdims=True) * (1.0 / D)
        return dy, dy * s, dy * p_ * s * (1.0 - s), jnp.broadcast_to(part, (1, LANES))

    return _vcall(body, (T // tm,), (x3, zg, pp, target), [_rows(tm, D)] * 4,
                  [_sds((T, D), F32), _sds((T, D), BF16), _sds((T, D), BF16), _sds((1, LANES), F32)],
                  [_rows(tm, D)] * 3 + [_vec(LANES)], name, n_acc=1)


NT = (((1,), (1,)), ((), ()))
TN = (((0,), (0,)), ((), ()))


def _diag_mask(s):
    row = lax.broadcasted_iota(jnp.int32, s.shape, 0)
    col = lax.broadcasted_iota(jnp.int32, s.shape, 1)
    return jnp.where(col <= row, s, NEG)


def _causal_pairs(nq, key_major):
    if key_major:
        pairs = [(i, j) for j in range(nq) for i in range(j, nq)]
    else:
        pairs = [(i, j) for i in range(nq) for j in range(i + 1)]
    return (jnp.asarray([pr[0] for pr in pairs], jnp.int32), jnp.asarray([pr[1] for pr in pairs], jnp.int32))


def _mla_fwd(q, k, v, H, name, side=None):
    T = q.shape[0]
    tq = _pick(T, 512)
    nq = T // tq
    hb = 2 if H % 2 == 0 else 1
    qi_tab, kj_tab = _causal_pairs(nq, key_major=False)

    def kern(qi_ref, kj_ref, q_ref, k_ref, v_ref, o_ref, lse_ref, m_sc, l_sc, acc_sc):
        t = pl.program_id(1)
        qi, kj = qi_ref[t], kj_ref[t]

        @pl.when(kj == 0)
        def _():
            m_sc[...] = jnp.full_like(m_sc, NEG)
            l_sc[...] = jnp.zeros_like(l_sc)
            acc_sc[...] = jnp.zeros_like(acc_sc)

        def tile(diagonal):
            for hh in range(hb):
                qs = slice(hh * MLA_QK_PAD, (hh + 1) * MLA_QK_PAD)
                vs = slice(hh * MLA_V, (hh + 1) * MLA_V)
                s = lax.dot_general(q_ref[:, qs], k_ref[:, qs], NT, preferred_element_type=F32)
                if diagonal:
                    s = _diag_mask(s)
                m_prev = m_sc[hh]
                m_new = jnp.maximum(m_prev, jnp.max(s, axis=1, keepdims=True))
                alpha = jnp.exp(m_prev - m_new)
                p = jnp.exp(s - m_new)
                l_new = alpha * l_sc[hh] + jnp.sum(p, axis=1, keepdims=True)
                acc = alpha * acc_sc[hh] + jnp.dot(p.astype(BF16), v_ref[:, vs], preferred_element_type=F32)
                if diagonal:
                    o_ref[:, vs] = (acc / l_new).astype(o_ref.dtype)
                    lse_ref[:, hh * LANES:(hh + 1) * LANES] = jnp.broadcast_to(m_new + jnp.log(l_new), (tq, LANES))
                else:
                    m_sc[hh] = m_new
                    l_sc[hh] = l_new
                    acc_sc[hh] = acc

        @pl.when(kj < qi)
        def _():
            tile(False)

        @pl.when(kj == qi)
        def _():
            tile(True)

    qspec = lambda w: pl.BlockSpec((tq, hb * w), lambda h, t, qi_ref, kj_ref: (qi_ref[t], h))
    kspec = lambda w: pl.BlockSpec((tq, hb * w), lambda h, t, qi_ref, kj_ref: (kj_ref[t], h))
    n_pairs = qi_tab.shape[0]
    s_in, s_out, s_sems, s_alias = _side_parts(side, 5, 2)
    grid_spec = pltpu.PrefetchScalarGridSpec(
        num_scalar_prefetch=2, grid=(H // hb, n_pairs),
        in_specs=[qspec(MLA_QK_PAD), kspec(MLA_QK_PAD), kspec(MLA_V)] + [ANY] * len(s_in),
        out_specs=[qspec(MLA_V), qspec(LANES)] + [ANY] * len(s_out),
        scratch_shapes=[pltpu.VMEM((hb, tq, 1), F32), pltpu.VMEM((hb, tq, 1), F32),
                        pltpu.VMEM((hb, tq, MLA_V), F32)] + s_sems)
    step_of = lambda: pl.program_id(0) * n_pairs + pl.program_id(1)
    return pl.pallas_call(
        _carry(kern, side, 5, 2, 3, step_of, (H // hb) * n_pairs), name=name, grid_spec=grid_spec,
        out_shape=[_sds((T, H * MLA_V), BF16), _sds((T, H * LANES), F32)] + s_out, input_output_aliases=s_alias,
        compiler_params=_params(2))(qi_tab, kj_tab, q, k, v, *s_in)


def _mla_bwd(q, k, v, do, o, lse, H, name, side=None):
    T = q.shape[0]
    tq = _pick(T, 512)
    nq = T // tq
    qi_tab, kj_tab = _causal_pairs(nq, key_major=True)

    def kern(qi_ref, kj_ref, q_ref, k_ref, v_ref, do_ref, o_ref, lse_ref, dq_ref, dk_ref, dv_ref, dk_sc, dv_sc):
        t = pl.program_id(1)
        qi, kj = qi_ref[t], kj_ref[t]
        rows = pl.ds(pl.multiple_of(qi * tq, tq), tq)

        def tile(diagonal):
            s = lax.dot_general(q_ref[...], k_ref[...], NT, preferred_element_type=F32)
            if diagonal:
                s = _diag_mask(s)
            p = jnp.exp(s - lse_ref[:, :1])
            dl = jnp.sum(do_ref[...].astype(F32) * o_ref[...].astype(F32), axis=1, keepdims=True)
            dp = lax.dot_general(do_ref[...], v_ref[...], NT, preferred_element_type=F32)
            ds = (p * (dp - dl)).astype(BF16)
            dv = lax.dot_general(p.astype(BF16), do_ref[...], TN, preferred_element_type=F32)
            dk = lax.dot_general(ds, q_ref[...], TN, preferred_element_type=F32)
            dq = jnp.dot(ds, k_ref[...], preferred_element_type=F32)
            if diagonal:
                dv_sc[...] = dv
                dk_sc[...] = dk
            else:
                dv_sc[...] += dv
                dk_sc[...] += dk

            @pl.when(kj == 0)
            def _():
                dq_ref[rows, :] = dq

            @pl.when(kj > 0)
            def _():
                dq_ref[rows, :] += dq

        @pl.when(qi == kj)
        def _():
            tile(True)

        @pl.when(qi > kj)
        def _():
            tile(False)

        @pl.when(qi == nq - 1)
        def _():
            dk_ref[...] = dk_sc[...]
            dv_ref[...] = dv_sc[...]

    qspec = lambda w: pl.BlockSpec((tq, w), lambda h, t, qi_ref, kj_ref: (qi_ref[t], h))
    kspec = lambda w: pl.BlockSpec((tq, w), lambda h, t, qi_ref, kj_ref: (kj_ref[t], h))
    n_pairs = qi_tab.shape[0]
    s_in, s_out, s_sems, s_alias = _side_parts(side, 8, 3)
    grid_spec = pltpu.PrefetchScalarGridSpec(
        num_scalar_prefetch=2, grid=(H, n_pairs),
        in_specs=[qspec(MLA_QK_PAD), kspec(MLA_QK_PAD), kspec(MLA_V), qspec(MLA_V), qspec(MLA_V), qspec(LANES)]
        + [ANY] * len(s_in),
        out_specs=[pl.BlockSpec((T, MLA_QK_PAD), lambda h, t, qi_ref, kj_ref: (0, h)), kspec(MLA_QK_PAD), kspec(MLA_V)]
        + [ANY] * len(s_out),
        scratch_shapes=[pltpu.VMEM((tq, MLA_QK_PAD), F32), pltpu.VMEM((tq, MLA_V), F32)] + s_sems)
    step_of = lambda: pl.program_id(0) * n_pairs + pl.program_id(1)
    return pl.pallas_call(
        _carry(kern, side, 8, 3, 2, step_of, H * n_pairs), name=name, grid_spec=grid_spec,
        out_shape=[_sds((T, H * MLA_QK_PAD), F32), _sds((T, H * MLA_QK_PAD), F32), _sds((T, H * MLA_V), F32)] + s_out,
        input_output_aliases=s_alias, compiler_params=_params(2))(qi_tab, kj_tab, q, k, v, do, o, lse, *s_in)


class _DilGeometry:
    def __init__(self, T, dil, HD, grp):
        self.dil, self.sub = dil, max(1, 8 // dil)
        self.tb = self.sub * DIL_BLOCK * dil
        assert T % self.tb == 0, (T, dil)
        self.nblk = T // self.tb
        last = self.nblk - 1
        self.cur_g = pl.BlockSpec((self.tb, DIL_HEAD), lambda i, h: (i, grp * HD + h))
        self.prev_g = pl.BlockSpec((self.tb, DIL_HEAD), lambda i, h: (jnp.maximum(i - 1, 0), grp * HD + h))
        self.next_g = pl.BlockSpec((self.tb, DIL_HEAD), lambda i, h: (jnp.minimum(i + 1, last), grp * HD + h))
        self.cur = pl.BlockSpec((self.tb, DIL_HEAD), lambda i, h: (i, h))
        self.next = pl.BlockSpec((self.tb, DIL_HEAD), lambda i, h: (jnp.minimum(i + 1, last), h))

    def rows(self, b, r):
        if self.dil == 1:
            return pl.ds(b * DIL_BLOCK, DIL_BLOCK)
        return pl.ds(b * DIL_BLOCK * self.dil + r, DIL_BLOCK, stride=self.dil)

    def tiles(self):
        return [(b, r) for b in range(self.sub) for r in range(self.dil)]

    def masks(self, i):
        row = lax.broadcasted_iota(jnp.int32, (DIL_BLOCK, DIL_BLOCK), 0)
        col = lax.broadcasted_iota(jnp.int32, (DIL_BLOCK, DIL_BLOCK), 1)
        no_prev = jnp.where(i > 0, 0, 2 * DIL_BLOCK)
        no_next = jnp.where(i + 1 < self.nblk, 0, 2 * DIL_BLOCK)
        return col <= row, col >= row, col >= row + no_prev, col >= row + no_next


def _dil_fwd(qd, kd, vd, grp, dil, HD, name):
    T = qd.shape[0]
    W = HD * DIL_HEAD
    geo = _DilGeometry(T, dil, HD, grp)

    def kern(q_ref, kc_ref, kp_ref, vc_ref, vp_ref, o_ref, lse_ref):
        ok_cur, ok_prev, ok_first, _ = geo.masks(pl.program_id(0))
        for b, r in geo.tiles():
            R = geo.rows(b, r)
            q = q_ref[R, :].astype(BF16)
            if b == 0:
                Rp = geo.rows(geo.sub - 1, r)
                kp, vp, ok = kp_ref[Rp, :], vp_ref[Rp, :], ok_first
            else:
                Rp = geo.rows(b - 1, r)
                kp, vp, ok = kc_ref[Rp, :], vc_ref[Rp, :], ok_prev
            sa = jnp.where(ok, lax.dot_general(q, kp.astype(BF16), NT, preferred_element_type=F32), NEG)
            sb = jnp.where(ok_cur, lax.dot_general(q, kc_ref[R, :].astype(BF16), NT, preferred_element_type=F32), NEG)
            m = jnp.maximum(jnp.max(sa, axis=1, keepdims=True), jnp.max(sb, axis=1, keepdims=True))
            ea, eb = jnp.exp(sa - m), jnp.exp(sb - m)
            l = jnp.sum(ea, axis=1, keepdims=True) + jnp.sum(eb, axis=1, keepdims=True)
            acc = (jnp.dot(ea.astype(BF16), vp.astype(BF16), preferred_element_type=F32)
                   + jnp.dot(eb.astype(BF16), vc_ref[R, :].astype(BF16), preferred_element_type=F32))
            o_ref[R, :] = acc / l
            lse_ref[R, :] = jnp.broadcast_to(m + jnp.log(l), (DIL_BLOCK, DIL_HEAD))

    return pl.pallas_call(
        kern, name=name, grid=(geo.nblk, HD), in_specs=[geo.cur_g, geo.cur_g, geo.prev_g, geo.cur_g, geo.prev_g],
        out_specs=[geo.cur, geo.cur], out_shape=[_sds((T, W), F32)] * 2,
        compiler_params=_params(2))(qd, kd, kd, vd, vd)


def _dil_delta(do, o, HD, name):
    T, W = do.shape
    tm = _pick(T, 512, 8)

    def body(d, ov):
        prod = d * ov.astype(F32)
        return jnp.concatenate(
            [jnp.broadcast_to(jnp.sum(prod[:, h * DIL_HEAD:(h + 1) * DIL_HEAD], axis=1, keepdims=True), (tm, DIL_HEAD))
             for h in range(HD)], axis=1)

    return _vcall(body, (T // tm,), (do, o), [_rows(tm, W)] * 2, [_sds((T, W), F32)], [_rows(tm, W)], name)[0]


def _dil_bwd(qd, kd, vd, do, delta, lse, grp, dil, HD, name):
    T = qd.shape[0]
    W = HD * DIL_HEAD
    geo = _DilGeometry(T, dil, HD, grp)

    def kern(q_ref, k_ref, v_ref, do_ref, dl_ref, ls_ref, kp_ref, vp_ref, qn_ref, don_ref, dln_ref, lsn_ref,
             dq_ref, dk_ref, dv_ref):
        ok_cur, ok_prev, ok_first, ok_next = geo.masks(pl.program_id(0))

        def tile(q, do_, dl, ls, k, v, ok):
            s = jnp.where(ok, lax.dot_general(q, k, NT, preferred_element_type=F32), NEG)
            p = jnp.exp(s - ls)
            ds = p * (lax.dot_general(do_, v, NT, preferred_element_type=F32) - dl)
            return ds.astype(BF16), p.astype(BF16)

        for b, r in geo.tiles():
            R = geo.rows(b, r)
            q, do_ = q_ref[R, :].astype(BF16), do_ref[R, :].astype(BF16)
            k, v = k_ref[R, :].astype(BF16), v_ref[R, :].astype(BF16)
            dl, ls = dl_ref[R, :], ls_ref[R, :]
            ds, p_ = tile(q, do_, dl, ls, k, v, ok_cur)
            dq = jnp.dot(ds, k, preferred_element_type=F32)
            dk_ref[R, :] = lax.dot_general(ds, q, TN, preferred_element_type=F32)
            dv_ref[R, :] = lax.dot_general(p_, do_, TN, preferred_element_type=F32)
            if b == 0:
                Rp = geo.rows(geo.sub - 1, r)
                kp, vp = kp_ref[Rp, :].astype(BF16), vp_ref[Rp, :].astype(BF16)
                ds, p_ = tile(q, do_, dl, ls, kp, vp, ok_first)
            else:
                Rp = geo.rows(b - 1, r)
                kp, vp = k_ref[Rp, :].astype(BF16), v_ref[Rp, :].astype(BF16)
                ds, p_ = tile(q, do_, dl, ls, kp, vp, ok_prev)
                dk_ref[Rp, :] += lax.dot_general(ds, q, TN, preferred_element_type=F32)
                dv_ref[Rp, :] += lax.dot_general(p_, do_, TN, preferred_element_type=F32)
            dq_ref[R, :] = dq + jnp.dot(ds, kp, preferred_element_type=F32)
        for r in range(dil):
            R, Rn = geo.rows(geo.sub - 1, r), geo.rows(0, r)
            qn, don = qn_ref[Rn, :].astype(BF16), don_ref[Rn, :].astype(BF16)
            ds, p_ = tile(qn, don, dln_ref[Rn, :], lsn_ref[Rn, :], k_ref[R, :].astype(BF16), v_ref[R, :].astype(BF16),
                          ok_next)
            dk_ref[R, :] += lax.dot_general(ds, qn, TN, preferred_element_type=F32)
            dv_ref[R, :] += lax.dot_general(p_, don, TN, preferred_element_type=F32)

    return pl.pallas_call(
        kern, name=name, grid=(geo.nblk, HD),
        in_specs=[geo.cur_g, geo.cur_g, geo.cur_g, geo.cur, geo.cur, geo.cur, geo.prev_g, geo.prev_g,
                  geo.next_g, geo.next, geo.next, geo.next],
        out_specs=[geo.cur] * 3, out_shape=[_sds((T, W), F32)] * 3,
        compiler_params=_params(2))(qd, kd, vd, do, delta, lse, kd, vd, qd, do, delta, lse)


def _place():
    return lax.axis_index("x"), lax.axis_index("y"), lax.axis_index("c")


def _other_chips(x, y):
    return [(1 - x, y), (x, 1 - y), (1 - x, 1 - y)]


def _kind(name, shard_shape):
    if name in ROW_SHARDED:
        return "row"
    return "col" if shard_shape[1] % LANES == 0 else "stack"


def _remote(src, dst, send_sem, recv_sem, to):
    return pltpu.make_async_remote_copy(src_ref=src, dst_ref=dst, send_sem=send_sem, recv_sem=recv_sem,
                                        device_id=to, device_id_type=MESH_ID)


def _row_tile(rows, cols, itemsize, align):
    return _pick(rows, max(align, (2 * 1024 * 1024) // (cols * itemsize)), align)


def _dma_sems(n):
    return [pltpu.SemaphoreType.DMA((n,)), pltpu.SemaphoreType.DMA((n,))]


def _gather_plan(shard_shapes):
    info, buf_shapes = {}, {}
    for out_name, names in GATHER_PLAN:
        r, c = shard_shapes[names[0]]
        kind = _kind(names[0], (r, c))
        assert kind == "col" or len(names) == 1, out_name
        buf_shapes[out_name] = (r, 4 * c * len(names)) if kind == "col" else (4, r, c)
        for i, n in enumerate(names):
            assert tuple(shard_shapes[n]) == (r, c), n
            info[n] = (out_name, kind, i * 4 * c, r, c)
    return info, buf_shapes


def _place_own(shard, buf_shape, kind, base, me, name, prev=None):
    r, c = shard.shape
    tr = _row_tile(r, c, 2, 16)

    def kern(me_ref, x_ref, *rest):
        rest[-1][...] = x_ref[...]

    if kind == "col":
        out_spec = pl.BlockSpec((tr, c), lambda i, me_ref: (i, base // c + me_ref[0]))
    else:
        out_spec = pl.BlockSpec((None, tr, c), lambda i, me_ref: (me_ref[0], i, 0))
    in_specs = [pl.BlockSpec((tr, c), lambda i, me_ref: (i, 0))] + ([ANY] if prev is not None else [])
    grid_spec = pltpu.PrefetchScalarGridSpec(num_scalar_prefetch=1, grid=(r // tr,), in_specs=in_specs,
                                             out_specs=out_spec)
    args = (me, shard) + ((prev,) if prev is not None else ())
    return pl.pallas_call(kern, name=name, grid_spec=grid_spec, out_shape=_sds(buf_shape, shard.dtype),
                          input_output_aliases={2: 0} if prev is not None else {}, compiler_params=_params(1))(*args)


def _buffers_of(names, info):
    out_names = []
    for n in names:
        if info[n][0] not in out_names:
            out_names.append(info[n][0])
    return out_names


def _ag_side(names, shards, bufs, info):
    out_names = _buffers_of(names, info)
    n_w = len(names)

    def region(outs, w, chip, h):
        out_name, kind, base, r, cc = info[names[w]]
        o = outs[out_names.index(out_name)]
        rows = pl.ds(h * (r // 2), r // 2)
        if kind == "col":
            return o.at[rows, pl.ds(pl.multiple_of(base + chip * cc, LANES), cc)]
        return o.at[chip, rows, :]

    def hop(first, sending):
        def fn(ins, outs, sems):
            x, y, c = _place()
            me, sibling, cps = 2 * x + y, (x, y, 1 - c), []
            for w in range(n_w):
                r = info[names[w]][3]
                for j, (px, py) in enumerate(_other_chips(x, y)):
                    k = 3 * w + j + (0 if first else 3 * n_w)
                    if first and sending:
                        src, dst, to = ins[w].at[pl.ds(c * (r // 2), r // 2), :], region(outs, w, me, c), (px, py, c)
                    elif first:
                        src = dst = region(outs, w, 2 * px + py, c)
                        to = (px, py, c)
                    else:
                        src = dst = region(outs, w, 2 * px + py, c if sending else 1 - c)
                        to = sibling
                    cps.append(_remote(src, dst, sems[0].at[k], sems[1].at[k], to))
            return cps
        return fn

    return _Side([shards[n] for n in names] + [bufs[o] for o in out_names],
                 [_sds(bufs[o].shape, bufs[o].dtype) for o in out_names], {n_w + i: i for i in range(len(out_names))},
                 _dma_sems(6 * n_w), [(hop(True, True), hop(True, False)), (hop(False, True), hop(False, False))])


def _rs_sibling_side(views):
    n = len(views)

    def fn(sending):
        def copies(ins, outs, sems):
            x, y, c = _place()
            return [_remote(ins[w].at[:, 1 - c] if sending else outs[w], outs[w], sems[0].at[w], sems[1].at[w],
                            (x, y, 1 - c)) for w in range(n)]
        return copies

    return _Side(views, [_sds((v.shape[0],) + v.shape[2:], v.dtype) for v in views], {}, _dma_sems(n),
                 [(fn(True), fn(False))])


def _rs_chips_side(parts, kinds, widths):
    n = len(parts)

    def piece(ins, w, chip):
        if kinds[w] == "col":
            return ins[w].at[0, :, pl.ds(pl.multiple_of(chip * widths[w], LANES), widths[w])]
        return ins[w].at[chip]

    def fn(sending):
        def copies(ins, outs, sems):
            x, y, c = _place()
            cps = []
            for w in range(n):
                for j, (px, py) in enumerate(_other_chips(x, y)):
                    k = 3 * w + j
                    src = piece(ins, w, 2 * px + py) if sending else outs[w].at[j]
                    cps.append(_remote(src, outs[w].at[j], sems[0].at[k], sems[1].at[k], (px, py, c)))
            return cps
        return copies

    return _Side(parts, [_sds((3, p_.shape[1], widths[w]), p_.dtype) for w, p_ in enumerate(parts)], {},
                 _dma_sems(3 * n), [(fn(True), fn(False))])


def _rs_join_side(halves):
    n = len(halves)

    def fn(sending):
        def copies(ins, outs, sems):
            x, y, c = _place()
            return [_remote(ins[w] if sending else outs[w], outs[w], sems[0].at[w], sems[1].at[w], (x, y, 1 - c))
                    for w in range(n)]
        return copies

    return _Side(halves, [_sds(h.shape, h.dtype) for h in halves], {}, _dma_sems(n), [(fn(True), fn(False))])


def _pair_sum(g, got, c_idx, name):
    n, _, rows, C = g.shape
    tr = _row_tile(rows, C, 4, 16)

    def kern(c_ref, a_ref, b_ref, o_ref):
        o_ref[...] = (a_ref[...] + b_ref[...]).astype(o_ref.dtype)

    grid_spec = pltpu.PrefetchScalarGridSpec(
        num_scalar_prefetch=1, grid=(n, rows // tr),
        in_specs=[pl.BlockSpec((None, None, tr, C), lambda j, i, c_ref: (j, c_ref[0], i, 0)),
                  pl.BlockSpec((None, tr, C), lambda j, i, c_ref: (j, i, 0))],
        out_specs=pl.BlockSpec((None, tr, C), lambda j, i, c_ref: (j, i, 0)))
    return pl.pallas_call(kern, name=name, grid_spec=grid_spec, out_shape=_sds((n, rows, C), BF16),
                          compiler_params=_params(2))(c_idx, g, got)


def _sum_pieces(pair, recv, kind, me, name):
    _, rows, c = recv.shape
    tr = _row_tile(rows, c, 8, 16)

    def kern(me_ref, own_ref, r_ref, o_ref):
        acc = own_ref[...].astype(F32)
        for j in range(3):
            acc = acc + r_ref[j].astype(F32)
        o_ref[...] = acc

    if kind == "col":
        own_spec = pl.BlockSpec((None, tr, c), lambda i, me_ref: (0, i, me_ref[0]))
    else:
        own_spec = pl.BlockSpec((None, tr, c), lambda i, me_ref: (me_ref[0], i, 0))
    grid_spec = pltpu.PrefetchScalarGridSpec(
        num_scalar_prefetch=1, grid=(rows // tr,),
        in_specs=[own_spec, pl.BlockSpec((3, tr, c), lambda i, me_ref: (0, i, 0))],
        out_specs=pl.BlockSpec((tr, c), lambda i, me_ref: (i, 0)))
    return pl.pallas_call(kern, name=name, grid_spec=grid_spec, out_shape=_sds((rows, c), F32),
                          compiler_params=_params(1))(me, pair, recv)


def _all_reduce_small(vec):
    N = vec.shape[1]
    n_dev = 8

    def body(v_ref, out_ref, slots, send_sems, recv_sems):
        x, y, c = _place()
        me = 4 * x + 2 * y + c
        slots[me] = v_ref[...]
        sent = []
        for k in range(1, n_dev):
            px, py, pc = x ^ (k >> 2), y ^ ((k >> 1) & 1), c ^ (k & 1)
            cp = pltpu.make_async_remote_copy(src_ref=v_ref, dst_ref=slots.at[me], send_sem=send_sems.at[k - 1],
                                              recv_sem=recv_sems.at[k - 1], device_id=(px, py, pc),
                                              device_id_type=MESH_ID)
            cp.start()
            sent.append(cp)
        for k in range(1, n_dev):
            px, py, pc = x ^ (k >> 2), y ^ ((k >> 1) & 1), c ^ (k & 1)
            slot = slots.at[4 * px + 2 * py + pc]
            pltpu.make_async_remote_copy(src_ref=slot, dst_ref=slot, send_sem=send_sems.at[k - 1],
                                         recv_sem=recv_sems.at[k - 1], device_id=(px, py, pc),
                                         device_id_type=MESH_ID).wait_recv()
        for cp in sent:
            cp.wait_send()
        acc = slots[0]
        for j in range(1, n_dev):
            acc = acc + slots[j]
        out_ref[...] = acc

    vm = pl.BlockSpec(memory_space=pltpu.VMEM)
    return pl.pallas_call(
        body, name="ar_gains", out_shape=_sds((1, N), F32), in_specs=[vm], out_specs=vm,
        scratch_shapes=[pltpu.VMEM((n_dev, 1, N), F32), pltpu.SemaphoreType.DMA((n_dev - 1,)),
                        pltpu.SemaphoreType.DMA((n_dev - 1,))])(vec)


def _adamw_math(wv, gv, mv, vv):
    m2 = ADAM_B1 * mv + (1.0 - ADAM_B1) * gv
    v2 = ADAM_B2 * vv + (1.0 - ADAM_B2) * (gv * gv)
    m_hat = m2 / (1.0 - ADAM_B1 ** ADAM_STEP)
    v_hat = v2 / (1.0 - ADAM_B2 ** ADAM_STEP)
    return -ADAM_LR * (m_hat / (jnp.sqrt(v_hat) + ADAM_EPS) + ADAM_WD * wv), m2, v2


def _adamw(w, g, m, v, name):
    R, C = w.shape
    tr = _row_tile(R, C, 8, 8)
    return _vcall(_adamw_math, (R // tr,), (w, g, m, v), [_rows(tr, C)] * 4, [_sds((R, C), F32)] * 3,
                  [_rows(tr, C)] * 3, name)


def _adamw_halves(w, own, recv, m, v, c_idx, name):
    R, C = w.shape
    rows = R // 2
    tr = _row_tile(rows, C, 8, 8)
    nb = rows // tr

    def kern(c_ref, w_ref, own_ref, recv_ref, m_ref, v_ref, g_out, d_out, m_out, v_out):
        def update(g_ref):
            g = g_ref[...]
            g_out[...] = g
            d_out[...], m_out[...], v_out[...] = _adamw_math(w_ref[...], g, m_ref[...], v_ref[...])

        @pl.when(pl.program_id(0) == c_ref[0])
        def _():
            update(own_ref)

        @pl.when(pl.program_id(0) != c_ref[0])
        def _():
            update(recv_ref)

    full = pl.BlockSpec((tr, C), lambda h, i, c_ref: (h * nb + i, 0))
    own_spec = pl.BlockSpec((tr, C), lambda h, i, c_ref: (jnp.where(h == c_ref[0], i, 0), 0))
    recv_spec = pl.BlockSpec((tr, C), lambda h, i, c_ref: (jnp.where(h == c_ref[0], 0, i), 0))
    grid_spec = pltpu.PrefetchScalarGridSpec(num_scalar_prefetch=1, grid=(2, nb),
                                             in_specs=[full, own_spec, recv_spec, full, full], out_specs=[full] * 4)
    return pl.pallas_call(kern, name=name, grid_spec=grid_spec, out_shape=[_sds((R, C), F32)] * 4,
                          compiler_params=_params(2))(c_idx, w, own, recv, m, v)


def _pad_to(a, n, axis):
    extra = n - a.shape[axis]
    if extra == 0:
        return a
    pads = [(0, 0)] * a.ndim
    pads[axis] = (0, extra)
    return jnp.pad(a, pads)


def _round_up(n, m):
    return -(-n // m) * m


def _natural(buf, kind):
    if kind == "col":
        return buf
    n, r, c = buf.shape
    return buf.reshape(n * r, c) if kind == "row" else buf.transpose(1, 0, 2).reshape(r, n * c)


def _halves_view(g, kind, shard_shape):
    r, c = shard_shape
    if kind == "col":
        return g.reshape(1, 2, r // 2, 4 * c)
    if kind == "stack":
        g = g.reshape(r, 4, c).transpose(1, 0, 2)
    return g.reshape(4, 2, r // 2, c)


def _pack_small(vals):
    return jnp.concatenate([_pad_to(vals[n].reshape(1, -1), _round_up(vals[n].size, LANES), 1) for n in SMALL], axis=1)


def _unpack_small(vec, shapes):
    out, off = {}, 0
    for n in SMALL:
        size = int(np.prod(shapes[n]))
        out[n] = vec[:, off:off + size].reshape(shapes[n])
        off += _round_up(size, LANES)
    return out


def _mm_s(sched, a, b, mode, out_dtype, name, **kw):
    side = sched.side(name)
    if side is None:
        return _mm(a, b, mode, out_dtype, name, **kw)
    out, side_outs = _mm(a, b, mode, out_dtype, name, side=side, **kw)
    sched.done(name, side_outs)
    return out


def _call_s(sched, name, n_out, fn):
    side = sched.side(name)
    outs = fn(side)
    if side is not None:
        sched.done(name, list(outs[n_out:]))
    return outs[:n_out]


def _ffn_fwd(sched, x, g, tf, tag):
    w = tag[-1]
    n = _rms_fwd(x, g, f"{tag}_norm")
    a, b, act = _call_s(sched, f"{tag}_gate_up", 3,
                        lambda side: _gate_up(n, sched.weight(f"w{w}_gu"), tf, f"{tag}_gate_up", side=side))
    out = _mm_s(sched, act, sched.weight(f"w{w}_down"), "nn", F32, f"{tag}_down", res=x, alpha=0.5)
    return out, (n, a, b, act)


def _ffn_bwd(sched, dout, x, g, saved, tf, tag):
    w = tag[-1]
    w_gu, w_d = sched.weight(f"w{w}_gu"), sched.weight(f"w{w}_down")
    n, a, b, act = saved
    F = act.shape[1]
    dout_b = dout.astype(BF16)
    sched.grad(f"w{w}_down", _mm_s(sched, act, dout_b, "tn", F32, f"{tag}_d_wdown", alpha=0.5))
    da, db = _call_s(sched, f"{tag}_d_act", 2,
                     lambda side: _d_gate_up(dout_b, w_d, a, b, tf, f"{tag}_d_act", side=side))
    sched.grad(f"w{w}_gate", _mm_s(sched, n, da, "tn", F32, f"{tag}_d_wgate"))
    sched.grad(f"w{w}_up", _mm_s(sched, n, db, "tn", F32, f"{tag}_d_wup"))
    dn = _mm_s(sched, da, w_gu, "nt", F32, f"{tag}_d_norm", a2=db, b2=w_gu, b2_k_offset=F)
    return _rms_bwd_call(x, g, dn, dout, f"{tag}_d_x")


def _local_step(sched, x, p, pos_b, target, Gn, dims):
    T, D = x.shape
    H, HD, QL, KVL, LP, tf = dims["H"], dims["HD"], dims["QL"], dims["KVL"], dims["LP"], dims["tf"]
    Wd = HD * DIL_HEAD
    scale_mla, scale_dil = MLA_QK ** -0.5, DIL_HEAD ** -0.5
    kr_col = (QL + KVL) // LANES
    tab_mla = tuple(_rope_tables(pos_b, MLA_ROPE, "rope_tab_mla"))
    tab_dil = tuple(_rope_tables(pos_b, DIL_ROT, "rope_tab_dil"))

    W = sched.weight
    mm = functools.partial(_mm_s, sched)

    x1, ffn1 = _ffn_fwd(sched, x, Gn["g_ffn1"], tf, "ffn1")
    h = _rms_fwd(x1, Gn["g_mix"], "mix_norm")
    lat = mm(h, W("w_lat"), "nn", F32, "proj_lat")
    pd = mm(h, W("w_dil"), "nn", BF16, "proj_dil")
    pg = mm(h, W("w_gin"), "nn", BF16, "proj_gate")

    cq, ckv = _lat_fwd(lat, Gn["g_cq"], Gn["g_ckv"], "lat_norm")
    q_raw = mm(cq, W("w_uq"), "nn", F32, "mla_q_up")
    kv = mm(ckv, W("w_ukv"), "nn", F32, "mla_kv_up")
    q = _mla_q_prep(q_raw, Gn["g_q_mla"], tab_mla, H, scale_mla, "mla_q_prep")
    k, v = _mla_k_prep(kv, lat, kr_col, Gn["g_k_mla"], tab_mla, H, "mla_k_prep")
    o_mla, lse_mla = _call_s(sched, "mla_attn", 2, lambda side: _mla_fwd(q, k, v, H, "mla_attn", side=side))

    qd, kd, vd = _dil_prep(pd, Gn["g_q_dil"], Gn["g_k_dil"], tab_dil, HD, scale_dil, "dil_prep")
    og, lg = [], []
    for grp, (win, dil) in enumerate(DIL_GROUPS):
        o_, l_ = _dil_fwd(qd, kd, vd, grp, dil, HD, f"dil_attn{grp}")
        og.append(o_)
        lg.append(l_)
    o_dil, lse_dil = _dil_merge(og, lg, "dil_merge")

    bm = mm(o_mla, W("w_br_mla"), "nn", F32, "branch_mla")
    bd = mm(o_dil, W("w_br_dil"), "nn", F32, "branch_dil")
    merged = _gate_merge(pg, bm, bd, "gate_merge")
    x2 = mm(merged, W("w_o"), "nn", F32, "out_proj", res=x1)

    x3, ffn2 = _ffn_fwd(sched, x2, Gn["g_ffn2"], tf, "ffn2")
    n4 = _rms_fwd(x3, Gn["g_ple"], "ple_norm")
    zg = mm(n4, W("w_ple_gate"), "nn", F32, "ple_gate")
    p_b = p.astype(BF16)
    pp = mm(p_b, W("w_ple_proj"), "nn", F32, "ple_proj")
    dy, dpp, dzg, loss = _ple_loss(x3, zg, pp, target, "ple_loss")

    gg = {}
    sched.grad("w_ple_proj", mm(p_b, dpp, "tn", F32, "d_w_ple_proj"))
    sched.grad("w_ple_gate", mm(n4, dzg, "tn", F32, "d_w_ple_gate"))
    dn4 = mm(dzg, W("w_ple_gate"), "nt", F32, "d_ple_norm")
    dx3, gg["g_ple"] = _rms_bwd_call(x3, Gn["g_ple"], dn4, dy, "d_x3")

    dx2, gg["g_ffn2"] = _ffn_bwd(sched, dx3, x2, Gn["g_ffn2"], ffn2, tf, "ffn2")

    dx2_b = dx2.astype(BF16)
    sched.grad("w_o", mm(merged, dx2_b, "tn", F32, "d_w_o"))
    dmerged = mm(dx2_b, W("w_o"), "nt", F32, "d_merged")
    dbm, dbd, dpg = _gate_bwd(dmerged, pg, bm, bd, "d_gate")
    sched.grad("w_br_mla", mm(o_mla, dbm, "tn", F32, "d_w_br_mla"))
    sched.grad("w_br_dil", mm(o_dil, dbd, "tn", F32, "d_w_br_dil"))
    do_mla = mm(dbm, W("w_br_mla"), "nt", BF16, "d_o_mla")
    do_dil = mm(dbd, W("w_br_dil"), "nt", F32, "d_o_dil")
    delta_dil = _dil_delta(do_dil, o_dil, HD, "dil_delta")

    dh = mm(dpg, W("w_gin"), "nt", F32, "d_h_gate")
    sched.grad("w_gin", mm(h, dpg, "tn", F32, "d_w_gin"))
    gq_d, gk_d = Gn["g_q_dil"], Gn["g_k_dil"]
    dgq_d, dgk_d = [], []
    for grp, (win, dil) in enumerate(DIL_GROUPS):
        dq_, dk_, dv_ = _dil_bwd(qd, kd, vd, do_dil, delta_dil, lse_dil, grp, dil, HD, f"dil_bwd{grp}")
        dpd_g, dgq_, dgk_ = _dil_prep_bwd(dq_, dk_, dv_, pd, grp, gq_d, gk_d, tab_dil, HD, scale_dil, f"d_dil_prep{grp}")
        dgq_d.append(dgq_)
        dgk_d.append(dgk_)
        w_g = W("w_dil")[:, grp * 3 * Wd:(grp + 1) * 3 * Wd]
        dh = mm(dpd_g, w_g, "nt", F32, f"d_h_dil{grp}", res=dh)
        sched.grad(f"w_dil{grp}", mm(h, dpd_g, "tn", F32, f"d_w_dil{grp}"))
    gg["g_q_dil"] = jnp.concatenate(dgq_d, axis=0)
    gg["g_k_dil"] = jnp.concatenate(dgk_d, axis=0)

    dq, dk, dv = _call_s(sched, "mla_bwd", 3,
                         lambda side: _mla_bwd(q, k, v, do_mla, o_mla, lse_mla, H, "mla_bwd", side=side))
    dq_raw, gg["g_q_mla"] = _mla_q_bwd(dq, q_raw, Gn["g_q_mla"], tab_mla, H, scale_mla, "d_mla_q_prep")
    dkv, dkr, gg["g_k_mla"] = _mla_k_bwd(dk, dv, kv, lat, kr_col, Gn["g_k_mla"], tab_mla, H, "d_mla_k_prep")
    sched.grad("w_uq", mm(cq, dq_raw, "tn", F32, "d_w_uq"))
    sched.grad("w_ukv", mm(ckv, dkv, "tn", F32, "d_w_ukv"))
    dcq = mm(dq_raw, W("w_uq"), "nt", F32, "d_cq")
    dckv = mm(dkv, W("w_ukv"), "nt", F32, "d_ckv")
    dlat, gg["g_cq"], gg["g_ckv"] = _lat_bwd(dcq, dckv, dkr, lat, Gn["g_cq"], Gn["g_ckv"], "d_lat_norm")
    dh = mm(dlat, W("w_lat"), "nt", F32, "d_h_lat", res=dh)
    sched.grad("w_lat", mm(h, dlat, "tn", F32, "d_w_lat"))

    dx1, gg["g_mix"] = _rms_bwd_call(x1, Gn["g_mix"], dh, dx2, "d_x1")
    dx, gg["g_ffn1"] = _ffn_bwd(sched, dx1, x, Gn["g_ffn1"], ffn1, tf, "ffn1")
    return loss, dx, gg


def _layout_weight(name, full, dims):
    H, QL, KVL, LP, Wd = dims["H"], dims["QL"], dims["KVL"], dims["LP"], dims["HD"] * DIL_HEAD
    off_dil = QL + KVL + MLA_ROPE
    off_gate = off_dil + 3 * len(DIL_GROUPS) * Wd
    if name == "w_lat":
        return _pad_to(full("w_in")[:, :off_dil], LP, 1)
    if name == "w_dil":
        return full("w_in")[:, off_dil:off_gate]
    if name == "w_gin":
        return full("w_in")[:, off_gate:]
    if name == "w_uq":
        return _pad_to(full("w_uq").reshape(QL, H, MLA_QK), MLA_QK_PAD, 2).reshape(QL, H * MLA_QK_PAD)
    return full(name)


def _natural_grad(name, gw, dims):
    H, QL, KVL = dims["H"], dims["QL"], dims["KVL"]
    if name == "w_in":
        return jnp.concatenate([gw["w_lat"][:, :QL + KVL + MLA_ROPE]] + [gw[f"w_dil{g}"] for g in range(len(DIL_GROUPS))]
                               + [gw["w_gin"]], axis=1)
    if name == "w_uq":
        return gw["w_uq"].reshape(QL, H, MLA_QK_PAD)[:, :, :MLA_QK].reshape(QL, H * MLA_QK)
    return gw[name]


WEIGHT_SOURCES = {"w1_gu": ("w1_gate", "w1_up"), "w2_gu": ("w2_gate", "w2_up"), "w_lat": ("w_in",), "w_dil": ("w_in",),
                  "w_gin": ("w_in",)}
AG_FIRST = ("w1_gate", "w1_up")
AG_RIDES = {"ffn1_gate_up": ("w1_down", "w_in"), "ffn1_down": ("w_uq", "w_ukv"),
            "mla_attn": ("w_br_mla", "w_br_dil", "w_o", "w_ple_gate", "w_ple_proj", "w2_gate", "w2_up", "w2_down")}
RS_FFN2 = ("w_ple_proj", "w_ple_gate", "w2_down", "w2_gate", "w2_up")
RS_MIXER = ("w_o", "w_br_mla", "w_br_dil", "w_in", "w_uq", "w_ukv")
RS_RIDES = {
    "d_merged": (("sibling", RS_FFN2),),
    "mla_bwd": (("chips", RS_FFN2),),
    "d_h_lat": (("join", RS_FFN2),),
    "ffn1_d_wdown": (("sibling", RS_MIXER),),
    "ffn1_d_act": (("chips", ("w_o", "w_br_mla", "w_br_dil", "w_uq", "w_ukv")), ("sibling", ("w1_down",))),
    "ffn1_d_wgate": (("chips", ("w1_down",)),),
    "ffn1_d_wup": (("sibling", ("w1_gate",)),),
    "ffn1_d_norm": (("chips", ("w_in", "w1_gate")), ("sibling", ("w1_up",))),
}
RS_LAST = ((("chips", ("w1_up",)),), (("join", RS_MIXER + ("w1_down", "w1_gate", "w1_up")),))


class _MeshSchedule:
    def __init__(self, w, m, v, dims):
        self.w, self.m, self.v, self.dims = w, m, v, dims
        self.shapes = {n: tuple(w[n].shape[1:]) for n in BIG}
        self.kinds = {n: _kind(n, self.shapes[n]) for n in BIG}
        self.info, buf_shapes = _gather_plan(self.shapes)
        x, y, c = _place()
        self.me = (2 * x + y).astype(jnp.int32).reshape(1)
        self.c_idx = c.astype(jnp.int32).reshape(1)
        self.shards = {n: w[n][0].astype(BF16) for n in BIG}
        self.bufs, self.gathered, self.layout = {}, set(), {}
        for n in BIG:
            out_name, kind, base, _, _ = self.info[n]
            self.bufs[out_name] = _place_own(self.shards[n], buf_shapes[out_name], kind, base, self.me, f"ag_own_{n}",
                                             prev=self.bufs.get(out_name))
        self.gw, self.views, self.pairs, self.halves, self.recv = {}, {}, {}, {}, {}
        self._ag_done(AG_FIRST, _run_side(self._ag(AG_FIRST), "ag_first"))

    def _ag(self, names):
        return _ag_side(names, self.shards, self.bufs, self.info)

    def _ag_done(self, names, outs):
        for out_name, buf in zip(_buffers_of(names, self.info), outs):
            self.bufs[out_name] = buf
        self.gathered.update(names)

    def weight(self, name):
        if name not in self.layout:
            assert all(s in self.gathered for s in WEIGHT_SOURCES.get(name, (name,))), name
            if name in self.bufs and name not in self.info:
                self.layout[name] = self.bufs[name]
            else:
                full = lambda n: _natural(self.bufs[self.info[n][0]], self.info[n][1])
                self.layout[name] = _layout_weight(name, full, self.dims)
        return self.layout[name]

    def grad(self, name, g):
        self.gw[name] = g

    def _rs_side(self, stages):
        sides = []
        for stage, names in stages:
            if stage == "sibling":
                for n in names:
                    self.views[n] = _halves_view(_natural_grad(n, self.gw, self.dims), self.kinds[n], self.shapes[n])
                sides.append(_rs_sibling_side([self.views[n] for n in names]))
            elif stage == "chips":
                sides.append(_rs_chips_side([self.pairs[n] for n in names], [self.kinds[n] for n in names],
                                            [self.shapes[n][1] for n in names]))
            else:
                sides.append(_rs_join_side([self.halves[n] for n in names]))
        return sides[0] if len(sides) == 1 else _merge_sides(sides)

    def _rs_done(self, stages, outs):
        for stage, names in stages:
            got, outs = outs[:len(names)], outs[len(names):]
            for n, a in zip(names, got):
                if stage == "sibling":
                    self.pairs[n] = _pair_sum(self.views[n], a, self.c_idx, f"rs_pair_{n}")
                elif stage == "chips":
                    self.halves[n] = _sum_pieces(self.pairs[n], a, self.kinds[n], self.me, f"rs_sum_{n}")
                else:
                    self.recv[n] = a

    def side(self, tag):
        if tag in AG_RIDES:
            return self._ag(AG_RIDES[tag])
        if tag in RS_RIDES:
            return self._rs_side(RS_RIDES[tag])
        return None

    def done(self, tag, outs):
        if tag in AG_RIDES:
            self._ag_done(AG_RIDES[tag], outs)
        else:
            self._rs_done(RS_RIDES[tag], outs)

    def finish(self):
        for k, stages in enumerate(RS_LAST):
            self._rs_done(stages, _run_side(self._rs_side(stages), f"rs_last{k}"))
        outs = {"grad": {}, "delta": {}, "m": {}, "v": {}}
        for n in BIG:
            res = _adamw_halves(self.w[n][0], self.halves[n], self.recv[n], self.m[n][0], self.v[n][0], self.c_idx,
                                f"adamw_{n}")
            for kind, a in zip(("grad", "delta", "m", "v"), res):
                outs[kind][n] = a.reshape((1,) + a.shape)
        return outs


def _step(x, p, positions, loss_target, w, m, v):
    T, D = x.shape[1], x.shape[2]
    QL, KVL = w["g_cq"].shape[1], w["g_ckv"].shape[1]
    dims = {
        "H": 4 * w["w_uq"].shape[2] // MLA_QK, "HD": w["w_br_dil"].shape[1] // DIL_HEAD, "QL": QL, "KVL": KVL,
        "LP": _round_up(QL + KVL + MLA_ROPE, LANES), "tf": _pick(4 * w["w1_gate"].shape[2], 512),
    }
    small_shapes = {n: w[n].shape for n in SMALL}
    sched = _MeshSchedule(w, m, v, dims)
    Gn = {n: w[n] for n in SMALL}
    Gn["g_q_mla"] = _pad_to(Gn["g_q_mla"], MLA_QK_PAD, 1)
    Gn["g_k_mla"] = _pad_to(Gn["g_k_mla"], MLA_QK_PAD, 1)
    Gn["g_q_dil"] = Gn["g_q_dil"].reshape(len(DIL_GROUPS), 1, DIL_HEAD)
    Gn["g_k_dil"] = Gn["g_k_dil"].reshape(len(DIL_GROUPS), 1, DIL_HEAD)

    pos_b = jnp.broadcast_to(positions.astype(F32).reshape(T, 1), (T, LANES))
    loss, dx, gg = _local_step(sched, x[0], p[0, 0], pos_b, loss_target[0], Gn, dims)
    loss = lax.psum(loss[0, 0], ("x", "y", "c"))
    outs = sched.finish()

    gg["g_q_mla"] = gg["g_q_mla"][:, :MLA_QK]
    gg["g_k_mla"] = gg["g_k_mla"][:, :MLA_QK]
    g_small = _all_reduce_small(_pack_small(gg))
    d_s, m_s, v_s = _adamw(_pack_small({n: w[n] for n in SMALL}), g_small, _pack_small({n: m[n] for n in SMALL}),
                           _pack_small({n: v[n] for n in SMALL}), "adamw_gains")
    for kind, buf in (("grad", g_small), ("delta", d_s), ("m", m_s), ("v", v_s)):
        outs[kind].update(_unpack_small(buf, small_shapes))

    grad_x = dx.reshape(1, T, D)
    return (loss, grad_x, *[outs["grad"][n] for n in WEIGHTS], *[outs["delta"][n] for n in WEIGHTS],
            *[outs["m"][n] for n in WEIGHTS], *[outs["v"][n] for n in WEIGHTS])


def kernel(x, p, positions, g_ffn1, w1_gate, w1_up, w1_down, g_mix, w_in, g_cq, w_uq, g_ckv, w_ukv, g_q_mla, g_k_mla, g_q_dil, g_k_dil, w_br_mla, w_br_dil, w_o, g_ffn2, w2_gate, w2_up, w2_down, g_ple, w_ple_gate, w_ple_proj, loss_target, m_g_ffn1, m_w1_gate, m_w1_up, m_w1_down, m_g_mix, m_w_in, m_g_cq, m_w_uq, m_g_ckv, m_w_ukv, m_g_q_mla, m_g_k_mla, m_g_q_dil, m_g_k_dil, m_w_br_mla, m_w_br_dil, m_w_o, m_g_ffn2, m_w2_gate, m_w2_up, m_w2_down, m_g_ple, m_w_ple_gate, m_w_ple_proj, v_g_ffn1, v_w1_gate, v_w1_up, v_w1_down, v_g_mix, v_w_in, v_g_cq, v_w_uq, v_g_ckv, v_w_ukv, v_g_q_mla, v_g_k_mla, v_g_q_dil, v_g_k_dil, v_w_br_mla, v_w_br_dil, v_w_o, v_g_ffn2, v_w2_gate, v_w2_up, v_w2_down, v_g_ple, v_w_ple_gate, v_w_ple_proj):
    args = locals()
    w = {n: args[n] for n in WEIGHTS}
    m = {n: args["m_" + n] for n in WEIGHTS}
    v = {n: args["v_" + n] for n in WEIGHTS}
    return _step(x, p, positions, loss_target, w, m, v)
```

```python
import functools

import numpy as np
import jax
import jax.numpy as jnp
from jax import lax
from jax.experimental import pallas as pl
from jax.experimental.pallas import tpu as pltpu

F32 = jnp.float32
BF16 = jnp.bfloat16
MESH_ID = pl.DeviceIdType.MESH

MLA_NOPE = 128
MLA_ROPE = 64
MLA_V = 128
MLA_QK = MLA_NOPE + MLA_ROPE
MLA_QK_PAD = 256
DIL_GROUPS = ((128, 1), (512, 4), (2048, 16))
DIL_HEAD = 128
DIL_ROT = DIL_HEAD // 4
DIL_BLOCK = 128
ROPE_THETA = 500000.0
EPS = 1e-6
NEG = -1e30
ADAM_LR = 0.001
ADAM_B1 = 0.9
ADAM_B2 = 0.999
ADAM_EPS = 1e-08
ADAM_WD = 0.01
ADAM_STEP = 10

LANES = 128
VMEM_LIMIT_BYTES = 56 * 1024 * 1024
MM_VMEM_BYTES = 46 * 1024 * 1024

BIG = ("w1_gate", "w1_up", "w1_down", "w_in", "w_uq", "w_ukv", "w_br_mla", "w_br_dil", "w_o",
       "w2_gate", "w2_up", "w2_down", "w_ple_gate", "w_ple_proj")
GATHER_PLAN = (("w1_gu", ("w1_gate", "w1_up")), ("w1_down", ("w1_down",)), ("w_in", ("w_in",)), ("w_uq", ("w_uq",)),
               ("w_ukv", ("w_ukv",)), ("w_br_mla", ("w_br_mla",)), ("w_br_dil", ("w_br_dil",)), ("w_o", ("w_o",)),
               ("w2_gu", ("w2_gate", "w2_up")), ("w2_down", ("w2_down",)), ("w_ple_gate", ("w_ple_gate",)),
               ("w_ple_proj", ("w_ple_proj",)))
ROW_SHARDED = ("w1_down", "w_o", "w2_down", "w_ple_gate")
SMALL = ("g_ffn1", "g_mix", "g_cq", "g_ckv", "g_q_mla", "g_k_mla", "g_q_dil", "g_k_dil", "g_ffn2", "g_ple")
WEIGHTS = ("g_ffn1", "w1_gate", "w1_up", "w1_down", "g_mix", "w_in", "g_cq", "w_uq", "g_ckv", "w_ukv", "g_q_mla",
           "g_k_mla", "g_q_dil", "g_k_dil", "w_br_mla", "w_br_dil", "w_o", "g_ffn2", "w2_gate", "w2_up", "w2_down",
           "g_ple", "w_ple_gate", "w_ple_proj")


def _pick(n, target, align=LANES):
    if n <= target:
        return n
    t = (target // align) * align
    while t >= align:
        if n % t == 0:
            return t
        t -= align
    return n


def _params(n_axes):
    return pltpu.CompilerParams(dimension_semantics=("arbitrary",) * n_axes, vmem_limit_bytes=VMEM_LIMIT_BYTES)


def _sigmoid(x):
    return 1.0 / (1.0 + jnp.exp(-x))


ANY = pl.BlockSpec(memory_space=pl.ANY)


class _Side:
    def __init__(self, arrays, out_shapes, aliases, sem_shapes, phases):
        self.arrays, self.out_shapes, self.aliases = list(arrays), list(out_shapes), dict(aliases)
        self.sem_shapes, self.phases = list(sem_shapes), list(phases)

    def start(self, p, ins, outs, sems):
        for cp in self.phases[p][0](ins, outs, sems):
            cp.start()

    def wait(self, p, ins, outs, sems):
        for cp in self.phases[p][1](ins, outs, sems):
            cp.wait_recv()
        for cp in self.phases[p][0](ins, outs, sems):
            cp.wait_send()

    def run(self, step, n_steps, ins, outs, sems):
        n_ph = len(self.phases)
        assert n_ph <= 2
        starts = (0, int(0.85 * (n_steps - 1)))
        if n_steps <= n_ph:
            @pl.when(step == n_steps - 1)
            def _():
                for p in range(n_ph):
                    self.start(p, ins, outs, sems)
                    self.wait(p, ins, outs, sems)
            return
        for p in range(n_ph):
            @pl.when(step == starts[p])
            def _(p=p):
                if p > 0:
                    self.wait(p - 1, ins, outs, sems)
                self.start(p, ins, outs, sems)

        @pl.when(step == n_steps - 1)
        def _():
            self.wait(n_ph - 1, ins, outs, sems)


def _merge_sides(sides):
    arrays, out_shapes, aliases, sem_shapes, spans = [], [], {}, [], []
    for s in sides:
        assert len(s.phases) == 1
        spans.append((len(arrays), len(out_shapes), len(sem_shapes), s))
        aliases.update({len(arrays) + i: len(out_shapes) + o for i, o in s.aliases.items()})
        arrays += s.arrays
        out_shapes += s.out_shapes
        sem_shapes += s.sem_shapes

    def part(which):
        def fn(ins, outs, sems):
            cps = []
            for a0, o0, s0, s in spans:
                cps += s.phases[0][which](ins[a0:a0 + len(s.arrays)], outs[o0:o0 + len(s.out_shapes)],
                                          sems[s0:s0 + len(s.sem_shapes)])
            return cps
        return fn

    return _Side(arrays, out_shapes, aliases, sem_shapes, [(part(0), part(1))])


def _side_parts(side, n_lead, n_out):
    if side is None:
        return [], [], [], {}
    return (side.arrays, side.out_shapes, side.sem_shapes, {n_lead + i: n_out + o for i, o in side.aliases.items()})


def _carry(kern, side, n_lead, n_out, n_scratch, step_of, n_steps):
    if side is None:
        return kern
    a = n_lead
    b = a + len(side.arrays)
    c = b + n_out
    d = c + len(side.out_shapes)
    e = d + n_scratch

    def wrapped(*refs):
        side.run(step_of(), n_steps, refs[a:b], refs[c:d], refs[e:])
        kern(*refs[:a], *refs[b:c], *refs[d:e])

    return wrapped


def _run_side(side, name):
    n_in, n_out = len(side.arrays), len(side.out_shapes)

    def body(*refs):
        ins, outs, sems = refs[:n_in], refs[n_in:n_in + n_out], refs[n_in + n_out:]
        for p in range(len(side.phases)):
            side.start(p, ins, outs, sems)
            side.wait(p, ins, outs, sems)

    return pl.pallas_call(body, name=name, out_shape=side.out_shapes, in_specs=[ANY] * n_in, out_specs=[ANY] * n_out,
                          scratch_shapes=side.sem_shapes, input_output_aliases=side.aliases)(*side.arrays)


def _mm_tiles(M, N, K, n_pairs, out_bytes, has_res):
    tm = _pick(M, 1024)
    tks = sorted({_pick(K, t) for t in (8192, 5632, 4096, 2816, 2048, 1408, 1024, 512)}, reverse=True)
    tns = sorted({_pick(N, t) for t in (1536, 1024, 512)}, reverse=True)
    for tk in tks:
        for tn in tns:
            need = 4 * n_pairs * (tm * tk + tk * tn) + tm * tn * (4 * (K > tk) + 2 * out_bytes + 8 * has_res + 4)
            if need <= MM_VMEM_BYTES:
                return tm, tn, tk
    raise ValueError((M, N, K))


def _mm(a, b, mode, out_dtype, name, res=None, alpha=1.0, a2=None, b2=None, b2_k_offset=0, side=None):
    if mode == "nn":
        (M, K), (K2, N) = a.shape, b.shape
    elif mode == "nt":
        (M, K), (N, K2) = a.shape, b.shape
    else:
        (K, M), (K2, N) = a.shape, b.shape
    assert K == K2 or (mode == "nt" and K2 > K), (name, a.shape, b.shape)
    assert a.dtype == BF16 and b.dtype == BF16, name
    tm, tn, tk = _mm_tiles(M, N, K, 1 if a2 is None else 2, jnp.dtype(out_dtype).itemsize, res is not None)
    nk = K // tk
    assert b2_k_offset % tk == 0 and (b2_k_offset == 0 or mode == "nt"), name
    k_off2 = b2_k_offset // tk
    if mode == "nn":
        a_spec = pl.BlockSpec((tm, tk), lambda i, j, k: (i, k))
        b_spec = pl.BlockSpec((tk, tn), lambda i, j, k: (k, j))
        dims = (((1,), (0,)), ((), ()))
    elif mode == "nt":
        a_spec = pl.BlockSpec((tm, tk), lambda i, j, k: (i, k))
        b_spec = pl.BlockSpec((tn, tk), lambda i, j, k: (j, k))
        b2_spec = pl.BlockSpec((tn, tk), lambda i, j, k: (j, k + k_off2))
        dims = (((1,), (1,)), ((), ()))
    else:
        a_spec = pl.BlockSpec((tk, tm), lambda i, j, k: (k, i))
        b_spec = pl.BlockSpec((tk, tn), lambda i, j, k: (k, j))
        dims = (((0,), (0,)), ((), ()))
    o_spec = pl.BlockSpec((tm, tn), lambda i, j, k: (i, j))
    has_res = res is not None
    n_pairs = 1 if a2 is None else 2
    n_main = 2 * n_pairs + int(has_res)
    n_side_in = len(side.arrays) if side else 0
    n_side_out = len(side.out_shapes) if side else 0
    n_acc = 1 if nk > 1 else 0
    gi, gj = M // tm, N // tn
    n_steps = gi * gj * nk

    def kern(*refs):
        r_ref = refs[2 * n_pairs] if has_res else None
        o_ref = refs[n_main + n_side_in]
        if side:
            step = (pl.program_id(0) * gj + pl.program_id(1)) * nk + pl.program_id(2)
            side.run(step, n_steps, refs[n_main:n_main + n_side_in],
                     refs[n_main + n_side_in + 1:n_main + n_side_in + 1 + n_side_out],
                     refs[n_main + n_side_in + 1 + n_side_out + n_acc:])
        part = lax.dot_general(refs[0][...], refs[1][...], dims, preferred_element_type=F32)
        if n_pairs == 2:
            part = part + lax.dot_general(refs[2][...], refs[3][...], dims, preferred_element_type=F32)

        def finish(r):
            if alpha != 1.0:
                r = r * alpha
            if has_res:
                r = r_ref[...] + r
            o_ref[...] = r.astype(o_ref.dtype)

        if nk == 1:
            finish(part)
            return
        acc_ref = refs[n_main + n_side_in + 1 + n_side_out]
        k = pl.program_id(2)

        @pl.when(k == 0)
        def _():
            acc_ref[...] = part

        @pl.when(k > 0)
        def _():
            acc_ref[...] += part

        @pl.when(k == nk - 1)
        def _():
            finish(acc_ref[...])

    ins = (a, b) + ((a2, b2) if n_pairs == 2 else ()) + ((res,) if has_res else ())
    in_specs = [a_spec, b_spec] + ([a_spec, b2_spec if mode == "nt" else b_spec] if n_pairs == 2 else [])
    in_specs += [o_spec] if has_res else []
    out_shape = jax.ShapeDtypeStruct((M, N), out_dtype)
    scratch = [pltpu.VMEM((tm, tn), F32)] if nk > 1 else []
    if not side:
        return pl.pallas_call(kern, name=name, grid=(gi, gj, nk), in_specs=in_specs, out_specs=o_spec,
                              out_shape=out_shape, scratch_shapes=scratch, compiler_params=_params(3))(*ins)
    outs = pl.pallas_call(
        kern, name=name, grid=(gi, gj, nk), in_specs=in_specs + [ANY] * n_side_in,
        out_specs=[o_spec] + [ANY] * n_side_out, out_shape=[out_shape] + list(side.out_shapes),
        scratch_shapes=scratch + list(side.sem_shapes),
        input_output_aliases={n_main + i: 1 + o for i, o in side.aliases.items()},
        compiler_params=_params(3))(*ins, *side.arrays)
    return outs[0], list(outs[1:])


def _vcall(body, grid, ins, in_specs, out_shapes, out_specs, name, n_inner_acc=0, n_acc=0):
    n_in, n_out = len(ins), len(out_shapes)
    n_plain = n_out - n_acc - n_inner_acc

    def kern(*refs):
        vals = body(*[r[...] for r in refs[:n_in]])
        if not isinstance(vals, (tuple, list)):
            vals = (vals,)
        out_refs = refs[n_in:]
        inner_first = pl.program_id(len(grid) - 1) == 0
        first = inner_first
        for ax in range(len(grid) - 1):
            first = jnp.logical_and(first, pl.program_id(ax) == 0)
        for idx, (r, v) in enumerate(zip(out_refs, vals)):
            if idx < n_plain:
                r[...] = v.astype(r.dtype)
                continue
            start = inner_first if idx < n_plain + n_inner_acc else first

            @pl.when(start)
            def _(r=r, v=v):
                r[...] = v.astype(r.dtype)

            @pl.when(jnp.logical_not(start))
            def _(r=r, v=v):
                r[...] += v.astype(r.dtype)

    out = pl.pallas_call(kern, name=name, grid=grid, in_specs=in_specs, out_specs=out_specs, out_shape=out_shapes,
                         compiler_params=_params(len(grid)))(*ins)
    return out


def _rows(tm, c):
    return pl.BlockSpec((tm, c), lambda i: (i, 0))


def _vec(c):
    return pl.BlockSpec((1, c), lambda i: (0, 0))


def _sds(shape, dtype):
    return jax.ShapeDtypeStruct(shape, dtype)


def _rstd(x, c):
    return lax.rsqrt(jnp.sum(x * x, axis=-1, keepdims=True) * (1.0 / c) + EPS)


def _rms_bwd(xh, r, g, dn, c):
    u = dn * g
    dx = r * (u - xh * (jnp.sum(xh * u, axis=-1, keepdims=True) * (1.0 / c)))
    return dx, jnp.sum(dn * xh, axis=0, keepdims=True)


def _rope(t, c, sa, sb, half):
    return t * c + pltpu.roll(t, LANES - half, 1) * sa + pltpu.roll(t, half, 1) * sb


def _rope_t(d, c, sa, sb, half):
    return d * c + pltpu.roll(d * sa, half, 1) + pltpu.roll(d * sb, LANES - half, 1)


def _rope_tables(pos_b, rd, name):
    T = pos_b.shape[0]
    half = rd // 2
    inv = ROPE_THETA ** (-jnp.arange(half, dtype=F32) * 2.0 / rd)
    inv_full = jnp.concatenate([inv, inv, jnp.zeros((LANES - rd,), F32)]).reshape(1, LANES)
    lane = np.arange(LANES)
    ma = jnp.asarray((lane < half).astype(np.float32)).reshape(1, LANES)
    mb = jnp.asarray(((lane >= half) & (lane < rd)).astype(np.float32)).reshape(1, LANES)
    tm = _pick(T, 1024, 8)

    def body(pos, invf, a, b):
        ang = pos * invf
        c, s = jnp.cos(ang), jnp.sin(ang)
        inside = a + b
        return c * inside + (1.0 - inside), -s * a, s * b

    return _vcall(body, (T // tm,), (pos_b, inv_full, ma, mb), [_rows(tm, LANES)] + [_vec(LANES)] * 3,
                  [_sds((T, LANES), F32)] * 3, [_rows(tm, LANES)] * 3, name)


def _rms_fwd(x, g, name):
    T, C = x.shape
    tm = _pick(T, 512, 8)

    def body(xv, gv):
        return xv * _rstd(xv, C) * gv

    return _vcall(body, (T // tm,), (x, g), [_rows(tm, C), _vec(C)], [_sds((T, C), BF16)], [_rows(tm, C)], name)[0]


def _rms_bwd_call(x, g, dn, dres, name):
    T, C = x.shape
    tm = _pick(T, 256, 8)

    def body(xv, gv, dnv, drv):
        r = _rstd(xv, C)
        dx, dg = _rms_bwd(xv * r, r, gv, dnv.astype(F32), C)
        return drv + dx, dg

    return _vcall(body, (T // tm,), (x, g, dn, dres), [_rows(tm, C), _vec(C), _rows(tm, C), _rows(tm, C)],
                  [_sds((T, C), F32), _sds((1, C), F32)], [_rows(tm, C), _vec(C)], name, n_acc=1)


def _gate_up(n, w_gu, tf, name, side=None):
    T, D = n.shape
    F = w_gu.shape[1] // 2
    tm = _pick(T, 1024, 16)
    nf = F // tf

    def kern(n_ref, wg_ref, wu_ref, a_ref, b_ref, act_ref):
        x = n_ref[...]
        a = jnp.dot(x, wg_ref[...], preferred_element_type=F32)
        b = jnp.dot(x, wu_ref[...], preferred_element_type=F32)
        a_ref[...] = a.astype(BF16)
        b_ref[...] = b.astype(BF16)
        act_ref[...] = (a * _sigmoid(a) * b).astype(BF16)

    tile = pl.BlockSpec((tm, tf), lambda i, j: (i, j))
    s_in, s_out, s_sems, s_alias = _side_parts(side, 3, 3)
    step_of = lambda: pl.program_id(0) * nf + pl.program_id(1)
    outs = pl.pallas_call(
        _carry(kern, side, 3, 3, 0, step_of, (T // tm) * nf), name=name, grid=(T // tm, nf),
        in_specs=[pl.BlockSpec((tm, D), lambda i, j: (i, 0)), pl.BlockSpec((D, tf), lambda i, j: (0, j)),
                  pl.BlockSpec((D, tf), lambda i, j: (0, j + nf))] + [ANY] * len(s_in),
        out_specs=[tile] * 3 + [ANY] * len(s_out), out_shape=[_sds((T, F), BF16)] * 3 + s_out,
        scratch_shapes=s_sems, input_output_aliases=s_alias, compiler_params=_params(2))(n, w_gu, w_gu, *s_in)
    return outs


def _d_gate_up(dout_b, w_d, a, b, tf, name, side=None):
    T, D = dout_b.shape
    F = w_d.shape[0]
    tm = _pick(T, 1024, 16)
    nf = F // tf

    def kern(d_ref, w_ref, a_ref, b_ref, da_ref, db_ref):
        d = 0.5 * lax.dot_general(d_ref[...], w_ref[...], NT, preferred_element_type=F32)
        a, b = a_ref[...].astype(F32), b_ref[...].astype(F32)
        sg = _sigmoid(a)
        da_ref[...] = (d * b * (sg * (1.0 + a * (1.0 - sg)))).astype(BF16)
        db_ref[...] = (d * (a * sg)).astype(BF16)

    tile = pl.BlockSpec((tm, tf), lambda i, j: (i, j))
    s_in, s_out, s_sems, s_alias = _side_parts(side, 4, 2)
    step_of = lambda: pl.program_id(0) * nf + pl.program_id(1)
    outs = pl.pallas_call(
        _carry(kern, side, 4, 2, 0, step_of, (T // tm) * nf), name=name, grid=(T // tm, nf),
        in_specs=[pl.BlockSpec((tm, D), lambda i, j: (i, 0)), pl.BlockSpec((tf, D), lambda i, j: (j, 0)), tile, tile]
        + [ANY] * len(s_in),
        out_specs=[tile] * 2 + [ANY] * len(s_out), out_shape=[_sds((T, F), BF16)] * 2 + s_out,
        scratch_shapes=s_sems, input_output_aliases=s_alias, compiler_params=_params(2))(dout_b, w_d, a, b, *s_in)
    return outs


def _lat_fwd(lat, g_cq, g_ckv, name):
    T, LP = lat.shape
    QL, KVL = g_cq.shape[1], g_ckv.shape[1]
    tm = _pick(T, 512, 8)

    def body(v, gq, gk):
        xq, xk = v[:, :QL], v[:, QL:QL + KVL]
        return xq * _rstd(xq, QL) * gq, xk * _rstd(xk, KVL) * gk

    return _vcall(body, (T // tm,), (lat, g_cq, g_ckv), [_rows(tm, LP), _vec(QL), _vec(KVL)],
                  [_sds((T, QL), BF16), _sds((T, KVL), BF16)], [_rows(tm, QL), _rows(tm, KVL)], name)


def _lat_bwd(dcq, dckv, dkr, lat, g_cq, g_ckv, name):
    T, LP = lat.shape
    QL, KVL = g_cq.shape[1], g_ckv.shape[1]
    tm = _pick(T, 512, 8)

    def body(dq, dk, dr, v, gq, gk):
        xq, xk = v[:, :QL], v[:, QL:QL + KVL]
        rq, rk = _rstd(xq, QL), _rstd(xk, KVL)
        dxq, dgq = _rms_bwd(xq * rq, rq, gq, dq, QL)
        dxk, dgk = _rms_bwd(xk * rk, rk, gk, dk, KVL)
        return jnp.concatenate([dxq, dxk, dr], axis=1), dgq, dgk

    return _vcall(body, (T // tm,), (dcq, dckv, dkr, lat, g_cq, g_ckv),
                  [_rows(tm, QL), _rows(tm, KVL), _rows(tm, LANES), _rows(tm, LP), _vec(QL), _vec(KVL)],
                  [_sds((T, LP), BF16), _sds((1, QL), F32), _sds((1, KVL), F32)],
                  [_rows(tm, LP), _vec(QL), _vec(KVL)], name, n_acc=2)


def _head_spec(tm, w):
    return pl.BlockSpec((tm, w), lambda i, h: (i, h))


def _row2(tm, w, col=0):
    return pl.BlockSpec((tm, w), lambda i, h: (i, col))


def _vec2(w):
    return pl.BlockSpec((1, w), lambda i, h: (0, 0))


def _mla_q_prep(q_raw, g_q, tabs, H, scale, name):
    T = q_raw.shape[0]
    tm = _pick(T, 512, 8)
    half = MLA_ROPE // 2

    def body(x, g, c, sa, sb):
        n = x * _rstd(x, MLA_QK) * g
        return jnp.concatenate([n[:, :LANES], _rope(n[:, LANES:], c, sa, sb, half)], axis=1) * scale

    return _vcall(body, (T // tm, H), (q_raw, g_q) + tabs,
                  [_head_spec(tm, MLA_QK_PAD), _vec2(MLA_QK_PAD)] + [_row2(tm, LANES)] * 3,
                  [_sds((T, H * MLA_QK_PAD), BF16)], [_head_spec(tm, MLA_QK_PAD)], name)[0]


def _mla_q_bwd(dq, q_raw, g_q, tabs, H, scale, name):
    T = q_raw.shape[0]
    tm = _pick(T, 512, 8)
    half = MLA_ROPE // 2

    def body(d, x, g, c, sa, sb):
        r = _rstd(x, MLA_QK)
        d = d * scale
        dn = jnp.concatenate([d[:, :LANES], _rope_t(d[:, LANES:], c, sa, sb, half)], axis=1)
        return _rms_bwd(x * r, r, g, dn, MLA_QK)

    return _vcall(body, (T // tm, H), (dq, q_raw, g_q) + tabs,
                  [_head_spec(tm, MLA_QK_PAD), _head_spec(tm, MLA_QK_PAD), _vec2(MLA_QK_PAD)] + [_row2(tm, LANES)] * 3,
                  [_sds((T, H * MLA_QK_PAD), BF16), _sds((1, MLA_QK_PAD), F32)],
                  [_head_spec(tm, MLA_QK_PAD), _vec2(MLA_QK_PAD)], name, n_acc=1)


def _mla_k_prep(kv, lat, kr_col, g_k, tabs, H, name):
    T = kv.shape[0]
    tm = _pick(T, 512, 8)
    half = MLA_ROPE // 2

    def body(x, kr, g, c, sa, sb):
        kn = x[:, :LANES]
        r = lax.rsqrt((jnp.sum(kn * kn, axis=-1, keepdims=True) + jnp.sum(kr * kr, axis=-1, keepdims=True))
                      * (1.0 / MLA_QK) + EPS)
        k0 = kn * r * g[:, :LANES]
        k1 = _rope(kr * r * g[:, LANES:], c, sa, sb, half)
        return jnp.concatenate([k0, k1], axis=1), x[:, LANES:]

    return _vcall(body, (T // tm, H), (kv, lat, g_k) + tabs,
                  [_head_spec(tm, 2 * LANES), _row2(tm, LANES, kr_col), _vec2(MLA_QK_PAD)] + [_row2(tm, LANES)] * 3,
                  [_sds((T, H * MLA_QK_PAD), BF16), _sds((T, H * MLA_V), BF16)],
                  [_head_spec(tm, MLA_QK_PAD), _head_spec(tm, MLA_V)], name)


def _mla_k_bwd(dk, dv, kv, lat, kr_col, g_k, tabs, H, name):
    T = kv.shape[0]
    tm = _pick(T, 512, 8)
    half = MLA_ROPE // 2

    def body(d, dvv, x, kr, g, c, sa, sb):
        xx = jnp.concatenate([x[:, :LANES], kr], axis=1)
        r = _rstd(xx, MLA_QK)
        dn = jnp.concatenate([d[:, :LANES], _rope_t(d[:, LANES:], c, sa, sb, half)], axis=1)
        dx, dg = _rms_bwd(xx * r, r, g, dn, MLA_QK)
        return jnp.concatenate([dx[:, :LANES], dvv], axis=1), dx[:, LANES:], dg

    return _vcall(body, (T // tm, H), (dk, dv, kv, lat, g_k) + tabs,
                  [_head_spec(tm, MLA_QK_PAD), _head_spec(tm, MLA_V), _head_spec(tm, 2 * LANES),
                   _row2(tm, LANES, kr_col), _vec2(MLA_QK_PAD)] + [_row2(tm, LANES)] * 3,
                  [_sds((T, H * 2 * LANES), BF16), _sds((T, LANES), F32), _sds((1, MLA_QK_PAD), F32)],
                  [_head_spec(tm, 2 * LANES), _row2(tm, LANES), _vec2(MLA_QK_PAD)], name, n_inner_acc=1, n_acc=1)


def _dil_prep(pd, g_q, g_k, tabs, HD, scale, name):
    T = pd.shape[0]
    W = HD * DIL_HEAD
    G = len(DIL_GROUPS)
    tm = _pick(T, 256, 8)
    half = DIL_ROT // 2

    def body(xq, xk, xv, gq, gk, c, sa, sb):
        outs = []
        for x, g, s in ((xq, gq, scale), (xk, gk, 1.0)):
            heads = []
            for h in range(HD):
                xs = x[:, h * DIL_HEAD:(h + 1) * DIL_HEAD].astype(F32)
                n = _rope(xs * _rstd(xs, DIL_HEAD) * g, c, sa, sb, half)
                heads.append(n * s if s != 1.0 else n)
            outs.append(jnp.concatenate(heads, axis=1))
        return outs[0], outs[1], xv

    gspec = pl.BlockSpec((None, 1, DIL_HEAD), lambda i, g: (g, 0, 0))
    return _vcall(body, (T // tm, G), (pd, pd, pd, g_q, g_k) + tabs,
                  [pl.BlockSpec((tm, W), lambda i, g: (i, 3 * g)), pl.BlockSpec((tm, W), lambda i, g: (i, 3 * g + 1)),
                   pl.BlockSpec((tm, W), lambda i, g: (i, 3 * g + 2)), gspec, gspec] + [_row2(tm, LANES)] * 3,
                  [_sds((T, G * W), F32)] * 3, [pl.BlockSpec((tm, W), lambda i, g: (i, g))] * 3, name)


def _dil_prep_bwd(dq, dk, dv, pd, grp, g_q, g_k, tabs, HD, scale, name):
    T = pd.shape[0]
    W = HD * DIL_HEAD
    tm = _pick(T, 256, 8)
    half = DIL_ROT // 2

    def body(dqv, dkv, dvv, xq, xk, gq, gk, c, sa, sb):
        cols, dgs = [], []
        for d, x, g, s in ((dqv, xq, gq, scale), (dkv, xk, gk, 1.0)):
            heads, dg = [], None
            for h in range(HD):
                sl = slice(h * DIL_HEAD, (h + 1) * DIL_HEAD)
                xs = x[:, sl].astype(F32)
                r = _rstd(xs, DIL_HEAD)
                dh = d[:, sl] * s if s != 1.0 else d[:, sl]
                dx, dgh = _rms_bwd(xs * r, r, g, _rope_t(dh, c, sa, sb, half), DIL_HEAD)
                heads.append(dx)
                dg = dgh if dg is None else dg + dgh
            cols.append(jnp.concatenate(heads, axis=1))
            dgs.append(dg)
        return jnp.concatenate(cols + [dvv], axis=1), dgs[0], dgs[1]

    gq, gk = g_q[grp], g_k[grp]
    return _vcall(body, (T // tm,), (dq, dk, dv, pd, pd, gq, gk) + tabs,
                  [_rows(tm, W)] * 3 + [pl.BlockSpec((tm, W), lambda i: (i, 3 * grp)),
                                        pl.BlockSpec((tm, W), lambda i: (i, 3 * grp + 1)),
                                        _vec(DIL_HEAD), _vec(DIL_HEAD)] + [_rows(tm, LANES)] * 3,
                  [_sds((T, 3 * W), BF16), _sds((1, DIL_HEAD), F32), _sds((1, DIL_HEAD), F32)],
                  [_rows(tm, 3 * W), _vec(DIL_HEAD), _vec(DIL_HEAD)], name, n_acc=2)


def _dil_merge(os_, lses, name):
    T, W = os_[0].shape
    tm = _pick(T, 256, 8)

    def body(o0, o1, o2, l0, l1, l2):
        m = jnp.maximum(jnp.maximum(l0, l1), l2)
        w0, w1, w2 = jnp.exp(l0 - m), jnp.exp(l1 - m), jnp.exp(l2 - m)
        z = w0 + w1 + w2
        return (w0 * o0 + w1 * o1 + w2 * o2) / z, m + jnp.log(z)

    return _vcall(body, (T // tm,), tuple(os_) + tuple(lses), [_rows(tm, W)] * 6,
                  [_sds((T, W), BF16), _sds((T, W), F32)], [_rows(tm, W)] * 2, name)


def _gate_merge(pg, bm, bd, name):
    T, D = bm.shape
    tm = _pick(T, 256, 8)

    def body(g, m, d):
        g = g.astype(F32)
        return _sigmoid(g[:, :D]) * m + _sigmoid(g[:, D:]) * d

    return _vcall(body, (T // tm,), (pg, bm, bd), [_rows(tm, 2 * D), _rows(tm, D), _rows(tm, D)],
                  [_sds((T, D), BF16)], [_rows(tm, D)], name)[0]


def _gate_bwd(dmerged, pg, bm, bd, name):
    T, D = bm.shape
    tm = _pick(T, 256, 8)

    def body(dm, g, m, d):
        g = g.astype(F32)
        s0, s1 = _sigmoid(g[:, :D]), _sigmoid(g[:, D:])
        dpg = jnp.concatenate([dm * m * s0 * (1.0 - s0), dm * d * s1 * (1.0 - s1)], axis=1)
        return dm * s0, dm * s1, dpg

    return _vcall(body, (T // tm,), (dmerged, pg, bm, bd), [_rows(tm, D), _rows(tm, 2 * D), _rows(tm, D), _rows(tm, D)],
                  [_sds((T, D), BF16), _sds((T, D), BF16), _sds((T, 2 * D), BF16)],
                  [_rows(tm, D), _rows(tm, D), _rows(tm, 2 * D)], name)


def _ple_loss(x3, zg, pp, target, name):
    T, D = x3.shape
    tm = _pick(T, 256, 8)

    def body(x, z, p_, t):
        s = _sigmoid(z)
        e = x + s * p_ - t
        dy = e * (1.0 / D)
        part = 0.5 * jnp.sum(jnp.sum(e * e, axis=1, keepdims=True), axis=0, keepdims=True) * (1.0 / D)
        return dy, dy * s, dy * p_ * s * (1.0 - s), jnp.broadcast_to(part, (1, LANES))

    return _vcall(body, (T // tm,), (x3, zg, pp, target), [_rows(tm, D)] * 4,
                  [_sds((T, D), F32), _sds((T, D), BF16), _sds((T, D), BF16), _sds((1, LANES), F32)],
                  [_rows(tm, D)] * 3 + [_vec(LANES)], name, n_acc=1)


NT = (((1,), (1,)), ((), ()))
TN = (((0,), (0,)), ((), ()))


def _diag_mask(s):
    row = lax.broadcasted_iota(jnp.int32, s.shape, 0)
    col = lax.broadcasted_iota(jnp.int32, s.shape, 1)
    return jnp.where(col <= row, s, NEG)


def _causal_pairs(nq, key_major):
    if key_major:
        pairs = [(i, j) for j in range(nq) for i in range(j, nq)]
    else:
        pairs = [(i, j) for i in range(nq) for j in range(i + 1)]
    return (jnp.asarray([pr[0] for pr in pairs], jnp.int32), jnp.asarray([pr[1] for pr in pairs], jnp.int32))


def _mla_fwd(q, k, v, H, name, side=None):
    T = q.shape[0]
    tq = _pick(T, 512)
    nq = T // tq
    hb = 2 if H % 2 == 0 else 1
    qi_tab, kj_tab = _causal_pairs(nq, key_major=False)

    def kern(qi_ref, kj_ref, q_ref, k_ref, vt_ref, o_ref, ot_ref, lse_ref, m_sc, l_sc, acc_sc):
        t = pl.program_id(1)
        qi, kj = qi_ref[t], kj_ref[t]

        @pl.when(kj == 0)
        def _():
            m_sc[...] = jnp.full_like(m_sc, NEG)
            l_sc[...] = jnp.zeros_like(l_sc)
            acc_sc[...] = jnp.zeros_like(acc_sc)

        def tile(diagonal):
            for hh in range(hb):
                qs = slice(hh * MLA_QK_PAD, (hh + 1) * MLA_QK_PAD)
                vs = slice(hh * MLA_V, (hh + 1) * MLA_V)
                st = lax.dot_general(k_ref[:, qs], q_ref[:, qs], NT, preferred_element_type=F32)
                if diagonal:
                    key = lax.broadcasted_iota(jnp.int32, st.shape, 0)
                    qry = lax.broadcasted_iota(jnp.int32, st.shape, 1)
                    st = jnp.where(key <= qry, st, NEG)
                m_prev = m_sc[hh]
                m_new = jnp.maximum(m_prev, jnp.max(st, axis=0, keepdims=True))
                alpha = jnp.exp(m_prev - m_new)
                pt = jnp.exp(st - m_new)
                l_new = alpha * l_sc[hh] + jnp.sum(pt, axis=0, keepdims=True)
                acc = alpha * acc_sc[hh] + jnp.dot(vt_ref[vs, :], pt.astype(BF16), preferred_element_type=F32)
                if diagonal:
                    out_t = acc / l_new
                    o_ref[:, vs] = out_t.T.astype(o_ref.dtype)
                    ot_ref[vs, :] = out_t.astype(ot_ref.dtype)
                    lse_ref[hh] = m_new + jnp.log(l_new)
                else:
                    m_sc[hh] = m_new
                    l_sc[hh] = l_new
                    acc_sc[hh] = acc

        @pl.when(kj < qi)
        def _():
            tile(False)

        @pl.when(kj == qi)
        def _():
            tile(True)

    qspec = lambda w: pl.BlockSpec((tq, hb * w), lambda h, t, qi_ref, kj_ref: (qi_ref[t], h))
    kspec = lambda w: pl.BlockSpec((tq, hb * w), lambda h, t, qi_ref, kj_ref: (kj_ref[t], h))
    vt_spec = pl.BlockSpec((hb * MLA_V, tq), lambda h, t, qi_ref, kj_ref: (h, kj_ref[t]))
    ot_spec = pl.BlockSpec((hb * MLA_V, tq), lambda h, t, qi_ref, kj_ref: (h, qi_ref[t]))
    lse_spec = pl.BlockSpec((hb, 1, tq), lambda h, t, qi_ref, kj_ref: (h, 0, qi_ref[t]))
    n_pairs = qi_tab.shape[0]
    s_in, s_out, s_sems, s_alias = _side_parts(side, 5, 3)
    grid_spec = pltpu.PrefetchScalarGridSpec(
        num_scalar_prefetch=2, grid=(H // hb, n_pairs),
        in_specs=[qspec(MLA_QK_PAD), kspec(MLA_QK_PAD), vt_spec] + [ANY] * len(s_in),
        out_specs=[qspec(MLA_V), ot_spec, lse_spec] + [ANY] * len(s_out),
        scratch_shapes=[pltpu.VMEM((hb, 1, tq), F32), pltpu.VMEM((hb, 1, tq), F32),
                        pltpu.VMEM((hb, MLA_V, tq), F32)] + s_sems)
    step_of = lambda: pl.program_id(0) * n_pairs + pl.program_id(1)
    return pl.pallas_call(
        _carry(kern, side, 5, 3, 3, step_of, (H // hb) * n_pairs), name=name, grid_spec=grid_spec,
        out_shape=[_sds((T, H * MLA_V), BF16), _sds((H * MLA_V, T), BF16), _sds((H, 1, T), F32)] + s_out,
        input_output_aliases=s_alias, compiler_params=_params(2))(qi_tab, kj_tab, q, k, v.T, *s_in)


def _mla_bwd(q, k, v, do, do_t, o_t, lse, H, name, side=None):
    T = q.shape[0]
    tq = _pick(T, 512)
    nq = T // tq
    qi_tab, kj_tab = _causal_pairs(nq, key_major=True)

    def kern(qi_ref, kj_ref, q_ref, k_ref, v_ref, do_ref, dot_ref, ot_ref, lse_ref, dq_ref, dk_ref, dv_ref, dk_sc, dv_sc):
        t = pl.program_id(1)
        qi, kj = qi_ref[t], kj_ref[t]
        rows = pl.ds(pl.multiple_of(qi * tq, tq), tq)

        def tile(diagonal):
            st = lax.dot_general(k_ref[...], q_ref[...], NT, preferred_element_type=F32)
            if diagonal:
                key = lax.broadcasted_iota(jnp.int32, st.shape, 0)
                qry = lax.broadcasted_iota(jnp.int32, st.shape, 1)
                st = jnp.where(key <= qry, st, NEG)
            pt = jnp.exp(st - lse_ref[...])
            dl = jnp.sum(dot_ref[...].astype(F32) * ot_ref[...].astype(F32), axis=0, keepdims=True)
            dpt = jnp.dot(v_ref[...], dot_ref[...], preferred_element_type=F32)
            dst = (pt * (dpt - dl)).astype(BF16)
            dv = jnp.dot(pt.astype(BF16), do_ref[...], preferred_element_type=F32)
            dk = jnp.dot(dst, q_ref[...], preferred_element_type=F32)
            dq = lax.dot_general(dst, k_ref[...], TN, preferred_element_type=F32)
            if diagonal:
                dv_sc[...] = dv
                dk_sc[...] = dk
            else:
                dv_sc[...] += dv
                dk_sc[...] += dk

            @pl.when(kj == 0)
            def _():
                dq_ref[rows, :] = dq

            @pl.when(kj > 0)
            def _():
                dq_ref[rows, :] += dq

        @pl.when(qi == kj)
        def _():
            tile(True)

        @pl.when(qi > kj)
        def _():
            tile(False)

        @pl.when(qi == nq - 1)
        def _():
            dk_ref[...] = dk_sc[...]
            dv_ref[...] = dv_sc[...]

    qspec = lambda w: pl.BlockSpec((tq, w), lambda h, t, qi_ref, kj_ref: (qi_ref[t], h))
    kspec = lambda w: pl.BlockSpec((tq, w), lambda h, t, qi_ref, kj_ref: (kj_ref[t], h))
    t_spec = pl.BlockSpec((MLA_V, tq), lambda h, t, qi_ref, kj_ref: (h, qi_ref[t]))
    lse_spec = pl.BlockSpec((None, 1, tq), lambda h, t, qi_ref, kj_ref: (h, 0, qi_ref[t]))
    n_pairs = qi_tab.shape[0]
    s_in, s_out, s_sems, s_alias = _side_parts(side, 9, 3)
    grid_spec = pltpu.PrefetchScalarGridSpec(
        num_scalar_prefetch=2, grid=(H, n_pairs),
        in_specs=[qspec(MLA_QK_PAD), kspec(MLA_QK_PAD), kspec(MLA_V), qspec(MLA_V), t_spec, t_spec, lse_spec]
        + [ANY] * len(s_in),
        out_specs=[pl.BlockSpec((T, MLA_QK_PAD), lambda h, t, qi_ref, kj_ref: (0, h)), kspec(MLA_QK_PAD), kspec(MLA_V)]
        + [ANY] * len(s_out),
        scratch_shapes=[pltpu.VMEM((tq, MLA_QK_PAD), F32), pltpu.VMEM((tq, MLA_V), F32)] + s_sems)
    step_of = lambda: pl.program_id(0) * n_pairs + pl.program_id(1)
    return pl.pallas_call(
        _carry(kern, side, 9, 3, 2, step_of, H * n_pairs), name=name, grid_spec=grid_spec,
        out_shape=[_sds((T, H * MLA_QK_PAD), F32), _sds((T, H * MLA_QK_PAD), F32), _sds((T, H * MLA_V), F32)] + s_out,
        input_output_aliases=s_alias, compiler_params=_params(2))(qi_tab, kj_tab, q, k, v, do, do_t, o_t, lse, *s_in)


class _DilGeometry:
    def __init__(self, T, dil, HD, grp):
        self.dil, self.sub = dil, max(1, 8 // dil)
        self.tb = self.sub * DIL_BLOCK * dil
        assert T % self.tb == 0, (T, dil)
        self.nblk = T // self.tb
        last = self.nblk - 1
        self.cur_g = pl.BlockSpec((self.tb, DIL_HEAD), lambda i, h: (i, grp * HD + h))
        self.prev_g = pl.BlockSpec((self.tb, DIL_HEAD), lambda i, h: (jnp.maximum(i - 1, 0), grp * HD + h))
        self.next_g = pl.BlockSpec((self.tb, DIL_HEAD), lambda i, h: (jnp.minimum(i + 1, last), grp * HD + h))
        self.cur = pl.BlockSpec((self.tb, DIL_HEAD), lambda i, h: (i, h))
        self.next = pl.BlockSpec((self.tb, DIL_HEAD), lambda i, h: (jnp.minimum(i + 1, last), h))

    def rows(self, b, r):
        if self.dil == 1:
            return pl.ds(b * DIL_BLOCK, DIL_BLOCK)
        return pl.ds(b * DIL_BLOCK * self.dil + r, DIL_BLOCK, stride=self.dil)

    def tiles(self):
        return [(b, r) for b in range(self.sub) for r in range(self.dil)]

    def rows2(self, b, r):
        if self.dil == 1:
            return pl.ds(b * DIL_BLOCK, 2 * DIL_BLOCK)
        return pl.ds(b * DIL_BLOCK * self.dil + r, 2 * DIL_BLOCK, stride=self.dil)

    def keys(self, cur_ref, prev_ref, b, r):
        if b > 0:
            return cur_ref[self.rows2(b - 1, r), :].astype(BF16)
        return jnp.concatenate([prev_ref[self.rows(self.sub - 1, r), :], cur_ref[self.rows(0, r), :]],
                               axis=0).astype(BF16)

    def masks(self, i):
        row = lax.broadcasted_iota(jnp.int32, (DIL_BLOCK, 2 * DIL_BLOCK), 0)
        col = lax.broadcasted_iota(jnp.int32, (DIL_BLOCK, 2 * DIL_BLOCK), 1)
        band = (col >= row) & (col <= row + DIL_BLOCK)
        first = band & (col >= jnp.where(i > 0, 0, DIL_BLOCK))
        no_next = jnp.where(i + 1 < self.nblk, 0, 2 * DIL_BLOCK)
        ok_next = col[:, :DIL_BLOCK] >= row[:, :DIL_BLOCK] + no_next
        return band, first, ok_next


def _twice(x):
    return jnp.concatenate([x, x], axis=1)


def _dil_fwd(qd, kd, vd, grp, dil, HD, name):
    T = qd.shape[0]
    W = HD * DIL_HEAD
    geo = _DilGeometry(T, dil, HD, grp)

    def kern(q_ref, kc_ref, kp_ref, vc_ref, vp_ref, o_ref, lse_ref):
        band, first, _ = geo.masks(pl.program_id(0))
        for b, r in geo.tiles():
            R = geo.rows(b, r)
            q = q_ref[R, :].astype(BF16)
            kk, vv = geo.keys(kc_ref, kp_ref, b, r), geo.keys(vc_ref, vp_ref, b, r)
            s = jnp.where(band if b > 0 else first, lax.dot_general(q, kk, NT, preferred_element_type=F32), NEG)
            m = jnp.max(s, axis=1, keepdims=True)
            e = jnp.exp(s - m)
            l = jnp.sum(e, axis=1, keepdims=True)
            o_ref[R, :] = jnp.dot(e.astype(BF16), vv, preferred_element_type=F32) / l
            lse_ref[R, :] = jnp.broadcast_to(m + jnp.log(l), (DIL_BLOCK, DIL_HEAD))

    return pl.pallas_call(
        kern, name=name, grid=(geo.nblk, HD), in_specs=[geo.cur_g, geo.cur_g, geo.prev_g, geo.cur_g, geo.prev_g],
        out_specs=[geo.cur, geo.cur], out_shape=[_sds((T, W), F32)] * 2,
        compiler_params=_params(2))(qd, kd, kd, vd, vd)


def _dil_delta(do, o, HD, name):
    T, W = do.shape
    tm = _pick(T, 512, 8)

    def body(d, ov):
        prod = d * ov.astype(F32)
        return jnp.concatenate(
            [jnp.broadcast_to(jnp.sum(prod[:, h * DIL_HEAD:(h + 1) * DIL_HEAD], axis=1, keepdims=True), (tm, DIL_HEAD))
             for h in range(HD)], axis=1)

    return _vcall(body, (T // tm,), (do, o), [_rows(tm, W)] * 2, [_sds((T, W), F32)], [_rows(tm, W)], name)[0]


def _dil_bwd(qd, kd, vd, do, delta, lse, grp, dil, HD, name):
    T = qd.shape[0]
    W = HD * DIL_HEAD
    geo = _DilGeometry(T, dil, HD, grp)

    def kern(q_ref, k_ref, v_ref, do_ref, dl_ref, ls_ref, kp_ref, vp_ref, qn_ref, don_ref, dln_ref, lsn_ref,
             dq_ref, dk_ref, dv_ref):
        band, first, ok_next = geo.masks(pl.program_id(0))

        def tile(q, do_, dl, ls, k, v, ok):
            s = jnp.where(ok, lax.dot_general(q, k, NT, preferred_element_type=F32), NEG)
            p = jnp.exp(s - ls)
            ds = p * (lax.dot_general(do_, v, NT, preferred_element_type=F32) - dl)
            return ds.astype(BF16), p.astype(BF16)

        dk_ref[...] = jnp.zeros_like(dk_ref)
        dv_ref[...] = jnp.zeros_like(dv_ref)
        for b, r in geo.tiles():
            R = geo.rows(b, r)
            q, do_ = q_ref[R, :].astype(BF16), do_ref[R, :].astype(BF16)
            kk, vv = geo.keys(k_ref, kp_ref, b, r), geo.keys(v_ref, vp_ref, b, r)
            ds, p_ = tile(q, do_, _twice(dl_ref[R, :]), _twice(ls_ref[R, :]), kk, vv, band if b > 0 else first)
            dq_ref[R, :] = jnp.dot(ds, kk, preferred_element_type=F32)
            dkk = lax.dot_general(ds, q, TN, preferred_element_type=F32)
            dvv = lax.dot_general(p_, do_, TN, preferred_element_type=F32)
            if b > 0:
                R2 = geo.rows2(b - 1, r)
                dk_ref[R2, :] += dkk
                dv_ref[R2, :] += dvv
            else:
                dk_ref[R, :] += dkk[DIL_BLOCK:]
                dv_ref[R, :] += dvv[DIL_BLOCK:]
        for r in range(dil):
            R, Rn = geo.rows(geo.sub - 1, r), geo.rows(0, r)
            qn, don = qn_ref[Rn, :].astype(BF16), don_ref[Rn, :].astype(BF16)
            ds, p_ = tile(qn, don, dln_ref[Rn, :], lsn_ref[Rn, :], k_ref[R, :].astype(BF16), v_ref[R, :].astype(BF16),
                          ok_next)
            dk_ref[R, :] += lax.dot_general(ds, qn, TN, preferred_element_type=F32)
            dv_ref[R, :] += lax.dot_general(p_, don, TN, preferred_element_type=F32)

    return pl.pallas_call(
        kern, name=name, grid=(geo.nblk, HD),
        in_specs=[geo.cur_g, geo.cur_g, geo.cur_g, geo.cur, geo.cur, geo.cur, geo.prev_g, geo.prev_g,
                  geo.next_g, geo.next, geo.next, geo.next],
        out_specs=[geo.cur] * 3, out_shape=[_sds((T, W), F32)] * 3,
        compiler_params=_params(2))(qd, kd, vd, do, delta, lse, kd, vd, qd, do, delta, lse)


def _place():
    return lax.axis_index("x"), lax.axis_index("y"), lax.axis_index("c")


def _other_chips(x, y):
    return [(1 - x, y), (x, 1 - y), (1 - x, 1 - y)]


def _kind(name, shard_shape):
    if name in ROW_SHARDED:
        return "row"
    return "col" if shard_shape[1] % LANES == 0 else "stack"


def _remote(src, dst, send_sem, recv_sem, to):
    return pltpu.make_async_remote_copy(src_ref=src, dst_ref=dst, send_sem=send_sem, recv_sem=recv_sem,
                                        device_id=to, device_id_type=MESH_ID)


def _row_tile(rows, cols, itemsize, align):
    return _pick(rows, max(align, (2 * 1024 * 1024) // (cols * itemsize)), align)


def _dma_sems(n):
    return [pltpu.SemaphoreType.DMA((n,)), pltpu.SemaphoreType.DMA((n,))]


def _gather_plan(shard_shapes):
    info, buf_shapes = {}, {}
    for out_name, names in GATHER_PLAN:
        r, c = shard_shapes[names[0]]
        kind = _kind(names[0], (r, c))
        assert kind == "col" or len(names) == 1, out_name
        buf_shapes[out_name] = (r, 4 * c * len(names)) if kind == "col" else (4, r, c)
        for i, n in enumerate(names):
            assert tuple(shard_shapes[n]) == (r, c), n
            info[n] = (out_name, kind, i * 4 * c, r, c)
    return info, buf_shapes


def _place_own(shard, buf_shape, kind, base, me, name, prev=None):
    r, c = shard.shape
    tr = _row_tile(r, c, 2, 16)

    def kern(me_ref, x_ref, *rest):
        rest[-1][...] = x_ref[...]

    if kind == "col":
        out_spec = pl.BlockSpec((tr, c), lambda i, me_ref: (i, base // c + me_ref[0]))
    else:
        out_spec = pl.BlockSpec((None, tr, c), lambda i, me_ref: (me_ref[0], i, 0))
    in_specs = [pl.BlockSpec((tr, c), lambda i, me_ref: (i, 0))] + ([ANY] if prev is not None else [])
    grid_spec = pltpu.PrefetchScalarGridSpec(num_scalar_prefetch=1, grid=(r // tr,), in_specs=in_specs,
                                             out_specs=out_spec)
    args = (me, shard) + ((prev,) if prev is not None else ())
    return pl.pallas_call(kern, name=name, grid_spec=grid_spec, out_shape=_sds(buf_shape, shard.dtype),
                          input_output_aliases={2: 0} if prev is not None else {}, compiler_params=_params(1))(*args)


def _buffers_of(names, info):
    out_names = []
    for n in names:
        if info[n][0] not in out_names:
            out_names.append(info[n][0])
    return out_names


def _ag_side(names, shards, bufs, info):
    out_names = _buffers_of(names, info)
    n_w = len(names)

    def region(outs, w, chip, h):
        out_name, kind, base, r, cc = info[names[w]]
        o = outs[out_names.index(out_name)]
        rows = pl.ds(h * (r // 2), r // 2)
        if kind == "col":
            return o.at[rows, pl.ds(pl.multiple_of(base + chip * cc, LANES), cc)]
        return o.at[chip, rows, :]

    def hop(first, sending):
        def fn(ins, outs, sems):
            x, y, c = _place()
            me, sibling, cps = 2 * x + y, (x, y, 1 - c), []
            for w in range(n_w):
                r = info[names[w]][3]
                for j, (px, py) in enumerate(_other_chips(x, y)):
                    k = 3 * w + j + (0 if first else 3 * n_w)
                    if first and sending:
                        src, dst, to = ins[w].at[pl.ds(c * (r // 2), r // 2), :], region(outs, w, me, c), (px, py, c)
                    elif first:
                        src = dst = region(outs, w, 2 * px + py, c)
                        to = (px, py, c)
                    else:
                        src = dst = region(outs, w, 2 * px + py, c if sending else 1 - c)
                        to = sibling
                    cps.append(_remote(src, dst, sems[0].at[k], sems[1].at[k], to))
            return cps
        return fn

    return _Side([shards[n] for n in names] + [bufs[o] for o in out_names],
                 [_sds(bufs[o].shape, bufs[o].dtype) for o in out_names], {n_w + i: i for i in range(len(out_names))},
                 _dma_sems(6 * n_w), [(hop(True, True), hop(True, False)), (hop(False, True), hop(False, False))])


def _rs_sibling_side(views):
    n = len(views)

    def fn(sending):
        def copies(ins, outs, sems):
            x, y, c = _place()
            return [_remote(ins[w].at[:, 1 - c] if sending else outs[w], outs[w], sems[0].at[w], sems[1].at[w],
                            (x, y, 1 - c)) for w in range(n)]
        return copies

    return _Side(views, [_sds((v.shape[0],) + v.shape[2:], v.dtype) for v in views], {}, _dma_sems(n),
                 [(fn(True), fn(False))])


def _rs_chips_side(parts, kinds, widths):
    n = len(parts)

    def piece(ins, w, chip):
        if kinds[w] == "col":
            return ins[w].at[0, :, pl.ds(pl.multiple_of(chip * widths[w], LANES), widths[w])]
        return ins[w].at[chip]

    def fn(sending):
        def copies(ins, outs, sems):
            x, y, c = _place()
            cps = []
            for w in range(n):
                for j, (px, py) in enumerate(_other_chips(x, y)):
                    k = 3 * w + j
                    src = piece(ins, w, 2 * px + py) if sending else outs[w].at[j]
                    cps.append(_remote(src, outs[w].at[j], sems[0].at[k], sems[1].at[k], (px, py, c)))
            return cps
        return copies

    return _Side(parts, [_sds((3, p_.shape[1], widths[w]), p_.dtype) for w, p_ in enumerate(parts)], {},
                 _dma_sems(3 * n), [(fn(True), fn(False))])


def _rs_join_side(halves):
    n = len(halves)

    def fn(sending):
        def copies(ins, outs, sems):
            x, y, c = _place()
            return [_remote(ins[w] if sending else outs[w], outs[w], sems[0].at[w], sems[1].at[w], (x, y, 1 - c))
                    for w in range(n)]
        return copies

    return _Side(halves, [_sds(h.shape, h.dtype) for h in halves], {}, _dma_sems(n), [(fn(True), fn(False))])


def _pair_sum(g, got, c_idx, name):
    n, _, rows, C = g.shape
    tr = _row_tile(rows, C, 4, 16)

    def kern(c_ref, a_ref, b_ref, o_ref):
        o_ref[...] = (a_ref[...] + b_ref[...]).astype(o_ref.dtype)

    grid_spec = pltpu.PrefetchScalarGridSpec(
        num_scalar_prefetch=1, grid=(n, rows // tr),
        in_specs=[pl.BlockSpec((None, None, tr, C), lambda j, i, c_ref: (j, c_ref[0], i, 0)),
                  pl.BlockSpec((None, tr, C), lambda j, i, c_ref: (j, i, 0))],
        out_specs=pl.BlockSpec((None, tr, C), lambda j, i, c_ref: (j, i, 0)))
    return pl.pallas_call(kern, name=name, grid_spec=grid_spec, out_shape=_sds((n, rows, C), BF16),
                          compiler_params=_params(2))(c_idx, g, got)


def _sum_pieces(pair, recv, kind, me, name):
    _, rows, c = recv.shape
    tr = _row_tile(rows, c, 8, 16)

    def kern(me_ref, own_ref, r_ref, o_ref):
        acc = own_ref[...].astype(F32)
        for j in range(3):
            acc = acc + r_ref[j].astype(F32)
        o_ref[...] = acc

    if kind == "col":
        own_spec = pl.BlockSpec((None, tr, c), lambda i, me_ref: (0, i, me_ref[0]))
    else:
        own_spec = pl.BlockSpec((None, tr, c), lambda i, me_ref: (me_ref[0], i, 0))
    grid_spec = pltpu.PrefetchScalarGridSpec(
        num_scalar_prefetch=1, grid=(rows // tr,),
        in_specs=[own_spec, pl.BlockSpec((3, tr, c), lambda i, me_ref: (0, i, 0))],
        out_specs=pl.BlockSpec((tr, c), lambda i, me_ref: (i, 0)))
    return pl.pallas_call(kern, name=name, grid_spec=grid_spec, out_shape=_sds((rows, c), F32),
                          compiler_params=_params(1))(me, pair, recv)


def _all_reduce_small(vec):
    N = vec.shape[1]
    n_dev = 8

    def body(v_ref, out_ref, slots, send_sems, recv_sems):
        x, y, c = _place()
        me = 4 * x + 2 * y + c
        slots[me] = v_ref[...]
        sent = []
        for k in range(1, n_dev):
            px, py, pc = x ^ (k >> 2), y ^ ((k >> 1) & 1), c ^ (k & 1)
            cp = pltpu.make_async_remote_copy(src_ref=v_ref, dst_ref=slots.at[me], send_sem=send_sems.at[k - 1],
                                              recv_sem=recv_sems.at[k - 1], device_id=(px, py, pc),
                                              device_id_type=MESH_ID)
            cp.start()
            sent.append(cp)
        for k in range(1, n_dev):
            px, py, pc = x ^ (k >> 2), y ^ ((k >> 1) & 1), c ^ (k & 1)
            slot = slots.at[4 * px + 2 * py + pc]
            pltpu.make_async_remote_copy(src_ref=slot, dst_ref=slot, send_sem=send_sems.at[k - 1],
                                         recv_sem=recv_sems.at[k - 1], device_id=(px, py, pc),
                                         device_id_type=MESH_ID).wait_recv()
        for cp in sent:
            cp.wait_send()
        acc = slots[0]
        for j in range(1, n_dev):
            acc = acc + slots[j]
        out_ref[...] = acc

    vm = pl.BlockSpec(memory_space=pltpu.VMEM)
    return pl.pallas_call(
        body, name="ar_gains", out_shape=_sds((1, N), F32), in_specs=[vm], out_specs=vm,
        scratch_shapes=[pltpu.VMEM((n_dev, 1, N), F32), pltpu.SemaphoreType.DMA((n_dev - 1,)),
                        pltpu.SemaphoreType.DMA((n_dev - 1,))])(vec)


def _adamw_math(wv, gv, mv, vv):
    m2 = ADAM_B1 * mv + (1.0 - ADAM_B1) * gv
    v2 = ADAM_B2 * vv + (1.0 - ADAM_B2) * (gv * gv)
    m_hat = m2 / (1.0 - ADAM_B1 ** ADAM_STEP)
    v_hat = v2 / (1.0 - ADAM_B2 ** ADAM_STEP)
    return -ADAM_LR * (m_hat / (jnp.sqrt(v_hat) + ADAM_EPS) + ADAM_WD * wv), m2, v2


def _adamw(w, g, m, v, name):
    R, C = w.shape
    tr = _row_tile(R, C, 8, 8)
    return _vcall(_adamw_math, (R // tr,), (w, g, m, v), [_rows(tr, C)] * 4, [_sds((R, C), F32)] * 3,
                  [_rows(tr, C)] * 3, name)


def _adamw_halves(w, own, recv, m, v, c_idx, name):
    R, C = w.shape
    rows = R // 2
    tr = _row_tile(rows, C, 8, 8)
    nb = rows // tr

    def kern(c_ref, w_ref, own_ref, recv_ref, m_ref, v_ref, g_out, d_out, m_out, v_out):
        def update(g_ref):
            g = g_ref[...]
            g_out[...] = g
            d_out[...], m_out[...], v_out[...] = _adamw_math(w_ref[...], g, m_ref[...], v_ref[...])

        @pl.when(pl.program_id(0) == c_ref[0])
        def _():
            update(own_ref)

        @pl.when(pl.program_id(0) != c_ref[0])
        def _():
            update(recv_ref)

    full = pl.BlockSpec((tr, C), lambda h, i, c_ref: (h * nb + i, 0))
    own_spec = pl.BlockSpec((tr, C), lambda h, i, c_ref: (jnp.where(h == c_ref[0], i, 0), 0))
    recv_spec = pl.BlockSpec((tr, C), lambda h, i, c_ref: (jnp.where(h == c_ref[0], 0, i), 0))
    grid_spec = pltpu.PrefetchScalarGridSpec(num_scalar_prefetch=1, grid=(2, nb),
                                             in_specs=[full, own_spec, recv_spec, full, full], out_specs=[full] * 4)
    return pl.pallas_call(kern, name=name, grid_spec=grid_spec, out_shape=[_sds((R, C), F32)] * 4,
                          compiler_params=_params(2))(c_idx, w, own, recv, m, v)


def _pad_to(a, n, axis):
    extra = n - a.shape[axis]
    if extra == 0:
        return a
    pads = [(0, 0)] * a.ndim
    pads[axis] = (0, extra)
    return jnp.pad(a, pads)


def _round_up(n, m):
    return -(-n // m) * m


def _natural(buf, kind):
    if kind == "col":
        return buf
    n, r, c = buf.shape
    return buf.reshape(n * r, c) if kind == "row" else buf.transpose(1, 0, 2).reshape(r, n * c)


def _halves_view(g, kind, shard_shape):
    r, c = shard_shape
    if kind == "col":
        return g.reshape(1, 2, r // 2, 4 * c)
    if kind == "stack":
        g = g.reshape(r, 4, c).transpose(1, 0, 2)
    return g.reshape(4, 2, r // 2, c)


def _pack_small(vals):
    return jnp.concatenate([_pad_to(vals[n].reshape(1, -1), _round_up(vals[n].size, LANES), 1) for n in SMALL], axis=1)


def _unpack_small(vec, shapes):
    out, off = {}, 0
    for n in SMALL:
        size = int(np.prod(shapes[n]))
        out[n] = vec[:, off:off + size].reshape(shapes[n])
        off += _round_up(size, LANES)
    return out


def _mm_s(sched, a, b, mode, out_dtype, name, **kw):
    side = sched.side(name)
    if side is None:
        return _mm(a, b, mode, out_dtype, name, **kw)
    out, side_outs = _mm(a, b, mode, out_dtype, name, side=side, **kw)
    sched.done(name, side_outs)
    return out


def _call_s(sched, name, n_out, fn):
    side = sched.side(name)
    outs = fn(side)
    if side is not None:
        sched.done(name, list(outs[n_out:]))
    return outs[:n_out]


def _ffn_fwd(sched, x, g, tf, tag):
    w = tag[-1]
    n = _rms_fwd(x, g, f"{tag}_norm")
    a, b, act = _call_s(sched, f"{tag}_gate_up", 3,
                        lambda side: _gate_up(n, sched.weight(f"w{w}_gu"), tf, f"{tag}_gate_up", side=side))
    out = _mm_s(sched, act, sched.weight(f"w{w}_down"), "nn", F32, f"{tag}_down", res=x, alpha=0.5)
    return out, (n, a, b, act)


def _ffn_bwd(sched, dout, x, g, saved, tf, tag):
    w = tag[-1]
    w_gu, w_d = sched.weight(f"w{w}_gu"), sched.weight(f"w{w}_down")
    n, a, b, act = saved
    F = act.shape[1]
    dout_b = dout.astype(BF16)
    sched.grad(f"w{w}_down", _mm_s(sched, act, dout_b, "tn", F32, f"{tag}_d_wdown", alpha=0.5))
    da, db = _call_s(sched, f"{tag}_d_act", 2,
                     lambda side: _d_gate_up(dout_b, w_d, a, b, tf, f"{tag}_d_act", side=side))
    sched.grad(f"w{w}_gate", _mm_s(sched, n, da, "tn", F32, f"{tag}_d_wgate"))
    sched.grad(f"w{w}_up", _mm_s(sched, n, db, "tn", F32, f"{tag}_d_wup"))
    dn = _mm_s(sched, da, w_gu, "nt", F32, f"{tag}_d_norm", a2=db, b2=w_gu, b2_k_offset=F)
    return _rms_bwd_call(x, g, dn, dout, f"{tag}_d_x")


def _local_step(sched, x, p, pos_b, target, Gn, dims):
    T, D = x.shape
    H, HD, QL, KVL, LP, tf = dims["H"], dims["HD"], dims["QL"], dims["KVL"], dims["LP"], dims["tf"]
    Wd = HD * DIL_HEAD
    scale_mla, scale_dil = MLA_QK ** -0.5, DIL_HEAD ** -0.5
    kr_col = (QL + KVL) // LANES
    tab_mla = tuple(_rope_tables(pos_b, MLA_ROPE, "rope_tab_mla"))
    tab_dil = tuple(_rope_tables(pos_b, DIL_ROT, "rope_tab_dil"))

    W = sched.weight
    mm = functools.partial(_mm_s, sched)

    x1, ffn1 = _ffn_fwd(sched, x, Gn["g_ffn1"], tf, "ffn1")
    h = _rms_fwd(x1, Gn["g_mix"], "mix_norm")
    lat = mm(h, W("w_lat"), "nn", F32, "proj_lat")
    pd = mm(h, W("w_dil"), "nn", BF16, "proj_dil")
    pg = mm(h, W("w_gin"), "nn", BF16, "proj_gate")

    cq, ckv = _lat_fwd(lat, Gn["g_cq"], Gn["g_ckv"], "lat_norm")
    q_raw = mm(cq, W("w_uq"), "nn", F32, "mla_q_up")
    kv = mm(ckv, W("w_ukv"), "nn", F32, "mla_kv_up")
    q = _mla_q_prep(q_raw, Gn["g_q_mla"], tab_mla, H, scale_mla, "mla_q_prep")
    k, v = _mla_k_prep(kv, lat, kr_col, Gn["g_k_mla"], tab_mla, H, "mla_k_prep")
    o_mla, o_mla_t, lse_mla = _call_s(sched, "mla_attn", 3, lambda side: _mla_fwd(q, k, v, H, "mla_attn", side=side))

    qd, kd, vd = _dil_prep(pd, Gn["g_q_dil"], Gn["g_k_dil"], tab_dil, HD, scale_dil, "dil_prep")
    og, lg = [], []
    for grp, (win, dil) in enumerate(DIL_GROUPS):
        o_, l_ = _dil_fwd(qd, kd, vd, grp, dil, HD, f"dil_attn{grp}")
        og.append(o_)
        lg.append(l_)
    o_dil, lse_dil = _dil_merge(og, lg, "dil_merge")

    bm = mm(o_mla, W("w_br_mla"), "nn", F32, "branch_mla")
    bd = mm(o_dil, W("w_br_dil"), "nn", F32, "branch_dil")
    merged = _gate_merge(pg, bm, bd, "gate_merge")
    x2 = mm(merged, W("w_o"), "nn", F32, "out_proj", res=x1)

    x3, ffn2 = _ffn_fwd(sched, x2, Gn["g_ffn2"], tf, "ffn2")
    n4 = _rms_fwd(x3, Gn["g_ple"], "ple_norm")
    zg = mm(n4, W("w_ple_gate"), "nn", F32, "ple_gate")
    p_b = p.astype(BF16)
    pp = mm(p_b, W("w_ple_proj"), "nn", F32, "ple_proj")
    dy, dpp, dzg, loss = _ple_loss(x3, zg, pp, target, "ple_loss")

    gg = {}
    sched.grad("w_ple_proj", mm(p_b, dpp, "tn", F32, "d_w_ple_proj"))
    sched.grad("w_ple_gate", mm(n4, dzg, "tn", F32, "d_w_ple_gate"))
    dn4 = mm(dzg, W("w_ple_gate"), "nt", F32, "d_ple_norm")
    dx3, gg["g_ple"] = _rms_bwd_call(x3, Gn["g_ple"], dn4, dy, "d_x3")

    dx2, gg["g_ffn2"] = _ffn_bwd(sched, dx3, x2, Gn["g_ffn2"], ffn2, tf, "ffn2")

    dx2_b = dx2.astype(BF16)
    sched.grad("w_o", mm(merged, dx2_b, "tn", F32, "d_w_o"))
    dmerged = mm(dx2_b, W("w_o"), "nt", F32, "d_merged")
    dbm, dbd, dpg = _gate_bwd(dmerged, pg, bm, bd, "d_gate")
    sched.grad("w_br_mla", mm(o_mla, dbm, "tn", F32, "d_w_br_mla"))
    sched.grad("w_br_dil", mm(o_dil, dbd, "tn", F32, "d_w_br_dil"))
    do_mla = mm(dbm, W("w_br_mla"), "nt", BF16, "d_o_mla")
    do_dil = mm(dbd, W("w_br_dil"), "nt", F32, "d_o_dil")
    delta_dil = _dil_delta(do_dil, o_dil, HD, "dil_delta")

    dh = mm(dpg, W("w_gin"), "nt", F32, "d_h_gate")
    sched.grad("w_gin", mm(h, dpg, "tn", F32, "d_w_gin"))
    gq_d, gk_d = Gn["g_q_dil"], Gn["g_k_dil"]
    dgq_d, dgk_d = [], []
    for grp, (win, dil) in enumerate(DIL_GROUPS):
        dq_, dk_, dv_ = _dil_bwd(qd, kd, vd, do_dil, delta_dil, lse_dil, grp, dil, HD, f"dil_bwd{grp}")
        dpd_g, dgq_, dgk_ = _dil_prep_bwd(dq_, dk_, dv_, pd, grp, gq_d, gk_d, tab_dil, HD, scale_dil, f"d_dil_prep{grp}")
        dgq_d.append(dgq_)
        dgk_d.append(dgk_)
        w_g = W("w_dil")[:, grp * 3 * Wd:(grp + 1) * 3 * Wd]
        dh = mm(dpd_g, w_g, "nt", F32, f"d_h_dil{grp}", res=dh)
        sched.grad(f"w_dil{grp}", mm(h, dpd_g, "tn", F32, f"d_w_dil{grp}"))
    gg["g_q_dil"] = jnp.concatenate(dgq_d, axis=0)
    gg["g_k_dil"] = jnp.concatenate(dgk_d, axis=0)

    dq, dk, dv = _call_s(sched, "mla_bwd", 3,
                         lambda side: _mla_bwd(q, k, v, do_mla, do_mla.T, o_mla_t, lse_mla, H, "mla_bwd", side=side))
    dq_raw, gg["g_q_mla"] = _mla_q_bwd(dq, q_raw, Gn["g_q_mla"], tab_mla, H, scale_mla, "d_mla_q_prep")
    dkv, dkr, gg["g_k_mla"] = _mla_k_bwd(dk, dv, kv, lat, kr_col, Gn["g_k_mla"], tab_mla, H, "d_mla_k_prep")
    sched.grad("w_uq", mm(cq, dq_raw, "tn", F32, "d_w_uq"))
    sched.grad("w_ukv", mm(ckv, dkv, "tn", F32, "d_w_ukv"))
    dcq = mm(dq_raw, W("w_uq"), "nt", F32, "d_cq")
    dckv = mm(dkv, W("w_ukv"), "nt", F32, "d_ckv")
    dlat, gg["g_cq"], gg["g_ckv"] = _lat_bwd(dcq, dckv, dkr, lat, Gn["g_cq"], Gn["g_ckv"], "d_lat_norm")
    dh = mm(dlat, W("w_lat"), "nt", F32, "d_h_lat", res=dh)
    sched.grad("w_lat", mm(h, dlat, "tn", F32, "d_w_lat"))

    dx1, gg["g_mix"] = _rms_bwd_call(x1, Gn["g_mix"], dh, dx2, "d_x1")
    dx, gg["g_ffn1"] = _ffn_bwd(sched, dx1, x, Gn["g_ffn1"], ffn1, tf, "ffn1")
    return loss, dx, gg


def _layout_weight(name, full, dims):
    H, QL, KVL, LP, Wd = dims["H"], dims["QL"], dims["KVL"], dims["LP"], dims["HD"] * DIL_HEAD
    off_dil = QL + KVL + MLA_ROPE
    off_gate = off_dil + 3 * len(DIL_GROUPS) * Wd
    if name == "w_lat":
        return _pad_to(full("w_in")[:, :off_dil], LP, 1)
    if name == "w_dil":
        return full("w_in")[:, off_dil:off_gate]
    if name == "w_gin":
        return full("w_in")[:, off_gate:]
    if name == "w_uq":
        return _pad_to(full("w_uq").reshape(QL, H, MLA_QK), MLA_QK_PAD, 2).reshape(QL, H * MLA_QK_PAD)
    return full(name)


def _natural_grad(name, gw, dims):
    H, QL, KVL = dims["H"], dims["QL"], dims["KVL"]
    if name == "w_in":
        return jnp.concatenate([gw["w_lat"][:, :QL + KVL + MLA_ROPE]] + [gw[f"w_dil{g}"] for g in range(len(DIL_GROUPS))]
                               + [gw["w_gin"]], axis=1)
    if name == "w_uq":
        return gw["w_uq"].reshape(QL, H, MLA_QK_PAD)[:, :, :MLA_QK].reshape(QL, H * MLA_QK)
    return gw[name]


WEIGHT_SOURCES = {"w1_gu": ("w1_gate", "w1_up"), "w2_gu": ("w2_gate", "w2_up"), "w_lat": ("w_in",), "w_dil": ("w_in",),
                  "w_gin": ("w_in",)}
AG_FIRST = ("w1_gate", "w1_up")
AG_RIDES = {"ffn1_gate_up": ("w1_down", "w_in"), "ffn1_down": ("w_uq", "w_ukv"),
            "mla_attn": ("w_br_mla", "w_br_dil", "w_o", "w_ple_gate", "w_ple_proj", "w2_gate", "w2_up", "w2_down")}
RS_FFN2 = ("w_ple_proj", "w_ple_gate", "w2_down", "w2_gate", "w2_up")
RS_MIXER = ("w_o", "w_br_mla", "w_br_dil", "w_in", "w_uq", "w_ukv")
RS_RIDES = {
    "d_merged": (("sibling", RS_FFN2),),
    "mla_bwd": (("chips", RS_FFN2),),
    "d_h_lat": (("join", RS_FFN2),),
    "ffn1_d_wdown": (("sibling", RS_MIXER),),
    "ffn1_d_act": (("chips", ("w_o", "w_br_mla", "w_br_dil", "w_uq", "w_ukv")), ("sibling", ("w1_down",))),
    "ffn1_d_wgate": (("chips", ("w1_down",)),),
    "ffn1_d_wup": (("sibling", ("w1_gate",)),),
    "ffn1_d_norm": (("chips", ("w_in", "w1_gate")), ("sibling", ("w1_up",))),
}
RS_LAST = ((("chips", ("w1_up",)),), (("join", RS_MIXER + ("w1_down", "w1_gate", "w1_up")),))


class _MeshSchedule:
    def __init__(self, w, m, v, dims):
        self.w, self.m, self.v, self.dims = w, m, v, dims
        self.shapes = {n: tuple(w[n].shape[1:]) for n in BIG}
        self.kinds = {n: _kind(n, self.shapes[n]) for n in BIG}
        self.info, buf_shapes = _gather_plan(self.shapes)
        x, y, c = _place()
        self.me = (2 * x + y).astype(jnp.int32).reshape(1)
        self.c_idx = c.astype(jnp.int32).reshape(1)
        self.shards = {n: w[n][0].astype(BF16) for n in BIG}
        self.bufs, self.gathered, self.layout = {}, set(), {}
        for n in BIG:
            out_name, kind, base, _, _ = self.info[n]
            self.bufs[out_name] = _place_own(self.shards[n], buf_shapes[out_name], kind, base, self.me, f"ag_own_{n}",
                                             prev=self.bufs.get(out_name))
        self.gw, self.views, self.pairs, self.halves, self.recv = {}, {}, {}, {}, {}
        self._ag_done(AG_FIRST, _run_side(self._ag(AG_FIRST), "ag_first"))

    def _ag(self, names):
        return _ag_side(names, self.shards, self.bufs, self.info)

    def _ag_done(self, names, outs):
        for out_name, buf in zip(_buffers_of(names, self.info), outs):
            self.bufs[out_name] = buf
        self.gathered.update(names)

    def weight(self, name):
        if name not in self.layout:
            assert all(s in self.gathered for s in WEIGHT_SOURCES.get(name, (name,))), name
            if name in self.bufs and name not in self.info:
                self.layout[name] = self.bufs[name]
            else:
                full = lambda n: _natural(self.bufs[self.info[n][0]], self.info[n][1])
                self.layout[name] = _layout_weight(name, full, self.dims)
        return self.layout[name]

    def grad(self, name, g):
        self.gw[name] = g

    def _rs_side(self, stages):
        sides = []
        for stage, names in stages:
            if stage == "sibling":
                for n in names:
                    self.views[n] = _halves_view(_natural_grad(n, self.gw, self.dims), self.kinds[n], self.shapes[n])
                sides.append(_rs_sibling_side([self.views[n] for n in names]))
            elif stage == "chips":
                sides.append(_rs_chips_side([self.pairs[n] for n in names], [self.kinds[n] for n in names],
                                            [self.shapes[n][1] for n in names]))
            else:
                sides.append(_rs_join_side([self.halves[n] for n in names]))
        return sides[0] if len(sides) == 1 else _merge_sides(sides)

    def _rs_done(self, stages, outs):
        for stage, names in stages:
            got, outs = outs[:len(names)], outs[len(names):]
            for n, a in zip(names, got):
                if stage == "sibling":
                    self.pairs[n] = _pair_sum(self.views[n], a, self.c_idx, f"rs_pair_{n}")
                elif stage == "chips":
                    self.halves[n] = _sum_pieces(self.pairs[n], a, self.kinds[n], self.me, f"rs_sum_{n}")
                else:
                    self.recv[n] = a

    def side(self, tag):
        if tag in AG_RIDES:
            return self._ag(AG_RIDES[tag])
        if tag in RS_RIDES:
            return self._rs_side(RS_RIDES[tag])
        return None

    def done(self, tag, outs):
        if tag in AG_RIDES:
            self._ag_done(AG_RIDES[tag], outs)
        else:
            self._rs_done(RS_RIDES[tag], outs)

    def finish(self):
        for k, stages in enumerate(RS_LAST):
            self._rs_done(stages, _run_side(self._rs_side(stages), f"rs_last{k}"))
        outs = {"grad": {}, "delta": {}, "m": {}, "v": {}}
        for n in BIG:
            res = _adamw_halves(self.w[n][0], self.halves[n], self.recv[n], self.m[n][0], self.v[n][0], self.c_idx,
                                f"adamw_{n}")
            for kind, a in zip(("grad", "delta", "m", "v"), res):
                outs[kind][n] = a.reshape((1,) + a.shape)
        return outs


def _step(x, p, positions, loss_target, w, m, v):
    T, D = x.shape[1], x.shape[2]
    QL, KVL = w["g_cq"].shape[1], w["g_ckv"].shape[1]
    dims = {
        "H": 4 * w["w_uq"].shape[2] // MLA_QK, "HD": w["w_br_dil"].shape[1] // DIL_HEAD, "QL": QL, "KVL": KVL,
        "LP": _round_up(QL + KVL + MLA_ROPE, LANES), "tf": _pick(4 * w["w1_gate"].shape[2], 512),
    }
    small_shapes = {n: w[n].shape for n in SMALL}
    sched = _MeshSchedule(w, m, v, dims)
    Gn = {n: w[n] for n in SMALL}
    Gn["g_q_mla"] = _pad_to(Gn["g_q_mla"], MLA_QK_PAD, 1)
    Gn["g_k_mla"] = _pad_to(Gn["g_k_mla"], MLA_QK_PAD, 1)
    Gn["g_q_dil"] = Gn["g_q_dil"].reshape(len(DIL_GROUPS), 1, DIL_HEAD)
    Gn["g_k_dil"] = Gn["g_k_dil"].reshape(len(DIL_GROUPS), 1, DIL_HEAD)

    pos_b = jnp.broadcast_to(positions.astype(F32).reshape(T, 1), (T, LANES))
    loss, dx, gg = _local_step(sched, x[0], p[0, 0], pos_b, loss_target[0], Gn, dims)
    loss = lax.psum(loss[0, 0], ("x", "y", "c"))
    outs = sched.finish()

    gg["g_q_mla"] = gg["g_q_mla"][:, :MLA_QK]
    gg["g_k_mla"] = gg["g_k_mla"][:, :MLA_QK]
    g_small = _all_reduce_small(_pack_small(gg))
    d_s, m_s, v_s = _adamw(_pack_small({n: w[n] for n in SMALL}), g_small, _pack_small({n: m[n] for n in SMALL}),
                           _pack_small({n: v[n] for n in SMALL}), "adamw_gains")
    for kind, buf in (("grad", g_small), ("delta", d_s), ("m", m_s), ("v", v_s)):
        outs[kind].update(_unpack_small(buf, small_shapes))

    grad_x = dx.reshape(1, T, D)
    return (loss, grad_x, *[outs["grad"][n] for n in WEIGHTS], *[outs["delta"][n] for n in WEIGHTS],
            *[outs["m"][n] for n in WEIGHTS], *[outs["v"][n] for n in WEIGHTS])


def kernel(x, p, positions, g_ffn1, w1_gate, w1_up, w1_down, g_mix, w_in, g_cq, w_uq, g_ckv, w_ukv, g_q_mla, g_k_mla, g_q_dil, g_k_dil, w_br_mla, w_br_dil, w_o, g_ffn2, w2_gate, w2_up, w2_down, g_ple, w_ple_gate, w_ple_proj, loss_target, m_g_ffn1, m_w1_gate, m_w1_up, m_w1_down, m_g_mix, m_w_in, m_g_cq, m_w_uq, m_g_ckv, m_w_ukv, m_g_q_mla, m_g_k_mla, m_g_q_dil, m_g_k_dil, m_w_br_mla, m_w_br_dil, m_w_o, m_g_ffn2, m_w2_gate, m_w2_up, m_w2_down, m_g_ple, m_w_ple_gate, m_w_ple_proj, v_g_ffn1, v_w1_gate, v_w1_up, v_w1_down, v_g_mix, v_w_in, v_g_cq, v_w_uq, v_g_ckv, v_w_ukv, v_g_q_mla, v_g_k_mla, v_g_q_dil, v_g_k_dil, v_w_br_mla, v_w_br_dil, v_w_o, v_g_ffn2, v_w2_gate, v_w2_up, v_w2_down, v_g_ple, v_w_ple_gate, v_w_ple_proj):
    args = locals()
    w = {n: args[n] for n in WEIGHTS}
    m = {n: args["m_" + n] for n in WEIGHTS}
    v = {n: args["v_" + n] for n in WEIGHTS}
    return _step(x, p, positions, loss_target, w, m, v)
```

```python
import functools

import numpy as np
import jax
import jax.numpy as jnp
from jax import lax
from jax.experimental import pallas as pl
from jax.experimental.pallas import tpu as pltpu

F32 = jnp.float32
BF16 = jnp.bfloat16
MESH_ID = pl.DeviceIdType.MESH

MLA_NOPE = 128
MLA_ROPE = 64
MLA_V = 128
MLA_QK = MLA_NOPE + MLA_ROPE
MLA_QK_PAD = 256
DIL_GROUPS = ((128, 1), (512, 4), (2048, 16))
DIL_HEAD = 128
DIL_ROT = DIL_HEAD // 4
DIL_BLOCK = 128
ROPE_THETA = 500000.0
EPS = 1e-6
NEG = -1e30
ADAM_LR = 0.001
ADAM_B1 = 0.9
ADAM_B2 = 0.999
ADAM_EPS = 1e-08
ADAM_WD = 0.01
ADAM_STEP = 10

LANES = 128
VMEM_LIMIT_BYTES = 56 * 1024 * 1024
MM_VMEM_BYTES = 46 * 1024 * 1024

BIG = ("w1_gate", "w1_up", "w1_down", "w_in", "w_uq", "w_ukv", "w_br_mla", "w_br_dil", "w_o",
       "w2_gate", "w2_up", "w2_down", "w_ple_gate", "w_ple_proj")
GATHER_PLAN = (("w1_gu", ("w1_gate", "w1_up")), ("w1_down", ("w1_down",)), ("w_in", ("w_in",)), ("w_uq", ("w_uq",)),
               ("w_ukv", ("w_ukv",)), ("w_br_mla", ("w_br_mla",)), ("w_br_dil", ("w_br_dil",)), ("w_o", ("w_o",)),
               ("w2_gu", ("w2_gate", "w2_up")), ("w2_down", ("w2_down",)), ("w_ple_gate", ("w_ple_gate",)),
               ("w_ple_proj", ("w_ple_proj",)))
ROW_SHARDED = ("w1_down", "w_o", "w2_down", "w_ple_gate")
SMALL = ("g_ffn1", "g_mix", "g_cq", "g_ckv", "g_q_mla", "g_k_mla", "g_q_dil", "g_k_dil", "g_ffn2", "g_ple")
WEIGHTS = ("g_ffn1", "w1_gate", "w1_up", "w1_down", "g_mix", "w_in", "g_cq", "w_uq", "g_ckv", "w_ukv", "g_q_mla",
           "g_k_mla", "g_q_dil", "g_k_dil", "w_br_mla", "w_br_dil", "w_o", "g_ffn2", "w2_gate", "w2_up", "w2_down",
           "g_ple", "w_ple_gate", "w_ple_proj")


def _pick(n, target, align=LANES):
    if n <= target:
        return n
    t = (target // align) * align
    while t >= align:
        if n % t == 0:
            return t
        t -= align
    return n


def _params(n_axes):
    return pltpu.CompilerParams(dimension_semantics=("arbitrary",) * n_axes, vmem_limit_bytes=VMEM_LIMIT_BYTES)


def _sigmoid(x):
    return 1.0 / (1.0 + jnp.exp(-x))


ANY = pl.BlockSpec(memory_space=pl.ANY)


class _Side:
    def __init__(self, arrays, out_shapes, aliases, sem_shapes, phases):
        self.arrays, self.out_shapes, self.aliases = list(arrays), list(out_shapes), dict(aliases)
        self.sem_shapes, self.phases = list(sem_shapes), list(phases)

    def start(self, p, ins, outs, sems):
        for cp in self.phases[p][0](ins, outs, sems):
            cp.start()

    def wait(self, p, ins, outs, sems):
        for cp in self.phases[p][1](ins, outs, sems):
            cp.wait_recv()
        for cp in self.phases[p][0](ins, outs, sems):
            cp.wait_send()

    def run(self, step, n_steps, ins, outs, sems):
        n_ph = len(self.phases)
        assert n_ph <= 2
        starts = (0, int(0.85 * (n_steps - 1)))
        if n_steps <= n_ph:
            @pl.when(step == n_steps - 1)
            def _():
                for p in range(n_ph):
                    self.start(p, ins, outs, sems)
                    self.wait(p, ins, outs, sems)
            return
        for p in range(n_ph):
            @pl.when(step == starts[p])
            def _(p=p):
                if p > 0:
                    self.wait(p - 1, ins, outs, sems)
                self.start(p, ins, outs, sems)

        @pl.when(step == n_steps - 1)
        def _():
            self.wait(n_ph - 1, ins, outs, sems)


def _merge_sides(sides):
    arrays, out_shapes, aliases, sem_shapes, spans = [], [], {}, [], []
    for s in sides:
        assert len(s.phases) == 1
        spans.append((len(arrays), len(out_shapes), len(sem_shapes), s))
        aliases.update({len(arrays) + i: len(out_shapes) + o for i, o in s.aliases.items()})
        arrays += s.arrays
        out_shapes += s.out_shapes
        sem_shapes += s.sem_shapes

    def part(which):
        def fn(ins, outs, sems):
            cps = []
            for a0, o0, s0, s in spans:
                cps += s.phases[0][which](ins[a0:a0 + len(s.arrays)], outs[o0:o0 + len(s.out_shapes)],
                                          sems[s0:s0 + len(s.sem_shapes)])
            return cps
        return fn

    return _Side(arrays, out_shapes, aliases, sem_shapes, [(part(0), part(1))])


def _side_parts(side, n_lead, n_out):
    if side is None:
        return [], [], [], {}
    return (side.arrays, side.out_shapes, side.sem_shapes, {n_lead + i: n_out + o for i, o in side.aliases.items()})


def _carry(kern, side, n_lead, n_out, n_scratch, step_of, n_steps):
    if side is None:
        return kern
    a = n_lead
    b = a + len(side.arrays)
    c = b + n_out
    d = c + len(side.out_shapes)
    e = d + n_scratch

    def wrapped(*refs):
        side.run(step_of(), n_steps, refs[a:b], refs[c:d], refs[e:])
        kern(*refs[:a], *refs[b:c], *refs[d:e])

    return wrapped


def _run_side(side, name):
    n_in, n_out = len(side.arrays), len(side.out_shapes)

    def body(*refs):
        ins, outs, sems = refs[:n_in], refs[n_in:n_in + n_out], refs[n_in + n_out:]
        for p in range(len(side.phases)):
            side.start(p, ins, outs, sems)
            side.wait(p, ins, outs, sems)

    return pl.pallas_call(body, name=name, out_shape=side.out_shapes, in_specs=[ANY] * n_in, out_specs=[ANY] * n_out,
                          scratch_shapes=side.sem_shapes, input_output_aliases=side.aliases)(*side.arrays)


def _mm_tiles(M, N, K, n_pairs, out_bytes, has_res):
    tm = _pick(M, 1024)
    tks = sorted({_pick(K, t) for t in (8192, 5632, 4096, 2816, 2048, 1408, 1024, 512)}, reverse=True)
    tns = sorted({_pick(N, t) for t in (1536, 1024, 512)}, reverse=True)
    for tk in tks:
        for tn in tns:
            need = 4 * n_pairs * (tm * tk + tk * tn) + tm * tn * (4 * (K > tk) + 2 * out_bytes + 8 * has_res + 4)
            if need <= MM_VMEM_BYTES:
                return tm, tn, tk
    raise ValueError((M, N, K))


def _mm(a, b, mode, out_dtype, name, res=None, alpha=1.0, a2=None, b2=None, b2_k_offset=0, side=None):
    if mode == "nn":
        (M, K), (K2, N) = a.shape, b.shape
    elif mode == "nt":
        (M, K), (N, K2) = a.shape, b.shape
    else:
        (K, M), (K2, N) = a.shape, b.shape
    assert K == K2 or (mode == "nt" and K2 > K), (name, a.shape, b.shape)
    assert a.dtype == BF16 and b.dtype == BF16, name
    tm, tn, tk = _mm_tiles(M, N, K, 1 if a2 is None else 2, jnp.dtype(out_dtype).itemsize, res is not None)
    nk = K // tk
    assert b2_k_offset % tk == 0 and (b2_k_offset == 0 or mode == "nt"), name
    k_off2 = b2_k_offset // tk
    if mode == "nn":
        a_spec = pl.BlockSpec((tm, tk), lambda i, j, k: (i, k))
        b_spec = pl.BlockSpec((tk, tn), lambda i, j, k: (k, j))
        dims = (((1,), (0,)), ((), ()))
    elif mode == "nt":
        a_spec = pl.BlockSpec((tm, tk), lambda i, j, k: (i, k))
        b_spec = pl.BlockSpec((tn, tk), lambda i, j, k: (j, k))
        b2_spec = pl.BlockSpec((tn, tk), lambda i, j, k: (j, k + k_off2))
        dims = (((1,), (1,)), ((), ()))
    else:
        a_spec = pl.BlockSpec((tk, tm), lambda i, j, k: (k, i))
        b_spec = pl.BlockSpec((tk, tn), lambda i, j, k: (k, j))
        dims = (((0,), (0,)), ((), ()))
    o_spec = pl.BlockSpec((tm, tn), lambda i, j, k: (i, j))
    has_res = res is not None
    n_pairs = 1 if a2 is None else 2
    n_main = 2 * n_pairs + int(has_res)
    n_side_in = len(side.arrays) if side else 0
    n_side_out = len(side.out_shapes) if side else 0
    n_acc = 1 if nk > 1 else 0
    gi, gj = M // tm, N // tn
    n_steps = gi * gj * nk

    def kern(*refs):
        r_ref = refs[2 * n_pairs] if has_res else None
        o_ref = refs[n_main + n_side_in]
        if side:
            step = (pl.program_id(0) * gj + pl.program_id(1)) * nk + pl.program_id(2)
            side.run(step, n_steps, refs[n_main:n_main + n_side_in],
                     refs[n_main + n_side_in + 1:n_main + n_side_in + 1 + n_side_out],
                     refs[n_main + n_side_in + 1 + n_side_out + n_acc:])
        part = lax.dot_general(refs[0][...], refs[1][...], dims, preferred_element_type=F32)
        if n_pairs == 2:
            part = part + lax.dot_general(refs[2][...], refs[3][...], dims, preferred_element_type=F32)

        def finish(r):
            if alpha != 1.0:
                r = r * alpha
            if has_res:
                r = r_ref[...] + r
            o_ref[...] = r.astype(o_ref.dtype)

        if nk == 1:
            finish(part)
            return
        acc_ref = refs[n_main + n_side_in + 1 + n_side_out]
        k = pl.program_id(2)

        @pl.when(k == 0)
        def _():
            acc_ref[...] = part

        @pl.when(k > 0)
        def _():
            acc_ref[...] += part

        @pl.when(k == nk - 1)
        def _():
            finish(acc_ref[...])

    ins = (a, b) + ((a2, b2) if n_pairs == 2 else ()) + ((res,) if has_res else ())
    in_specs = [a_spec, b_spec] + ([a_spec, b2_spec if mode == "nt" else b_spec] if n_pairs == 2 else [])
    in_specs += [o_spec] if has_res else []
    out_shape = jax.ShapeDtypeStruct((M, N), out_dtype)
    scratch = [pltpu.VMEM((tm, tn), F32)] if nk > 1 else []
    if not side:
        return pl.pallas_call(kern, name=name, grid=(gi, gj, nk), in_specs=in_specs, out_specs=o_spec,
                              out_shape=out_shape, scratch_shapes=scratch, compiler_params=_params(3))(*ins)
    outs = pl.pallas_call(
        kern, name=name, grid=(gi, gj, nk), in_specs=in_specs + [ANY] * n_side_in,
        out_specs=[o_spec] + [ANY] * n_side_out, out_shape=[out_shape] + list(side.out_shapes),
        scratch_shapes=scratch + list(side.sem_shapes),
        input_output_aliases={n_main + i: 1 + o for i, o in side.aliases.items()},
        compiler_params=_params(3))(*ins, *side.arrays)
    return outs[0], list(outs[1:])


def _vcall(body, grid, ins, in_specs, out_shapes, out_specs, name, n_inner_acc=0, n_acc=0, side=None):
    n_in, n_out = len(ins), len(out_shapes)
    n_plain = n_out - n_acc - n_inner_acc

    def kern(*refs):
        vals = body(*[r[...] for r in refs[:n_in]])
        if not isinstance(vals, (tuple, list)):
            vals = (vals,)
        out_refs = refs[n_in:]
        inner_first = pl.program_id(len(grid) - 1) == 0
        first = inner_first
        for ax in range(len(grid) - 1):
            first = jnp.logical_and(first, pl.program_id(ax) == 0)
        for idx, (r, v) in enumerate(zip(out_refs, vals)):
            if idx < n_plain:
                r[...] = v.astype(r.dtype)
                continue
            start = inner_first if idx < n_plain + n_inner_acc else first

            @pl.when(start)
            def _(r=r, v=v):
                r[...] = v.astype(r.dtype)

            @pl.when(jnp.logical_not(start))
            def _(r=r, v=v):
                r[...] += v.astype(r.dtype)

    s_in, s_out, s_sems, s_alias = _side_parts(side, n_in, n_out)

    def step_of():
        step = pl.program_id(0)
        for ax in range(1, len(grid)):
            step = step * grid[ax] + pl.program_id(ax)
        return step

    return pl.pallas_call(
        _carry(kern, side, n_in, n_out, 0, step_of, int(np.prod(grid))), name=name, grid=grid,
        in_specs=list(in_specs) + [ANY] * len(s_in), out_specs=list(out_specs) + [ANY] * len(s_out),
        out_shape=list(out_shapes) + list(s_out), scratch_shapes=s_sems, input_output_aliases=s_alias,
        compiler_params=_params(len(grid)))(*ins, *s_in)


def _rows(tm, c):
    return pl.BlockSpec((tm, c), lambda i: (i, 0))


def _vec(c):
    return pl.BlockSpec((1, c), lambda i: (0, 0))


def _sds(shape, dtype):
    return jax.ShapeDtypeStruct(shape, dtype)


def _rstd(x, c):
    return lax.rsqrt(jnp.sum(x * x, axis=-1, keepdims=True) * (1.0 / c) + EPS)


def _rms_bwd(xh, r, g, dn, c):
    u = dn * g
    dx = r * (u - xh * (jnp.sum(xh * u, axis=-1, keepdims=True) * (1.0 / c)))
    return dx, jnp.sum(dn * xh, axis=0, keepdims=True)


def _rope(t, c, sa, sb, half):
    return t * c + pltpu.roll(t, LANES - half, 1) * sa + pltpu.roll(t, half, 1) * sb


def _rope_t(d, c, sa, sb, half):
    return d * c + pltpu.roll(d * sa, half, 1) + pltpu.roll(d * sb, LANES - half, 1)


def _rope_tables(pos_b, rd, name):
    T = pos_b.shape[0]
    half = rd // 2
    inv = ROPE_THETA ** (-jnp.arange(half, dtype=F32) * 2.0 / rd)
    inv_full = jnp.concatenate([inv, inv, jnp.zeros((LANES - rd,), F32)]).reshape(1, LANES)
    lane = np.arange(LANES)
    ma = jnp.asarray((lane < half).astype(np.float32)).reshape(1, LANES)
    mb = jnp.asarray(((lane >= half) & (lane < rd)).astype(np.float32)).reshape(1, LANES)
    tm = _pick(T, 1024, 8)

    def body(pos, invf, a, b):
        ang = pos * invf
        c, s = jnp.cos(ang), jnp.sin(ang)
        inside = a + b
        return c * inside + (1.0 - inside), -s * a, s * b

    return _vcall(body, (T // tm,), (pos_b, inv_full, ma, mb), [_rows(tm, LANES)] + [_vec(LANES)] * 3,
                  [_sds((T, LANES), F32)] * 3, [_rows(tm, LANES)] * 3, name)


def _rms_fwd(x, g, name):
    T, C = x.shape
    tm = _pick(T, 512, 8)

    def body(xv, gv):
        return xv * _rstd(xv, C) * gv

    return _vcall(body, (T // tm,), (x, g), [_rows(tm, C), _vec(C)], [_sds((T, C), BF16)], [_rows(tm, C)], name)[0]


def _rms_bwd_call(x, g, dn, dres, name, side=None):
    T, C = x.shape
    tm = _pick(T, 256, 8)

    def body(xv, gv, dnv, drv):
        r = _rstd(xv, C)
        dx, dg = _rms_bwd(xv * r, r, gv, dnv.astype(F32), C)
        return drv + dx, dg

    return _vcall(body, (T // tm,), (x, g, dn, dres), [_rows(tm, C), _vec(C), _rows(tm, C), _rows(tm, C)],
                  [_sds((T, C), F32), _sds((1, C), F32)], [_rows(tm, C), _vec(C)], name, n_acc=1, side=side)


def _gate_up(n, w_gu, tf, name, side=None):
    T, D = n.shape
    F = w_gu.shape[1] // 2
    tm = _pick(T, 1024, 16)
    nf = F // tf

    def kern(n_ref, wg_ref, wu_ref, a_ref, b_ref, act_ref):
        x = n_ref[...]
        a = jnp.dot(x, wg_ref[...], preferred_element_type=F32)
        b = jnp.dot(x, wu_ref[...], preferred_element_type=F32)
        a_ref[...] = a.astype(BF16)
        b_ref[...] = b.astype(BF16)
        act_ref[...] = (a * _sigmoid(a) * b).astype(BF16)

    tile = pl.BlockSpec((tm, tf), lambda i, j: (i, j))
    s_in, s_out, s_sems, s_alias = _side_parts(side, 3, 3)
    step_of = lambda: pl.program_id(0) * nf + pl.program_id(1)
    outs = pl.pallas_call(
        _carry(kern, side, 3, 3, 0, step_of, (T // tm) * nf), name=name, grid=(T // tm, nf),
        in_specs=[pl.BlockSpec((tm, D), lambda i, j: (i, 0)), pl.BlockSpec((D, tf), lambda i, j: (0, j)),
                  pl.BlockSpec((D, tf), lambda i, j: (0, j + nf))] + [ANY] * len(s_in),
        out_specs=[tile] * 3 + [ANY] * len(s_out), out_shape=[_sds((T, F), BF16)] * 3 + s_out,
        scratch_shapes=s_sems, input_output_aliases=s_alias, compiler_params=_params(2))(n, w_gu, w_gu, *s_in)
    return outs


def _d_gate_up(dout_b, w_d, a, b, tf, name, side=None):
    T, D = dout_b.shape
    F = w_d.shape[0]
    tm = _pick(T, 1024, 16)
    nf = F // tf

    def kern(d_ref, w_ref, a_ref, b_ref, da_ref, db_ref):
        d = 0.5 * lax.dot_general(d_ref[...], w_ref[...], NT, preferred_element_type=F32)
        a, b = a_ref[...].astype(F32), b_ref[...].astype(F32)
        sg = _sigmoid(a)
        da_ref[...] = (d * b * (sg * (1.0 + a * (1.0 - sg)))).astype(BF16)
        db_ref[...] = (d * (a * sg)).astype(BF16)

    tile = pl.BlockSpec((tm, tf), lambda i, j: (i, j))
    s_in, s_out, s_sems, s_alias = _side_parts(side, 4, 2)
    step_of = lambda: pl.program_id(0) * nf + pl.program_id(1)
    outs = pl.pallas_call(
        _carry(kern, side, 4, 2, 0, step_of, (T // tm) * nf), name=name, grid=(T // tm, nf),
        in_specs=[pl.BlockSpec((tm, D), lambda i, j: (i, 0)), pl.BlockSpec((tf, D), lambda i, j: (j, 0)), tile, tile]
        + [ANY] * len(s_in),
        out_specs=[tile] * 2 + [ANY] * len(s_out), out_shape=[_sds((T, F), BF16)] * 2 + s_out,
        scratch_shapes=s_sems, input_output_aliases=s_alias, compiler_params=_params(2))(dout_b, w_d, a, b, *s_in)
    return outs


def _lat_fwd(lat, g_cq, g_ckv, name):
    T, LP = lat.shape
    QL, KVL = g_cq.shape[1], g_ckv.shape[1]
    tm = _pick(T, 512, 8)

    def body(v, gq, gk):
        xq, xk = v[:, :QL], v[:, QL:QL + KVL]
        return xq * _rstd(xq, QL) * gq, xk * _rstd(xk, KVL) * gk

    return _vcall(body, (T // tm,), (lat, g_cq, g_ckv), [_rows(tm, LP), _vec(QL), _vec(KVL)],
                  [_sds((T, QL), BF16), _sds((T, KVL), BF16)], [_rows(tm, QL), _rows(tm, KVL)], name)


def _lat_bwd(dcq, dckv, dkr, lat, g_cq, g_ckv, name):
    T, LP = lat.shape
    QL, KVL = g_cq.shape[1], g_ckv.shape[1]
    tm = _pick(T, 512, 8)

    def body(dq, dk, dr, v, gq, gk):
        xq, xk = v[:, :QL], v[:, QL:QL + KVL]
        rq, rk = _rstd(xq, QL), _rstd(xk, KVL)
        dxq, dgq = _rms_bwd(xq * rq, rq, gq, dq, QL)
        dxk, dgk = _rms_bwd(xk * rk, rk, gk, dk, KVL)
        return jnp.concatenate([dxq, dxk, dr], axis=1), dgq, dgk

    return _vcall(body, (T // tm,), (dcq, dckv, dkr, lat, g_cq, g_ckv),
                  [_rows(tm, QL), _rows(tm, KVL), _rows(tm, LANES), _rows(tm, LP), _vec(QL), _vec(KVL)],
                  [_sds((T, LP), BF16), _sds((1, QL), F32), _sds((1, KVL), F32)],
                  [_rows(tm, LP), _vec(QL), _vec(KVL)], name, n_acc=2)


def _head_spec(tm, w):
    return pl.BlockSpec((tm, w), lambda i, h: (i, h))


def _row2(tm, w, col=0):
    return pl.BlockSpec((tm, w), lambda i, h: (i, col))


def _vec2(w):
    return pl.BlockSpec((1, w), lambda i, h: (0, 0))


def _mla_q_prep(q_raw, g_q, tabs, H, scale, name):
    T = q_raw.shape[0]
    tm = _pick(T, 512, 8)
    half = MLA_ROPE // 2

    def body(x, g, c, sa, sb):
        n = x * _rstd(x, MLA_QK) * g
        return jnp.concatenate([n[:, :LANES], _rope(n[:, LANES:], c, sa, sb, half)], axis=1) * scale

    return _vcall(body, (T // tm, H), (q_raw, g_q) + tabs,
                  [_head_spec(tm, MLA_QK_PAD), _vec2(MLA_QK_PAD)] + [_row2(tm, LANES)] * 3,
                  [_sds((T, H * MLA_QK_PAD), BF16)], [_head_spec(tm, MLA_QK_PAD)], name)[0]


def _mla_q_bwd(dq, q_raw, g_q, tabs, H, scale, name):
    T = q_raw.shape[0]
    tm = _pick(T, 512, 8)
    half = MLA_ROPE // 2

    def body(d, x, g, c, sa, sb):
        r = _rstd(x, MLA_QK)
        d = d * scale
        dn = jnp.concatenate([d[:, :LANES], _rope_t(d[:, LANES:], c, sa, sb, half)], axis=1)
        return _rms_bwd(x * r, r, g, dn, MLA_QK)

    return _vcall(body, (T // tm, H), (dq, q_raw, g_q) + tabs,
                  [_head_spec(tm, MLA_QK_PAD), _head_spec(tm, MLA_QK_PAD), _vec2(MLA_QK_PAD)] + [_row2(tm, LANES)] * 3,
                  [_sds((T, H * MLA_QK_PAD), BF16), _sds((1, MLA_QK_PAD), F32)],
                  [_head_spec(tm, MLA_QK_PAD), _vec2(MLA_QK_PAD)], name, n_acc=1)


def _mla_k_prep(kv, lat, kr_col, g_k, tabs, H, name):
    T = kv.shape[0]
    tm = _pick(T, 512, 8)
    half = MLA_ROPE // 2

    def body(x, kr, g, c, sa, sb):
        kn = x[:, :LANES]
        r = lax.rsqrt((jnp.sum(kn * kn, axis=-1, keepdims=True) + jnp.sum(kr * kr, axis=-1, keepdims=True))
                      * (1.0 / MLA_QK) + EPS)
        k0 = kn * r * g[:, :LANES]
        k1 = _rope(kr * r * g[:, LANES:], c, sa, sb, half)
        return jnp.concatenate([k0, k1], axis=1), x[:, LANES:]

    return _vcall(body, (T // tm, H), (kv, lat, g_k) + tabs,
                  [_head_spec(tm, 2 * LANES), _row2(tm, LANES, kr_col), _vec2(MLA_QK_PAD)] + [_row2(tm, LANES)] * 3,
                  [_sds((T, H * MLA_QK_PAD), BF16), _sds((T, H * MLA_V), BF16)],
                  [_head_spec(tm, MLA_QK_PAD), _head_spec(tm, MLA_V)], name)


def _mla_k_bwd(dk, dv, kv, lat, kr_col, g_k, tabs, H, name):
    T = kv.shape[0]
    tm = _pick(T, 512, 8)
    half = MLA_ROPE // 2

    def body(d, dvv, x, kr, g, c, sa, sb):
        xx = jnp.concatenate([x[:, :LANES], kr], axis=1)
        r = _rstd(xx, MLA_QK)
        dn = jnp.concatenate([d[:, :LANES], _rope_t(d[:, LANES:], c, sa, sb, half)], axis=1)
        dx, dg = _rms_bwd(xx * r, r, g, dn, MLA_QK)
        return jnp.concatenate([dx[:, :LANES], dvv], axis=1), dx[:, LANES:], dg

    return _vcall(body, (T // tm, H), (dk, dv, kv, lat, g_k) + tabs,
                  [_head_spec(tm, MLA_QK_PAD), _head_spec(tm, MLA_V), _head_spec(tm, 2 * LANES),
                   _row2(tm, LANES, kr_col), _vec2(MLA_QK_PAD)] + [_row2(tm, LANES)] * 3,
                  [_sds((T, H * 2 * LANES), BF16), _sds((T, LANES), F32), _sds((1, MLA_QK_PAD), F32)],
                  [_head_spec(tm, 2 * LANES), _row2(tm, LANES), _vec2(MLA_QK_PAD)], name, n_inner_acc=1, n_acc=1)


def _dil_prep(pd, g_q, g_k, tabs, HD, scale, name):
    T = pd.shape[0]
    W = HD * DIL_HEAD
    G = len(DIL_GROUPS)
    tm = _pick(T, 256, 8)
    half = DIL_ROT // 2

    def body(xq, xk, xv, gq, gk, c, sa, sb):
        outs = []
        for x, g, s in ((xq, gq, scale), (xk, gk, 1.0)):
            heads = []
            for h in range(HD):
                xs = x[:, h * DIL_HEAD:(h + 1) * DIL_HEAD].astype(F32)
                n = _rope(xs * _rstd(xs, DIL_HEAD) * g, c, sa, sb, half)
                heads.append(n * s if s != 1.0 else n)
            outs.append(jnp.concatenate(heads, axis=1))
        return outs[0], outs[1], xv

    gspec = pl.BlockSpec((None, 1, DIL_HEAD), lambda i, g: (g, 0, 0))
    return _vcall(body, (T // tm, G), (pd, pd, pd, g_q, g_k) + tabs,
                  [pl.BlockSpec((tm, W), lambda i, g: (i, 3 * g)), pl.BlockSpec((tm, W), lambda i, g: (i, 3 * g + 1)),
                   pl.BlockSpec((tm, W), lambda i, g: (i, 3 * g + 2)), gspec, gspec] + [_row2(tm, LANES)] * 3,
                  [_sds((T, G * W), F32)] * 3, [pl.BlockSpec((tm, W), lambda i, g: (i, g))] * 3, name)


def _dil_prep_bwd(dq, dk, dv, pd, grp, g_q, g_k, tabs, HD, scale, name):
    T = pd.shape[0]
    W = HD * DIL_HEAD
    tm = _pick(T, 256, 8)
    half = DIL_ROT // 2

    def body(dqv, dkv, dvv, xq, xk, gq, gk, c, sa, sb):
        cols, dgs = [], []
        for d, x, g, s in ((dqv, xq, gq, scale), (dkv, xk, gk, 1.0)):
            heads, dg = [], None
            for h in range(HD):
                sl = slice(h * DIL_HEAD, (h + 1) * DIL_HEAD)
                xs = x[:, sl].astype(F32)
                r = _rstd(xs, DIL_HEAD)
                dh = d[:, sl] * s if s != 1.0 else d[:, sl]
                dx, dgh = _rms_bwd(xs * r, r, g, _rope_t(dh, c, sa, sb, half), DIL_HEAD)
                heads.append(dx)
                dg = dgh if dg is None else dg + dgh
            cols.append(jnp.concatenate(heads, axis=1))
            dgs.append(dg)
        return jnp.concatenate(cols + [dvv], axis=1), dgs[0], dgs[1]

    gq, gk = g_q[grp], g_k[grp]
    return _vcall(body, (T // tm,), (dq, dk, dv, pd, pd, gq, gk) + tabs,
                  [_rows(tm, W)] * 3 + [pl.BlockSpec((tm, W), lambda i: (i, 3 * grp)),
                                        pl.BlockSpec((tm, W), lambda i: (i, 3 * grp + 1)),
                                        _vec(DIL_HEAD), _vec(DIL_HEAD)] + [_rows(tm, LANES)] * 3,
                  [_sds((T, 3 * W), BF16), _sds((1, DIL_HEAD), F32), _sds((1, DIL_HEAD), F32)],
                  [_rows(tm, 3 * W), _vec(DIL_HEAD), _vec(DIL_HEAD)], name, n_acc=2)


def _dil_merge(os_, lses, name):
    T, W = os_[0].shape
    tm = _pick(T, 256, 8)

    def body(o0, o1, o2, l0, l1, l2):
        m = jnp.maximum(jnp.maximum(l0, l1), l2)
        w0, w1, w2 = jnp.exp(l0 - m), jnp.exp(l1 - m), jnp.exp(l2 - m)
        z = w0 + w1 + w2
        return (w0 * o0 + w1 * o1 + w2 * o2) / z, m + jnp.log(z)

    return _vcall(body, (T // tm,), tuple(os_) + tuple(lses), [_rows(tm, W)] * 6,
                  [_sds((T, W), BF16), _sds((T, W), F32)], [_rows(tm, W)] * 2, name)


def _gate_merge(pg, bm, bd, name):
    T, D = bm.shape
    tm = _pick(T, 256, 8)

    def body(g, m, d):
        g = g.astype(F32)
        return _sigmoid(g[:, :D]) * m + _sigmoid(g[:, D:]) * d

    return _vcall(body, (T // tm,), (pg, bm, bd), [_rows(tm, 2 * D), _rows(tm, D), _rows(tm, D)],
                  [_sds((T, D), BF16)], [_rows(tm, D)], name)[0]


def _gate_bwd(dmerged, pg, bm, bd, name):
    T, D = bm.shape
    tm = _pick(T, 256, 8)

    def body(dm, g, m, d):
        g = g.astype(F32)
        s0, s1 = _sigmoid(g[:, :D]), _sigmoid(g[:, D:])
        dpg = jnp.concatenate([dm * m * s0 * (1.0 - s0), dm * d * s1 * (1.0 - s1)], axis=1)
        return dm * s0, dm * s1, dpg

    return _vcall(body, (T // tm,), (dmerged, pg, bm, bd), [_rows(tm, D), _rows(tm, 2 * D), _rows(tm, D), _rows(tm, D)],
                  [_sds((T, D), BF16), _sds((T, D), BF16), _sds((T, 2 * D), BF16)],
                  [_rows(tm, D), _rows(tm, D), _rows(tm, 2 * D)], name)


def _ple_loss(x3, zg, pp, target, name):
    T, D = x3.shape
    tm = _pick(T, 256, 8)

    def body(x, z, p_, t):
        s = _sigmoid(z)
        e = x + s * p_ - t
        dy = e * (1.0 / D)
        part = 0.5 * jnp.sum(jnp.sum(e * e, axis=1, keepdims=True), axis=0, keepdims=True) * (1.0 / D)
        return dy, dy * s, dy * p_ * s * (1.0 - s), jnp.broadcast_to(part, (1, LANES))

    return _vcall(body, (T // tm,), (x3, zg, pp, target), [_rows(tm, D)] * 4,
                  [_sds((T, D), F32), _sds((T, D), BF16), _sds((T, D), BF16), _sds((1, LANES), F32)],
                  [_rows(tm, D)] * 3 + [_vec(LANES)], name, n_acc=1)


NT = (((1,), (1,)), ((), ()))
TN = (((0,), (0,)), ((), ()))


def _diag_mask(s):
    row = lax.broadcasted_iota(jnp.int32, s.shape, 0)
    col = lax.broadcasted_iota(jnp.int32, s.shape, 1)
    return jnp.where(col <= row, s, NEG)


def _causal_pairs(nq, key_major):
    if key_major:
        pairs = [(i, j) for j in range(nq) for i in range(j, nq)]
    else:
        pairs = [(i, j) for i in range(nq) for j in range(i + 1)]
    return (jnp.asarray([pr[0] for pr in pairs], jnp.int32), jnp.asarray([pr[1] for pr in pairs], jnp.int32))


def _mla_fwd(q, k, v, H, name, side=None):
    T = q.shape[0]
    tq = _pick(T, 512)
    nq = T // tq
    hb = 2 if H % 2 == 0 else 1
    qi_tab, kj_tab = _causal_pairs(nq, key_major=False)

    def kern(qi_ref, kj_ref, q_ref, k_ref, vt_ref, o_ref, ot_ref, lse_ref, m_sc, l_sc, acc_sc):
        t = pl.program_id(1)
        qi, kj = qi_ref[t], kj_ref[t]

        @pl.when(kj == 0)
        def _():
            m_sc[...] = jnp.full_like(m_sc, NEG)
            l_sc[...] = jnp.zeros_like(l_sc)
            acc_sc[...] = jnp.zeros_like(acc_sc)

        def tile(diagonal):
            for hh in range(hb):
                qs = slice(hh * MLA_QK_PAD, (hh + 1) * MLA_QK_PAD)
                vs = slice(hh * MLA_V, (hh + 1) * MLA_V)
                st = lax.dot_general(k_ref[:, qs], q_ref[:, qs], NT, preferred_element_type=F32)
                if diagonal:
                    key = lax.broadcasted_iota(jnp.int32, st.shape, 0)
                    qry = lax.broadcasted_iota(jnp.int32, st.shape, 1)
                    st = jnp.where(key <= qry, st, NEG)
                m_prev = m_sc[hh]
                m_new = jnp.maximum(m_prev, jnp.max(st, axis=0, keepdims=True))
                alpha = jnp.exp(m_prev - m_new)
                pt = jnp.exp(st - m_new)
                l_new = alpha * l_sc[hh] + jnp.sum(pt, axis=0, keepdims=True)
                acc = alpha * acc_sc[hh] + jnp.dot(vt_ref[vs, :], pt.astype(BF16), preferred_element_type=F32)
                if diagonal:
                    out_t = acc / l_new
                    o_ref[:, vs] = out_t.T.astype(o_ref.dtype)
                    ot_ref[vs, :] = out_t.astype(ot_ref.dtype)
                    lse_ref[hh] = m_new + jnp.log(l_new)
                else:
                    m_sc[hh] = m_new
                    l_sc[hh] = l_new
                    acc_sc[hh] = acc

        @pl.when(kj < qi)
        def _():
            tile(False)

        @pl.when(kj == qi)
        def _():
            tile(True)

    qspec = lambda w: pl.BlockSpec((tq, hb * w), lambda h, t, qi_ref, kj_ref: (qi_ref[t], h))
    kspec = lambda w: pl.BlockSpec((tq, hb * w), lambda h, t, qi_ref, kj_ref: (kj_ref[t], h))
    vt_spec = pl.BlockSpec((hb * MLA_V, tq), lambda h, t, qi_ref, kj_ref: (h, kj_ref[t]))
    ot_spec = pl.BlockSpec((hb * MLA_V, tq), lambda h, t, qi_ref, kj_ref: (h, qi_ref[t]))
    lse_spec = pl.BlockSpec((hb, 1, tq), lambda h, t, qi_ref, kj_ref: (h, 0, qi_ref[t]))
    n_pairs = qi_tab.shape[0]
    s_in, s_out, s_sems, s_alias = _side_parts(side, 5, 3)
    grid_spec = pltpu.PrefetchScalarGridSpec(
        num_scalar_prefetch=2, grid=(H // hb, n_pairs),
        in_specs=[qspec(MLA_QK_PAD), kspec(MLA_QK_PAD), vt_spec] + [ANY] * len(s_in),
        out_specs=[qspec(MLA_V), ot_spec, lse_spec] + [ANY] * len(s_out),
        scratch_shapes=[pltpu.VMEM((hb, 1, tq), F32), pltpu.VMEM((hb, 1, tq), F32),
                        pltpu.VMEM((hb, MLA_V, tq), F32)] + s_sems)
    step_of = lambda: pl.program_id(0) * n_pairs + pl.program_id(1)
    return pl.pallas_call(
        _carry(kern, side, 5, 3, 3, step_of, (H // hb) * n_pairs), name=name, grid_spec=grid_spec,
        out_shape=[_sds((T, H * MLA_V), BF16), _sds((H * MLA_V, T), BF16), _sds((H, 1, T), F32)] + s_out,
        input_output_aliases=s_alias, compiler_params=_params(2))(qi_tab, kj_tab, q, k, v.T, *s_in)


def _mla_bwd(q, k, v, do, do_t, o_t, lse, H, name, side=None):
    T = q.shape[0]
    tq = _pick(T, 512)
    nq = T // tq
    qi_tab, kj_tab = _causal_pairs(nq, key_major=True)

    def kern(qi_ref, kj_ref, q_ref, k_ref, v_ref, do_ref, dot_ref, ot_ref, lse_ref, dq_ref, dk_ref, dv_ref, dk_sc, dv_sc):
        t = pl.program_id(1)
        qi, kj = qi_ref[t], kj_ref[t]
        rows = pl.ds(pl.multiple_of(qi * tq, tq), tq)

        def tile(diagonal):
            st = lax.dot_general(k_ref[...], q_ref[...], NT, preferred_element_type=F32)
            if diagonal:
                key = lax.broadcasted_iota(jnp.int32, st.shape, 0)
                qry = lax.broadcasted_iota(jnp.int32, st.shape, 1)
                st = jnp.where(key <= qry, st, NEG)
            pt = jnp.exp(st - lse_ref[...])
            dl = jnp.sum(dot_ref[...].astype(F32) * ot_ref[...].astype(F32), axis=0, keepdims=True)
            dpt = jnp.dot(v_ref[...], dot_ref[...], preferred_element_type=F32)
            dst = (pt * (dpt - dl)).astype(BF16)
            dv = jnp.dot(pt.astype(BF16), do_ref[...], preferred_element_type=F32)
            dk = jnp.dot(dst, q_ref[...], preferred_element_type=F32)
            dq = lax.dot_general(dst, k_ref[...], TN, preferred_element_type=F32)
            if diagonal:
                dv_sc[...] = dv
                dk_sc[...] = dk
            else:
                dv_sc[...] += dv
                dk_sc[...] += dk

            @pl.when(kj == 0)
            def _():
                dq_ref[rows, :] = dq

            @pl.when(kj > 0)
            def _():
                dq_ref[rows, :] += dq

        @pl.when(qi == kj)
        def _():
            tile(True)

        @pl.when(qi > kj)
        def _():
            tile(False)

        @pl.when(qi == nq - 1)
        def _():
            dk_ref[...] = dk_sc[...]
            dv_ref[...] = dv_sc[...]

    qspec = lambda w: pl.BlockSpec((tq, w), lambda h, t, qi_ref, kj_ref: (qi_ref[t], h))
    kspec = lambda w: pl.BlockSpec((tq, w), lambda h, t, qi_ref, kj_ref: (kj_ref[t], h))
    t_spec = pl.BlockSpec((MLA_V, tq), lambda h, t, qi_ref, kj_ref: (h, qi_ref[t]))
    lse_spec = pl.BlockSpec((None, 1, tq), lambda h, t, qi_ref, kj_ref: (h, 0, qi_ref[t]))
    n_pairs = qi_tab.shape[0]
    s_in, s_out, s_sems, s_alias = _side_parts(side, 9, 3)
    grid_spec = pltpu.PrefetchScalarGridSpec(
        num_scalar_prefetch=2, grid=(H, n_pairs),
        in_specs=[qspec(MLA_QK_PAD), kspec(MLA_QK_PAD), kspec(MLA_V), qspec(MLA_V), t_spec, t_spec, lse_spec]
        + [ANY] * len(s_in),
        out_specs=[pl.BlockSpec((T, MLA_QK_PAD), lambda h, t, qi_ref, kj_ref: (0, h)), kspec(MLA_QK_PAD), kspec(MLA_V)]
        + [ANY] * len(s_out),
        scratch_shapes=[pltpu.VMEM((tq, MLA_QK_PAD), F32), pltpu.VMEM((tq, MLA_V), F32)] + s_sems)
    step_of = lambda: pl.program_id(0) * n_pairs + pl.program_id(1)
    return pl.pallas_call(
        _carry(kern, side, 9, 3, 2, step_of, H * n_pairs), name=name, grid_spec=grid_spec,
        out_shape=[_sds((T, H * MLA_QK_PAD), F32), _sds((T, H * MLA_QK_PAD), F32), _sds((T, H * MLA_V), F32)] + s_out,
        input_output_aliases=s_alias, compiler_params=_params(2))(qi_tab, kj_tab, q, k, v, do, do_t, o_t, lse, *s_in)


class _DilGeometry:
    def __init__(self, T, dil, HD, grp):
        self.dil, self.sub = dil, max(1, 8 // dil)
        self.tb = self.sub * DIL_BLOCK * dil
        assert T % self.tb == 0, (T, dil)
        self.nblk = T // self.tb
        last = self.nblk - 1
        self.cur_g = pl.BlockSpec((self.tb, DIL_HEAD), lambda i, h: (i, grp * HD + h))
        self.prev_g = pl.BlockSpec((self.tb, DIL_HEAD), lambda i, h: (jnp.maximum(i - 1, 0), grp * HD + h))
        self.next_g = pl.BlockSpec((self.tb, DIL_HEAD), lambda i, h: (jnp.minimum(i + 1, last), grp * HD + h))
        self.cur = pl.BlockSpec((self.tb, DIL_HEAD), lambda i, h: (i, h))
        self.next = pl.BlockSpec((self.tb, DIL_HEAD), lambda i, h: (jnp.minimum(i + 1, last), h))

    def rows(self, b, r):
        if self.dil == 1:
            return pl.ds(b * DIL_BLOCK, DIL_BLOCK)
        return pl.ds(b * DIL_BLOCK * self.dil + r, DIL_BLOCK, stride=self.dil)

    def tiles(self):
        return [(b, r) for b in range(self.sub) for r in range(self.dil)]

    def rows2(self, b, r):
        if self.dil == 1:
            return pl.ds(b * DIL_BLOCK, 2 * DIL_BLOCK)
        return pl.ds(b * DIL_BLOCK * self.dil + r, 2 * DIL_BLOCK, stride=self.dil)

    def keys(self, cur_ref, prev_ref, b, r):
        if b > 0:
            return cur_ref[self.rows2(b - 1, r), :].astype(BF16)
        return jnp.concatenate([prev_ref[self.rows(self.sub - 1, r), :], cur_ref[self.rows(0, r), :]],
                               axis=0).astype(BF16)

    def masks(self, i):
        row = lax.broadcasted_iota(jnp.int32, (DIL_BLOCK, 2 * DIL_BLOCK), 0)
        col = lax.broadcasted_iota(jnp.int32, (DIL_BLOCK, 2 * DIL_BLOCK), 1)
        band = (col >= row) & (col <= row + DIL_BLOCK)
        first = band & (col >= jnp.where(i > 0, 0, DIL_BLOCK))
        no_next = jnp.where(i + 1 < self.nblk, 0, 2 * DIL_BLOCK)
        ok_next = col[:, :DIL_BLOCK] >= row[:, :DIL_BLOCK] + no_next
        return band, first, ok_next


def _twice(x):
    return jnp.concatenate([x, x], axis=1)


def _dil_fwd(qd, kd, vd, grp, dil, HD, name):
    T = qd.shape[0]
    W = HD * DIL_HEAD
    geo = _DilGeometry(T, dil, HD, grp)

    def kern(q_ref, kc_ref, kp_ref, vc_ref, vp_ref, o_ref, lse_ref):
        band, first, _ = geo.masks(pl.program_id(0))
        for b, r in geo.tiles():
            R = geo.rows(b, r)
            q = q_ref[R, :].astype(BF16)
            kk, vv = geo.keys(kc_ref, kp_ref, b, r), geo.keys(vc_ref, vp_ref, b, r)
            s = jnp.where(band if b > 0 else first, lax.dot_general(q, kk, NT, preferred_element_type=F32), NEG)
            m = jnp.max(s, axis=1, keepdims=True)
            e = jnp.exp(s - m)
            l = jnp.sum(e, axis=1, keepdims=True)
            o_ref[R, :] = jnp.dot(e.astype(BF16), vv, preferred_element_type=F32) / l
            lse_ref[R, :] = jnp.broadcast_to(m + jnp.log(l), (DIL_BLOCK, DIL_HEAD))

    return pl.pallas_call(
        kern, name=name, grid=(geo.nblk, HD), in_specs=[geo.cur_g, geo.cur_g, geo.prev_g, geo.cur_g, geo.prev_g],
        out_specs=[geo.cur, geo.cur], out_shape=[_sds((T, W), F32)] * 2,
        compiler_params=_params(2))(qd, kd, kd, vd, vd)


def _dil_delta(do, o, HD, name):
    T, W = do.shape
    tm = _pick(T, 512, 8)

    def body(d, ov):
        prod = d * ov.astype(F32)
        return jnp.concatenate(
            [jnp.broadcast_to(jnp.sum(prod[:, h * DIL_HEAD:(h + 1) * DIL_HEAD], axis=1, keepdims=True), (tm, DIL_HEAD))
             for h in range(HD)], axis=1)

    return _vcall(body, (T // tm,), (do, o), [_rows(tm, W)] * 2, [_sds((T, W), F32)], [_rows(tm, W)], name)[0]


def _dil_bwd(qd, kd, vd, do, delta, lse, grp, dil, HD, name):
    T = qd.shape[0]
    W = HD * DIL_HEAD
    geo = _DilGeometry(T, dil, HD, grp)

    def kern(q_ref, k_ref, v_ref, do_ref, dl_ref, ls_ref, kp_ref, vp_ref, qn_ref, don_ref, dln_ref, lsn_ref,
             dq_ref, dk_ref, dv_ref):
        band, first, ok_next = geo.masks(pl.program_id(0))

        def tile(q, do_, dl, ls, k, v, ok):
            s = jnp.where(ok, lax.dot_general(q, k, NT, preferred_element_type=F32), NEG)
            p = jnp.exp(s - ls)
            ds = p * (lax.dot_general(do_, v, NT, preferred_element_type=F32) - dl)
            return ds.astype(BF16), p.astype(BF16)

        dk_ref[...] = jnp.zeros_like(dk_ref)
        dv_ref[...] = jnp.zeros_like(dv_ref)
        for b, r in geo.tiles():
            R = geo.rows(b, r)
            q, do_ = q_ref[R, :].astype(BF16), do_ref[R, :].astype(BF16)
            kk, vv = geo.keys(k_ref, kp_ref, b, r), geo.keys(v_ref, vp_ref, b, r)
            ds, p_ = tile(q, do_, _twice(dl_ref[R, :]), _twice(ls_ref[R, :]), kk, vv, band if b > 0 else first)
            dq_ref[R, :] = jnp.dot(ds, kk, preferred_element_type=F32)
            dkk = lax.dot_general(ds, q, TN, preferred_element_type=F32)
            dvv = lax.dot_general(p_, do_, TN, preferred_element_type=F32)
            if b > 0:
                R2 = geo.rows2(b - 1, r)
                dk_ref[R2, :] += dkk
                dv_ref[R2, :] += dvv
            else:
                dk_ref[R, :] += dkk[DIL_BLOCK:]
                dv_ref[R, :] += dvv[DIL_BLOCK:]
        for r in range(dil):
            R, Rn = geo.rows(geo.sub - 1, r), geo.rows(0, r)
            qn, don = qn_ref[Rn, :].astype(BF16), don_ref[Rn, :].astype(BF16)
            ds, p_ = tile(qn, don, dln_ref[Rn, :], lsn_ref[Rn, :], k_ref[R, :].astype(BF16), v_ref[R, :].astype(BF16),
                          ok_next)
            dk_ref[R, :] += lax.dot_general(ds, qn, TN, preferred_element_type=F32)
            dv_ref[R, :] += lax.dot_general(p_, don, TN, preferred_element_type=F32)

    return pl.pallas_call(
        kern, name=name, grid=(geo.nblk, HD),
        in_specs=[geo.cur_g, geo.cur_g, geo.cur_g, geo.cur, geo.cur, geo.cur, geo.prev_g, geo.prev_g,
                  geo.next_g, geo.next, geo.next, geo.next],
        out_specs=[geo.cur] * 3, out_shape=[_sds((T, W), F32)] * 3,
        compiler_params=_params(2))(qd, kd, vd, do, delta, lse, kd, vd, qd, do, delta, lse)


def _place():
    return lax.axis_index("x"), lax.axis_index("y"), lax.axis_index("c")


def _other_chips(x, y):
    return [(1 - x, y), (x, 1 - y), (1 - x, 1 - y)]


def _kind(name, shard_shape):
    if name in ROW_SHARDED:
        return "row"
    return "col" if shard_shape[1] % LANES == 0 else "stack"


def _remote(src, dst, send_sem, recv_sem, to):
    return pltpu.make_async_remote_copy(src_ref=src, dst_ref=dst, send_sem=send_sem, recv_sem=recv_sem,
                                        device_id=to, device_id_type=MESH_ID)


def _row_tile(rows, cols, itemsize, align):
    return _pick(rows, max(align, (2 * 1024 * 1024) // (cols * itemsize)), align)


def _dma_sems(n):
    return [pltpu.SemaphoreType.DMA((n,)), pltpu.SemaphoreType.DMA((n,))]


def _gather_plan(shard_shapes):
    info, buf_shapes = {}, {}
    for out_name, names in GATHER_PLAN:
        r, c = shard_shapes[names[0]]
        kind = _kind(names[0], (r, c))
        assert kind == "col" or len(names) == 1, out_name
        buf_shapes[out_name] = (r, 4 * c * len(names)) if kind == "col" else (4, r, c)
        for i, n in enumerate(names):
            assert tuple(shard_shapes[n]) == (r, c), n
            info[n] = (out_name, kind, i * 4 * c, r, c)
    return info, buf_shapes


def _place_own(shard, buf_shape, kind, base, me, name, prev=None):
    r, c = shard.shape
    tr = _row_tile(r, c, 2, 16)

    def kern(me_ref, x_ref, *rest):
        rest[-1][...] = x_ref[...]

    if kind == "col":
        out_spec = pl.BlockSpec((tr, c), lambda i, me_ref: (i, base // c + me_ref[0]))
    else:
        out_spec = pl.BlockSpec((None, tr, c), lambda i, me_ref: (me_ref[0], i, 0))
    in_specs = [pl.BlockSpec((tr, c), lambda i, me_ref: (i, 0))] + ([ANY] if prev is not None else [])
    grid_spec = pltpu.PrefetchScalarGridSpec(num_scalar_prefetch=1, grid=(r // tr,), in_specs=in_specs,
                                             out_specs=out_spec)
    args = (me, shard) + ((prev,) if prev is not None else ())
    return pl.pallas_call(kern, name=name, grid_spec=grid_spec, out_shape=_sds(buf_shape, shard.dtype),
                          input_output_aliases={2: 0} if prev is not None else {}, compiler_params=_params(1))(*args)


def _ag_entry(e):
    return e if isinstance(e, tuple) else (e, 0, 1)


def _buffers_of(names, info):
    out_names = []
    for n in [_ag_entry(e)[0] for e in names]:
        if info[n][0] not in out_names:
            out_names.append(info[n][0])
    return out_names


def _ag_side(names, shards, bufs, info):
    entries = [_ag_entry(e) for e in names]
    names = [e[0] for e in entries]
    out_names = _buffers_of(names, info)
    n_w = len(names)

    def rows_of(w, h):
        r = info[names[w]][3]
        _, p, parts = entries[w]
        size = r // (2 * parts)
        return pl.ds(h * (r // 2) + p * size, size)

    def region(outs, w, chip, h):
        out_name, kind, base, r, cc = info[names[w]]
        o = outs[out_names.index(out_name)]
        if kind == "col":
            return o.at[rows_of(w, h), pl.ds(pl.multiple_of(base + chip * cc, LANES), cc)]
        return o.at[chip, rows_of(w, h), :]

    def hop(first, sending):
        def fn(ins, outs, sems):
            x, y, c = _place()
            me, sibling, cps = 2 * x + y, (x, y, 1 - c), []
            for w in range(n_w):
                for j, (px, py) in enumerate(_other_chips(x, y)):
                    k = 3 * w + j + (0 if first else 3 * n_w)
                    if first and sending:
                        src, dst, to = ins[w].at[rows_of(w, c), :], region(outs, w, me, c), (px, py, c)
                    elif first:
                        src = dst = region(outs, w, 2 * px + py, c)
                        to = (px, py, c)
                    else:
                        src = dst = region(outs, w, 2 * px + py, c if sending else 1 - c)
                        to = sibling
                    cps.append(_remote(src, dst, sems[0].at[k], sems[1].at[k], to))
            return cps
        return fn

    return _Side([shards[n] for n in names] + [bufs[o] for o in out_names],
                 [_sds(bufs[o].shape, bufs[o].dtype) for o in out_names], {n_w + i: i for i in range(len(out_names))},
                 _dma_sems(6 * n_w), [(hop(True, True), hop(True, False)), (hop(False, True), hop(False, False))])


def _rs_sibling_side(views):
    n = len(views)

    def fn(sending):
        def copies(ins, outs, sems):
            x, y, c = _place()
            return [_remote(ins[w].at[:, 1 - c] if sending else outs[w], outs[w], sems[0].at[w], sems[1].at[w],
                            (x, y, 1 - c)) for w in range(n)]
        return copies

    return _Side(views, [_sds((v.shape[0],) + v.shape[2:], v.dtype) for v in views], {}, _dma_sems(n),
                 [(fn(True), fn(False))])


def _rs_chips_side(parts, kinds, widths):
    n = len(parts)

    def piece(ins, w, chip):
        if kinds[w] == "col":
            return ins[w].at[0, :, pl.ds(pl.multiple_of(chip * widths[w], LANES), widths[w])]
        return ins[w].at[chip]

    def fn(sending):
        def copies(ins, outs, sems):
            x, y, c = _place()
            cps = []
            for w in range(n):
                for j, (px, py) in enumerate(_other_chips(x, y)):
                    k = 3 * w + j
                    src = piece(ins, w, 2 * px + py) if sending else outs[w].at[j]
                    cps.append(_remote(src, outs[w].at[j], sems[0].at[k], sems[1].at[k], (px, py, c)))
            return cps
        return copies

    return _Side(parts, [_sds((3, p_.shape[1], widths[w]), p_.dtype) for w, p_ in enumerate(parts)], {},
                 _dma_sems(3 * n), [(fn(True), fn(False))])


def _rs_join_side(halves):
    n = len(halves)

    def fn(sending):
        def copies(ins, outs, sems):
            x, y, c = _place()
            return [_remote(ins[w] if sending else outs[w], outs[w], sems[0].at[w], sems[1].at[w], (x, y, 1 - c))
                    for w in range(n)]
        return copies

    return _Side(halves, [_sds(h.shape, h.dtype) for h in halves], {}, _dma_sems(n), [(fn(True), fn(False))])


def _pair_sum(g, got, c_idx, name):
    n, _, rows, C = g.shape
    tr = _row_tile(rows, C, 2, 16)

    def kern(c_ref, a_ref, b_ref, o_ref):
        o_ref[...] = (a_ref[...].astype(F32) + b_ref[...].astype(F32)).astype(o_ref.dtype)

    grid_spec = pltpu.PrefetchScalarGridSpec(
        num_scalar_prefetch=1, grid=(n, rows // tr),
        in_specs=[pl.BlockSpec((None, None, tr, C), lambda j, i, c_ref: (j, c_ref[0], i, 0)),
                  pl.BlockSpec((None, tr, C), lambda j, i, c_ref: (j, i, 0))],
        out_specs=pl.BlockSpec((None, tr, C), lambda j, i, c_ref: (j, i, 0)))
    return pl.pallas_call(kern, name=name, grid_spec=grid_spec, out_shape=_sds((n, rows, C), BF16),
                          compiler_params=_params(2))(c_idx, g, got)


def _sum_pieces(pair, recv, kind, me, name):
    _, rows, c = recv.shape
    tr = _row_tile(rows, c, 8, 16)

    def kern(me_ref, own_ref, r_ref, o_ref):
        acc = own_ref[...].astype(F32)
        for j in range(3):
            acc = acc + r_ref[j].astype(F32)
        o_ref[...] = acc

    if kind == "col":
        own_spec = pl.BlockSpec((None, tr, c), lambda i, me_ref: (0, i, me_ref[0]))
    else:
        own_spec = pl.BlockSpec((None, tr, c), lambda i, me_ref: (me_ref[0], i, 0))
    grid_spec = pltpu.PrefetchScalarGridSpec(
        num_scalar_prefetch=1, grid=(rows // tr,),
        in_specs=[own_spec, pl.BlockSpec((3, tr, c), lambda i, me_ref: (0, i, 0))],
        out_specs=pl.BlockSpec((tr, c), lambda i, me_ref: (i, 0)))
    return pl.pallas_call(kern, name=name, grid_spec=grid_spec, out_shape=_sds((rows, c), F32),
                          compiler_params=_params(1))(me, pair, recv)


def _all_reduce_small(vec):
    N = vec.shape[1]
    n_dev = 8

    def body(v_ref, out_ref, slots, send_sems, recv_sems):
        x, y, c = _place()
        me = 4 * x + 2 * y + c
        slots[me] = v_ref[...]
        sent = []
        for k in range(1, n_dev):
            px, py, pc = x ^ (k >> 2), y ^ ((k >> 1) & 1), c ^ (k & 1)
            cp = pltpu.make_async_remote_copy(src_ref=v_ref, dst_ref=slots.at[me], send_sem=send_sems.at[k - 1],
                                              recv_sem=recv_sems.at[k - 1], device_id=(px, py, pc),
                                              device_id_type=MESH_ID)
            cp.start()
            sent.append(cp)
        for k in range(1, n_dev):
            px, py, pc = x ^ (k >> 2), y ^ ((k >> 1) & 1), c ^ (k & 1)
            slot = slots.at[4 * px + 2 * py + pc]
            pltpu.make_async_remote_copy(src_ref=slot, dst_ref=slot, send_sem=send_sems.at[k - 1],
                                         recv_sem=recv_sems.at[k - 1], device_id=(px, py, pc),
                                         device_id_type=MESH_ID).wait_recv()
        for cp in sent:
            cp.wait_send()
        acc = slots[0]
        for j in range(1, n_dev):
            acc = acc + slots[j]
        out_ref[...] = acc

    vm = pl.BlockSpec(memory_space=pltpu.VMEM)
    return pl.pallas_call(
        body, name="ar_gains", out_shape=_sds((1, N), F32), in_specs=[vm], out_specs=vm,
        scratch_shapes=[pltpu.VMEM((n_dev, 1, N), F32), pltpu.SemaphoreType.DMA((n_dev - 1,)),
                        pltpu.SemaphoreType.DMA((n_dev - 1,))])(vec)


def _adamw_math(wv, gv, mv, vv):
    m2 = ADAM_B1 * mv + (1.0 - ADAM_B1) * gv
    v2 = ADAM_B2 * vv + (1.0 - ADAM_B2) * (gv * gv)
    m_hat = m2 / (1.0 - ADAM_B1 ** ADAM_STEP)
    v_hat = v2 / (1.0 - ADAM_B2 ** ADAM_STEP)
    return -ADAM_LR * (m_hat / (jnp.sqrt(v_hat) + ADAM_EPS) + ADAM_WD * wv), m2, v2


def _adamw(w, g, m, v, name):
    R, C = w.shape
    tr = _row_tile(R, C, 8, 8)
    return _vcall(_adamw_math, (R // tr,), (w, g, m, v), [_rows(tr, C)] * 4, [_sds((R, C), F32)] * 3,
                  [_rows(tr, C)] * 3, name)


def _adamw_halves(w, own, recv, m, v, c_idx, name):
    R, C = w.shape
    rows = R // 2
    tr = _row_tile(rows, C, 8, 8)
    nb = rows // tr

    def kern(c_ref, w_ref, own_ref, recv_ref, m_ref, v_ref, g_out, d_out, m_out, v_out):
        def update(g_ref):
            g = g_ref[...]
            g_out[...] = g
            d_out[...], m_out[...], v_out[...] = _adamw_math(w_ref[...], g, m_ref[...], v_ref[...])

        @pl.when(pl.program_id(0) == c_ref[0])
        def _():
            update(own_ref)

        @pl.when(pl.program_id(0) != c_ref[0])
        def _():
            update(recv_ref)

    full = pl.BlockSpec((tr, C), lambda h, i, c_ref: (h * nb + i, 0))
    own_spec = pl.BlockSpec((tr, C), lambda h, i, c_ref: (jnp.where(h == c_ref[0], i, 0), 0))
    recv_spec = pl.BlockSpec((tr, C), lambda h, i, c_ref: (jnp.where(h == c_ref[0], 0, i), 0))
    grid_spec = pltpu.PrefetchScalarGridSpec(num_scalar_prefetch=1, grid=(2, nb),
                                             in_specs=[full, own_spec, recv_spec, full, full], out_specs=[full] * 4)
    return pl.pallas_call(kern, name=name, grid_spec=grid_spec, out_shape=[_sds((R, C), F32)] * 4,
                          compiler_params=_params(2))(c_idx, w, own, recv, m, v)


def _pad_to(a, n, axis):
    extra = n - a.shape[axis]
    if extra == 0:
        return a
    pads = [(0, 0)] * a.ndim
    pads[axis] = (0, extra)
    return jnp.pad(a, pads)


def _round_up(n, m):
    return -(-n // m) * m


def _natural(buf, kind):
    if kind == "col":
        return buf
    n, r, c = buf.shape
    return buf.reshape(n * r, c) if kind == "row" else buf.transpose(1, 0, 2).reshape(r, n * c)


def _halves_view(g, kind, shard_shape):
    r, c = shard_shape
    if kind == "col":
        return g.reshape(1, 2, r // 2, 4 * c)
    if kind == "stack":
        g = g.reshape(r, 4, c).transpose(1, 0, 2)
    return g.reshape(4, 2, r // 2, c)


def _pack_small(vals):
    return jnp.concatenate([_pad_to(vals[n].reshape(1, -1), _round_up(vals[n].size, LANES), 1) for n in SMALL], axis=1)


def _unpack_small(vec, shapes):
    out, off = {}, 0
    for n in SMALL:
        size = int(np.prod(shapes[n]))
        out[n] = vec[:, off:off + size].reshape(shapes[n])
        off += _round_up(size, LANES)
    return out


def _mm_s(sched, a, b, mode, out_dtype, name, **kw):
    side = sched.side(name)
    if side is None:
        return _mm(a, b, mode, out_dtype, name, **kw)
    out, side_outs = _mm(a, b, mode, out_dtype, name, side=side, **kw)
    sched.done(name, side_outs)
    return out


def _call_s(sched, name, n_out, fn):
    side = sched.side(name)
    outs = fn(side)
    if side is not None:
        sched.done(name, list(outs[n_out:]))
    return outs[:n_out]


def _ffn_fwd(sched, x, g, tf, tag):
    w = tag[-1]
    n = _rms_fwd(x, g, f"{tag}_norm")
    a, b, act = _call_s(sched, f"{tag}_gate_up", 3,
                        lambda side: _gate_up(n, sched.weight(f"w{w}_gu"), tf, f"{tag}_gate_up", side=side))
    out = _mm_s(sched, act, sched.weight(f"w{w}_down"), "nn", F32, f"{tag}_down", res=x, alpha=0.5)
    return out, (n, a, b, act)


def _ffn_bwd(sched, dout, x, g, saved, tf, tag):
    w = tag[-1]
    w_gu, w_d = sched.weight(f"w{w}_gu"), sched.weight(f"w{w}_down")
    n, a, b, act = saved
    F = act.shape[1]
    dout_b = dout.astype(BF16)
    sched.grad(f"w{w}_down", _mm_s(sched, act, dout_b, "tn", BF16, f"{tag}_d_wdown", alpha=0.5))
    da, db = _call_s(sched, f"{tag}_d_act", 2,
                     lambda side: _d_gate_up(dout_b, w_d, a, b, tf, f"{tag}_d_act", side=side))
    sched.grad(f"w{w}_gate", _mm_s(sched, n, da, "tn", BF16, f"{tag}_d_wgate"))
    sched.grad(f"w{w}_up", _mm_s(sched, n, db, "tn", BF16, f"{tag}_d_wup"))
    dn = _mm_s(sched, da, w_gu, "nt", F32, f"{tag}_d_norm", a2=db, b2=w_gu, b2_k_offset=F)
    return _call_s(sched, f"{tag}_d_x", 2, lambda side: _rms_bwd_call(x, g, dn, dout, f"{tag}_d_x", side=side))


def _local_step(sched, x, p, pos_b, target, Gn, dims):
    T, D = x.shape
    H, HD, QL, KVL, LP, tf = dims["H"], dims["HD"], dims["QL"], dims["KVL"], dims["LP"], dims["tf"]
    Wd = HD * DIL_HEAD
    scale_mla, scale_dil = MLA_QK ** -0.5, DIL_HEAD ** -0.5
    kr_col = (QL + KVL) // LANES
    tab_mla = tuple(_rope_tables(pos_b, MLA_ROPE, "rope_tab_mla"))
    tab_dil = tuple(_rope_tables(pos_b, DIL_ROT, "rope_tab_dil"))

    W = sched.weight
    mm = functools.partial(_mm_s, sched)

    x1, ffn1 = _ffn_fwd(sched, x, Gn["g_ffn1"], tf, "ffn1")
    h = _rms_fwd(x1, Gn["g_mix"], "mix_norm")
    lat = mm(h, W("w_lat"), "nn", F32, "proj_lat")
    pd = mm(h, W("w_dil"), "nn", BF16, "proj_dil")
    pg = mm(h, W("w_gin"), "nn", BF16, "proj_gate")

    cq, ckv = _lat_fwd(lat, Gn["g_cq"], Gn["g_ckv"], "lat_norm")
    q_raw = mm(cq, W("w_uq"), "nn", F32, "mla_q_up")
    kv = mm(ckv, W("w_ukv"), "nn", F32, "mla_kv_up")
    q = _mla_q_prep(q_raw, Gn["g_q_mla"], tab_mla, H, scale_mla, "mla_q_prep")
    k, v = _mla_k_prep(kv, lat, kr_col, Gn["g_k_mla"], tab_mla, H, "mla_k_prep")
    o_mla, o_mla_t, lse_mla = _call_s(sched, "mla_attn", 3, lambda side: _mla_fwd(q, k, v, H, "mla_attn", side=side))

    qd, kd, vd = _dil_prep(pd, Gn["g_q_dil"], Gn["g_k_dil"], tab_dil, HD, scale_dil, "dil_prep")
    og, lg = [], []
    for grp, (win, dil) in enumerate(DIL_GROUPS):
        o_, l_ = _dil_fwd(qd, kd, vd, grp, dil, HD, f"dil_attn{grp}")
        og.append(o_)
        lg.append(l_)
    o_dil, lse_dil = _dil_merge(og, lg, "dil_merge")

    bm = mm(o_mla, W("w_br_mla"), "nn", F32, "branch_mla")
    bd = mm(o_dil, W("w_br_dil"), "nn", F32, "branch_dil")
    merged = _gate_merge(pg, bm, bd, "gate_merge")
    x2 = mm(merged, W("w_o"), "nn", F32, "out_proj", res=x1)

    x3, ffn2 = _ffn_fwd(sched, x2, Gn["g_ffn2"], tf, "ffn2")
    n4 = _rms_fwd(x3, Gn["g_ple"], "ple_norm")
    zg = mm(n4, W("w_ple_gate"), "nn", F32, "ple_gate")
    p_b = p.astype(BF16)
    pp = mm(p_b, W("w_ple_proj"), "nn", F32, "ple_proj")
    dy, dpp, dzg, loss = _ple_loss(x3, zg, pp, target, "ple_loss")

    gg = {}
    sched.grad("w_ple_proj", mm(p_b, dpp, "tn", BF16, "d_w_ple_proj"))
    sched.grad("w_ple_gate", mm(n4, dzg, "tn", BF16, "d_w_ple_gate"))
    dn4 = mm(dzg, W("w_ple_gate"), "nt", F32, "d_ple_norm")
    dx3, gg["g_ple"] = _rms_bwd_call(x3, Gn["g_ple"], dn4, dy, "d_x3")

    dx2, gg["g_ffn2"] = _ffn_bwd(sched, dx3, x2, Gn["g_ffn2"], ffn2, tf, "ffn2")

    dx2_b = dx2.astype(BF16)
    sched.grad("w_o", mm(merged, dx2_b, "tn", BF16, "d_w_o"))
    dmerged = mm(dx2_b, W("w_o"), "nt", F32, "d_merged")
    dbm, dbd, dpg = _gate_bwd(dmerged, pg, bm, bd, "d_gate")
    sched.grad("w_br_mla", mm(o_mla, dbm, "tn", BF16, "d_w_br_mla"))
    sched.grad("w_br_dil", mm(o_dil, dbd, "tn", BF16, "d_w_br_dil"))
    do_mla = mm(dbm, W("w_br_mla"), "nt", BF16, "d_o_mla")
    do_dil = mm(dbd, W("w_br_dil"), "nt", F32, "d_o_dil")
    delta_dil = _dil_delta(do_dil, o_dil, HD, "dil_delta")

    dh = mm(dpg, W("w_gin"), "nt", F32, "d_h_gate")
    sched.grad("w_gin", mm(h, dpg, "tn", BF16, "d_w_gin"))
    gq_d, gk_d = Gn["g_q_dil"], Gn["g_k_dil"]
    dgq_d, dgk_d = [], []
    for grp, (win, dil) in enumerate(DIL_GROUPS):
        dq_, dk_, dv_ = _dil_bwd(qd, kd, vd, do_dil, delta_dil, lse_dil, grp, dil, HD, f"dil_bwd{grp}")
        dpd_g, dgq_, dgk_ = _dil_prep_bwd(dq_, dk_, dv_, pd, grp, gq_d, gk_d, tab_dil, HD, scale_dil, f"d_dil_prep{grp}")
        dgq_d.append(dgq_)
        dgk_d.append(dgk_)
        w_g = W("w_dil")[:, grp * 3 * Wd:(grp + 1) * 3 * Wd]
        dh = mm(dpd_g, w_g, "nt", F32, f"d_h_dil{grp}", res=dh)
        sched.grad(f"w_dil{grp}", mm(h, dpd_g, "tn", BF16, f"d_w_dil{grp}"))
    gg["g_q_dil"] = jnp.concatenate(dgq_d, axis=0)
    gg["g_k_dil"] = jnp.concatenate(dgk_d, axis=0)

    dq, dk, dv = _call_s(sched, "mla_bwd", 3,
                         lambda side: _mla_bwd(q, k, v, do_mla, do_mla.T, o_mla_t, lse_mla, H, "mla_bwd", side=side))
    dq_raw, gg["g_q_mla"] = _mla_q_bwd(dq, q_raw, Gn["g_q_mla"], tab_mla, H, scale_mla, "d_mla_q_prep")
    dkv, dkr, gg["g_k_mla"] = _mla_k_bwd(dk, dv, kv, lat, kr_col, Gn["g_k_mla"], tab_mla, H, "d_mla_k_prep")
    sched.grad("w_uq", mm(cq, dq_raw, "tn", BF16, "d_w_uq"))
    sched.grad("w_ukv", mm(ckv, dkv, "tn", BF16, "d_w_ukv"))
    dcq = mm(dq_raw, W("w_uq"), "nt", F32, "d_cq")
    dckv = mm(dkv, W("w_ukv"), "nt", F32, "d_ckv")
    dlat, gg["g_cq"], gg["g_ckv"] = _lat_bwd(dcq, dckv, dkr, lat, Gn["g_cq"], Gn["g_ckv"], "d_lat_norm")
    dh = mm(dlat, W("w_lat"), "nt", F32, "d_h_lat", res=dh)
    sched.grad("w_lat", mm(h, dlat, "tn", BF16, "d_w_lat"))

    dx1, gg["g_mix"] = _rms_bwd_call(x1, Gn["g_mix"], dh, dx2, "d_x1")
    dx, gg["g_ffn1"] = _ffn_bwd(sched, dx1, x, Gn["g_ffn1"], ffn1, tf, "ffn1")
    return loss, dx, gg


def _layout_weight(name, full, dims):
    H, QL, KVL, LP, Wd = dims["H"], dims["QL"], dims["KVL"], dims["LP"], dims["HD"] * DIL_HEAD
    off_dil = QL + KVL + MLA_ROPE
    off_gate = off_dil + 3 * len(DIL_GROUPS) * Wd
    if name == "w_lat":
        return _pad_to(full("w_in")[:, :off_dil], LP, 1)
    if name == "w_dil":
        return full("w_in")[:, off_dil:off_gate]
    if name == "w_gin":
        return full("w_in")[:, off_gate:]
    if name == "w_uq":
        return _pad_to(full("w_uq").reshape(QL, H, MLA_QK), MLA_QK_PAD, 2).reshape(QL, H * MLA_QK_PAD)
    return full(name)


def _natural_grad(name, gw, dims):
    H, QL, KVL = dims["H"], dims["QL"], dims["KVL"]
    if name == "w_in":
        return jnp.concatenate([gw["w_lat"][:, :QL + KVL + MLA_ROPE]] + [gw[f"w_dil{g}"] for g in range(len(DIL_GROUPS))]
                               + [gw["w_gin"]], axis=1)
    if name == "w_uq":
        return gw["w_uq"].reshape(QL, H, MLA_QK_PAD)[:, :, :MLA_QK].reshape(QL, H * MLA_QK)
    return gw[name]


WEIGHT_SOURCES = {"w1_gu": ("w1_gate", "w1_up"), "w2_gu": ("w2_gate", "w2_up"), "w_lat": ("w_in",), "w_dil": ("w_in",),
                  "w_gin": ("w_in",)}
AG_FIRST = ("w1_gate", "w1_up")
AG_RIDES = {"ffn1_gate_up": ("w1_down", ("w_in", 0, 2)), "ffn1_down": (("w_in", 1, 2), "w_uq", "w_ukv"),
            "mla_attn": ("w_br_mla", "w_br_dil", "w_o", "w_ple_gate", "w_ple_proj", "w2_gate", "w2_up", "w2_down")}
RS_FFN2 = ("w_ple_proj", "w_ple_gate", "w2_down", "w2_gate", "w2_up")
RS_MIXER = ("w_o", "w_br_mla", "w_br_dil", "w_in", "w_uq", "w_ukv")
RS_RIDES = {
    "d_merged": (("sibling", RS_FFN2),),
    "mla_bwd": (("chips", RS_FFN2),),
    "d_h_lat": (("join", RS_FFN2),),
    "ffn1_d_wdown": (("sibling", RS_MIXER),),
    "ffn1_d_act": (("chips", ("w_o", "w_br_mla", "w_br_dil", "w_uq", "w_ukv")), ("sibling", ("w1_down",))),
    "ffn1_d_wgate": (("chips", ("w1_down",)),),
    "ffn1_d_wup": (("sibling", ("w1_gate",)),),
    "ffn1_d_norm": (("chips", ("w_in", "w1_gate")), ("sibling", ("w1_up",))),
    "ffn1_d_x": (("chips", ("w1_up",)),),
}
RS_LAST = ((("join", RS_MIXER + ("w1_down", "w1_gate", "w1_up")),),)


class _MeshSchedule:
    def __init__(self, w, m, v, dims):
        self.w, self.m, self.v, self.dims = w, m, v, dims
        self.shapes = {n: tuple(w[n].shape[1:]) for n in BIG}
        self.kinds = {n: _kind(n, self.shapes[n]) for n in BIG}
        self.info, buf_shapes = _gather_plan(self.shapes)
        x, y, c = _place()
        self.me = (2 * x + y).astype(jnp.int32).reshape(1)
        self.c_idx = c.astype(jnp.int32).reshape(1)
        self.shards = {n: w[n][0].astype(BF16) for n in BIG}
        self.bufs, self.gathered, self.parts_done, self.layout = {}, set(), {}, {}
        for n in BIG:
            out_name, kind, base, _, _ = self.info[n]
            self.bufs[out_name] = _place_own(self.shards[n], buf_shapes[out_name], kind, base, self.me, f"ag_own_{n}",
                                             prev=self.bufs.get(out_name))
        self.gw, self.views, self.pairs, self.halves, self.recv = {}, {}, {}, {}, {}
        self._ag_done(AG_FIRST, _run_side(self._ag(AG_FIRST), "ag_first"))

    def _ag(self, names):
        return _ag_side(names, self.shards, self.bufs, self.info)

    def _ag_done(self, names, outs):
        for out_name, buf in zip(_buffers_of(names, self.info), outs):
            self.bufs[out_name] = buf
        for name, _, parts in [_ag_entry(e) for e in names]:
            self.parts_done[name] = self.parts_done.get(name, 0) + 1
            if self.parts_done[name] == parts:
                self.gathered.add(name)

    def weight(self, name):
        if name not in self.layout:
            assert all(s in self.gathered for s in WEIGHT_SOURCES.get(name, (name,))), name
            if name in self.bufs and name not in self.info:
                self.layout[name] = self.bufs[name]
            else:
                full = lambda n: _natural(self.bufs[self.info[n][0]], self.info[n][1])
                self.layout[name] = _layout_weight(name, full, self.dims)
        return self.layout[name]

    def grad(self, name, g):
        self.gw[name] = g

    def _rs_side(self, stages):
        sides = []
        for stage, names in stages:
            if stage == "sibling":
                for n in names:
                    self.views[n] = _halves_view(_natural_grad(n, self.gw, self.dims), self.kinds[n], self.shapes[n])
                sides.append(_rs_sibling_side([self.views[n] for n in names]))
            elif stage == "chips":
                sides.append(_rs_chips_side([self.pairs[n] for n in names], [self.kinds[n] for n in names],
                                            [self.shapes[n][1] for n in names]))
            else:
                sides.append(_rs_join_side([self.halves[n] for n in names]))
        return sides[0] if len(sides) == 1 else _merge_sides(sides)

    def _rs_done(self, stages, outs):
        for stage, names in stages:
            got, outs = outs[:len(names)], outs[len(names):]
            for n, a in zip(names, got):
                if stage == "sibling":
                    self.pairs[n] = _pair_sum(self.views[n], a, self.c_idx, f"rs_pair_{n}")
                elif stage == "chips":
                    self.halves[n] = _sum_pieces(self.pairs[n], a, self.kinds[n], self.me, f"rs_sum_{n}")
                else:
                    self.recv[n] = a

    def side(self, tag):
        if tag in AG_RIDES:
            return self._ag(AG_RIDES[tag])
        if tag in RS_RIDES:
            return self._rs_side(RS_RIDES[tag])
        return None

    def done(self, tag, outs):
        if tag in AG_RIDES:
            self._ag_done(AG_RIDES[tag], outs)
        else:
            self._rs_done(RS_RIDES[tag], outs)

    def finish(self):
        for k, stages in enumerate(RS_LAST):
            self._rs_done(stages, _run_side(self._rs_side(stages), f"rs_last{k}"))
        outs = {"grad": {}, "delta": {}, "m": {}, "v": {}}
        for n in BIG:
            res = _adamw_halves(self.w[n][0], self.halves[n], self.recv[n], self.m[n][0], self.v[n][0], self.c_idx,
                                f"adamw_{n}")
            for kind, a in zip(("grad", "delta", "m", "v"), res):
                outs[kind][n] = a.reshape((1,) + a.shape)
        return outs


def _step(x, p, positions, loss_target, w, m, v):
    T, D = x.shape[1], x.shape[2]
    QL, KVL = w["g_cq"].shape[1], w["g_ckv"].shape[1]
    dims = {
        "H": 4 * w["w_uq"].shape[2] // MLA_QK, "HD": w["w_br_dil"].shape[1] // DIL_HEAD, "QL": QL, "KVL": KVL,
        "LP": _round_up(QL + KVL + MLA_ROPE, LANES), "tf": _pick(4 * w["w1_gate"].shape[2], 512),
    }
    small_shapes = {n: w[n].shape for n in SMALL}
    sched = _MeshSchedule(w, m, v, dims)
    Gn = {n: w[n] for n in SMALL}
    Gn["g_q_mla"] = _pad_to(Gn["g_q_mla"], MLA_QK_PAD, 1)
    Gn["g_k_mla"] = _pad_to(Gn["g_k_mla"], MLA_QK_PAD, 1)
    Gn["g_q_dil"] = Gn["g_q_dil"].reshape(len(DIL_GROUPS), 1, DIL_HEAD)
    Gn["g_k_dil"] = Gn["g_k_dil"].reshape(len(DIL_GROUPS), 1, DIL_HEAD)

    pos_b = jnp.broadcast_to(positions.astype(F32).reshape(T, 1), (T, LANES))
    loss, dx, gg = _local_step(sched, x[0], p[0, 0], pos_b, loss_target[0], Gn, dims)
    loss = lax.psum(loss[0, 0], ("x", "y", "c"))
    outs = sched.finish()

    gg["g_q_mla"] = gg["g_q_mla"][:, :MLA_QK]
    gg["g_k_mla"] = gg["g_k_mla"][:, :MLA_QK]
    g_small = _all_reduce_small(_pack_small(gg))
    d_s, m_s, v_s = _adamw(_pack_small({n: w[n] for n in SMALL}), g_small, _pack_small({n: m[n] for n in SMALL}),
                           _pack_small({n: v[n] for n in SMALL}), "adamw_gains")
    for kind, buf in (("grad", g_small), ("delta", d_s), ("m", m_s), ("v", v_s)):
        outs[kind].update(_unpack_small(buf, small_shapes))

    grad_x = dx.reshape(1, T, D)
    return (loss, grad_x, *[outs["grad"][n] for n in WEIGHTS], *[outs["delta"][n] for n in WEIGHTS],
            *[outs["m"][n] for n in WEIGHTS], *[outs["v"][n] for n in WEIGHTS])


def kernel(x, p, positions, g_ffn1, w1_gate, w1_up, w1_down, g_mix, w_in, g_cq, w_uq, g_ckv, w_ukv, g_q_mla, g_k_mla, g_q_dil, g_k_dil, w_br_mla, w_br_dil, w_o, g_ffn2, w2_gate, w2_up, w2_down, g_ple, w_ple_gate, w_ple_proj, loss_target, m_g_ffn1, m_w1_gate, m_w1_up, m_w1_down, m_g_mix, m_w_in, m_g_cq, m_w_uq, m_g_ckv, m_w_ukv, m_g_q_mla, m_g_k_mla, m_g_q_dil, m_g_k_dil, m_w_br_mla, m_w_br_dil, m_w_o, m_g_ffn2, m_w2_gate, m_w2_up, m_w2_down, m_g_ple, m_w_ple_gate, m_w_ple_proj, v_g_ffn1, v_w1_gate, v_w1_up, v_w1_down, v_g_mix, v_w_in, v_g_cq, v_w_uq, v_g_ckv, v_w_ukv, v_g_q_mla, v_g_k_mla, v_g_q_dil, v_g_k_dil, v_w_br_mla, v_w_br_dil, v_w_o, v_g_ffn2, v_w2_gate, v_w2_up, v_w2_down, v_g_ple, v_w_ple_gate, v_w_ple_proj):
    args = locals()
    w = {n: args[n] for n in WEIGHTS}
    m = {n: args["m_" + n] for n in WEIGHTS}
    v = {n: args["v_" + n] for n in WEIGHTS}
    return _step(x, p, positions, loss_target, w, m, v)
```

```python
import functools

import numpy as np
import jax
import jax.numpy as jnp
from jax import lax
from jax.experimental import pallas as pl
from jax.experimental.pallas import tpu as pltpu

F32 = jnp.float32
BF16 = jnp.bfloat16
MESH_ID = pl.DeviceIdType.MESH

MLA_NOPE = 128
MLA_ROPE = 64
MLA_V = 128
MLA_QK = MLA_NOPE + MLA_ROPE
MLA_QK_PAD = 256
DIL_GROUPS = ((128, 1), (512, 4), (2048, 16))
DIL_HEAD = 128
DIL_ROT = DIL_HEAD // 4
DIL_BLOCK = 128
ROPE_THETA = 500000.0
EPS = 1e-6
NEG = -1e30
ADAM_LR = 0.001
ADAM_B1 = 0.9
ADAM_B2 = 0.999
ADAM_EPS = 1e-08
ADAM_WD = 0.01
ADAM_STEP = 10

LANES = 128
VMEM_LIMIT_BYTES = 56 * 1024 * 1024
MM_VMEM_BYTES = 46 * 1024 * 1024

BIG = ("w1_gate", "w1_up", "w1_down", "w_in", "w_uq", "w_ukv", "w_br_mla", "w_br_dil", "w_o",
       "w2_gate", "w2_up", "w2_down", "w_ple_gate", "w_ple_proj")
GATHER_PLAN = (("w1_gu", ("w1_gate", "w1_up")), ("w1_down", ("w1_down",)), ("w_in", ("w_in",)), ("w_uq", ("w_uq",)),
               ("w_ukv", ("w_ukv",)), ("w_br_mla", ("w_br_mla",)), ("w_br_dil", ("w_br_dil",)), ("w_o", ("w_o",)),
               ("w2_gu", ("w2_gate", "w2_up")), ("w2_down", ("w2_down",)), ("w_ple_gate", ("w_ple_gate",)),
               ("w_ple_proj", ("w_ple_proj",)))
ROW_SHARDED = ("w1_down", "w_o", "w2_down", "w_ple_gate")
SMALL = ("g_ffn1", "g_mix", "g_cq", "g_ckv", "g_q_mla", "g_k_mla", "g_q_dil", "g_k_dil", "g_ffn2", "g_ple")
WEIGHTS = ("g_ffn1", "w1_gate", "w1_up", "w1_down", "g_mix", "w_in", "g_cq", "w_uq", "g_ckv", "w_ukv", "g_q_mla",
           "g_k_mla", "g_q_dil", "g_k_dil", "w_br_mla", "w_br_dil", "w_o", "g_ffn2", "w2_gate", "w2_up", "w2_down",
           "g_ple", "w_ple_gate", "w_ple_proj")


def _pick(n, target, align=LANES):
    if n <= target:
        return n
    t = (target // align) * align
    while t >= align:
        if n % t == 0:
            return t
        t -= align
    return n


def _params(n_axes):
    return pltpu.CompilerParams(dimension_semantics=("arbitrary",) * n_axes, vmem_limit_bytes=VMEM_LIMIT_BYTES)


def _sigmoid(x):
    return 0.5 * jnp.tanh(0.5 * x) + 0.5


ANY = pl.BlockSpec(memory_space=pl.ANY)


class _Side:
    def __init__(self, arrays, out_shapes, aliases, sem_shapes, phases):
        self.arrays, self.out_shapes, self.aliases = list(arrays), list(out_shapes), dict(aliases)
        self.sem_shapes, self.phases = list(sem_shapes), list(phases)

    def start(self, p, ins, outs, sems):
        for cp in self.phases[p][0](ins, outs, sems):
            cp.start()

    def wait(self, p, ins, outs, sems):
        for cp in self.phases[p][1](ins, outs, sems):
            cp.wait_recv()
        for cp in self.phases[p][0](ins, outs, sems):
            cp.wait_send()

    def run(self, step, n_steps, ins, outs, sems):
        n_ph = len(self.phases)
        assert n_ph <= 2
        starts = (0, int(0.85 * (n_steps - 1)))
        if n_steps <= n_ph:
            @pl.when(step == n_steps - 1)
            def _():
                for p in range(n_ph):
                    self.start(p, ins, outs, sems)
                    self.wait(p, ins, outs, sems)
            return
        for p in range(n_ph):
            @pl.when(step == starts[p])
            def _(p=p):
                if p > 0:
                    self.wait(p - 1, ins, outs, sems)
                self.start(p, ins, outs, sems)

        @pl.when(step == n_steps - 1)
        def _():
            self.wait(n_ph - 1, ins, outs, sems)


def _merge_sides(sides):
    arrays, out_shapes, aliases, sem_shapes, spans = [], [], {}, [], []
    for s in sides:
        assert len(s.phases) == 1
        spans.append((len(arrays), len(out_shapes), len(sem_shapes), s))
        aliases.update({len(arrays) + i: len(out_shapes) + o for i, o in s.aliases.items()})
        arrays += s.arrays
        out_shapes += s.out_shapes
        sem_shapes += s.sem_shapes

    def part(which):
        def fn(ins, outs, sems):
            cps = []
            for a0, o0, s0, s in spans:
                cps += s.phases[0][which](ins[a0:a0 + len(s.arrays)], outs[o0:o0 + len(s.out_shapes)],
                                          sems[s0:s0 + len(s.sem_shapes)])
            return cps
        return fn

    return _Side(arrays, out_shapes, aliases, sem_shapes, [(part(0), part(1))])


def _side_parts(side, n_lead, n_out):
    if side is None:
        return [], [], [], {}
    return (side.arrays, side.out_shapes, side.sem_shapes, {n_lead + i: n_out + o for i, o in side.aliases.items()})


def _carry(kern, side, n_lead, n_out, n_scratch, step_of, n_steps):
    if side is None:
        return kern
    a = n_lead
    b = a + len(side.arrays)
    c = b + n_out
    d = c + len(side.out_shapes)
    e = d + n_scratch

    def wrapped(*refs):
        side.run(step_of(), n_steps, refs[a:b], refs[c:d], refs[e:])
        kern(*refs[:a], *refs[b:c], *refs[d:e])

    return wrapped


def _run_side(side, name):
    n_in, n_out = len(side.arrays), len(side.out_shapes)

    def body(*refs):
        ins, outs, sems = refs[:n_in], refs[n_in:n_in + n_out], refs[n_in + n_out:]
        for p in range(len(side.phases)):
            side.start(p, ins, outs, sems)
            side.wait(p, ins, outs, sems)

    return pl.pallas_call(body, name=name, out_shape=side.out_shapes, in_specs=[ANY] * n_in, out_specs=[ANY] * n_out,
                          scratch_shapes=side.sem_shapes, input_output_aliases=side.aliases)(*side.arrays)


def _mm_tiles(M, N, K, n_pairs, out_bytes, has_res):
    tm = _pick(M, 1024)
    tks = sorted({_pick(K, t) for t in (8192, 5632, 4096, 2816, 2048, 1408, 1024, 512)}, reverse=True)
    tns = sorted({_pick(N, t) for t in (1536, 1024, 512)}, reverse=True)
    for tk in tks:
        for tn in tns:
            need = 4 * n_pairs * (tm * tk + tk * tn) + tm * tn * (4 * (K > tk) + 2 * out_bytes + 8 * has_res + 4)
            if need <= MM_VMEM_BYTES:
                return tm, tn, tk
    raise ValueError((M, N, K))


def _mm(a, b, mode, out_dtype, name, res=None, alpha=1.0, a2=None, b2=None, b2_k_offset=0, side=None):
    if mode == "nn":
        (M, K), (K2, N) = a.shape, b.shape
    elif mode == "nt":
        (M, K), (N, K2) = a.shape, b.shape
    else:
        (K, M), (K2, N) = a.shape, b.shape
    assert K == K2 or (mode == "nt" and K2 > K), (name, a.shape, b.shape)
    assert a.dtype == BF16 and b.dtype == BF16, name
    tm, tn, tk = _mm_tiles(M, N, K, 1 if a2 is None else 2, jnp.dtype(out_dtype).itemsize, res is not None)
    nk = K // tk
    assert b2_k_offset % tk == 0 and (b2_k_offset == 0 or mode == "nt"), name
    k_off2 = b2_k_offset // tk
    if mode == "nn":
        a_spec = pl.BlockSpec((tm, tk), lambda i, j, k: (i, k))
        b_spec = pl.BlockSpec((tk, tn), lambda i, j, k: (k, j))
        dims = (((1,), (0,)), ((), ()))
    elif mode == "nt":
        a_spec = pl.BlockSpec((tm, tk), lambda i, j, k: (i, k))
        b_spec = pl.BlockSpec((tn, tk), lambda i, j, k: (j, k))
        b2_spec = pl.BlockSpec((tn, tk), lambda i, j, k: (j, k + k_off2))
        dims = (((1,), (1,)), ((), ()))
    else:
        a_spec = pl.BlockSpec((tk, tm), lambda i, j, k: (k, i))
        b_spec = pl.BlockSpec((tk, tn), lambda i, j, k: (k, j))
        dims = (((0,), (0,)), ((), ()))
    o_spec = pl.BlockSpec((tm, tn), lambda i, j, k: (i, j))
    has_res = res is not None
    n_pairs = 1 if a2 is None else 2
    n_main = 2 * n_pairs + int(has_res)
    n_side_in = len(side.arrays) if side else 0
    n_side_out = len(side.out_shapes) if side else 0
    n_acc = 1 if nk > 1 else 0
    gi, gj = M // tm, N // tn
    n_steps = gi * gj * nk

    def kern(*refs):
        r_ref = refs[2 * n_pairs] if has_res else None
        o_ref = refs[n_main + n_side_in]
        if side:
            step = (pl.program_id(0) * gj + pl.program_id(1)) * nk + pl.program_id(2)
            side.run(step, n_steps, refs[n_main:n_main + n_side_in],
                     refs[n_main + n_side_in + 1:n_main + n_side_in + 1 + n_side_out],
                     refs[n_main + n_side_in + 1 + n_side_out + n_acc:])
        part = lax.dot_general(refs[0][...], refs[1][...], dims, preferred_element_type=F32)
        if n_pairs == 2:
            part = part + lax.dot_general(refs[2][...], refs[3][...], dims, preferred_element_type=F32)

        def finish(r):
            if alpha != 1.0:
                r = r * alpha
            if has_res:
                r = r_ref[...] + r
            o_ref[...] = r.astype(o_ref.dtype)

        if nk == 1:
            finish(part)
            return
        acc_ref = refs[n_main + n_side_in + 1 + n_side_out]
        k = pl.program_id(2)

        @pl.when(k == 0)
        def _():
            acc_ref[...] = part

        @pl.when(k > 0)
        def _():
            acc_ref[...] += part

        @pl.when(k == nk - 1)
        def _():
            finish(acc_ref[...])

    ins = (a, b) + ((a2, b2) if n_pairs == 2 else ()) + ((res,) if has_res else ())
    in_specs = [a_spec, b_spec] + ([a_spec, b2_spec if mode == "nt" else b_spec] if n_pairs == 2 else [])
    in_specs += [o_spec] if has_res else []
    out_shape = jax.ShapeDtypeStruct((M, N), out_dtype)
    scratch = [pltpu.VMEM((tm, tn), F32)] if nk > 1 else []
    if not side:
        return pl.pallas_call(kern, name=name, grid=(gi, gj, nk), in_specs=in_specs, out_specs=o_spec,
                              out_shape=out_shape, scratch_shapes=scratch, compiler_params=_params(3))(*ins)
    outs = pl.pallas_call(
        kern, name=name, grid=(gi, gj, nk), in_specs=in_specs + [ANY] * n_side_in,
        out_specs=[o_spec] + [ANY] * n_side_out, out_shape=[out_shape] + list(side.out_shapes),
        scratch_shapes=scratch + list(side.sem_shapes),
        input_output_aliases={n_main + i: 1 + o for i, o in side.aliases.items()},
        compiler_params=_params(3))(*ins, *side.arrays)
    return outs[0], list(outs[1:])


def _vcall(body, grid, ins, in_specs, out_shapes, out_specs, name, n_inner_acc=0, n_acc=0, side=None, carried=None):
    n_body = len(ins)
    if carried is not None:
        ins, in_specs = tuple(ins) + (carried,), list(in_specs) + [ANY]
    n_in, n_out = len(ins), len(out_shapes)
    n_plain = n_out - n_acc - n_inner_acc

    def kern(*refs):
        vals = body(*[r[...] for r in refs[:n_body]])
        if not isinstance(vals, (tuple, list)):
            vals = (vals,)
        out_refs = refs[n_in:]
        inner_first = pl.program_id(len(grid) - 1) == 0
        first = inner_first
        for ax in range(len(grid) - 1):
            first = jnp.logical_and(first, pl.program_id(ax) == 0)
        for idx, (r, v) in enumerate(zip(out_refs, vals)):
            if idx < n_plain:
                r[...] = v.astype(r.dtype)
                continue
            start = inner_first if idx < n_plain + n_inner_acc else first

            @pl.when(start)
            def _(r=r, v=v):
                r[...] = v.astype(r.dtype)

            @pl.when(jnp.logical_not(start))
            def _(r=r, v=v):
                r[...] += v.astype(r.dtype)

    s_in, s_out, s_sems, s_alias = _side_parts(side, n_in, n_out)
    if carried is not None:
        s_alias = {**s_alias, n_body: 0}

    def step_of():
        step = pl.program_id(0)
        for ax in range(1, len(grid)):
            step = step * grid[ax] + pl.program_id(ax)
        return step

    return pl.pallas_call(
        _carry(kern, side, n_in, n_out, 0, step_of, int(np.prod(grid))), name=name, grid=grid,
        in_specs=list(in_specs) + [ANY] * len(s_in), out_specs=list(out_specs) + [ANY] * len(s_out),
        out_shape=list(out_shapes) + list(s_out), scratch_shapes=s_sems, input_output_aliases=s_alias,
        compiler_params=_params(len(grid)))(*ins, *s_in)


def _rows(tm, c):
    return pl.BlockSpec((tm, c), lambda i: (i, 0))


def _vec(c):
    return pl.BlockSpec((1, c), lambda i: (0, 0))


def _sds(shape, dtype):
    return jax.ShapeDtypeStruct(shape, dtype)


def _rstd(x, c):
    return lax.rsqrt(jnp.sum(x * x, axis=-1, keepdims=True) * (1.0 / c) + EPS)


def _rms_bwd(xh, r, g, dn, c):
    u = dn * g
    dx = r * (u - xh * (jnp.sum(xh * u, axis=-1, keepdims=True) * (1.0 / c)))
    return dx, jnp.sum(dn * xh, axis=0, keepdims=True)


def _rope(t, c, sa, sb, half):
    return t * c + pltpu.roll(t, LANES - half, 1) * sa + pltpu.roll(t, half, 1) * sb


def _rope_t(d, c, sa, sb, half):
    return d * c + pltpu.roll(d * sa, half, 1) + pltpu.roll(d * sb, LANES - half, 1)


def _rope_tables(pos_b, rd, name):
    T = pos_b.shape[0]
    half = rd // 2
    inv = ROPE_THETA ** (-jnp.arange(half, dtype=F32) * 2.0 / rd)
    inv_full = jnp.concatenate([inv, inv, jnp.zeros((LANES - rd,), F32)]).reshape(1, LANES)
    lane = np.arange(LANES)
    ma = jnp.asarray((lane < half).astype(np.float32)).reshape(1, LANES)
    mb = jnp.asarray(((lane >= half) & (lane < rd)).astype(np.float32)).reshape(1, LANES)
    tm = _pick(T, 1024, 8)

    def body(pos, invf, a, b):
        ang = pos * invf
        c, s = jnp.cos(ang), jnp.sin(ang)
        inside = a + b
        return c * inside + (1.0 - inside), -s * a, s * b

    return _vcall(body, (T // tm,), (pos_b, inv_full, ma, mb), [_rows(tm, LANES)] + [_vec(LANES)] * 3,
                  [_sds((T, LANES), F32)] * 3, [_rows(tm, LANES)] * 3, name)


def _rms_fwd(x, g, name):
    T, C = x.shape
    tm = _pick(T, 512, 8)

    def body(xv, gv):
        return xv * _rstd(xv, C) * gv

    return _vcall(body, (T // tm,), (x, g), [_rows(tm, C), _vec(C)], [_sds((T, C), BF16)], [_rows(tm, C)], name)[0]


def _rms_bwd_call(x, g, dn, dres, name, side=None, with_bf16=True):
    T, C = x.shape
    tm = _pick(T, 256, 8)

    def body(xv, gv, dnv, drv):
        r = _rstd(xv, C)
        dx, dg = _rms_bwd(xv * r, r, gv, dnv.astype(F32), C)
        dx = drv + dx
        return (dx, dx, dg) if with_bf16 else (dx, dg)

    n_dx = 2 if with_bf16 else 1
    return _vcall(body, (T // tm,), (x, g, dn, dres), [_rows(tm, C), _vec(C), _rows(tm, C), _rows(tm, C)],
                  [_sds((T, C), F32), _sds((T, C), BF16)][:n_dx] + [_sds((1, C), F32)],
                  [_rows(tm, C)] * n_dx + [_vec(C)], name, n_acc=1, side=side)


def _gate_up(n, w_gu, tf, name, side=None):
    T, D = n.shape
    F = w_gu.shape[1] // 2
    tm = _pick(T, 1024, 16)
    nf = F // tf

    def kern(n_ref, wg_ref, wu_ref, a_ref, b_ref, act_ref):
        x = n_ref[...]
        a = jnp.dot(x, wg_ref[...], preferred_element_type=F32)
        b = jnp.dot(x, wu_ref[...], preferred_element_type=F32)
        a_ref[...] = a.astype(BF16)
        b_ref[...] = b.astype(BF16)
        act_ref[...] = (a * _sigmoid(a) * b).astype(BF16)

    tile = pl.BlockSpec((tm, tf), lambda i, j: (i, j))
    s_in, s_out, s_sems, s_alias = _side_parts(side, 3, 3)
    step_of = lambda: pl.program_id(0) * nf + pl.program_id(1)
    outs = pl.pallas_call(
        _carry(kern, side, 3, 3, 0, step_of, (T // tm) * nf), name=name, grid=(T // tm, nf),
        in_specs=[pl.BlockSpec((tm, D), lambda i, j: (i, 0)), pl.BlockSpec((D, tf), lambda i, j: (0, j)),
                  pl.BlockSpec((D, tf), lambda i, j: (0, j + nf))] + [ANY] * len(s_in),
        out_specs=[tile] * 3 + [ANY] * len(s_out), out_shape=[_sds((T, F), BF16)] * 3 + s_out,
        scratch_shapes=s_sems, input_output_aliases=s_alias, compiler_params=_params(2))(n, w_gu, w_gu, *s_in)
    return outs


def _d_gate_up(dout_b, w_d, a, b, tf, name, side=None):
    T, D = dout_b.shape
    F = w_d.shape[0]
    tm = _pick(T, 1024, 16)
    nf = F // tf

    def kern(d_ref, w_ref, a_ref, b_ref, da_ref, db_ref):
        d = 0.5 * lax.dot_general(d_ref[...], w_ref[...], NT, preferred_element_type=F32)
        a, b = a_ref[...].astype(F32), b_ref[...].astype(F32)
        sg = _sigmoid(a)
        da_ref[...] = (d * b * (sg * (1.0 + a * (1.0 - sg)))).astype(BF16)
        db_ref[...] = (d * (a * sg)).astype(BF16)

    tile = pl.BlockSpec((tm, tf), lambda i, j: (i, j))
    s_in, s_out, s_sems, s_alias = _side_parts(side, 4, 2)
    step_of = lambda: pl.program_id(0) * nf + pl.program_id(1)
    outs = pl.pallas_call(
        _carry(kern, side, 4, 2, 0, step_of, (T // tm) * nf), name=name, grid=(T // tm, nf),
        in_specs=[pl.BlockSpec((tm, D), lambda i, j: (i, 0)), pl.BlockSpec((tf, D), lambda i, j: (j, 0)), tile, tile]
        + [ANY] * len(s_in),
        out_specs=[tile] * 2 + [ANY] * len(s_out), out_shape=[_sds((T, F), BF16)] * 2 + s_out,
        scratch_shapes=s_sems, input_output_aliases=s_alias, compiler_params=_params(2))(dout_b, w_d, a, b, *s_in)
    return outs


def _lat_fwd(lat, g_cq, g_ckv, name):
    T, LP = lat.shape
    QL, KVL = g_cq.shape[1], g_ckv.shape[1]
    tm = _pick(T, 512, 8)

    def body(v, gq, gk):
        xq, xk = v[:, :QL], v[:, QL:QL + KVL]
        return xq * _rstd(xq, QL) * gq, xk * _rstd(xk, KVL) * gk

    return _vcall(body, (T // tm,), (lat, g_cq, g_ckv), [_rows(tm, LP), _vec(QL), _vec(KVL)],
                  [_sds((T, QL), BF16), _sds((T, KVL), BF16)], [_rows(tm, QL), _rows(tm, KVL)], name)


def _lat_bwd(dcq, dckv, dkr, lat, g_cq, g_ckv, name):
    T, LP = lat.shape
    QL, KVL = g_cq.shape[1], g_ckv.shape[1]
    tm = _pick(T, 512, 8)

    def body(dq, dk, dr, v, gq, gk):
        xq, xk = v[:, :QL], v[:, QL:QL + KVL]
        rq, rk = _rstd(xq, QL), _rstd(xk, KVL)
        dxq, dgq = _rms_bwd(xq * rq, rq, gq, dq, QL)
        dxk, dgk = _rms_bwd(xk * rk, rk, gk, dk, KVL)
        return jnp.concatenate([dxq, dxk, dr], axis=1), dgq, dgk

    return _vcall(body, (T // tm,), (dcq, dckv, dkr, lat, g_cq, g_ckv),
                  [_rows(tm, QL), _rows(tm, KVL), _rows(tm, LANES), _rows(tm, LP), _vec(QL), _vec(KVL)],
                  [_sds((T, LP), BF16), _sds((1, QL), F32), _sds((1, KVL), F32)],
                  [_rows(tm, LP), _vec(QL), _vec(KVL)], name, n_acc=2)


def _head_spec(tm, w):
    return pl.BlockSpec((tm, w), lambda i, h: (i, h))


def _row2(tm, w, col=0):
    return pl.BlockSpec((tm, w), lambda i, h: (i, col))


def _vec2(w):
    return pl.BlockSpec((1, w), lambda i, h: (0, 0))


def _mla_q_prep(q_raw, g_q, tabs, H, scale, name):
    T = q_raw.shape[0]
    tm = _pick(T, 1024, 8)
    half = MLA_ROPE // 2

    def body(x, g, c, sa, sb):
        n = x * _rstd(x, MLA_QK) * g
        return jnp.concatenate([n[:, :LANES], _rope(n[:, LANES:], c, sa, sb, half)], axis=1) * scale

    return _vcall(body, (T // tm, H), (q_raw, g_q) + tabs,
                  [_head_spec(tm, MLA_QK_PAD), _vec2(MLA_QK_PAD)] + [_row2(tm, LANES)] * 3,
                  [_sds((T, H * MLA_QK_PAD), BF16)], [_head_spec(tm, MLA_QK_PAD)], name)[0]


def _mla_q_bwd(dq, q_raw, g_q, tabs, H, scale, name):
    T = q_raw.shape[0]
    tm = _pick(T, 1024, 8)
    half = MLA_ROPE // 2

    def body(d, x, g, c, sa, sb):
        r = _rstd(x, MLA_QK)
        d = d * scale
        dn = jnp.concatenate([d[:, :LANES], _rope_t(d[:, LANES:], c, sa, sb, half)], axis=1)
        return _rms_bwd(x * r, r, g, dn, MLA_QK)

    return _vcall(body, (T // tm, H), (dq, q_raw, g_q) + tabs,
                  [_head_spec(tm, MLA_QK_PAD), _head_spec(tm, MLA_QK_PAD), _vec2(MLA_QK_PAD)] + [_row2(tm, LANES)] * 3,
                  [_sds((T, H * MLA_QK_PAD), BF16), _sds((1, MLA_QK_PAD), F32)],
                  [_head_spec(tm, MLA_QK_PAD), _vec2(MLA_QK_PAD)], name, n_acc=1)


def _mla_k_prep(kv, lat, kr_col, g_k, tabs, H, name):
    T = kv.shape[0]
    tm = _pick(T, 1024, 8)
    half = MLA_ROPE // 2

    def body(x, kr, g, c, sa, sb):
        kn = x[:, :LANES]
        r = lax.rsqrt((jnp.sum(kn * kn, axis=-1, keepdims=True) + jnp.sum(kr * kr, axis=-1, keepdims=True))
                      * (1.0 / MLA_QK) + EPS)
        k0 = kn * r * g[:, :LANES]
        k1 = _rope(kr * r * g[:, LANES:], c, sa, sb, half)
        return jnp.concatenate([k0, k1], axis=1), x[:, LANES:]

    return _vcall(body, (T // tm, H), (kv, lat, g_k) + tabs,
                  [_head_spec(tm, 2 * LANES), _row2(tm, LANES, kr_col), _vec2(MLA_QK_PAD)] + [_row2(tm, LANES)] * 3,
                  [_sds((T, H * MLA_QK_PAD), BF16), _sds((T, H * MLA_V), BF16)],
                  [_head_spec(tm, MLA_QK_PAD), _head_spec(tm, MLA_V)], name)


def _mla_k_bwd(dk, dv, kv, lat, kr_col, g_k, tabs, H, name):
    T = kv.shape[0]
    tm = _pick(T, 1024, 8)
    half = MLA_ROPE // 2

    def body(d, dvv, x, kr, g, c, sa, sb):
        xx = jnp.concatenate([x[:, :LANES], kr], axis=1)
        r = _rstd(xx, MLA_QK)
        dn = jnp.concatenate([d[:, :LANES], _rope_t(d[:, LANES:], c, sa, sb, half)], axis=1)
        dx, dg = _rms_bwd(xx * r, r, g, dn, MLA_QK)
        return jnp.concatenate([dx[:, :LANES], dvv], axis=1), dx[:, LANES:], dg

    return _vcall(body, (T // tm, H), (dk, dv, kv, lat, g_k) + tabs,
                  [_head_spec(tm, MLA_QK_PAD), _head_spec(tm, MLA_V), _head_spec(tm, 2 * LANES),
                   _row2(tm, LANES, kr_col), _vec2(MLA_QK_PAD)] + [_row2(tm, LANES)] * 3,
                  [_sds((T, H * 2 * LANES), BF16), _sds((T, LANES), F32), _sds((1, MLA_QK_PAD), F32)],
                  [_head_spec(tm, 2 * LANES), _row2(tm, LANES), _vec2(MLA_QK_PAD)], name, n_inner_acc=1, n_acc=1)


def _dil_prep(pd, g_q, g_k, tabs, HD, scale, name):
    T = pd.shape[0]
    W = HD * DIL_HEAD
    G = len(DIL_GROUPS)
    tm = _pick(T, 512, 8)
    half = DIL_ROT // 2

    def body(xq, xk, xv, gq, gk, c, sa, sb):
        outs = []
        for x, g, s in ((xq, gq, scale), (xk, gk, 1.0)):
            heads = []
            for h in range(HD):
                xs = x[:, h * DIL_HEAD:(h + 1) * DIL_HEAD].astype(F32)
                n = _rope(xs * _rstd(xs, DIL_HEAD) * g, c, sa, sb, half)
                heads.append(n * s if s != 1.0 else n)
            outs.append(jnp.concatenate(heads, axis=1))
        return outs[0], outs[1], xv

    gspec = pl.BlockSpec((None, 1, DIL_HEAD), lambda i, g: (g, 0, 0))
    return _vcall(body, (T // tm, G), (pd, pd, pd, g_q, g_k) + tabs,
                  [pl.BlockSpec((tm, W), lambda i, g: (i, 3 * g)), pl.BlockSpec((tm, W), lambda i, g: (i, 3 * g + 1)),
                   pl.BlockSpec((tm, W), lambda i, g: (i, 3 * g + 2)), gspec, gspec] + [_row2(tm, LANES)] * 3,
                  [_sds((T, G * W), F32)] * 3, [pl.BlockSpec((tm, W), lambda i, g: (i, g))] * 3, name)


def _dil_prep_bwd(dq, dk, dv, pd, grp, g_q, g_k, tabs, HD, scale, name, prev=None):
    T = pd.shape[0]
    W = HD * DIL_HEAD
    tm = _pick(T, 256, 8)
    half = DIL_ROT // 2

    def body(dqv, dkv, dvv, xq, xk, gq, gk, c, sa, sb):
        cols, dgs = [], []
        for d, x, g, s in ((dqv, xq, gq, scale), (dkv, xk, gk, 1.0)):
            heads, dg = [], None
            for h in range(HD):
                sl = slice(h * DIL_HEAD, (h + 1) * DIL_HEAD)
                xs = x[:, sl].astype(F32)
                r = _rstd(xs, DIL_HEAD)
                dh = d[:, sl] * s if s != 1.0 else d[:, sl]
                dx, dgh = _rms_bwd(xs * r, r, g, _rope_t(dh, c, sa, sb, half), DIL_HEAD)
                heads.append(dx)
                dg = dgh if dg is None else dg + dgh
            cols.append(jnp.concatenate(heads, axis=1))
            dgs.append(dg)
        return jnp.concatenate(cols + [dvv], axis=1), dgs[0], dgs[1]

    gq, gk = g_q[grp], g_k[grp]
    G = len(DIL_GROUPS)
    return _vcall(body, (T // tm,), (dq, dk, dv, pd, pd, gq, gk) + tabs,
                  [_rows(tm, W)] * 3 + [pl.BlockSpec((tm, W), lambda i: (i, 3 * grp)),
                                        pl.BlockSpec((tm, W), lambda i: (i, 3 * grp + 1)),
                                        _vec(DIL_HEAD), _vec(DIL_HEAD)] + [_rows(tm, LANES)] * 3,
                  [_sds((T, 3 * G * W), BF16), _sds((1, DIL_HEAD), F32), _sds((1, DIL_HEAD), F32)],
                  [pl.BlockSpec((tm, 3 * W), lambda i: (i, grp)), _vec(DIL_HEAD), _vec(DIL_HEAD)], name, n_acc=2,
                  carried=prev)


def _dil_merge(os_, lses, name):
    T, W = os_[0].shape
    tm = _pick(T, 256, 8)

    def body(o0, o1, o2, l0, l1, l2):
        m = jnp.maximum(jnp.maximum(l0, l1), l2)
        w0, w1, w2 = jnp.exp(l0 - m), jnp.exp(l1 - m), jnp.exp(l2 - m)
        z = w0 + w1 + w2
        return (w0 * o0 + w1 * o1 + w2 * o2) / z, m + jnp.log(z)

    return _vcall(body, (T // tm,), tuple(os_) + tuple(lses), [_rows(tm, W)] * 6,
                  [_sds((T, W), BF16), _sds((T, W), F32)], [_rows(tm, W)] * 2, name)


def _gate_merge(pg, bm, bd, name):
    T, D = bm.shape
    tm = _pick(T, 256, 8)

    def body(g, m, d):
        g = g.astype(F32)
        return _sigmoid(g[:, :D]) * m + _sigmoid(g[:, D:]) * d

    return _vcall(body, (T // tm,), (pg, bm, bd), [_rows(tm, 2 * D), _rows(tm, D), _rows(tm, D)],
                  [_sds((T, D), BF16)], [_rows(tm, D)], name)[0]


def _gate_bwd(dmerged, pg, bm, bd, name):
    T, D = bm.shape
    tm = _pick(T, 256, 8)

    def body(dm, g, m, d):
        g = g.astype(F32)
        s0, s1 = _sigmoid(g[:, :D]), _sigmoid(g[:, D:])
        dpg = jnp.concatenate([dm * m * s0 * (1.0 - s0), dm * d * s1 * (1.0 - s1)], axis=1)
        return dm * s0, dm * s1, dpg

    return _vcall(body, (T // tm,), (dmerged, pg, bm, bd), [_rows(tm, D), _rows(tm, 2 * D), _rows(tm, D), _rows(tm, D)],
                  [_sds((T, D), BF16), _sds((T, D), BF16), _sds((T, 2 * D), BF16)],
                  [_rows(tm, D), _rows(tm, D), _rows(tm, 2 * D)], name)


def _ple_loss(x3, zg, pp, target, name):
    T, D = x3.shape
    tm = _pick(T, 256, 8)

    def body(x, z, p_, t):
        s = _sigmoid(z)
        e = x + s * p_ - t
        dy = e * (1.0 / D)
        part = 0.5 * jnp.sum(jnp.sum(e * e, axis=1, keepdims=True), axis=0, keepdims=True) * (1.0 / D)
        return dy, dy * s, dy * p_ * s * (1.0 - s), jnp.broadcast_to(part, (1, LANES))

    return _vcall(body, (T // tm,), (x3, zg, pp, target), [_rows(tm, D)] * 4,
                  [_sds((T, D), F32), _sds((T, D), BF16), _sds((T, D), BF16), _sds((1, LANES), F32)],
                  [_rows(tm, D)] * 3 + [_vec(LANES)], name, n_acc=1)


NT = (((1,), (1,)), ((), ()))
TN = (((0,), (0,)), ((), ()))


def _diag_mask(s):
    row = lax.broadcasted_iota(jnp.int32, s.shape, 0)
    col = lax.broadcasted_iota(jnp.int32, s.shape, 1)
    return jnp.where(col <= row, s, NEG)


def _causal_pairs(nq, key_major):
    if key_major:
        pairs = [(i, j) for j in range(nq) for i in range(j, nq)]
    else:
        pairs = [(i, j) for i in range(nq) for j in range(i + 1)]
    return (jnp.asarray([pr[0] for pr in pairs], jnp.int32), jnp.asarray([pr[1] for pr in pairs], jnp.int32))


def _mla_fwd(q, k, v, H, name, side=None):
    T = q.shape[0]
    tq = _pick(T, 512)
    nq = T // tq
    hb = 2 if H % 2 == 0 else 1
    qi_tab, kj_tab = _causal_pairs(nq, key_major=False)

    def kern(qi_ref, kj_ref, q_ref, k_ref, vt_ref, o_ref, ot_ref, lse_ref, m_sc, l_sc, acc_sc):
        t = pl.program_id(1)
        qi, kj = qi_ref[t], kj_ref[t]

        @pl.when(kj == 0)
        def _():
            m_sc[...] = jnp.full_like(m_sc, NEG)
            l_sc[...] = jnp.zeros_like(l_sc)
            acc_sc[...] = jnp.zeros_like(acc_sc)

        def tile(diagonal):
            for hh in range(hb):
                qs = slice(hh * MLA_QK_PAD, (hh + 1) * MLA_QK_PAD)
                vs = slice(hh * MLA_V, (hh + 1) * MLA_V)
                st = lax.dot_general(k_ref[:, qs], q_ref[:, qs], NT, preferred_element_type=F32)
                if diagonal:
                    key = lax.broadcasted_iota(jnp.int32, st.shape, 0)
                    qry = lax.broadcasted_iota(jnp.int32, st.shape, 1)
                    st = jnp.where(key <= qry, st, NEG)
                m_prev = m_sc[hh]
                m_new = jnp.maximum(m_prev, jnp.max(st, axis=0, keepdims=True))
                alpha = jnp.exp(m_prev - m_new)
                pt = jnp.exp(st - m_new)
                l_new = alpha * l_sc[hh] + jnp.sum(pt, axis=0, keepdims=True)
                acc = alpha * acc_sc[hh] + jnp.dot(vt_ref[vs, :], pt.astype(BF16), preferred_element_type=F32)
                if diagonal:
                    out_t = acc / l_new
                    o_ref[:, vs] = out_t.T.astype(o_ref.dtype)
                    ot_ref[vs, :] = out_t.astype(ot_ref.dtype)
                    lse_ref[hh] = m_new + jnp.log(l_new)
                else:
                    m_sc[hh] = m_new
                    l_sc[hh] = l_new
                    acc_sc[hh] = acc

        @pl.when(kj < qi)
        def _():
            tile(False)

        @pl.when(kj == qi)
        def _():
            tile(True)

    qspec = lambda w: pl.BlockSpec((tq, hb * w), lambda h, t, qi_ref, kj_ref: (qi_ref[t], h))
    kspec = lambda w: pl.BlockSpec((tq, hb * w), lambda h, t, qi_ref, kj_ref: (kj_ref[t], h))
    vt_spec = pl.BlockSpec((hb * MLA_V, tq), lambda h, t, qi_ref, kj_ref: (h, kj_ref[t]))
    ot_spec = pl.BlockSpec((hb * MLA_V, tq), lambda h, t, qi_ref, kj_ref: (h, qi_ref[t]))
    lse_spec = pl.BlockSpec((hb, 1, tq), lambda h, t, qi_ref, kj_ref: (h, 0, qi_ref[t]))
    n_pairs = qi_tab.shape[0]
    s_in, s_out, s_sems, s_alias = _side_parts(side, 5, 3)
    grid_spec = pltpu.PrefetchScalarGridSpec(
        num_scalar_prefetch=2, grid=(H // hb, n_pairs),
        in_specs=[qspec(MLA_QK_PAD), kspec(MLA_QK_PAD), vt_spec] + [ANY] * len(s_in),
        out_specs=[qspec(MLA_V), ot_spec, lse_spec] + [ANY] * len(s_out),
        scratch_shapes=[pltpu.VMEM((hb, 1, tq), F32), pltpu.VMEM((hb, 1, tq), F32),
                        pltpu.VMEM((hb, MLA_V, tq), F32)] + s_sems)
    step_of = lambda: pl.program_id(0) * n_pairs + pl.program_id(1)
    return pl.pallas_call(
        _carry(kern, side, 5, 3, 3, step_of, (H // hb) * n_pairs), name=name, grid_spec=grid_spec,
        out_shape=[_sds((T, H * MLA_V), BF16), _sds((H * MLA_V, T), BF16), _sds((H, 1, T), F32)] + s_out,
        input_output_aliases=s_alias, compiler_params=_params(2))(qi_tab, kj_tab, q, k, v.T, *s_in)


def _mla_bwd(q, k, v, do, do_t, o_t, lse, H, name, side=None):
    T = q.shape[0]
    tq = _pick(T, 512)
    nq = T // tq
    qi_tab, kj_tab = _causal_pairs(nq, key_major=True)

    def kern(qi_ref, kj_ref, q_ref, k_ref, v_ref, do_ref, dot_ref, ot_ref, lse_ref, dq_ref, dk_ref, dv_ref, dk_sc, dv_sc):
        t = pl.program_id(1)
        qi, kj = qi_ref[t], kj_ref[t]
        rows = pl.ds(pl.multiple_of(qi * tq, tq), tq)

        def tile(diagonal):
            st = lax.dot_general(k_ref[...], q_ref[...], NT, preferred_element_type=F32)
            if diagonal:
                key = lax.broadcasted_iota(jnp.int32, st.shape, 0)
                qry = lax.broadcasted_iota(jnp.int32, st.shape, 1)
                st = jnp.where(key <= qry, st, NEG)
            pt = jnp.exp(st - lse_ref[...])
            dl = jnp.sum(dot_ref[...].astype(F32) * ot_ref[...].astype(F32), axis=0, keepdims=True)
            dpt = jnp.dot(v_ref[...], dot_ref[...], preferred_element_type=F32)
            dst = (pt * (dpt - dl)).astype(BF16)
            dv = jnp.dot(pt.astype(BF16), do_ref[...], preferred_element_type=F32)
            dk = jnp.dot(dst, q_ref[...], preferred_element_type=F32)
            dq = lax.dot_general(dst, k_ref[...], TN, preferred_element_type=F32)
            if diagonal:
                dv_sc[...] = dv
                dk_sc[...] = dk
            else:
                dv_sc[...] += dv
                dk_sc[...] += dk

            @pl.when(kj == 0)
            def _():
                dq_ref[rows, :] = dq

            @pl.when(kj > 0)
            def _():
                dq_ref[rows, :] += dq

        @pl.when(qi == kj)
        def _():
            tile(True)

        @pl.when(qi > kj)
        def _():
            tile(False)

        @pl.when(qi == nq - 1)
        def _():
            dk_ref[...] = dk_sc[...]
            dv_ref[...] = dv_sc[...]

    qspec = lambda w: pl.BlockSpec((tq, w), lambda h, t, qi_ref, kj_ref: (qi_ref[t], h))
    kspec = lambda w: pl.BlockSpec((tq, w), lambda h, t, qi_ref, kj_ref: (kj_ref[t], h))
    t_spec = pl.BlockSpec((MLA_V, tq), lambda h, t, qi_ref, kj_ref: (h, qi_ref[t]))
    lse_spec = pl.BlockSpec((None, 1, tq), lambda h, t, qi_ref, kj_ref: (h, 0, qi_ref[t]))
    n_pairs = qi_tab.shape[0]
    s_in, s_out, s_sems, s_alias = _side_parts(side, 9, 3)
    grid_spec = pltpu.PrefetchScalarGridSpec(
        num_scalar_prefetch=2, grid=(H, n_pairs),
        in_specs=[qspec(MLA_QK_PAD), kspec(MLA_QK_PAD), kspec(MLA_V), qspec(MLA_V), t_spec, t_spec, lse_spec]
        + [ANY] * len(s_in),
        out_specs=[pl.BlockSpec((T, MLA_QK_PAD), lambda h, t, qi_ref, kj_ref: (0, h)), kspec(MLA_QK_PAD), kspec(MLA_V)]
        + [ANY] * len(s_out),
        scratch_shapes=[pltpu.VMEM((tq, MLA_QK_PAD), F32), pltpu.VMEM((tq, MLA_V), F32)] + s_sems)
    step_of = lambda: pl.program_id(0) * n_pairs + pl.program_id(1)
    return pl.pallas_call(
        _carry(kern, side, 9, 3, 2, step_of, H * n_pairs), name=name, grid_spec=grid_spec,
        out_shape=[_sds((T, H * MLA_QK_PAD), F32), _sds((T, H * MLA_QK_PAD), F32), _sds((T, H * MLA_V), F32)] + s_out,
        input_output_aliases=s_alias, compiler_params=_params(2))(qi_tab, kj_tab, q, k, v, do, do_t, o_t, lse, *s_in)


class _DilGeometry:
    def __init__(self, T, dil, HD, grp):
        self.dil, self.sub = dil, max(1, 8 // dil)
        self.tb = self.sub * DIL_BLOCK * dil
        assert T % self.tb == 0, (T, dil)
        self.nblk = T // self.tb
        last = self.nblk - 1
        self.cur_g = pl.BlockSpec((self.tb, DIL_HEAD), lambda i, h: (i, grp * HD + h))
        self.prev_g = pl.BlockSpec((self.tb, DIL_HEAD), lambda i, h: (jnp.maximum(i - 1, 0), grp * HD + h))
        self.next_g = pl.BlockSpec((self.tb, DIL_HEAD), lambda i, h: (jnp.minimum(i + 1, last), grp * HD + h))
        self.cur = pl.BlockSpec((self.tb, DIL_HEAD), lambda i, h: (i, h))
        self.next = pl.BlockSpec((self.tb, DIL_HEAD), lambda i, h: (jnp.minimum(i + 1, last), h))

    def rows(self, b, r):
        if self.dil == 1:
            return pl.ds(b * DIL_BLOCK, DIL_BLOCK)
        return pl.ds(b * DIL_BLOCK * self.dil + r, DIL_BLOCK, stride=self.dil)

    def tiles(self):
        return [(b, r) for b in range(self.sub) for r in range(self.dil)]

    def rows2(self, b, r):
        if self.dil == 1:
            return pl.ds(b * DIL_BLOCK, 2 * DIL_BLOCK)
        return pl.ds(b * DIL_BLOCK * self.dil + r, 2 * DIL_BLOCK, stride=self.dil)

    def keys(self, cur_ref, prev_ref, b, r):
        if b > 0:
            return cur_ref[self.rows2(b - 1, r), :].astype(BF16)
        return jnp.concatenate([prev_ref[self.rows(self.sub - 1, r), :], cur_ref[self.rows(0, r), :]],
                               axis=0).astype(BF16)

    def masks(self, i):
        row = lax.broadcasted_iota(jnp.int32, (DIL_BLOCK, 2 * DIL_BLOCK), 0)
        col = lax.broadcasted_iota(jnp.int32, (DIL_BLOCK, 2 * DIL_BLOCK), 1)
        band = (col >= row) & (col <= row + DIL_BLOCK)
        first = band & (col >= jnp.where(i > 0, 0, DIL_BLOCK))
        no_next = jnp.where(i + 1 < self.nblk, 0, 2 * DIL_BLOCK)
        ok_next = col[:, :DIL_BLOCK] >= row[:, :DIL_BLOCK] + no_next
        return band, first, ok_next


def _twice(x):
    return jnp.concatenate([x, x], axis=1)


def _dil_fwd(qd, kd, vd, grp, dil, HD, name):
    T = qd.shape[0]
    W = HD * DIL_HEAD
    geo = _DilGeometry(T, dil, HD, grp)

    def kern(q_ref, kc_ref, kp_ref, vc_ref, vp_ref, o_ref, lse_ref):
        band, first, _ = geo.masks(pl.program_id(0))
        for b, r in geo.tiles():
            R = geo.rows(b, r)
            q = q_ref[R, :].astype(BF16)
            kk, vv = geo.keys(kc_ref, kp_ref, b, r), geo.keys(vc_ref, vp_ref, b, r)
            s = jnp.where(band if b > 0 else first, lax.dot_general(q, kk, NT, preferred_element_type=F32), NEG)
            m = jnp.max(s, axis=1, keepdims=True)
            e = jnp.exp(s - m)
            l = jnp.sum(e, axis=1, keepdims=True)
            o_ref[R, :] = jnp.dot(e.astype(BF16), vv, preferred_element_type=F32) / l
            lse_ref[R, :] = jnp.broadcast_to(m + jnp.log(l), (DIL_BLOCK, DIL_HEAD))

    return pl.pallas_call(
        kern, name=name, grid=(geo.nblk, HD), in_specs=[geo.cur_g, geo.cur_g, geo.prev_g, geo.cur_g, geo.prev_g],
        out_specs=[geo.cur, geo.cur], out_shape=[_sds((T, W), F32)] * 2,
        compiler_params=_params(2))(qd, kd, kd, vd, vd)


def _dil_delta(do, o, HD, name):
    T, W = do.shape
    tm = _pick(T, 512, 8)

    def body(d, ov):
        prod = d * ov.astype(F32)
        return jnp.concatenate(
            [jnp.broadcast_to(jnp.sum(prod[:, h * DIL_HEAD:(h + 1) * DIL_HEAD], axis=1, keepdims=True), (tm, DIL_HEAD))
             for h in range(HD)], axis=1)

    return _vcall(body, (T // tm,), (do, o), [_rows(tm, W)] * 2, [_sds((T, W), F32)], [_rows(tm, W)], name)[0]


def _dil_bwd(qd, kd, vd, do, delta, lse, grp, dil, HD, name):
    T = qd.shape[0]
    W = HD * DIL_HEAD
    geo = _DilGeometry(T, dil, HD, grp)

    def kern(q_ref, k_ref, v_ref, do_ref, dl_ref, ls_ref, kp_ref, vp_ref, qn_ref, don_ref, dln_ref, lsn_ref,
             dq_ref, dk_ref, dv_ref):
        band, first, ok_next = geo.masks(pl.program_id(0))

        def tile(q, do_, dl, ls, k, v, ok):
            s = jnp.where(ok, lax.dot_general(q, k, NT, preferred_element_type=F32), NEG)
            p = jnp.exp(s - ls)
            ds = p * (lax.dot_general(do_, v, NT, preferred_element_type=F32) - dl)
            return ds.astype(BF16), p.astype(BF16)

        dk_ref[...] = jnp.zeros_like(dk_ref)
        dv_ref[...] = jnp.zeros_like(dv_ref)
        for b, r in geo.tiles():
            R = geo.rows(b, r)
            q, do_ = q_ref[R, :].astype(BF16), do_ref[R, :].astype(BF16)
            kk, vv = geo.keys(k_ref, kp_ref, b, r), geo.keys(v_ref, vp_ref, b, r)
            ds, p_ = tile(q, do_, _twice(dl_ref[R, :]), _twice(ls_ref[R, :]), kk, vv, band if b > 0 else first)
            dq_ref[R, :] = jnp.dot(ds, kk, preferred_element_type=F32)
            dkk = lax.dot_general(ds, q, TN, preferred_element_type=F32)
            dvv = lax.dot_general(p_, do_, TN, preferred_element_type=F32)
            if b > 0:
                R2 = geo.rows2(b - 1, r)
                dk_ref[R2, :] += dkk
                dv_ref[R2, :] += dvv
            else:
                dk_ref[R, :] += dkk[DIL_BLOCK:]
                dv_ref[R, :] += dvv[DIL_BLOCK:]
        for r in range(dil):
            R, Rn = geo.rows(geo.sub - 1, r), geo.rows(0, r)
            qn, don = qn_ref[Rn, :].astype(BF16), don_ref[Rn, :].astype(BF16)
            ds, p_ = tile(qn, don, dln_ref[Rn, :], lsn_ref[Rn, :], k_ref[R, :].astype(BF16), v_ref[R, :].astype(BF16),
                          ok_next)
            dk_ref[R, :] += lax.dot_general(ds, qn, TN, preferred_element_type=F32)
            dv_ref[R, :] += lax.dot_general(p_, don, TN, preferred_element_type=F32)

    return pl.pallas_call(
        kern, name=name, grid=(geo.nblk, HD),
        in_specs=[geo.cur_g, geo.cur_g, geo.cur_g, geo.cur, geo.cur, geo.cur, geo.prev_g, geo.prev_g,
                  geo.next_g, geo.next, geo.next, geo.next],
        out_specs=[geo.cur] * 3, out_shape=[_sds((T, W), F32)] * 3,
        compiler_params=_params(2))(qd, kd, vd, do, delta, lse, kd, vd, qd, do, delta, lse)


def _place():
    return lax.axis_index("x"), lax.axis_index("y"), lax.axis_index("c")


def _other_chips(x, y):
    return [(1 - x, y), (x, 1 - y), (1 - x, 1 - y)]


def _kind(name, shard_shape):
    if name in ROW_SHARDED:
        return "row"
    return "col" if shard_shape[1] % LANES == 0 else "stack"


def _remote(src, dst, send_sem, recv_sem, to):
    return pltpu.make_async_remote_copy(src_ref=src, dst_ref=dst, send_sem=send_sem, recv_sem=recv_sem,
                                        device_id=to, device_id_type=MESH_ID)


def _row_tile(rows, cols, itemsize, align):
    return _pick(rows, max(align, (2 * 1024 * 1024) // (cols * itemsize)), align)


def _dma_sems(n):
    return [pltpu.SemaphoreType.DMA((n,)), pltpu.SemaphoreType.DMA((n,))]


def _gather_plan(shard_shapes):
    info, buf_shapes = {}, {}
    for out_name, names in GATHER_PLAN:
        r, c = shard_shapes[names[0]]
        kind = _kind(names[0], (r, c))
        assert kind == "col" or len(names) == 1, out_name
        buf_shapes[out_name] = (r, 4 * c * len(names)) if kind == "col" else (4, r, c)
        for i, n in enumerate(names):
            assert tuple(shard_shapes[n]) == (r, c), n
            info[n] = (out_name, kind, i * 4 * c, r, c)
    return info, buf_shapes


def _place_own(shard, buf_shape, kind, base, me, name, prev=None):
    r, c = shard.shape
    tr = _row_tile(r, c, 2, 16)

    def kern(me_ref, x_ref, *rest):
        rest[-1][...] = x_ref[...]

    if kind == "col":
        out_spec = pl.BlockSpec((tr, c), lambda i, me_ref: (i, base // c + me_ref[0]))
    else:
        out_spec = pl.BlockSpec((None, tr, c), lambda i, me_ref: (me_ref[0], i, 0))
    in_specs = [pl.BlockSpec((tr, c), lambda i, me_ref: (i, 0))] + ([ANY] if prev is not None else [])
    grid_spec = pltpu.PrefetchScalarGridSpec(num_scalar_prefetch=1, grid=(r // tr,), in_specs=in_specs,
                                             out_specs=out_spec)
    args = (me, shard) + ((prev,) if prev is not None else ())
    return pl.pallas_call(kern, name=name, grid_spec=grid_spec, out_shape=_sds(buf_shape, shard.dtype),
                          input_output_aliases={2: 0} if prev is not None else {}, compiler_params=_params(1))(*args)


def _ag_entry(e):
    return e if isinstance(e, tuple) else (e, 0, 1)


def _buffers_of(names, info):
    out_names = []
    for n in [_ag_entry(e)[0] for e in names]:
        if info[n][0] not in out_names:
            out_names.append(info[n][0])
    return out_names


def _ag_side(names, shards, bufs, info):
    entries = [_ag_entry(e) for e in names]
    names = [e[0] for e in entries]
    out_names = _buffers_of(names, info)
    n_w = len(names)

    def rows_of(w, h):
        r = info[names[w]][3]
        _, p, parts = entries[w]
        size = r // (2 * parts)
        return pl.ds(h * (r // 2) + p * size, size)

    def region(outs, w, chip, h):
        out_name, kind, base, r, cc = info[names[w]]
        o = outs[out_names.index(out_name)]
        if kind == "col":
            return o.at[rows_of(w, h), pl.ds(pl.multiple_of(base + chip * cc, LANES), cc)]
        return o.at[chip, rows_of(w, h), :]

    def hop(first, sending):
        def fn(ins, outs, sems):
            x, y, c = _place()
            me, sibling, cps = 2 * x + y, (x, y, 1 - c), []
            for w in range(n_w):
                for j, (px, py) in enumerate(_other_chips(x, y)):
                    k = 3 * w + j + (0 if first else 3 * n_w)
                    if first and sending:
                        src, dst, to = ins[w].at[rows_of(w, c), :], region(outs, w, me, c), (px, py, c)
                    elif first:
                        src = dst = region(outs, w, 2 * px + py, c)
                        to = (px, py, c)
                    else:
                        src = dst = region(outs, w, 2 * px + py, c if sending else 1 - c)
                        to = sibling
                    cps.append(_remote(src, dst, sems[0].at[k], sems[1].at[k], to))
            return cps
        return fn

    return _Side([shards[n] for n in names] + [bufs[o] for o in out_names],
                 [_sds(bufs[o].shape, bufs[o].dtype) for o in out_names], {n_w + i: i for i in range(len(out_names))},
                 _dma_sems(6 * n_w), [(hop(True, True), hop(True, False)), (hop(False, True), hop(False, False))])


def _rs_sibling_side(views):
    n = len(views)

    def fn(sending):
        def copies(ins, outs, sems):
            x, y, c = _place()
            return [_remote(ins[w].at[:, 1 - c] if sending else outs[w], outs[w], sems[0].at[w], sems[1].at[w],
                            (x, y, 1 - c)) for w in range(n)]
        return copies

    return _Side(views, [_sds((v.shape[0],) + v.shape[2:], v.dtype) for v in views], {}, _dma_sems(n),
                 [(fn(True), fn(False))])


def _rs_chips_side(parts, kinds, widths):
    n = len(parts)

    def piece(ins, w, chip):
        if kinds[w] == "col":
            return ins[w].at[0, :, pl.ds(pl.multiple_of(chip * widths[w], LANES), widths[w])]
        return ins[w].at[chip]

    def fn(sending):
        def copies(ins, outs, sems):
            x, y, c = _place()
            cps = []
            for w in range(n):
                for j, (px, py) in enumerate(_other_chips(x, y)):
                    k = 3 * w + j
                    src = piece(ins, w, 2 * px + py) if sending else outs[w].at[j]
                    cps.append(_remote(src, outs[w].at[j], sems[0].at[k], sems[1].at[k], (px, py, c)))
            return cps
        return copies

    return _Side(parts, [_sds((3, p_.shape[1], widths[w]), p_.dtype) for w, p_ in enumerate(parts)], {},
                 _dma_sems(3 * n), [(fn(True), fn(False))])


def _rs_join_side(halves):
    n = len(halves)

    def fn(sending):
        def copies(ins, outs, sems):
            x, y, c = _place()
            return [_remote(ins[w] if sending else outs[w], outs[w], sems[0].at[w], sems[1].at[w], (x, y, 1 - c))
                    for w in range(n)]
        return copies

    return _Side(halves, [_sds(h.shape, h.dtype) for h in halves], {}, _dma_sems(n), [(fn(True), fn(False))])


def _pair_sum(g, got, c_idx, name):
    n, _, rows, C = g.shape
    tr = _row_tile(rows, C, 2, 16)

    def kern(c_ref, a_ref, b_ref, o_ref):
        o_ref[...] = (a_ref[...].astype(F32) + b_ref[...].astype(F32)).astype(o_ref.dtype)

    grid_spec = pltpu.PrefetchScalarGridSpec(
        num_scalar_prefetch=1, grid=(n, rows // tr),
        in_specs=[pl.BlockSpec((None, None, tr, C), lambda j, i, c_ref: (j, c_ref[0], i, 0)),
                  pl.BlockSpec((None, tr, C), lambda j, i, c_ref: (j, i, 0))],
        out_specs=pl.BlockSpec((None, tr, C), lambda j, i, c_ref: (j, i, 0)))
    return pl.pallas_call(kern, name=name, grid_spec=grid_spec, out_shape=_sds((n, rows, C), BF16),
                          compiler_params=_params(2))(c_idx, g, got)


def _sum_pieces(pair, recv, kind, me, name):
    _, rows, c = recv.shape
    tr = _row_tile(rows, c, 8, 16)

    def kern(me_ref, own_ref, r_ref, o_ref):
        acc = own_ref[...].astype(F32)
        for j in range(3):
            acc = acc + r_ref[j].astype(F32)
        o_ref[...] = acc

    if kind == "col":
        own_spec = pl.BlockSpec((None, tr, c), lambda i, me_ref: (0, i, me_ref[0]))
    else:
        own_spec = pl.BlockSpec((None, tr, c), lambda i, me_ref: (me_ref[0], i, 0))
    grid_spec = pltpu.PrefetchScalarGridSpec(
        num_scalar_prefetch=1, grid=(rows // tr,),
        in_specs=[own_spec, pl.BlockSpec((3, tr, c), lambda i, me_ref: (0, i, 0))],
        out_specs=pl.BlockSpec((tr, c), lambda i, me_ref: (i, 0)))
    return pl.pallas_call(kern, name=name, grid_spec=grid_spec, out_shape=_sds((rows, c), F32),
                          compiler_params=_params(1))(me, pair, recv)


def _all_reduce_small(vec):
    N = vec.shape[1]
    n_dev = 8

    def body(v_ref, out_ref, slots, send_sems, recv_sems):
        x, y, c = _place()
        me = 4 * x + 2 * y + c
        slots[me] = v_ref[...]
        sent = []
        for k in range(1, n_dev):
            px, py, pc = x ^ (k >> 2), y ^ ((k >> 1) & 1), c ^ (k & 1)
            cp = pltpu.make_async_remote_copy(src_ref=v_ref, dst_ref=slots.at[me], send_sem=send_sems.at[k - 1],
                                              recv_sem=recv_sems.at[k - 1], device_id=(px, py, pc),
                                              device_id_type=MESH_ID)
            cp.start()
            sent.append(cp)
        for k in range(1, n_dev):
            px, py, pc = x ^ (k >> 2), y ^ ((k >> 1) & 1), c ^ (k & 1)
            slot = slots.at[4 * px + 2 * py + pc]
            pltpu.make_async_remote_copy(src_ref=slot, dst_ref=slot, send_sem=send_sems.at[k - 1],
                                         recv_sem=recv_sems.at[k - 1], device_id=(px, py, pc),
                                         device_id_type=MESH_ID).wait_recv()
        for cp in sent:
            cp.wait_send()
        acc = slots[0]
        for j in range(1, n_dev):
            acc = acc + slots[j]
        out_ref[...] = acc

    vm = pl.BlockSpec(memory_space=pltpu.VMEM)
    return pl.pallas_call(
        body, name="ar_gains", out_shape=_sds((1, N), F32), in_specs=[vm], out_specs=vm,
        scratch_shapes=[pltpu.VMEM((n_dev, 1, N), F32), pltpu.SemaphoreType.DMA((n_dev - 1,)),
                        pltpu.SemaphoreType.DMA((n_dev - 1,))])(vec)


def _adamw_math(wv, gv, mv, vv):
    m2 = ADAM_B1 * mv + (1.0 - ADAM_B1) * gv
    v2 = ADAM_B2 * vv + (1.0 - ADAM_B2) * (gv * gv)
    m_hat = m2 / (1.0 - ADAM_B1 ** ADAM_STEP)
    v_hat = v2 / (1.0 - ADAM_B2 ** ADAM_STEP)
    return -ADAM_LR * (m_hat / (jnp.sqrt(v_hat) + ADAM_EPS) + ADAM_WD * wv), m2, v2


def _adamw(w, g, m, v, name):
    R, C = w.shape
    tr = _row_tile(R, C, 8, 8)
    return _vcall(_adamw_math, (R // tr,), (w, g, m, v), [_rows(tr, C)] * 4, [_sds((R, C), F32)] * 3,
                  [_rows(tr, C)] * 3, name)


def _adamw_halves(w, own, recv, m, v, c_idx, name):
    R, C = w.shape
    rows = R // 2
    tr = _row_tile(rows, C, 8, 8)
    nb = rows // tr

    def kern(c_ref, w_ref, own_ref, recv_ref, m_ref, v_ref, g_out, d_out, m_out, v_out):
        def update(g_ref):
            g = g_ref[...]
            g_out[...] = g
            d_out[...], m_out[...], v_out[...] = _adamw_math(w_ref[...], g, m_ref[...], v_ref[...])

        @pl.when(pl.program_id(0) == c_ref[0])
        def _():
            update(own_ref)

        @pl.when(pl.program_id(0) != c_ref[0])
        def _():
            update(recv_ref)

    full = pl.BlockSpec((tr, C), lambda h, i, c_ref: (h * nb + i, 0))
    own_spec = pl.BlockSpec((tr, C), lambda h, i, c_ref: (jnp.where(h == c_ref[0], i, 0), 0))
    recv_spec = pl.BlockSpec((tr, C), lambda h, i, c_ref: (jnp.where(h == c_ref[0], 0, i), 0))
    grid_spec = pltpu.PrefetchScalarGridSpec(num_scalar_prefetch=1, grid=(2, nb),
                                             in_specs=[full, own_spec, recv_spec, full, full], out_specs=[full] * 4)
    return pl.pallas_call(kern, name=name, grid_spec=grid_spec, out_shape=[_sds((R, C), F32)] * 4,
                          compiler_params=_params(2))(c_idx, w, own, recv, m, v)


def _pad_to(a, n, axis):
    extra = n - a.shape[axis]
    if extra == 0:
        return a
    pads = [(0, 0)] * a.ndim
    pads[axis] = (0, extra)
    return jnp.pad(a, pads)


def _round_up(n, m):
    return -(-n // m) * m


def _natural(buf, kind):
    if kind == "col":
        return buf
    n, r, c = buf.shape
    return buf.reshape(n * r, c) if kind == "row" else buf.transpose(1, 0, 2).reshape(r, n * c)


def _halves_view(g, kind, shard_shape):
    r, c = shard_shape
    if kind == "col":
        return g.reshape(1, 2, r // 2, 4 * c)
    if kind == "stack":
        g = g.reshape(r, 4, c).transpose(1, 0, 2)
    return g.reshape(4, 2, r // 2, c)


def _pack_small(vals):
    return jnp.concatenate([_pad_to(vals[n].reshape(1, -1), _round_up(vals[n].size, LANES), 1) for n in SMALL], axis=1)


def _unpack_small(vec, shapes):
    out, off = {}, 0
    for n in SMALL:
        size = int(np.prod(shapes[n]))
        out[n] = vec[:, off:off + size].reshape(shapes[n])
        off += _round_up(size, LANES)
    return out


def _mm_s(sched, a, b, mode, out_dtype, name, **kw):
    side = sched.side(name)
    if side is None:
        return _mm(a, b, mode, out_dtype, name, **kw)
    out, side_outs = _mm(a, b, mode, out_dtype, name, side=side, **kw)
    sched.done(name, side_outs)
    return out


def _call_s(sched, name, n_out, fn):
    side = sched.side(name)
    outs = fn(side)
    if side is not None:
        sched.done(name, list(outs[n_out:]))
    return outs[:n_out]


def _ffn_fwd(sched, x, g, tf, tag):
    w = tag[-1]
    n = _rms_fwd(x, g, f"{tag}_norm")
    a, b, act = _call_s(sched, f"{tag}_gate_up", 3,
                        lambda side: _gate_up(n, sched.weight(f"w{w}_gu"), tf, f"{tag}_gate_up", side=side))
    out = _mm_s(sched, act, sched.weight(f"w{w}_down"), "nn", F32, f"{tag}_down", res=x, alpha=0.5)
    return out, (n, a, b, act)


def _ffn_bwd(sched, dout, dout_b, x, g, saved, tf, tag, with_bf16):
    w = tag[-1]
    w_gu, w_d = sched.weight(f"w{w}_gu"), sched.weight(f"w{w}_down")
    n, a, b, act = saved
    F = act.shape[1]
    sched.grad(f"w{w}_down", _mm_s(sched, act, dout_b, "tn", BF16, f"{tag}_d_wdown", alpha=0.5))
    da, db = _call_s(sched, f"{tag}_d_act", 2,
                     lambda side: _d_gate_up(dout_b, w_d, a, b, tf, f"{tag}_d_act", side=side))
    sched.grad(f"w{w}_gate", _mm_s(sched, n, da, "tn", BF16, f"{tag}_d_wgate"))
    sched.grad(f"w{w}_up", _mm_s(sched, n, db, "tn", BF16, f"{tag}_d_wup"))
    dn = _mm_s(sched, da, w_gu, "nt", F32, f"{tag}_d_norm", a2=db, b2=w_gu, b2_k_offset=F)
    return _call_s(sched, f"{tag}_d_x", 3 if with_bf16 else 2,
                   lambda side: _rms_bwd_call(x, g, dn, dout, f"{tag}_d_x", side=side, with_bf16=with_bf16))


def _local_step(sched, x, p, pos_b, target, Gn, dims):
    T, D = x.shape
    H, HD, QL, KVL, LP, tf = dims["H"], dims["HD"], dims["QL"], dims["KVL"], dims["LP"], dims["tf"]
    Wd = HD * DIL_HEAD
    scale_mla, scale_dil = MLA_QK ** -0.5, DIL_HEAD ** -0.5
    kr_col = (QL + KVL) // LANES
    tab_mla = tuple(_rope_tables(pos_b, MLA_ROPE, "rope_tab_mla"))
    tab_dil = tuple(_rope_tables(pos_b, DIL_ROT, "rope_tab_dil"))

    W = sched.weight
    mm = functools.partial(_mm_s, sched)

    x1, ffn1 = _ffn_fwd(sched, x, Gn["g_ffn1"], tf, "ffn1")
    h = _rms_fwd(x1, Gn["g_mix"], "mix_norm")
    lat = mm(h, W("w_lat"), "nn", F32, "proj_lat")
    pd = mm(h, W("w_dil"), "nn", BF16, "proj_dil")
    pg = mm(h, W("w_gin"), "nn", BF16, "proj_gate")

    cq, ckv = _lat_fwd(lat, Gn["g_cq"], Gn["g_ckv"], "lat_norm")
    q_raw = mm(cq, W("w_uq"), "nn", F32, "mla_q_up")
    kv = mm(ckv, W("w_ukv"), "nn", F32, "mla_kv_up")
    q = _mla_q_prep(q_raw, Gn["g_q_mla"], tab_mla, H, scale_mla, "mla_q_prep")
    k, v = _mla_k_prep(kv, lat, kr_col, Gn["g_k_mla"], tab_mla, H, "mla_k_prep")
    o_mla, o_mla_t, lse_mla = _call_s(sched, "mla_attn", 3, lambda side: _mla_fwd(q, k, v, H, "mla_attn", side=side))

    qd, kd, vd = _dil_prep(pd, Gn["g_q_dil"], Gn["g_k_dil"], tab_dil, HD, scale_dil, "dil_prep")
    og, lg = [], []
    for grp, (win, dil) in enumerate(DIL_GROUPS):
        o_, l_ = _dil_fwd(qd, kd, vd, grp, dil, HD, f"dil_attn{grp}")
        og.append(o_)
        lg.append(l_)
    o_dil, lse_dil = _dil_merge(og, lg, "dil_merge")

    bm = mm(o_mla, W("w_br_mla"), "nn", F32, "branch_mla")
    bd = mm(o_dil, W("w_br_dil"), "nn", F32, "branch_dil")
    merged = _gate_merge(pg, bm, bd, "gate_merge")
    x2 = mm(merged, W("w_o"), "nn", F32, "out_proj", res=x1)

    x3, ffn2 = _ffn_fwd(sched, x2, Gn["g_ffn2"], tf, "ffn2")
    n4 = _rms_fwd(x3, Gn["g_ple"], "ple_norm")
    zg = mm(n4, W("w_ple_gate"), "nn", F32, "ple_gate")
    p_b = p.astype(BF16)
    pp = mm(p_b, W("w_ple_proj"), "nn", F32, "ple_proj")
    dy, dpp, dzg, loss = _ple_loss(x3, zg, pp, target, "ple_loss")

    gg = {}
    sched.grad("w_ple_proj", mm(p_b, dpp, "tn", BF16, "d_w_ple_proj"))
    sched.grad("w_ple_gate", mm(n4, dzg, "tn", BF16, "d_w_ple_gate"))
    dn4 = mm(dzg, W("w_ple_gate"), "nt", F32, "d_ple_norm")
    dx3, dx3_b, gg["g_ple"] = _rms_bwd_call(x3, Gn["g_ple"], dn4, dy, "d_x3")

    dx2, dx2_b, gg["g_ffn2"] = _ffn_bwd(sched, dx3, dx3_b, x2, Gn["g_ffn2"], ffn2, tf, "ffn2", True)

    sched.grad("w_o", mm(merged, dx2_b, "tn", BF16, "d_w_o"))
    dmerged = mm(dx2_b, W("w_o"), "nt", F32, "d_merged")
    dbm, dbd, dpg = _gate_bwd(dmerged, pg, bm, bd, "d_gate")
    sched.grad("w_br_mla", mm(o_mla, dbm, "tn", BF16, "d_w_br_mla"))
    sched.grad("w_br_dil", mm(o_dil, dbd, "tn", BF16, "d_w_br_dil"))
    do_mla = mm(dbm, W("w_br_mla"), "nt", BF16, "d_o_mla")
    do_dil = mm(dbd, W("w_br_dil"), "nt", F32, "d_o_dil")
    delta_dil = _dil_delta(do_dil, o_dil, HD, "dil_delta")

    dh = mm(dpg, W("w_gin"), "nt", F32, "d_h_gate")
    sched.grad("w_gin", mm(h, dpg, "tn", BF16, "d_w_gin"))
    gq_d, gk_d = Gn["g_q_dil"], Gn["g_k_dil"]
    dgq_d, dgk_d, dpd = [], [], None
    for grp, (win, dil) in enumerate(DIL_GROUPS):
        dq_, dk_, dv_ = _dil_bwd(qd, kd, vd, do_dil, delta_dil, lse_dil, grp, dil, HD, f"dil_bwd{grp}")
        dpd, dgq_, dgk_ = _dil_prep_bwd(dq_, dk_, dv_, pd, grp, gq_d, gk_d, tab_dil, HD, scale_dil,
                                        f"d_dil_prep{grp}", prev=dpd)
        dgq_d.append(dgq_)
        dgk_d.append(dgk_)
    dh = mm(dpd, W("w_dil"), "nt", F32, "d_h_dil", res=dh)
    sched.grad("w_dil", mm(h, dpd, "tn", BF16, "d_w_dil"))
    gg["g_q_dil"] = jnp.concatenate(dgq_d, axis=0)
    gg["g_k_dil"] = jnp.concatenate(dgk_d, axis=0)

    dq, dk, dv = _call_s(sched, "mla_bwd", 3,
                         lambda side: _mla_bwd(q, k, v, do_mla, do_mla.T, o_mla_t, lse_mla, H, "mla_bwd", side=side))
    dq_raw, gg["g_q_mla"] = _mla_q_bwd(dq, q_raw, Gn["g_q_mla"], tab_mla, H, scale_mla, "d_mla_q_prep")
    dkv, dkr, gg["g_k_mla"] = _mla_k_bwd(dk, dv, kv, lat, kr_col, Gn["g_k_mla"], tab_mla, H, "d_mla_k_prep")
    sched.grad("w_uq", mm(cq, dq_raw, "tn", BF16, "d_w_uq"))
    sched.grad("w_ukv", mm(ckv, dkv, "tn", BF16, "d_w_ukv"))
    dcq = mm(dq_raw, W("w_uq"), "nt", F32, "d_cq")
    dckv = mm(dkv, W("w_ukv"), "nt", F32, "d_ckv")
    dlat, gg["g_cq"], gg["g_ckv"] = _lat_bwd(dcq, dckv, dkr, lat, Gn["g_cq"], Gn["g_ckv"], "d_lat_norm")
    dh = mm(dlat, W("w_lat"), "nt", F32, "d_h_lat", res=dh)
    sched.grad("w_lat", mm(h, dlat, "tn", BF16, "d_w_lat"))

    dx1, dx1_b, gg["g_mix"] = _rms_bwd_call(x1, Gn["g_mix"], dh, dx2, "d_x1")
    dx, gg["g_ffn1"] = _ffn_bwd(sched, dx1, dx1_b, x, Gn["g_ffn1"], ffn1, tf, "ffn1", False)
    return loss, dx, gg


def _layout_weight(name, full, dims):
    H, QL, KVL, LP, Wd = dims["H"], dims["QL"], dims["KVL"], dims["LP"], dims["HD"] * DIL_HEAD
    off_dil = QL + KVL + MLA_ROPE
    off_gate = off_dil + 3 * len(DIL_GROUPS) * Wd
    if name == "w_lat":
        return _pad_to(full("w_in")[:, :off_dil], LP, 1)
    if name == "w_dil":
        return full("w_in")[:, off_dil:off_gate]
    if name == "w_gin":
        return full("w_in")[:, off_gate:]
    if name == "w_uq":
        return _pad_to(full("w_uq").reshape(QL, H, MLA_QK), MLA_QK_PAD, 2).reshape(QL, H * MLA_QK_PAD)
    return full(name)


def _natural_grad(name, gw, dims):
    H, QL, KVL = dims["H"], dims["QL"], dims["KVL"]
    if name == "w_in":
        return jnp.concatenate([gw["w_lat"][:, :QL + KVL + MLA_ROPE], gw["w_dil"], gw["w_gin"]], axis=1)
    if name == "w_uq":
        return gw["w_uq"].reshape(QL, H, MLA_QK_PAD)[:, :, :MLA_QK].reshape(QL, H * MLA_QK)
    return gw[name]


WEIGHT_SOURCES = {"w1_gu": ("w1_gate", "w1_up"), "w2_gu": ("w2_gate", "w2_up"), "w_lat": ("w_in",), "w_dil": ("w_in",),
                  "w_gin": ("w_in",)}
AG_FIRST = ("w1_gate", "w1_up")
AG_RIDES = {"ffn1_gate_up": ("w1_down", ("w_in", 0, 2)), "ffn1_down": (("w_in", 1, 2), "w_uq", "w_ukv"),
            "mla_attn": ("w_br_mla", "w_br_dil", "w_o", "w_ple_gate", "w_ple_proj", "w2_gate", "w2_up", "w2_down")}
RS_FFN2 = ("w_ple_proj", "w_ple_gate", "w2_down", "w2_gate", "w2_up")
RS_MIXER = ("w_o", "w_br_mla", "w_br_dil", "w_in", "w_uq", "w_ukv")
RS_RIDES = {
    "d_merged": (("sibling", RS_FFN2),),
    "mla_bwd": (("chips", RS_FFN2),),
    "d_h_lat": (("join", RS_FFN2),),
    "ffn1_d_wdown": (("sibling", RS_MIXER),),
    "ffn1_d_act": (("chips", ("w_o", "w_br_mla", "w_br_dil", "w_uq", "w_ukv")), ("sibling", ("w1_down",))),
    "ffn1_d_wgate": (("chips", ("w1_down",)),),
    "ffn1_d_wup": (("sibling", ("w1_gate",)),),
    "ffn1_d_norm": (("chips", ("w_in", "w1_gate")), ("sibling", ("w1_up",))),
    "ffn1_d_x": (("chips", ("w1_up",)),),
}
RS_LAST = ((("join", RS_MIXER + ("w1_down", "w1_gate", "w1_up")),),)


class _MeshSchedule:
    def __init__(self, w, m, v, dims):
        self.w, self.m, self.v, self.dims = w, m, v, dims
        self.shapes = {n: tuple(w[n].shape[1:]) for n in BIG}
        self.kinds = {n: _kind(n, self.shapes[n]) for n in BIG}
        self.info, buf_shapes = _gather_plan(self.shapes)
        x, y, c = _place()
        self.me = (2 * x + y).astype(jnp.int32).reshape(1)
        self.c_idx = c.astype(jnp.int32).reshape(1)
        self.shards = {n: w[n][0].astype(BF16) for n in BIG}
        self.bufs, self.gathered, self.parts_done, self.layout = {}, set(), {}, {}
        for n in BIG:
            out_name, kind, base, _, _ = self.info[n]
            self.bufs[out_name] = _place_own(self.shards[n], buf_shapes[out_name], kind, base, self.me, f"ag_own_{n}",
                                             prev=self.bufs.get(out_name))
        self.gw, self.views, self.pairs, self.halves, self.recv = {}, {}, {}, {}, {}
        self._ag_done(AG_FIRST, _run_side(self._ag(AG_FIRST), "ag_first"))

    def _ag(self, names):
        return _ag_side(names, self.shards, self.bufs, self.info)

    def _ag_done(self, names, outs):
        for out_name, buf in zip(_buffers_of(names, self.info), outs):
            self.bufs[out_name] = buf
        for name, _, parts in [_ag_entry(e) for e in names]:
            self.parts_done[name] = self.parts_done.get(name, 0) + 1
            if self.parts_done[name] == parts:
                self.gathered.add(name)

    def weight(self, name):
        if name not in self.layout:
            assert all(s in self.gathered for s in WEIGHT_SOURCES.get(name, (name,))), name
            if name in self.bufs and name not in self.info:
                self.layout[name] = self.bufs[name]
            else:
                full = lambda n: _natural(self.bufs[self.info[n][0]], self.info[n][1])
                self.layout[name] = _layout_weight(name, full, self.dims)
        return self.layout[name]

    def grad(self, name, g):
        self.gw[name] = g

    def _rs_side(self, stages):
        sides = []
        for stage, names in stages:
            if stage == "sibling":
                for n in names:
                    self.views[n] = _halves_view(_natural_grad(n, self.gw, self.dims), self.kinds[n], self.shapes[n])
                sides.append(_rs_sibling_side([self.views[n] for n in names]))
            elif stage == "chips":
                sides.append(_rs_chips_side([self.pairs[n] for n in names], [self.kinds[n] for n in names],
                                            [self.shapes[n][1] for n in names]))
            else:
                sides.append(_rs_join_side([self.halves[n] for n in names]))
        return sides[0] if len(sides) == 1 else _merge_sides(sides)

    def _rs_done(self, stages, outs):
        for stage, names in stages:
            got, outs = outs[:len(names)], outs[len(names):]
            for n, a in zip(names, got):
                if stage == "sibling":
                    self.pairs[n] = _pair_sum(self.views[n], a, self.c_idx, f"rs_pair_{n}")
                elif stage == "chips":
                    self.halves[n] = _sum_pieces(self.pairs[n], a, self.kinds[n], self.me, f"rs_sum_{n}")
                else:
                    self.recv[n] = a

    def side(self, tag):
        if tag in AG_RIDES:
            return self._ag(AG_RIDES[tag])
        if tag in RS_RIDES:
            return self._rs_side(RS_RIDES[tag])
        return None

    def done(self, tag, outs):
        if tag in AG_RIDES:
            self._ag_done(AG_RIDES[tag], outs)
        else:
            self._rs_done(RS_RIDES[tag], outs)

    def finish(self):
        for k, stages in enumerate(RS_LAST):
            self._rs_done(stages, _run_side(self._rs_side(stages), f"rs_last{k}"))
        outs = {"grad": {}, "delta": {}, "m": {}, "v": {}}
        for n in BIG:
            res = _adamw_halves(self.w[n][0], self.halves[n], self.recv[n], self.m[n][0], self.v[n][0], self.c_idx,
                                f"adamw_{n}")
            for kind, a in zip(("grad", "delta", "m", "v"), res):
                outs[kind][n] = a.reshape((1,) + a.shape)
        return outs


def _step(x, p, positions, loss_target, w, m, v):
    T, D = x.shape[1], x.shape[2]
    QL, KVL = w["g_cq"].shape[1], w["g_ckv"].shape[1]
    dims = {
        "H": 4 * w["w_uq"].shape[2] // MLA_QK, "HD": w["w_br_dil"].shape[1] // DIL_HEAD, "QL": QL, "KVL": KVL,
        "LP": _round_up(QL + KVL + MLA_ROPE, LANES), "tf": _pick(4 * w["w1_gate"].shape[2], 512),
    }
    small_shapes = {n: w[n].shape for n in SMALL}
    sched = _MeshSchedule(w, m, v, dims)
    Gn = {n: w[n] for n in SMALL}
    Gn["g_q_mla"] = _pad_to(Gn["g_q_mla"], MLA_QK_PAD, 1)
    Gn["g_k_mla"] = _pad_to(Gn["g_k_mla"], MLA_QK_PAD, 1)
    Gn["g_q_dil"] = Gn["g_q_dil"].reshape(len(DIL_GROUPS), 1, DIL_HEAD)
    Gn["g_k_dil"] = Gn["g_k_dil"].reshape(len(DIL_GROUPS), 1, DIL_HEAD)

    pos_b = jnp.broadcast_to(positions.astype(F32).reshape(T, 1), (T, LANES))
    loss, dx, gg = _local_step(sched, x[0], p[0, 0], pos_b, loss_target[0], Gn, dims)
    loss = lax.psum(loss[0, 0], ("x", "y", "c"))
    outs = sched.finish()

    gg["g_q_mla"] = gg["g_q_mla"][:, :MLA_QK]
    gg["g_k_mla"] = gg["g_k_mla"][:, :MLA_QK]
    g_small = _all_reduce_small(_pack_small(gg))
    d_s, m_s, v_s = _adamw(_pack_small({n: w[n] for n in SMALL}), g_small, _pack_small({n: m[n] for n in SMALL}),
                           _pack_small({n: v[n] for n in SMALL}), "adamw_gains")
    for kind, buf in (("grad", g_small), ("delta", d_s), ("m", m_s), ("v", v_s)):
        outs[kind].update(_unpack_small(buf, small_shapes))

    grad_x = dx.reshape(1, T, D)
    return (loss, grad_x, *[outs["grad"][n] for n in WEIGHTS], *[outs["delta"][n] for n in WEIGHTS],
            *[outs["m"][n] for n in WEIGHTS], *[outs["v"][n] for n in WEIGHTS])


def kernel(x, p, positions, g_ffn1, w1_gate, w1_up, w1_down, g_mix, w_in, g_cq, w_uq, g_ckv, w_ukv, g_q_mla, g_k_mla, g_q_dil, g_k_dil, w_br_mla, w_br_dil, w_o, g_ffn2, w2_gate, w2_up, w2_down, g_ple, w_ple_gate, w_ple_proj, loss_target, m_g_ffn1, m_w1_gate, m_w1_up, m_w1_down, m_g_mix, m_w_in, m_g_cq, m_w_uq, m_g_ckv, m_w_ukv, m_g_q_mla, m_g_k_mla, m_g_q_dil, m_g_k_dil, m_w_br_mla, m_w_br_dil, m_w_o, m_g_ffn2, m_w2_gate, m_w2_up, m_w2_down, m_g_ple, m_w_ple_gate, m_w_ple_proj, v_g_ffn1, v_w1_gate, v_w1_up, v_w1_down, v_g_mix, v_w_in, v_g_cq, v_w_uq, v_g_ckv, v_w_ukv, v_g_q_mla, v_g_k_mla, v_g_q_dil, v_g_k_dil, v_w_br_mla, v_w_br_dil, v_w_o, v_g_ffn2, v_w2_gate, v_w2_up, v_w2_down, v_g_ple, v_w_ple_gate, v_w_ple_proj):
    args = locals()
    w = {n: args[n] for n in WEIGHTS}
    m = {n: args["m_" + n] for n in WEIGHTS}
    v = {n: args["v_" + n] for n in WEIGHTS}
    return _step(x, p, positions, loss_target, w, m, v)
```

```python
import functools

import numpy as np
import jax
import jax.numpy as jnp
from jax import lax
from jax.experimental import pallas as pl
from jax.experimental.pallas import tpu as pltpu

F32 = jnp.float32
BF16 = jnp.bfloat16
MESH_ID = pl.DeviceIdType.MESH

MLA_NOPE = 128
MLA_ROPE = 64
MLA_V = 128
MLA_QK = MLA_NOPE + MLA_ROPE
MLA_QK_PAD = 256
DIL_GROUPS = ((128, 1), (512, 4), (2048, 16))
DIL_HEAD = 128
DIL_ROT = DIL_HEAD // 4
DIL_BLOCK = 128
ROPE_THETA = 500000.0
EPS = 1e-6
NEG = -1e30
ADAM_LR = 0.001
ADAM_B1 = 0.9
ADAM_B2 = 0.999
ADAM_EPS = 1e-08
ADAM_WD = 0.01
ADAM_STEP = 10

LANES = 128
VMEM_LIMIT_BYTES = 56 * 1024 * 1024
MM_VMEM_BYTES = 46 * 1024 * 1024

BIG = ("w1_gate", "w1_up", "w1_down", "w_in", "w_uq", "w_ukv", "w_br_mla", "w_br_dil", "w_o",
       "w2_gate", "w2_up", "w2_down", "w_ple_gate", "w_ple_proj")
GATHER_PLAN = (("w1_gu", ("w1_gate", "w1_up")), ("w1_down", ("w1_down",)), ("w_in", ("w_in",)), ("w_uq", ("w_uq",)),
               ("w_ukv", ("w_ukv",)), ("w_br_mla", ("w_br_mla",)), ("w_br_dil", ("w_br_dil",)), ("w_o", ("w_o",)),
               ("w2_gu", ("w2_gate", "w2_up")), ("w2_down", ("w2_down",)), ("w_ple_gate", ("w_ple_gate",)),
               ("w_ple_proj", ("w_ple_proj",)))
ROW_SHARDED = ("w1_down", "w_o", "w2_down", "w_ple_gate")
SMALL = ("g_ffn1", "g_mix", "g_cq", "g_ckv", "g_q_mla", "g_k_mla", "g_q_dil", "g_k_dil", "g_ffn2", "g_ple")
WEIGHTS = ("g_ffn1", "w1_gate", "w1_up", "w1_down", "g_mix", "w_in", "g_cq", "w_uq", "g_ckv", "w_ukv", "g_q_mla",
           "g_k_mla", "g_q_dil", "g_k_dil", "w_br_mla", "w_br_dil", "w_o", "g_ffn2", "w2_gate", "w2_up", "w2_down",
           "g_ple", "w_ple_gate", "w_ple_proj")


def _pick(n, target, align=LANES):
    if n <= target:
        return n
    t = (target // align) * align
    while t >= align:
        if n % t == 0:
            return t
        t -= align
    return n


def _params(n_axes):
    return pltpu.CompilerParams(dimension_semantics=("arbitrary",) * n_axes, vmem_limit_bytes=VMEM_LIMIT_BYTES)


def _sigmoid(x):
    return 0.5 * jnp.tanh(0.5 * x) + 0.5


ANY = pl.BlockSpec(memory_space=pl.ANY)


class _Side:
    def __init__(self, arrays, out_shapes, aliases, sem_shapes, phases):
        self.arrays, self.out_shapes, self.aliases = list(arrays), list(out_shapes), dict(aliases)
        self.sem_shapes, self.phases = list(sem_shapes), list(phases)

    def start(self, p, ins, outs, sems):
        for cp in self.phases[p][0](ins, outs, sems):
            cp.start()

    def wait(self, p, ins, outs, sems):
        for cp in self.phases[p][1](ins, outs, sems):
            cp.wait_recv()
        for cp in self.phases[p][0](ins, outs, sems):
            cp.wait_send()

    def run(self, step, n_steps, ins, outs, sems):
        n_ph = len(self.phases)
        assert n_ph <= 2
        starts = (0, int(0.85 * (n_steps - 1)))
        if n_steps <= n_ph:
            @pl.when(step == n_steps - 1)
            def _():
                for p in range(n_ph):
                    self.start(p, ins, outs, sems)
                    self.wait(p, ins, outs, sems)
            return
        for p in range(n_ph):
            @pl.when(step == starts[p])
            def _(p=p):
                if p > 0:
                    self.wait(p - 1, ins, outs, sems)
                self.start(p, ins, outs, sems)

        @pl.when(step == n_steps - 1)
        def _():
            self.wait(n_ph - 1, ins, outs, sems)


def _merge_sides(sides):
    arrays, out_shapes, aliases, sem_shapes, spans = [], [], {}, [], []
    for s in sides:
        assert len(s.phases) == 1
        spans.append((len(arrays), len(out_shapes), len(sem_shapes), s))
        aliases.update({len(arrays) + i: len(out_shapes) + o for i, o in s.aliases.items()})
        arrays += s.arrays
        out_shapes += s.out_shapes
        sem_shapes += s.sem_shapes

    def part(which):
        def fn(ins, outs, sems):
            cps = []
            for a0, o0, s0, s in spans:
                cps += s.phases[0][which](ins[a0:a0 + len(s.arrays)], outs[o0:o0 + len(s.out_shapes)],
                                          sems[s0:s0 + len(s.sem_shapes)])
            return cps
        return fn

    return _Side(arrays, out_shapes, aliases, sem_shapes, [(part(0), part(1))])


def _side_parts(side, n_lead, n_out):
    if side is None:
        return [], [], [], {}
    return (side.arrays, side.out_shapes, side.sem_shapes, {n_lead + i: n_out + o for i, o in side.aliases.items()})


def _carry(kern, side, n_lead, n_out, n_scratch, step_of, n_steps):
    if side is None:
        return kern
    a = n_lead
    b = a + len(side.arrays)
    c = b + n_out
    d = c + len(side.out_shapes)
    e = d + n_scratch

    def wrapped(*refs):
        side.run(step_of(), n_steps, refs[a:b], refs[c:d], refs[e:])
        kern(*refs[:a], *refs[b:c], *refs[d:e])

    return wrapped


def _run_side(side, name):
    n_in, n_out = len(side.arrays), len(side.out_shapes)

    def body(*refs):
        ins, outs, sems = refs[:n_in], refs[n_in:n_in + n_out], refs[n_in + n_out:]
        for p in range(len(side.phases)):
            side.start(p, ins, outs, sems)
            side.wait(p, ins, outs, sems)

    return pl.pallas_call(body, name=name, out_shape=side.out_shapes, in_specs=[ANY] * n_in, out_specs=[ANY] * n_out,
                          scratch_shapes=side.sem_shapes, input_output_aliases=side.aliases)(*side.arrays)


def _mm_tiles(M, N, K, n_pairs, out_bytes, has_res):
    tm = _pick(M, 1024)
    tks = sorted({_pick(K, t) for t in (8192, 5632, 4096, 2816, 2048, 1408, 1024, 512)}, reverse=True)
    tns = sorted({_pick(N, t) for t in (1536, 1024, 512)}, reverse=True)
    for tk in tks:
        for tn in tns:
            need = 4 * n_pairs * (tm * tk + tk * tn) + tm * tn * (4 * (K > tk) + 2 * out_bytes + 8 * has_res + 4)
            if need <= MM_VMEM_BYTES:
                return tm, tn, tk
    raise ValueError((M, N, K))


def _mm(a, b, mode, out_dtype, name, res=None, alpha=1.0, a2=None, b2=None, b2_k_offset=0, side=None):
    if mode == "nn":
        (M, K), (K2, N) = a.shape, b.shape
    elif mode == "nt":
        (M, K), (N, K2) = a.shape, b.shape
    else:
        (K, M), (K2, N) = a.shape, b.shape
    assert K == K2 or (mode == "nt" and K2 > K), (name, a.shape, b.shape)
    assert a.dtype == BF16 and b.dtype == BF16, name
    tm, tn, tk = _mm_tiles(M, N, K, 1 if a2 is None else 2, jnp.dtype(out_dtype).itemsize, res is not None)
    nk = K // tk
    assert b2_k_offset % tk == 0 and (b2_k_offset == 0 or mode == "nt"), name
    k_off2 = b2_k_offset // tk
    if mode == "nn":
        a_spec = pl.BlockSpec((tm, tk), lambda i, j, k: (i, k))
        b_spec = pl.BlockSpec((tk, tn), lambda i, j, k: (k, j))
        dims = (((1,), (0,)), ((), ()))
    elif mode == "nt":
        a_spec = pl.BlockSpec((tm, tk), lambda i, j, k: (i, k))
        b_spec = pl.BlockSpec((tn, tk), lambda i, j, k: (j, k))
        b2_spec = pl.BlockSpec((tn, tk), lambda i, j, k: (j, k + k_off2))
        dims = (((1,), (1,)), ((), ()))
    else:
        a_spec = pl.BlockSpec((tk, tm), lambda i, j, k: (k, i))
        b_spec = pl.BlockSpec((tk, tn), lambda i, j, k: (k, j))
        dims = (((0,), (0,)), ((), ()))
    o_spec = pl.BlockSpec((tm, tn), lambda i, j, k: (i, j))
    has_res = res is not None
    n_pairs = 1 if a2 is None else 2
    n_main = 2 * n_pairs + int(has_res)
    n_side_in = len(side.arrays) if side else 0
    n_side_out = len(side.out_shapes) if side else 0
    n_acc = 1 if nk > 1 else 0
    gi, gj = M // tm, N // tn
    n_steps = gi * gj * nk

    def kern(*refs):
        r_ref = refs[2 * n_pairs] if has_res else None
        o_ref = refs[n_main + n_side_in]
        if side:
            step = (pl.program_id(0) * gj + pl.program_id(1)) * nk + pl.program_id(2)
            side.run(step, n_steps, refs[n_main:n_main + n_side_in],
                     refs[n_main + n_side_in + 1:n_main + n_side_in + 1 + n_side_out],
                     refs[n_main + n_side_in + 1 + n_side_out + n_acc:])
        part = lax.dot_general(refs[0][...], refs[1][...], dims, preferred_element_type=F32)
        if n_pairs == 2:
            part = part + lax.dot_general(refs[2][...], refs[3][...], dims, preferred_element_type=F32)

        def finish(r):
            if alpha != 1.0:
                r = r * alpha
            if has_res:
                r = r_ref[...] + r
            o_ref[...] = r.astype(o_ref.dtype)

        if nk == 1:
            finish(part)
            return
        acc_ref = refs[n_main + n_side_in + 1 + n_side_out]
        k = pl.program_id(2)

        @pl.when(k == 0)
        def _():
            acc_ref[...] = part

        @pl.when(k > 0)
        def _():
            acc_ref[...] += part

        @pl.when(k == nk - 1)
        def _():
            finish(acc_ref[...])

    ins = (a, b) + ((a2, b2) if n_pairs == 2 else ()) + ((res,) if has_res else ())
    in_specs = [a_spec, b_spec] + ([a_spec, b2_spec if mode == "nt" else b_spec] if n_pairs == 2 else [])
    in_specs += [o_spec] if has_res else []
    out_shape = jax.ShapeDtypeStruct((M, N), out_dtype)
    scratch = [pltpu.VMEM((tm, tn), F32)] if nk > 1 else []
    if not side:
        return pl.pallas_call(kern, name=name, grid=(gi, gj, nk), in_specs=in_specs, out_specs=o_spec,
                              out_shape=out_shape, scratch_shapes=scratch, compiler_params=_params(3))(*ins)
    outs = pl.pallas_call(
        kern, name=name, grid=(gi, gj, nk), in_specs=in_specs + [ANY] * n_side_in,
        out_specs=[o_spec] + [ANY] * n_side_out, out_shape=[out_shape] + list(side.out_shapes),
        scratch_shapes=scratch + list(side.sem_shapes),
        input_output_aliases={n_main + i: 1 + o for i, o in side.aliases.items()},
        compiler_params=_params(3))(*ins, *side.arrays)
    return outs[0], list(outs[1:])


def _vcall(body, grid, ins, in_specs, out_shapes, out_specs, name, n_inner_acc=0, n_acc=0, side=None, carried=None):
    n_body = len(ins)
    if carried is not None:
        ins, in_specs = tuple(ins) + (carried,), list(in_specs) + [ANY]
    n_in, n_out = len(ins), len(out_shapes)
    n_plain = n_out - n_acc - n_inner_acc

    def kern(*refs):
        vals = body(*[r[...] for r in refs[:n_body]])
        if not isinstance(vals, (tuple, list)):
            vals = (vals,)
        out_refs = refs[n_in:]
        inner_first = pl.program_id(len(grid) - 1) == 0
        first = inner_first
        for ax in range(len(grid) - 1):
            first = jnp.logical_and(first, pl.program_id(ax) == 0)
        for idx, (r, v) in enumerate(zip(out_refs, vals)):
            if idx < n_plain:
                r[...] = v.astype(r.dtype)
                continue
            start = inner_first if idx < n_plain + n_inner_acc else first

            @pl.when(start)
            def _(r=r, v=v):
                r[...] = v.astype(r.dtype)

            @pl.when(jnp.logical_not(start))
            def _(r=r, v=v):
                r[...] += v.astype(r.dtype)

    s_in, s_out, s_sems, s_alias = _side_parts(side, n_in, n_out)
    if carried is not None:
        s_alias = {**s_alias, n_body: 0}

    def step_of():
        step = pl.program_id(0)
        for ax in range(1, len(grid)):
            step = step * grid[ax] + pl.program_id(ax)
        return step

    return pl.pallas_call(
        _carry(kern, side, n_in, n_out, 0, step_of, int(np.prod(grid))), name=name, grid=grid,
        in_specs=list(in_specs) + [ANY] * len(s_in), out_specs=list(out_specs) + [ANY] * len(s_out),
        out_shape=list(out_shapes) + list(s_out), scratch_shapes=s_sems, input_output_aliases=s_alias,
        compiler_params=_params(len(grid)))(*ins, *s_in)


def _rows(tm, c):
    return pl.BlockSpec((tm, c), lambda i: (i, 0))


def _vec(c):
    return pl.BlockSpec((1, c), lambda i: (0, 0))


def _sds(shape, dtype):
    return jax.ShapeDtypeStruct(shape, dtype)


def _rstd(x, c):
    return lax.rsqrt(jnp.sum(x * x, axis=-1, keepdims=True) * (1.0 / c) + EPS)


def _rms_bwd(xh, r, g, dn, c):
    u = dn * g
    dx = r * (u - xh * (jnp.sum(xh * u, axis=-1, keepdims=True) * (1.0 / c)))
    return dx, jnp.sum(dn * xh, axis=0, keepdims=True)


def _rope(t, c, sa, sb, half):
    return t * c + pltpu.roll(t, LANES - half, 1) * sa + pltpu.roll(t, half, 1) * sb


def _rope_t(d, c, sa, sb, half):
    return d * c + pltpu.roll(d * sa, half, 1) + pltpu.roll(d * sb, LANES - half, 1)


def _rope_tables(pos_b, rd, name):
    T = pos_b.shape[0]
    half = rd // 2
    inv = ROPE_THETA ** (-jnp.arange(half, dtype=F32) * 2.0 / rd)
    inv_full = jnp.concatenate([inv, inv, jnp.zeros((LANES - rd,), F32)]).reshape(1, LANES)
    lane = np.arange(LANES)
    ma = jnp.asarray((lane < half).astype(np.float32)).reshape(1, LANES)
    mb = jnp.asarray(((lane >= half) & (lane < rd)).astype(np.float32)).reshape(1, LANES)
    tm = _pick(T, 1024, 8)

    def body(pos, invf, a, b):
        ang = pos * invf
        c, s = jnp.cos(ang), jnp.sin(ang)
        inside = a + b
        return c * inside + (1.0 - inside), -s * a, s * b

    return _vcall(body, (T // tm,), (pos_b, inv_full, ma, mb), [_rows(tm, LANES)] + [_vec(LANES)] * 3,
                  [_sds((T, LANES), F32)] * 3, [_rows(tm, LANES)] * 3, name)


def _rms_fwd(x, g, name):
    T, C = x.shape
    tm = _pick(T, 512, 8)

    def body(xv, gv):
        return xv * _rstd(xv, C) * gv

    return _vcall(body, (T // tm,), (x, g), [_rows(tm, C), _vec(C)], [_sds((T, C), BF16)], [_rows(tm, C)], name)[0]


def _rms_bwd_call(x, g, dn, dres, name, side=None, with_bf16=True):
    T, C = x.shape
    tm = _pick(T, 256, 8)

    def body(xv, gv, dnv, drv):
        r = _rstd(xv, C)
        dx, dg = _rms_bwd(xv * r, r, gv, dnv.astype(F32), C)
        dx = drv + dx
        return (dx, dx, dg) if with_bf16 else (dx, dg)

    n_dx = 2 if with_bf16 else 1
    return _vcall(body, (T // tm,), (x, g, dn, dres), [_rows(tm, C), _vec(C), _rows(tm, C), _rows(tm, C)],
                  [_sds((T, C), F32), _sds((T, C), BF16)][:n_dx] + [_sds((1, C), F32)],
                  [_rows(tm, C)] * n_dx + [_vec(C)], name, n_acc=1, side=side)


def _gate_up(n, w_gu, tf, name, side=None):
    T, D = n.shape
    F = w_gu.shape[1] // 2
    tm = _pick(T, 1024, 16)
    nf = F // tf

    def kern(n_ref, wg_ref, wu_ref, a_ref, b_ref, act_ref):
        x = n_ref[...]
        a = jnp.dot(x, wg_ref[...], preferred_element_type=F32)
        b = jnp.dot(x, wu_ref[...], preferred_element_type=F32)
        a_ref[...] = a.astype(BF16)
        b_ref[...] = b.astype(BF16)
        act_ref[...] = (a * _sigmoid(a) * b).astype(BF16)

    tile = pl.BlockSpec((tm, tf), lambda i, j: (i, j))
    s_in, s_out, s_sems, s_alias = _side_parts(side, 3, 3)
    step_of = lambda: pl.program_id(0) * nf + pl.program_id(1)
    outs = pl.pallas_call(
        _carry(kern, side, 3, 3, 0, step_of, (T // tm) * nf), name=name, grid=(T // tm, nf),
        in_specs=[pl.BlockSpec((tm, D), lambda i, j: (i, 0)), pl.BlockSpec((D, tf), lambda i, j: (0, j)),
                  pl.BlockSpec((D, tf), lambda i, j: (0, j + nf))] + [ANY] * len(s_in),
        out_specs=[tile] * 3 + [ANY] * len(s_out), out_shape=[_sds((T, F), BF16)] * 3 + s_out,
        scratch_shapes=s_sems, input_output_aliases=s_alias, compiler_params=_params(2))(n, w_gu, w_gu, *s_in)
    return outs


def _d_gate_up(dout_b, w_d, a, b, tf, name, side=None):
    T, D = dout_b.shape
    F = w_d.shape[0]
    tm = _pick(T, 1024, 16)
    nf = F // tf

    def kern(d_ref, w_ref, a_ref, b_ref, da_ref, db_ref):
        d = 0.5 * lax.dot_general(d_ref[...], w_ref[...], NT, preferred_element_type=F32)
        a, b = a_ref[...].astype(F32), b_ref[...].astype(F32)
        sg = _sigmoid(a)
        da_ref[...] = (d * b * (sg * (1.0 + a * (1.0 - sg)))).astype(BF16)
        db_ref[...] = (d * (a * sg)).astype(BF16)

    tile = pl.BlockSpec((tm, tf), lambda i, j: (i, j))
    s_in, s_out, s_sems, s_alias = _side_parts(side, 4, 2)
    step_of = lambda: pl.program_id(0) * nf + pl.program_id(1)
    outs = pl.pallas_call(
        _carry(kern, side, 4, 2, 0, step_of, (T // tm) * nf), name=name, grid=(T // tm, nf),
        in_specs=[pl.BlockSpec((tm, D), lambda i, j: (i, 0)), pl.BlockSpec((tf, D), lambda i, j: (j, 0)), tile, tile]
        + [ANY] * len(s_in),
        out_specs=[tile] * 2 + [ANY] * len(s_out), out_shape=[_sds((T, F), BF16)] * 2 + s_out,
        scratch_shapes=s_sems, input_output_aliases=s_alias, compiler_params=_params(2))(dout_b, w_d, a, b, *s_in)
    return outs


def _lat_fwd(lat, g_cq, g_ckv, name):
    T, LP = lat.shape
    QL, KVL = g_cq.shape[1], g_ckv.shape[1]
    tm = _pick(T, 512, 8)

    def body(v, gq, gk):
        xq, xk = v[:, :QL], v[:, QL:QL + KVL]
        return xq * _rstd(xq, QL) * gq, xk * _rstd(xk, KVL) * gk

    return _vcall(body, (T // tm,), (lat, g_cq, g_ckv), [_rows(tm, LP), _vec(QL), _vec(KVL)],
                  [_sds((T, QL), BF16), _sds((T, KVL), BF16)], [_rows(tm, QL), _rows(tm, KVL)], name)


def _lat_bwd(dcq, dckv, dkr, lat, g_cq, g_ckv, name):
    T, LP = lat.shape
    QL, KVL = g_cq.shape[1], g_ckv.shape[1]
    tm = _pick(T, 512, 8)

    def body(dq, dk, dr, v, gq, gk):
        xq, xk = v[:, :QL], v[:, QL:QL + KVL]
        rq, rk = _rstd(xq, QL), _rstd(xk, KVL)
        dxq, dgq = _rms_bwd(xq * rq, rq, gq, dq, QL)
        dxk, dgk = _rms_bwd(xk * rk, rk, gk, dk, KVL)
        return jnp.concatenate([dxq, dxk, dr], axis=1), dgq, dgk

    return _vcall(body, (T // tm,), (dcq, dckv, dkr, lat, g_cq, g_ckv),
                  [_rows(tm, QL), _rows(tm, KVL), _rows(tm, LANES), _rows(tm, LP), _vec(QL), _vec(KVL)],
                  [_sds((T, LP), BF16), _sds((1, QL), F32), _sds((1, KVL), F32)],
                  [_rows(tm, LP), _vec(QL), _vec(KVL)], name, n_acc=2)


def _head_spec(tm, w):
    return pl.BlockSpec((tm, w), lambda i, h: (i, h))


def _row2(tm, w, col=0):
    return pl.BlockSpec((tm, w), lambda i, h: (i, col))


def _vec2(w):
    return pl.BlockSpec((1, w), lambda i, h: (0, 0))


def _mla_q_prep(q_raw, g_q, tabs, H, scale, name):
    T = q_raw.shape[0]
    tm = _pick(T, 1024, 8)
    half = MLA_ROPE // 2

    def body(x, g, c, sa, sb):
        n = x * _rstd(x, MLA_QK) * g
        return jnp.concatenate([n[:, :LANES], _rope(n[:, LANES:], c, sa, sb, half)], axis=1) * scale

    return _vcall(body, (T // tm, H), (q_raw, g_q) + tabs,
                  [_head_spec(tm, MLA_QK_PAD), _vec2(MLA_QK_PAD)] + [_row2(tm, LANES)] * 3,
                  [_sds((T, H * MLA_QK_PAD), BF16)], [_head_spec(tm, MLA_QK_PAD)], name)[0]


def _mla_q_bwd(dq, q_raw, g_q, tabs, H, scale, name):
    T = q_raw.shape[0]
    tm = _pick(T, 1024, 8)
    half = MLA_ROPE // 2

    def body(d, x, g, c, sa, sb):
        r = _rstd(x, MLA_QK)
        d = d * scale
        dn = jnp.concatenate([d[:, :LANES], _rope_t(d[:, LANES:], c, sa, sb, half)], axis=1)
        return _rms_bwd(x * r, r, g, dn, MLA_QK)

    return _vcall(body, (T // tm, H), (dq, q_raw, g_q) + tabs,
                  [_head_spec(tm, MLA_QK_PAD), _head_spec(tm, MLA_QK_PAD), _vec2(MLA_QK_PAD)] + [_row2(tm, LANES)] * 3,
                  [_sds((T, H * MLA_QK_PAD), BF16), _sds((1, MLA_QK_PAD), F32)],
                  [_head_spec(tm, MLA_QK_PAD), _vec2(MLA_QK_PAD)], name, n_acc=1)


def _mla_k_prep(kv, lat, kr_col, g_k, tabs, H, name):
    T = kv.shape[0]
    tm = _pick(T, 1024, 8)
    half = MLA_ROPE // 2

    def body(x, kr, g, c, sa, sb):
        kn = x[:, :LANES]
        r = lax.rsqrt((jnp.sum(kn * kn, axis=-1, keepdims=True) + jnp.sum(kr * kr, axis=-1, keepdims=True))
                      * (1.0 / MLA_QK) + EPS)
        k0 = kn * r * g[:, :LANES]
        k1 = _rope(kr * r * g[:, LANES:], c, sa, sb, half)
        return jnp.concatenate([k0, k1], axis=1), x[:, LANES:]

    return _vcall(body, (T // tm, H), (kv, lat, g_k) + tabs,
                  [_head_spec(tm, 2 * LANES), _row2(tm, LANES, kr_col), _vec2(MLA_QK_PAD)] + [_row2(tm, LANES)] * 3,
                  [_sds((T, H * MLA_QK_PAD), BF16), _sds((T, H * MLA_V), BF16)],
                  [_head_spec(tm, MLA_QK_PAD), _head_spec(tm, MLA_V)], name)


def _mla_k_bwd(dk, dv, kv, lat, kr_col, g_k, tabs, H, name):
    T = kv.shape[0]
    tm = _pick(T, 1024, 8)
    half = MLA_ROPE // 2

    def body(d, dvv, x, kr, g, c, sa, sb):
        xx = jnp.concatenate([x[:, :LANES], kr], axis=1)
        r = _rstd(xx, MLA_QK)
        dn = jnp.concatenate([d[:, :LANES], _rope_t(d[:, LANES:], c, sa, sb, half)], axis=1)
        dx, dg = _rms_bwd(xx * r, r, g, dn, MLA_QK)
        return jnp.concatenate([dx[:, :LANES], dvv], axis=1), dx[:, LANES:], dg

    return _vcall(body, (T // tm, H), (dk, dv, kv, lat, g_k) + tabs,
                  [_head_spec(tm, MLA_QK_PAD), _head_spec(tm, MLA_V), _head_spec(tm, 2 * LANES),
                   _row2(tm, LANES, kr_col), _vec2(MLA_QK_PAD)] + [_row2(tm, LANES)] * 3,
                  [_sds((T, H * 2 * LANES), BF16), _sds((T, LANES), F32), _sds((1, MLA_QK_PAD), F32)],
                  [_head_spec(tm, 2 * LANES), _row2(tm, LANES), _vec2(MLA_QK_PAD)], name, n_inner_acc=1, n_acc=1)


def _dil_prep(pd, g_q, g_k, tabs, HD, scale, name):
    T = pd.shape[0]
    W = HD * DIL_HEAD
    G = len(DIL_GROUPS)
    tm = _pick(T, 512, 8)
    half = DIL_ROT // 2

    def body(xq, xk, xv, gq, gk, c, sa, sb):
        outs = []
        for x, g, s in ((xq, gq, scale), (xk, gk, 1.0)):
            heads = []
            for h in range(HD):
                xs = x[:, h * DIL_HEAD:(h + 1) * DIL_HEAD].astype(F32)
                n = _rope(xs * _rstd(xs, DIL_HEAD) * g, c, sa, sb, half)
                heads.append(n * s if s != 1.0 else n)
            outs.append(jnp.concatenate(heads, axis=1))
        return outs[0], outs[1], xv

    gspec = pl.BlockSpec((None, 1, DIL_HEAD), lambda i, g: (g, 0, 0))
    return _vcall(body, (T // tm, G), (pd, pd, pd, g_q, g_k) + tabs,
                  [pl.BlockSpec((tm, W), lambda i, g: (i, 3 * g)), pl.BlockSpec((tm, W), lambda i, g: (i, 3 * g + 1)),
                   pl.BlockSpec((tm, W), lambda i, g: (i, 3 * g + 2)), gspec, gspec] + [_row2(tm, LANES)] * 3,
                  [_sds((T, G * W), F32)] * 3, [pl.BlockSpec((tm, W), lambda i, g: (i, g))] * 3, name)


def _dil_prep_bwd(dq, dk, dv, pd, grp, g_q, g_k, tabs, HD, scale, name, prev=None):
    T = pd.shape[0]
    W = HD * DIL_HEAD
    tm = _pick(T, 256, 8)
    half = DIL_ROT // 2

    def body(dqv, dkv, dvv, xq, xk, gq, gk, c, sa, sb):
        cols, dgs = [], []
        for d, x, g, s in ((dqv, xq, gq, scale), (dkv, xk, gk, 1.0)):
            heads, dg = [], None
            for h in range(HD):
                sl = slice(h * DIL_HEAD, (h + 1) * DIL_HEAD)
                xs = x[:, sl].astype(F32)
                r = _rstd(xs, DIL_HEAD)
                dh = d[:, sl] * s if s != 1.0 else d[:, sl]
                dx, dgh = _rms_bwd(xs * r, r, g, _rope_t(dh, c, sa, sb, half), DIL_HEAD)
                heads.append(dx)
                dg = dgh if dg is None else dg + dgh
            cols.append(jnp.concatenate(heads, axis=1))
            dgs.append(dg)
        return jnp.concatenate(cols + [dvv], axis=1), dgs[0], dgs[1]

    gq, gk = g_q[grp], g_k[grp]
    G = len(DIL_GROUPS)
    return _vcall(body, (T // tm,), (dq, dk, dv, pd, pd, gq, gk) + tabs,
                  [_rows(tm, W)] * 3 + [pl.BlockSpec((tm, W), lambda i: (i, 3 * grp)),
                                        pl.BlockSpec((tm, W), lambda i: (i, 3 * grp + 1)),
                                        _vec(DIL_HEAD), _vec(DIL_HEAD)] + [_rows(tm, LANES)] * 3,
                  [_sds((T, 3 * G * W), BF16), _sds((1, DIL_HEAD), F32), _sds((1, DIL_HEAD), F32)],
                  [pl.BlockSpec((tm, 3 * W), lambda i: (i, grp)), _vec(DIL_HEAD), _vec(DIL_HEAD)], name, n_acc=2,
                  carried=prev)


def _dil_merge(os_, lses, name):
    T, W = os_[0].shape
    tm = _pick(T, 256, 8)

    def body(o0, o1, o2, l0, l1, l2):
        m = jnp.maximum(jnp.maximum(l0, l1), l2)
        w0, w1, w2 = jnp.exp(l0 - m), jnp.exp(l1 - m), jnp.exp(l2 - m)
        z = w0 + w1 + w2
        return (w0 * o0 + w1 * o1 + w2 * o2) / z, m + jnp.log(z)

    return _vcall(body, (T // tm,), tuple(os_) + tuple(lses), [_rows(tm, W)] * 6,
                  [_sds((T, W), BF16), _sds((T, W), F32)], [_rows(tm, W)] * 2, name)


def _gate_merge(pg, bm, bd, name):
    T, D = bm.shape
    tm = _pick(T, 256, 8)

    def body(g, m, d):
        g = g.astype(F32)
        return _sigmoid(g[:, :D]) * m + _sigmoid(g[:, D:]) * d

    return _vcall(body, (T // tm,), (pg, bm, bd), [_rows(tm, 2 * D), _rows(tm, D), _rows(tm, D)],
                  [_sds((T, D), BF16)], [_rows(tm, D)], name)[0]


def _gate_bwd(dmerged, pg, bm, bd, name):
    T, D = bm.shape
    tm = _pick(T, 256, 8)

    def body(dm, g, m, d):
        g = g.astype(F32)
        s0, s1 = _sigmoid(g[:, :D]), _sigmoid(g[:, D:])
        dpg = jnp.concatenate([dm * m * s0 * (1.0 - s0), dm * d * s1 * (1.0 - s1)], axis=1)
        return dm * s0, dm * s1, dpg

    return _vcall(body, (T // tm,), (dmerged, pg, bm, bd), [_rows(tm, D), _rows(tm, 2 * D), _rows(tm, D), _rows(tm, D)],
                  [_sds((T, D), BF16), _sds((T, D), BF16), _sds((T, 2 * D), BF16)],
                  [_rows(tm, D), _rows(tm, D), _rows(tm, 2 * D)], name)


def _ple_loss(x3, zg, pp, target, name):
    T, D = x3.shape
    tm = _pick(T, 256, 8)

    def body(x, z, p_, t):
        s = _sigmoid(z)
        e = x + s * p_ - t
        dy = e * (1.0 / D)
        part = 0.5 * jnp.sum(jnp.sum(e * e, axis=1, keepdims=True), axis=0, keepdims=True) * (1.0 / D)
        return dy, dy * s, dy * p_ * s * (1.0 - s), jnp.broadcast_to(part, (1, LANES))

    return _vcall(body, (T // tm,), (x3, zg, pp, target), [_rows(tm, D)] * 4,
                  [_sds((T, D), F32), _sds((T, D), BF16), _sds((T, D), BF16), _sds((1, LANES), F32)],
                  [_rows(tm, D)] * 3 + [_vec(LANES)], name, n_acc=1)


NT = (((1,), (1,)), ((), ()))
TN = (((0,), (0,)), ((), ()))


def _diag_mask(s):
    row = lax.broadcasted_iota(jnp.int32, s.shape, 0)
    col = lax.broadcasted_iota(jnp.int32, s.shape, 1)
    return jnp.where(col <= row, s, NEG)


def _causal_pairs(nq, key_major, kr=1):
    if key_major:
        pairs = [(i, j) for j in range(nq // kr) for i in range(kr * j, nq)]
    else:
        pairs = [(i, j) for i in range(nq) for j in range(i // kr + 1)]
    return (jnp.asarray([pr[0] for pr in pairs], jnp.int32), jnp.asarray([pr[1] for pr in pairs], jnp.int32))


def _mla_fwd(q, k, v, H, name, side=None):
    T = q.shape[0]
    tq = _pick(T, 512)
    nq = T // tq
    hb = 2 if H % 2 == 0 else 1
    kr = 2 if nq % 2 == 0 else 1
    tk = kr * tq
    qi_tab, kj_tab = _causal_pairs(nq, key_major=False, kr=kr)

    def kern(qi_ref, kj_ref, q_ref, k_ref, vt_ref, o_ref, ot_ref, lse_ref, m_sc, l_sc, acc_sc):
        t = pl.program_id(1)
        qi, kj = qi_ref[t], kj_ref[t]

        @pl.when(kj == 0)
        def _():
            m_sc[...] = jnp.full_like(m_sc, NEG)
            l_sc[...] = jnp.zeros_like(l_sc)
            acc_sc[...] = jnp.zeros_like(acc_sc)

        def tile(diagonal):
            for hh in range(hb):
                qs = slice(hh * MLA_QK_PAD, (hh + 1) * MLA_QK_PAD)
                vs = slice(hh * MLA_V, (hh + 1) * MLA_V)
                st = lax.dot_general(k_ref[:, qs], q_ref[:, qs], NT, preferred_element_type=F32)
                if diagonal:
                    key = lax.broadcasted_iota(jnp.int32, st.shape, 0)
                    qry = lax.broadcasted_iota(jnp.int32, st.shape, 1) + (qi - kr * kj) * tq
                    st = jnp.where(key <= qry, st, NEG)
                m_prev = m_sc[hh]
                m_new = jnp.maximum(m_prev, jnp.max(st, axis=0, keepdims=True))
                alpha = jnp.exp(m_prev - m_new)
                pt = jnp.exp(st - m_new)
                l_new = alpha * l_sc[hh] + jnp.sum(pt, axis=0, keepdims=True)
                acc = alpha * acc_sc[hh] + jnp.dot(vt_ref[vs, :], pt.astype(BF16), preferred_element_type=F32)
                if diagonal:
                    out_t = acc / l_new
                    o_ref[:, vs] = out_t.T.astype(o_ref.dtype)
                    ot_ref[vs, :] = out_t.astype(ot_ref.dtype)
                    lse_ref[hh] = m_new + jnp.log(l_new)
                else:
                    m_sc[hh] = m_new
                    l_sc[hh] = l_new
                    acc_sc[hh] = acc

        @pl.when(kj < qi // kr)
        def _():
            tile(False)

        @pl.when(kj == qi // kr)
        def _():
            tile(True)

    qspec = lambda w: pl.BlockSpec((tq, hb * w), lambda h, t, qi_ref, kj_ref: (qi_ref[t], h))
    kspec = lambda w: pl.BlockSpec((tk, hb * w), lambda h, t, qi_ref, kj_ref: (kj_ref[t], h))
    vt_spec = pl.BlockSpec((hb * MLA_V, tk), lambda h, t, qi_ref, kj_ref: (h, kj_ref[t]))
    ot_spec = pl.BlockSpec((hb * MLA_V, tq), lambda h, t, qi_ref, kj_ref: (h, qi_ref[t]))
    lse_spec = pl.BlockSpec((hb, 1, tq), lambda h, t, qi_ref, kj_ref: (h, 0, qi_ref[t]))
    n_pairs = qi_tab.shape[0]
    s_in, s_out, s_sems, s_alias = _side_parts(side, 5, 3)
    grid_spec = pltpu.PrefetchScalarGridSpec(
        num_scalar_prefetch=2, grid=(H // hb, n_pairs),
        in_specs=[qspec(MLA_QK_PAD), kspec(MLA_QK_PAD), vt_spec] + [ANY] * len(s_in),
        out_specs=[qspec(MLA_V), ot_spec, lse_spec] + [ANY] * len(s_out),
        scratch_shapes=[pltpu.VMEM((hb, 1, tq), F32), pltpu.VMEM((hb, 1, tq), F32),
                        pltpu.VMEM((hb, MLA_V, tq), F32)] + s_sems)
    step_of = lambda: pl.program_id(0) * n_pairs + pl.program_id(1)
    return pl.pallas_call(
        _carry(kern, side, 5, 3, 3, step_of, (H // hb) * n_pairs), name=name, grid_spec=grid_spec,
        out_shape=[_sds((T, H * MLA_V), BF16), _sds((H * MLA_V, T), BF16), _sds((H, 1, T), F32)] + s_out,
        input_output_aliases=s_alias, compiler_params=_params(2))(qi_tab, kj_tab, q, k, v.T, *s_in)


def _mla_bwd(q, k, v, do, do_t, o_t, lse, H, name, side=None):
    T = q.shape[0]
    tq = _pick(T, 512)
    nq = T // tq
    kr = 2 if nq % 2 == 0 else 1
    tk = kr * tq
    qi_tab, kj_tab = _causal_pairs(nq, key_major=True, kr=kr)

    def kern(qi_ref, kj_ref, q_ref, k_ref, v_ref, do_ref, dot_ref, ot_ref, lse_ref, dq_ref, dk_ref, dv_ref, dk_sc, dv_sc):
        t = pl.program_id(1)
        qi, kj = qi_ref[t], kj_ref[t]
        rows = pl.ds(pl.multiple_of(qi * tq, tq), tq)

        def tile(masked, first):
            st = lax.dot_general(k_ref[...], q_ref[...], NT, preferred_element_type=F32)
            if masked:
                key = lax.broadcasted_iota(jnp.int32, st.shape, 0)
                qry = lax.broadcasted_iota(jnp.int32, st.shape, 1) + (qi - kr * kj) * tq
                st = jnp.where(key <= qry, st, NEG)
            pt = jnp.exp(st - lse_ref[...])
            dl = jnp.sum(dot_ref[...].astype(F32) * ot_ref[...].astype(F32), axis=0, keepdims=True)
            dpt = jnp.dot(v_ref[...], dot_ref[...], preferred_element_type=F32)
            dst = (pt * (dpt - dl)).astype(BF16)
            dv = jnp.dot(pt.astype(BF16), do_ref[...], preferred_element_type=F32)
            dk = jnp.dot(dst, q_ref[...], preferred_element_type=F32)
            dq = lax.dot_general(dst, k_ref[...], TN, preferred_element_type=F32)
            if first:
                dv_sc[...] = dv
                dk_sc[...] = dk
            else:
                dv_sc[...] += dv
                dk_sc[...] += dk

            @pl.when(kj == 0)
            def _():
                dq_ref[rows, :] = dq

            @pl.when(kj > 0)
            def _():
                dq_ref[rows, :] += dq

        @pl.when(qi == kr * kj)
        def _():
            tile(True, True)

        if kr > 1:
            @pl.when((qi > kr * kj) & (qi < kr * kj + kr))
            def _():
                tile(True, False)

        @pl.when(qi >= kr * kj + kr)
        def _():
            tile(False, False)

        @pl.when(qi == nq - 1)
        def _():
            dk_ref[...] = dk_sc[...]
            dv_ref[...] = dv_sc[...]

    qspec = lambda w: pl.BlockSpec((tq, w), lambda h, t, qi_ref, kj_ref: (qi_ref[t], h))
    kspec = lambda w: pl.BlockSpec((tk, w), lambda h, t, qi_ref, kj_ref: (kj_ref[t], h))
    t_spec = pl.BlockSpec((MLA_V, tq), lambda h, t, qi_ref, kj_ref: (h, qi_ref[t]))
    lse_spec = pl.BlockSpec((None, 1, tq), lambda h, t, qi_ref, kj_ref: (h, 0, qi_ref[t]))
    n_pairs = qi_tab.shape[0]
    s_in, s_out, s_sems, s_alias = _side_parts(side, 9, 3)
    grid_spec = pltpu.PrefetchScalarGridSpec(
        num_scalar_prefetch=2, grid=(H, n_pairs),
        in_specs=[qspec(MLA_QK_PAD), kspec(MLA_QK_PAD), kspec(MLA_V), qspec(MLA_V), t_spec, t_spec, lse_spec]
        + [ANY] * len(s_in),
        out_specs=[pl.BlockSpec((T, MLA_QK_PAD), lambda h, t, qi_ref, kj_ref: (0, h)), kspec(MLA_QK_PAD), kspec(MLA_V)]
        + [ANY] * len(s_out),
        scratch_shapes=[pltpu.VMEM((tk, MLA_QK_PAD), F32), pltpu.VMEM((tk, MLA_V), F32)] + s_sems)
    step_of = lambda: pl.program_id(0) * n_pairs + pl.program_id(1)
    return pl.pallas_call(
        _carry(kern, side, 9, 3, 2, step_of, H * n_pairs), name=name, grid_spec=grid_spec,
        out_shape=[_sds((T, H * MLA_QK_PAD), F32), _sds((T, H * MLA_QK_PAD), F32), _sds((T, H * MLA_V), F32)] + s_out,
        input_output_aliases=s_alias, compiler_params=_params(2))(qi_tab, kj_tab, q, k, v, do, do_t, o_t, lse, *s_in)


class _DilGeometry:
    def __init__(self, T, dil, HD, grp):
        self.dil, self.sub = dil, max(1, 8 // dil)
        self.tb = self.sub * DIL_BLOCK * dil
        assert T % self.tb == 0, (T, dil)
        self.nblk = T // self.tb
        last = self.nblk - 1
        self.cur_g = pl.BlockSpec((self.tb, DIL_HEAD), lambda i, h: (i, grp * HD + h))
        self.prev_g = pl.BlockSpec((self.tb, DIL_HEAD), lambda i, h: (jnp.maximum(i - 1, 0), grp * HD + h))
        self.next_g = pl.BlockSpec((self.tb, DIL_HEAD), lambda i, h: (jnp.minimum(i + 1, last), grp * HD + h))
        self.cur = pl.BlockSpec((self.tb, DIL_HEAD), lambda i, h: (i, h))
        self.next = pl.BlockSpec((self.tb, DIL_HEAD), lambda i, h: (jnp.minimum(i + 1, last), h))

    def rows(self, b, r):
        if self.dil == 1:
            return pl.ds(b * DIL_BLOCK, DIL_BLOCK)
        return pl.ds(b * DIL_BLOCK * self.dil + r, DIL_BLOCK, stride=self.dil)

    def tiles(self):
        return [(b, r) for b in range(self.sub) for r in range(self.dil)]

    def rows2(self, b, r):
        if self.dil == 1:
            return pl.ds(b * DIL_BLOCK, 2 * DIL_BLOCK)
        return pl.ds(b * DIL_BLOCK * self.dil + r, 2 * DIL_BLOCK, stride=self.dil)

    def keys(self, cur_ref, prev_ref, b, r):
        if b > 0:
            return cur_ref[self.rows2(b - 1, r), :].astype(BF16)
        return jnp.concatenate([prev_ref[self.rows(self.sub - 1, r), :], cur_ref[self.rows(0, r), :]],
                               axis=0).astype(BF16)

    def masks(self, i):
        row = lax.broadcasted_iota(jnp.int32, (DIL_BLOCK, 2 * DIL_BLOCK), 0)
        col = lax.broadcasted_iota(jnp.int32, (DIL_BLOCK, 2 * DIL_BLOCK), 1)
        band = (col >= row) & (col <= row + DIL_BLOCK)
        first = band & (col >= jnp.where(i > 0, 0, DIL_BLOCK))
        no_next = jnp.where(i + 1 < self.nblk, 0, 2 * DIL_BLOCK)
        ok_next = col[:, :DIL_BLOCK] >= row[:, :DIL_BLOCK] + no_next
        return band, first, ok_next


def _twice(x):
    return jnp.concatenate([x, x], axis=1)


def _dil_fwd(qd, kd, vd, grp, dil, HD, name):
    T = qd.shape[0]
    W = HD * DIL_HEAD
    geo = _DilGeometry(T, dil, HD, grp)

    def kern(q_ref, kc_ref, kp_ref, vc_ref, vp_ref, o_ref, lse_ref):
        band, first, _ = geo.masks(pl.program_id(0))
        for b, r in geo.tiles():
            R = geo.rows(b, r)
            q = q_ref[R, :].astype(BF16)
            kk, vv = geo.keys(kc_ref, kp_ref, b, r), geo.keys(vc_ref, vp_ref, b, r)
            s = jnp.where(band if b > 0 else first, lax.dot_general(q, kk, NT, preferred_element_type=F32), NEG)
            m = jnp.max(s, axis=1, keepdims=True)
            e = jnp.exp(s - m)
            l = jnp.sum(e, axis=1, keepdims=True)
            o_ref[R, :] = jnp.dot(e.astype(BF16), vv, preferred_element_type=F32) / l
            lse_ref[R, :] = jnp.broadcast_to(m + jnp.log(l), (DIL_BLOCK, DIL_HEAD))

    return pl.pallas_call(
        kern, name=name, grid=(geo.nblk, HD), in_specs=[geo.cur_g, geo.cur_g, geo.prev_g, geo.cur_g, geo.prev_g],
        out_specs=[geo.cur, geo.cur], out_shape=[_sds((T, W), F32)] * 2,
        compiler_params=_params(2))(qd, kd, kd, vd, vd)


def _dil_delta(do, o, HD, name):
    T, W = do.shape
    tm = _pick(T, 512, 8)

    def body(d, ov):
        prod = d * ov.astype(F32)
        return jnp.concatenate(
            [jnp.broadcast_to(jnp.sum(prod[:, h * DIL_HEAD:(h + 1) * DIL_HEAD], axis=1, keepdims=True), (tm, DIL_HEAD))
             for h in range(HD)], axis=1)

    return _vcall(body, (T // tm,), (do, o), [_rows(tm, W)] * 2, [_sds((T, W), F32)], [_rows(tm, W)], name)[0]


def _dil_bwd(qd, kd, vd, do, delta, lse, grp, dil, HD, name):
    T = qd.shape[0]
    W = HD * DIL_HEAD
    geo = _DilGeometry(T, dil, HD, grp)

    def kern(q_ref, k_ref, v_ref, do_ref, dl_ref, ls_ref, kp_ref, vp_ref, qn_ref, don_ref, dln_ref, lsn_ref,
             dq_ref, dk_ref, dv_ref):
        band, first, ok_next = geo.masks(pl.program_id(0))

        def tile(q, do_, dl, ls, k, v, ok):
            s = jnp.where(ok, lax.dot_general(q, k, NT, preferred_element_type=F32), NEG)
            p = jnp.exp(s - ls)
            ds = p * (lax.dot_general(do_, v, NT, preferred_element_type=F32) - dl)
            return ds.astype(BF16), p.astype(BF16)

        dk_ref[...] = jnp.zeros_like(dk_ref)
        dv_ref[...] = jnp.zeros_like(dv_ref)
        for b, r in geo.tiles():
            R = geo.rows(b, r)
            q, do_ = q_ref[R, :].astype(BF16), do_ref[R, :].astype(BF16)
            kk, vv = geo.keys(k_ref, kp_ref, b, r), geo.keys(v_ref, vp_ref, b, r)
            ds, p_ = tile(q, do_, _twice(dl_ref[R, :]), _twice(ls_ref[R, :]), kk, vv, band if b > 0 else first)
            dq_ref[R, :] = jnp.dot(ds, kk, preferred_element_type=F32)
            dkk = lax.dot_general(ds, q, TN, preferred_element_type=F32)
            dvv = lax.dot_general(p_, do_, TN, preferred_element_type=F32)
            if b > 0:
                R2 = geo.rows2(b - 1, r)
                dk_ref[R2, :] += dkk
                dv_ref[R2, :] += dvv
            else:
                dk_ref[R, :] += dkk[DIL_BLOCK:]
                dv_ref[R, :] += dvv[DIL_BLOCK:]
        for r in range(dil):
            R, Rn = geo.rows(geo.sub - 1, r), geo.rows(0, r)
            qn, don = qn_ref[Rn, :].astype(BF16), don_ref[Rn, :].astype(BF16)
            ds, p_ = tile(qn, don, dln_ref[Rn, :], lsn_ref[Rn, :], k_ref[R, :].astype(BF16), v_ref[R, :].astype(BF16),
                          ok_next)
            dk_ref[R, :] += lax.dot_general(ds, qn, TN, preferred_element_type=F32)
            dv_ref[R, :] += lax.dot_general(p_, don, TN, preferred_element_type=F32)

    return pl.pallas_call(
        kern, name=name, grid=(geo.nblk, HD),
        in_specs=[geo.cur_g, geo.cur_g, geo.cur_g, geo.cur, geo.cur, geo.cur, geo.prev_g, geo.prev_g,
                  geo.next_g, geo.next, geo.next, geo.next],
        out_specs=[geo.cur] * 3, out_shape=[_sds((T, W), F32)] * 3,
        compiler_params=_params(2))(qd, kd, vd, do, delta, lse, kd, vd, qd, do, delta, lse)


def _place():
    return lax.axis_index("x"), lax.axis_index("y"), lax.axis_index("c")


def _other_chips(x, y):
    return [(1 - x, y), (x, 1 - y), (1 - x, 1 - y)]


def _kind(name, shard_shape):
    if name in ROW_SHARDED:
        return "row"
    return "col" if shard_shape[1] % LANES == 0 else "stack"


def _remote(src, dst, send_sem, recv_sem, to):
    return pltpu.make_async_remote_copy(src_ref=src, dst_ref=dst, send_sem=send_sem, recv_sem=recv_sem,
                                        device_id=to, device_id_type=MESH_ID)


def _row_tile(rows, cols, itemsize, align):
    return _pick(rows, max(align, (2 * 1024 * 1024) // (cols * itemsize)), align)


def _dma_sems(n):
    return [pltpu.SemaphoreType.DMA((n,)), pltpu.SemaphoreType.DMA((n,))]


def _gather_plan(shard_shapes):
    info, buf_shapes = {}, {}
    for out_name, names in GATHER_PLAN:
        r, c = shard_shapes[names[0]]
        kind = _kind(names[0], (r, c))
        assert kind == "col" or len(names) == 1, out_name
        buf_shapes[out_name] = (r, 4 * c * len(names)) if kind == "col" else (4, r, c)
        for i, n in enumerate(names):
            assert tuple(shard_shapes[n]) == (r, c), n
            info[n] = (out_name, kind, i * 4 * c, r, c)
    return info, buf_shapes


def _place_own(shard, buf_shape, kind, base, me, name, prev=None):
    r, c = shard.shape
    tr = _row_tile(r, c, 2, 16)

    def kern(me_ref, x_ref, *rest):
        rest[-1][...] = x_ref[...]

    if kind == "col":
        out_spec = pl.BlockSpec((tr, c), lambda i, me_ref: (i, base // c + me_ref[0]))
    else:
        out_spec = pl.BlockSpec((None, tr, c), lambda i, me_ref: (me_ref[0], i, 0))
    in_specs = [pl.BlockSpec((tr, c), lambda i, me_ref: (i, 0))] + ([ANY] if prev is not None else [])
    grid_spec = pltpu.PrefetchScalarGridSpec(num_scalar_prefetch=1, grid=(r // tr,), in_specs=in_specs,
                                             out_specs=out_spec)
    args = (me, shard) + ((prev,) if prev is not None else ())
    return pl.pallas_call(kern, name=name, grid_spec=grid_spec, out_shape=_sds(buf_shape, shard.dtype),
                          input_output_aliases={2: 0} if prev is not None else {}, compiler_params=_params(1))(*args)


def _ag_entry(e):
    return e if isinstance(e, tuple) else (e, 0, 1)


def _buffers_of(names, info):
    out_names = []
    for n in [_ag_entry(e)[0] for e in names]:
        if info[n][0] not in out_names:
            out_names.append(info[n][0])
    return out_names


def _ag_side(names, shards, bufs, info):
    entries = [_ag_entry(e) for e in names]
    names = [e[0] for e in entries]
    out_names = _buffers_of(names, info)
    n_w = len(names)

    def rows_of(w, h):
        r = info[names[w]][3]
        _, p, parts = entries[w]
        size = r // (2 * parts)
        return pl.ds(h * (r // 2) + p * size, size)

    def region(outs, w, chip, h):
        out_name, kind, base, r, cc = info[names[w]]
        o = outs[out_names.index(out_name)]
        if kind == "col":
            return o.at[rows_of(w, h), pl.ds(pl.multiple_of(base + chip * cc, LANES), cc)]
        return o.at[chip, rows_of(w, h), :]

    def hop(first, sending):
        def fn(ins, outs, sems):
            x, y, c = _place()
            me, sibling, cps = 2 * x + y, (x, y, 1 - c), []
            for w in range(n_w):
                for j, (px, py) in enumerate(_other_chips(x, y)):
                    k = 3 * w + j + (0 if first else 3 * n_w)
                    if first and sending:
                        src, dst, to = ins[w].at[rows_of(w, c), :], region(outs, w, me, c), (px, py, c)
                    elif first:
                        src = dst = region(outs, w, 2 * px + py, c)
                        to = (px, py, c)
                    else:
                        src = dst = region(outs, w, 2 * px + py, c if sending else 1 - c)
                        to = sibling
                    cps.append(_remote(src, dst, sems[0].at[k], sems[1].at[k], to))
            return cps
        return fn

    return _Side([shards[n] for n in names] + [bufs[o] for o in out_names],
                 [_sds(bufs[o].shape, bufs[o].dtype) for o in out_names], {n_w + i: i for i in range(len(out_names))},
                 _dma_sems(6 * n_w), [(hop(True, True), hop(True, False)), (hop(False, True), hop(False, False))])


def _rs_sibling_side(views):
    n = len(views)

    def fn(sending):
        def copies(ins, outs, sems):
            x, y, c = _place()
            return [_remote(ins[w].at[:, 1 - c] if sending else outs[w], outs[w], sems[0].at[w], sems[1].at[w],
                            (x, y, 1 - c)) for w in range(n)]
        return copies

    return _Side(views, [_sds((v.shape[0],) + v.shape[2:], v.dtype) for v in views], {}, _dma_sems(n),
                 [(fn(True), fn(False))])


def _rs_chips_side(parts, kinds, widths):
    n = len(parts)

    def piece(ins, w, chip):
        if kinds[w] == "col":
            return ins[w].at[0, :, pl.ds(pl.multiple_of(chip * widths[w], LANES), widths[w])]
        return ins[w].at[chip]

    def fn(sending):
        def copies(ins, outs, sems):
            x, y, c = _place()
            cps = []
            for w in range(n):
                for j, (px, py) in enumerate(_other_chips(x, y)):
                    k = 3 * w + j
                    src = piece(ins, w, 2 * px + py) if sending else outs[w].at[j]
                    cps.append(_remote(src, outs[w].at[j], sems[0].at[k], sems[1].at[k], (px, py, c)))
            return cps
        return copies

    return _Side(parts, [_sds((3, p_.shape[1], widths[w]), p_.dtype) for w, p_ in enumerate(parts)], {},
                 _dma_sems(3 * n), [(fn(True), fn(False))])


def _rs_join_side(halves):
    n = len(halves)

    def fn(sending):
        def copies(ins, outs, sems):
            x, y, c = _place()
            return [_remote(ins[w] if sending else outs[w], outs[w], sems[0].at[w], sems[1].at[w], (x, y, 1 - c))
                    for w in range(n)]
        return copies

    return _Side(halves, [_sds(h.shape, h.dtype) for h in halves], {}, _dma_sems(n), [(fn(True), fn(False))])


def _pair_sum(g, got, c_idx, name):
    n, _, rows, C = g.shape
    tr = _row_tile(rows, C, 2, 16)

    def kern(c_ref, a_ref, b_ref, o_ref):
        o_ref[...] = (a_ref[...].astype(F32) + b_ref[...].astype(F32)).astype(o_ref.dtype)

    grid_spec = pltpu.PrefetchScalarGridSpec(
        num_scalar_prefetch=1, grid=(n, rows // tr),
        in_specs=[pl.BlockSpec((None, None, tr, C), lambda j, i, c_ref: (j, c_ref[0], i, 0)),
                  pl.BlockSpec((None, tr, C), lambda j, i, c_ref: (j, i, 0))],
        out_specs=pl.BlockSpec((None, tr, C), lambda j, i, c_ref: (j, i, 0)))
    return pl.pallas_call(kern, name=name, grid_spec=grid_spec, out_shape=_sds((n, rows, C), BF16),
                          compiler_params=_params(2))(c_idx, g, got)


def _sum_pieces(pair, recv, kind, me, name):
    _, rows, c = recv.shape
    tr = _row_tile(rows, c, 8, 16)

    def kern(me_ref, own_ref, r_ref, o_ref):
        acc = own_ref[...].astype(F32)
        for j in range(3):
            acc = acc + r_ref[j].astype(F32)
        o_ref[...] = acc

    if kind == "col":
        own_spec = pl.BlockSpec((None, tr, c), lambda i, me_ref: (0, i, me_ref[0]))
    else:
        own_spec = pl.BlockSpec((None, tr, c), lambda i, me_ref: (me_ref[0], i, 0))
    grid_spec = pltpu.PrefetchScalarGridSpec(
        num_scalar_prefetch=1, grid=(rows // tr,),
        in_specs=[own_spec, pl.BlockSpec((3, tr, c), lambda i, me_ref: (0, i, 0))],
        out_specs=pl.BlockSpec((tr, c), lambda i, me_ref: (i, 0)))
    return pl.pallas_call(kern, name=name, grid_spec=grid_spec, out_shape=_sds((rows, c), F32),
                          compiler_params=_params(1))(me, pair, recv)


def _all_reduce_small(vec):
    N = vec.shape[1]
    n_dev = 8

    def body(v_ref, out_ref, slots, send_sems, recv_sems):
        x, y, c = _place()
        me = 4 * x + 2 * y + c
        slots[me] = v_ref[...]
        sent = []
        for k in range(1, n_dev):
            px, py, pc = x ^ (k >> 2), y ^ ((k >> 1) & 1), c ^ (k & 1)
            cp = pltpu.make_async_remote_copy(src_ref=v_ref, dst_ref=slots.at[me], send_sem=send_sems.at[k - 1],
                                              recv_sem=recv_sems.at[k - 1], device_id=(px, py, pc),
                                              device_id_type=MESH_ID)
            cp.start()
            sent.append(cp)
        for k in range(1, n_dev):
            px, py, pc = x ^ (k >> 2), y ^ ((k >> 1) & 1), c ^ (k & 1)
            slot = slots.at[4 * px + 2 * py + pc]
            pltpu.make_async_remote_copy(src_ref=slot, dst_ref=slot, send_sem=send_sems.at[k - 1],
                                         recv_sem=recv_sems.at[k - 1], device_id=(px, py, pc),
                                         device_id_type=MESH_ID).wait_recv()
        for cp in sent:
            cp.wait_send()
        acc = slots[0]
        for j in range(1, n_dev):
            acc = acc + slots[j]
        out_ref[...] = acc

    vm = pl.BlockSpec(memory_space=pltpu.VMEM)
    return pl.pallas_call(
        body, name="ar_gains", out_shape=_sds((1, N), F32), in_specs=[vm], out_specs=vm,
        scratch_shapes=[pltpu.VMEM((n_dev, 1, N), F32), pltpu.SemaphoreType.DMA((n_dev - 1,)),
                        pltpu.SemaphoreType.DMA((n_dev - 1,))])(vec)


def _adamw_math(wv, gv, mv, vv):
    m2 = ADAM_B1 * mv + (1.0 - ADAM_B1) * gv
    v2 = ADAM_B2 * vv + (1.0 - ADAM_B2) * (gv * gv)
    m_hat = m2 / (1.0 - ADAM_B1 ** ADAM_STEP)
    v_hat = v2 / (1.0 - ADAM_B2 ** ADAM_STEP)
    return -ADAM_LR * (m_hat / (jnp.sqrt(v_hat) + ADAM_EPS) + ADAM_WD * wv), m2, v2


def _adamw(w, g, m, v, name):
    R, C = w.shape
    tr = _row_tile(R, C, 8, 8)
    return _vcall(_adamw_math, (R // tr,), (w, g, m, v), [_rows(tr, C)] * 4, [_sds((R, C), F32)] * 3,
                  [_rows(tr, C)] * 3, name)


def _adamw_halves(w, own, recv, m, v, c_idx, name):
    R, C = w.shape
    rows = R // 2
    tr = _row_tile(rows, C, 8, 8)
    nb = rows // tr

    def kern(c_ref, w_ref, own_ref, recv_ref, m_ref, v_ref, g_out, d_out, m_out, v_out):
        def update(g_ref):
            g = g_ref[...]
            g_out[...] = g
            d_out[...], m_out[...], v_out[...] = _adamw_math(w_ref[...], g, m_ref[...], v_ref[...])

        @pl.when(pl.program_id(0) == c_ref[0])
        def _():
            update(own_ref)

        @pl.when(pl.program_id(0) != c_ref[0])
        def _():
            update(recv_ref)

    full = pl.BlockSpec((tr, C), lambda h, i, c_ref: (h * nb + i, 0))
    own_spec = pl.BlockSpec((tr, C), lambda h, i, c_ref: (jnp.where(h == c_ref[0], i, 0), 0))
    recv_spec = pl.BlockSpec((tr, C), lambda h, i, c_ref: (jnp.where(h == c_ref[0], 0, i), 0))
    grid_spec = pltpu.PrefetchScalarGridSpec(num_scalar_prefetch=1, grid=(2, nb),
                                             in_specs=[full, own_spec, recv_spec, full, full], out_specs=[full] * 4)
    return pl.pallas_call(kern, name=name, grid_spec=grid_spec, out_shape=[_sds((R, C), F32)] * 4,
                          compiler_params=_params(2))(c_idx, w, own, recv, m, v)


def _pad_to(a, n, axis):
    extra = n - a.shape[axis]
    if extra == 0:
        return a
    pads = [(0, 0)] * a.ndim
    pads[axis] = (0, extra)
    return jnp.pad(a, pads)


def _round_up(n, m):
    return -(-n // m) * m


def _natural(buf, kind):
    if kind == "col":
        return buf
    n, r, c = buf.shape
    return buf.reshape(n * r, c) if kind == "row" else buf.transpose(1, 0, 2).reshape(r, n * c)


def _halves_view(g, kind, shard_shape):
    r, c = shard_shape
    if kind == "col":
        return g.reshape(1, 2, r // 2, 4 * c)
    if kind == "stack":
        g = g.reshape(r, 4, c).transpose(1, 0, 2)
    return g.reshape(4, 2, r // 2, c)


def _pack_small(vals):
    return jnp.concatenate([_pad_to(vals[n].reshape(1, -1), _round_up(vals[n].size, LANES), 1) for n in SMALL], axis=1)


def _unpack_small(vec, shapes):
    out, off = {}, 0
    for n in SMALL:
        size = int(np.prod(shapes[n]))
        out[n] = vec[:, off:off + size].reshape(shapes[n])
        off += _round_up(size, LANES)
    return out


def _mm_s(sched, a, b, mode, out_dtype, name, **kw):
    side = sched.side(name)
    if side is None:
        return _mm(a, b, mode, out_dtype, name, **kw)
    out, side_outs = _mm(a, b, mode, out_dtype, name, side=side, **kw)
    sched.done(name, side_outs)
    return out


def _call_s(sched, name, n_out, fn):
    side = sched.side(name)
    outs = fn(side)
    if side is not None:
        sched.done(name, list(outs[n_out:]))
    return outs[:n_out]


def _ffn_fwd(sched, x, g, tf, tag):
    w = tag[-1]
    n = _rms_fwd(x, g, f"{tag}_norm")
    a, b, act = _call_s(sched, f"{tag}_gate_up", 3,
                        lambda side: _gate_up(n, sched.weight(f"w{w}_gu"), tf, f"{tag}_gate_up", side=side))
    out = _mm_s(sched, act, sched.weight(f"w{w}_down"), "nn", F32, f"{tag}_down", res=x, alpha=0.5)
    return out, (n, a, b, act)


def _ffn_bwd(sched, dout, dout_b, x, g, saved, tf, tag, with_bf16):
    w = tag[-1]
    w_gu, w_d = sched.weight(f"w{w}_gu"), sched.weight(f"w{w}_down")
    n, a, b, act = saved
    F = act.shape[1]
    sched.grad(f"w{w}_down", _mm_s(sched, act, dout_b, "tn", BF16, f"{tag}_d_wdown", alpha=0.5))
    da, db = _call_s(sched, f"{tag}_d_act", 2,
                     lambda side: _d_gate_up(dout_b, w_d, a, b, tf, f"{tag}_d_act", side=side))
    sched.grad(f"w{w}_gate", _mm_s(sched, n, da, "tn", BF16, f"{tag}_d_wgate"))
    sched.grad(f"w{w}_up", _mm_s(sched, n, db, "tn", BF16, f"{tag}_d_wup"))
    dn = _mm_s(sched, da, w_gu, "nt", F32, f"{tag}_d_norm", a2=db, b2=w_gu, b2_k_offset=F)
    return _call_s(sched, f"{tag}_d_x", 3 if with_bf16 else 2,
                   lambda side: _rms_bwd_call(x, g, dn, dout, f"{tag}_d_x", side=side, with_bf16=with_bf16))


def _local_step(sched, x, p, pos_b, target, Gn, dims):
    T, D = x.shape
    H, HD, QL, KVL, LP, tf = dims["H"], dims["HD"], dims["QL"], dims["KVL"], dims["LP"], dims["tf"]
    Wd = HD * DIL_HEAD
    scale_mla, scale_dil = MLA_QK ** -0.5, DIL_HEAD ** -0.5
    kr_col = (QL + KVL) // LANES
    tab_mla = tuple(_rope_tables(pos_b, MLA_ROPE, "rope_tab_mla"))
    tab_dil = tuple(_rope_tables(pos_b, DIL_ROT, "rope_tab_dil"))

    W = sched.weight
    mm = functools.partial(_mm_s, sched)

    x1, ffn1 = _ffn_fwd(sched, x, Gn["g_ffn1"], tf, "ffn1")
    h = _rms_fwd(x1, Gn["g_mix"], "mix_norm")
    lat = mm(h, W("w_lat"), "nn", F32, "proj_lat")
    pd = mm(h, W("w_dil"), "nn", BF16, "proj_dil")
    pg = mm(h, W("w_gin"), "nn", BF16, "proj_gate")

    cq, ckv = _lat_fwd(lat, Gn["g_cq"], Gn["g_ckv"], "lat_norm")
    q_raw = mm(cq, W("w_uq"), "nn", F32, "mla_q_up")
    kv = mm(ckv, W("w_ukv"), "nn", F32, "mla_kv_up")
    q = _mla_q_prep(q_raw, Gn["g_q_mla"], tab_mla, H, scale_mla, "mla_q_prep")
    k, v = _mla_k_prep(kv, lat, kr_col, Gn["g_k_mla"], tab_mla, H, "mla_k_prep")
    o_mla, o_mla_t, lse_mla = _call_s(sched, "mla_attn", 3, lambda side: _mla_fwd(q, k, v, H, "mla_attn", side=side))

    qd, kd, vd = _dil_prep(pd, Gn["g_q_dil"], Gn["g_k_dil"], tab_dil, HD, scale_dil, "dil_prep")
    og, lg = [], []
    for grp, (win, dil) in enumerate(DIL_GROUPS):
        o_, l_ = _dil_fwd(qd, kd, vd, grp, dil, HD, f"dil_attn{grp}")
        og.append(o_)
        lg.append(l_)
    o_dil, lse_dil = _dil_merge(og, lg, "dil_merge")

    bm = mm(o_mla, W("w_br_mla"), "nn", F32, "branch_mla")
    bd = mm(o_dil, W("w_br_dil"), "nn", F32, "branch_dil")
    merged = _gate_merge(pg, bm, bd, "gate_merge")
    x2 = mm(merged, W("w_o"), "nn", F32, "out_proj", res=x1)

    x3, ffn2 = _ffn_fwd(sched, x2, Gn["g_ffn2"], tf, "ffn2")
    n4 = _rms_fwd(x3, Gn["g_ple"], "ple_norm")
    zg = mm(n4, W("w_ple_gate"), "nn", F32, "ple_gate")
    p_b = p.astype(BF16)
    pp = mm(p_b, W("w_ple_proj"), "nn", F32, "ple_proj")
    dy, dpp, dzg, loss = _ple_loss(x3, zg, pp, target, "ple_loss")

    gg = {}
    sched.grad("w_ple_proj", mm(p_b, dpp, "tn", BF16, "d_w_ple_proj"))
    sched.grad("w_ple_gate", mm(n4, dzg, "tn", BF16, "d_w_ple_gate"))
    dn4 = mm(dzg, W("w_ple_gate"), "nt", F32, "d_ple_norm")
    dx3, dx3_b, gg["g_ple"] = _rms_bwd_call(x3, Gn["g_ple"], dn4, dy, "d_x3")

    dx2, dx2_b, gg["g_ffn2"] = _ffn_bwd(sched, dx3, dx3_b, x2, Gn["g_ffn2"], ffn2, tf, "ffn2", True)

    sched.grad("w_o", mm(merged, dx2_b, "tn", BF16, "d_w_o"))
    dmerged = mm(dx2_b, W("w_o"), "nt", F32, "d_merged")
    dbm, dbd, dpg = _gate_bwd(dmerged, pg, bm, bd, "d_gate")
    sched.grad("w_br_mla", mm(o_mla, dbm, "tn", BF16, "d_w_br_mla"))
    sched.grad("w_br_dil", mm(o_dil, dbd, "tn", BF16, "d_w_br_dil"))
    do_mla = mm(dbm, W("w_br_mla"), "nt", BF16, "d_o_mla")
    do_dil = mm(dbd, W("w_br_dil"), "nt", F32, "d_o_dil")
    delta_dil = _dil_delta(do_dil, o_dil, HD, "dil_delta")

    dh = mm(dpg, W("w_gin"), "nt", F32, "d_h_gate")
    sched.grad("w_gin", mm(h, dpg, "tn", BF16, "d_w_gin"))
    gq_d, gk_d = Gn["g_q_dil"], Gn["g_k_dil"]
    dgq_d, dgk_d, dpd = [], [], None
    for grp, (win, dil) in enumerate(DIL_GROUPS):
        dq_, dk_, dv_ = _dil_bwd(qd, kd, vd, do_dil, delta_dil, lse_dil, grp, dil, HD, f"dil_bwd{grp}")
        dpd, dgq_, dgk_ = _dil_prep_bwd(dq_, dk_, dv_, pd, grp, gq_d, gk_d, tab_dil, HD, scale_dil,
                                        f"d_dil_prep{grp}", prev=dpd)
        dgq_d.append(dgq_)
        dgk_d.append(dgk_)
    dh = mm(dpd, W("w_dil"), "nt", F32, "d_h_dil", res=dh)
    sched.grad("w_dil", mm(h, dpd, "tn", BF16, "d_w_dil"))
    gg["g_q_dil"] = jnp.concatenate(dgq_d, axis=0)
    gg["g_k_dil"] = jnp.concatenate(dgk_d, axis=0)

    dq, dk, dv = _call_s(sched, "mla_bwd", 3,
                         lambda side: _mla_bwd(q, k, v, do_mla, do_mla.T, o_mla_t, lse_mla, H, "mla_bwd", side=side))
    dq_raw, gg["g_q_mla"] = _mla_q_bwd(dq, q_raw, Gn["g_q_mla"], tab_mla, H, scale_mla, "d_mla_q_prep")
    dkv, dkr, gg["g_k_mla"] = _mla_k_bwd(dk, dv, kv, lat, kr_col, Gn["g_k_mla"], tab_mla, H, "d_mla_k_prep")
    sched.grad("w_uq", mm(cq, dq_raw, "tn", BF16, "d_w_uq"))
    sched.grad("w_ukv", mm(ckv, dkv, "tn", BF16, "d_w_ukv"))
    dcq = mm(dq_raw, W("w_uq"), "nt", F32, "d_cq")
    dckv = mm(dkv, W("w_ukv"), "nt", F32, "d_ckv")
    dlat, gg["g_cq"], gg["g_ckv"] = _lat_bwd(dcq, dckv, dkr, lat, Gn["g_cq"], Gn["g_ckv"], "d_lat_norm")
    dh = mm(dlat, W("w_lat"), "nt", F32, "d_h_lat", res=dh)
    sched.grad("w_lat", mm(h, dlat, "tn", BF16, "d_w_lat"))

    dx1, dx1_b, gg["g_mix"] = _rms_bwd_call(x1, Gn["g_mix"], dh, dx2, "d_x1")
    dx, gg["g_ffn1"] = _ffn_bwd(sched, dx1, dx1_b, x, Gn["g_ffn1"], ffn1, tf, "ffn1", False)
    return loss, dx, gg


def _layout_weight(name, full, dims):
    H, QL, KVL, LP, Wd = dims["H"], dims["QL"], dims["KVL"], dims["LP"], dims["HD"] * DIL_HEAD
    off_dil = QL + KVL + MLA_ROPE
    off_gate = off_dil + 3 * len(DIL_GROUPS) * Wd
    if name == "w_lat":
        return _pad_to(full("w_in")[:, :off_dil], LP, 1)
    if name == "w_dil":
        return full("w_in")[:, off_dil:off_gate]
    if name == "w_gin":
        return full("w_in")[:, off_gate:]
    if name == "w_uq":
        return _pad_to(full("w_uq").reshape(QL, H, MLA_QK), MLA_QK_PAD, 2).reshape(QL, H * MLA_QK_PAD)
    return full(name)


def _natural_grad(name, gw, dims):
    H, QL, KVL = dims["H"], dims["QL"], dims["KVL"]
    if name == "w_in":
        return jnp.concatenate([gw["w_lat"][:, :QL + KVL + MLA_ROPE], gw["w_dil"], gw["w_gin"]], axis=1)
    if name == "w_uq":
        return gw["w_uq"].reshape(QL, H, MLA_QK_PAD)[:, :, :MLA_QK].reshape(QL, H * MLA_QK)
    return gw[name]


WEIGHT_SOURCES = {"w1_gu": ("w1_gate", "w1_up"), "w2_gu": ("w2_gate", "w2_up"), "w_lat": ("w_in",), "w_dil": ("w_in",),
                  "w_gin": ("w_in",)}
AG_FIRST = ("w1_gate", "w1_up")
AG_RIDES = {"ffn1_gate_up": ("w1_down", ("w_in", 0, 2)), "ffn1_down": (("w_in", 1, 2), "w_uq", "w_ukv"),
            "mla_attn": ("w_br_mla", "w_br_dil", "w_o", "w_ple_gate", "w_ple_proj", "w2_gate", "w2_up", "w2_down")}
RS_FFN2 = ("w_ple_proj", "w_ple_gate", "w2_down", "w2_gate", "w2_up")
RS_MIXER = ("w_o", "w_br_mla", "w_br_dil", "w_in", "w_uq", "w_ukv")
RS_RIDES = {
    "d_merged": (("sibling", RS_FFN2),),
    "mla_bwd": (("chips", RS_FFN2),),
    "d_h_lat": (("join", RS_FFN2),),
    "ffn1_d_wdown": (("sibling", RS_MIXER),),
    "ffn1_d_act": (("chips", ("w_o", "w_br_mla", "w_br_dil", "w_uq", "w_ukv")), ("sibling", ("w1_down",))),
    "ffn1_d_wgate": (("chips", ("w1_down",)),),
    "ffn1_d_wup": (("sibling", ("w1_gate",)),),
    "ffn1_d_norm": (("chips", ("w_in", "w1_gate")), ("sibling", ("w1_up",))),
    "ffn1_d_x": (("chips", ("w1_up",)),),
}
RS_LAST = ((("join", RS_MIXER + ("w1_down", "w1_gate", "w1_up")),),)


class _MeshSchedule:
    def __init__(self, w, m, v, dims):
        self.w, self.m, self.v, self.dims = w, m, v, dims
        self.shapes = {n: tuple(w[n].shape[1:]) for n in BIG}
        self.kinds = {n: _kind(n, self.shapes[n]) for n in BIG}
        self.info, buf_shapes = _gather_plan(self.shapes)
        x, y, c = _place()
        self.me = (2 * x + y).astype(jnp.int32).reshape(1)
        self.c_idx = c.astype(jnp.int32).reshape(1)
        self.shards = {n: w[n][0].astype(BF16) for n in BIG}
        self.bufs, self.gathered, self.parts_done, self.layout = {}, set(), {}, {}
        for n in BIG:
            out_name, kind, base, _, _ = self.info[n]
            self.bufs[out_name] = _place_own(self.shards[n], buf_shapes[out_name], kind, base, self.me, f"ag_own_{n}",
                                             prev=self.bufs.get(out_name))
        self.gw, self.views, self.pairs, self.halves, self.recv = {}, {}, {}, {}, {}
        self._ag_done(AG_FIRST, _run_side(self._ag(AG_FIRST), "ag_first"))

    def _ag(self, names):
        return _ag_side(names, self.shards, self.bufs, self.info)

    def _ag_done(self, names, outs):
        for out_name, buf in zip(_buffers_of(names, self.info), outs):
            self.bufs[out_name] = buf
        for name, _, parts in [_ag_entry(e) for e in names]:
            self.parts_done[name] = self.parts_done.get(name, 0) + 1
            if self.parts_done[name] == parts:
                self.gathered.add(name)

    def weight(self, name):
        if name not in self.layout:
            assert all(s in self.gathered for s in WEIGHT_SOURCES.get(name, (name,))), name
            if name in self.bufs and name not in self.info:
                self.layout[name] = self.bufs[name]
            else:
                full = lambda n: _natural(self.bufs[self.info[n][0]], self.info[n][1])
                self.layout[name] = _layout_weight(name, full, self.dims)
        return self.layout[name]

    def grad(self, name, g):
        self.gw[name] = g

    def _rs_side(self, stages):
        sides = []
        for stage, names in stages:
            if stage == "sibling":
                for n in names:
                    self.views[n] = _halves_view(_natural_grad(n, self.gw, self.dims), self.kinds[n], self.shapes[n])
                sides.append(_rs_sibling_side([self.views[n] for n in names]))
            elif stage == "chips":
                sides.append(_rs_chips_side([self.pairs[n] for n in names], [self.kinds[n] for n in names],
                                            [self.shapes[n][1] for n in names]))
            else:
                sides.append(_rs_join_side([self.halves[n] for n in names]))
        return sides[0] if len(sides) == 1 else _merge_sides(sides)

    def _rs_done(self, stages, outs):
        for stage, names in stages:
            got, outs = outs[:len(names)], outs[len(names):]
            for n, a in zip(names, got):
                if stage == "sibling":
                    self.pairs[n] = _pair_sum(self.views[n], a, self.c_idx, f"rs_pair_{n}")
                elif stage == "chips":
                    self.halves[n] = _sum_pieces(self.pairs[n], a, self.kinds[n], self.me, f"rs_sum_{n}")
                else:
                    self.recv[n] = a

    def side(self, tag):
        if tag in AG_RIDES:
            return self._ag(AG_RIDES[tag])
        if tag in RS_RIDES:
            return self._rs_side(RS_RIDES[tag])
        return None

    def done(self, tag, outs):
        if tag in AG_RIDES:
            self._ag_done(AG_RIDES[tag], outs)
        else:
            self._rs_done(RS_RIDES[tag], outs)

    def finish(self):
        for k, stages in enumerate(RS_LAST):
            self._rs_done(stages, _run_side(self._rs_side(stages), f"rs_last{k}"))
        outs = {"grad": {}, "delta": {}, "m": {}, "v": {}}
        for n in BIG:
            res = _adamw_halves(self.w[n][0], self.halves[n], self.recv[n], self.m[n][0], self.v[n][0], self.c_idx,
                                f"adamw_{n}")
            for kind, a in zip(("grad", "delta", "m", "v"), res):
                outs[kind][n] = a.reshape((1,) + a.shape)
        return outs


def _step(x, p, positions, loss_target, w, m, v):
    T, D = x.shape[1], x.shape[2]
    QL, KVL = w["g_cq"].shape[1], w["g_ckv"].shape[1]
    dims = {
        "H": 4 * w["w_uq"].shape[2] // MLA_QK, "HD": w["w_br_dil"].shape[1] // DIL_HEAD, "QL": QL, "KVL": KVL,
        "LP": _round_up(QL + KVL + MLA_ROPE, LANES), "tf": _pick(4 * w["w1_gate"].shape[2], 512),
    }
    small_shapes = {n: w[n].shape for n in SMALL}
    sched = _MeshSchedule(w, m, v, dims)
    Gn = {n: w[n] for n in SMALL}
    Gn["g_q_mla"] = _pad_to(Gn["g_q_mla"], MLA_QK_PAD, 1)
    Gn["g_k_mla"] = _pad_to(Gn["g_k_mla"], MLA_QK_PAD, 1)
    Gn["g_q_dil"] = Gn["g_q_dil"].reshape(len(DIL_GROUPS), 1, DIL_HEAD)
    Gn["g_k_dil"] = Gn["g_k_dil"].reshape(len(DIL_GROUPS), 1, DIL_HEAD)

    pos_b = jnp.broadcast_to(positions.astype(F32).reshape(T, 1), (T, LANES))
    loss, dx, gg = _local_step(sched, x[0], p[0, 0], pos_b, loss_target[0], Gn, dims)
    loss = lax.psum(loss[0, 0], ("x", "y", "c"))
    outs = sched.finish()

    gg["g_q_mla"] = gg["g_q_mla"][:, :MLA_QK]
    gg["g_k_mla"] = gg["g_k_mla"][:, :MLA_QK]
    g_small = _all_reduce_small(_pack_small(gg))
    d_s, m_s, v_s = _adamw(_pack_small({n: w[n] for n in SMALL}), g_small, _pack_small({n: m[n] for n in SMALL}),
                           _pack_small({n: v[n] for n in SMALL}), "adamw_gains")
    for kind, buf in (("grad", g_small), ("delta", d_s), ("m", m_s), ("v", v_s)):
        outs[kind].update(_unpack_small(buf, small_shapes))

    grad_x = dx.reshape(1, T, D)
    return (loss, grad_x, *[outs["grad"][n] for n in WEIGHTS], *[outs["delta"][n] for n in WEIGHTS],
            *[outs["m"][n] for n in WEIGHTS], *[outs["v"][n] for n in WEIGHTS])


def kernel(x, p, positions, g_ffn1, w1_gate, w1_up, w1_down, g_mix, w_in, g_cq, w_uq, g_ckv, w_ukv, g_q_mla, g_k_mla, g_q_dil, g_k_dil, w_br_mla, w_br_dil, w_o, g_ffn2, w2_gate, w2_up, w2_down, g_ple, w_ple_gate, w_ple_proj, loss_target, m_g_ffn1, m_w1_gate, m_w1_up, m_w1_down, m_g_mix, m_w_in, m_g_cq, m_w_uq, m_g_ckv, m_w_ukv, m_g_q_mla, m_g_k_mla, m_g_q_dil, m_g_k_dil, m_w_br_mla, m_w_br_dil, m_w_o, m_g_ffn2, m_w2_gate, m_w2_up, m_w2_down, m_g_ple, m_w_ple_gate, m_w_ple_proj, v_g_ffn1, v_w1_gate, v_w1_up, v_w1_down, v_g_mix, v_w_in, v_g_cq, v_w_uq, v_g_ckv, v_w_ukv, v_g_q_mla, v_g_k_mla, v_g_q_dil, v_g_k_dil, v_w_br_mla, v_w_br_dil, v_w_o, v_g_ffn2, v_w2_gate, v_w2_up, v_w2_down, v_g_ple, v_w_ple_gate, v_w_ple_proj):
    args = locals()
    w = {n: args[n] for n in WEIGHTS}
    m = {n: args["m_" + n] for n in WEIGHTS}
    v = {n: args["v_" + n] for n in WEIGHTS}
    return _step(x, p, positions, loss_target, w, m, v)
```

```python
import functools

import numpy as np
import jax
import jax.numpy as jnp
from jax import lax
from jax.experimental import pallas as pl
from jax.experimental.pallas import tpu as pltpu

F32 = jnp.float32
BF16 = jnp.bfloat16
MESH_ID = pl.DeviceIdType.MESH

MLA_NOPE = 128
MLA_ROPE = 64
MLA_V = 128
MLA_QK = MLA_NOPE + MLA_ROPE
MLA_QK_PAD = 256
DIL_GROUPS = ((128, 1), (512, 4), (2048, 16))
DIL_HEAD = 128
DIL_ROT = DIL_HEAD // 4
DIL_BLOCK = 128
ROPE_THETA = 500000.0
EPS = 1e-6
NEG = -1e30
ADAM_LR = 0.001
ADAM_B1 = 0.9
ADAM_B2 = 0.999
ADAM_EPS = 1e-08
ADAM_WD = 0.01
ADAM_STEP = 10

LANES = 128
VMEM_LIMIT_BYTES = 56 * 1024 * 1024
MM_VMEM_BYTES = 46 * 1024 * 1024

BIG = ("w1_gate", "w1_up", "w1_down", "w_in", "w_uq", "w_ukv", "w_br_mla", "w_br_dil", "w_o",
       "w2_gate", "w2_up", "w2_down", "w_ple_gate", "w_ple_proj")
GATHER_PLAN = (("w1_gu", ("w1_gate", "w1_up")), ("w1_down", ("w1_down",)), ("w_in", ("w_in",)), ("w_uq", ("w_uq",)),
               ("w_ukv", ("w_ukv",)), ("w_br_mla", ("w_br_mla",)), ("w_br_dil", ("w_br_dil",)), ("w_o", ("w_o",)),
               ("w2_gu", ("w2_gate", "w2_up")), ("w2_down", ("w2_down",)), ("w_ple_gate", ("w_ple_gate",)),
               ("w_ple_proj", ("w_ple_proj",)))
ROW_SHARDED = ("w1_down", "w_o", "w2_down", "w_ple_gate")
SMALL = ("g_ffn1", "g_mix", "g_cq", "g_ckv", "g_q_mla", "g_k_mla", "g_q_dil", "g_k_dil", "g_ffn2", "g_ple")
WEIGHTS = ("g_ffn1", "w1_gate", "w1_up", "w1_down", "g_mix", "w_in", "g_cq", "w_uq", "g_ckv", "w_ukv", "g_q_mla",
           "g_k_mla", "g_q_dil", "g_k_dil", "w_br_mla", "w_br_dil", "w_o", "g_ffn2", "w2_gate", "w2_up", "w2_down",
           "g_ple", "w_ple_gate", "w_ple_proj")


def _pick(n, target, align=LANES):
    if n <= target:
        return n
    t = (target // align) * align
    while t >= align:
        if n % t == 0:
            return t
        t -= align
    return n


def _params(n_axes):
    return pltpu.CompilerParams(dimension_semantics=("arbitrary",) * n_axes, vmem_limit_bytes=VMEM_LIMIT_BYTES)


def _sigmoid(x):
    return 0.5 * jnp.tanh(0.5 * x) + 0.5


ANY = pl.BlockSpec(memory_space=pl.ANY)


class _Side:
    def __init__(self, arrays, out_shapes, aliases, sem_shapes, phases):
        self.arrays, self.out_shapes, self.aliases = list(arrays), list(out_shapes), dict(aliases)
        self.sem_shapes, self.phases = list(sem_shapes), list(phases)

    def start(self, p, ins, outs, sems):
        for cp in self.phases[p][0](ins, outs, sems):
            cp.start()

    def wait(self, p, ins, outs, sems):
        for cp in self.phases[p][1](ins, outs, sems):
            cp.wait_recv()
        for cp in self.phases[p][0](ins, outs, sems):
            cp.wait_send()

    def run(self, step, n_steps, ins, outs, sems):
        n_ph = len(self.phases)
        assert n_ph <= 2
        starts = (0, int(0.85 * (n_steps - 1)))
        if n_steps <= n_ph:
            @pl.when(step == n_steps - 1)
            def _():
                for p in range(n_ph):
                    self.start(p, ins, outs, sems)
                    self.wait(p, ins, outs, sems)
            return
        for p in range(n_ph):
            @pl.when(step == starts[p])
            def _(p=p):
                if p > 0:
                    self.wait(p - 1, ins, outs, sems)
                self.start(p, ins, outs, sems)

        @pl.when(step == n_steps - 1)
        def _():
            self.wait(n_ph - 1, ins, outs, sems)


def _merge_sides(sides):
    arrays, out_shapes, aliases, sem_shapes, spans = [], [], {}, [], []
    for s in sides:
        assert len(s.phases) == 1
        spans.append((len(arrays), len(out_shapes), len(sem_shapes), s))
        aliases.update({len(arrays) + i: len(out_shapes) + o for i, o in s.aliases.items()})
        arrays += s.arrays
        out_shapes += s.out_shapes
        sem_shapes += s.sem_shapes

    def part(which):
        def fn(ins, outs, sems):
            cps = []
            for a0, o0, s0, s in spans:
                cps += s.phases[0][which](ins[a0:a0 + len(s.arrays)], outs[o0:o0 + len(s.out_shapes)],
                                          sems[s0:s0 + len(s.sem_shapes)])
            return cps
        return fn

    return _Side(arrays, out_shapes, aliases, sem_shapes, [(part(0), part(1))])


def _side_parts(side, n_lead, n_out):
    if side is None:
        return [], [], [], {}
    return (side.arrays, side.out_shapes, side.sem_shapes, {n_lead + i: n_out + o for i, o in side.aliases.items()})


def _carry(kern, side, n_lead, n_out, n_scratch, step_of, n_steps):
    if side is None:
        return kern
    a = n_lead
    b = a + len(side.arrays)
    c = b + n_out
    d = c + len(side.out_shapes)
    e = d + n_scratch

    def wrapped(*refs):
        side.run(step_of(), n_steps, refs[a:b], refs[c:d], refs[e:])
        kern(*refs[:a], *refs[b:c], *refs[d:e])

    return wrapped


def _run_side(side, name):
    n_in, n_out = len(side.arrays), len(side.out_shapes)

    def body(*refs):
        ins, outs, sems = refs[:n_in], refs[n_in:n_in + n_out], refs[n_in + n_out:]
        if len(side.phases) == 1:
            side.start(0, ins, outs, sems)
            side.wait(0, ins, outs, sems)
            return
        (sends1, recvs1), (sends2, recvs2) = side.phases
        side.start(0, ins, outs, sems)
        forwards = sends2(ins, outs, sems)
        for landed, fw in zip(recvs1(ins, outs, sems), forwards):
            landed.wait_recv()
            fw.start()
        for cp in sends1(ins, outs, sems):
            cp.wait_send()
        for cp in recvs2(ins, outs, sems):
            cp.wait_recv()
        for fw in forwards:
            fw.wait_send()

    return pl.pallas_call(body, name=name, out_shape=side.out_shapes, in_specs=[ANY] * n_in, out_specs=[ANY] * n_out,
                          scratch_shapes=side.sem_shapes, input_output_aliases=side.aliases)(*side.arrays)


def _mm_tiles(M, N, K, n_pairs, out_bytes, has_res):
    tm = _pick(M, 1024)
    tks = sorted({_pick(K, t) for t in (8192, 5632, 4096, 2816, 2048, 1408, 1024, 512)}, reverse=True)
    tns = sorted({_pick(N, t) for t in (1536, 1024, 512)}, reverse=True)
    for tk in tks:
        for tn in tns:
            need = 4 * n_pairs * (tm * tk + tk * tn) + tm * tn * (4 * (K > tk) + 2 * out_bytes + 8 * has_res + 4)
            if need <= MM_VMEM_BYTES:
                return tm, tn, tk
    raise ValueError((M, N, K))


def _mm(a, b, mode, out_dtype, name, res=None, alpha=1.0, a2=None, b2=None, b2_k_offset=0, side=None):
    if mode == "nn":
        (M, K), (K2, N) = a.shape, b.shape
    elif mode == "nt":
        (M, K), (N, K2) = a.shape, b.shape
    else:
        (K, M), (K2, N) = a.shape, b.shape
    assert K == K2 or (mode == "nt" and K2 > K), (name, a.shape, b.shape)
    assert a.dtype == BF16 and b.dtype == BF16, name
    tm, tn, tk = _mm_tiles(M, N, K, 1 if a2 is None else 2, jnp.dtype(out_dtype).itemsize, res is not None)
    nk = K // tk
    assert b2_k_offset % tk == 0 and (b2_k_offset == 0 or mode == "nt"), name
    k_off2 = b2_k_offset // tk
    if mode == "nn":
        a_spec = pl.BlockSpec((tm, tk), lambda i, j, k: (i, k))
        b_spec = pl.BlockSpec((tk, tn), lambda i, j, k: (k, j))
        dims = (((1,), (0,)), ((), ()))
    elif mode == "nt":
        a_spec = pl.BlockSpec((tm, tk), lambda i, j, k: (i, k))
        b_spec = pl.BlockSpec((tn, tk), lambda i, j, k: (j, k))
        b2_spec = pl.BlockSpec((tn, tk), lambda i, j, k: (j, k + k_off2))
        dims = (((1,), (1,)), ((), ()))
    else:
        a_spec = pl.BlockSpec((tk, tm), lambda i, j, k: (k, i))
        b_spec = pl.BlockSpec((tk, tn), lambda i, j, k: (k, j))
        dims = (((0,), (0,)), ((), ()))
    o_spec = pl.BlockSpec((tm, tn), lambda i, j, k: (i, j))
    has_res = res is not None
    n_pairs = 1 if a2 is None else 2
    n_main = 2 * n_pairs + int(has_res)
    n_side_in = len(side.arrays) if side else 0
    n_side_out = len(side.out_shapes) if side else 0
    n_acc = 1 if nk > 1 else 0
    gi, gj = M // tm, N // tn
    n_steps = gi * gj * nk

    def kern(*refs):
        r_ref = refs[2 * n_pairs] if has_res else None
        o_ref = refs[n_main + n_side_in]
        if side:
            step = (pl.program_id(0) * gj + pl.program_id(1)) * nk + pl.program_id(2)
            side.run(step, n_steps, refs[n_main:n_main + n_side_in],
                     refs[n_main + n_side_in + 1:n_main + n_side_in + 1 + n_side_out],
                     refs[n_main + n_side_in + 1 + n_side_out + n_acc:])
        part = lax.dot_general(refs[0][...], refs[1][...], dims, preferred_element_type=F32)
        if n_pairs == 2:
            part = part + lax.dot_general(refs[2][...], refs[3][...], dims, preferred_element_type=F32)

        def finish(r):
            if alpha != 1.0:
                r = r * alpha
            if has_res:
                r = r_ref[...] + r
            o_ref[...] = r.astype(o_ref.dtype)

        if nk == 1:
            finish(part)
            return
        acc_ref = refs[n_main + n_side_in + 1 + n_side_out]
        k = pl.program_id(2)

        @pl.when(k == 0)
        def _():
            acc_ref[...] = part

        @pl.when(k > 0)
        def _():
            acc_ref[...] += part

        @pl.when(k == nk - 1)
        def _():
            finish(acc_ref[...])

    ins = (a, b) + ((a2, b2) if n_pairs == 2 else ()) + ((res,) if has_res else ())
    in_specs = [a_spec, b_spec] + ([a_spec, b2_spec if mode == "nt" else b_spec] if n_pairs == 2 else [])
    in_specs += [o_spec] if has_res else []
    out_shape = jax.ShapeDtypeStruct((M, N), out_dtype)
    scratch = [pltpu.VMEM((tm, tn), F32)] if nk > 1 else []
    if not side:
        return pl.pallas_call(kern, name=name, grid=(gi, gj, nk), in_specs=in_specs, out_specs=o_spec,
                              out_shape=out_shape, scratch_shapes=scratch, compiler_params=_params(3))(*ins)
    outs = pl.pallas_call(
        kern, name=name, grid=(gi, gj, nk), in_specs=in_specs + [ANY] * n_side_in,
        out_specs=[o_spec] + [ANY] * n_side_out, out_shape=[out_shape] + list(side.out_shapes),
        scratch_shapes=scratch + list(side.sem_shapes),
        input_output_aliases={n_main + i: 1 + o for i, o in side.aliases.items()},
        compiler_params=_params(3))(*ins, *side.arrays)
    return outs[0], list(outs[1:])


def _vcall(body, grid, ins, in_specs, out_shapes, out_specs, name, n_inner_acc=0, n_acc=0, side=None, carried=None):
    n_body = len(ins)
    if carried is not None:
        ins, in_specs = tuple(ins) + (carried,), list(in_specs) + [ANY]
    n_in, n_out = len(ins), len(out_shapes)
    n_plain = n_out - n_acc - n_inner_acc

    def kern(*refs):
        vals = body(*[r[...] for r in refs[:n_body]])
        if not isinstance(vals, (tuple, list)):
            vals = (vals,)
        out_refs = refs[n_in:]
        inner_first = pl.program_id(len(grid) - 1) == 0
        first = inner_first
        for ax in range(len(grid) - 1):
            first = jnp.logical_and(first, pl.program_id(ax) == 0)
        for idx, (r, v) in enumerate(zip(out_refs, vals)):
            if idx < n_plain:
                r[...] = v.astype(r.dtype)
                continue
            start = inner_first if idx < n_plain + n_inner_acc else first

            @pl.when(start)
            def _(r=r, v=v):
                r[...] = v.astype(r.dtype)

            @pl.when(jnp.logical_not(start))
            def _(r=r, v=v):
                r[...] += v.astype(r.dtype)

    s_in, s_out, s_sems, s_alias = _side_parts(side, n_in, n_out)
    if carried is not None:
        s_alias = {**s_alias, n_body: 0}

    def step_of():
        step = pl.program_id(0)
        for ax in range(1, len(grid)):
            step = step * grid[ax] + pl.program_id(ax)
        return step

    return pl.pallas_call(
        _carry(kern, side, n_in, n_out, 0, step_of, int(np.prod(grid))), name=name, grid=grid,
        in_specs=list(in_specs) + [ANY] * len(s_in), out_specs=list(out_specs) + [ANY] * len(s_out),
        out_shape=list(out_shapes) + list(s_out), scratch_shapes=s_sems, input_output_aliases=s_alias,
        compiler_params=_params(len(grid)))(*ins, *s_in)


def _rows(tm, c):
    return pl.BlockSpec((tm, c), lambda i: (i, 0))


def _vec(c):
    return pl.BlockSpec((1, c), lambda i: (0, 0))


def _sds(shape, dtype):
    return jax.ShapeDtypeStruct(shape, dtype)


def _rstd(x, c):
    return lax.rsqrt(jnp.sum(x * x, axis=-1, keepdims=True) * (1.0 / c) + EPS)


def _rms_bwd(xh, r, g, dn, c):
    u = dn * g
    dx = r * (u - xh * (jnp.sum(xh * u, axis=-1, keepdims=True) * (1.0 / c)))
    return dx, jnp.sum(dn * xh, axis=0, keepdims=True)


def _rope(t, c, sa, sb, half):
    return t * c + pltpu.roll(t, LANES - half, 1) * sa + pltpu.roll(t, half, 1) * sb


def _rope_t(d, c, sa, sb, half):
    return d * c + pltpu.roll(d * sa, half, 1) + pltpu.roll(d * sb, LANES - half, 1)


def _rope_tables(pos_b, rd, name):
    T = pos_b.shape[0]
    half = rd // 2
    inv = ROPE_THETA ** (-jnp.arange(half, dtype=F32) * 2.0 / rd)
    inv_full = jnp.concatenate([inv, inv, jnp.zeros((LANES - rd,), F32)]).reshape(1, LANES)
    lane = np.arange(LANES)
    ma = jnp.asarray((lane < half).astype(np.float32)).reshape(1, LANES)
    mb = jnp.asarray(((lane >= half) & (lane < rd)).astype(np.float32)).reshape(1, LANES)
    tm = _pick(T, 1024, 8)

    def body(pos, invf, a, b):
        ang = pos * invf
        c, s = jnp.cos(ang), jnp.sin(ang)
        inside = a + b
        return c * inside + (1.0 - inside), -s * a, s * b

    return _vcall(body, (T // tm,), (pos_b, inv_full, ma, mb), [_rows(tm, LANES)] + [_vec(LANES)] * 3,
                  [_sds((T, LANES), F32)] * 3, [_rows(tm, LANES)] * 3, name)


def _rms_fwd(x, g, name):
    T, C = x.shape
    tm = _pick(T, 512, 8)

    def body(xv, gv):
        return xv * _rstd(xv, C) * gv

    return _vcall(body, (T // tm,), (x, g), [_rows(tm, C), _vec(C)], [_sds((T, C), BF16)], [_rows(tm, C)], name)[0]


def _rms_bwd_call(x, g, dn, dres, name, side=None, with_bf16=True):
    T, C = x.shape
    tm = _pick(T, 256, 8)

    def body(xv, gv, dnv, drv):
        r = _rstd(xv, C)
        dx, dg = _rms_bwd(xv * r, r, gv, dnv.astype(F32), C)
        dx = drv + dx
        return (dx, dx, dg) if with_bf16 else (dx, dg)

    n_dx = 2 if with_bf16 else 1
    return _vcall(body, (T // tm,), (x, g, dn, dres), [_rows(tm, C), _vec(C), _rows(tm, C), _rows(tm, C)],
                  [_sds((T, C), F32), _sds((T, C), BF16)][:n_dx] + [_sds((1, C), F32)],
                  [_rows(tm, C)] * n_dx + [_vec(C)], name, n_acc=1, side=side)


def _gate_up(n, w_gu, tf, name, side=None):
    T, D = n.shape
    F = w_gu.shape[1] // 2
    tm = _pick(T, 1024, 16)
    nf = F // tf

    def kern(n_ref, wg_ref, wu_ref, a_ref, b_ref, act_ref):
        x = n_ref[...]
        a = jnp.dot(x, wg_ref[...], preferred_element_type=F32)
        b = jnp.dot(x, wu_ref[...], preferred_element_type=F32)
        a_ref[...] = a.astype(BF16)
        b_ref[...] = b.astype(BF16)
        act_ref[...] = (a * _sigmoid(a) * b).astype(BF16)

    tile = pl.BlockSpec((tm, tf), lambda i, j: (i, j))
    s_in, s_out, s_sems, s_alias = _side_parts(side, 3, 3)
    step_of = lambda: pl.program_id(0) * nf + pl.program_id(1)
    outs = pl.pallas_call(
        _carry(kern, side, 3, 3, 0, step_of, (T // tm) * nf), name=name, grid=(T // tm, nf),
        in_specs=[pl.BlockSpec((tm, D), lambda i, j: (i, 0)), pl.BlockSpec((D, tf), lambda i, j: (0, j)),
                  pl.BlockSpec((D, tf), lambda i, j: (0, j + nf))] + [ANY] * len(s_in),
        out_specs=[tile] * 3 + [ANY] * len(s_out), out_shape=[_sds((T, F), BF16)] * 3 + s_out,
        scratch_shapes=s_sems, input_output_aliases=s_alias, compiler_params=_params(2))(n, w_gu, w_gu, *s_in)
    return outs


def _d_gate_up(dout_b, w_d, a, b, tf, name, side=None):
    T, D = dout_b.shape
    F = w_d.shape[0]
    tm = _pick(T, 1024, 16)
    nf = F // tf

    n_chunks = 4 if tm % 64 == 0 else 1

    def kern(d_ref, w_ref, a_ref, b_ref, da_ref, db_ref):
        for u in range(n_chunks):
            rows = slice(u * (tm // n_chunks), (u + 1) * (tm // n_chunks))
            d = 0.5 * lax.dot_general(d_ref[rows, :], w_ref[...], NT, preferred_element_type=F32)
            a, b = a_ref[rows, :].astype(F32), b_ref[rows, :].astype(F32)
            sg = _sigmoid(a)
            da_ref[rows, :] = (d * b * (sg * (1.0 + a * (1.0 - sg)))).astype(BF16)
            db_ref[rows, :] = (d * (a * sg)).astype(BF16)

    tile = pl.BlockSpec((tm, tf), lambda i, j: (i, j))
    s_in, s_out, s_sems, s_alias = _side_parts(side, 4, 2)
    step_of = lambda: pl.program_id(0) * nf + pl.program_id(1)
    outs = pl.pallas_call(
        _carry(kern, side, 4, 2, 0, step_of, (T // tm) * nf), name=name, grid=(T // tm, nf),
        in_specs=[pl.BlockSpec((tm, D), lambda i, j: (i, 0)), pl.BlockSpec((tf, D), lambda i, j: (j, 0)), tile, tile]
        + [ANY] * len(s_in),
        out_specs=[tile] * 2 + [ANY] * len(s_out), out_shape=[_sds((T, F), BF16)] * 2 + s_out,
        scratch_shapes=s_sems, input_output_aliases=s_alias, compiler_params=_params(2))(dout_b, w_d, a, b, *s_in)
    return outs


def _lat_fwd(lat, g_cq, g_ckv, name):
    T, LP = lat.shape
    QL, KVL = g_cq.shape[1], g_ckv.shape[1]
    tm = _pick(T, 512, 8)

    def body(v, gq, gk):
        xq, xk = v[:, :QL], v[:, QL:QL + KVL]
        return xq * _rstd(xq, QL) * gq, xk * _rstd(xk, KVL) * gk

    return _vcall(body, (T // tm,), (lat, g_cq, g_ckv), [_rows(tm, LP), _vec(QL), _vec(KVL)],
                  [_sds((T, QL), BF16), _sds((T, KVL), BF16)], [_rows(tm, QL), _rows(tm, KVL)], name)


def _lat_bwd(dcq, dckv, dkr, lat, g_cq, g_ckv, name):
    T, LP = lat.shape
    QL, KVL = g_cq.shape[1], g_ckv.shape[1]
    tm = _pick(T, 512, 8)

    def body(dq, dk, dr, v, gq, gk):
        xq, xk = v[:, :QL], v[:, QL:QL + KVL]
        rq, rk = _rstd(xq, QL), _rstd(xk, KVL)
        dxq, dgq = _rms_bwd(xq * rq, rq, gq, dq, QL)
        dxk, dgk = _rms_bwd(xk * rk, rk, gk, dk, KVL)
        return jnp.concatenate([dxq, dxk, dr], axis=1), dgq, dgk

    return _vcall(body, (T // tm,), (dcq, dckv, dkr, lat, g_cq, g_ckv),
                  [_rows(tm, QL), _rows(tm, KVL), _rows(tm, LANES), _rows(tm, LP), _vec(QL), _vec(KVL)],
                  [_sds((T, LP), BF16), _sds((1, QL), F32), _sds((1, KVL), F32)],
                  [_rows(tm, LP), _vec(QL), _vec(KVL)], name, n_acc=2)


def _head_spec(tm, w):
    return pl.BlockSpec((tm, w), lambda i, h: (i, h))


def _row2(tm, w, col=0):
    return pl.BlockSpec((tm, w), lambda i, h: (i, col))


def _vec2(w):
    return pl.BlockSpec((1, w), lambda i, h: (0, 0))


def _mla_q_prep(q_raw, g_q, tabs, H, scale, name):
    T = q_raw.shape[0]
    tm = _pick(T, 1024, 8)
    half = MLA_ROPE // 2

    def body(x, g, c, sa, sb):
        n = x * _rstd(x, MLA_QK) * g
        return jnp.concatenate([n[:, :LANES], _rope(n[:, LANES:], c, sa, sb, half)], axis=1) * scale

    return _vcall(body, (T // tm, H), (q_raw, g_q) + tabs,
                  [_head_spec(tm, MLA_QK_PAD), _vec2(MLA_QK_PAD)] + [_row2(tm, LANES)] * 3,
                  [_sds((T, H * MLA_QK_PAD), BF16)], [_head_spec(tm, MLA_QK_PAD)], name)[0]


def _mla_q_bwd(dq, q_raw, g_q, tabs, H, scale, name):
    T = q_raw.shape[0]
    tm = _pick(T, 1024, 8)
    half = MLA_ROPE // 2

    def body(d, x, g, c, sa, sb):
        r = _rstd(x, MLA_QK)
        d = d * scale
        dn = jnp.concatenate([d[:, :LANES], _rope_t(d[:, LANES:], c, sa, sb, half)], axis=1)
        return _rms_bwd(x * r, r, g, dn, MLA_QK)

    return _vcall(body, (T // tm, H), (dq, q_raw, g_q) + tabs,
                  [_head_spec(tm, MLA_QK_PAD), _head_spec(tm, MLA_QK_PAD), _vec2(MLA_QK_PAD)] + [_row2(tm, LANES)] * 3,
                  [_sds((T, H * MLA_QK_PAD), BF16), _sds((1, MLA_QK_PAD), F32)],
                  [_head_spec(tm, MLA_QK_PAD), _vec2(MLA_QK_PAD)], name, n_acc=1)


def _mla_k_prep(kv, lat, kr_col, g_k, tabs, H, name):
    T = kv.shape[0]
    tm = _pick(T, 1024, 8)
    half = MLA_ROPE // 2

    def body(x, kr, g, c, sa, sb):
        kn = x[:, :LANES]
        r = lax.rsqrt((jnp.sum(kn * kn, axis=-1, keepdims=True) + jnp.sum(kr * kr, axis=-1, keepdims=True))
                      * (1.0 / MLA_QK) + EPS)
        k0 = kn * r * g[:, :LANES]
        k1 = _rope(kr * r * g[:, LANES:], c, sa, sb, half)
        return jnp.concatenate([k0, k1], axis=1), x[:, LANES:]

    return _vcall(body, (T // tm, H), (kv, lat, g_k) + tabs,
                  [_head_spec(tm, 2 * LANES), _row2(tm, LANES, kr_col), _vec2(MLA_QK_PAD)] + [_row2(tm, LANES)] * 3,
                  [_sds((T, H * MLA_QK_PAD), BF16), _sds((T, H * MLA_V), BF16)],
                  [_head_spec(tm, MLA_QK_PAD), _head_spec(tm, MLA_V)], name)


def _mla_k_bwd(dk, dv, kv, lat, kr_col, g_k, tabs, H, name):
    T = kv.shape[0]
    tm = _pick(T, 1024, 8)
    half = MLA_ROPE // 2

    def body(d, dvv, x, kr, g, c, sa, sb):
        xx = jnp.concatenate([x[:, :LANES], kr], axis=1)
        r = _rstd(xx, MLA_QK)
        dn = jnp.concatenate([d[:, :LANES], _rope_t(d[:, LANES:], c, sa, sb, half)], axis=1)
        dx, dg = _rms_bwd(xx * r, r, g, dn, MLA_QK)
        return jnp.concatenate([dx[:, :LANES], dvv], axis=1), dx[:, LANES:], dg

    return _vcall(body, (T // tm, H), (dk, dv, kv, lat, g_k) + tabs,
                  [_head_spec(tm, MLA_QK_PAD), _head_spec(tm, MLA_V), _head_spec(tm, 2 * LANES),
                   _row2(tm, LANES, kr_col), _vec2(MLA_QK_PAD)] + [_row2(tm, LANES)] * 3,
                  [_sds((T, H * 2 * LANES), BF16), _sds((T, LANES), F32), _sds((1, MLA_QK_PAD), F32)],
                  [_head_spec(tm, 2 * LANES), _row2(tm, LANES), _vec2(MLA_QK_PAD)], name, n_inner_acc=1, n_acc=1)


def _dil_prep(pd, g_q, g_k, tabs, HD, scale, name):
    T = pd.shape[0]
    W = HD * DIL_HEAD
    G = len(DIL_GROUPS)
    tm = _pick(T, 512, 8)
    half = DIL_ROT // 2

    def body(xq, xk, xv, gq, gk, c, sa, sb):
        outs = []
        for x, g, s in ((xq, gq, scale), (xk, gk, 1.0)):
            heads = []
            for h in range(HD):
                xs = x[:, h * DIL_HEAD:(h + 1) * DIL_HEAD].astype(F32)
                n = _rope(xs * _rstd(xs, DIL_HEAD) * g, c, sa, sb, half)
                heads.append(n * s if s != 1.0 else n)
            outs.append(jnp.concatenate(heads, axis=1))
        return outs[0], outs[1], xv

    gspec = pl.BlockSpec((None, 1, DIL_HEAD), lambda i, g: (g, 0, 0))
    return _vcall(body, (T // tm, G), (pd, pd, pd, g_q, g_k) + tabs,
                  [pl.BlockSpec((tm, W), lambda i, g: (i, 3 * g)), pl.BlockSpec((tm, W), lambda i, g: (i, 3 * g + 1)),
                   pl.BlockSpec((tm, W), lambda i, g: (i, 3 * g + 2)), gspec, gspec] + [_row2(tm, LANES)] * 3,
                  [_sds((T, G * W), F32)] * 3, [pl.BlockSpec((tm, W), lambda i, g: (i, g))] * 3, name)


def _dil_prep_bwd(dq, dk, dv, pd, grp, g_q, g_k, tabs, HD, scale, name, prev=None):
    T = pd.shape[0]
    W = HD * DIL_HEAD
    tm = _pick(T, 256, 8)
    half = DIL_ROT // 2

    def body(dqv, dkv, dvv, xq, xk, gq, gk, c, sa, sb):
        cols, dgs = [], []
        for d, x, g, s in ((dqv, xq, gq, scale), (dkv, xk, gk, 1.0)):
            heads, dg = [], None
            for h in range(HD):
                sl = slice(h * DIL_HEAD, (h + 1) * DIL_HEAD)
                xs = x[:, sl].astype(F32)
                r = _rstd(xs, DIL_HEAD)
                dh = d[:, sl] * s if s != 1.0 else d[:, sl]
                dx, dgh = _rms_bwd(xs * r, r, g, _rope_t(dh, c, sa, sb, half), DIL_HEAD)
                heads.append(dx)
                dg = dgh if dg is None else dg + dgh
            cols.append(jnp.concatenate(heads, axis=1))
            dgs.append(dg)
        return jnp.concatenate(cols + [dvv], axis=1), dgs[0], dgs[1]

    gq, gk = g_q[grp], g_k[grp]
    G = len(DIL_GROUPS)
    return _vcall(body, (T // tm,), (dq, dk, dv, pd, pd, gq, gk) + tabs,
                  [_rows(tm, W)] * 3 + [pl.BlockSpec((tm, W), lambda i: (i, 3 * grp)),
                                        pl.BlockSpec((tm, W), lambda i: (i, 3 * grp + 1)),
                                        _vec(DIL_HEAD), _vec(DIL_HEAD)] + [_rows(tm, LANES)] * 3,
                  [_sds((T, 3 * G * W), BF16), _sds((1, DIL_HEAD), F32), _sds((1, DIL_HEAD), F32)],
                  [pl.BlockSpec((tm, 3 * W), lambda i: (i, grp)), _vec(DIL_HEAD), _vec(DIL_HEAD)], name, n_acc=2,
                  carried=prev)


def _dil_merge(os_, lses, name):
    T, W = os_[0].shape
    tm = _pick(T, 256, 8)

    def body(o0, o1, o2, l0, l1, l2):
        m = jnp.maximum(jnp.maximum(l0, l1), l2)
        w0, w1, w2 = jnp.exp(l0 - m), jnp.exp(l1 - m), jnp.exp(l2 - m)
        z = w0 + w1 + w2
        return (w0 * o0 + w1 * o1 + w2 * o2) / z, m + jnp.log(z)

    return _vcall(body, (T // tm,), tuple(os_) + tuple(lses), [_rows(tm, W)] * 6,
                  [_sds((T, W), BF16), _sds((T, W), F32)], [_rows(tm, W)] * 2, name)


def _gate_merge(pg, bm, bd, name):
    T, D = bm.shape
    tm = _pick(T, 256, 8)

    def body(g, m, d):
        g = g.astype(F32)
        return _sigmoid(g[:, :D]) * m + _sigmoid(g[:, D:]) * d

    return _vcall(body, (T // tm,), (pg, bm, bd), [_rows(tm, 2 * D), _rows(tm, D), _rows(tm, D)],
                  [_sds((T, D), BF16)], [_rows(tm, D)], name)[0]


def _gate_bwd(dmerged, pg, bm, bd, name):
    T, D = bm.shape
    tm = _pick(T, 256, 8)

    def body(dm, g, m, d):
        g = g.astype(F32)
        s0, s1 = _sigmoid(g[:, :D]), _sigmoid(g[:, D:])
        dpg = jnp.concatenate([dm * m * s0 * (1.0 - s0), dm * d * s1 * (1.0 - s1)], axis=1)
        return dm * s0, dm * s1, dpg

    return _vcall(body, (T // tm,), (dmerged, pg, bm, bd), [_rows(tm, D), _rows(tm, 2 * D), _rows(tm, D), _rows(tm, D)],
                  [_sds((T, D), BF16), _sds((T, D), BF16), _sds((T, 2 * D), BF16)],
                  [_rows(tm, D), _rows(tm, D), _rows(tm, 2 * D)], name)


def _ple_loss(x3, zg, pp, target, name):
    T, D = x3.shape
    tm = _pick(T, 256, 8)

    def body(x, z, p_, t):
        s = _sigmoid(z)
        e = x + s * p_ - t
        dy = e * (1.0 / D)
        part = 0.5 * jnp.sum(jnp.sum(e * e, axis=1, keepdims=True), axis=0, keepdims=True) * (1.0 / D)
        return dy, dy * s, dy * p_ * s * (1.0 - s), jnp.broadcast_to(part, (1, LANES))

    return _vcall(body, (T // tm,), (x3, zg, pp, target), [_rows(tm, D)] * 4,
                  [_sds((T, D), F32), _sds((T, D), BF16), _sds((T, D), BF16), _sds((1, LANES), F32)],
                  [_rows(tm, D)] * 3 + [_vec(LANES)], name, n_acc=1)


NT = (((1,), (1,)), ((), ()))
TN = (((0,), (0,)), ((), ()))


def _diag_mask(s):
    row = lax.broadcasted_iota(jnp.int32, s.shape, 0)
    col = lax.broadcasted_iota(jnp.int32, s.shape, 1)
    return jnp.where(col <= row, s, NEG)


def _causal_pairs(nq, key_major, kr=1):
    if key_major:
        pairs = [(i, j) for j in range(nq // kr) for i in range(kr * j, nq)]
    else:
        pairs = [(i, j) for i in range(nq) for j in range(i // kr + 1)]
    return (jnp.asarray([pr[0] for pr in pairs], jnp.int32), jnp.asarray([pr[1] for pr in pairs], jnp.int32))


def _mla_fwd(q, k, v, H, name, side=None):
    T = q.shape[0]
    tq = _pick(T, 512)
    nq = T // tq
    hb = 2 if H % 2 == 0 else 1
    kr = 2 if nq % 2 == 0 else 1
    tk = kr * tq
    qi_tab, kj_tab = _causal_pairs(nq, key_major=False, kr=kr)

    def kern(qi_ref, kj_ref, q_ref, k_ref, vt_ref, o_ref, ot_ref, lse_ref, m_sc, l_sc, acc_sc):
        t = pl.program_id(1)
        qi, kj = qi_ref[t], kj_ref[t]

        @pl.when(kj == 0)
        def _():
            m_sc[...] = jnp.full_like(m_sc, NEG)
            l_sc[...] = jnp.zeros_like(l_sc)
            acc_sc[...] = jnp.zeros_like(acc_sc)

        def tile(diagonal):
            for hh in range(hb):
                qs = slice(hh * MLA_QK_PAD, (hh + 1) * MLA_QK_PAD)
                vs = slice(hh * MLA_V, (hh + 1) * MLA_V)
                st = lax.dot_general(k_ref[:, qs], q_ref[:, qs], NT, preferred_element_type=F32)
                if diagonal:
                    key = lax.broadcasted_iota(jnp.int32, st.shape, 0)
                    qry = lax.broadcasted_iota(jnp.int32, st.shape, 1) + (qi - kr * kj) * tq
                    st = jnp.where(key <= qry, st, NEG)
                m_prev = m_sc[hh]
                m_new = jnp.maximum(m_prev, jnp.max(st, axis=0, keepdims=True))
                alpha = jnp.exp(m_prev - m_new)
                pt = jnp.exp(st - m_new)
                l_new = alpha * l_sc[hh] + jnp.sum(pt, axis=0, keepdims=True)
                acc = alpha * acc_sc[hh] + jnp.dot(vt_ref[vs, :], pt.astype(BF16), preferred_element_type=F32)
                if diagonal:
                    out_t = acc / l_new
                    o_ref[:, vs] = out_t.T.astype(o_ref.dtype)
                    ot_ref[vs, :] = out_t.astype(ot_ref.dtype)
                    lse_ref[hh] = m_new + jnp.log(l_new)
                else:
                    m_sc[hh] = m_new
                    l_sc[hh] = l_new
                    acc_sc[hh] = acc

        @pl.when(kj < qi // kr)
        def _():
            tile(False)

        @pl.when(kj == qi // kr)
        def _():
            tile(True)

    qspec = lambda w: pl.BlockSpec((tq, hb * w), lambda h, t, qi_ref, kj_ref: (qi_ref[t], h))
    kspec = lambda w: pl.BlockSpec((tk, hb * w), lambda h, t, qi_ref, kj_ref: (kj_ref[t], h))
    vt_spec = pl.BlockSpec((hb * MLA_V, tk), lambda h, t, qi_ref, kj_ref: (h, kj_ref[t]))
    ot_spec = pl.BlockSpec((hb * MLA_V, tq), lambda h, t, qi_ref, kj_ref: (h, qi_ref[t]))
    lse_spec = pl.BlockSpec((hb, 1, tq), lambda h, t, qi_ref, kj_ref: (h, 0, qi_ref[t]))
    n_pairs = qi_tab.shape[0]
    s_in, s_out, s_sems, s_alias = _side_parts(side, 5, 3)
    grid_spec = pltpu.PrefetchScalarGridSpec(
        num_scalar_prefetch=2, grid=(H // hb, n_pairs),
        in_specs=[qspec(MLA_QK_PAD), kspec(MLA_QK_PAD), vt_spec] + [ANY] * len(s_in),
        out_specs=[qspec(MLA_V), ot_spec, lse_spec] + [ANY] * len(s_out),
        scratch_shapes=[pltpu.VMEM((hb, 1, tq), F32), pltpu.VMEM((hb, 1, tq), F32),
                        pltpu.VMEM((hb, MLA_V, tq), F32)] + s_sems)
    step_of = lambda: pl.program_id(0) * n_pairs + pl.program_id(1)
    return pl.pallas_call(
        _carry(kern, side, 5, 3, 3, step_of, (H // hb) * n_pairs), name=name, grid_spec=grid_spec,
        out_shape=[_sds((T, H * MLA_V), BF16), _sds((H * MLA_V, T), BF16), _sds((H, 1, T), F32)] + s_out,
        input_output_aliases=s_alias, compiler_params=_params(2))(qi_tab, kj_tab, q, k, v.T, *s_in)


def _mla_bwd(q, k, v, do, do_t, o_t, lse, H, name, side=None):
    T = q.shape[0]
    tq = _pick(T, 512)
    nq = T // tq
    kr = 2 if nq % 2 == 0 else 1
    tk = kr * tq
    qi_tab, kj_tab = _causal_pairs(nq, key_major=True, kr=kr)

    def kern(qi_ref, kj_ref, q_ref, k_ref, v_ref, do_ref, dot_ref, ot_ref, lse_ref, dq_ref, dk_ref, dv_ref, dk_sc, dv_sc):
        t = pl.program_id(1)
        qi, kj = qi_ref[t], kj_ref[t]
        rows = pl.ds(pl.multiple_of(qi * tq, tq), tq)

        def tile(masked, first):
            st = lax.dot_general(k_ref[...], q_ref[...], NT, preferred_element_type=F32)
            if masked:
                key = lax.broadcasted_iota(jnp.int32, st.shape, 0)
                qry = lax.broadcasted_iota(jnp.int32, st.shape, 1) + (qi - kr * kj) * tq
                st = jnp.where(key <= qry, st, NEG)
            pt = jnp.exp(st - lse_ref[...])
            dl = jnp.sum(dot_ref[...].astype(F32) * ot_ref[...].astype(F32), axis=0, keepdims=True)
            dpt = jnp.dot(v_ref[...], dot_ref[...], preferred_element_type=F32)
            dst = (pt * (dpt - dl)).astype(BF16)
            dv = jnp.dot(pt.astype(BF16), do_ref[...], preferred_element_type=F32)
            dk = jnp.dot(dst, q_ref[...], preferred_element_type=F32)
            dq = lax.dot_general(dst, k_ref[...], TN, preferred_element_type=F32)
            if first:
                dv_sc[...] = dv
                dk_sc[...] = dk
            else:
                dv_sc[...] += dv
                dk_sc[...] += dk

            @pl.when(kj == 0)
            def _():
                dq_ref[rows, :] = dq

            @pl.when(kj > 0)
            def _():
                dq_ref[rows, :] += dq

        @pl.when(qi == kr * kj)
        def _():
            tile(True, True)

        if kr > 1:
            @pl.when((qi > kr * kj) & (qi < kr * kj + kr))
            def _():
                tile(True, False)

        @pl.when(qi >= kr * kj + kr)
        def _():
            tile(False, False)

        @pl.when(qi == nq - 1)
        def _():
            dk_ref[...] = dk_sc[...]
            dv_ref[...] = dv_sc[...]

    qspec = lambda w: pl.BlockSpec((tq, w), lambda h, t, qi_ref, kj_ref: (qi_ref[t], h))
    kspec = lambda w: pl.BlockSpec((tk, w), lambda h, t, qi_ref, kj_ref: (kj_ref[t], h))
    t_spec = pl.BlockSpec((MLA_V, tq), lambda h, t, qi_ref, kj_ref: (h, qi_ref[t]))
    lse_spec = pl.BlockSpec((None, 1, tq), lambda h, t, qi_ref, kj_ref: (h, 0, qi_ref[t]))
    n_pairs = qi_tab.shape[0]
    s_in, s_out, s_sems, s_alias = _side_parts(side, 9, 3)
    grid_spec = pltpu.PrefetchScalarGridSpec(
        num_scalar_prefetch=2, grid=(H, n_pairs),
        in_specs=[qspec(MLA_QK_PAD), kspec(MLA_QK_PAD), kspec(MLA_V), qspec(MLA_V), t_spec, t_spec, lse_spec]
        + [ANY] * len(s_in),
        out_specs=[pl.BlockSpec((T, MLA_QK_PAD), lambda h, t, qi_ref, kj_ref: (0, h)), kspec(MLA_QK_PAD), kspec(MLA_V)]
        + [ANY] * len(s_out),
        scratch_shapes=[pltpu.VMEM((tk, MLA_QK_PAD), F32), pltpu.VMEM((tk, MLA_V), F32)] + s_sems)
    step_of = lambda: pl.program_id(0) * n_pairs + pl.program_id(1)
    return pl.pallas_call(
        _carry(kern, side, 9, 3, 2, step_of, H * n_pairs), name=name, grid_spec=grid_spec,
        out_shape=[_sds((T, H * MLA_QK_PAD), F32), _sds((T, H * MLA_QK_PAD), F32), _sds((T, H * MLA_V), F32)] + s_out,
        input_output_aliases=s_alias, compiler_params=_params(2))(qi_tab, kj_tab, q, k, v, do, do_t, o_t, lse, *s_in)


class _DilGeometry:
    def __init__(self, T, dil, HD, grp):
        self.dil, self.sub = dil, max(1, 8 // dil)
        self.tb = self.sub * DIL_BLOCK * dil
        assert T % self.tb == 0, (T, dil)
        self.nblk = T // self.tb
        last = self.nblk - 1
        self.cur_g = pl.BlockSpec((self.tb, DIL_HEAD), lambda i, h: (i, grp * HD + h))
        self.prev_g = pl.BlockSpec((self.tb, DIL_HEAD), lambda i, h: (jnp.maximum(i - 1, 0), grp * HD + h))
        self.next_g = pl.BlockSpec((self.tb, DIL_HEAD), lambda i, h: (jnp.minimum(i + 1, last), grp * HD + h))
        self.cur = pl.BlockSpec((self.tb, DIL_HEAD), lambda i, h: (i, h))
        self.next = pl.BlockSpec((self.tb, DIL_HEAD), lambda i, h: (jnp.minimum(i + 1, last), h))

    def rows(self, b, r):
        if self.dil == 1:
            return pl.ds(b * DIL_BLOCK, DIL_BLOCK)
        return pl.ds(b * DIL_BLOCK * self.dil + r, DIL_BLOCK, stride=self.dil)

    def tiles(self):
        return [(b, r) for b in range(self.sub) for r in range(self.dil)]

    def rows2(self, b, r):
        if self.dil == 1:
            return pl.ds(b * DIL_BLOCK, 2 * DIL_BLOCK)
        return pl.ds(b * DIL_BLOCK * self.dil + r, 2 * DIL_BLOCK, stride=self.dil)

    def keys(self, cur_ref, prev_ref, b, r):
        if b > 0:
            return cur_ref[self.rows2(b - 1, r), :].astype(BF16)
        return jnp.concatenate([prev_ref[self.rows(self.sub - 1, r), :], cur_ref[self.rows(0, r), :]],
                               axis=0).astype(BF16)

    def masks(self, i):
        row = lax.broadcasted_iota(jnp.int32, (DIL_BLOCK, 2 * DIL_BLOCK), 0)
        col = lax.broadcasted_iota(jnp.int32, (DIL_BLOCK, 2 * DIL_BLOCK), 1)
        band = (col >= row) & (col <= row + DIL_BLOCK)
        first = band & (col >= jnp.where(i > 0, 0, DIL_BLOCK))
        no_next = jnp.where(i + 1 < self.nblk, 0, 2 * DIL_BLOCK)
        ok_next = col[:, :DIL_BLOCK] >= row[:, :DIL_BLOCK] + no_next
        return band, first, ok_next


def _twice(x):
    return jnp.concatenate([x, x], axis=1)


def _dil_fwd(qd, kd, vd, grp, dil, HD, name):
    T = qd.shape[0]
    W = HD * DIL_HEAD
    geo = _DilGeometry(T, dil, HD, grp)

    def kern(q_ref, kc_ref, kp_ref, vc_ref, vp_ref, o_ref, lse_ref):
        band, first, _ = geo.masks(pl.program_id(0))
        for b, r in geo.tiles():
            R = geo.rows(b, r)
            q = q_ref[R, :].astype(BF16)
            kk, vv = geo.keys(kc_ref, kp_ref, b, r), geo.keys(vc_ref, vp_ref, b, r)
            s = jnp.where(band if b > 0 else first, lax.dot_general(q, kk, NT, preferred_element_type=F32), NEG)
            m = jnp.max(s, axis=1, keepdims=True)
            e = jnp.exp(s - m)
            l = jnp.sum(e, axis=1, keepdims=True)
            o_ref[R, :] = jnp.dot(e.astype(BF16), vv, preferred_element_type=F32) / l
            lse_ref[R, :] = jnp.broadcast_to(m + jnp.log(l), (DIL_BLOCK, DIL_HEAD))

    return pl.pallas_call(
        kern, name=name, grid=(geo.nblk, HD), in_specs=[geo.cur_g, geo.cur_g, geo.prev_g, geo.cur_g, geo.prev_g],
        out_specs=[geo.cur, geo.cur], out_shape=[_sds((T, W), F32)] * 2,
        compiler_params=_params(2))(qd, kd, kd, vd, vd)


def _dil_delta(do, o, HD, name):
    T, W = do.shape
    tm = _pick(T, 512, 8)

    def body(d, ov):
        prod = d * ov.astype(F32)
        return jnp.concatenate(
            [jnp.broadcast_to(jnp.sum(prod[:, h * DIL_HEAD:(h + 1) * DIL_HEAD], axis=1, keepdims=True), (tm, DIL_HEAD))
             for h in range(HD)], axis=1)

    return _vcall(body, (T // tm,), (do, o), [_rows(tm, W)] * 2, [_sds((T, W), F32)], [_rows(tm, W)], name)[0]


def _dil_bwd(qd, kd, vd, do, delta, lse, grp, dil, HD, name):
    T = qd.shape[0]
    W = HD * DIL_HEAD
    geo = _DilGeometry(T, dil, HD, grp)

    def kern(q_ref, k_ref, v_ref, do_ref, dl_ref, ls_ref, kp_ref, vp_ref, qn_ref, don_ref, dln_ref, lsn_ref,
             dq_ref, dk_ref, dv_ref):
        band, first, ok_next = geo.masks(pl.program_id(0))

        def tile(q, do_, dl, ls, k, v, ok):
            s = jnp.where(ok, lax.dot_general(q, k, NT, preferred_element_type=F32), NEG)
            p = jnp.exp(s - ls)
            ds = p * (lax.dot_general(do_, v, NT, preferred_element_type=F32) - dl)
            return ds.astype(BF16), p.astype(BF16)

        dk_ref[...] = jnp.zeros_like(dk_ref)
        dv_ref[...] = jnp.zeros_like(dv_ref)
        for b, r in geo.tiles():
            R = geo.rows(b, r)
            q, do_ = q_ref[R, :].astype(BF16), do_ref[R, :].astype(BF16)
            kk, vv = geo.keys(k_ref, kp_ref, b, r), geo.keys(v_ref, vp_ref, b, r)
            ds, p_ = tile(q, do_, _twice(dl_ref[R, :]), _twice(ls_ref[R, :]), kk, vv, band if b > 0 else first)
            dq_ref[R, :] = jnp.dot(ds, kk, preferred_element_type=F32)
            dkk = lax.dot_general(ds, q, TN, preferred_element_type=F32)
            dvv = lax.dot_general(p_, do_, TN, preferred_element_type=F32)
            if b > 0:
                R2 = geo.rows2(b - 1, r)
                dk_ref[R2, :] += dkk
                dv_ref[R2, :] += dvv
            else:
                dk_ref[R, :] += dkk[DIL_BLOCK:]
                dv_ref[R, :] += dvv[DIL_BLOCK:]
        for r in range(dil):
            R, Rn = geo.rows(geo.sub - 1, r), geo.rows(0, r)
            qn, don = qn_ref[Rn, :].astype(BF16), don_ref[Rn, :].astype(BF16)
            ds, p_ = tile(qn, don, dln_ref[Rn, :], lsn_ref[Rn, :], k_ref[R, :].astype(BF16), v_ref[R, :].astype(BF16),
                          ok_next)
            dk_ref[R, :] += lax.dot_general(ds, qn, TN, preferred_element_type=F32)
            dv_ref[R, :] += lax.dot_general(p_, don, TN, preferred_element_type=F32)

    return pl.pallas_call(
        kern, name=name, grid=(geo.nblk, HD),
        in_specs=[geo.cur_g, geo.cur_g, geo.cur_g, geo.cur, geo.cur, geo.cur, geo.prev_g, geo.prev_g,
                  geo.next_g, geo.next, geo.next, geo.next],
        out_specs=[geo.cur] * 3, out_shape=[_sds((T, W), F32)] * 3,
        compiler_params=_params(2))(qd, kd, vd, do, delta, lse, kd, vd, qd, do, delta, lse)


def _place():
    return lax.axis_index("x"), lax.axis_index("y"), lax.axis_index("c")


def _other_chips(x, y):
    return [(1 - x, y), (x, 1 - y), (1 - x, 1 - y)]


def _kind(name, shard_shape):
    if name in ROW_SHARDED:
        return "row"
    return "col" if shard_shape[1] % LANES == 0 else "stack"


def _remote(src, dst, send_sem, recv_sem, to):
    return pltpu.make_async_remote_copy(src_ref=src, dst_ref=dst, send_sem=send_sem, recv_sem=recv_sem,
                                        device_id=to, device_id_type=MESH_ID)


def _row_tile(rows, cols, itemsize, align):
    return _pick(rows, max(align, (2 * 1024 * 1024) // (cols * itemsize)), align)


def _dma_sems(n):
    return [pltpu.SemaphoreType.DMA((n,)), pltpu.SemaphoreType.DMA((n,))]


def _gather_plan(shard_shapes):
    info, buf_shapes = {}, {}
    for out_name, names in GATHER_PLAN:
        r, c = shard_shapes[names[0]]
        kind = _kind(names[0], (r, c))
        assert kind == "col" or len(names) == 1, out_name
        buf_shapes[out_name] = (r, 4 * c * len(names)) if kind == "col" else (4, r, c)
        for i, n in enumerate(names):
            assert tuple(shard_shapes[n]) == (r, c), n
            info[n] = (out_name, kind, i * 4 * c, r, c)
    return info, buf_shapes


def _place_own(shard, buf_shape, kind, base, me, name, prev=None):
    r, c = shard.shape
    tr = _row_tile(r, c, 2, 16)

    def kern(me_ref, x_ref, *rest):
        rest[-1][...] = x_ref[...]

    if kind == "col":
        out_spec = pl.BlockSpec((tr, c), lambda i, me_ref: (i, base // c + me_ref[0]))
    else:
        out_spec = pl.BlockSpec((None, tr, c), lambda i, me_ref: (me_ref[0], i, 0))
    in_specs = [pl.BlockSpec((tr, c), lambda i, me_ref: (i, 0))] + ([ANY] if prev is not None else [])
    grid_spec = pltpu.PrefetchScalarGridSpec(num_scalar_prefetch=1, grid=(r // tr,), in_specs=in_specs,
                                             out_specs=out_spec)
    args = (me, shard) + ((prev,) if prev is not None else ())
    return pl.pallas_call(kern, name=name, grid_spec=grid_spec, out_shape=_sds(buf_shape, shard.dtype),
                          input_output_aliases={2: 0} if prev is not None else {}, compiler_params=_params(1))(*args)


def _ag_entry(e):
    return e if isinstance(e, tuple) else (e, 0, 1)


def _buffers_of(names, info):
    out_names = []
    for n in [_ag_entry(e)[0] for e in names]:
        if info[n][0] not in out_names:
            out_names.append(info[n][0])
    return out_names


def _ag_side(names, shards, bufs, info):
    entries = [_ag_entry(e) for e in names]
    names = [e[0] for e in entries]
    out_names = _buffers_of(names, info)
    n_w = len(names)

    def rows_of(w, h):
        r = info[names[w]][3]
        _, p, parts = entries[w]
        size = r // (2 * parts)
        return pl.ds(h * (r // 2) + p * size, size)

    def region(outs, w, chip, h):
        out_name, kind, base, r, cc = info[names[w]]
        o = outs[out_names.index(out_name)]
        if kind == "col":
            return o.at[rows_of(w, h), pl.ds(pl.multiple_of(base + chip * cc, LANES), cc)]
        return o.at[chip, rows_of(w, h), :]

    def hop(first, sending):
        def fn(ins, outs, sems):
            x, y, c = _place()
            me, sibling, cps = 2 * x + y, (x, y, 1 - c), []
            for w in range(n_w):
                for j, (px, py) in enumerate(_other_chips(x, y)):
                    k = 3 * w + j + (0 if first else 3 * n_w)
                    if first and sending:
                        src, dst, to = ins[w].at[rows_of(w, c), :], region(outs, w, me, c), (px, py, c)
                    elif first:
                        src = dst = region(outs, w, 2 * px + py, c)
                        to = (px, py, c)
                    else:
                        src = dst = region(outs, w, 2 * px + py, c if sending else 1 - c)
                        to = sibling
                    cps.append(_remote(src, dst, sems[0].at[k], sems[1].at[k], to))
            return cps
        return fn

    return _Side([shards[n] for n in names] + [bufs[o] for o in out_names],
                 [_sds(bufs[o].shape, bufs[o].dtype) for o in out_names], {n_w + i: i for i in range(len(out_names))},
                 _dma_sems(6 * n_w), [(hop(True, True), hop(True, False)), (hop(False, True), hop(False, False))])


def _rs_sibling_side(views):
    n = len(views)

    def fn(sending):
        def copies(ins, outs, sems):
            x, y, c = _place()
            return [_remote(ins[w].at[:, 1 - c] if sending else outs[w], outs[w], sems[0].at[w], sems[1].at[w],
                            (x, y, 1 - c)) for w in range(n)]
        return copies

    return _Side(views, [_sds((v.shape[0],) + v.shape[2:], v.dtype) for v in views], {}, _dma_sems(n),
                 [(fn(True), fn(False))])


def _rs_chips_side(parts, kinds, widths):
    n = len(parts)

    def piece(ins, w, chip):
        if kinds[w] == "col":
            return ins[w].at[0, :, pl.ds(pl.multiple_of(chip * widths[w], LANES), widths[w])]
        return ins[w].at[chip]

    def fn(sending):
        def copies(ins, outs, sems):
            x, y, c = _place()
            cps = []
            for w in range(n):
                for j, (px, py) in enumerate(_other_chips(x, y)):
                    k = 3 * w + j
                    src = piece(ins, w, 2 * px + py) if sending else outs[w].at[j]
                    cps.append(_remote(src, outs[w].at[j], sems[0].at[k], sems[1].at[k], (px, py, c)))
            return cps
        return copies

    return _Side(parts, [_sds((3, p_.shape[1], widths[w]), p_.dtype) for w, p_ in enumerate(parts)], {},
                 _dma_sems(3 * n), [(fn(True), fn(False))])


def _rs_join_side(halves):
    n = len(halves)

    def fn(sending):
        def copies(ins, outs, sems):
            x, y, c = _place()
            return [_remote(ins[w] if sending else outs[w], outs[w], sems[0].at[w], sems[1].at[w], (x, y, 1 - c))
                    for w in range(n)]
        return copies

    return _Side(halves, [_sds(h.shape, h.dtype) for h in halves], {}, _dma_sems(n), [(fn(True), fn(False))])


def _pair_sum(g, got, c_idx, name):
    n, _, rows, C = g.shape
    tr = _row_tile(rows, C, 2, 16)

    def kern(c_ref, a_ref, b_ref, o_ref):
        o_ref[...] = (a_ref[...].astype(F32) + b_ref[...].astype(F32)).astype(o_ref.dtype)

    grid_spec = pltpu.PrefetchScalarGridSpec(
        num_scalar_prefetch=1, grid=(n, rows // tr),
        in_specs=[pl.BlockSpec((None, None, tr, C), lambda j, i, c_ref: (j, c_ref[0], i, 0)),
                  pl.BlockSpec((None, tr, C), lambda j, i, c_ref: (j, i, 0))],
        out_specs=pl.BlockSpec((None, tr, C), lambda j, i, c_ref: (j, i, 0)))
    return pl.pallas_call(kern, name=name, grid_spec=grid_spec, out_shape=_sds((n, rows, C), BF16),
                          compiler_params=_params(2))(c_idx, g, got)


def _sum_pieces(pair, recv, kind, me, name):
    _, rows, c = recv.shape
    tr = _row_tile(rows, c, 8, 16)

    def kern(me_ref, own_ref, r_ref, o_ref):
        acc = own_ref[...].astype(F32)
        for j in range(3):
            acc = acc + r_ref[j].astype(F32)
        o_ref[...] = acc

    if kind == "col":
        own_spec = pl.BlockSpec((None, tr, c), lambda i, me_ref: (0, i, me_ref[0]))
    else:
        own_spec = pl.BlockSpec((None, tr, c), lambda i, me_ref: (me_ref[0], i, 0))
    grid_spec = pltpu.PrefetchScalarGridSpec(
        num_scalar_prefetch=1, grid=(rows // tr,),
        in_specs=[own_spec, pl.BlockSpec((3, tr, c), lambda i, me_ref: (0, i, 0))],
        out_specs=pl.BlockSpec((tr, c), lambda i, me_ref: (i, 0)))
    return pl.pallas_call(kern, name=name, grid_spec=grid_spec, out_shape=_sds((rows, c), F32),
                          compiler_params=_params(1))(me, pair, recv)


def _all_reduce_small(vec):
    N = vec.shape[1]
    n_dev = 8

    def body(v_ref, out_ref, slots, send_sems, recv_sems):
        x, y, c = _place()
        me = 4 * x + 2 * y + c
        slots[me] = v_ref[...]
        sent = []
        for k in range(1, n_dev):
            px, py, pc = x ^ (k >> 2), y ^ ((k >> 1) & 1), c ^ (k & 1)
            cp = pltpu.make_async_remote_copy(src_ref=v_ref, dst_ref=slots.at[me], send_sem=send_sems.at[k - 1],
                                              recv_sem=recv_sems.at[k - 1], device_id=(px, py, pc),
                                              device_id_type=MESH_ID)
            cp.start()
            sent.append(cp)
        for k in range(1, n_dev):
            px, py, pc = x ^ (k >> 2), y ^ ((k >> 1) & 1), c ^ (k & 1)
            slot = slots.at[4 * px + 2 * py + pc]
            pltpu.make_async_remote_copy(src_ref=slot, dst_ref=slot, send_sem=send_sems.at[k - 1],
                                         recv_sem=recv_sems.at[k - 1], device_id=(px, py, pc),
                                         device_id_type=MESH_ID).wait_recv()
        for cp in sent:
            cp.wait_send()
        acc = slots[0]
        for j in range(1, n_dev):
            acc = acc + slots[j]
        out_ref[...] = acc

    vm = pl.BlockSpec(memory_space=pltpu.VMEM)
    return pl.pallas_call(
        body, name="ar_gains", out_shape=_sds((1, N), F32), in_specs=[vm], out_specs=vm,
        scratch_shapes=[pltpu.VMEM((n_dev, 1, N), F32), pltpu.SemaphoreType.DMA((n_dev - 1,)),
                        pltpu.SemaphoreType.DMA((n_dev - 1,))])(vec)


def _adamw_math(wv, gv, mv, vv):
    m2 = ADAM_B1 * mv + (1.0 - ADAM_B1) * gv
    v2 = ADAM_B2 * vv + (1.0 - ADAM_B2) * (gv * gv)
    m_hat = m2 / (1.0 - ADAM_B1 ** ADAM_STEP)
    v_hat = v2 / (1.0 - ADAM_B2 ** ADAM_STEP)
    return -ADAM_LR * (m_hat / (jnp.sqrt(v_hat) + ADAM_EPS) + ADAM_WD * wv), m2, v2


def _adamw(w, g, m, v, name):
    R, C = w.shape
    tr = _row_tile(R, C, 8, 8)
    return _vcall(_adamw_math, (R // tr,), (w, g, m, v), [_rows(tr, C)] * 4, [_sds((R, C), F32)] * 3,
                  [_rows(tr, C)] * 3, name)


def _adamw_halves(w, own, recv, m, v, c_idx, name):
    R, C = w.shape
    rows = R // 2
    tr = _row_tile(rows, C, 8, 8)
    nb = rows // tr

    def kern(c_ref, w_ref, own_ref, recv_ref, m_ref, v_ref, g_out, d_out, m_out, v_out):
        def update(g_ref):
            g = g_ref[...]
            g_out[...] = g
            d_out[...], m_out[...], v_out[...] = _adamw_math(w_ref[...], g, m_ref[...], v_ref[...])

        @pl.when(pl.program_id(0) == c_ref[0])
        def _():
            update(own_ref)

        @pl.when(pl.program_id(0) != c_ref[0])
        def _():
            update(recv_ref)

    full = pl.BlockSpec((tr, C), lambda h, i, c_ref: (h * nb + i, 0))
    own_spec = pl.BlockSpec((tr, C), lambda h, i, c_ref: (jnp.where(h == c_ref[0], i, 0), 0))
    recv_spec = pl.BlockSpec((tr, C), lambda h, i, c_ref: (jnp.where(h == c_ref[0], 0, i), 0))
    grid_spec = pltpu.PrefetchScalarGridSpec(num_scalar_prefetch=1, grid=(2, nb),
                                             in_specs=[full, own_spec, recv_spec, full, full], out_specs=[full] * 4)
    return pl.pallas_call(kern, name=name, grid_spec=grid_spec, out_shape=[_sds((R, C), F32)] * 4,
                          compiler_params=_params(2))(c_idx, w, own, recv, m, v)


def _pad_to(a, n, axis):
    extra = n - a.shape[axis]
    if extra == 0:
        return a
    pads = [(0, 0)] * a.ndim
    pads[axis] = (0, extra)
    return jnp.pad(a, pads)


def _round_up(n, m):
    return -(-n // m) * m


def _natural(buf, kind):
    if kind == "col":
        return buf
    n, r, c = buf.shape
    return buf.reshape(n * r, c) if kind == "row" else buf.transpose(1, 0, 2).reshape(r, n * c)


def _halves_view(g, kind, shard_shape):
    r, c = shard_shape
    if kind == "col":
        return g.reshape(1, 2, r // 2, 4 * c)
    if kind == "stack":
        g = g.reshape(r, 4, c).transpose(1, 0, 2)
    return g.reshape(4, 2, r // 2, c)


def _pack_small(vals):
    return jnp.concatenate([_pad_to(vals[n].reshape(1, -1), _round_up(vals[n].size, LANES), 1) for n in SMALL], axis=1)


def _unpack_small(vec, shapes):
    out, off = {}, 0
    for n in SMALL:
        size = int(np.prod(shapes[n]))
        out[n] = vec[:, off:off + size].reshape(shapes[n])
        off += _round_up(size, LANES)
    return out


def _mm_s(sched, a, b, mode, out_dtype, name, **kw):
    side = sched.side(name)
    if side is None:
        return _mm(a, b, mode, out_dtype, name, **kw)
    out, side_outs = _mm(a, b, mode, out_dtype, name, side=side, **kw)
    sched.done(name, side_outs)
    return out


def _call_s(sched, name, n_out, fn):
    side = sched.side(name)
    outs = fn(side)
    if side is not None:
        sched.done(name, list(outs[n_out:]))
    return outs[:n_out]


def _ffn_fwd(sched, x, g, tf, tag):
    w = tag[-1]
    n = _rms_fwd(x, g, f"{tag}_norm")
    a, b, act = _call_s(sched, f"{tag}_gate_up", 3,
                        lambda side: _gate_up(n, sched.weight(f"w{w}_gu"), tf, f"{tag}_gate_up", side=side))
    out = _mm_s(sched, act, sched.weight(f"w{w}_down"), "nn", F32, f"{tag}_down", res=x, alpha=0.5)
    return out, (n, a, b, act)


def _ffn_bwd(sched, dout, dout_b, x, g, saved, tf, tag, with_bf16):
    w = tag[-1]
    w_gu, w_d = sched.weight(f"w{w}_gu"), sched.weight(f"w{w}_down")
    n, a, b, act = saved
    F = act.shape[1]
    sched.grad(f"w{w}_down", _mm_s(sched, act, dout_b, "tn", BF16, f"{tag}_d_wdown", alpha=0.5))
    da, db = _call_s(sched, f"{tag}_d_act", 2,
                     lambda side: _d_gate_up(dout_b, w_d, a, b, tf, f"{tag}_d_act", side=side))
    sched.grad(f"w{w}_gate", _mm_s(sched, n, da, "tn", BF16, f"{tag}_d_wgate"))
    sched.grad(f"w{w}_up", _mm_s(sched, n, db, "tn", BF16, f"{tag}_d_wup"))
    dn = _mm_s(sched, da, w_gu, "nt", F32, f"{tag}_d_norm", a2=db, b2=w_gu, b2_k_offset=F)
    return _call_s(sched, f"{tag}_d_x", 3 if with_bf16 else 2,
                   lambda side: _rms_bwd_call(x, g, dn, dout, f"{tag}_d_x", side=side, with_bf16=with_bf16))


def _local_step(sched, x, p, pos_b, target, Gn, dims):
    T, D = x.shape
    H, HD, QL, KVL, LP, tf = dims["H"], dims["HD"], dims["QL"], dims["KVL"], dims["LP"], dims["tf"]
    Wd = HD * DIL_HEAD
    scale_mla, scale_dil = MLA_QK ** -0.5, DIL_HEAD ** -0.5
    kr_col = (QL + KVL) // LANES
    tab_mla = tuple(_rope_tables(pos_b, MLA_ROPE, "rope_tab_mla"))
    tab_dil = tuple(_rope_tables(pos_b, DIL_ROT, "rope_tab_dil"))

    W = sched.weight
    mm = functools.partial(_mm_s, sched)

    x1, ffn1 = _ffn_fwd(sched, x, Gn["g_ffn1"], tf, "ffn1")
    h = _rms_fwd(x1, Gn["g_mix"], "mix_norm")
    lat = mm(h, W("w_lat"), "nn", F32, "proj_lat")
    pd = mm(h, W("w_dil"), "nn", BF16, "proj_dil")
    pg = mm(h, W("w_gin"), "nn", BF16, "proj_gate")

    cq, ckv = _lat_fwd(lat, Gn["g_cq"], Gn["g_ckv"], "lat_norm")
    q_raw = mm(cq, W("w_uq"), "nn", F32, "mla_q_up")
    kv = mm(ckv, W("w_ukv"), "nn", F32, "mla_kv_up")
    q = _mla_q_prep(q_raw, Gn["g_q_mla"], tab_mla, H, scale_mla, "mla_q_prep")
    k, v = _mla_k_prep(kv, lat, kr_col, Gn["g_k_mla"], tab_mla, H, "mla_k_prep")
    o_mla, o_mla_t, lse_mla = _call_s(sched, "mla_attn", 3, lambda side: _mla_fwd(q, k, v, H, "mla_attn", side=side))

    qd, kd, vd = _dil_prep(pd, Gn["g_q_dil"], Gn["g_k_dil"], tab_dil, HD, scale_dil, "dil_prep")
    og, lg = [], []
    for grp, (win, dil) in enumerate(DIL_GROUPS):
        o_, l_ = _dil_fwd(qd, kd, vd, grp, dil, HD, f"dil_attn{grp}")
        og.append(o_)
        lg.append(l_)
    o_dil, lse_dil = _dil_merge(og, lg, "dil_merge")

    bm = mm(o_mla, W("w_br_mla"), "nn", F32, "branch_mla")
    bd = mm(o_dil, W("w_br_dil"), "nn", F32, "branch_dil")
    merged = _gate_merge(pg, bm, bd, "gate_merge")
    x2 = mm(merged, W("w_o"), "nn", F32, "out_proj", res=x1)

    x3, ffn2 = _ffn_fwd(sched, x2, Gn["g_ffn2"], tf, "ffn2")
    n4 = _rms_fwd(x3, Gn["g_ple"], "ple_norm")
    zg = mm(n4, W("w_ple_gate"), "nn", F32, "ple_gate")
    p_b = p.astype(BF16)
    pp = mm(p_b, W("w_ple_proj"), "nn", F32, "ple_proj")
    dy, dpp, dzg, loss = _ple_loss(x3, zg, pp, target, "ple_loss")

    gg = {}
    sched.grad("w_ple_proj", mm(p_b, dpp, "tn", BF16, "d_w_ple_proj"))
    sched.grad("w_ple_gate", mm(n4, dzg, "tn", BF16, "d_w_ple_gate"))
    dn4 = mm(dzg, W("w_ple_gate"), "nt", F32, "d_ple_norm")
    dx3, dx3_b, gg["g_ple"] = _rms_bwd_call(x3, Gn["g_ple"], dn4, dy, "d_x3")

    dx2, dx2_b, gg["g_ffn2"] = _ffn_bwd(sched, dx3, dx3_b, x2, Gn["g_ffn2"], ffn2, tf, "ffn2", True)

    sched.grad("w_o", mm(merged, dx2_b, "tn", BF16, "d_w_o"))
    dmerged = mm(dx2_b, W("w_o"), "nt", F32, "d_merged")
    dbm, dbd, dpg = _gate_bwd(dmerged, pg, bm, bd, "d_gate")
    sched.grad("w_br_mla", mm(o_mla, dbm, "tn", BF16, "d_w_br_mla"))
    sched.grad("w_br_dil", mm(o_dil, dbd, "tn", BF16, "d_w_br_dil"))
    do_mla = mm(dbm, W("w_br_mla"), "nt", BF16, "d_o_mla")
    do_dil = mm(dbd, W("w_br_dil"), "nt", F32, "d_o_dil")
    delta_dil = _dil_delta(do_dil, o_dil, HD, "dil_delta")

    dh = mm(dpg, W("w_gin"), "nt", F32, "d_h_gate")
    sched.grad("w_gin", mm(h, dpg, "tn", BF16, "d_w_gin"))
    gq_d, gk_d = Gn["g_q_dil"], Gn["g_k_dil"]
    dgq_d, dgk_d, dpd = [], [], None
    for grp, (win, dil) in enumerate(DIL_GROUPS):
        dq_, dk_, dv_ = _dil_bwd(qd, kd, vd, do_dil, delta_dil, lse_dil, grp, dil, HD, f"dil_bwd{grp}")
        dpd, dgq_, dgk_ = _dil_prep_bwd(dq_, dk_, dv_, pd, grp, gq_d, gk_d, tab_dil, HD, scale_dil,
                                        f"d_dil_prep{grp}", prev=dpd)
        dgq_d.append(dgq_)
        dgk_d.append(dgk_)
    dh = mm(dpd, W("w_dil"), "nt", F32, "d_h_dil", res=dh)
    sched.grad("w_dil", mm(h, dpd, "tn", BF16, "d_w_dil"))
    gg["g_q_dil"] = jnp.concatenate(dgq_d, axis=0)
    gg["g_k_dil"] = jnp.concatenate(dgk_d, axis=0)

    dq, dk, dv = _call_s(sched, "mla_bwd", 3,
                         lambda side: _mla_bwd(q, k, v, do_mla, do_mla.T, o_mla_t, lse_mla, H, "mla_bwd", side=side))
    dq_raw, gg["g_q_mla"] = _mla_q_bwd(dq, q_raw, Gn["g_q_mla"], tab_mla, H, scale_mla, "d_mla_q_prep")
    dkv, dkr, gg["g_k_mla"] = _mla_k_bwd(dk, dv, kv, lat, kr_col, Gn["g_k_mla"], tab_mla, H, "d_mla_k_prep")
    sched.grad("w_uq", mm(cq, dq_raw, "tn", BF16, "d_w_uq"))
    sched.grad("w_ukv", mm(ckv, dkv, "tn", BF16, "d_w_ukv"))
    dcq = mm(dq_raw, W("w_uq"), "nt", F32, "d_cq")
    dckv = mm(dkv, W("w_ukv"), "nt", F32, "d_ckv")
    dlat, gg["g_cq"], gg["g_ckv"] = _lat_bwd(dcq, dckv, dkr, lat, Gn["g_cq"], Gn["g_ckv"], "d_lat_norm")
    dh = mm(dlat, W("w_lat"), "nt", F32, "d_h_lat", res=dh)
    sched.grad("w_lat", mm(h, dlat, "tn", BF16, "d_w_lat"))

    dx1, dx1_b, gg["g_mix"] = _rms_bwd_call(x1, Gn["g_mix"], dh, dx2, "d_x1")
    dx, gg["g_ffn1"] = _ffn_bwd(sched, dx1, dx1_b, x, Gn["g_ffn1"], ffn1, tf, "ffn1", False)
    return loss, dx, gg


def _layout_weight(name, full, dims):
    H, QL, KVL, LP, Wd = dims["H"], dims["QL"], dims["KVL"], dims["LP"], dims["HD"] * DIL_HEAD
    off_dil = QL + KVL + MLA_ROPE
    off_gate = off_dil + 3 * len(DIL_GROUPS) * Wd
    if name == "w_lat":
        return _pad_to(full("w_in")[:, :off_dil], LP, 1)
    if name == "w_dil":
        return full("w_in")[:, off_dil:off_gate]
    if name == "w_gin":
        return full("w_in")[:, off_gate:]
    if name == "w_uq":
        return _pad_to(full("w_uq").reshape(QL, H, MLA_QK), MLA_QK_PAD, 2).reshape(QL, H * MLA_QK_PAD)
    return full(name)


def _natural_grad(name, gw, dims):
    H, QL, KVL = dims["H"], dims["QL"], dims["KVL"]
    if name == "w_in":
        return jnp.concatenate([gw["w_lat"][:, :QL + KVL + MLA_ROPE], gw["w_dil"], gw["w_gin"]], axis=1)
    if name == "w_uq":
        return gw["w_uq"].reshape(QL, H, MLA_QK_PAD)[:, :, :MLA_QK].reshape(QL, H * MLA_QK)
    return gw[name]


WEIGHT_SOURCES = {"w1_gu": ("w1_gate", "w1_up"), "w2_gu": ("w2_gate", "w2_up"), "w_lat": ("w_in",), "w_dil": ("w_in",),
                  "w_gin": ("w_in",)}
AG_FIRST = ("w1_gate", "w1_up")
AG_RIDES = {"ffn1_gate_up": ("w1_down", ("w_in", 0, 2)), "ffn1_down": (("w_in", 1, 2), "w_uq", "w_ukv"),
            "mla_attn": ("w_br_mla", "w_br_dil", "w_o", "w_ple_gate", "w_ple_proj", "w2_gate", "w2_up", "w2_down")}
RS_FFN2 = ("w_ple_proj", "w_ple_gate", "w2_down", "w2_gate", "w2_up")
RS_MIXER = ("w_o", "w_br_mla", "w_br_dil", "w_in", "w_uq", "w_ukv")
RS_RIDES = {
    "d_merged": (("sibling", RS_FFN2),),
    "mla_bwd": (("chips", RS_FFN2),),
    "d_h_lat": (("join", RS_FFN2),),
    "ffn1_d_wdown": (("sibling", RS_MIXER),),
    "ffn1_d_act": (("chips", ("w_o", "w_br_mla", "w_br_dil", "w_uq", "w_ukv")), ("sibling", ("w1_down",))),
    "ffn1_d_wgate": (("chips", ("w1_down",)),),
    "ffn1_d_wup": (("sibling", ("w1_gate",)),),
    "ffn1_d_norm": (("chips", ("w_in", "w1_gate")), ("sibling", ("w1_up",))),
    "ffn1_d_x": (("chips", ("w1_up",)),),
}
RS_LAST = ((("join", RS_MIXER + ("w1_down", "w1_gate", "w1_up")),),)


class _MeshSchedule:
    def __init__(self, w, m, v, dims):
        self.w, self.m, self.v, self.dims = w, m, v, dims
        self.shapes = {n: tuple(w[n].shape[1:]) for n in BIG}
        self.kinds = {n: _kind(n, self.shapes[n]) for n in BIG}
        self.info, buf_shapes = _gather_plan(self.shapes)
        x, y, c = _place()
        self.me = (2 * x + y).astype(jnp.int32).reshape(1)
        self.c_idx = c.astype(jnp.int32).reshape(1)
        self.shards = {n: w[n][0].astype(BF16) for n in BIG}
        self.bufs, self.gathered, self.parts_done, self.layout = {}, set(), {}, {}
        for n in BIG:
            out_name, kind, base, _, _ = self.info[n]
            self.bufs[out_name] = _place_own(self.shards[n], buf_shapes[out_name], kind, base, self.me, f"ag_own_{n}",
                                             prev=self.bufs.get(out_name))
        self.gw, self.views, self.pairs, self.halves, self.recv = {}, {}, {}, {}, {}
        self._ag_done(AG_FIRST, _run_side(self._ag(AG_FIRST), "ag_first"))

    def _ag(self, names):
        return _ag_side(names, self.shards, self.bufs, self.info)

    def _ag_done(self, names, outs):
        for out_name, buf in zip(_buffers_of(names, self.info), outs):
            self.bufs[out_name] = buf
        for name, _, parts in [_ag_entry(e) for e in names]:
            self.parts_done[name] = self.parts_done.get(name, 0) + 1
            if self.parts_done[name] == parts:
                self.gathered.add(name)

    def weight(self, name):
        if name not in self.layout:
            assert all(s in self.gathered for s in WEIGHT_SOURCES.get(name, (name,))), name
            if name in self.bufs and name not in self.info:
                self.layout[name] = self.bufs[name]
            else:
                full = lambda n: _natural(self.bufs[self.info[n][0]], self.info[n][1])
                self.layout[name] = _layout_weight(name, full, self.dims)
        return self.layout[name]

    def grad(self, name, g):
        self.gw[name] = g

    def _rs_side(self, stages):
        sides = []
        for stage, names in stages:
            if stage == "sibling":
                for n in names:
                    self.views[n] = _halves_view(_natural_grad(n, self.gw, self.dims), self.kinds[n], self.shapes[n])
                sides.append(_rs_sibling_side([self.views[n] for n in names]))
            elif stage == "chips":
                sides.append(_rs_chips_side([self.pairs[n] for n in names], [self.kinds[n] for n in names],
                                            [self.shapes[n][1] for n in names]))
            else:
                sides.append(_rs_join_side([self.halves[n] for n in names]))
        return sides[0] if len(sides) == 1 else _merge_sides(sides)

    def _rs_done(self, stages, outs):
        for stage, names in stages:
            got, outs = outs[:len(names)], outs[len(names):]
            for n, a in zip(names, got):
                if stage == "sibling":
                    self.pairs[n] = _pair_sum(self.views[n], a, self.c_idx, f"rs_pair_{n}")
                elif stage == "chips":
                    self.halves[n] = _sum_pieces(self.pairs[n], a, self.kinds[n], self.me, f"rs_sum_{n}")
                else:
                    self.recv[n] = a

    def side(self, tag):
        if tag in AG_RIDES:
            return self._ag(AG_RIDES[tag])
        if tag in RS_RIDES:
            return self._rs_side(RS_RIDES[tag])
        return None

    def done(self, tag, outs):
        if tag in AG_RIDES:
            self._ag_done(AG_RIDES[tag], outs)
        else:
            self._rs_done(RS_RIDES[tag], outs)

    def finish(self):
        for k, stages in enumerate(RS_LAST):
            self._rs_done(stages, _run_side(self._rs_side(stages), f"rs_last{k}"))
        outs = {"grad": {}, "delta": {}, "m": {}, "v": {}}
        for n in BIG:
            res = _adamw_halves(self.w[n][0], self.halves[n], self.recv[n], self.m[n][0], self.v[n][0], self.c_idx,
                                f"adamw_{n}")
            for kind, a in zip(("grad", "delta", "m", "v"), res):
                outs[kind][n] = a.reshape((1,) + a.shape)
        return outs


def _step(x, p, positions, loss_target, w, m, v):
    T, D = x.shape[1], x.shape[2]
    QL, KVL = w["g_cq"].shape[1], w["g_ckv"].shape[1]
    dims = {
        "H": 4 * w["w_uq"].shape[2] // MLA_QK, "HD": w["w_br_dil"].shape[1] // DIL_HEAD, "QL": QL, "KVL": KVL,
        "LP": _round_up(QL + KVL + MLA_ROPE, LANES), "tf": _pick(4 * w["w1_gate"].shape[2], 512),
    }
    small_shapes = {n: w[n].shape for n in SMALL}
    sched = _MeshSchedule(w, m, v, dims)
    Gn = {n: w[n] for n in SMALL}
    Gn["g_q_mla"] = _pad_to(Gn["g_q_mla"], MLA_QK_PAD, 1)
    Gn["g_k_mla"] = _pad_to(Gn["g_k_mla"], MLA_QK_PAD, 1)
    Gn["g_q_dil"] = Gn["g_q_dil"].reshape(len(DIL_GROUPS), 1, DIL_HEAD)
    Gn["g_k_dil"] = Gn["g_k_dil"].reshape(len(DIL_GROUPS), 1, DIL_HEAD)

    pos_b = jnp.broadcast_to(positions.astype(F32).reshape(T, 1), (T, LANES))
    loss, dx, gg = _local_step(sched, x[0], p[0, 0], pos_b, loss_target[0], Gn, dims)
    loss = lax.psum(loss[0, 0], ("x", "y", "c"))
    outs = sched.finish()

    gg["g_q_mla"] = gg["g_q_mla"][:, :MLA_QK]
    gg["g_k_mla"] = gg["g_k_mla"][:, :MLA_QK]
    g_small = _all_reduce_small(_pack_small(gg))
    d_s, m_s, v_s = _adamw(_pack_small({n: w[n] for n in SMALL}), g_small, _pack_small({n: m[n] for n in SMALL}),
                           _pack_small({n: v[n] for n in SMALL}), "adamw_gains")
    for kind, buf in (("grad", g_small), ("delta", d_s), ("m", m_s), ("v", v_s)):
        outs[kind].update(_unpack_small(buf, small_shapes))

    grad_x = dx.reshape(1, T, D)
    return (loss, grad_x, *[outs["grad"][n] for n in WEIGHTS], *[outs["delta"][n] for n in WEIGHTS],
            *[outs["m"][n] for n in WEIGHTS], *[outs["v"][n] for n in WEIGHTS])


def kernel(x, p, positions, g_ffn1, w1_gate, w1_up, w1_down, g_mix, w_in, g_cq, w_uq, g_ckv, w_ukv, g_q_mla, g_k_mla, g_q_dil, g_k_dil, w_br_mla, w_br_dil, w_o, g_ffn2, w2_gate, w2_up, w2_down, g_ple, w_ple_gate, w_ple_proj, loss_target, m_g_ffn1, m_w1_gate, m_w1_up, m_w1_down, m_g_mix, m_w_in, m_g_cq, m_w_uq, m_g_ckv, m_w_ukv, m_g_q_mla, m_g_k_mla, m_g_q_dil, m_g_k_dil, m_w_br_mla, m_w_br_dil, m_w_o, m_g_ffn2, m_w2_gate, m_w2_up, m_w2_down, m_g_ple, m_w_ple_gate, m_w_ple_proj, v_g_ffn1, v_w1_gate, v_w1_up, v_w1_down, v_g_mix, v_w_in, v_g_cq, v_w_uq, v_g_ckv, v_w_ukv, v_g_q_mla, v_g_k_mla, v_g_q_dil, v_g_k_dil, v_w_br_mla, v_w_br_dil, v_w_o, v_g_ffn2, v_w2_gate, v_w2_up, v_w2_down, v_g_ple, v_w_ple_gate, v_w_ple_proj):
    args = locals()
    w = {n: args[n] for n in WEIGHTS}
    m = {n: args["m_" + n] for n in WEIGHTS}
    v = {n: args["v_" + n] for n in WEIGHTS}
    return _step(x, p, positions, loss_target, w, m, v)
```

```python
import functools

import numpy as np
import jax
import jax.numpy as jnp
from jax import lax
from jax.experimental import pallas as pl
from jax.experimental.pallas import tpu as pltpu

F32 = jnp.float32
BF16 = jnp.bfloat16
MESH_ID = pl.DeviceIdType.MESH

MLA_NOPE = 128
MLA_ROPE = 64
MLA_V = 128
MLA_QK = MLA_NOPE + MLA_ROPE
MLA_QK_PAD = 256
DIL_GROUPS = ((128, 1), (512, 4), (2048, 16))
DIL_HEAD = 128
DIL_ROT = DIL_HEAD // 4
DIL_BLOCK = 128
ROPE_THETA = 500000.0
EPS = 1e-6
NEG = -1e30
ADAM_LR = 0.001
ADAM_B1 = 0.9
ADAM_B2 = 0.999
ADAM_EPS = 1e-08
ADAM_WD = 0.01
ADAM_STEP = 10

LANES = 128
VMEM_LIMIT_BYTES = 56 * 1024 * 1024
MM_VMEM_BYTES = 46 * 1024 * 1024

BIG = ("w1_gate", "w1_up", "w1_down", "w_in", "w_uq", "w_ukv", "w_br_mla", "w_br_dil", "w_o",
       "w2_gate", "w2_up", "w2_down", "w_ple_gate", "w_ple_proj")
GATHER_PLAN = (("w1_gu", ("w1_gate", "w1_up")), ("w1_down", ("w1_down",)), ("w_in", ("w_in",)), ("w_uq", ("w_uq",)),
               ("w_ukv", ("w_ukv",)), ("w_br_mla", ("w_br_mla",)), ("w_br_dil", ("w_br_dil",)), ("w_o", ("w_o",)),
               ("w2_gu", ("w2_gate", "w2_up")), ("w2_down", ("w2_down",)), ("w_ple_gate", ("w_ple_gate",)),
               ("w_ple_proj", ("w_ple_proj",)))
ROW_SHARDED = ("w1_down", "w_o", "w2_down", "w_ple_gate")
SMALL = ("g_ffn1", "g_mix", "g_cq", "g_ckv", "g_q_mla", "g_k_mla", "g_q_dil", "g_k_dil", "g_ffn2", "g_ple")
WEIGHTS = ("g_ffn1", "w1_gate", "w1_up", "w1_down", "g_mix", "w_in", "g_cq", "w_uq", "g_ckv", "w_ukv", "g_q_mla",
           "g_k_mla", "g_q_dil", "g_k_dil", "w_br_mla", "w_br_dil", "w_o", "g_ffn2", "w2_gate", "w2_up", "w2_down",
           "g_ple", "w_ple_gate", "w_ple_proj")


def _pick(n, target, align=LANES):
    if n <= target:
        return n
    t = (target // align) * align
    while t >= align:
        if n % t == 0:
            return t
        t -= align
    return n


def _params(n_axes):
    return pltpu.CompilerParams(dimension_semantics=("arbitrary",) * n_axes, vmem_limit_bytes=VMEM_LIMIT_BYTES)


def _sigmoid(x):
    return 0.5 * jnp.tanh(0.5 * x) + 0.5


ANY = pl.BlockSpec(memory_space=pl.ANY)


class _Side:
    def __init__(self, arrays, out_shapes, aliases, sem_shapes, phases):
        self.arrays, self.out_shapes, self.aliases = list(arrays), list(out_shapes), dict(aliases)
        self.sem_shapes, self.phases = list(sem_shapes), list(phases)

    def start(self, p, ins, outs, sems):
        for cp in self.phases[p][0](ins, outs, sems):
            cp.start()

    def wait(self, p, ins, outs, sems):
        for cp in self.phases[p][1](ins, outs, sems):
            cp.wait_recv()
        for cp in self.phases[p][0](ins, outs, sems):
            cp.wait_send()

    def run(self, step, n_steps, ins, outs, sems):
        n_ph = len(self.phases)
        assert n_ph <= 2
        starts = (0, int(0.85 * (n_steps - 1)))
        if n_steps <= n_ph:
            @pl.when(step == n_steps - 1)
            def _():
                for p in range(n_ph):
                    self.start(p, ins, outs, sems)
                    self.wait(p, ins, outs, sems)
            return
        for p in range(n_ph):
            @pl.when(step == starts[p])
            def _(p=p):
                if p > 0:
                    self.wait(p - 1, ins, outs, sems)
                self.start(p, ins, outs, sems)

        @pl.when(step == n_steps - 1)
        def _():
            self.wait(n_ph - 1, ins, outs, sems)


def _merge_sides(sides):
    arrays, out_shapes, aliases, sem_shapes, spans = [], [], {}, [], []
    for s in sides:
        assert len(s.phases) == 1
        spans.append((len(arrays), len(out_shapes), len(sem_shapes), s))
        aliases.update({len(arrays) + i: len(out_shapes) + o for i, o in s.aliases.items()})
        arrays += s.arrays
        out_shapes += s.out_shapes
        sem_shapes += s.sem_shapes

    def part(which):
        def fn(ins, outs, sems):
            cps = []
            for a0, o0, s0, s in spans:
                cps += s.phases[0][which](ins[a0:a0 + len(s.arrays)], outs[o0:o0 + len(s.out_shapes)],
                                          sems[s0:s0 + len(s.sem_shapes)])
            return cps
        return fn

    return _Side(arrays, out_shapes, aliases, sem_shapes, [(part(0), part(1))])


def _side_parts(side, n_lead, n_out):
    if side is None:
        return [], [], [], {}
    return (side.arrays, side.out_shapes, side.sem_shapes, {n_lead + i: n_out + o for i, o in side.aliases.items()})


def _carry(kern, side, n_lead, n_out, n_scratch, step_of, n_steps):
    if side is None:
        return kern
    a = n_lead
    b = a + len(side.arrays)
    c = b + n_out
    d = c + len(side.out_shapes)
    e = d + n_scratch

    def wrapped(*refs):
        side.run(step_of(), n_steps, refs[a:b], refs[c:d], refs[e:])
        kern(*refs[:a], *refs[b:c], *refs[d:e])

    return wrapped


def _run_side(side, name):
    n_in, n_out = len(side.arrays), len(side.out_shapes)

    def body(*refs):
        ins, outs, sems = refs[:n_in], refs[n_in:n_in + n_out], refs[n_in + n_out:]
        for p in range(len(side.phases)):
            side.start(p, ins, outs, sems)
            side.wait(p, ins, outs, sems)

    return pl.pallas_call(body, name=name, out_shape=side.out_shapes, in_specs=[ANY] * n_in, out_specs=[ANY] * n_out,
                          scratch_shapes=side.sem_shapes, input_output_aliases=side.aliases)(*side.arrays)


def _mm_tiles(M, N, K, n_pairs, out_bytes, has_res):
    tm = _pick(M, 1024)
    tks = sorted({_pick(K, t) for t in (8192, 5632, 4096, 2816, 2048, 1408, 1024, 512)}, reverse=True)
    tns = sorted({_pick(N, t) for t in (1536, 1024, 512)}, reverse=True)
    for tk in tks:
        for tn in tns:
            need = 4 * n_pairs * (tm * tk + tk * tn) + tm * tn * (4 * (K > tk) + 2 * out_bytes + 8 * has_res + 4)
            if need <= MM_VMEM_BYTES:
                return tm, tn, tk
    raise ValueError((M, N, K))


def _mm(a, b, mode, out_dtype, name, res=None, alpha=1.0, a2=None, b2=None, b2_k_offset=0, side=None):
    if mode == "nn":
        (M, K), (K2, N) = a.shape, b.shape
    elif mode == "nt":
        (M, K), (N, K2) = a.shape, b.shape
    else:
        (K, M), (K2, N) = a.shape, b.shape
    assert K == K2 or (mode == "nt" and K2 > K), (name, a.shape, b.shape)
    assert a.dtype == BF16 and b.dtype == BF16, name
    tm, tn, tk = _mm_tiles(M, N, K, 1 if a2 is None else 2, jnp.dtype(out_dtype).itemsize, res is not None)
    nk = K // tk
    assert b2_k_offset % tk == 0 and (b2_k_offset == 0 or mode == "nt"), name
    k_off2 = b2_k_offset // tk
    if mode == "nn":
        a_spec = pl.BlockSpec((tm, tk), lambda i, j, k: (i, k))
        b_spec = pl.BlockSpec((tk, tn), lambda i, j, k: (k, j))
        dims = (((1,), (0,)), ((), ()))
    elif mode == "nt":
        a_spec = pl.BlockSpec((tm, tk), lambda i, j, k: (i, k))
        b_spec = pl.BlockSpec((tn, tk), lambda i, j, k: (j, k))
        b2_spec = pl.BlockSpec((tn, tk), lambda i, j, k: (j, k + k_off2))
        dims = (((1,), (1,)), ((), ()))
    else:
        a_spec = pl.BlockSpec((tk, tm), lambda i, j, k: (k, i))
        b_spec = pl.BlockSpec((tk, tn), lambda i, j, k: (k, j))
        dims = (((0,), (0,)), ((), ()))
    o_spec = pl.BlockSpec((tm, tn), lambda i, j, k: (i, j))
    has_res = res is not None
    n_pairs = 1 if a2 is None else 2
    n_main = 2 * n_pairs + int(has_res)
    n_side_in = len(side.arrays) if side else 0
    n_side_out = len(side.out_shapes) if side else 0
    n_acc = 1 if nk > 1 else 0
    gi, gj = M // tm, N // tn
    n_steps = gi * gj * nk

    def kern(*refs):
        r_ref = refs[2 * n_pairs] if has_res else None
        o_ref = refs[n_main + n_side_in]
        if side:
            step = (pl.program_id(0) * gj + pl.program_id(1)) * nk + pl.program_id(2)
            side.run(step, n_steps, refs[n_main:n_main + n_side_in],
                     refs[n_main + n_side_in + 1:n_main + n_side_in + 1 + n_side_out],
                     refs[n_main + n_side_in + 1 + n_side_out + n_acc:])
        part = lax.dot_general(refs[0][...], refs[1][...], dims, preferred_element_type=F32)
        if n_pairs == 2:
            part = part + lax.dot_general(refs[2][...], refs[3][...], dims, preferred_element_type=F32)

        def finish(r):
            if alpha != 1.0:
                r = r * alpha
            if has_res:
                r = r_ref[...] + r
            o_ref[...] = r.astype(o_ref.dtype)

        if nk == 1:
            finish(part)
            return
        acc_ref = refs[n_main + n_side_in + 1 + n_side_out]
        k = pl.program_id(2)

        @pl.when(k == 0)
        def _():
            acc_ref[...] = part

        @pl.when(k > 0)
        def _():
            acc_ref[...] += part

        @pl.when(k == nk - 1)
        def _():
            finish(acc_ref[...])

    ins = (a, b) + ((a2, b2) if n_pairs == 2 else ()) + ((res,) if has_res else ())
    in_specs = [a_spec, b_spec] + ([a_spec, b2_spec if mode == "nt" else b_spec] if n_pairs == 2 else [])
    in_specs += [o_spec] if has_res else []
    out_shape = jax.ShapeDtypeStruct((M, N), out_dtype)
    scratch = [pltpu.VMEM((tm, tn), F32)] if nk > 1 else []
    if not side:
        return pl.pallas_call(kern, name=name, grid=(gi, gj, nk), in_specs=in_specs, out_specs=o_spec,
                              out_shape=out_shape, scratch_shapes=scratch, compiler_params=_params(3))(*ins)
    outs = pl.pallas_call(
        kern, name=name, grid=(gi, gj, nk), in_specs=in_specs + [ANY] * n_side_in,
        out_specs=[o_spec] + [ANY] * n_side_out, out_shape=[out_shape] + list(side.out_shapes),
        scratch_shapes=scratch + list(side.sem_shapes),
        input_output_aliases={n_main + i: 1 + o for i, o in side.aliases.items()},
        compiler_params=_params(3))(*ins, *side.arrays)
    return outs[0], list(outs[1:])


def _vcall(body, grid, ins, in_specs, out_shapes, out_specs, name, n_inner_acc=0, n_acc=0, side=None, carried=None):
    n_body = len(ins)
    if carried is not None:
        ins, in_specs = tuple(ins) + (carried,), list(in_specs) + [ANY]
    n_in, n_out = len(ins), len(out_shapes)
    n_plain = n_out - n_acc - n_inner_acc

    def kern(*refs):
        vals = body(*[r[...] for r in refs[:n_body]])
        if not isinstance(vals, (tuple, list)):
            vals = (vals,)
        out_refs = refs[n_in:]
        inner_first = pl.program_id(len(grid) - 1) == 0
        first = inner_first
        for ax in range(len(grid) - 1):
            first = jnp.logical_and(first, pl.program_id(ax) == 0)
        for idx, (r, v) in enumerate(zip(out_refs, vals)):
            if idx < n_plain:
                r[...] = v.astype(r.dtype)
                continue
            start = inner_first if idx < n_plain + n_inner_acc else first

            @pl.when(start)
            def _(r=r, v=v):
                r[...] = v.astype(r.dtype)

            @pl.when(jnp.logical_not(start))
            def _(r=r, v=v):
                r[...] += v.astype(r.dtype)

    s_in, s_out, s_sems, s_alias = _side_parts(side, n_in, n_out)
    if carried is not None:
        s_alias = {**s_alias, n_body: 0}

    def step_of():
        step = pl.program_id(0)
        for ax in range(1, len(grid)):
            step = step * grid[ax] + pl.program_id(ax)
        return step

    return pl.pallas_call(
        _carry(kern, side, n_in, n_out, 0, step_of, int(np.prod(grid))), name=name, grid=grid,
        in_specs=list(in_specs) + [ANY] * len(s_in), out_specs=list(out_specs) + [ANY] * len(s_out),
        out_shape=list(out_shapes) + list(s_out), scratch_shapes=s_sems, input_output_aliases=s_alias,
        compiler_params=_params(len(grid)))(*ins, *s_in)


def _rows(tm, c):
    return pl.BlockSpec((tm, c), lambda i: (i, 0))


def _vec(c):
    return pl.BlockSpec((1, c), lambda i: (0, 0))


def _sds(shape, dtype):
    return jax.ShapeDtypeStruct(shape, dtype)


def _rstd(x, c):
    return lax.rsqrt(jnp.sum(x * x, axis=-1, keepdims=True) * (1.0 / c) + EPS)


def _rms_bwd(xh, r, g, dn, c):
    u = dn * g
    dx = r * (u - xh * (jnp.sum(xh * u, axis=-1, keepdims=True) * (1.0 / c)))
    return dx, jnp.sum(dn * xh, axis=0, keepdims=True)


def _rope(t, c, sa, sb, half):
    return t * c + pltpu.roll(t, LANES - half, 1) * sa + pltpu.roll(t, half, 1) * sb


def _rope_t(d, c, sa, sb, half):
    return d * c + pltpu.roll(d * sa, half, 1) + pltpu.roll(d * sb, LANES - half, 1)


def _rope_tables(pos_b, rd, name):
    T = pos_b.shape[0]
    half = rd // 2
    inv = ROPE_THETA ** (-jnp.arange(half, dtype=F32) * 2.0 / rd)
    inv_full = jnp.concatenate([inv, inv, jnp.zeros((LANES - rd,), F32)]).reshape(1, LANES)
    lane = np.arange(LANES)
    ma = jnp.asarray((lane < half).astype(np.float32)).reshape(1, LANES)
    mb = jnp.asarray(((lane >= half) & (lane < rd)).astype(np.float32)).reshape(1, LANES)
    tm = _pick(T, 1024, 8)

    def body(pos, invf, a, b):
        ang = pos * invf
        c, s = jnp.cos(ang), jnp.sin(ang)
        inside = a + b
        return c * inside + (1.0 - inside), -s * a, s * b

    return _vcall(body, (T // tm,), (pos_b, inv_full, ma, mb), [_rows(tm, LANES)] + [_vec(LANES)] * 3,
                  [_sds((T, LANES), F32)] * 3, [_rows(tm, LANES)] * 3, name)


def _rms_fwd(x, g, name, side=None):
    T, C = x.shape
    tm = _pick(T, 512, 8)

    def body(xv, gv):
        return xv * _rstd(xv, C) * gv

    outs = _vcall(body, (T // tm,), (x, g), [_rows(tm, C), _vec(C)], [_sds((T, C), BF16)], [_rows(tm, C)], name,
                  side=side)
    return outs if side is not None else outs[0]


def _rms_bwd_call(x, g, dn, dres, name, side=None, with_bf16=True):
    T, C = x.shape
    tm = _pick(T, 256, 8)

    def body(xv, gv, dnv, drv):
        r = _rstd(xv, C)
        dx, dg = _rms_bwd(xv * r, r, gv, dnv.astype(F32), C)
        dx = drv + dx
        return (dx, dx, dg) if with_bf16 else (dx, dg)

    n_dx = 2 if with_bf16 else 1
    return _vcall(body, (T // tm,), (x, g, dn, dres), [_rows(tm, C), _vec(C), _rows(tm, C), _rows(tm, C)],
                  [_sds((T, C), F32), _sds((T, C), BF16)][:n_dx] + [_sds((1, C), F32)],
                  [_rows(tm, C)] * n_dx + [_vec(C)], name, n_acc=1, side=side)


def _gate_up(n, w_gu, tf, name, side=None):
    T, D = n.shape
    F = w_gu.shape[1] // 2
    tm = _pick(T, 1024, 16)
    nf = F // tf

    def kern(n_ref, wg_ref, wu_ref, a_ref, b_ref, act_ref):
        x = n_ref[...]
        a = jnp.dot(x, wg_ref[...], preferred_element_type=F32)
        b = jnp.dot(x, wu_ref[...], preferred_element_type=F32)
        a_ref[...] = a.astype(BF16)
        b_ref[...] = b.astype(BF16)
        act_ref[...] = (a * _sigmoid(a) * b).astype(BF16)

    tile = pl.BlockSpec((tm, tf), lambda i, j: (i, j))
    s_in, s_out, s_sems, s_alias = _side_parts(side, 3, 3)
    step_of = lambda: pl.program_id(0) * nf + pl.program_id(1)
    outs = pl.pallas_call(
        _carry(kern, side, 3, 3, 0, step_of, (T // tm) * nf), name=name, grid=(T // tm, nf),
        in_specs=[pl.BlockSpec((tm, D), lambda i, j: (i, 0)), pl.BlockSpec((D, tf), lambda i, j: (0, j)),
                  pl.BlockSpec((D, tf), lambda i, j: (0, j + nf))] + [ANY] * len(s_in),
        out_specs=[tile] * 3 + [ANY] * len(s_out), out_shape=[_sds((T, F), BF16)] * 3 + s_out,
        scratch_shapes=s_sems, input_output_aliases=s_alias, compiler_params=_params(2))(n, w_gu, w_gu, *s_in)
    return outs


def _d_gate_up(dout_b, w_d, a, b, tf, name, side=None):
    T, D = dout_b.shape
    F = w_d.shape[0]
    tm = _pick(T, 1024, 16)
    nf = F // tf

    def kern(d_ref, w_ref, a_ref, b_ref, da_ref, db_ref):
        d = 0.5 * lax.dot_general(d_ref[...], w_ref[...], NT, preferred_element_type=F32)
        a, b = a_ref[...].astype(F32), b_ref[...].astype(F32)
        sg = _sigmoid(a)
        da_ref[...] = (d * b * (sg * (1.0 + a * (1.0 - sg)))).astype(BF16)
        db_ref[...] = (d * (a * sg)).astype(BF16)

    tile = pl.BlockSpec((tm, tf), lambda i, j: (i, j))
    s_in, s_out, s_sems, s_alias = _side_parts(side, 4, 2)
    step_of = lambda: pl.program_id(0) * nf + pl.program_id(1)
    outs = pl.pallas_call(
        _carry(kern, side, 4, 2, 0, step_of, (T // tm) * nf), name=name, grid=(T // tm, nf),
        in_specs=[pl.BlockSpec((tm, D), lambda i, j: (i, 0)), pl.BlockSpec((tf, D), lambda i, j: (j, 0)), tile, tile]
        + [ANY] * len(s_in),
        out_specs=[tile] * 2 + [ANY] * len(s_out), out_shape=[_sds((T, F), BF16)] * 2 + s_out,
        scratch_shapes=s_sems, input_output_aliases=s_alias, compiler_params=_params(2))(dout_b, w_d, a, b, *s_in)
    return outs


def _lat_fwd(lat, g_cq, g_ckv, name):
    T, LP = lat.shape
    QL, KVL = g_cq.shape[1], g_ckv.shape[1]
    tm = _pick(T, 512, 8)

    def body(v, gq, gk):
        xq, xk = v[:, :QL], v[:, QL:QL + KVL]
        return xq * _rstd(xq, QL) * gq, xk * _rstd(xk, KVL) * gk

    return _vcall(body, (T // tm,), (lat, g_cq, g_ckv), [_rows(tm, LP), _vec(QL), _vec(KVL)],
                  [_sds((T, QL), BF16), _sds((T, KVL), BF16)], [_rows(tm, QL), _rows(tm, KVL)], name)


def _lat_bwd(dcq, dckv, dkr, lat, g_cq, g_ckv, name):
    T, LP = lat.shape
    QL, KVL = g_cq.shape[1], g_ckv.shape[1]
    tm = _pick(T, 512, 8)

    def body(dq, dk, dr, v, gq, gk):
        xq, xk = v[:, :QL], v[:, QL:QL + KVL]
        rq, rk = _rstd(xq, QL), _rstd(xk, KVL)
        dxq, dgq = _rms_bwd(xq * rq, rq, gq, dq, QL)
        dxk, dgk = _rms_bwd(xk * rk, rk, gk, dk, KVL)
        return jnp.concatenate([dxq, dxk, dr], axis=1), dgq, dgk

    return _vcall(body, (T // tm,), (dcq, dckv, dkr, lat, g_cq, g_ckv),
                  [_rows(tm, QL), _rows(tm, KVL), _rows(tm, LANES), _rows(tm, LP), _vec(QL), _vec(KVL)],
                  [_sds((T, LP), BF16), _sds((1, QL), F32), _sds((1, KVL), F32)],
                  [_rows(tm, LP), _vec(QL), _vec(KVL)], name, n_acc=2)


def _head_spec(tm, w):
    return pl.BlockSpec((tm, w), lambda i, h: (i, h))


def _row2(tm, w, col=0):
    return pl.BlockSpec((tm, w), lambda i, h: (i, col))


def _vec2(w):
    return pl.BlockSpec((1, w), lambda i, h: (0, 0))


def _mla_q_prep(q_raw, g_q, tabs, H, scale, name):
    T = q_raw.shape[0]
    tm = _pick(T, 1024, 8)
    half = MLA_ROPE // 2

    def body(x, g, c, sa, sb):
        n = x * _rstd(x, MLA_QK) * g
        return jnp.concatenate([n[:, :LANES], _rope(n[:, LANES:], c, sa, sb, half)], axis=1) * scale

    return _vcall(body, (T // tm, H), (q_raw, g_q) + tabs,
                  [_head_spec(tm, MLA_QK_PAD), _vec2(MLA_QK_PAD)] + [_row2(tm, LANES)] * 3,
                  [_sds((T, H * MLA_QK_PAD), BF16)], [_head_spec(tm, MLA_QK_PAD)], name)[0]


def _mla_q_bwd(dq, q_raw, g_q, tabs, H, scale, name):
    T = q_raw.shape[0]
    tm = _pick(T, 1024, 8)
    half = MLA_ROPE // 2

    def body(d, x, g, c, sa, sb):
        r = _rstd(x, MLA_QK)
        d = d * scale
        dn = jnp.concatenate([d[:, :LANES], _rope_t(d[:, LANES:], c, sa, sb, half)], axis=1)
        return _rms_bwd(x * r, r, g, dn, MLA_QK)

    return _vcall(body, (T // tm, H), (dq, q_raw, g_q) + tabs,
                  [_head_spec(tm, MLA_QK_PAD), _head_spec(tm, MLA_QK_PAD), _vec2(MLA_QK_PAD)] + [_row2(tm, LANES)] * 3,
                  [_sds((T, H * MLA_QK_PAD), BF16), _sds((1, MLA_QK_PAD), F32)],
                  [_head_spec(tm, MLA_QK_PAD), _vec2(MLA_QK_PAD)], name, n_acc=1)


def _mla_k_prep(kv, lat, kr_col, g_k, tabs, H, name):
    T = kv.shape[0]
    tm = _pick(T, 1024, 8)
    half = MLA_ROPE // 2

    def body(x, kr, g, c, sa, sb):
        kn = x[:, :LANES]
        r = lax.rsqrt((jnp.sum(kn * kn, axis=-1, keepdims=True) + jnp.sum(kr * kr, axis=-1, keepdims=True))
                      * (1.0 / MLA_QK) + EPS)
        k0 = kn * r * g[:, :LANES]
        k1 = _rope(kr * r * g[:, LANES:], c, sa, sb, half)
        return jnp.concatenate([k0, k1], axis=1), x[:, LANES:]

    return _vcall(body, (T // tm, H), (kv, lat, g_k) + tabs,
                  [_head_spec(tm, 2 * LANES), _row2(tm, LANES, kr_col), _vec2(MLA_QK_PAD)] + [_row2(tm, LANES)] * 3,
                  [_sds((T, H * MLA_QK_PAD), BF16), _sds((T, H * MLA_V), BF16)],
                  [_head_spec(tm, MLA_QK_PAD), _head_spec(tm, MLA_V)], name)


def _mla_k_bwd(dk, dv, kv, lat, kr_col, g_k, tabs, H, name):
    T = kv.shape[0]
    tm = _pick(T, 1024, 8)
    half = MLA_ROPE // 2

    def body(d, dvv, x, kr, g, c, sa, sb):
        xx = jnp.concatenate([x[:, :LANES], kr], axis=1)
        r = _rstd(xx, MLA_QK)
        dn = jnp.concatenate([d[:, :LANES], _rope_t(d[:, LANES:], c, sa, sb, half)], axis=1)
        dx, dg = _rms_bwd(xx * r, r, g, dn, MLA_QK)
        return jnp.concatenate([dx[:, :LANES], dvv], axis=1), dx[:, LANES:], dg

    return _vcall(body, (T // tm, H), (dk, dv, kv, lat, g_k) + tabs,
                  [_head_spec(tm, MLA_QK_PAD), _head_spec(tm, MLA_V), _head_spec(tm, 2 * LANES),
                   _row2(tm, LANES, kr_col), _vec2(MLA_QK_PAD)] + [_row2(tm, LANES)] * 3,
                  [_sds((T, H * 2 * LANES), BF16), _sds((T, LANES), F32), _sds((1, MLA_QK_PAD), F32)],
                  [_head_spec(tm, 2 * LANES), _row2(tm, LANES), _vec2(MLA_QK_PAD)], name, n_inner_acc=1, n_acc=1)


def _dil_prep(pd, g_q, g_k, tabs, HD, scale, name):
    T = pd.shape[0]
    W = HD * DIL_HEAD
    G = len(DIL_GROUPS)
    tm = _pick(T, 512, 8)
    half = DIL_ROT // 2

    def body(xq, xk, xv, gq, gk, c, sa, sb):
        outs = []
        for x, g, s in ((xq, gq, scale), (xk, gk, 1.0)):
            heads = []
            for h in range(HD):
                xs = x[:, h * DIL_HEAD:(h + 1) * DIL_HEAD].astype(F32)
                n = _rope(xs * _rstd(xs, DIL_HEAD) * g, c, sa, sb, half)
                heads.append(n * s if s != 1.0 else n)
            outs.append(jnp.concatenate(heads, axis=1))
        return outs[0], outs[1], xv

    gspec = pl.BlockSpec((None, 1, DIL_HEAD), lambda i, g: (g, 0, 0))
    return _vcall(body, (T // tm, G), (pd, pd, pd, g_q, g_k) + tabs,
                  [pl.BlockSpec((tm, W), lambda i, g: (i, 3 * g)), pl.BlockSpec((tm, W), lambda i, g: (i, 3 * g + 1)),
                   pl.BlockSpec((tm, W), lambda i, g: (i, 3 * g + 2)), gspec, gspec] + [_row2(tm, LANES)] * 3,
                  [_sds((T, G * W), F32)] * 3, [pl.BlockSpec((tm, W), lambda i, g: (i, g))] * 3, name)


def _dil_prep_bwd(dq, dk, dv, pd, grp, g_q, g_k, tabs, HD, scale, name, prev=None):
    T = pd.shape[0]
    W = HD * DIL_HEAD
    tm = _pick(T, 256, 8)
    half = DIL_ROT // 2

    def body(dqv, dkv, dvv, xq, xk, gq, gk, c, sa, sb):
        cols, dgs = [], []
        for d, x, g, s in ((dqv, xq, gq, scale), (dkv, xk, gk, 1.0)):
            heads, dg = [], None
            for h in range(HD):
                sl = slice(h * DIL_HEAD, (h + 1) * DIL_HEAD)
                xs = x[:, sl].astype(F32)
                r = _rstd(xs, DIL_HEAD)
                dh = d[:, sl] * s if s != 1.0 else d[:, sl]
                dx, dgh = _rms_bwd(xs * r, r, g, _rope_t(dh, c, sa, sb, half), DIL_HEAD)
                heads.append(dx)
                dg = dgh if dg is None else dg + dgh
            cols.append(jnp.concatenate(heads, axis=1))
            dgs.append(dg)
        return jnp.concatenate(cols + [dvv], axis=1), dgs[0], dgs[1]

    gq, gk = g_q[grp], g_k[grp]
    G = len(DIL_GROUPS)
    return _vcall(body, (T // tm,), (dq, dk, dv, pd, pd, gq, gk) + tabs,
                  [_rows(tm, W)] * 3 + [pl.BlockSpec((tm, W), lambda i: (i, 3 * grp)),
                                        pl.BlockSpec((tm, W), lambda i: (i, 3 * grp + 1)),
                                        _vec(DIL_HEAD), _vec(DIL_HEAD)] + [_rows(tm, LANES)] * 3,
                  [_sds((T, 3 * G * W), BF16), _sds((1, DIL_HEAD), F32), _sds((1, DIL_HEAD), F32)],
                  [pl.BlockSpec((tm, 3 * W), lambda i: (i, grp)), _vec(DIL_HEAD), _vec(DIL_HEAD)], name, n_acc=2,
                  carried=prev)


def _dil_merge(os_, lses, name):
    T, W = os_[0].shape
    tm = _pick(T, 256, 8)

    def body(o0, o1, o2, l0, l1, l2):
        m = jnp.maximum(jnp.maximum(l0, l1), l2)
        w0, w1, w2 = jnp.exp(l0 - m), jnp.exp(l1 - m), jnp.exp(l2 - m)
        z = w0 + w1 + w2
        return (w0 * o0 + w1 * o1 + w2 * o2) / z, m + jnp.log(z)

    return _vcall(body, (T // tm,), tuple(os_) + tuple(lses), [_rows(tm, W)] * 6,
                  [_sds((T, W), BF16), _sds((T, W), F32)], [_rows(tm, W)] * 2, name)


def _gate_merge(pg, bm, bd, name):
    T, D = bm.shape
    tm = _pick(T, 256, 8)

    def body(g, m, d):
        g = g.astype(F32)
        return _sigmoid(g[:, :D]) * m + _sigmoid(g[:, D:]) * d

    return _vcall(body, (T // tm,), (pg, bm, bd), [_rows(tm, 2 * D), _rows(tm, D), _rows(tm, D)],
                  [_sds((T, D), BF16)], [_rows(tm, D)], name)[0]


def _gate_bwd(dmerged, pg, bm, bd, name):
    T, D = bm.shape
    tm = _pick(T, 256, 8)

    def body(dm, g, m, d):
        g = g.astype(F32)
        s0, s1 = _sigmoid(g[:, :D]), _sigmoid(g[:, D:])
        dpg = jnp.concatenate([dm * m * s0 * (1.0 - s0), dm * d * s1 * (1.0 - s1)], axis=1)
        return dm * s0, dm * s1, dpg

    return _vcall(body, (T // tm,), (dmerged, pg, bm, bd), [_rows(tm, D), _rows(tm, 2 * D), _rows(tm, D), _rows(tm, D)],
                  [_sds((T, D), BF16), _sds((T, D), BF16), _sds((T, 2 * D), BF16)],
                  [_rows(tm, D), _rows(tm, D), _rows(tm, 2 * D)], name)


def _ple_loss(x3, zg, pp, target, name):
    T, D = x3.shape
    tm = _pick(T, 256, 8)

    def body(x, z, p_, t):
        s = _sigmoid(z)
        e = x + s * p_ - t
        dy = e * (1.0 / D)
        part = 0.5 * jnp.sum(jnp.sum(e * e, axis=1, keepdims=True), axis=0, keepdims=True) * (1.0 / D)
        return dy, dy * s, dy * p_ * s * (1.0 - s), jnp.broadcast_to(part, (1, LANES))

    return _vcall(body, (T // tm,), (x3, zg, pp, target), [_rows(tm, D)] * 4,
                  [_sds((T, D), F32), _sds((T, D), BF16), _sds((T, D), BF16), _sds((1, LANES), F32)],
                  [_rows(tm, D)] * 3 + [_vec(LANES)], name, n_acc=1)


NT = (((1,), (1,)), ((), ()))
TN = (((0,), (0,)), ((), ()))


def _diag_mask(s):
    row = lax.broadcasted_iota(jnp.int32, s.shape, 0)
    col = lax.broadcasted_iota(jnp.int32, s.shape, 1)
    return jnp.where(col <= row, s, NEG)


def _causal_pairs(nq, key_major, kr=1):
    if key_major:
        pairs = [(i, j) for j in range(nq // kr) for i in range(kr * j, nq)]
    else:
        pairs = [(i, j) for i in range(nq) for j in range(i // kr + 1)]
    return (jnp.asarray([pr[0] for pr in pairs], jnp.int32), jnp.asarray([pr[1] for pr in pairs], jnp.int32))


def _mla_fwd(q, k, v, H, name, side=None):
    T = q.shape[0]
    tq = _pick(T, 512)
    nq = T // tq
    hb = 2 if H % 2 == 0 else 1
    kr = 2 if nq % 2 == 0 else 1
    tk = kr * tq
    qi_tab, kj_tab = _causal_pairs(nq, key_major=False, kr=kr)

    def kern(qi_ref, kj_ref, q_ref, k_ref, vt_ref, o_ref, ot_ref, lse_ref, m_sc, l_sc, acc_sc):
        t = pl.program_id(1)
        qi, kj = qi_ref[t], kj_ref[t]

        @pl.when(kj == 0)
        def _():
            m_sc[...] = jnp.full_like(m_sc, NEG)
            l_sc[...] = jnp.zeros_like(l_sc)
            acc_sc[...] = jnp.zeros_like(acc_sc)

        def tile(diagonal):
            for hh in range(hb):
                qs = slice(hh * MLA_QK_PAD, (hh + 1) * MLA_QK_PAD)
                vs = slice(hh * MLA_V, (hh + 1) * MLA_V)
                st = lax.dot_general(k_ref[:, qs], q_ref[:, qs], NT, preferred_element_type=F32)
                if diagonal:
                    key = lax.broadcasted_iota(jnp.int32, st.shape, 0)
                    qry = lax.broadcasted_iota(jnp.int32, st.shape, 1) + (qi - kr * kj) * tq
                    st = jnp.where(key <= qry, st, NEG)
                m_prev = m_sc[hh]
                m_new = jnp.maximum(m_prev, jnp.max(st, axis=0, keepdims=True))
                alpha = jnp.exp(m_prev - m_new)
                pt = jnp.exp(st - m_new)
                l_new = alpha * l_sc[hh] + jnp.sum(pt, axis=0, keepdims=True)
                acc = alpha * acc_sc[hh] + jnp.dot(vt_ref[vs, :], pt.astype(BF16), preferred_element_type=F32)
                if diagonal:
                    out_t = acc / l_new
                    o_ref[:, vs] = out_t.T.astype(o_ref.dtype)
                    ot_ref[vs, :] = out_t.astype(ot_ref.dtype)
                    lse_ref[hh] = m_new + jnp.log(l_new)
                else:
                    m_sc[hh] = m_new
                    l_sc[hh] = l_new
                    acc_sc[hh] = acc

        @pl.when(kj < qi // kr)
        def _():
            tile(False)

        @pl.when(kj == qi // kr)
        def _():
            tile(True)

    qspec = lambda w: pl.BlockSpec((tq, hb * w), lambda h, t, qi_ref, kj_ref: (qi_ref[t], h))
    kspec = lambda w: pl.BlockSpec((tk, hb * w), lambda h, t, qi_ref, kj_ref: (kj_ref[t], h))
    vt_spec = pl.BlockSpec((hb * MLA_V, tk), lambda h, t, qi_ref, kj_ref: (h, kj_ref[t]))
    ot_spec = pl.BlockSpec((hb * MLA_V, tq), lambda h, t, qi_ref, kj_ref: (h, qi_ref[t]))
    lse_spec = pl.BlockSpec((hb, 1, tq), lambda h, t, qi_ref, kj_ref: (h, 0, qi_ref[t]))
    n_pairs = qi_tab.shape[0]
    s_in, s_out, s_sems, s_alias = _side_parts(side, 5, 3)
    grid_spec = pltpu.PrefetchScalarGridSpec(
        num_scalar_prefetch=2, grid=(H // hb, n_pairs),
        in_specs=[qspec(MLA_QK_PAD), kspec(MLA_QK_PAD), vt_spec] + [ANY] * len(s_in),
        out_specs=[qspec(MLA_V), ot_spec, lse_spec] + [ANY] * len(s_out),
        scratch_shapes=[pltpu.VMEM((hb, 1, tq), F32), pltpu.VMEM((hb, 1, tq), F32),
                        pltpu.VMEM((hb, MLA_V, tq), F32)] + s_sems)
    step_of = lambda: pl.program_id(0) * n_pairs + pl.program_id(1)
    return pl.pallas_call(
        _carry(kern, side, 5, 3, 3, step_of, (H // hb) * n_pairs), name=name, grid_spec=grid_spec,
        out_shape=[_sds((T, H * MLA_V), BF16), _sds((H * MLA_V, T), BF16), _sds((H, 1, T), F32)] + s_out,
        input_output_aliases=s_alias, compiler_params=_params(2))(qi_tab, kj_tab, q, k, v.T, *s_in)


def _mla_bwd(q, k, v, do, do_t, o_t, lse, H, name, side=None):
    T = q.shape[0]
    tq = _pick(T, 512)
    nq = T // tq
    kr = 2 if nq % 2 == 0 else 1
    tk = kr * tq
    qi_tab, kj_tab = _causal_pairs(nq, key_major=True, kr=kr)

    def kern(qi_ref, kj_ref, q_ref, k_ref, v_ref, do_ref, dot_ref, ot_ref, lse_ref, dq_ref, dk_ref, dv_ref, dk_sc, dv_sc):
        t = pl.program_id(1)
        qi, kj = qi_ref[t], kj_ref[t]
        rows = pl.ds(pl.multiple_of(qi * tq, tq), tq)

        def tile(masked, first):
            st = lax.dot_general(k_ref[...], q_ref[...], NT, preferred_element_type=F32)
            if masked:
                key = lax.broadcasted_iota(jnp.int32, st.shape, 0)
                qry = lax.broadcasted_iota(jnp.int32, st.shape, 1) + (qi - kr * kj) * tq
                st = jnp.where(key <= qry, st, NEG)
            pt = jnp.exp(st - lse_ref[...])
            dl = jnp.sum(dot_ref[...].astype(F32) * ot_ref[...].astype(F32), axis=0, keepdims=True)
            dpt = jnp.dot(v_ref[...], dot_ref[...], preferred_element_type=F32)
            dst = (pt * (dpt - dl)).astype(BF16)
            dv = jnp.dot(pt.astype(BF16), do_ref[...], preferred_element_type=F32)
            dk = jnp.dot(dst, q_ref[...], preferred_element_type=F32)
            dq = lax.dot_general(dst, k_ref[...], TN, preferred_element_type=F32)
            if first:
                dv_sc[...] = dv
                dk_sc[...] = dk
            else:
                dv_sc[...] += dv
                dk_sc[...] += dk

            @pl.when(kj == 0)
            def _():
                dq_ref[rows, :] = dq

            @pl.when(kj > 0)
            def _():
                dq_ref[rows, :] += dq

        @pl.when(qi == kr * kj)
        def _():
            tile(True, True)

        if kr > 1:
            @pl.when((qi > kr * kj) & (qi < kr * kj + kr))
            def _():
                tile(True, False)

        @pl.when(qi >= kr * kj + kr)
        def _():
            tile(False, False)

        @pl.when(qi == nq - 1)
        def _():
            dk_ref[...] = dk_sc[...]
            dv_ref[...] = dv_sc[...]

    qspec = lambda w: pl.BlockSpec((tq, w), lambda h, t, qi_ref, kj_ref: (qi_ref[t], h))
    kspec = lambda w: pl.BlockSpec((tk, w), lambda h, t, qi_ref, kj_ref: (kj_ref[t], h))
    t_spec = pl.BlockSpec((MLA_V, tq), lambda h, t, qi_ref, kj_ref: (h, qi_ref[t]))
    lse_spec = pl.BlockSpec((None, 1, tq), lambda h, t, qi_ref, kj_ref: (h, 0, qi_ref[t]))
    n_pairs = qi_tab.shape[0]
    s_in, s_out, s_sems, s_alias = _side_parts(side, 9, 3)
    grid_spec = pltpu.PrefetchScalarGridSpec(
        num_scalar_prefetch=2, grid=(H, n_pairs),
        in_specs=[qspec(MLA_QK_PAD), kspec(MLA_QK_PAD), kspec(MLA_V), qspec(MLA_V), t_spec, t_spec, lse_spec]
        + [ANY] * len(s_in),
        out_specs=[pl.BlockSpec((T, MLA_QK_PAD), lambda h, t, qi_ref, kj_ref: (0, h)), kspec(MLA_QK_PAD), kspec(MLA_V)]
        + [ANY] * len(s_out),
        scratch_shapes=[pltpu.VMEM((tk, MLA_QK_PAD), F32), pltpu.VMEM((tk, MLA_V), F32)] + s_sems)
    step_of = lambda: pl.program_id(0) * n_pairs + pl.program_id(1)
    return pl.pallas_call(
        _carry(kern, side, 9, 3, 2, step_of, H * n_pairs), name=name, grid_spec=grid_spec,
        out_shape=[_sds((T, H * MLA_QK_PAD), F32), _sds((T, H * MLA_QK_PAD), F32), _sds((T, H * MLA_V), F32)] + s_out,
        input_output_aliases=s_alias, compiler_params=_params(2))(qi_tab, kj_tab, q, k, v, do, do_t, o_t, lse, *s_in)


class _DilGeometry:
    def __init__(self, T, dil, HD, grp):
        self.dil, self.sub = dil, max(1, 8 // dil)
        self.tb = self.sub * DIL_BLOCK * dil
        assert T % self.tb == 0, (T, dil)
        self.nblk = T // self.tb
        last = self.nblk - 1
        self.cur_g = pl.BlockSpec((self.tb, DIL_HEAD), lambda i, h: (i, grp * HD + h))
        self.prev_g = pl.BlockSpec((self.tb, DIL_HEAD), lambda i, h: (jnp.maximum(i - 1, 0), grp * HD + h))
        self.next_g = pl.BlockSpec((self.tb, DIL_HEAD), lambda i, h: (jnp.minimum(i + 1, last), grp * HD + h))
        self.cur = pl.BlockSpec((self.tb, DIL_HEAD), lambda i, h: (i, h))
        self.next = pl.BlockSpec((self.tb, DIL_HEAD), lambda i, h: (jnp.minimum(i + 1, last), h))

    def rows(self, b, r):
        if self.dil == 1:
            return pl.ds(b * DIL_BLOCK, DIL_BLOCK)
        return pl.ds(b * DIL_BLOCK * self.dil + r, DIL_BLOCK, stride=self.dil)

    def tiles(self):
        return [(b, r) for b in range(self.sub) for r in range(self.dil)]

    def rows2(self, b, r):
        if self.dil == 1:
            return pl.ds(b * DIL_BLOCK, 2 * DIL_BLOCK)
        return pl.ds(b * DIL_BLOCK * self.dil + r, 2 * DIL_BLOCK, stride=self.dil)

    def keys(self, cur_ref, prev_ref, b, r):
        if b > 0:
            return cur_ref[self.rows2(b - 1, r), :].astype(BF16)
        return jnp.concatenate([prev_ref[self.rows(self.sub - 1, r), :], cur_ref[self.rows(0, r), :]],
                               axis=0).astype(BF16)

    def masks(self, i):
        row = lax.broadcasted_iota(jnp.int32, (DIL_BLOCK, 2 * DIL_BLOCK), 0)
        col = lax.broadcasted_iota(jnp.int32, (DIL_BLOCK, 2 * DIL_BLOCK), 1)
        band = (col >= row) & (col <= row + DIL_BLOCK)
        first = band & (col >= jnp.where(i > 0, 0, DIL_BLOCK))
        no_next = jnp.where(i + 1 < self.nblk, 0, 2 * DIL_BLOCK)
        ok_next = col[:, :DIL_BLOCK] >= row[:, :DIL_BLOCK] + no_next
        return band, first, ok_next


def _twice(x):
    return jnp.concatenate([x, x], axis=1)


def _dil_fwd(qd, kd, vd, grp, dil, HD, name):
    T = qd.shape[0]
    W = HD * DIL_HEAD
    geo = _DilGeometry(T, dil, HD, grp)

    def kern(q_ref, kc_ref, kp_ref, vc_ref, vp_ref, o_ref, lse_ref):
        band, first, _ = geo.masks(pl.program_id(0))
        for b, r in geo.tiles():
            R = geo.rows(b, r)
            q = q_ref[R, :].astype(BF16)
            kk, vv = geo.keys(kc_ref, kp_ref, b, r), geo.keys(vc_ref, vp_ref, b, r)
            s = jnp.where(band if b > 0 else first, lax.dot_general(q, kk, NT, preferred_element_type=F32), NEG)
            m = jnp.max(s, axis=1, keepdims=True)
            e = jnp.exp(s - m)
            l = jnp.sum(e, axis=1, keepdims=True)
            o_ref[R, :] = jnp.dot(e.astype(BF16), vv, preferred_element_type=F32) / l
            lse_ref[R, :] = jnp.broadcast_to(m + jnp.log(l), (DIL_BLOCK, DIL_HEAD))

    return pl.pallas_call(
        kern, name=name, grid=(geo.nblk, HD), in_specs=[geo.cur_g, geo.cur_g, geo.prev_g, geo.cur_g, geo.prev_g],
        out_specs=[geo.cur, geo.cur], out_shape=[_sds((T, W), F32)] * 2,
        compiler_params=_params(2))(qd, kd, kd, vd, vd)


def _dil_delta(do, o, HD, name):
    T, W = do.shape
    tm = _pick(T, 512, 8)

    def body(d, ov):
        prod = d * ov.astype(F32)
        return jnp.concatenate(
            [jnp.broadcast_to(jnp.sum(prod[:, h * DIL_HEAD:(h + 1) * DIL_HEAD], axis=1, keepdims=True), (tm, DIL_HEAD))
             for h in range(HD)], axis=1)

    return _vcall(body, (T // tm,), (do, o), [_rows(tm, W)] * 2, [_sds((T, W), F32)], [_rows(tm, W)], name)[0]


def _dil_bwd(qd, kd, vd, do, delta, lse, grp, dil, HD, name):
    T = qd.shape[0]
    W = HD * DIL_HEAD
    geo = _DilGeometry(T, dil, HD, grp)

    def kern(q_ref, k_ref, v_ref, do_ref, dl_ref, ls_ref, kp_ref, vp_ref, qn_ref, don_ref, dln_ref, lsn_ref,
             dq_ref, dk_ref, dv_ref):
        band, first, ok_next = geo.masks(pl.program_id(0))

        def tile(q, do_, dl, ls, k, v, ok):
            s = jnp.where(ok, lax.dot_general(q, k, NT, preferred_element_type=F32), NEG)
            p = jnp.exp(s - ls)
            ds = p * (lax.dot_general(do_, v, NT, preferred_element_type=F32) - dl)
            return ds.astype(BF16), p.astype(BF16)

        dk_ref[...] = jnp.zeros_like(dk_ref)
        dv_ref[...] = jnp.zeros_like(dv_ref)
        for b, r in geo.tiles():
            R = geo.rows(b, r)
            q, do_ = q_ref[R, :].astype(BF16), do_ref[R, :].astype(BF16)
            kk, vv = geo.keys(k_ref, kp_ref, b, r), geo.keys(v_ref, vp_ref, b, r)
            ds, p_ = tile(q, do_, _twice(dl_ref[R, :]), _twice(ls_ref[R, :]), kk, vv, band if b > 0 else first)
            dq_ref[R, :] = jnp.dot(ds, kk, preferred_element_type=F32)
            dkk = lax.dot_general(ds, q, TN, preferred_element_type=F32)
            dvv = lax.dot_general(p_, do_, TN, preferred_element_type=F32)
            if b > 0:
                R2 = geo.rows2(b - 1, r)
                dk_ref[R2, :] += dkk
                dv_ref[R2, :] += dvv
            else:
                dk_ref[R, :] += dkk[DIL_BLOCK:]
                dv_ref[R, :] += dvv[DIL_BLOCK:]
        for r in range(dil):
            R, Rn = geo.rows(geo.sub - 1, r), geo.rows(0, r)
            qn, don = qn_ref[Rn, :].astype(BF16), don_ref[Rn, :].astype(BF16)
            ds, p_ = tile(qn, don, dln_ref[Rn, :], lsn_ref[Rn, :], k_ref[R, :].astype(BF16), v_ref[R, :].astype(BF16),
                          ok_next)
            dk_ref[R, :] += lax.dot_general(ds, qn, TN, preferred_element_type=F32)
            dv_ref[R, :] += lax.dot_general(p_, don, TN, preferred_element_type=F32)

    return pl.pallas_call(
        kern, name=name, grid=(geo.nblk, HD),
        in_specs=[geo.cur_g, geo.cur_g, geo.cur_g, geo.cur, geo.cur, geo.cur, geo.prev_g, geo.prev_g,
                  geo.next_g, geo.next, geo.next, geo.next],
        out_specs=[geo.cur] * 3, out_shape=[_sds((T, W), F32)] * 3,
        compiler_params=_params(2))(qd, kd, vd, do, delta, lse, kd, vd, qd, do, delta, lse)


def _place():
    return lax.axis_index("x"), lax.axis_index("y"), lax.axis_index("c")


def _other_chips(x, y):
    return [(1 - x, y), (x, 1 - y), (1 - x, 1 - y)]


def _kind(name, shard_shape):
    if name in ROW_SHARDED:
        return "row"
    return "col" if shard_shape[1] % LANES == 0 else "stack"


def _remote(src, dst, send_sem, recv_sem, to):
    return pltpu.make_async_remote_copy(src_ref=src, dst_ref=dst, send_sem=send_sem, recv_sem=recv_sem,
                                        device_id=to, device_id_type=MESH_ID)


def _row_tile(rows, cols, itemsize, align):
    return _pick(rows, max(align, (2 * 1024 * 1024) // (cols * itemsize)), align)


def _dma_sems(n):
    return [pltpu.SemaphoreType.DMA((n,)), pltpu.SemaphoreType.DMA((n,))]


def _gather_plan(shard_shapes):
    info, buf_shapes = {}, {}
    for out_name, names in GATHER_PLAN:
        r, c = shard_shapes[names[0]]
        kind = _kind(names[0], (r, c))
        assert kind == "col" or len(names) == 1, out_name
        buf_shapes[out_name] = (r, 4 * c * len(names)) if kind == "col" else (4, r, c)
        for i, n in enumerate(names):
            assert tuple(shard_shapes[n]) == (r, c), n
            info[n] = (out_name, kind, i * 4 * c, r, c)
    return info, buf_shapes


def _place_own(shard, buf_shape, kind, base, me, name, prev=None):
    r, c = shard.shape
    tr = _row_tile(r, c, 2, 16)

    def kern(me_ref, x_ref, *rest):
        rest[-1][...] = x_ref[...]

    if kind == "col":
        out_spec = pl.BlockSpec((tr, c), lambda i, me_ref: (i, base // c + me_ref[0]))
    else:
        out_spec = pl.BlockSpec((None, tr, c), lambda i, me_ref: (me_ref[0], i, 0))
    in_specs = [pl.BlockSpec((tr, c), lambda i, me_ref: (i, 0))] + ([ANY] if prev is not None else [])
    grid_spec = pltpu.PrefetchScalarGridSpec(num_scalar_prefetch=1, grid=(r // tr,), in_specs=in_specs,
                                             out_specs=out_spec)
    args = (me, shard) + ((prev,) if prev is not None else ())
    return pl.pallas_call(kern, name=name, grid_spec=grid_spec, out_shape=_sds(buf_shape, shard.dtype),
                          input_output_aliases={2: 0} if prev is not None else {}, compiler_params=_params(1))(*args)


def _ag_entry(e):
    return e if isinstance(e, tuple) else (e, 0, 1)


def _buffers_of(names, info):
    out_names = []
    for n in [_ag_entry(e)[0] for e in names]:
        if info[n][0] not in out_names:
            out_names.append(info[n][0])
    return out_names


def _ag_side(names, shards, bufs, info):
    entries = [_ag_entry(e) for e in names]
    names = [e[0] for e in entries]
    out_names = _buffers_of(names, info)
    n_w = len(names)

    def rows_of(w, h):
        r = info[names[w]][3]
        _, p, parts = entries[w]
        size = r // (2 * parts)
        return pl.ds(h * (r // 2) + p * size, size)

    def region(outs, w, chip, h):
        out_name, kind, base, r, cc = info[names[w]]
        o = outs[out_names.index(out_name)]
        if kind == "col":
            return o.at[rows_of(w, h), pl.ds(pl.multiple_of(base + chip * cc, LANES), cc)]
        return o.at[chip, rows_of(w, h), :]

    def hop(first, sending):
        def fn(ins, outs, sems):
            x, y, c = _place()
            me, sibling, cps = 2 * x + y, (x, y, 1 - c), []
            for w in range(n_w):
                for j, (px, py) in enumerate(_other_chips(x, y)):
                    k = 3 * w + j + (0 if first else 3 * n_w)
                    if first and sending:
                        src, dst, to = ins[w].at[rows_of(w, c), :], region(outs, w, me, c), (px, py, c)
                    elif first:
                        src = dst = region(outs, w, 2 * px + py, c)
                        to = (px, py, c)
                    else:
                        src = dst = region(outs, w, 2 * px + py, c if sending else 1 - c)
                        to = sibling
                    cps.append(_remote(src, dst, sems[0].at[k], sems[1].at[k], to))
            return cps
        return fn

    return _Side([shards[n] for n in names] + [bufs[o] for o in out_names],
                 [_sds(bufs[o].shape, bufs[o].dtype) for o in out_names], {n_w + i: i for i in range(len(out_names))},
                 _dma_sems(6 * n_w), [(hop(True, True), hop(True, False)), (hop(False, True), hop(False, False))])


def _rs_sibling_side(views):
    n = len(views)

    def fn(sending):
        def copies(ins, outs, sems):
            x, y, c = _place()
            return [_remote(ins[w].at[:, 1 - c] if sending else outs[w], outs[w], sems[0].at[w], sems[1].at[w],
                            (x, y, 1 - c)) for w in range(n)]
        return copies

    return _Side(views, [_sds((v.shape[0],) + v.shape[2:], v.dtype) for v in views], {}, _dma_sems(n),
                 [(fn(True), fn(False))])


def _rs_chips_side(parts, kinds, widths):
    n = len(parts)

    def piece(ins, w, chip):
        if kinds[w] == "col":
            return ins[w].at[0, :, pl.ds(pl.multiple_of(chip * widths[w], LANES), widths[w])]
        return ins[w].at[chip]

    def fn(sending):
        def copies(ins, outs, sems):
            x, y, c = _place()
            cps = []
            for w in range(n):
                for j, (px, py) in enumerate(_other_chips(x, y)):
                    k = 3 * w + j
                    src = piece(ins, w, 2 * px + py) if sending else outs[w].at[j]
                    cps.append(_remote(src, outs[w].at[j], sems[0].at[k], sems[1].at[k], (px, py, c)))
            return cps
        return copies

    return _Side(parts, [_sds((3, p_.shape[1], widths[w]), p_.dtype) for w, p_ in enumerate(parts)], {},
                 _dma_sems(3 * n), [(fn(True), fn(False))])


def _rs_join_side(halves):
    n = len(halves)

    def fn(sending):
        def copies(ins, outs, sems):
            x, y, c = _place()
            return [_remote(ins[w] if sending else outs[w], outs[w], sems[0].at[w], sems[1].at[w], (x, y, 1 - c))
                    for w in range(n)]
        return copies

    return _Side(halves, [_sds(h.shape, h.dtype) for h in halves], {}, _dma_sems(n), [(fn(True), fn(False))])


def _pair_sum(g, got, c_idx, name):
    n, _, rows, C = g.shape
    tr = _row_tile(rows, C, 2, 16)

    def kern(c_ref, a_ref, b_ref, o_ref):
        o_ref[...] = (a_ref[...].astype(F32) + b_ref[...].astype(F32)).astype(o_ref.dtype)

    grid_spec = pltpu.PrefetchScalarGridSpec(
        num_scalar_prefetch=1, grid=(n, rows // tr),
        in_specs=[pl.BlockSpec((None, None, tr, C), lambda j, i, c_ref: (j, c_ref[0], i, 0)),
                  pl.BlockSpec((None, tr, C), lambda j, i, c_ref: (j, i, 0))],
        out_specs=pl.BlockSpec((None, tr, C), lambda j, i, c_ref: (j, i, 0)))
    return pl.pallas_call(kern, name=name, grid_spec=grid_spec, out_shape=_sds((n, rows, C), BF16),
                          compiler_params=_params(2))(c_idx, g, got)


def _sum_pieces(pair, recv, kind, me, name):
    _, rows, c = recv.shape
    tr = _row_tile(rows, c, 8, 16)

    def kern(me_ref, own_ref, r_ref, o_ref):
        acc = own_ref[...].astype(F32)
        for j in range(3):
            acc = acc + r_ref[j].astype(F32)
        o_ref[...] = acc

    if kind == "col":
        own_spec = pl.BlockSpec((None, tr, c), lambda i, me_ref: (0, i, me_ref[0]))
    else:
        own_spec = pl.BlockSpec((None, tr, c), lambda i, me_ref: (me_ref[0], i, 0))
    grid_spec = pltpu.PrefetchScalarGridSpec(
        num_scalar_prefetch=1, grid=(rows // tr,),
        in_specs=[own_spec, pl.BlockSpec((3, tr, c), lambda i, me_ref: (0, i, 0))],
        out_specs=pl.BlockSpec((tr, c), lambda i, me_ref: (i, 0)))
    return pl.pallas_call(kern, name=name, grid_spec=grid_spec, out_shape=_sds((rows, c), F32),
                          compiler_params=_params(1))(me, pair, recv)


def _all_reduce_small(vec):
    N = vec.shape[1]
    n_dev = 8

    def body(v_ref, out_ref, slots, send_sems, recv_sems):
        x, y, c = _place()
        me = 4 * x + 2 * y + c
        slots[me] = v_ref[...]
        sent = []
        for k in range(1, n_dev):
            px, py, pc = x ^ (k >> 2), y ^ ((k >> 1) & 1), c ^ (k & 1)
            cp = pltpu.make_async_remote_copy(src_ref=v_ref, dst_ref=slots.at[me], send_sem=send_sems.at[k - 1],
                                              recv_sem=recv_sems.at[k - 1], device_id=(px, py, pc),
                                              device_id_type=MESH_ID)
            cp.start()
            sent.append(cp)
        for k in range(1, n_dev):
            px, py, pc = x ^ (k >> 2), y ^ ((k >> 1) & 1), c ^ (k & 1)
            slot = slots.at[4 * px + 2 * py + pc]
            pltpu.make_async_remote_copy(src_ref=slot, dst_ref=slot, send_sem=send_sems.at[k - 1],
                                         recv_sem=recv_sems.at[k - 1], device_id=(px, py, pc),
                                         device_id_type=MESH_ID).wait_recv()
        for cp in sent:
            cp.wait_send()
        acc = slots[0]
        for j in range(1, n_dev):
            acc = acc + slots[j]
        out_ref[...] = acc

    vm = pl.BlockSpec(memory_space=pltpu.VMEM)
    return pl.pallas_call(
        body, name="ar_gains", out_shape=_sds((1, N), F32), in_specs=[vm], out_specs=vm,
        scratch_shapes=[pltpu.VMEM((n_dev, 1, N), F32), pltpu.SemaphoreType.DMA((n_dev - 1,)),
                        pltpu.SemaphoreType.DMA((n_dev - 1,))])(vec)


def _adamw_math(wv, gv, mv, vv):
    m2 = ADAM_B1 * mv + (1.0 - ADAM_B1) * gv
    v2 = ADAM_B2 * vv + (1.0 - ADAM_B2) * (gv * gv)
    m_hat = m2 / (1.0 - ADAM_B1 ** ADAM_STEP)
    v_hat = v2 / (1.0 - ADAM_B2 ** ADAM_STEP)
    return -ADAM_LR * (m_hat / (jnp.sqrt(v_hat) + ADAM_EPS) + ADAM_WD * wv), m2, v2


def _adamw(w, g, m, v, name):
    R, C = w.shape
    tr = _row_tile(R, C, 8, 8)
    return _vcall(_adamw_math, (R // tr,), (w, g, m, v), [_rows(tr, C)] * 4, [_sds((R, C), F32)] * 3,
                  [_rows(tr, C)] * 3, name)


def _adamw_halves(w, own, recv, m, v, c_idx, name):
    R, C = w.shape
    rows = R // 2
    tr = _row_tile(rows, C, 8, 8)
    nb = rows // tr

    def kern(c_ref, w_ref, own_ref, recv_ref, m_ref, v_ref, g_out, d_out, m_out, v_out):
        def update(g_ref):
            g = g_ref[...]
            g_out[...] = g
            d_out[...], m_out[...], v_out[...] = _adamw_math(w_ref[...], g, m_ref[...], v_ref[...])

        @pl.when(pl.program_id(0) == c_ref[0])
        def _():
            update(own_ref)

        @pl.when(pl.program_id(0) != c_ref[0])
        def _():
            update(recv_ref)

    full = pl.BlockSpec((tr, C), lambda h, i, c_ref: (h * nb + i, 0))
    own_spec = pl.BlockSpec((tr, C), lambda h, i, c_ref: (jnp.where(h == c_ref[0], i, 0), 0))
    recv_spec = pl.BlockSpec((tr, C), lambda h, i, c_ref: (jnp.where(h == c_ref[0], 0, i), 0))
    grid_spec = pltpu.PrefetchScalarGridSpec(num_scalar_prefetch=1, grid=(2, nb),
                                             in_specs=[full, own_spec, recv_spec, full, full], out_specs=[full] * 4)
    return pl.pallas_call(kern, name=name, grid_spec=grid_spec, out_shape=[_sds((R, C), F32)] * 4,
                          compiler_params=_params(2))(c_idx, w, own, recv, m, v)


def _pad_to(a, n, axis):
    extra = n - a.shape[axis]
    if extra == 0:
        return a
    pads = [(0, 0)] * a.ndim
    pads[axis] = (0, extra)
    return jnp.pad(a, pads)


def _round_up(n, m):
    return -(-n // m) * m


def _natural(buf, kind):
    if kind == "col":
        return buf
    n, r, c = buf.shape
    return buf.reshape(n * r, c) if kind == "row" else buf.transpose(1, 0, 2).reshape(r, n * c)


def _halves_view(g, kind, shard_shape):
    r, c = shard_shape
    if kind == "col":
        return g.reshape(1, 2, r // 2, 4 * c)
    if kind == "stack":
        g = g.reshape(r, 4, c).transpose(1, 0, 2)
    return g.reshape(4, 2, r // 2, c)


def _pack_small(vals):
    return jnp.concatenate([_pad_to(vals[n].reshape(1, -1), _round_up(vals[n].size, LANES), 1) for n in SMALL], axis=1)


def _unpack_small(vec, shapes):
    out, off = {}, 0
    for n in SMALL:
        size = int(np.prod(shapes[n]))
        out[n] = vec[:, off:off + size].reshape(shapes[n])
        off += _round_up(size, LANES)
    return out


def _mm_s(sched, a, b, mode, out_dtype, name, **kw):
    side = sched.side(name)
    if side is None:
        return _mm(a, b, mode, out_dtype, name, **kw)
    out, side_outs = _mm(a, b, mode, out_dtype, name, side=side, **kw)
    sched.done(name, side_outs)
    return out


def _call_s(sched, name, n_out, fn):
    side = sched.side(name)
    outs = fn(side)
    if side is not None:
        sched.done(name, list(outs[n_out:]))
    return outs[:n_out]


def _ffn_fwd(sched, x, g, tf, tag):
    w = tag[-1]
    n = _call_s(sched, f"{tag}_norm", 1, lambda side: _rms_fwd(x, g, f"{tag}_norm", side=side) if side else
                [_rms_fwd(x, g, f"{tag}_norm")])[0]
    a, b, act = _call_s(sched, f"{tag}_gate_up", 3,
                        lambda side: _gate_up(n, sched.weight(f"w{w}_gu"), tf, f"{tag}_gate_up", side=side))
    out = _mm_s(sched, act, sched.weight(f"w{w}_down"), "nn", F32, f"{tag}_down", res=x, alpha=0.5)
    return out, (n, a, b, act)


def _ffn_bwd(sched, dout, dout_b, x, g, saved, tf, tag, with_bf16):
    w = tag[-1]
    w_gu, w_d = sched.weight(f"w{w}_gu"), sched.weight(f"w{w}_down")
    n, a, b, act = saved
    F = act.shape[1]
    sched.grad(f"w{w}_down", _mm_s(sched, act, dout_b, "tn", BF16, f"{tag}_d_wdown", alpha=0.5))
    da, db = _call_s(sched, f"{tag}_d_act", 2,
                     lambda side: _d_gate_up(dout_b, w_d, a, b, tf, f"{tag}_d_act", side=side))
    sched.grad(f"w{w}_gate", _mm_s(sched, n, da, "tn", BF16, f"{tag}_d_wgate"))
    sched.grad(f"w{w}_up", _mm_s(sched, n, db, "tn", BF16, f"{tag}_d_wup"))
    dn = _mm_s(sched, da, w_gu, "nt", F32, f"{tag}_d_norm", a2=db, b2=w_gu, b2_k_offset=F)
    return _call_s(sched, f"{tag}_d_x", 3 if with_bf16 else 2,
                   lambda side: _rms_bwd_call(x, g, dn, dout, f"{tag}_d_x", side=side, with_bf16=with_bf16))


def _local_step(sched, x, p, pos_b, target, Gn, dims):
    T, D = x.shape
    H, HD, QL, KVL, LP, tf = dims["H"], dims["HD"], dims["QL"], dims["KVL"], dims["LP"], dims["tf"]
    Wd = HD * DIL_HEAD
    scale_mla, scale_dil = MLA_QK ** -0.5, DIL_HEAD ** -0.5
    kr_col = (QL + KVL) // LANES
    tab_mla = tuple(_rope_tables(pos_b, MLA_ROPE, "rope_tab_mla"))
    tab_dil = tuple(_rope_tables(pos_b, DIL_ROT, "rope_tab_dil"))

    W = sched.weight
    mm = functools.partial(_mm_s, sched)

    x1, ffn1 = _ffn_fwd(sched, x, Gn["g_ffn1"], tf, "ffn1")
    h = _rms_fwd(x1, Gn["g_mix"], "mix_norm")
    lat = mm(h, W("w_lat"), "nn", F32, "proj_lat")
    pd = mm(h, W("w_dil"), "nn", BF16, "proj_dil")
    pg = mm(h, W("w_gin"), "nn", BF16, "proj_gate")

    cq, ckv = _lat_fwd(lat, Gn["g_cq"], Gn["g_ckv"], "lat_norm")
    q_raw = mm(cq, W("w_uq"), "nn", F32, "mla_q_up")
    kv = mm(ckv, W("w_ukv"), "nn", F32, "mla_kv_up")
    q = _mla_q_prep(q_raw, Gn["g_q_mla"], tab_mla, H, scale_mla, "mla_q_prep")
    k, v = _mla_k_prep(kv, lat, kr_col, Gn["g_k_mla"], tab_mla, H, "mla_k_prep")
    o_mla, o_mla_t, lse_mla = _call_s(sched, "mla_attn", 3, lambda side: _mla_fwd(q, k, v, H, "mla_attn", side=side))

    qd, kd, vd = _dil_prep(pd, Gn["g_q_dil"], Gn["g_k_dil"], tab_dil, HD, scale_dil, "dil_prep")
    og, lg = [], []
    for grp, (win, dil) in enumerate(DIL_GROUPS):
        o_, l_ = _dil_fwd(qd, kd, vd, grp, dil, HD, f"dil_attn{grp}")
        og.append(o_)
        lg.append(l_)
    o_dil, lse_dil = _dil_merge(og, lg, "dil_merge")

    bm = mm(o_mla, W("w_br_mla"), "nn", F32, "branch_mla")
    bd = mm(o_dil, W("w_br_dil"), "nn", F32, "branch_dil")
    merged = _gate_merge(pg, bm, bd, "gate_merge")
    x2 = mm(merged, W("w_o"), "nn", F32, "out_proj", res=x1)

    x3, ffn2 = _ffn_fwd(sched, x2, Gn["g_ffn2"], tf, "ffn2")
    n4 = _rms_fwd(x3, Gn["g_ple"], "ple_norm")
    zg = mm(n4, W("w_ple_gate"), "nn", F32, "ple_gate")
    p_b = p.astype(BF16)
    pp = mm(p_b, W("w_ple_proj"), "nn", F32, "ple_proj")
    dy, dpp, dzg, loss = _ple_loss(x3, zg, pp, target, "ple_loss")

    gg = {}
    sched.grad("w_ple_proj", mm(p_b, dpp, "tn", BF16, "d_w_ple_proj"))
    sched.grad("w_ple_gate", mm(n4, dzg, "tn", BF16, "d_w_ple_gate"))
    dn4 = mm(dzg, W("w_ple_gate"), "nt", F32, "d_ple_norm")
    dx3, dx3_b, gg["g_ple"] = _rms_bwd_call(x3, Gn["g_ple"], dn4, dy, "d_x3")

    dx2, dx2_b, gg["g_ffn2"] = _ffn_bwd(sched, dx3, dx3_b, x2, Gn["g_ffn2"], ffn2, tf, "ffn2", True)

    sched.grad("w_o", mm(merged, dx2_b, "tn", BF16, "d_w_o"))
    dmerged = mm(dx2_b, W("w_o"), "nt", F32, "d_merged")
    dbm, dbd, dpg = _gate_bwd(dmerged, pg, bm, bd, "d_gate")
    sched.grad("w_br_mla", mm(o_mla, dbm, "tn", BF16, "d_w_br_mla"))
    sched.grad("w_br_dil", mm(o_dil, dbd, "tn", BF16, "d_w_br_dil"))
    do_mla = mm(dbm, W("w_br_mla"), "nt", BF16, "d_o_mla")
    do_dil = mm(dbd, W("w_br_dil"), "nt", F32, "d_o_dil")
    delta_dil = _dil_delta(do_dil, o_dil, HD, "dil_delta")

    dh = mm(dpg, W("w_gin"), "nt", F32, "d_h_gate")
    sched.grad("w_gin", mm(h, dpg, "tn", BF16, "d_w_gin"))
    gq_d, gk_d = Gn["g_q_dil"], Gn["g_k_dil"]
    dgq_d, dgk_d, dpd = [], [], None
    for grp, (win, dil) in enumerate(DIL_GROUPS):
        dq_, dk_, dv_ = _dil_bwd(qd, kd, vd, do_dil, delta_dil, lse_dil, grp, dil, HD, f"dil_bwd{grp}")
        dpd, dgq_, dgk_ = _dil_prep_bwd(dq_, dk_, dv_, pd, grp, gq_d, gk_d, tab_dil, HD, scale_dil,
                                        f"d_dil_prep{grp}", prev=dpd)
        dgq_d.append(dgq_)
        dgk_d.append(dgk_)
    dh = mm(dpd, W("w_dil"), "nt", F32, "d_h_dil", res=dh)
    sched.grad("w_dil", mm(h, dpd, "tn", BF16, "d_w_dil"))
    gg["g_q_dil"] = jnp.concatenate(dgq_d, axis=0)
    gg["g_k_dil"] = jnp.concatenate(dgk_d, axis=0)

    dq, dk, dv = _call_s(sched, "mla_bwd", 3,
                         lambda side: _mla_bwd(q, k, v, do_mla, do_mla.T, o_mla_t, lse_mla, H, "mla_bwd", side=side))
    dq_raw, gg["g_q_mla"] = _mla_q_bwd(dq, q_raw, Gn["g_q_mla"], tab_mla, H, scale_mla, "d_mla_q_prep")
    dkv, dkr, gg["g_k_mla"] = _mla_k_bwd(dk, dv, kv, lat, kr_col, Gn["g_k_mla"], tab_mla, H, "d_mla_k_prep")
    sched.grad("w_uq", mm(cq, dq_raw, "tn", BF16, "d_w_uq"))
    sched.grad("w_ukv", mm(ckv, dkv, "tn", BF16, "d_w_ukv"))
    dcq = mm(dq_raw, W("w_uq"), "nt", F32, "d_cq")
    dckv = mm(dkv, W("w_ukv"), "nt", F32, "d_ckv")
    dlat, gg["g_cq"], gg["g_ckv"] = _lat_bwd(dcq, dckv, dkr, lat, Gn["g_cq"], Gn["g_ckv"], "d_lat_norm")
    dh = mm(dlat, W("w_lat"), "nt", F32, "d_h_lat", res=dh)
    sched.grad("w_lat", mm(h, dlat, "tn", BF16, "d_w_lat"))

    dx1, dx1_b, gg["g_mix"] = _rms_bwd_call(x1, Gn["g_mix"], dh, dx2, "d_x1")
    dx, gg["g_ffn1"] = _ffn_bwd(sched, dx1, dx1_b, x, Gn["g_ffn1"], ffn1, tf, "ffn1", False)
    return loss, dx, gg


def _layout_weight(name, full, dims):
    H, QL, KVL, LP, Wd = dims["H"], dims["QL"], dims["KVL"], dims["LP"], dims["HD"] * DIL_HEAD
    off_dil = QL + KVL + MLA_ROPE
    off_gate = off_dil + 3 * len(DIL_GROUPS) * Wd
    if name == "w_lat":
        return _pad_to(full("w_in")[:, :off_dil], LP, 1)
    if name == "w_dil":
        return full("w_in")[:, off_dil:off_gate]
    if name == "w_gin":
        return full("w_in")[:, off_gate:]
    if name == "w_uq":
        return _pad_to(full("w_uq").reshape(QL, H, MLA_QK), MLA_QK_PAD, 2).reshape(QL, H * MLA_QK_PAD)
    return full(name)


def _natural_grad(name, gw, dims):
    H, QL, KVL = dims["H"], dims["QL"], dims["KVL"]
    if name == "w_in":
        return jnp.concatenate([gw["w_lat"][:, :QL + KVL + MLA_ROPE], gw["w_dil"], gw["w_gin"]], axis=1)
    if name == "w_uq":
        return gw["w_uq"].reshape(QL, H, MLA_QK_PAD)[:, :, :MLA_QK].reshape(QL, H * MLA_QK)
    return gw[name]


WEIGHT_SOURCES = {"w1_gu": ("w1_gate", "w1_up"), "w2_gu": ("w2_gate", "w2_up"), "w_lat": ("w_in",), "w_dil": ("w_in",),
                  "w_gin": ("w_in",)}
AG_RIDES = {"ffn1_norm": ("w1_gate", "w1_up"),
            "ffn1_gate_up": ("w1_down", ("w_in", 0, 2)), "ffn1_down": (("w_in", 1, 2), "w_uq", "w_ukv"),
            "mla_attn": ("w_br_mla", "w_br_dil", "w_o", "w_ple_gate", "w_ple_proj", "w2_gate", "w2_up", "w2_down")}
RS_FFN2 = ("w_ple_proj", "w_ple_gate", "w2_down", "w2_gate", "w2_up")
RS_MIXER = ("w_o", "w_br_mla", "w_br_dil", "w_in", "w_uq", "w_ukv")
RS_RIDES = {
    "d_merged": (("sibling", RS_FFN2),),
    "mla_bwd": (("chips", RS_FFN2),),
    "d_h_lat": (("join", RS_FFN2),),
    "ffn1_d_wdown": (("sibling", RS_MIXER),),
    "ffn1_d_act": (("chips", ("w_o", "w_br_mla", "w_br_dil", "w_uq", "w_ukv")), ("sibling", ("w1_down",))),
    "ffn1_d_wgate": (("chips", ("w1_down",)),),
    "ffn1_d_wup": (("sibling", ("w1_gate",)),),
    "ffn1_d_norm": (("chips", ("w_in", "w1_gate")), ("sibling", ("w1_up",))),
    "ffn1_d_x": (("chips", ("w1_up",)),),
}
RS_LAST = ((("join", RS_MIXER + ("w1_down", "w1_gate", "w1_up")),),)


class _MeshSchedule:
    def __init__(self, w, m, v, dims):
        self.w, self.m, self.v, self.dims = w, m, v, dims
        self.shapes = {n: tuple(w[n].shape[1:]) for n in BIG}
        self.kinds = {n: _kind(n, self.shapes[n]) for n in BIG}
        self.info, buf_shapes = _gather_plan(self.shapes)
        x, y, c = _place()
        self.me = (2 * x + y).astype(jnp.int32).reshape(1)
        self.c_idx = c.astype(jnp.int32).reshape(1)
        self.shards = {n: w[n][0].astype(BF16) for n in BIG}
        self.bufs, self.gathered, self.parts_done, self.layout = {}, set(), {}, {}
        for n in BIG:
            out_name, kind, base, _, _ = self.info[n]
            self.bufs[out_name] = _place_own(self.shards[n], buf_shapes[out_name], kind, base, self.me, f"ag_own_{n}",
                                             prev=self.bufs.get(out_name))
        self.gw, self.views, self.pairs, self.halves, self.recv = {}, {}, {}, {}, {}

    def _ag(self, names):
        return _ag_side(names, self.shards, self.bufs, self.info)

    def _ag_done(self, names, outs):
        for out_name, buf in zip(_buffers_of(names, self.info), outs):
            self.bufs[out_name] = buf
        for name, _, parts in [_ag_entry(e) for e in names]:
            self.parts_done[name] = self.parts_done.get(name, 0) + 1
            if self.parts_done[name] == parts:
                self.gathered.add(name)

    def weight(self, name):
        if name not in self.layout:
            assert all(s in self.gathered for s in WEIGHT_SOURCES.get(name, (name,))), name
            if name in self.bufs and name not in self.info:
                self.layout[name] = self.bufs[name]
            else:
                full = lambda n: _natural(self.bufs[self.info[n][0]], self.info[n][1])
                self.layout[name] = _layout_weight(name, full, self.dims)
        return self.layout[name]

    def grad(self, name, g):
        self.gw[name] = g

    def _rs_side(self, stages):
        sides = []
        for stage, names in stages:
            if stage == "sibling":
                for n in names:
                    self.views[n] = _halves_view(_natural_grad(n, self.gw, self.dims), self.kinds[n], self.shapes[n])
                sides.append(_rs_sibling_side([self.views[n] for n in names]))
            elif stage == "chips":
                sides.append(_rs_chips_side([self.pairs[n] for n in names], [self.kinds[n] for n in names],
                                            [self.shapes[n][1] for n in names]))
            else:
                sides.append(_rs_join_side([self.halves[n] for n in names]))
        return sides[0] if len(sides) == 1 else _merge_sides(sides)

    def _rs_done(self, stages, outs):
        for stage, names in stages:
            got, outs = outs[:len(names)], outs[len(names):]
            for n, a in zip(names, got):
                if stage == "sibling":
                    self.pairs[n] = _pair_sum(self.views[n], a, self.c_idx, f"rs_pair_{n}")
                elif stage == "chips":
                    self.halves[n] = _sum_pieces(self.pairs[n], a, self.kinds[n], self.me, f"rs_sum_{n}")
                else:
                    self.recv[n] = a

    def side(self, tag):
        if tag in AG_RIDES:
            return self._ag(AG_RIDES[tag])
        if tag in RS_RIDES:
            return self._rs_side(RS_RIDES[tag])
        return None

    def done(self, tag, outs):
        if tag in AG_RIDES:
            self._ag_done(AG_RIDES[tag], outs)
        else:
            self._rs_done(RS_RIDES[tag], outs)

    def finish(self):
        for k, stages in enumerate(RS_LAST):
            self._rs_done(stages, _run_side(self._rs_side(stages), f"rs_last{k}"))
        outs = {"grad": {}, "delta": {}, "m": {}, "v": {}}
        for n in BIG:
            res = _adamw_halves(self.w[n][0], self.halves[n], self.recv[n], self.m[n][0], self.v[n][0], self.c_idx,
                                f"adamw_{n}")
            for kind, a in zip(("grad", "delta", "m", "v"), res):
                outs[kind][n] = a.reshape((1,) + a.shape)
        return outs


def _step(x, p, positions, loss_target, w, m, v):
    T, D = x.shape[1], x.shape[2]
    QL, KVL = w["g_cq"].shape[1], w["g_ckv"].shape[1]
    dims = {
        "H": 4 * w["w_uq"].shape[2] // MLA_QK, "HD": w["w_br_dil"].shape[1] // DIL_HEAD, "QL": QL, "KVL": KVL,
        "LP": _round_up(QL + KVL + MLA_ROPE, LANES), "tf": _pick(4 * w["w1_gate"].shape[2], 512),
    }
    small_shapes = {n: w[n].shape for n in SMALL}
    sched = _MeshSchedule(w, m, v, dims)
    Gn = {n: w[n] for n in SMALL}
    Gn["g_q_mla"] = _pad_to(Gn["g_q_mla"], MLA_QK_PAD, 1)
    Gn["g_k_mla"] = _pad_to(Gn["g_k_mla"], MLA_QK_PAD, 1)
    Gn["g_q_dil"] = Gn["g_q_dil"].reshape(len(DIL_GROUPS), 1, DIL_HEAD)
    Gn["g_k_dil"] = Gn["g_k_dil"].reshape(len(DIL_GROUPS), 1, DIL_HEAD)

    pos_b = jnp.broadcast_to(positions.astype(F32).reshape(T, 1), (T, LANES))
    loss, dx, gg = _local_step(sched, x[0], p[0, 0], pos_b, loss_target[0], Gn, dims)
    loss = lax.psum(loss[0, 0], ("x", "y", "c"))
    outs = sched.finish()

    gg["g_q_mla"] = gg["g_q_mla"][:, :MLA_QK]
    gg["g_k_mla"] = gg["g_k_mla"][:, :MLA_QK]
    g_small = _all_reduce_small(_pack_small(gg))
    d_s, m_s, v_s = _adamw(_pack_small({n: w[n] for n in SMALL}), g_small, _pack_small({n: m[n] for n in SMALL}),
                           _pack_small({n: v[n] for n in SMALL}), "adamw_gains")
    for kind, buf in (("grad", g_small), ("delta", d_s), ("m", m_s), ("v", v_s)):
        outs[kind].update(_unpack_small(buf, small_shapes))

    grad_x = dx.reshape(1, T, D)
    return (loss, grad_x, *[outs["grad"][n] for n in WEIGHTS], *[outs["delta"][n] for n in WEIGHTS],
            *[outs["m"][n] for n in WEIGHTS], *[outs["v"][n] for n in WEIGHTS])


def kernel(x, p, positions, g_ffn1, w1_gate, w1_up, w1_down, g_mix, w_in, g_cq, w_uq, g_ckv, w_ukv, g_q_mla, g_k_mla, g_q_dil, g_k_dil, w_br_mla, w_br_dil, w_o, g_ffn2, w2_gate, w2_up, w2_down, g_ple, w_ple_gate, w_ple_proj, loss_target, m_g_ffn1, m_w1_gate, m_w1_up, m_w1_down, m_g_mix, m_w_in, m_g_cq, m_w_uq, m_g_ckv, m_w_ukv, m_g_q_mla, m_g_k_mla, m_g_q_dil, m_g_k_dil, m_w_br_mla, m_w_br_dil, m_w_o, m_g_ffn2, m_w2_gate, m_w2_up, m_w2_down, m_g_ple, m_w_ple_gate, m_w_ple_proj, v_g_ffn1, v_w1_gate, v_w1_up, v_w1_down, v_g_mix, v_w_in, v_g_cq, v_w_uq, v_g_ckv, v_w_ukv, v_g_q_mla, v_g_k_mla, v_g_q_dil, v_g_k_dil, v_w_br_mla, v_w_br_dil, v_w_o, v_g_ffn2, v_w2_gate, v_w2_up, v_w2_down, v_g_ple, v_w_ple_gate, v_w_ple_proj):
    args = locals()
    w = {n: args[n] for n in WEIGHTS}
    m = {n: args["m_" + n] for n in WEIGHTS}
    v = {n: args["v_" + n] for n in WEIGHTS}
    return _step(x, p, positions, loss_target, w, m, v)
```
